```python
import math
import jax, jax.numpy as jnp
from jax import lax
import numpy as np

D_MODEL = 2048
BATCH = 4
SEQ = 2048
DEPTH = 1
DEC_BATCH = 128
DEC_SEQ = 1
PAST_LEN = 2048
PAGE_SIZE = 128

ATTN_HEADS = 8
ATTN_HEAD_DIM = 128
IDX_HEADS = 16
IDX_HEAD_DIM = 64
TOPK_MAX = 256
Q_BLOCK = 128
RET_HEADS = 8
RET_DK = 128
RET_DV = 128
RET_CHUNK = 128
ROPE_BASE = 10000.0
D_FF = 4 * D_MODEL
EPS = 1e-6
MIX_WIDTH = ATTN_HEADS * ATTN_HEAD_DIM + RET_HEADS * RET_DV
PROJ_SPLITS = (ATTN_HEADS * ATTN_HEAD_DIM, ATTN_HEAD_DIM, ATTN_HEAD_DIM,
               IDX_HEADS * IDX_HEAD_DIM, IDX_HEAD_DIM, IDX_HEADS,
               RET_HEADS * RET_DK, RET_HEADS * RET_DK, RET_HEADS * RET_DV, RET_HEADS * RET_DV)
PROJ_WIDTH = sum(PROJ_SPLITS)

kernel_name = 'dsa_retention_parallel_hybrid_step'


def rmsnorm(x, g):
    xf = x.astype(jnp.float32)
    y = xf * lax.rsqrt(jnp.mean(jnp.square(xf), -1, keepdims=True) + EPS)
    return (y * g.astype(jnp.float32)).astype(x.dtype)


def layernorm(x, g, b):
    xf = x.astype(jnp.float32)
    mu = jnp.mean(xf, -1, keepdims=True)
    var = jnp.mean(jnp.square(xf - mu), -1, keepdims=True)
    y = (xf - mu) * lax.rsqrt(var + EPS) * g.astype(jnp.float32) + b.astype(jnp.float32)
    return y.astype(x.dtype)


def rotary(x, pos):
    half = x.shape[-1] // 2
    inv = ROPE_BASE ** (-jnp.arange(half, dtype=jnp.float32) / half)
    ang = pos[:, None] * inv[None, :]
    cos = jnp.cos(ang)[None, :, None, :]
    sin = jnp.sin(ang)[None, :, None, :]
    xf = x.astype(jnp.float32)
    x1, x2 = xf[..., :half], xf[..., half:]
    return jnp.concatenate([x1 * cos - x2 * sin, x2 * cos + x1 * sin], -1).astype(x.dtype)


def ret_log_gamma():
    return jnp.log1p(-jnp.exp2(-5.0 - jnp.arange(RET_HEADS, dtype=jnp.float32)))


def mixer_inputs(x, pos0, norm1_g, w_in, idx_k_norm_g, idx_k_norm_b):
    b, n, _ = x.shape
    h = rmsnorm(x, norm1_g)
    proj = h @ w_in
    cuts = np.cumsum(PROJ_SPLITS)[:-1].tolist()
    qa, ka, va, qi, ki, wi, qr, kr, vr, gr = jnp.split(proj, cuts, axis=-1)
    qa = qa.reshape(b, n, ATTN_HEADS, ATTN_HEAD_DIM)
    qi = qi.reshape(b, n, IDX_HEADS, IDX_HEAD_DIM)
    ki = layernorm(ki, idx_k_norm_g, idx_k_norm_b)
    wi = wi * (IDX_HEADS ** -0.5)
    pos = jnp.arange(n, dtype=jnp.float32) + pos0
    qr = rotary(qr.reshape(b, n, RET_HEADS, RET_DK), pos)
    kr = rotary(kr.reshape(b, n, RET_HEADS, RET_DK), pos) * (RET_DK ** -0.5)
    vr = vr.reshape(b, n, RET_HEADS, RET_DV)
    return qa, ka, va, qi, ki, wi, qr, kr, vr, gr


def dsa_attend(q, qi, wi, k, v, ki, q_pos0):
    b, nq = q.shape[:2]
    L = k.shape[1]
    k_sel = min(TOPK_MAX, L // 4)
    qb = Q_BLOCK if nq % Q_BLOCK == 0 else nq
    nb = nq // qb
    kpos = jnp.arange(L, dtype=jnp.int32)
    qpos = (jnp.arange(nq, dtype=jnp.int32) + q_pos0).reshape(nb, qb)
    kif = ki.astype(jnp.float32)

    def to_blocks(a):
        return a.reshape((b, nb, qb) + a.shape[2:]).swapaxes(0, 1)

    def block(args):
        qq, qqi, ww, pp = args
        s = jnp.einsum('bqhd,bkd->bqhk', qqi.astype(jnp.float32), kif)
        score = jnp.einsum('bqhk,bqh->bqk', jax.nn.relu(s), ww.astype(jnp.float32)) * (IDX_HEAD_DIM ** -0.5)
        causal = kpos[None, None, :] <= pp[None, :, None]
        score = jnp.where(causal, score, -jnp.inf)
        _, idx = lax.top_k(score, k_sel)
        kg = jax.vmap(lambda kk, ii: kk[ii])(k, idx)
        vg = jax.vmap(lambda vv, ii: vv[ii])(v, idx)
        valid = idx <= pp[None, :, None]
        sc = jnp.einsum('bqhd,bqkd->bqhk', qq, kg).astype(jnp.float32) * (ATTN_HEAD_DIM ** -0.5)
        sc = jnp.where(valid[:, :, None, :], sc, -jnp.inf)
        p = jax.nn.softmax(sc, axis=-1)
        return jnp.einsum('bqhk,bqkd->bqhd', p.astype(vg.dtype), vg)

    out = lax.map(block, (to_blocks(q), to_blocks(qi), to_blocks(wi), qpos))
    return out.swapaxes(0, 1).reshape(b, nq, ATTN_HEADS * ATTN_HEAD_DIM)


def ret_chunk(R, q, k, v, lg):
    q = q.astype(jnp.float32)
    k = k.astype(jnp.float32)
    v = v.astype(jnp.float32)
    n = q.shape[1]
    i = jnp.arange(n, dtype=jnp.float32)
    diff = i[:, None] - i[None, :]
    decay = jnp.where(diff[None] >= 0, jnp.exp(jnp.maximum(diff, 0.0)[None] * lg[:, None, None]), 0.0)
    inner = jnp.einsum('bhij,bjhe->bihe', jnp.einsum('bihd,bjhd->bhij', q, k) * decay[None], v)
    cross = jnp.einsum('bihd,bhde->bihe', q, R) * jnp.exp((i + 1.0)[:, None] * lg[None, :])[None, :, :, None]
    zeta = jnp.exp((n - 1.0 - i)[:, None] * lg[None, :])
    R_new = jnp.exp(n * lg)[None, :, None, None] * R + jnp.einsum('bjhd,bjhe,jh->bhde', k, v, zeta)
    return R_new, inner + cross


def retention_prompt(q, k, v):
    b, n = q.shape[:2]
    c = RET_CHUNK if n % RET_CHUNK == 0 else n
    nc = n // c
    lg = ret_log_gamma()

    def chunks(a):
        return a.reshape((b, nc, c) + a.shape[2:]).swapaxes(0, 1)

    R0 = jnp.zeros((b, RET_HEADS, RET_DK, RET_DV), jnp.float32)
    R, out = lax.scan(lambda R, xs: ret_chunk(R, xs[0], xs[1], xs[2], lg), R0, (chunks(q), chunks(k), chunks(v)))
    return out.swapaxes(0, 1).reshape(b, n, RET_HEADS, RET_DV), R


def finish(x, attn_o, ret_o, gr, w_out, norm2_g, w_up, w_down, final_norm_g):
    b, n, _ = x.shape
    rn = ret_o * lax.rsqrt(jnp.mean(jnp.square(ret_o), -1, keepdims=True) + EPS)
    rg = rn.reshape(b, n, RET_HEADS * RET_DV) * jax.nn.silu(gr.astype(jnp.float32))
    mixed = jnp.concatenate([attn_o.astype(x.dtype), rg.astype(x.dtype)], -1) @ w_out
    x = x + mixed
    u = rmsnorm(x, norm2_g) @ w_up
    x = x + jnp.square(jax.nn.relu(u)) @ w_down
    return rmsnorm(x, final_norm_g)


def gather_pages(cache, page_table):
    return cache[page_table].reshape(page_table.shape[0], -1, cache.shape[-1])


def setup_inputs(seed: int = 0) -> dict:
    key = jax.random.key(seed)
    ks = jax.random.split(key, 17)
    n_pages = PAST_LEN // PAGE_SIZE
    used = DEC_BATCH * n_pages
    n_pool = used + max(1, used // 4)
    f32 = jnp.float32
    gamma = 1.0 - jnp.exp2(-5.0 - jnp.arange(RET_HEADS, dtype=f32))
    ret_std = jnp.sqrt(1.0 / (1.0 - gamma ** 2)) * (RET_DK ** -0.5)
    page_table = jax.random.permutation(ks[6], n_pool)[:used].reshape(DEC_BATCH, n_pages).astype(jnp.int32)
    return {
        'x_prompt': jax.random.normal(ks[0], (BATCH, SEQ, D_MODEL), f32),
        'x_sample': jax.random.normal(ks[1], (DEC_BATCH, DEC_SEQ, D_MODEL), f32),
        'cache_k': jax.random.normal(ks[2], (n_pool, PAGE_SIZE, ATTN_HEAD_DIM), f32),
        'cache_v': jax.random.normal(ks[3], (n_pool, PAGE_SIZE, ATTN_HEAD_DIM), f32),
        'cache_idx_k': jax.random.normal(ks[4], (n_pool, PAGE_SIZE, IDX_HEAD_DIM), f32),
        'state_ret': jax.random.normal(ks[5], (DEC_BATCH, RET_HEADS, RET_DK, RET_DV), f32) * ret_std[None, :, None, None],
        'page_table': page_table,
        'norm1_g': 1.0 + 0.02 * jax.random.normal(ks[7], (D_MODEL,), f32),
        'w_in': jax.random.normal(ks[8], (D_MODEL, PROJ_WIDTH), f32) * (D_MODEL ** -0.5),
        'idx_k_norm_g': 1.0 + 0.02 * jax.random.normal(ks[9], (IDX_HEAD_DIM,), f32),
        'idx_k_norm_b': 0.02 * jax.random.normal(ks[10], (IDX_HEAD_DIM,), f32),
        'w_out': jax.random.normal(ks[11], (MIX_WIDTH, D_MODEL), f32) * (MIX_WIDTH ** -0.5),
        'norm2_g': 1.0 + 0.02 * jax.random.normal(ks[12], (D_MODEL,), f32),
        'w_up': jax.random.normal(ks[13], (D_MODEL, D_FF), f32) * (D_MODEL ** -0.5),
        'w_down': jax.random.normal(ks[14], (D_FF, D_MODEL), f32) * (D_FF ** -0.5),
        'final_norm_g': 1.0 + 0.02 * jax.random.normal(ks[15], (D_MODEL,), f32),
    }


def reference(x_prompt, x_sample, cache_k, cache_v, cache_idx_k, state_ret, page_table,
              norm1_g, w_in, idx_k_norm_g, idx_k_norm_b, w_out, norm2_g, w_up, w_down, final_norm_g):
    past_len = page_table.shape[1] * cache_k.shape[1]
    y_prompt = x_prompt
    y_sample = x_sample
    for _layer in range(DEPTH):
        qa, ka, va, qi, ki, wi, qr, kr, vr, gr = mixer_inputs(
            y_prompt, 0, norm1_g, w_in, idx_k_norm_g, idx_k_norm_b)
        attn_p = dsa_attend(qa, qi, wi, ka, va, ki, 0)
        ret_p, ret_state_p = retention_prompt(qr, kr, vr)
        y_prompt = finish(y_prompt, attn_p, ret_p, gr, w_out, norm2_g, w_up, w_down, final_norm_g)
        sqa, ska, sva, sqi, ski, swi, sqr, skr, svr, sgr = mixer_inputs(
            y_sample, past_len, norm1_g, w_in, idx_k_norm_g, idx_k_norm_b)
        k_all = jnp.concatenate([gather_pages(cache_k, page_table), ska.astype(cache_k.dtype)], 1)
        v_all = jnp.concatenate([gather_pages(cache_v, page_table), sva.astype(cache_v.dtype)], 1)
        ik_all = jnp.concatenate([gather_pages(cache_idx_k, page_table), ski.astype(cache_idx_k.dtype)], 1)
        attn_s = dsa_attend(sqa, sqi, swi, k_all, v_all, ik_all, past_len)
        ret_state_s, ret_s = ret_chunk(state_ret.astype(jnp.float32), sqr, skr, svr, ret_log_gamma())
        y_sample = finish(y_sample, attn_s, ret_s, sgr, w_out, norm2_g, w_up, w_down, final_norm_g)
    return (y_prompt, y_sample, ka, va, ki, ret_state_p, ska, sva, ski, ret_state_s)
```

```python
import functools

import numpy as np
import jax
import jax.numpy as jnp
from jax import lax
from jax.experimental import pallas as pl
from jax.experimental.pallas import tpu as pltpu

F32 = jnp.float32
BF16 = jnp.bfloat16
I32 = jnp.int32

D_MODEL = 2048
ATTN_HEADS = 8
ATTN_HEAD_DIM = 128
IDX_HEADS = 16
IDX_HEAD_DIM = 64
TOPK_MAX = 256
RET_HEADS = 8
RET_DK = 128
RET_DV = 128
RET_CHUNK = 128
ROPE_BASE = 10000.0
D_FF = 4 * D_MODEL
EPS = 1e-6
Q_BLOCK = 128

LANES = 128
PROJ_TILE = 512
N_PROJ_TILES = 13
KEY_TILE = 256
INT_MIN = -2 ** 31
NEG_BIG = -1e30
VMEM_LIMIT = 56 * 1024 * 1024


def _cparams(sem):
    return pltpu.CompilerParams(dimension_semantics=sem, vmem_limit_bytes=VMEM_LIMIT)


def _proj_body(x_ref, g_ref, w_ref, cs_ref, lng_ref, lnb_ref,
               qa_ref, qi_ref, main_ref, ka_ref, va_ref, ki_ref, wi_ref, xn_ref):
    j = pl.program_id(1)

    @pl.when(j == 0)
    def _norm():
        x = x_ref[...]
        ms = jnp.mean(x * x, axis=-1, keepdims=True)
        xn_ref[...] = (x * lax.rsqrt(ms + EPS) * g_ref[...]).astype(BF16)

    acc = jnp.dot(xn_ref[...], w_ref[...], preferred_element_type=F32)

    @pl.when(j < 2)
    def _qa():
        for hh in range(4):
            qa_ref[hh] = acc[:, hh * 128:(hh + 1) * 128].astype(BF16)

    @pl.when((j >= 2) & (j < 4))
    def _qi():
        for hh in range(8):
            qi_ref[hh] = acc[:, hh * 64:(hh + 1) * 64].astype(BF16)

    @pl.when((j >= 4) & (j < 8))
    def _rot():
        cosf = cs_ref[:, :LANES]
        sinf = cs_ref[:, LANES:]
        scale = jnp.where(j >= 6, RET_DK ** -0.5, 1.0).astype(F32)
        for hh in range(4):
            xh = acc[:, hh * 128:(hh + 1) * 128]
            r = (xh * cosf + pltpu.roll(xh, 64, 1) * sinf) * scale
            main_ref[:, hh * 128:(hh + 1) * 128] = r.astype(BF16)

    @pl.when((j >= 8) & (j < 12))
    def _plain():
        main_ref[...] = acc.astype(BF16)

    @pl.when(j == 12)
    def _small():
        ka_ref[...] = acc[:, 0:128]
        va_ref[...] = acc[:, 128:256]
        kr = acc[:, 256:320]
        mu = jnp.mean(kr, axis=-1, keepdims=True)
        d = kr - mu
        var = jnp.mean(d * d, axis=-1, keepdims=True)
        ki_ref[...] = d * lax.rsqrt(var + EPS) * lng_ref[...] + lnb_ref[...]
        wi_ref[...] = acc[:, 320:336] * (IDX_HEADS ** -0.5)


def _project(x2d, g1, w_perm, cs, lng, lnb, tm):
    m = x2d.shape[0]
    n_pos_blocks = cs.shape[0] // tm
    grid = (m // tm, N_PROJ_TILES)
    out_shape = (
        jax.ShapeDtypeStruct((ATTN_HEADS, m, ATTN_HEAD_DIM), BF16),
        jax.ShapeDtypeStruct((IDX_HEADS, m, IDX_HEAD_DIM), BF16),
        jax.ShapeDtypeStruct((m, 4096), BF16),
        jax.ShapeDtypeStruct((m, ATTN_HEAD_DIM), F32),
        jax.ShapeDtypeStruct((m, ATTN_HEAD_DIM), F32),
        jax.ShapeDtypeStruct((m, IDX_HEAD_DIM), F32),
        jax.ShapeDtypeStruct((m, IDX_HEADS), F32),
    )
    in_specs = [
        pl.BlockSpec((tm, D_MODEL), lambda i, j: (i, 0)),
        pl.BlockSpec((1, D_MODEL), lambda i, j: (0, 0)),
        pl.BlockSpec((D_MODEL, PROJ_TILE), lambda i, j: (0, j)),
        pl.BlockSpec((tm, 2 * LANES), lambda i, j: (i % n_pos_blocks, 0)),
        pl.BlockSpec((1, IDX_HEAD_DIM), lambda i, j: (0, 0)),
        pl.BlockSpec((1, IDX_HEAD_DIM), lambda i, j: (0, 0)),
    ]
    out_specs = (
        pl.BlockSpec((4, tm, ATTN_HEAD_DIM), lambda i, j: (jnp.clip(j, 0, 1), i, 0)),
        pl.BlockSpec((8, tm, IDX_HEAD_DIM), lambda i, j: (jnp.clip(j - 2, 0, 1), i, 0)),
        pl.BlockSpec((tm, PROJ_TILE), lambda i, j: (i, jnp.clip(j - 4, 0, 7))),
        pl.BlockSpec((tm, ATTN_HEAD_DIM), lambda i, j: (i, 0)),
        pl.BlockSpec((tm, ATTN_HEAD_DIM), lambda i, j: (i, 0)),
        pl.BlockSpec((tm, IDX_HEAD_DIM), lambda i, j: (i, 0)),
        pl.BlockSpec((tm, IDX_HEADS), lambda i, j: (i, 0)),
    )
    return pl.pallas_call(
        _proj_body,
        grid=grid,
        in_specs=in_specs,
        out_specs=out_specs,
        out_shape=out_shape,
        scratch_shapes=[pltpu.VMEM((tm, D_MODEL), BF16)],
        compiler_params=_cparams(("arbitrary", "arbitrary")),
        name="proj",
    )(x2d, g1, w_perm, cs, lng, lnb)


def _score_to_key(score):
    bits = lax.bitcast_convert_type(score, I32)
    key = bits ^ ((bits >> 31) & 0x7FFFFFFF)
    return jnp.where(key == -1, 0, key)


def _threshold_search(count_ge, n_iter, rows):
    def body(it, t):
        bit = lax.shift_left(jnp.int32(1), 31 - it)
        cand = t ^ bit
        cnt = count_ge(cand)
        return jnp.where(cnt >= float(TOPK_MAX), cand, t)

    t0 = jnp.full((rows, LANES), INT_MIN, I32)
    return lax.fori_loop(0, n_iter, body, t0)


def _attn_body(qa_ref, qi_ref, wi_ref, ka_ref, va_ref, ki_ref, tri_ref, o_ref,
               kbf, vbf, kibf, wb, keys, sbuf, mrun, lrun, acc_s):
    qb = pl.program_id(1)
    n_heads_q = ATTN_HEADS * Q_BLOCK

    @pl.when(qb == 0)
    def _cast():
        kbf[...] = ka_ref[...].astype(BF16)
        vbf[...] = va_ref[...].astype(BF16)
        kibf[...] = ki_ref[...].astype(BF16)

    nk = (qb + 2) // 2
    w = wi_ref[...] * (IDX_HEAD_DIM ** -0.5)
    for h in range(IDX_HEADS):
        wb[h] = jnp.broadcast_to(w[:, h:h + 1], (Q_BLOCK, KEY_TILE))
    qi2 = qi_ref[...].reshape(IDX_HEADS * Q_BLOCK, IDX_HEAD_DIM)
    row = qb * Q_BLOCK + lax.broadcasted_iota(I32, (Q_BLOCK, KEY_TILE), 0)
    col0 = lax.broadcasted_iota(I32, (Q_BLOCK, KEY_TILE), 1)

    def idx_body(kt, carry):
        off = pl.multiple_of(kt * KEY_TILE, KEY_TILE)
        kit = kibf[pl.ds(off, KEY_TILE), :]
        s = lax.dot_general(qi2, kit, (((1,), (1,)), ((), ())), preferred_element_type=F32)
        score = jnp.zeros((Q_BLOCK, KEY_TILE), F32)
        for h in range(IDX_HEADS):
            score = score + jnp.maximum(s[h * Q_BLOCK:(h + 1) * Q_BLOCK], 0.0) * wb[h]
        key = _score_to_key(score)
        keys[kt] = jnp.where(col0 + off <= row, key, INT_MIN)
        return carry

    lax.fori_loop(0, nk, idx_body, 0)

    def count_cmp(cmp):
        def body(kt, acc):
            c = jnp.where(cmp(keys[kt]), 1.0, 0.0)
            return acc + c[:, :LANES] + c[:, LANES:]
        acc = lax.fori_loop(0, nk, body, jnp.zeros((Q_BLOCK, LANES), F32))
        return jnp.broadcast_to(jnp.sum(acc, axis=1, keepdims=True), (Q_BLOCK, LANES))

    def count_ge(cand):
        cand2 = jnp.concatenate([cand, cand], axis=1)
        return count_cmp(lambda kk: kk >= cand2)

    n_iter = jnp.where(qb >= TOPK_MAX // Q_BLOCK, 32, 0)
    t = _threshold_search(count_ge, n_iter, Q_BLOCK)
    t2 = jnp.concatenate([t, t], axis=1)
    need = float(TOPK_MAX) - count_cmp(lambda kk: kk > t2)
    need2 = jnp.concatenate([need, need], axis=1)

    qa2 = qa_ref[...].reshape(n_heads_q, ATTN_HEAD_DIM)
    mrun[...] = jnp.full((n_heads_q, LANES), NEG_BIG, F32)

    def p1_body(kt, tie_off):
        off = pl.multiple_of(kt * KEY_TILE, KEY_TILE)
        kk = keys[kt]
        eq = kk == t2
        tie = jnp.where(eq, 1.0, 0.0)
        rank = jnp.dot(tie.astype(BF16), tri_ref[...], preferred_element_type=F32) + tie_off
        sel = ((kk > t2) | (eq & (rank <= need2))) & (col0 + off <= row)
        s = lax.dot_general(qa2, kbf[pl.ds(off, KEY_TILE), :], (((1,), (1,)), ((), ())),
                            preferred_element_type=F32)
        for h in range(ATTN_HEADS):
            sl = slice(h * Q_BLOCK, (h + 1) * Q_BLOCK)
            sh = jnp.where(sel, s[sl] * (ATTN_HEAD_DIM ** -0.5), NEG_BIG)
            sbuf[kt, sl, :] = sh
            mrun[sl, :] = jnp.maximum(mrun[sl, :], jnp.maximum(sh[:, :LANES], sh[:, LANES:]))
        return tie_off + jnp.broadcast_to(jnp.sum(tie, axis=1, keepdims=True), (Q_BLOCK, KEY_TILE))

    lax.fori_loop(0, nk, p1_body, jnp.zeros((Q_BLOCK, KEY_TILE), F32))

    m = jnp.max(mrun[...], axis=1, keepdims=True)
    mrun[...] = jnp.broadcast_to(m, (n_heads_q, LANES))
    lrun[...] = jnp.zeros((n_heads_q, LANES), F32)
    acc_s[...] = jnp.zeros((n_heads_q, ATTN_HEAD_DIM), F32)

    def p2_body(kt, carry):
        off = pl.multiple_of(kt * KEY_TILE, KEY_TILE)
        mb = mrun[...]
        sb = sbuf[kt]
        p = jnp.exp(sb - jnp.concatenate([mb, mb], axis=1))
        lrun[...] = lrun[...] + p[:, :LANES] + p[:, LANES:]
        acc_s[...] = acc_s[...] + jnp.dot(p.astype(BF16), vbf[pl.ds(off, KEY_TILE), :],
                                          preferred_element_type=F32)
        return carry

    lax.fori_loop(0, nk, p2_body, 0)

    l = jnp.sum(lrun[...], axis=1, keepdims=True)
    out = acc_s[...] / l
    for h in range(ATTN_HEADS):
        o_ref[:, h * 128:(h + 1) * 128] = out[h * Q_BLOCK:(h + 1) * Q_BLOCK].astype(BF16)


def _prompt_attention(qa_hm, qi_hm, wi, ka, va, ki, tri, batch, seq):
    nq = seq // Q_BLOCK
    nkt = seq // KEY_TILE
    m = batch * seq
    n_heads_q = ATTN_HEADS * Q_BLOCK
    in_specs = [
        pl.BlockSpec((ATTN_HEADS, Q_BLOCK, ATTN_HEAD_DIM), lambda b, q: (0, b * nq + q, 0)),
        pl.BlockSpec((IDX_HEADS, Q_BLOCK, IDX_HEAD_DIM), lambda b, q: (0, b * nq + q, 0)),
        pl.BlockSpec((Q_BLOCK, IDX_HEADS), lambda b, q: (b * nq + q, 0)),
        pl.BlockSpec((seq, ATTN_HEAD_DIM), lambda b, q: (b, 0)),
        pl.BlockSpec((seq, ATTN_HEAD_DIM), lambda b, q: (b, 0)),
        pl.BlockSpec((seq, IDX_HEAD_DIM), lambda b, q: (b, 0)),
        pl.BlockSpec((KEY_TILE, KEY_TILE), lambda b, q: (0, 0)),
    ]
    return pl.pallas_call(
        _attn_body,
        grid=(batch, nq),
        in_specs=in_specs,
        out_specs=pl.BlockSpec((Q_BLOCK, ATTN_HEADS * ATTN_HEAD_DIM), lambda b, q: (b * nq + q, 0)),
        out_shape=jax.ShapeDtypeStruct((m, ATTN_HEADS * ATTN_HEAD_DIM), BF16),
        scratch_shapes=[
            pltpu.VMEM((seq, ATTN_HEAD_DIM), BF16),
            pltpu.VMEM((seq, ATTN_HEAD_DIM), BF16),
            pltpu.VMEM((seq, IDX_HEAD_DIM), BF16),
            pltpu.VMEM((IDX_HEADS, Q_BLOCK, KEY_TILE), F32),
            pltpu.VMEM((nkt, Q_BLOCK, KEY_TILE), I32),
            pltpu.VMEM((nkt, n_heads_q, KEY_TILE), F32),
            pltpu.VMEM((n_heads_q, LANES), F32),
            pltpu.VMEM((n_heads_q, LANES), F32),
            pltpu.VMEM((n_heads_q, ATTN_HEAD_DIM), F32),
        ],
        compiler_params=_cparams(("arbitrary", "arbitrary")),
        name="prompt_attn",
    )(qa_hm, qi_hm, wi, ka, va, ki, tri)


def _gate(o, g):
    rn = o * lax.rsqrt(jnp.mean(o * o, axis=-1, keepdims=True) + EPS)
    return rn * (g / (1.0 + jnp.exp(-g)))


def _ret_body(q_ref, k_ref, v_ref, g_ref, decay_ref, rsc_ref, zeta_ref, gpow_ref,
              rg_ref, st_ref):
    c = pl.program_id(1)

    @pl.when(c == 0)
    def _init():
        st_ref[...] = jnp.zeros(st_ref.shape, F32)

    for h in range(RET_HEADS):
        sl = slice(h * 128, (h + 1) * 128)
        q = q_ref[:, sl]
        k = k_ref[:, sl]
        v = v_ref[:, sl]
        r_old = st_ref[0, h]
        qk = lax.dot_general(q, k, (((1,), (1,)), ((), ())), preferred_element_type=F32)
        inner = jnp.dot((qk * decay_ref[h]).astype(BF16), v, preferred_element_type=F32)
        cross = jnp.dot(q, r_old.astype(BF16), preferred_element_type=F32) * rsc_ref[h]
        kz = (k.astype(F32) * zeta_ref[h]).astype(BF16)
        upd = lax.dot_general(kz, v, (((0,), (0,)), ((), ())), preferred_element_type=F32)
        st_ref[0, h] = r_old * gpow_ref[h] + upd
        rg_ref[:, sl] = _gate(inner + cross, g_ref[:, sl].astype(F32)).astype(BF16)


def _prompt_retention(main, decay, rsc, zeta, gpow, batch, seq):
    nc = seq // RET_CHUNK
    m = batch * seq
    width = RET_HEADS * RET_DV
    const3 = lambda b, c: (0, 0, 0)
    in_specs = [
        pl.BlockSpec((RET_CHUNK, width), lambda b, c: (b * nc + c, 0)),
        pl.BlockSpec((RET_CHUNK, width), lambda b, c: (b * nc + c, 1)),
        pl.BlockSpec((RET_CHUNK, width), lambda b, c: (b * nc + c, 2)),
        pl.BlockSpec((RET_CHUNK, width), lambda b, c: (b * nc + c, 3)),
        pl.BlockSpec((RET_HEADS, RET_CHUNK, RET_CHUNK), const3),
        pl.BlockSpec((RET_HEADS, RET_CHUNK, RET_DV), const3),
        pl.BlockSpec((RET_HEADS, RET_CHUNK, RET_DK), const3),
        pl.BlockSpec((RET_HEADS, 1, RET_DV), const3),
    ]
    return pl.pallas_call(
        _ret_body,
        grid=(batch, nc),
        in_specs=in_specs,
        out_specs=(
            pl.BlockSpec((RET_CHUNK, width), lambda b, c: (b * nc + c, 0)),
            pl.BlockSpec((1, RET_HEADS, RET_DK, RET_DV), lambda b, c: (b, 0, 0, 0)),
        ),
        out_shape=(
            jax.ShapeDtypeStruct((m, width), BF16),
            jax.ShapeDtypeStruct((batch, RET_HEADS, RET_DK, RET_DV), F32),
        ),
        compiler_params=_cparams(("arbitrary", "arbitrary")),
        name="prompt_ret",
    )(main, main, main, main, decay, rsc, zeta, gpow)


def _outproj_body(a_ref, r_ref, wa_ref, wr_ref, x_ref, g2_ref, x1_ref, h2_ref):
    mixed = (jnp.dot(a_ref[...], wa_ref[...], preferred_element_type=F32)
             + jnp.dot(r_ref[...], wr_ref[...], preferred_element_type=F32))
    x1 = x_ref[...] + mixed
    x1_ref[...] = x1
    ms = jnp.mean(x1 * x1, axis=-1, keepdims=True)
    h2_ref[...] = (x1 * lax.rsqrt(ms + EPS) * g2_ref[...]).astype(BF16)


def _out_projection(attn_o, rg, wa, wr, x2d, g2, tm):
    m = x2d.shape[0]
    half = attn_o.shape[1]
    in_specs = [
        pl.BlockSpec((tm, half), lambda i: (i, 0)),
        pl.BlockSpec((tm, half), lambda i: (i, 0)),
        pl.BlockSpec((half, D_MODEL), lambda i: (0, 0)),
        pl.BlockSpec((half, D_MODEL), lambda i: (0, 0)),
        pl.BlockSpec((tm, D_MODEL), lambda i: (i, 0)),
        pl.BlockSpec((1, D_MODEL), lambda i: (0, 0)),
    ]
    return pl.pallas_call(
        _outproj_body,
        grid=(m // tm,),
        in_specs=in_specs,
        out_specs=(pl.BlockSpec((tm, D_MODEL), lambda i: (i, 0)),
                   pl.BlockSpec((tm, D_MODEL), lambda i: (i, 0))),
        out_shape=(jax.ShapeDtypeStruct((m, D_MODEL), F32),
                   jax.ShapeDtypeStruct((m, D_MODEL), BF16)),
        compiler_params=_cparams(("arbitrary",)),
        name="out_proj",
    )(attn_o, rg, wa, wr, x2d, g2)


def _mlp_body(h2_ref, wu_ref, wd_ref, x1_ref, gf_ref, y_ref, acc_ref):
    f = pl.program_id(1)

    @pl.when(f == 0)
    def _init():
        acc_ref[...] = x1_ref[...]

    u = jnp.dot(h2_ref[...], wu_ref[...], preferred_element_type=F32)
    a = jnp.maximum(u, 0.0)
    acc_ref[...] += jnp.dot((a * a).astype(BF16), wd_ref[...], preferred_element_type=F32)

    @pl.when(f == pl.num_programs(1) - 1)
    def _final():
        x2 = acc_ref[...]
        ms = jnp.mean(x2 * x2, axis=-1, keepdims=True)
        y_ref[...] = x2 * lax.rsqrt(ms + EPS) * gf_ref[...]


def _mlp(h2, wu, wd, x1, gf, tm, tf):
    m = h2.shape[0]
    in_specs = [
        pl.BlockSpec((tm, D_MODEL), lambda i, f: (i, 0)),
        pl.BlockSpec((D_MODEL, tf), lambda i, f: (0, f)),
        pl.BlockSpec((tf, D_MODEL), lambda i, f: (f, 0)),
        pl.BlockSpec((tm, D_MODEL), lambda i, f: (i, 0)),
        pl.BlockSpec((1, D_MODEL), lambda i, f: (0, 0)),
    ]
    return pl.pallas_call(
        _mlp_body,
        grid=(m // tm, D_FF // tf),
        in_specs=in_specs,
        out_specs=pl.BlockSpec((tm, D_MODEL), lambda i, f: (i, 0)),
        out_shape=jax.ShapeDtypeStruct((m, D_MODEL), F32),
        scratch_shapes=[pltpu.VMEM((tm, D_MODEL), F32)],
        compiler_params=_cparams(("arbitrary", "arbitrary")),
        name="mlp",
    )(h2, wu, wd, x1, gf)


def _sidx_body(pt_ref, qi_ref, w_ref, kin_ref, *rest):
    n_pages = len(rest) - 1
    pages = rest[:n_pages]
    out_ref = rest[n_pages]
    qi = qi_ref[0]
    w = w_ref[0] * (IDX_HEAD_DIM ** -0.5)
    wbc = jnp.broadcast_to(w, (IDX_HEADS, LANES))
    for j in range(n_pages):
        pg = pages[j][0].astype(BF16)
        s = lax.dot_general(qi, pg, (((1,), (1,)), ((), ())), preferred_element_type=F32)
        out_ref[0, j:j + 1, :] = jnp.sum(jnp.maximum(s, 0.0) * wbc, axis=0, keepdims=True)
    sn = jnp.sum(qi.astype(F32) * kin_ref[0].astype(BF16).astype(F32), axis=1, keepdims=True)
    rn = jnp.sum(jnp.maximum(sn, 0.0) * w, axis=0, keepdims=True)
    lane = lax.broadcasted_iota(I32, (1, LANES), 1)
    out_ref[0, n_pages:n_pages + 1, :] = jnp.where(lane == 0, rn, -jnp.inf)


def _sample_index_scores(page_table, qi_s, wi_s, ki_s, cache_idx_k):
    nb, n_pages = page_table.shape
    page = cache_idx_k.shape[1]

    def page_spec(jj):
        return pl.BlockSpec((1, page, IDX_HEAD_DIM), lambda b, pt: (pt[b, jj], 0, 0))

    grid_spec = pltpu.PrefetchScalarGridSpec(
        num_scalar_prefetch=1,
        grid=(nb,),
        in_specs=[
            pl.BlockSpec((1, IDX_HEADS, IDX_HEAD_DIM), lambda b, pt: (b, 0, 0)),
            pl.BlockSpec((1, IDX_HEADS, 1), lambda b, pt: (b, 0, 0)),
            pl.BlockSpec((1, 1, IDX_HEAD_DIM), lambda b, pt: (b, 0, 0)),
        ] + [page_spec(jj) for jj in range(n_pages)],
        out_specs=pl.BlockSpec((1, n_pages + 1, LANES), lambda b, pt: (b, 0, 0)),
    )
    return pl.pallas_call(
        _sidx_body,
        grid_spec=grid_spec,
        out_shape=jax.ShapeDtypeStruct((nb, n_pages + 1, LANES), F32),
        compiler_params=_cparams(("arbitrary",)),
        name="sample_idx",
    )(page_table, qi_s, wi_s, ki_s, *([cache_idx_k] * n_pages))


def _ssel_body(sc_ref, tri_ref, sel_ref, keys):
    rows, width = sc_ref.shape
    nt = width // LANES
    n_valid = (nt - 1) * LANES + 1
    for kt in range(nt):
        col = kt * LANES + lax.broadcasted_iota(I32, (rows, LANES), 1)
        key = _score_to_key(sc_ref[:, kt * LANES:(kt + 1) * LANES])
        keys[kt] = jnp.where(col < n_valid, key, INT_MIN)

    def count_cmp(cmp):
        acc = jnp.zeros((rows, LANES), F32)
        for kt in range(nt):
            acc = acc + jnp.where(cmp(keys[kt]), 1.0, 0.0)
        return jnp.broadcast_to(jnp.sum(acc, axis=1, keepdims=True), (rows, LANES))

    t = _threshold_search(lambda cand: count_cmp(lambda kk: kk >= cand), 32, rows)
    need = float(TOPK_MAX) - count_cmp(lambda kk: kk > t)
    tie_off = jnp.zeros((rows, LANES), F32)
    for kt in range(nt):
        col = kt * LANES + lax.broadcasted_iota(I32, (rows, LANES), 1)
        kk = keys[kt]
        eq = kk == t
        tie = jnp.where(eq, 1.0, 0.0)
        rank = jnp.dot(tie.astype(BF16), tri_ref[...], preferred_element_type=F32) + tie_off
        sel = ((kk > t) | (eq & (rank <= need))) & (col < n_valid)
        sel_ref[:, kt * LANES:(kt + 1) * LANES] = jnp.where(sel, 1.0, 0.0)
        tie_off = tie_off + jnp.broadcast_to(jnp.sum(tie, axis=1, keepdims=True), (rows, LANES))


def _sample_select(scores2d, tri):
    rows, width = scores2d.shape
    return pl.pallas_call(
        _ssel_body,
        out_shape=jax.ShapeDtypeStruct((rows, width), F32),
        scratch_shapes=[pltpu.VMEM((width // LANES, rows, LANES), I32)],
        compiler_params=pltpu.CompilerParams(vmem_limit_bytes=VMEM_LIMIT),
        name="sample_select",
    )(scores2d, tri)


def _sattn_body(pt_ref, q_ref, sel_ref, kn_ref, vn_ref, *rest):
    n_pages = (len(rest) - 2) // 2
    kpages = rest[:n_pages]
    vpages = rest[n_pages:2 * n_pages]
    o_ref = rest[2 * n_pages]
    sbuf = rest[2 * n_pages + 1]
    q = q_ref[0]
    scale = ATTN_HEAD_DIM ** -0.5
    mrun = jnp.full((ATTN_HEADS, LANES), NEG_BIG, F32)
    for j in range(n_pages):
        kp = kpages[j][0].astype(BF16)
        s = lax.dot_general(q, kp, (((1,), (1,)), ((), ())), preferred_element_type=F32)
        sh = jnp.where(sel_ref[0, j:j + 1, :] > 0.5, s * scale, NEG_BIG)
        sbuf[j] = sh
        mrun = jnp.maximum(mrun, sh)
    kn = kn_ref[0].astype(BF16).astype(F32)
    sn = jnp.sum(q.astype(F32) * kn, axis=1, keepdims=True) * scale
    sel_n = sel_ref[0, n_pages:n_pages + 1, 0:1] > 0.5
    sn = jnp.where(sel_n, sn, NEG_BIG)
    m = jnp.maximum(jnp.max(mrun, axis=1, keepdims=True), sn)
    pn = jnp.exp(sn - m)
    l = pn
    acc = pn * vn_ref[0].astype(BF16).astype(F32)
    for j in range(n_pages):
        p = jnp.exp(sbuf[j] - m)
        l = l + jnp.sum(p, axis=1, keepdims=True)
        acc = acc + jnp.dot(p.astype(BF16), vpages[j][0].astype(BF16), preferred_element_type=F32)
    o_ref[0] = (acc / l).astype(BF16)


def _sample_attention(page_table, qa_s, sel3, ka_s, va_s, cache_k, cache_v):
    nb, n_pages = page_table.shape
    page = cache_k.shape[1]

    def page_spec(jj):
        return pl.BlockSpec((1, page, ATTN_HEAD_DIM), lambda b, pt: (pt[b, jj], 0, 0))

    grid_spec = pltpu.PrefetchScalarGridSpec(
        num_scalar_prefetch=1,
        grid=(nb,),
        in_specs=[
            pl.BlockSpec((1, ATTN_HEADS, ATTN_HEAD_DIM), lambda b, pt: (b, 0, 0)),
            pl.BlockSpec((1, n_pages + 1, LANES), lambda b, pt: (b, 0, 0)),
            pl.BlockSpec((1, 1, ATTN_HEAD_DIM), lambda b, pt: (b, 0, 0)),
            pl.BlockSpec((1, 1, ATTN_HEAD_DIM), lambda b, pt: (b, 0, 0)),
        ] + [page_spec(jj) for jj in range(n_pages)] * 2,
        out_specs=pl.BlockSpec((1, ATTN_HEADS, ATTN_HEAD_DIM), lambda b, pt: (b, 0, 0)),
        scratch_shapes=[pltpu.VMEM((n_pages, ATTN_HEADS, LANES), F32)],
    )
    return pl.pallas_call(
        _sattn_body,
        grid_spec=grid_spec,
        out_shape=jax.ShapeDtypeStruct((nb, ATTN_HEADS, ATTN_HEAD_DIM), BF16),
        compiler_params=_cparams(("arbitrary",)),
        name="sample_attn",
    )(page_table, qa_s, sel3, ka_s, va_s, *([cache_k] * n_pages), *([cache_v] * n_pages))


def _sret_body(qkvg_ref, st_ref, gam_ref, rg_ref, so_ref):
    ns = st_ref.shape[0]
    for s in range(ns):
        blk = qkvg_ref[s].astype(F32)
        q8 = blk[0:8]
        k8 = blk[8:16]
        v8 = blk[16:24]
        g8 = blk[24:32]
        q_t = q8.T
        k_t = k8.T
        qk = jnp.sum(q8 * k8, axis=1, keepdims=True)
        rows = []
        for h in range(RET_HEADS):
            r_old = st_ref[s, h]
            gam = gam_ref[h]
            qcol = jnp.broadcast_to(q_t[:, h:h + 1], (RET_DK, RET_DV))
            kcol = jnp.broadcast_to(k_t[:, h:h + 1], (RET_DK, RET_DV))
            vrow = v8[h:h + 1]
            q_r = jnp.sum(qcol * r_old, axis=0, keepdims=True)
            rows.append(gam * q_r + qk[h:h + 1] * vrow)
            so_ref[s, h] = gam * r_old + kcol * vrow
        ret = jnp.concatenate(rows, axis=0)
        rg_ref[s] = _gate(ret, g8).astype(BF16)


def _sample_retention(qkvg, state, gam, ns):
    nb = state.shape[0]
    return pl.pallas_call(
        _sret_body,
        grid=(nb // ns,),
        in_specs=[
            pl.BlockSpec((ns, 32, LANES), lambda i: (i, 0, 0)),
            pl.BlockSpec((ns, RET_HEADS, RET_DK, RET_DV), lambda i: (i, 0, 0, 0)),
            pl.BlockSpec((RET_HEADS, 1, LANES), lambda i: (0, 0, 0)),
        ],
        out_specs=(
            pl.BlockSpec((ns, RET_HEADS, RET_DV), lambda i: (i, 0, 0)),
            pl.BlockSpec((ns, RET_HEADS, RET_DK, RET_DV), lambda i: (i, 0, 0, 0)),
        ),
        out_shape=(
            jax.ShapeDtypeStruct((nb, RET_HEADS, RET_DV), BF16),
            jax.ShapeDtypeStruct(state.shape, F32),
        ),
        compiler_params=_cparams(("arbitrary",)),
        name="sample_ret",
    )(qkvg, state, gam)


def _permute_w_in(w_in):
    splits = (1024, 128, 128, 1024, 64, 16, 1024, 1024, 1024, 1024)
    offs = np.concatenate([[0], np.cumsum(splits)])
    qa, ka, va, qi, ki, wi, qr, kr, vr, gr = [
        w_in[:, int(offs[n]):int(offs[n + 1])] for n in range(10)]
    pad = jnp.zeros((w_in.shape[0], 48 + 128), w_in.dtype)
    return jnp.concatenate([qa, qi, qr, kr, vr, gr, ka, va, ki, wi, pad], axis=1).astype(BF16)


def _rotary_table(pos):
    half = RET_DK // 2
    inv = ROPE_BASE ** (-jnp.arange(half, dtype=F32) / half)
    ang = pos[:, None] * inv[None, :]
    cos = jnp.cos(ang)
    sin = jnp.sin(ang)
    return jnp.concatenate([cos, cos, -sin, sin], axis=1)


def _retention_constants():
    lg = jnp.log1p(-jnp.exp2(-5.0 - jnp.arange(RET_HEADS, dtype=F32)))
    n = RET_CHUNK
    i = jnp.arange(n, dtype=F32)
    diff = i[:, None] - i[None, :]
    decay = jnp.where(diff[None] >= 0,
                      jnp.exp(jnp.maximum(diff, 0.0)[None] * lg[:, None, None]), 0.0)
    rsc = jnp.exp((i + 1.0)[None, :] * lg[:, None])
    zeta = jnp.exp((n - 1.0 - i)[None, :] * lg[:, None])
    gpow = jnp.exp(n * lg)
    gam1 = jnp.exp(lg)
    rsc_b = jnp.broadcast_to(rsc[:, :, None], (RET_HEADS, n, RET_DV))
    zeta_b = jnp.broadcast_to(zeta[:, :, None], (RET_HEADS, n, RET_DK))
    gpow_b = jnp.broadcast_to(gpow[:, None, None], (RET_HEADS, 1, RET_DV))
    gam1_b = jnp.broadcast_to(gam1[:, None, None], (RET_HEADS, 1, LANES))
    return decay, rsc_b, zeta_b, gpow_b, gam1_b


def _upper_tri(n):
    r = lax.broadcasted_iota(I32, (n, n), 0)
    c = lax.broadcasted_iota(I32, (n, n), 1)
    return (r <= c).astype(BF16)


def kernel(x_prompt, x_sample, cache_k, cache_v, cache_idx_k, state_ret, page_table,
           norm1_g, w_in, idx_k_norm_g, idx_k_norm_b, w_out, norm2_g, w_up, w_down, final_norm_g):
    batch, seq, _ = x_prompt.shape
    nb = x_sample.shape[0]
    past_len = page_table.shape[1] * cache_k.shape[1]
    half_mix = ATTN_HEADS * ATTN_HEAD_DIM

    w_perm = _permute_w_in(w_in)
    wa = w_out[:half_mix].astype(BF16)
    wr = w_out[half_mix:].astype(BF16)
    wu = w_up.astype(BF16)
    wd = w_down.astype(BF16)
    g1 = norm1_g.reshape(1, D_MODEL)
    g2 = norm2_g.reshape(1, D_MODEL)
    gf = final_norm_g.reshape(1, D_MODEL)
    lng = idx_k_norm_g.reshape(1, IDX_HEAD_DIM)
    lnb = idx_k_norm_b.reshape(1, IDX_HEAD_DIM)
    decay, rsc_b, zeta_b, gpow_b, gam1_b = _retention_constants()

    xp = x_prompt.reshape(batch * seq, D_MODEL)
    cs_p = _rotary_table(jnp.arange(seq, dtype=F32))
    qa_p, qi_p, main_p, ka_p, va_p, ki_p, wi_p = _project(xp, g1, w_perm, cs_p, lng, lnb, tm=1024)
    attn_p = _prompt_attention(qa_p, qi_p, wi_p, ka_p, va_p, ki_p, _upper_tri(KEY_TILE), batch, seq)
    rg_p, ret_state_p = _prompt_retention(main_p, decay, rsc_b, zeta_b, gpow_b, batch, seq)
    x1_p, h2_p = _out_projection(attn_p, rg_p, wa, wr, xp, g2, tm=512)
    y_p = _mlp(h2_p, wu, wd, x1_p, gf, tm=512, tf=512)

    xs = x_sample.reshape(nb, D_MODEL)
    cs_s = _rotary_table(jnp.full((nb,), past_len, dtype=F32))
    qa_s, qi_s, main_s, ka_s, va_s, ki_s, wi_s = _project(xs, g1, w_perm, cs_s, lng, lnb, tm=nb)
    scores = _sample_index_scores(
        page_table,
        qi_s.transpose(1, 0, 2),
        wi_s.reshape(nb, IDX_HEADS, 1),
        ki_s.reshape(nb, 1, IDX_HEAD_DIM),
        cache_idx_k)
    n_rows = scores.shape[1]
    sel = _sample_select(scores.reshape(nb, n_rows * LANES), _upper_tri(LANES))
    attn_s = _sample_attention(
        page_table,
        qa_s.transpose(1, 0, 2),
        sel.reshape(nb, n_rows, LANES),
        ka_s.reshape(nb, 1, ATTN_HEAD_DIM),
        va_s.reshape(nb, 1, ATTN_HEAD_DIM),
        cache_k, cache_v)
    rg_s, ret_state_s = _sample_retention(main_s.reshape(nb, 32, LANES), state_ret, gam1_b, ns=4)
    x1_s, h2_s = _out_projection(attn_s.reshape(nb, half_mix), rg_s.reshape(nb, RET_HEADS * RET_DV),
                                 wa, wr, xs, g2, tm=nb)
    y_s = _mlp(h2_s, wu, wd, x1_s, gf, tm=nb, tf=512)

    return (
        y_p.reshape(batch, seq, D_MODEL),
        y_s.reshape(nb, 1, D_MODEL),
        ka_p.reshape(batch, seq, ATTN_HEAD_DIM),
        va_p.reshape(batch, seq, ATTN_HEAD_DIM),
        ki_p.reshape(batch, seq, IDX_HEAD_DIM),
        ret_state_p,
        ka_s.reshape(nb, 1, ATTN_HEAD_DIM),
        va_s.reshape(nb, 1, ATTN_HEAD_DIM),
        ki_s.reshape(nb, 1, IDX_HEAD_DIM),
        ret_state_s,
    )
```

```python
import functools

import numpy as np
import jax
import jax.numpy as jnp
from jax import lax
from jax.experimental import pallas as pl
from jax.experimental.pallas import tpu as pltpu

F32 = jnp.float32
BF16 = jnp.bfloat16
I32 = jnp.int32

D_MODEL = 2048
ATTN_HEADS = 8
ATTN_HEAD_DIM = 128
IDX_HEADS = 16
IDX_HEAD_DIM = 64
TOPK_MAX = 256
RET_HEADS = 8
RET_DK = 128
RET_DV = 128
RET_CHUNK = 128
ROPE_BASE = 10000.0
D_FF = 4 * D_MODEL
EPS = 1e-6
Q_BLOCK = 128

LANES = 128
PROJ_TILE = 512
N_PROJ_TILES = 13
KEY_TILE = 256
COUNT_ROWS = 64
INT_MIN = -2 ** 31
KEY_NEG_INF = -2 ** 31 + 0x7FFFFF
NEG_BIG = -1e30
VMEM_LIMIT = 56 * 1024 * 1024


def _cparams(sem):
    return pltpu.CompilerParams(dimension_semantics=sem, vmem_limit_bytes=VMEM_LIMIT)


def _proj_body(x_ref, g_ref, w_ref, cs_ref, lng_ref, lnb_ref,
               qa_ref, qi_ref, main_ref, ka_ref, va_ref, ki_ref, wi_ref, xn_ref):
    j = pl.program_id(1)

    @pl.when(j == 0)
    def _norm():
        x = x_ref[...]
        ms = jnp.mean(x * x, axis=-1, keepdims=True)
        xn_ref[...] = (x * lax.rsqrt(ms + EPS) * g_ref[...]).astype(BF16)

    acc = jnp.dot(xn_ref[...], w_ref[...], preferred_element_type=F32)

    @pl.when(j < 2)
    def _qa():
        for hh in range(4):
            qa_ref[hh] = acc[:, hh * 128:(hh + 1) * 128].astype(BF16)

    @pl.when((j >= 2) & (j < 4))
    def _qi():
        for hh in range(8):
            qi_ref[hh] = acc[:, hh * 64:(hh + 1) * 64].astype(BF16)

    @pl.when((j >= 4) & (j < 8))
    def _rot():
        cosf = cs_ref[:, :LANES]
        sinf = cs_ref[:, LANES:]
        scale = jnp.where(j >= 6, RET_DK ** -0.5, 1.0).astype(F32)
        for hh in range(4):
            xh = acc[:, hh * 128:(hh + 1) * 128]
            r = (xh * cosf + pltpu.roll(xh, 64, 1) * sinf) * scale
            main_ref[:, hh * 128:(hh + 1) * 128] = r.astype(BF16)

    @pl.when((j >= 8) & (j < 12))
    def _plain():
        main_ref[...] = acc.astype(BF16)

    @pl.when(j == 12)
    def _small():
        ka_ref[...] = acc[:, 0:128]
        va_ref[...] = acc[:, 128:256]
        kr = acc[:, 256:320]
        mu = jnp.mean(kr, axis=-1, keepdims=True)
        d = kr - mu
        var = jnp.mean(d * d, axis=-1, keepdims=True)
        ki_ref[...] = d * lax.rsqrt(var + EPS) * lng_ref[...] + lnb_ref[...]
        wi_ref[...] = acc[:, 320:336] * (IDX_HEADS ** -0.5)


def _project(x2d, g1, w_perm, cs, lng, lnb, tm):
    m = x2d.shape[0]
    n_pos_blocks = cs.shape[0] // tm
    grid = (m // tm, N_PROJ_TILES)
    out_shape = (
        jax.ShapeDtypeStruct((ATTN_HEADS, m, ATTN_HEAD_DIM), BF16),
        jax.ShapeDtypeStruct((IDX_HEADS, m, IDX_HEAD_DIM), BF16),
        jax.ShapeDtypeStruct((m, 4096), BF16),
        jax.ShapeDtypeStruct((m, ATTN_HEAD_DIM), F32),
        jax.ShapeDtypeStruct((m, ATTN_HEAD_DIM), F32),
        jax.ShapeDtypeStruct((m, IDX_HEAD_DIM), F32),
        jax.ShapeDtypeStruct((m, IDX_HEADS), F32),
    )
    in_specs = [
        pl.BlockSpec((tm, D_MODEL), lambda i, j: (i, 0)),
        pl.BlockSpec((1, D_MODEL), lambda i, j: (0, 0)),
        pl.BlockSpec((D_MODEL, PROJ_TILE), lambda i, j: (0, j)),
        pl.BlockSpec((tm, 2 * LANES), lambda i, j: (i % n_pos_blocks, 0)),
        pl.BlockSpec((1, IDX_HEAD_DIM), lambda i, j: (0, 0)),
        pl.BlockSpec((1, IDX_HEAD_DIM), lambda i, j: (0, 0)),
    ]
    out_specs = (
        pl.BlockSpec((4, tm, ATTN_HEAD_DIM), lambda i, j: (jnp.clip(j, 0, 1), i, 0)),
        pl.BlockSpec((8, tm, IDX_HEAD_DIM), lambda i, j: (jnp.clip(j - 2, 0, 1), i, 0)),
        pl.BlockSpec((tm, PROJ_TILE), lambda i, j: (i, jnp.clip(j - 4, 0, 7))),
        pl.BlockSpec((tm, ATTN_HEAD_DIM), lambda i, j: (i, 0)),
        pl.BlockSpec((tm, ATTN_HEAD_DIM), lambda i, j: (i, 0)),
        pl.BlockSpec((tm, IDX_HEAD_DIM), lambda i, j: (i, 0)),
        pl.BlockSpec((tm, IDX_HEADS), lambda i, j: (i, 0)),
    )
    return pl.pallas_call(
        _proj_body,
        grid=grid,
        in_specs=in_specs,
        out_specs=out_specs,
        out_shape=out_shape,
        scratch_shapes=[pltpu.VMEM((tm, D_MODEL), BF16)],
        compiler_params=_cparams(("arbitrary", "arbitrary")),
        name="proj",
    )(x2d, g1, w_perm, cs, lng, lnb)


def _key_to_float(key):
    bits = key ^ ((key >> 31) & 0x7FFFFFFF)
    return lax.bitcast_convert_type(bits, F32)


def _threshold_search(count_ge, n_iter, shape):
    def body(it, t):
        bit = lax.shift_left(jnp.int32(1), 31 - it)
        cand = t ^ bit
        cnt = count_ge(_key_to_float(cand))
        return jnp.where(cnt >= float(TOPK_MAX), cand, t)

    t = lax.fori_loop(0, n_iter, body, jnp.full(shape, INT_MIN, I32))
    return _key_to_float(jnp.maximum(t, KEY_NEG_INF))


def _attn_body(qa_ref, qi_ref, wit_ref, ka_ref, va_ref, ki_ref, tri_ref, o_ref,
               kbf, vtb, kibf, scr, sbuf, mrun, lrun, acc_s):
    qb = pl.program_id(1)
    n_heads_q = ATTN_HEADS * Q_BLOCK

    @pl.when(qb == 0)
    def _cast():
        kbf[...] = ka_ref[...].astype(BF16)
        kibf[...] = ki_ref[...].astype(BF16)
        for kt in range(vtb.shape[0]):
            vtb[kt] = va_ref[kt * KEY_TILE:(kt + 1) * KEY_TILE, :].T.astype(BF16)

    nk = (qb + 2) // 2
    wt = wit_ref[...] * (IDX_HEAD_DIM ** -0.5)
    qi2 = qi_ref[...].reshape(IDX_HEADS * Q_BLOCK, IDX_HEAD_DIM)
    qidx = qb * Q_BLOCK + lax.broadcasted_iota(I32, (KEY_TILE, Q_BLOCK), 1)
    kidx0 = lax.broadcasted_iota(I32, (KEY_TILE, Q_BLOCK), 0)

    def idx_body(kt, carry):
        off = pl.multiple_of(kt * KEY_TILE, KEY_TILE)
        kit = kibf[pl.ds(off, KEY_TILE), :]
        s = lax.dot_general(kit, qi2, (((1,), (1,)), ((), ())), preferred_element_type=F32)
        score = jnp.zeros((KEY_TILE, Q_BLOCK), F32)
        for h in range(IDX_HEADS):
            score = score + jnp.maximum(s[:, h * Q_BLOCK:(h + 1) * Q_BLOCK], 0.0) * wt[h:h + 1, :]
        scr[kt] = jnp.where(kidx0 + off <= qidx, score, -jnp.inf)
        return carry

    lax.fori_loop(0, nk, idx_body, 0)

    def count_cmp(cmp):
        def body(kt, acc):
            c = jnp.where(cmp(scr[kt]), 1.0, 0.0)
            return acc + jnp.sum(c.reshape(KEY_TILE // COUNT_ROWS, COUNT_ROWS, Q_BLOCK), axis=0)
        acc = lax.fori_loop(0, nk, body, jnp.zeros((COUNT_ROWS, Q_BLOCK), F32))
        return jnp.sum(acc, axis=0, keepdims=True)

    n_iter = jnp.where(qb >= TOPK_MAX // Q_BLOCK, 32, 0)
    tf = _threshold_search(lambda c: count_cmp(lambda sc: sc >= c), n_iter, (1, Q_BLOCK))
    need = float(TOPK_MAX) - count_cmp(lambda sc: sc > tf)
    excess = jnp.max(count_cmp(lambda sc: sc >= tf)) > float(TOPK_MAX)

    qa2 = qa_ref[...].reshape(n_heads_q, ATTN_HEAD_DIM)
    mrun[...] = jnp.full((8, n_heads_q), NEG_BIG, F32)

    def p1_body(kt, tie_off):
        off = pl.multiple_of(kt * KEY_TILE, KEY_TILE)
        sc = scr[kt]
        eq = sc == tf
        tie = jnp.where(eq, 1.0, 0.0)
        rank = lax.cond(
            excess,
            lambda: jnp.dot(tri_ref[...], tie.astype(BF16), preferred_element_type=F32) + tie_off,
            lambda: jnp.zeros((KEY_TILE, Q_BLOCK), F32))
        sel = ((sc > tf) | (eq & (rank <= need))) & (kidx0 + off <= qidx)
        s = lax.dot_general(kbf[pl.ds(off, KEY_TILE), :], qa2, (((1,), (1,)), ((), ())),
                            preferred_element_type=F32)
        for h in range(ATTN_HEADS):
            cs = slice(h * Q_BLOCK, (h + 1) * Q_BLOCK)
            sh = jnp.where(sel, s[:, cs] * (ATTN_HEAD_DIM ** -0.5), NEG_BIG)
            sbuf[kt, :, cs] = sh
            mrun[:, cs] = jnp.maximum(
                mrun[:, cs], jnp.max(sh.reshape(KEY_TILE // 8, 8, Q_BLOCK), axis=0))
        return tie_off + jnp.sum(tie, axis=0, keepdims=True)

    lax.fori_loop(0, nk, p1_body, jnp.zeros((1, Q_BLOCK), F32))

    m = jnp.max(mrun[...], axis=0, keepdims=True)
    lrun[...] = jnp.zeros((8, n_heads_q), F32)
    acc_s[...] = jnp.zeros((ATTN_HEAD_DIM, n_heads_q), F32)

    def p2_body(kt, carry):
        p = jnp.exp(sbuf[kt] - m)
        lrun[...] += jnp.sum(p.reshape(KEY_TILE // 8, 8, n_heads_q), axis=0)
        acc_s[...] += jnp.dot(vtb[kt], p.astype(BF16), preferred_element_type=F32)
        return carry

    lax.fori_loop(0, nk, p2_body, 0)

    out = acc_s[...] / jnp.sum(lrun[...], axis=0, keepdims=True)
    for h in range(ATTN_HEADS):
        cs = slice(h * Q_BLOCK, (h + 1) * Q_BLOCK)
        o_ref[:, cs] = out[:, cs].T.astype(BF16)


def _prompt_attention(qa_hm, qi_hm, wi_t, ka, va, ki, tri, batch, seq):
    nq = seq // Q_BLOCK
    nkt = seq // KEY_TILE
    m = batch * seq
    n_heads_q = ATTN_HEADS * Q_BLOCK
    in_specs = [
        pl.BlockSpec((ATTN_HEADS, Q_BLOCK, ATTN_HEAD_DIM), lambda b, q: (0, b * nq + q, 0)),
        pl.BlockSpec((IDX_HEADS, Q_BLOCK, IDX_HEAD_DIM), lambda b, q: (0, b * nq + q, 0)),
        pl.BlockSpec((IDX_HEADS, Q_BLOCK), lambda b, q: (0, b * nq + q)),
        pl.BlockSpec((seq, ATTN_HEAD_DIM), lambda b, q: (b, 0)),
        pl.BlockSpec((seq, ATTN_HEAD_DIM), lambda b, q: (b, 0)),
        pl.BlockSpec((seq, IDX_HEAD_DIM), lambda b, q: (b, 0)),
        pl.BlockSpec((KEY_TILE, KEY_TILE), lambda b, q: (0, 0)),
    ]
    return pl.pallas_call(
        _attn_body,
        grid=(batch, nq),
        in_specs=in_specs,
        out_specs=pl.BlockSpec((Q_BLOCK, ATTN_HEADS * ATTN_HEAD_DIM), lambda b, q: (b * nq + q, 0)),
        out_shape=jax.ShapeDtypeStruct((m, ATTN_HEADS * ATTN_HEAD_DIM), BF16),
        scratch_shapes=[
            pltpu.VMEM((seq, ATTN_HEAD_DIM), BF16),
            pltpu.VMEM((nkt, ATTN_HEAD_DIM, KEY_TILE), BF16),
            pltpu.VMEM((seq, IDX_HEAD_DIM), BF16),
            pltpu.VMEM((nkt, KEY_TILE, Q_BLOCK), F32),
            pltpu.VMEM((nkt, KEY_TILE, n_heads_q), F32),
            pltpu.VMEM((8, n_heads_q), F32),
            pltpu.VMEM((8, n_heads_q), F32),
            pltpu.VMEM((ATTN_HEAD_DIM, n_heads_q), F32),
        ],
        compiler_params=_cparams(("arbitrary", "arbitrary")),
        name="prompt_attn",
    )(qa_hm, qi_hm, wi_t, ka, va, ki, tri)


def _gate(o, g):
    rn = o * lax.rsqrt(jnp.mean(o * o, axis=-1, keepdims=True) + EPS)
    return rn * (g / (1.0 + jnp.exp(-g)))


def _ret_body(q_ref, k_ref, v_ref, g_ref, decay_ref, rsc_ref, zeta_ref, gpow_ref,
              rg_ref, st_ref):
    c = pl.program_id(1)

    @pl.when(c == 0)
    def _init():
        st_ref[...] = jnp.zeros(st_ref.shape, F32)

    for h in range(RET_HEADS):
        sl = slice(h * 128, (h + 1) * 128)
        q = q_ref[:, sl]
        k = k_ref[:, sl]
        v = v_ref[:, sl]
        r_old = st_ref[0, h]
        qk = lax.dot_general(q, k, (((1,), (1,)), ((), ())), preferred_element_type=F32)
        inner = jnp.dot((qk * decay_ref[h]).astype(BF16), v, preferred_element_type=F32)
        cross = jnp.dot(q, r_old.astype(BF16), preferred_element_type=F32) * rsc_ref[h]
        kz = (k.astype(F32) * zeta_ref[h]).astype(BF16)
        upd = lax.dot_general(kz, v, (((0,), (0,)), ((), ())), preferred_element_type=F32)
        st_ref[0, h] = r_old * gpow_ref[h] + upd
        rg_ref[:, sl] = _gate(inner + cross, g_ref[:, sl].astype(F32)).astype(BF16)


def _prompt_retention(main, decay, rsc, zeta, gpow, batch, seq):
    nc = seq // RET_CHUNK
    m = batch * seq
    width = RET_HEADS * RET_DV
    const3 = lambda b, c: (0, 0, 0)
    in_specs = [
        pl.BlockSpec((RET_CHUNK, width), lambda b, c: (b * nc + c, 0)),
        pl.BlockSpec((RET_CHUNK, width), lambda b, c: (b * nc + c, 1)),
        pl.BlockSpec((RET_CHUNK, width), lambda b, c: (b * nc + c, 2)),
        pl.BlockSpec((RET_CHUNK, width), lambda b, c: (b * nc + c, 3)),
        pl.BlockSpec((RET_HEADS, RET_CHUNK, RET_CHUNK), const3),
        pl.BlockSpec((RET_HEADS, RET_CHUNK, RET_DV), const3),
        pl.BlockSpec((RET_HEADS, RET_CHUNK, RET_DK), const3),
        pl.BlockSpec((RET_HEADS, 1, RET_DV), const3),
    ]
    return pl.pallas_call(
        _ret_body,
        grid=(batch, nc),
        in_specs=in_specs,
        out_specs=(
            pl.BlockSpec((RET_CHUNK, width), lambda b, c: (b * nc + c, 0)),
            pl.BlockSpec((1, RET_HEADS, RET_DK, RET_DV), lambda b, c: (b, 0, 0, 0)),
        ),
        out_shape=(
            jax.ShapeDtypeStruct((m, width), BF16),
            jax.ShapeDtypeStruct((batch, RET_HEADS, RET_DK, RET_DV), F32),
        ),
        compiler_params=_cparams(("arbitrary", "arbitrary")),
        name="prompt_ret",
    )(main, main, main, main, decay, rsc, zeta, gpow)


def _outproj_body(a_ref, r_ref, wa_ref, wr_ref, x_ref, g2_ref, x1_ref, h2_ref):
    mixed = (jnp.dot(a_ref[...], wa_ref[...], preferred_element_type=F32)
             + jnp.dot(r_ref[...], wr_ref[...], preferred_element_type=F32))
    x1 = x_ref[...] + mixed
    x1_ref[...] = x1
    ms = jnp.mean(x1 * x1, axis=-1, keepdims=True)
    h2_ref[...] = (x1 * lax.rsqrt(ms + EPS) * g2_ref[...]).astype(BF16)


def _out_projection(attn_o, rg, wa, wr, x2d, g2, tm):
    m = x2d.shape[0]
    half = attn_o.shape[1]
    in_specs = [
        pl.BlockSpec((tm, half), lambda i: (i, 0)),
        pl.BlockSpec((tm, half), lambda i: (i, 0)),
        pl.BlockSpec((half, D_MODEL), lambda i: (0, 0)),
        pl.BlockSpec((half, D_MODEL), lambda i: (0, 0)),
        pl.BlockSpec((tm, D_MODEL), lambda i: (i, 0)),
        pl.BlockSpec((1, D_MODEL), lambda i: (0, 0)),
    ]
    return pl.pallas_call(
        _outproj_body,
        grid=(m // tm,),
        in_specs=in_specs,
        out_specs=(pl.BlockSpec((tm, D_MODEL), lambda i: (i, 0)),
                   pl.BlockSpec((tm, D_MODEL), lambda i: (i, 0))),
        out_shape=(jax.ShapeDtypeStruct((m, D_MODEL), F32),
                   jax.ShapeDtypeStruct((m, D_MODEL), BF16)),
        compiler_params=_cparams(("arbitrary",)),
        name="out_proj",
    )(attn_o, rg, wa, wr, x2d, g2)


def _mlp_body(h2_ref, wu_ref, wd_ref, x1_ref, gf_ref, y_ref, acc_ref):
    f = pl.program_id(1)

    @pl.when(f == 0)
    def _init():
        acc_ref[...] = x1_ref[...]

    u = jnp.dot(h2_ref[...], wu_ref[...], preferred_element_type=F32)
    a = jnp.maximum(u, 0.0)
    acc_ref[...] += jnp.dot((a * a).astype(BF16), wd_ref[...], preferred_element_type=F32)

    @pl.when(f == pl.num_programs(1) - 1)
    def _final():
        x2 = acc_ref[...]
        ms = jnp.mean(x2 * x2, axis=-1, keepdims=True)
        y_ref[...] = x2 * lax.rsqrt(ms + EPS) * gf_ref[...]


def _mlp(h2, wu, wd, x1, gf, tm, tf):
    m = h2.shape[0]
    in_specs = [
        pl.BlockSpec((tm, D_MODEL), lambda i, f: (i, 0)),
        pl.BlockSpec((D_MODEL, tf), lambda i, f: (0, f)),
        pl.BlockSpec((tf, D_MODEL), lambda i, f: (f, 0)),
        pl.BlockSpec((tm, D_MODEL), lambda i, f: (i, 0)),
        pl.BlockSpec((1, D_MODEL), lambda i, f: (0, 0)),
    ]
    return pl.pallas_call(
        _mlp_body,
        grid=(m // tm, D_FF // tf),
        in_specs=in_specs,
        out_specs=pl.BlockSpec((tm, D_MODEL), lambda i, f: (i, 0)),
        out_shape=jax.ShapeDtypeStruct((m, D_MODEL), F32),
        scratch_shapes=[pltpu.VMEM((tm, D_MODEL), F32)],
        compiler_params=_cparams(("arbitrary", "arbitrary")),
        name="mlp",
    )(h2, wu, wd, x1, gf)


def _sidx_body(pt_ref, qi_ref, w_ref, kin_ref, *rest):
    n_pages = len(rest) - 2
    pages = rest[:n_pages]
    out_ref = rest[n_pages]
    kt_s = rest[n_pages + 1]
    page = pages[0].shape[2]
    past = n_pages * page
    qi = qi_ref[0]
    w = w_ref[0] * (IDX_HEAD_DIM ** -0.5)
    for j in range(n_pages):
        kt_s[:, j * page:(j + 1) * page] = pages[j][0].astype(BF16)
    s = jnp.dot(qi, kt_s[...], preferred_element_type=F32)
    out_ref[0, :, 0:past] = jnp.sum(jnp.maximum(s, 0.0) * w, axis=0, keepdims=True)
    sn = jnp.sum(qi.astype(F32) * kin_ref[0].astype(BF16).astype(F32), axis=1, keepdims=True)
    rn = jnp.sum(jnp.maximum(sn, 0.0) * w, axis=0, keepdims=True)
    lane = lax.broadcasted_iota(I32, (1, LANES), 1)
    out_ref[0, :, past:past + LANES] = jnp.where(lane == 0, rn, -jnp.inf)


def _sample_index_scores(page_table, qi_s, wi_s, ki_s, cache_idx_k_t):
    nb, n_pages = page_table.shape
    page = cache_idx_k_t.shape[2]
    width = n_pages * page + LANES

    def page_spec(jj):
        return pl.BlockSpec((1, IDX_HEAD_DIM, page), lambda b, pt: (pt[b, jj], 0, 0))

    grid_spec = pltpu.PrefetchScalarGridSpec(
        num_scalar_prefetch=1,
        grid=(nb,),
        in_specs=[
            pl.BlockSpec((1, IDX_HEADS, IDX_HEAD_DIM), lambda b, pt: (b, 0, 0)),
            pl.BlockSpec((1, IDX_HEADS, 1), lambda b, pt: (b, 0, 0)),
            pl.BlockSpec((1, 1, IDX_HEAD_DIM), lambda b, pt: (b, 0, 0)),
        ] + [page_spec(jj) for jj in range(n_pages)],
        out_specs=pl.BlockSpec((1, 1, width), lambda b, pt: (b, 0, 0)),
        scratch_shapes=[pltpu.VMEM((IDX_HEAD_DIM, n_pages * page), BF16)],
    )
    return pl.pallas_call(
        _sidx_body,
        grid_spec=grid_spec,
        out_shape=jax.ShapeDtypeStruct((nb, 1, width), F32),
        compiler_params=_cparams(("arbitrary",)),
        name="sample_idx",
    )(page_table, qi_s, wi_s, ki_s, *([cache_idx_k_t] * n_pages))


def _ssel_body(sc_ref, tri_ref, sel_ref):
    rows, width = sc_ref.shape
    nt = width // LANES
    n_valid = (nt - 1) * LANES + 1

    def tile(kt):
        return sc_ref[:, kt * LANES:(kt + 1) * LANES]

    def count_cmp(cmp):
        acc = jnp.zeros((rows, LANES), F32)
        for kt in range(nt):
            acc = acc + jnp.where(cmp(tile(kt)), 1.0, 0.0)
        return jnp.broadcast_to(jnp.sum(acc, axis=1, keepdims=True), (rows, LANES))

    tf = _threshold_search(lambda c: count_cmp(lambda sc: sc >= c), 32, (rows, LANES))
    need = float(TOPK_MAX) - count_cmp(lambda sc: sc > tf)
    tie_off = jnp.zeros((rows, LANES), F32)
    for kt in range(nt):
        col = kt * LANES + lax.broadcasted_iota(I32, (rows, LANES), 1)
        sc = tile(kt)
        eq = sc == tf
        tie = jnp.where(eq, 1.0, 0.0)
        rank = jnp.dot(tie.astype(BF16), tri_ref[...], preferred_element_type=F32) + tie_off
        sel = ((sc > tf) | (eq & (rank <= need))) & (col < n_valid)
        sel_ref[:, kt * LANES:(kt + 1) * LANES] = jnp.where(sel, 1.0, 0.0)
        tie_off = tie_off + jnp.broadcast_to(jnp.sum(tie, axis=1, keepdims=True), (rows, LANES))


def _sample_select(scores2d, tri):
    rows, width = scores2d.shape
    return pl.pallas_call(
        _ssel_body,
        out_shape=jax.ShapeDtypeStruct((rows, width), F32),
        compiler_params=pltpu.CompilerParams(vmem_limit_bytes=VMEM_LIMIT),
        name="sample_select",
    )(scores2d, tri)


def _sattn_body(pt_ref, q_ref, sel_ref, kn_ref, vn_ref, *rest):
    n_pages = (len(rest) - 3) // 2
    kpages = rest[:n_pages]
    vpages = rest[n_pages:2 * n_pages]
    o_ref = rest[2 * n_pages]
    k_s = rest[2 * n_pages + 1]
    v_s = rest[2 * n_pages + 2]
    page = kpages[0].shape[1]
    past = n_pages * page
    q = q_ref[0]
    scale = ATTN_HEAD_DIM ** -0.5
    for j in range(n_pages):
        k_s[j * page:(j + 1) * page, :] = kpages[j][0].astype(BF16)
        v_s[j * page:(j + 1) * page, :] = vpages[j][0].astype(BF16)
    s = lax.dot_general(q, k_s[...], (((1,), (1,)), ((), ())), preferred_element_type=F32)
    s = jnp.where(sel_ref[0, :, 0:past] > 0.5, s * scale, NEG_BIG)
    kn = kn_ref[0].astype(BF16).astype(F32)
    sn = jnp.sum(q.astype(F32) * kn, axis=1, keepdims=True) * scale
    sn = jnp.where(sel_ref[0, :, past:past + 1] > 0.5, sn, NEG_BIG)
    m = jnp.maximum(jnp.max(s, axis=1, keepdims=True), sn)
    pn = jnp.exp(sn - m)
    p = jnp.exp(s - m)
    l = pn + jnp.sum(p, axis=1, keepdims=True)
    acc = (pn * vn_ref[0].astype(BF16).astype(F32)
           + jnp.dot(p.astype(BF16), v_s[...], preferred_element_type=F32))
    o_ref[0] = (acc / l).astype(BF16)


def _sample_attention(page_table, qa_s, sel3, ka_s, va_s, cache_k, cache_v):
    nb, n_pages = page_table.shape
    page = cache_k.shape[1]

    def page_spec(jj):
        return pl.BlockSpec((1, page, ATTN_HEAD_DIM), lambda b, pt: (pt[b, jj], 0, 0))

    grid_spec = pltpu.PrefetchScalarGridSpec(
        num_scalar_prefetch=1,
        grid=(nb,),
        in_specs=[
            pl.BlockSpec((1, ATTN_HEADS, ATTN_HEAD_DIM), lambda b, pt: (b, 0, 0)),
            pl.BlockSpec((1, 1, sel3.shape[2]), lambda b, pt: (b, 0, 0)),
            pl.BlockSpec((1, 1, ATTN_HEAD_DIM), lambda b, pt: (b, 0, 0)),
            pl.BlockSpec((1, 1, ATTN_HEAD_DIM), lambda b, pt: (b, 0, 0)),
        ] + [page_spec(jj) for jj in range(n_pages)] * 2,
        out_specs=pl.BlockSpec((1, ATTN_HEADS, ATTN_HEAD_DIM), lambda b, pt: (b, 0, 0)),
        scratch_shapes=[pltpu.VMEM((n_pages * page, ATTN_HEAD_DIM), BF16),
                        pltpu.VMEM((n_pages * page, ATTN_HEAD_DIM), BF16)],
    )
    return pl.pallas_call(
        _sattn_body,
        grid_spec=grid_spec,
        out_shape=jax.ShapeDtypeStruct((nb, ATTN_HEADS, ATTN_HEAD_DIM), BF16),
        compiler_params=_cparams(("arbitrary",)),
        name="sample_attn",
    )(page_table, qa_s, sel3, ka_s, va_s, *([cache_k] * n_pages), *([cache_v] * n_pages))


def _sret_body(qkvg_ref, st_ref, gam_ref, rg_ref, so_ref):
    ns = st_ref.shape[0]
    for s in range(ns):
        blk = qkvg_ref[s].astype(F32)
        q8 = blk[0:8]
        k8 = blk[8:16]
        v8 = blk[16:24]
        g8 = blk[24:32]
        q_t = q8.T
        k_t = k8.T
        qk = jnp.sum(q8 * k8, axis=1, keepdims=True)
        rows = []
        for h in range(RET_HEADS):
            r_old = st_ref[s, h]
            gam = gam_ref[h]
            qcol = jnp.broadcast_to(q_t[:, h:h + 1], (RET_DK, RET_DV))
            kcol = jnp.broadcast_to(k_t[:, h:h + 1], (RET_DK, RET_DV))
            vrow = v8[h:h + 1]
            q_r = jnp.sum(qcol * r_old, axis=0, keepdims=True)
            rows.append(gam * q_r + qk[h:h + 1] * vrow)
            so_ref[s, h] = gam * r_old + kcol * vrow
        ret = jnp.concatenate(rows, axis=0)
        rg_ref[s] = _gate(ret, g8).astype(BF16)


def _sample_retention(qkvg, state, gam, ns):
    nb = state.shape[0]
    return pl.pallas_call(
        _sret_body,
        grid=(nb // ns,),
        in_specs=[
            pl.BlockSpec((ns, 32, LANES), lambda i: (i, 0, 0)),
            pl.BlockSpec((ns, RET_HEADS, RET_DK, RET_DV), lambda i: (i, 0, 0, 0)),
            pl.BlockSpec((RET_HEADS, 1, LANES), lambda i: (0, 0, 0)),
        ],
        out_specs=(
            pl.BlockSpec((ns, RET_HEADS, RET_DV), lambda i: (i, 0, 0)),
            pl.BlockSpec((ns, RET_HEADS, RET_DK, RET_DV), lambda i: (i, 0, 0, 0)),
        ),
        out_shape=(
            jax.ShapeDtypeStruct((nb, RET_HEADS, RET_DV), BF16),
            jax.ShapeDtypeStruct(state.shape, F32),
        ),
        compiler_params=_cparams(("arbitrary",)),
        name="sample_ret",
    )(qkvg, state, gam)


def _permute_w_in(w_in):
    splits = (1024, 128, 128, 1024, 64, 16, 1024, 1024, 1024, 1024)
    offs = np.concatenate([[0], np.cumsum(splits)])
    qa, ka, va, qi, ki, wi, qr, kr, vr, gr = [
        w_in[:, int(offs[n]):int(offs[n + 1])] for n in range(10)]
    pad = jnp.zeros((w_in.shape[0], 48 + 128), w_in.dtype)
    return jnp.concatenate([qa, qi, qr, kr, vr, gr, ka, va, ki, wi, pad], axis=1).astype(BF16)


def _rotary_table(pos):
    half = RET_DK // 2
    inv = ROPE_BASE ** (-jnp.arange(half, dtype=F32) / half)
    ang = pos[:, None] * inv[None, :]
    cos = jnp.cos(ang)
    sin = jnp.sin(ang)
    return jnp.concatenate([cos, cos, -sin, sin], axis=1)


def _retention_constants():
    lg = jnp.log1p(-jnp.exp2(-5.0 - jnp.arange(RET_HEADS, dtype=F32)))
    n = RET_CHUNK
    i = jnp.arange(n, dtype=F32)
    diff = i[:, None] - i[None, :]
    decay = jnp.where(diff[None] >= 0,
                      jnp.exp(jnp.maximum(diff, 0.0)[None] * lg[:, None, None]), 0.0)
    rsc = jnp.exp((i + 1.0)[None, :] * lg[:, None])
    zeta = jnp.exp((n - 1.0 - i)[None, :] * lg[:, None])
    gpow = jnp.exp(n * lg)
    gam1 = jnp.exp(lg)
    rsc_b = jnp.broadcast_to(rsc[:, :, None], (RET_HEADS, n, RET_DV))
    zeta_b = jnp.broadcast_to(zeta[:, :, None], (RET_HEADS, n, RET_DK))
    gpow_b = jnp.broadcast_to(gpow[:, None, None], (RET_HEADS, 1, RET_DV))
    gam1_b = jnp.broadcast_to(gam1[:, None, None], (RET_HEADS, 1, LANES))
    return decay, rsc_b, zeta_b, gpow_b, gam1_b


def _upper_tri(n):
    r = lax.broadcasted_iota(I32, (n, n), 0)
    c = lax.broadcasted_iota(I32, (n, n), 1)
    return (r <= c).astype(BF16)


def kernel(x_prompt, x_sample, cache_k, cache_v, cache_idx_k, state_ret, page_table,
           norm1_g, w_in, idx_k_norm_g, idx_k_norm_b, w_out, norm2_g, w_up, w_down, final_norm_g):
    batch, seq, _ = x_prompt.shape
    nb = x_sample.shape[0]
    past_len = page_table.shape[1] * cache_k.shape[1]
    half_mix = ATTN_HEADS * ATTN_HEAD_DIM

    w_perm = _permute_w_in(w_in)
    wa = w_out[:half_mix].astype(BF16)
    wr = w_out[half_mix:].astype(BF16)
    wu = w_up.astype(BF16)
    wd = w_down.astype(BF16)
    g1 = norm1_g.reshape(1, D_MODEL)
    g2 = norm2_g.reshape(1, D_MODEL)
    gf = final_norm_g.reshape(1, D_MODEL)
    lng = idx_k_norm_g.reshape(1, IDX_HEAD_DIM)
    lnb = idx_k_norm_b.reshape(1, IDX_HEAD_DIM)
    decay, rsc_b, zeta_b, gpow_b, gam1_b = _retention_constants()

    xp = x_prompt.reshape(batch * seq, D_MODEL)
    cs_p = _rotary_table(jnp.arange(seq, dtype=F32))
    qa_p, qi_p, main_p, ka_p, va_p, ki_p, wi_p = _project(xp, g1, w_perm, cs_p, lng, lnb, tm=1024)
    attn_p = _prompt_attention(qa_p, qi_p, wi_p.T, ka_p, va_p, ki_p, _upper_tri(KEY_TILE).T,
                               batch, seq)
    rg_p, ret_state_p = _prompt_retention(main_p, decay, rsc_b, zeta_b, gpow_b, batch, seq)
    x1_p, h2_p = _out_projection(attn_p, rg_p, wa, wr, xp, g2, tm=512)
    y_p = _mlp(h2_p, wu, wd, x1_p, gf, tm=512, tf=512)

    xs = x_sample.reshape(nb, D_MODEL)
    cs_s = _rotary_table(jnp.full((nb,), past_len, dtype=F32))
    qa_s, qi_s, main_s, ka_s, va_s, ki_s, wi_s = _project(xs, g1, w_perm, cs_s, lng, lnb, tm=nb)
    scores = _sample_index_scores(
        page_table,
        qi_s.transpose(1, 0, 2),
        wi_s.reshape(nb, IDX_HEADS, 1),
        ki_s.reshape(nb, 1, IDX_HEAD_DIM),
        jnp.swapaxes(cache_idx_k, 1, 2))
    width = scores.shape[2]
    sel = _sample_select(scores.reshape(nb, width), _upper_tri(LANES))
    attn_s = _sample_attention(
        page_table,
        qa_s.transpose(1, 0, 2),
        sel.reshape(nb, 1, width),
        ka_s.reshape(nb, 1, ATTN_HEAD_DIM),
        va_s.reshape(nb, 1, ATTN_HEAD_DIM),
        cache_k, cache_v)
    rg_s, ret_state_s = _sample_retention(main_s.reshape(nb, 32, LANES), state_ret, gam1_b, ns=4)
    x1_s, h2_s = _out_projection(attn_s.reshape(nb, half_mix), rg_s.reshape(nb, RET_HEADS * RET_DV),
                                 wa, wr, xs, g2, tm=nb)
    y_s = _mlp(h2_s, wu, wd, x1_s, gf, tm=nb, tf=512)

    return (
        y_p.reshape(batch, seq, D_MODEL),
        y_s.reshape(nb, 1, D_MODEL),
        ka_p.reshape(batch, seq, ATTN_HEAD_DIM),
        va_p.reshape(batch, seq, ATTN_HEAD_DIM),
        ki_p.reshape(batch, seq, IDX_HEAD_DIM),
        ret_state_p,
        ka_s.reshape(nb, 1, ATTN_HEAD_DIM),
        va_s.reshape(nb, 1, ATTN_HEAD_DIM),
        ki_s.reshape(nb, 1, IDX_HEAD_DIM),
        ret_state_s,
    )
```

```python
import functools

import numpy as np
import jax
import jax.numpy as jnp
from jax import lax
from jax.experimental import pallas as pl
from jax.experimental.pallas import tpu as pltpu

F32 = jnp.float32
BF16 = jnp.bfloat16
I32 = jnp.int32

D_MODEL = 2048
ATTN_HEADS = 8
ATTN_HEAD_DIM = 128
IDX_HEADS = 16
IDX_HEAD_DIM = 64
TOPK_MAX = 256
RET_HEADS = 8
RET_DK = 128
RET_DV = 128
RET_CHUNK = 128
ROPE_BASE = 10000.0
D_FF = 4 * D_MODEL
EPS = 1e-6
Q_BLOCK = 128

OFF_QA, OFF_KA, OFF_VA, OFF_QI, OFF_KI, OFF_WI = 0, 1024, 1152, 1280, 2304, 2368
OFF_QR, OFF_KR, OFF_VR, OFF_GR = 2384, 3408, 4432, 5456
RET_WIDTH = RET_HEADS * RET_DV

LANES = 128
PROJ_TILE = 512
KEY_TILE = 256
COUNT_ROWS = 64
INT_MIN = -2 ** 31
KEY_NEG_INF = -2 ** 31 + 0x7FFFFF
NEG_BIG = -1e30
VMEM_LIMIT = 56 * 1024 * 1024


def _cparams(sem):
    return pltpu.CompilerParams(dimension_semantics=sem, vmem_limit_bytes=VMEM_LIMIT)


def _resident(shape):
    zeros = (0,) * len(shape)
    return pl.BlockSpec(shape, lambda *_: zeros, pipeline_mode=pl.Buffered(1))


def _normed_input(x_ref, g_ref, xn_ref):
    x = x_ref[...]
    ms = jnp.mean(x * x, axis=-1, keepdims=True)
    xn_ref[...] = (x * lax.rsqrt(ms + EPS) * g_ref[...]).astype(BF16)


def _matmul_rows(xn_ref, wt_ref, r0, n):
    return lax.dot_general(xn_ref[...], wt_ref[r0:r0 + n, :], (((1,), (1,)), ((), ())),
                           preferred_element_type=F32)


def _proj_attn_body(x_ref, g_ref, wt_ref, lng_ref, lnb_ref,
                    qa_ref, qi_ref, ka_ref, va_ref, ki_ref, kd_ref, wi_ref, xn_ref):
    _normed_input(x_ref, g_ref, xn_ref)
    mm = functools.partial(_matmul_rows, xn_ref, wt_ref)
    for t in range(ATTN_HEADS * ATTN_HEAD_DIM // PROJ_TILE):
        acc = mm(OFF_QA + t * PROJ_TILE, PROJ_TILE)
        for hh in range(4):
            qa_ref[4 * t + hh] = acc[:, hh * LANES:(hh + 1) * LANES].astype(BF16)
    for t in range(IDX_HEADS * IDX_HEAD_DIM // PROJ_TILE):
        acc = mm(OFF_QI + t * PROJ_TILE, PROJ_TILE)
        for hh in range(4):
            qi_ref[4 * t + hh] = acc[:, hh * LANES:(hh + 1) * LANES].astype(BF16)
    ka_ref[...] = mm(OFF_KA, ATTN_HEAD_DIM)
    va_ref[...] = mm(OFF_VA, ATTN_HEAD_DIM)
    kw = mm(OFF_KI, LANES)
    lane = lax.broadcasted_iota(I32, kw.shape, 1)
    is_k = lane < IDX_HEAD_DIM
    mu = jnp.sum(jnp.where(is_k, kw, 0.0), axis=-1, keepdims=True) * (1.0 / IDX_HEAD_DIM)
    d = jnp.where(is_k, kw - mu, 0.0)
    var = jnp.sum(d * d, axis=-1, keepdims=True) * (1.0 / IDX_HEAD_DIM)
    kn = d * lax.rsqrt(var + EPS) * lng_ref[...] + lnb_ref[...]
    ki_ref[...] = kn[:, :IDX_HEAD_DIM]
    kd_ref[...] = jnp.where(is_k, kn, pltpu.roll(kn, IDX_HEAD_DIM, 1)).astype(BF16)
    wi_ref[...] = kw[:, IDX_HEAD_DIM:IDX_HEAD_DIM + IDX_HEADS] * (IDX_HEADS ** -0.5)


def _proj_ret_body(x_ref, g_ref, wt_ref, cs_ref, main_ref, xn_ref):
    _normed_input(x_ref, g_ref, xn_ref)
    base = OFF_QR
    cosf = cs_ref[:, :LANES]
    sinf = cs_ref[:, LANES:]
    tiles = RET_WIDTH // PROJ_TILE
    for seg, (off, scale) in enumerate(((OFF_QR, None), (OFF_KR, RET_DK ** -0.5))):
        for t in range(tiles):
            acc = _matmul_rows(xn_ref, wt_ref, off - base + t * PROJ_TILE, PROJ_TILE)
            for hh in range(PROJ_TILE // LANES):
                xh = acc[:, hh * LANES:(hh + 1) * LANES]
                r = xh * cosf + pltpu.roll(xh, RET_DK // 2, 1) * sinf
                if scale is not None:
                    r = r * scale
                c0 = seg * RET_WIDTH + t * PROJ_TILE + hh * LANES
                main_ref[:, c0:c0 + LANES] = r.astype(BF16)
    for seg, off in ((2, OFF_VR), (3, OFF_GR)):
        for t in range(tiles):
            acc = _matmul_rows(xn_ref, wt_ref, off - base + t * PROJ_TILE, PROJ_TILE)
            c0 = seg * RET_WIDTH + t * PROJ_TILE
            main_ref[:, c0:c0 + PROJ_TILE] = acc.astype(BF16)


def _project_attn(x2d, g1, wt_attn, lng, lnb, tm):
    m = x2d.shape[0]
    row = lambda i: (i, 0)
    out_shape = (
        jax.ShapeDtypeStruct((ATTN_HEADS, m, ATTN_HEAD_DIM), BF16),
        jax.ShapeDtypeStruct((IDX_HEADS // 2, m, LANES), BF16),
        jax.ShapeDtypeStruct((m, ATTN_HEAD_DIM), F32),
        jax.ShapeDtypeStruct((m, ATTN_HEAD_DIM), F32),
        jax.ShapeDtypeStruct((m, IDX_HEAD_DIM), F32),
        jax.ShapeDtypeStruct((m, LANES), BF16),
        jax.ShapeDtypeStruct((m, IDX_HEADS), F32),
    )
    out_specs = (
        pl.BlockSpec((ATTN_HEADS, tm, ATTN_HEAD_DIM), lambda i: (0, i, 0)),
        pl.BlockSpec((IDX_HEADS // 2, tm, LANES), lambda i: (0, i, 0)),
        pl.BlockSpec((tm, ATTN_HEAD_DIM), row),
        pl.BlockSpec((tm, ATTN_HEAD_DIM), row),
        pl.BlockSpec((tm, IDX_HEAD_DIM), row),
        pl.BlockSpec((tm, LANES), row),
        pl.BlockSpec((tm, IDX_HEADS), row),
    )
    return pl.pallas_call(
        _proj_attn_body,
        grid=(m // tm,),
        in_specs=[pl.BlockSpec((tm, D_MODEL), row), _resident((1, D_MODEL)),
                  _resident(wt_attn.shape), _resident((1, LANES)), _resident((1, LANES))],
        out_specs=out_specs,
        out_shape=out_shape,
        scratch_shapes=[pltpu.VMEM((tm, D_MODEL), BF16)],
        compiler_params=_cparams(("arbitrary",)),
        name="proj_attn",
    )(x2d, g1, wt_attn, lng, lnb)


def _project_ret(x2d, g1, wt_ret, cs, tm):
    m = x2d.shape[0]
    n_pos_blocks = cs.shape[0] // tm
    row = lambda i: (i, 0)
    return pl.pallas_call(
        _proj_ret_body,
        grid=(m // tm,),
        in_specs=[pl.BlockSpec((tm, D_MODEL), row), _resident((1, D_MODEL)),
                  _resident(wt_ret.shape),
                  pl.BlockSpec((tm, 2 * LANES), lambda i: (i % n_pos_blocks, 0))],
        out_specs=pl.BlockSpec((tm, 4 * RET_WIDTH), row),
        out_shape=jax.ShapeDtypeStruct((m, 4 * RET_WIDTH), BF16),
        scratch_shapes=[pltpu.VMEM((tm, D_MODEL), BF16)],
        compiler_params=_cparams(("arbitrary",)),
        name="proj_ret",
    )(x2d, g1, wt_ret, cs)


def _key_to_float(key):
    bits = key ^ ((key >> 31) & 0x7FFFFFFF)
    return lax.bitcast_convert_type(bits, F32)


def _threshold_search(count_ge, n_iter, shape):
    def body(it, t):
        bit = lax.shift_left(jnp.int32(1), 31 - it)
        cand = t ^ bit
        cnt = count_ge(_key_to_float(cand))
        return jnp.where(cnt >= float(TOPK_MAX), cand, t)

    t = lax.fori_loop(0, n_iter, body, jnp.full(shape, INT_MIN, I32))
    return _key_to_float(jnp.maximum(t, KEY_NEG_INF))


def _attn_body(qa_ref, qi_ref, wit_ref, ka_ref, va_ref, kd_ref, tri_ref, o_ref,
               kbf, vtb, scr, sbuf, mrun, lrun, acc_s):
    qb = pl.program_id(1)
    n_heads_q = ATTN_HEADS * Q_BLOCK
    n_pairs = IDX_HEADS // 2

    @pl.when(qb == 0)
    def _cast():
        kbf[...] = ka_ref[...].astype(BF16)
        for kt in range(vtb.shape[0]):
            vtb[kt] = va_ref[kt * KEY_TILE:(kt + 1) * KEY_TILE, :].T.astype(BF16)

    nk = (qb + 2) // 2
    wt = wit_ref[...] * (IDX_HEAD_DIM ** -0.5)
    qi2 = qi_ref[...].reshape(n_pairs * Q_BLOCK, LANES)
    lo_half = lax.broadcasted_iota(I32, (KEY_TILE, LANES), 1) < IDX_HEAD_DIM
    qidx = qb * Q_BLOCK + lax.broadcasted_iota(I32, (KEY_TILE, Q_BLOCK), 1)
    kidx0 = lax.broadcasted_iota(I32, (KEY_TILE, Q_BLOCK), 0)
    contract_last = (((1,), (1,)), ((), ()))

    def idx_body(kt, carry):
        off = pl.multiple_of(kt * KEY_TILE, KEY_TILE)
        kit = kd_ref[pl.ds(off, KEY_TILE), :]
        zero = jnp.zeros_like(kit)
        s_even = lax.dot_general(jnp.where(lo_half, kit, zero), qi2, contract_last,
                                 preferred_element_type=F32)
        s_odd = lax.dot_general(jnp.where(lo_half, zero, kit), qi2, contract_last,
                                preferred_element_type=F32)
        score = jnp.zeros((KEY_TILE, Q_BLOCK), F32)
        for g in range(n_pairs):
            cs = slice(g * Q_BLOCK, (g + 1) * Q_BLOCK)
            score = score + jnp.maximum(s_even[:, cs], 0.0) * wt[2 * g:2 * g + 1, :]
            score = score + jnp.maximum(s_odd[:, cs], 0.0) * wt[2 * g + 1:2 * g + 2, :]
        scr[kt] = jnp.where(kidx0 + off <= qidx, score, -jnp.inf)
        return carry

    lax.fori_loop(0, nk, idx_body, 0)

    def count_cmp(cmp):
        def body(kt, acc):
            c = jnp.where(cmp(scr[kt]), 1.0, 0.0)
            return acc + jnp.sum(c.reshape(KEY_TILE // COUNT_ROWS, COUNT_ROWS, Q_BLOCK), axis=0)
        acc = lax.fori_loop(0, nk, body, jnp.zeros((COUNT_ROWS, Q_BLOCK), F32))
        return jnp.sum(acc, axis=0, keepdims=True)

    n_iter = jnp.where(qb >= TOPK_MAX // Q_BLOCK, 32, 0)
    tf = _threshold_search(lambda c: count_cmp(lambda sc: sc >= c), n_iter, (1, Q_BLOCK))
    need = float(TOPK_MAX) - count_cmp(lambda sc: sc > tf)
    excess = jnp.max(count_cmp(lambda sc: sc >= tf)) > float(TOPK_MAX)

    qa2 = qa_ref[...].reshape(n_heads_q, ATTN_HEAD_DIM)
    mrun[...] = jnp.full((8, n_heads_q), NEG_BIG, F32)

    def p1_body(kt, tie_off):
        off = pl.multiple_of(kt * KEY_TILE, KEY_TILE)
        sc = scr[kt]
        eq = sc == tf
        tie = jnp.where(eq, 1.0, 0.0)
        rank = lax.cond(
            excess,
            lambda: jnp.dot(tri_ref[...], tie.astype(BF16), preferred_element_type=F32) + tie_off,
            lambda: jnp.zeros((KEY_TILE, Q_BLOCK), F32))
        sel = ((sc > tf) | (eq & (rank <= need))) & (kidx0 + off <= qidx)
        s = lax.dot_general(kbf[pl.ds(off, KEY_TILE), :], qa2, contract_last,
                            preferred_element_type=F32)
        for h in range(ATTN_HEADS):
            cs = slice(h * Q_BLOCK, (h + 1) * Q_BLOCK)
            sh = jnp.where(sel, s[:, cs] * (ATTN_HEAD_DIM ** -0.5), NEG_BIG)
            sbuf[kt, :, cs] = sh
            mrun[:, cs] = jnp.maximum(
                mrun[:, cs], jnp.max(sh.reshape(KEY_TILE // 8, 8, Q_BLOCK), axis=0))
        return tie_off + jnp.sum(tie, axis=0, keepdims=True)

    lax.fori_loop(0, nk, p1_body, jnp.zeros((1, Q_BLOCK), F32))

    m = jnp.max(mrun[...], axis=0, keepdims=True)
    lrun[...] = jnp.zeros((8, n_heads_q), F32)
    acc_s[...] = jnp.zeros((ATTN_HEAD_DIM, n_heads_q), F32)

    def p2_body(kt, carry):
        p = jnp.exp(sbuf[kt] - m)
        lrun[...] += jnp.sum(p.reshape(KEY_TILE // 8, 8, n_heads_q), axis=0)
        acc_s[...] += jnp.dot(vtb[kt], p.astype(BF16), preferred_element_type=F32)
        return carry

    lax.fori_loop(0, nk, p2_body, 0)

    out = acc_s[...] / jnp.sum(lrun[...], axis=0, keepdims=True)
    for h in range(ATTN_HEADS):
        cs = slice(h * Q_BLOCK, (h + 1) * Q_BLOCK)
        o_ref[:, cs] = out[:, cs].T.astype(BF16)


def _prompt_attention(qa_hm, qi_pm, wi_t, ka, va, kd, tri, batch, seq):
    nq = seq // Q_BLOCK
    nkt = seq // KEY_TILE
    m = batch * seq
    n_heads_q = ATTN_HEADS * Q_BLOCK
    in_specs = [
        pl.BlockSpec((ATTN_HEADS, Q_BLOCK, ATTN_HEAD_DIM), lambda b, q: (0, b * nq + q, 0)),
        pl.BlockSpec((IDX_HEADS // 2, Q_BLOCK, LANES), lambda b, q: (0, b * nq + q, 0)),
        pl.BlockSpec((IDX_HEADS, Q_BLOCK), lambda b, q: (0, b * nq + q)),
        pl.BlockSpec((seq, ATTN_HEAD_DIM), lambda b, q: (b, 0)),
        pl.BlockSpec((seq, ATTN_HEAD_DIM), lambda b, q: (b, 0)),
        pl.BlockSpec((seq, LANES), lambda b, q: (b, 0)),
        pl.BlockSpec((KEY_TILE, KEY_TILE), lambda b, q: (0, 0)),
    ]
    return pl.pallas_call(
        _attn_body,
        grid=(batch, nq),
        in_specs=in_specs,
        out_specs=pl.BlockSpec((Q_BLOCK, ATTN_HEADS * ATTN_HEAD_DIM), lambda b, q: (b * nq + q, 0)),
        out_shape=jax.ShapeDtypeStruct((m, ATTN_HEADS * ATTN_HEAD_DIM), BF16),
        scratch_shapes=[
            pltpu.VMEM((seq, ATTN_HEAD_DIM), BF16),
            pltpu.VMEM((nkt, ATTN_HEAD_DIM, KEY_TILE), BF16),
            pltpu.VMEM((nkt, KEY_TILE, Q_BLOCK), F32),
            pltpu.VMEM((nkt, KEY_TILE, n_heads_q), F32),
            pltpu.VMEM((8, n_heads_q), F32),
            pltpu.VMEM((8, n_heads_q), F32),
            pltpu.VMEM((ATTN_HEAD_DIM, n_heads_q), F32),
        ],
        compiler_params=_cparams(("arbitrary", "arbitrary")),
        name="prompt_attn",
    )(qa_hm, qi_pm, wi_t, ka, va, kd, tri)


def _gate(o, g):
    rn = o * lax.rsqrt(jnp.mean(o * o, axis=-1, keepdims=True) + EPS)
    return rn * (g / (1.0 + jnp.exp(-g)))


def _ret_body(q_ref, k_ref, v_ref, g_ref, decay_ref, rsc_ref, zeta_ref, gpow_ref,
              rg_ref, st_ref):
    c = pl.program_id(1)

    @pl.when(c == 0)
    def _init():
        st_ref[...] = jnp.zeros(st_ref.shape, F32)

    for h in range(RET_HEADS):
        sl = slice(h * 128, (h + 1) * 128)
        q = q_ref[:, sl]
        k = k_ref[:, sl]
        v = v_ref[:, sl]
        r_old = st_ref[0, h]
        qk = lax.dot_general(q, k, (((1,), (1,)), ((), ())), preferred_element_type=F32)
        inner = jnp.dot((qk * decay_ref[h]).astype(BF16), v, preferred_element_type=F32)
        cross = jnp.dot(q, r_old.astype(BF16), preferred_element_type=F32) * rsc_ref[h]
        kz = (k.astype(F32) * zeta_ref[h]).astype(BF16)
        upd = lax.dot_general(kz, v, (((0,), (0,)), ((), ())), preferred_element_type=F32)
        st_ref[0, h] = r_old * gpow_ref[h] + upd
        rg_ref[:, sl] = _gate(inner + cross, g_ref[:, sl].astype(F32)).astype(BF16)


def _prompt_retention(main, decay, rsc, zeta, gpow, batch, seq):
    nc = seq // RET_CHUNK
    m = batch * seq
    width = RET_WIDTH
    const3 = lambda b, c: (0, 0, 0)
    in_specs = [
        pl.BlockSpec((RET_CHUNK, width), lambda b, c: (b * nc + c, 0)),
        pl.BlockSpec((RET_CHUNK, width), lambda b, c: (b * nc + c, 1)),
        pl.BlockSpec((RET_CHUNK, width), lambda b, c: (b * nc + c, 2)),
        pl.BlockSpec((RET_CHUNK, width), lambda b, c: (b * nc + c, 3)),
        pl.BlockSpec((RET_HEADS, RET_CHUNK, RET_CHUNK), const3),
        pl.BlockSpec((RET_HEADS, RET_CHUNK, RET_DV), const3),
        pl.BlockSpec((RET_HEADS, RET_CHUNK, RET_DK), const3),
        pl.BlockSpec((RET_HEADS, 1, RET_DV), const3),
    ]
    return pl.pallas_call(
        _ret_body,
        grid=(batch, nc),
        in_specs=in_specs,
        out_specs=(
            pl.BlockSpec((RET_CHUNK, width), lambda b, c: (b * nc + c, 0)),
            pl.BlockSpec((1, RET_HEADS, RET_DK, RET_DV), lambda b, c: (b, 0, 0, 0)),
        ),
        out_shape=(
            jax.ShapeDtypeStruct((m, width), BF16),
            jax.ShapeDtypeStruct((batch, RET_HEADS, RET_DK, RET_DV), F32),
        ),
        compiler_params=_cparams(("arbitrary", "arbitrary")),
        name="prompt_ret",
    )(main, main, main, main, decay, rsc, zeta, gpow)


def _outproj_body(a_ref, r_ref, wa_ref, wr_ref, x_ref, g2_ref, x1_ref, h2_ref):
    mixed = (jnp.dot(a_ref[...], wa_ref[...], preferred_element_type=F32)
             + jnp.dot(r_ref[...], wr_ref[...], preferred_element_type=F32))
    x1 = x_ref[...] + mixed
    x1_ref[...] = x1
    ms = jnp.mean(x1 * x1, axis=-1, keepdims=True)
    h2_ref[...] = (x1 * lax.rsqrt(ms + EPS) * g2_ref[...]).astype(BF16)


def _out_projection(attn_o, rg, wa, wr, x2d, g2, tm):
    m = x2d.shape[0]
    half = attn_o.shape[1]
    in_specs = [
        pl.BlockSpec((tm, half), lambda i: (i, 0)),
        pl.BlockSpec((tm, half), lambda i: (i, 0)),
        pl.BlockSpec((half, D_MODEL), lambda i: (0, 0)),
        pl.BlockSpec((half, D_MODEL), lambda i: (0, 0)),
        pl.BlockSpec((tm, D_MODEL), lambda i: (i, 0)),
        pl.BlockSpec((1, D_MODEL), lambda i: (0, 0)),
    ]
    return pl.pallas_call(
        _outproj_body,
        grid=(m // tm,),
        in_specs=in_specs,
        out_specs=(pl.BlockSpec((tm, D_MODEL), lambda i: (i, 0)),
                   pl.BlockSpec((tm, D_MODEL), lambda i: (i, 0))),
        out_shape=(jax.ShapeDtypeStruct((m, D_MODEL), F32),
                   jax.ShapeDtypeStruct((m, D_MODEL), BF16)),
        compiler_params=_cparams(("arbitrary",)),
        name="out_proj",
    )(attn_o, rg, wa, wr, x2d, g2)


def _mlp_body(h2_ref, wu_ref, wd_ref, x1_ref, gf_ref, y_ref, acc_ref):
    f = pl.program_id(1)

    @pl.when(f == 0)
    def _init():
        acc_ref[...] = x1_ref[...]

    u = jnp.dot(h2_ref[...], wu_ref[...], preferred_element_type=F32)
    a = jnp.maximum(u, 0.0)
    acc_ref[...] += jnp.dot((a * a).astype(BF16), wd_ref[...], preferred_element_type=F32)

    @pl.when(f == pl.num_programs(1) - 1)
    def _final():
        x2 = acc_ref[...]
        ms = jnp.mean(x2 * x2, axis=-1, keepdims=True)
        y_ref[...] = x2 * lax.rsqrt(ms + EPS) * gf_ref[...]


def _mlp(h2, wu, wd, x1, gf, tm, tf):
    m = h2.shape[0]
    in_specs = [
        pl.BlockSpec((tm, D_MODEL), lambda i, f: (i, 0)),
        pl.BlockSpec((D_MODEL, tf), lambda i, f: (0, f)),
        pl.BlockSpec((tf, D_MODEL), lambda i, f: (f, 0)),
        pl.BlockSpec((tm, D_MODEL), lambda i, f: (i, 0)),
        pl.BlockSpec((1, D_MODEL), lambda i, f: (0, 0)),
    ]
    return pl.pallas_call(
        _mlp_body,
        grid=(m // tm, D_FF // tf),
        in_specs=in_specs,
        out_specs=pl.BlockSpec((tm, D_MODEL), lambda i, f: (i, 0)),
        out_shape=jax.ShapeDtypeStruct((m, D_MODEL), F32),
        scratch_shapes=[pltpu.VMEM((tm, D_MODEL), F32)],
        compiler_params=_cparams(("arbitrary", "arbitrary")),
        name="mlp",
    )(h2, wu, wd, x1, gf)


def _sidx_body(pt_ref, qi_ref, w_ref, kin_ref, *rest):
    n_pages = len(rest) - 2
    pages = rest[:n_pages]
    out_ref = rest[n_pages]
    kt_s = rest[n_pages + 1]
    page = pages[0].shape[2]
    past = n_pages * page
    qi = qi_ref[0]
    w = w_ref[0] * (IDX_HEAD_DIM ** -0.5)
    for j in range(n_pages):
        kt_s[:, j * page:(j + 1) * page] = pages[j][0].astype(BF16)
    s = jnp.dot(qi, kt_s[...], preferred_element_type=F32)
    out_ref[0, :, 0:past] = jnp.sum(jnp.maximum(s, 0.0) * w, axis=0, keepdims=True)
    sn = jnp.sum(qi.astype(F32) * kin_ref[0].astype(BF16).astype(F32), axis=1, keepdims=True)
    rn = jnp.sum(jnp.maximum(sn, 0.0) * w, axis=0, keepdims=True)
    lane = lax.broadcasted_iota(I32, (1, LANES), 1)
    out_ref[0, :, past:past + LANES] = jnp.where(lane == 0, rn, -jnp.inf)


def _sample_index_scores(page_table, qi_s, wi_s, ki_s, cache_idx_k_t):
    nb, n_pages = page_table.shape
    page = cache_idx_k_t.shape[2]
    width = n_pages * page + LANES

    def page_spec(jj):
        return pl.BlockSpec((1, IDX_HEAD_DIM, page), lambda b, pt: (pt[b, jj], 0, 0))

    grid_spec = pltpu.PrefetchScalarGridSpec(
        num_scalar_prefetch=1,
        grid=(nb,),
        in_specs=[
            pl.BlockSpec((1, IDX_HEADS, IDX_HEAD_DIM), lambda b, pt: (b, 0, 0)),
            pl.BlockSpec((1, IDX_HEADS, 1), lambda b, pt: (b, 0, 0)),
            pl.BlockSpec((1, 1, IDX_HEAD_DIM), lambda b, pt: (b, 0, 0)),
        ] + [page_spec(jj) for jj in range(n_pages)],
        out_specs=pl.BlockSpec((1, 1, width), lambda b, pt: (b, 0, 0)),
        scratch_shapes=[pltpu.VMEM((IDX_HEAD_DIM, n_pages * page), BF16)],
    )
    return pl.pallas_call(
        _sidx_body,
        grid_spec=grid_spec,
        out_shape=jax.ShapeDtypeStruct((nb, 1, width), F32),
        compiler_params=_cparams(("arbitrary",)),
        name="sample_idx",
    )(page_table, qi_s, wi_s, ki_s, *([cache_idx_k_t] * n_pages))


def _ssel_body(sc_ref, tri_ref, sel_ref):
    rows, width = sc_ref.shape
    nt = width // LANES
    n_valid = (nt - 1) * LANES + 1

    def tile(kt):
        return sc_ref[:, kt * LANES:(kt + 1) * LANES]

    def count_cmp(cmp):
        acc = jnp.zeros((rows, LANES), F32)
        for kt in range(nt):
            acc = acc + jnp.where(cmp(tile(kt)), 1.0, 0.0)
        return jnp.broadcast_to(jnp.sum(acc, axis=1, keepdims=True), (rows, LANES))

    tf = _threshold_search(lambda c: count_cmp(lambda sc: sc >= c), 32, (rows, LANES))
    need = float(TOPK_MAX) - count_cmp(lambda sc: sc > tf)
    tie_off = jnp.zeros((rows, LANES), F32)
    for kt in range(nt):
        col = kt * LANES + lax.broadcasted_iota(I32, (rows, LANES), 1)
        sc = tile(kt)
        eq = sc == tf
        tie = jnp.where(eq, 1.0, 0.0)
        rank = jnp.dot(tie.astype(BF16), tri_ref[...], preferred_element_type=F32) + tie_off
        sel = ((sc > tf) | (eq & (rank <= need))) & (col < n_valid)
        sel_ref[:, kt * LANES:(kt + 1) * LANES] = jnp.where(sel, 1.0, 0.0)
        tie_off = tie_off + jnp.broadcast_to(jnp.sum(tie, axis=1, keepdims=True), (rows, LANES))


def _sample_select(scores2d, tri):
    rows, width = scores2d.shape
    return pl.pallas_call(
        _ssel_body,
        out_shape=jax.ShapeDtypeStruct((rows, width), F32),
        compiler_params=pltpu.CompilerParams(vmem_limit_bytes=VMEM_LIMIT),
        name="sample_select",
    )(scores2d, tri)


def _sattn_body(pt_ref, q_ref, sel_ref, kn_ref, vn_ref, *rest):
    n_pages = (len(rest) - 3) // 2
    kpages = rest[:n_pages]
    vpages = rest[n_pages:2 * n_pages]
    o_ref = rest[2 * n_pages]
    k_s = rest[2 * n_pages + 1]
    v_s = rest[2 * n_pages + 2]
    page = kpages[0].shape[1]
    past = n_pages * page
    q = q_ref[0]
    scale = ATTN_HEAD_DIM ** -0.5
    for j in range(n_pages):
        k_s[j * page:(j + 1) * page, :] = kpages[j][0].astype(BF16)
        v_s[j * page:(j + 1) * page, :] = vpages[j][0].astype(BF16)
    s = lax.dot_general(q, k_s[...], (((1,), (1,)), ((), ())), preferred_element_type=F32)
    s = jnp.where(sel_ref[0, :, 0:past] > 0.5, s * scale, NEG_BIG)
    kn = kn_ref[0].astype(BF16).astype(F32)
    sn = jnp.sum(q.astype(F32) * kn, axis=1, keepdims=True) * scale
    sn = jnp.where(sel_ref[0, :, past:past + 1] > 0.5, sn, NEG_BIG)
    m = jnp.maximum(jnp.max(s, axis=1, keepdims=True), sn)
    pn = jnp.exp(sn - m)
    p = jnp.exp(s - m)
    l = pn + jnp.sum(p, axis=1, keepdims=True)
    acc = (pn * vn_ref[0].astype(BF16).astype(F32)
           + jnp.dot(p.astype(BF16), v_s[...], preferred_element_type=F32))
    o_ref[0] = (acc / l).astype(BF16)


def _sample_attention(page_table, qa_s, sel3, ka_s, va_s, cache_k, cache_v):
    nb, n_pages = page_table.shape
    page = cache_k.shape[1]

    def page_spec(jj):
        return pl.BlockSpec((1, page, ATTN_HEAD_DIM), lambda b, pt: (pt[b, jj], 0, 0))

    grid_spec = pltpu.PrefetchScalarGridSpec(
        num_scalar_prefetch=1,
        grid=(nb,),
        in_specs=[
            pl.BlockSpec((1, ATTN_HEADS, ATTN_HEAD_DIM), lambda b, pt: (b, 0, 0)),
            pl.BlockSpec((1, 1, sel3.shape[2]), lambda b, pt: (b, 0, 0)),
            pl.BlockSpec((1, 1, ATTN_HEAD_DIM), lambda b, pt: (b, 0, 0)),
            pl.BlockSpec((1, 1, ATTN_HEAD_DIM), lambda b, pt: (b, 0, 0)),
        ] + [page_spec(jj) for jj in range(n_pages)] * 2,
        out_specs=pl.BlockSpec((1, ATTN_HEADS, ATTN_HEAD_DIM), lambda b, pt: (b, 0, 0)),
        scratch_shapes=[pltpu.VMEM((n_pages * page, ATTN_HEAD_DIM), BF16),
                        pltpu.VMEM((n_pages * page, ATTN_HEAD_DIM), BF16)],
    )
    return pl.pallas_call(
        _sattn_body,
        grid_spec=grid_spec,
        out_shape=jax.ShapeDtypeStruct((nb, ATTN_HEADS, ATTN_HEAD_DIM), BF16),
        compiler_params=_cparams(("arbitrary",)),
        name="sample_attn",
    )(page_table, qa_s, sel3, ka_s, va_s, *([cache_k] * n_pages), *([cache_v] * n_pages))


def _sret_body(qkvg_ref, st_ref, gam_ref, rg_ref, so_ref):
    ns = st_ref.shape[0]
    for s in range(ns):
        blk = qkvg_ref[s].astype(F32)
        q8 = blk[0:8]
        k8 = blk[8:16]
        v8 = blk[16:24]
        g8 = blk[24:32]
        q_t = q8.T
        k_t = k8.T
        qk = jnp.sum(q8 * k8, axis=1, keepdims=True)
        rows = []
        for h in range(RET_HEADS):
            r_old = st_ref[s, h]
            gam = gam_ref[h]
            qcol = jnp.broadcast_to(q_t[:, h:h + 1], (RET_DK, RET_DV))
            kcol = jnp.broadcast_to(k_t[:, h:h + 1], (RET_DK, RET_DV))
            vrow = v8[h:h + 1]
            q_r = jnp.sum(qcol * r_old, axis=0, keepdims=True)
            rows.append(gam * q_r + qk[h:h + 1] * vrow)
            so_ref[s, h] = gam * r_old + kcol * vrow
        ret = jnp.concatenate(rows, axis=0)
        rg_ref[s] = _gate(ret, g8).astype(BF16)


def _sample_retention(qkvg, state, gam, ns):
    nb = state.shape[0]
    return pl.pallas_call(
        _sret_body,
        grid=(nb // ns,),
        in_specs=[
            pl.BlockSpec((ns, 32, LANES), lambda i: (i, 0, 0)),
            pl.BlockSpec((ns, RET_HEADS, RET_DK, RET_DV), lambda i: (i, 0, 0, 0)),
            pl.BlockSpec((RET_HEADS, 1, LANES), lambda i: (0, 0, 0)),
        ],
        out_specs=(
            pl.BlockSpec((ns, RET_HEADS, RET_DV), lambda i: (i, 0, 0)),
            pl.BlockSpec((ns, RET_HEADS, RET_DK, RET_DV), lambda i: (i, 0, 0, 0)),
        ),
        out_shape=(
            jax.ShapeDtypeStruct((nb, RET_HEADS, RET_DV), BF16),
            jax.ShapeDtypeStruct(state.shape, F32),
        ),
        compiler_params=_cparams(("arbitrary",)),
        name="sample_ret",
    )(qkvg, state, gam)


def _rotary_table(pos):
    half = RET_DK // 2
    inv = ROPE_BASE ** (-jnp.arange(half, dtype=F32) / half)
    ang = pos[:, None] * inv[None, :]
    cos = jnp.cos(ang)
    sin = jnp.sin(ang)
    return jnp.concatenate([cos, cos, -sin, sin], axis=1)


def _retention_constants():
    lg = jnp.log1p(-jnp.exp2(-5.0 - jnp.arange(RET_HEADS, dtype=F32)))
    n = RET_CHUNK
    i = jnp.arange(n, dtype=F32)
    diff = i[:, None] - i[None, :]
    decay = jnp.where(diff[None] >= 0,
                      jnp.exp(jnp.maximum(diff, 0.0)[None] * lg[:, None, None]), 0.0)
    rsc = jnp.exp((i + 1.0)[None, :] * lg[:, None])
    zeta = jnp.exp((n - 1.0 - i)[None, :] * lg[:, None])
    gpow = jnp.exp(n * lg)
    gam1 = jnp.exp(lg)
    rsc_b = jnp.broadcast_to(rsc[:, :, None], (RET_HEADS, n, RET_DV))
    zeta_b = jnp.broadcast_to(zeta[:, :, None], (RET_HEADS, n, RET_DK))
    gpow_b = jnp.broadcast_to(gpow[:, None, None], (RET_HEADS, 1, RET_DV))
    gam1_b = jnp.broadcast_to(gam1[:, None, None], (RET_HEADS, 1, LANES))
    return decay, rsc_b, zeta_b, gpow_b, gam1_b


def _upper_tri(n):
    r = lax.broadcasted_iota(I32, (n, n), 0)
    c = lax.broadcasted_iota(I32, (n, n), 1)
    return (r <= c).astype(BF16)


def _pad_lanes(v):
    return jnp.pad(v, (0, LANES - v.shape[0])).reshape(1, LANES)


def kernel(x_prompt, x_sample, cache_k, cache_v, cache_idx_k, state_ret, page_table,
           norm1_g, w_in, idx_k_norm_g, idx_k_norm_b, w_out, norm2_g, w_up, w_down, final_norm_g):
    batch, seq, _ = x_prompt.shape
    nb = x_sample.shape[0]
    past_len = page_table.shape[1] * cache_k.shape[1]
    half_mix = ATTN_HEADS * ATTN_HEAD_DIM

    wt = w_in.T.astype(BF16)
    wt_attn = wt[:OFF_QR + LANES]
    wt_ret = wt[OFF_QR:]
    wa = w_out[:half_mix].astype(BF16)
    wr = w_out[half_mix:].astype(BF16)
    wu = w_up.astype(BF16)
    wd = w_down.astype(BF16)
    g1 = norm1_g.reshape(1, D_MODEL)
    g2 = norm2_g.reshape(1, D_MODEL)
    gf = final_norm_g.reshape(1, D_MODEL)
    lng = _pad_lanes(idx_k_norm_g)
    lnb = _pad_lanes(idx_k_norm_b)
    decay, rsc_b, zeta_b, gpow_b, gam1_b = _retention_constants()

    xp = x_prompt.reshape(batch * seq, D_MODEL)
    cs_p = _rotary_table(jnp.arange(seq, dtype=F32))
    qa_p, qi_p, ka_p, va_p, ki_p, kd_p, wi_p = _project_attn(xp, g1, wt_attn, lng, lnb, tm=1024)
    main_p = _project_ret(xp, g1, wt_ret, cs_p, tm=512)
    attn_p = _prompt_attention(qa_p, qi_p, wi_p.T, ka_p, va_p, kd_p, _upper_tri(KEY_TILE).T,
                               batch, seq)
    rg_p, ret_state_p = _prompt_retention(main_p, decay, rsc_b, zeta_b, gpow_b, batch, seq)
    x1_p, h2_p = _out_projection(attn_p, rg_p, wa, wr, xp, g2, tm=512)
    y_p = _mlp(h2_p, wu, wd, x1_p, gf, tm=512, tf=1024)

    xs = x_sample.reshape(nb, D_MODEL)
    cs_s = _rotary_table(jnp.full((nb,), past_len, dtype=F32))
    qa_s, qi_s, ka_s, va_s, ki_s, _, wi_s = _project_attn(xs, g1, wt_attn, lng, lnb, tm=nb)
    main_s = _project_ret(xs, g1, wt_ret, cs_s, tm=nb)
    scores = _sample_index_scores(
        page_table,
        qi_s.transpose(1, 0, 2).reshape(nb, IDX_HEADS, IDX_HEAD_DIM),
        wi_s.reshape(nb, IDX_HEADS, 1),
        ki_s.reshape(nb, 1, IDX_HEAD_DIM),
        jnp.swapaxes(cache_idx_k, 1, 2))
    width = scores.shape[2]
    sel = _sample_select(scores.reshape(nb, width), _upper_tri(LANES))
    attn_s = _sample_attention(
        page_table,
        qa_s.transpose(1, 0, 2),
        sel.reshape(nb, 1, width),
        ka_s.reshape(nb, 1, ATTN_HEAD_DIM),
        va_s.reshape(nb, 1, ATTN_HEAD_DIM),
        cache_k, cache_v)
    rg_s, ret_state_s = _sample_retention(main_s.reshape(nb, 32, LANES), state_ret, gam1_b, ns=4)
    x1_s, h2_s = _out_projection(attn_s.reshape(nb, half_mix), rg_s.reshape(nb, RET_WIDTH),
                                 wa, wr, xs, g2, tm=nb)
    y_s = _mlp(h2_s, wu, wd, x1_s, gf, tm=nb, tf=1024)

    return (
        y_p.reshape(batch, seq, D_MODEL),
        y_s.reshape(nb, 1, D_MODEL),
        ka_p.reshape(batch, seq, ATTN_HEAD_DIM),
        va_p.reshape(batch, seq, ATTN_HEAD_DIM),
        ki_p.reshape(batch, seq, IDX_HEAD_DIM),
        ret_state_p,
        ka_s.reshape(nb, 1, ATTN_HEAD_DIM),
        va_s.reshape(nb, 1, ATTN_HEAD_DIM),
        ki_s.reshape(nb, 1, IDX_HEAD_DIM),
        ret_state_s,
    )
```

```python
import functools

import numpy as np
import jax
import jax.numpy as jnp
from jax import lax
from jax.experimental import pallas as pl
from jax.experimental.pallas import tpu as pltpu

F32 = jnp.float32
BF16 = jnp.bfloat16
I32 = jnp.int32

D_MODEL = 2048
ATTN_HEADS = 8
ATTN_HEAD_DIM = 128
IDX_HEADS = 16
IDX_HEAD_DIM = 64
TOPK_MAX = 256
RET_HEADS = 8
RET_DK = 128
RET_DV = 128
RET_CHUNK = 128
ROPE_BASE = 10000.0
D_FF = 4 * D_MODEL
EPS = 1e-6
Q_BLOCK = 256

OFF_QA, OFF_KA, OFF_VA, OFF_QI, OFF_KI, OFF_WI = 0, 1024, 1152, 1280, 2304, 2368
OFF_QR, OFF_KR, OFF_VR, OFF_GR = 2384, 3408, 4432, 5456
RET_WIDTH = RET_HEADS * RET_DV

LANES = 128
PROJ_TILE = 512
KEY_TILE = 256
COUNT_ROWS = 64
INT_MIN = -2 ** 31
KEY_NEG_INF = -2 ** 31 + 0x7FFFFF
NEG_BIG = -1e30
VMEM_LIMIT = 56 * 1024 * 1024


def _cparams(sem):
    return pltpu.CompilerParams(dimension_semantics=sem, vmem_limit_bytes=VMEM_LIMIT)


def _resident(shape):
    zeros = (0,) * len(shape)
    return pl.BlockSpec(shape, lambda *_: zeros, pipeline_mode=pl.Buffered(1))


def _normed_input(x_ref, g_ref, xn_ref):
    x = x_ref[...]
    ms = jnp.mean(x * x, axis=-1, keepdims=True)
    xn_ref[...] = (x * lax.rsqrt(ms + EPS) * g_ref[...]).astype(BF16)


def _matmul_rows(xn_ref, wt_ref, r0, n):
    return lax.dot_general(xn_ref[...], wt_ref[r0:r0 + n, :], (((1,), (1,)), ((), ())),
                           preferred_element_type=F32)


def _proj_attn_body(x_ref, g_ref, wt_ref, lng_ref, lnb_ref,
                    qa_ref, qi_ref, ka_ref, va_ref, ki_ref, kd_ref, wi_ref, xn_ref):
    _normed_input(x_ref, g_ref, xn_ref)
    mm = functools.partial(_matmul_rows, xn_ref, wt_ref)
    for t in range(ATTN_HEADS * ATTN_HEAD_DIM // PROJ_TILE):
        acc = mm(OFF_QA + t * PROJ_TILE, PROJ_TILE)
        for hh in range(4):
            qa_ref[4 * t + hh] = acc[:, hh * LANES:(hh + 1) * LANES].astype(BF16)
    for t in range(IDX_HEADS * IDX_HEAD_DIM // PROJ_TILE):
        acc = mm(OFF_QI + t * PROJ_TILE, PROJ_TILE)
        for hh in range(4):
            qi_ref[4 * t + hh] = acc[:, hh * LANES:(hh + 1) * LANES].astype(BF16)
    kv = mm(OFF_KA, 2 * ATTN_HEAD_DIM)
    ka_ref[...] = kv[:, :ATTN_HEAD_DIM]
    va_ref[...] = kv[:, ATTN_HEAD_DIM:]
    kw = mm(OFF_KI, LANES)
    lane = lax.broadcasted_iota(I32, kw.shape, 1)
    is_k = lane < IDX_HEAD_DIM
    mu = jnp.sum(jnp.where(is_k, kw, 0.0), axis=-1, keepdims=True) * (1.0 / IDX_HEAD_DIM)
    d = jnp.where(is_k, kw - mu, 0.0)
    var = jnp.sum(d * d, axis=-1, keepdims=True) * (1.0 / IDX_HEAD_DIM)
    kn = d * lax.rsqrt(var + EPS) * lng_ref[...] + lnb_ref[...]
    ki_ref[...] = kn[:, :IDX_HEAD_DIM]
    kd_ref[...] = jnp.where(is_k, kn, pltpu.roll(kn, IDX_HEAD_DIM, 1)).astype(BF16)
    wi_ref[...] = kw[:, IDX_HEAD_DIM:IDX_HEAD_DIM + IDX_HEADS] * (IDX_HEADS ** -0.5)


def _proj_ret_body(x_ref, g_ref, wt_ref, cs_ref, main_ref, xn_ref):
    _normed_input(x_ref, g_ref, xn_ref)
    base = OFF_QR
    cosf = cs_ref[:, :LANES]
    sinf = cs_ref[:, LANES:]
    tiles = RET_WIDTH // PROJ_TILE
    for seg, (off, scale) in enumerate(((OFF_QR, None), (OFF_KR, RET_DK ** -0.5))):
        for t in range(tiles):
            acc = _matmul_rows(xn_ref, wt_ref, off - base + t * PROJ_TILE, PROJ_TILE)
            for hh in range(PROJ_TILE // LANES):
                xh = acc[:, hh * LANES:(hh + 1) * LANES]
                r = xh * cosf + pltpu.roll(xh, RET_DK // 2, 1) * sinf
                if scale is not None:
                    r = r * scale
                c0 = seg * RET_WIDTH + t * PROJ_TILE + hh * LANES
                main_ref[:, c0:c0 + LANES] = r.astype(BF16)
    for seg, off in ((2, OFF_VR), (3, OFF_GR)):
        for t in range(tiles):
            acc = _matmul_rows(xn_ref, wt_ref, off - base + t * PROJ_TILE, PROJ_TILE)
            c0 = seg * RET_WIDTH + t * PROJ_TILE
            main_ref[:, c0:c0 + PROJ_TILE] = acc.astype(BF16)


def _project_attn(x2d, g1, wt_attn, lng, lnb, tm):
    m = x2d.shape[0]
    row = lambda i: (i, 0)
    out_shape = (
        jax.ShapeDtypeStruct((ATTN_HEADS, m, ATTN_HEAD_DIM), BF16),
        jax.ShapeDtypeStruct((IDX_HEADS // 2, m, LANES), BF16),
        jax.ShapeDtypeStruct((m, ATTN_HEAD_DIM), F32),
        jax.ShapeDtypeStruct((m, ATTN_HEAD_DIM), F32),
        jax.ShapeDtypeStruct((m, IDX_HEAD_DIM), F32),
        jax.ShapeDtypeStruct((m, LANES), BF16),
        jax.ShapeDtypeStruct((m, IDX_HEADS), F32),
    )
    out_specs = (
        pl.BlockSpec((ATTN_HEADS, tm, ATTN_HEAD_DIM), lambda i: (0, i, 0)),
        pl.BlockSpec((IDX_HEADS // 2, tm, LANES), lambda i: (0, i, 0)),
        pl.BlockSpec((tm, ATTN_HEAD_DIM), row),
        pl.BlockSpec((tm, ATTN_HEAD_DIM), row),
        pl.BlockSpec((tm, IDX_HEAD_DIM), row),
        pl.BlockSpec((tm, LANES), row),
        pl.BlockSpec((tm, IDX_HEADS), row),
    )
    return pl.pallas_call(
        _proj_attn_body,
        grid=(m // tm,),
        in_specs=[pl.BlockSpec((tm, D_MODEL), row), _resident((1, D_MODEL)),
                  _resident(wt_attn.shape), _resident((1, LANES)), _resident((1, LANES))],
        out_specs=out_specs,
        out_shape=out_shape,
        scratch_shapes=[pltpu.VMEM((tm, D_MODEL), BF16)],
        compiler_params=_cparams(("arbitrary",)),
        name="proj_attn",
    )(x2d, g1, wt_attn, lng, lnb)


def _project_ret(x2d, g1, wt_ret, cs, tm):
    m = x2d.shape[0]
    n_pos_blocks = cs.shape[0] // tm
    row = lambda i: (i, 0)
    return pl.pallas_call(
        _proj_ret_body,
        grid=(m // tm,),
        in_specs=[pl.BlockSpec((tm, D_MODEL), row), _resident((1, D_MODEL)),
                  _resident(wt_ret.shape),
                  pl.BlockSpec((tm, 2 * LANES), lambda i: (i % n_pos_blocks, 0))],
        out_specs=pl.BlockSpec((tm, 4 * RET_WIDTH), row),
        out_shape=jax.ShapeDtypeStruct((m, 4 * RET_WIDTH), BF16),
        scratch_shapes=[pltpu.VMEM((tm, D_MODEL), BF16)],
        compiler_params=_cparams(("arbitrary",)),
        name="proj_ret",
    )(x2d, g1, wt_ret, cs)


def _key_to_float(key):
    bits = key ^ ((key >> 31) & 0x7FFFFFFF)
    return lax.bitcast_convert_type(bits, F32)


def _threshold_search(count_ge, n_iter, shape):
    def body(it, t):
        bit = lax.shift_left(jnp.int32(1), 31 - it)
        cand = t ^ bit
        cnt = count_ge(_key_to_float(cand))
        return jnp.where(cnt >= float(TOPK_MAX), cand, t)

    t = lax.fori_loop(0, n_iter, body, jnp.full(shape, INT_MIN, I32))
    return _key_to_float(jnp.maximum(t, KEY_NEG_INF))


def _attn_body(qa_ref, qi_ref, wit_ref, ka_ref, va_ref, kd_ref, tri_ref, o_ref,
               kbf, vtb, scr, sbuf, mrun, lrun, acc_s):
    qb = pl.program_id(1)
    n_heads_q = ATTN_HEADS * Q_BLOCK
    n_pairs = IDX_HEADS // 2

    @pl.when(qb == 0)
    def _cast():
        kbf[...] = ka_ref[...].astype(BF16)
        for kt in range(vtb.shape[0]):
            vtb[kt] = va_ref[kt * KEY_TILE:(kt + 1) * KEY_TILE, :].T.astype(BF16)

    nk = ((qb + 1) * Q_BLOCK + KEY_TILE - 1) // KEY_TILE
    wt = wit_ref[...] * (IDX_HEAD_DIM ** -0.5)
    qi2 = qi_ref[...].reshape(n_pairs * Q_BLOCK, LANES)
    lo_half = lax.broadcasted_iota(I32, (KEY_TILE, LANES), 1) < IDX_HEAD_DIM
    qidx = qb * Q_BLOCK + lax.broadcasted_iota(I32, (KEY_TILE, Q_BLOCK), 1)
    kidx0 = lax.broadcasted_iota(I32, (KEY_TILE, Q_BLOCK), 0)
    contract_last = (((1,), (1,)), ((), ()))

    def idx_body(kt, carry):
        off = pl.multiple_of(kt * KEY_TILE, KEY_TILE)
        kit = kd_ref[pl.ds(off, KEY_TILE), :]
        zero = jnp.zeros_like(kit)
        s_even = lax.dot_general(jnp.where(lo_half, kit, zero), qi2, contract_last,
                                 preferred_element_type=F32)
        s_odd = lax.dot_general(jnp.where(lo_half, zero, kit), qi2, contract_last,
                                preferred_element_type=F32)
        score = jnp.zeros((KEY_TILE, Q_BLOCK), F32)
        for g in range(n_pairs):
            cs = slice(g * Q_BLOCK, (g + 1) * Q_BLOCK)
            score = score + jnp.maximum(s_even[:, cs], 0.0) * wt[2 * g:2 * g + 1, :]
            score = score + jnp.maximum(s_odd[:, cs], 0.0) * wt[2 * g + 1:2 * g + 2, :]
        scr[kt] = jnp.where(kidx0 + off <= qidx, score, -jnp.inf)
        s = lax.dot_general(kbf[pl.ds(off, KEY_TILE), :], qa2, contract_last,
                            preferred_element_type=F32)
        sbuf[kt] = s * (ATTN_HEAD_DIM ** -0.5)
        return carry

    qa2 = qa_ref[...].reshape(n_heads_q, ATTN_HEAD_DIM)
    lax.fori_loop(0, nk, idx_body, 0)

    def count_cmp(cmp):
        def body(kt, acc):
            c = jnp.where(cmp(scr[kt]), 1.0, 0.0)
            return acc + jnp.sum(c.reshape(KEY_TILE // COUNT_ROWS, COUNT_ROWS, Q_BLOCK), axis=0)
        acc = lax.fori_loop(0, nk, body, jnp.zeros((COUNT_ROWS, Q_BLOCK), F32))
        return jnp.sum(acc, axis=0, keepdims=True)

    n_iter = jnp.where(qb >= TOPK_MAX // Q_BLOCK, 32, 0)
    tf = _threshold_search(lambda c: count_cmp(lambda sc: sc >= c), n_iter, (1, Q_BLOCK))
    need = float(TOPK_MAX) - count_cmp(lambda sc: sc > tf)
    excess = jnp.max(count_cmp(lambda sc: sc >= tf)) > float(TOPK_MAX)

    mrun[...] = jnp.full((8, n_heads_q), NEG_BIG, F32)

    def mask_tile(kt, sel):
        for h in range(ATTN_HEADS):
            cs = slice(h * Q_BLOCK, (h + 1) * Q_BLOCK)
            sh = jnp.where(sel, sbuf[kt, :, cs], NEG_BIG)
            sbuf[kt, :, cs] = sh
            mrun[:, cs] = jnp.maximum(
                mrun[:, cs], jnp.max(sh.reshape(KEY_TILE // 8, 8, Q_BLOCK), axis=0))

    def p1_plain(kt, carry):
        off = pl.multiple_of(kt * KEY_TILE, KEY_TILE)
        mask_tile(kt, (scr[kt] >= tf) & (kidx0 + off <= qidx))
        return carry

    def p1_ties(kt, tie_off):
        off = pl.multiple_of(kt * KEY_TILE, KEY_TILE)
        sc = scr[kt]
        eq = sc == tf
        tie = jnp.where(eq, 1.0, 0.0)
        rank = jnp.dot(tri_ref[...], tie.astype(BF16), preferred_element_type=F32) + tie_off
        mask_tile(kt, ((sc > tf) | (eq & (rank <= need))) & (kidx0 + off <= qidx))
        return tie_off + jnp.sum(tie, axis=0, keepdims=True)

    @pl.when(excess)
    def _with_ties():
        lax.fori_loop(0, nk, p1_ties, jnp.zeros((1, Q_BLOCK), F32))

    @pl.when(jnp.logical_not(excess))
    def _without_ties():
        lax.fori_loop(0, nk, p1_plain, 0)

    m = jnp.max(mrun[...], axis=0, keepdims=True)
    lrun[...] = jnp.zeros((8, n_heads_q), F32)
    acc_s[...] = jnp.zeros((ATTN_HEAD_DIM, n_heads_q), F32)

    def p2_body(kt, carry):
        p = jnp.exp(sbuf[kt] - m)
        lrun[...] += jnp.sum(p.reshape(KEY_TILE // 8, 8, n_heads_q), axis=0)
        acc_s[...] += jnp.dot(vtb[kt], p.astype(BF16), preferred_element_type=F32)
        return carry

    lax.fori_loop(0, nk, p2_body, 0)

    out = acc_s[...] / jnp.sum(lrun[...], axis=0, keepdims=True)
    for h in range(ATTN_HEADS):
        oh = out[:, h * Q_BLOCK:(h + 1) * Q_BLOCK].T
        o_ref[:, h * ATTN_HEAD_DIM:(h + 1) * ATTN_HEAD_DIM] = oh.astype(BF16)


def _prompt_attention(qa_hm, qi_pm, wi_t, ka, va, kd, tri, batch, seq):
    nq = seq // Q_BLOCK
    nkt = seq // KEY_TILE
    m = batch * seq
    n_heads_q = ATTN_HEADS * Q_BLOCK
    in_specs = [
        pl.BlockSpec((ATTN_HEADS, Q_BLOCK, ATTN_HEAD_DIM), lambda b, q: (0, b * nq + q, 0)),
        pl.BlockSpec((IDX_HEADS // 2, Q_BLOCK, LANES), lambda b, q: (0, b * nq + q, 0)),
        pl.BlockSpec((IDX_HEADS, Q_BLOCK), lambda b, q: (0, b * nq + q)),
        pl.BlockSpec((seq, ATTN_HEAD_DIM), lambda b, q: (b, 0)),
        pl.BlockSpec((seq, ATTN_HEAD_DIM), lambda b, q: (b, 0)),
        pl.BlockSpec((seq, LANES), lambda b, q: (b, 0)),
        pl.BlockSpec((KEY_TILE, KEY_TILE), lambda b, q: (0, 0)),
    ]
    return pl.pallas_call(
        _attn_body,
        grid=(batch, nq),
        in_specs=in_specs,
        out_specs=pl.BlockSpec((Q_BLOCK, ATTN_HEADS * ATTN_HEAD_DIM), lambda b, q: (b * nq + q, 0)),
        out_shape=jax.ShapeDtypeStruct((m, ATTN_HEADS * ATTN_HEAD_DIM), BF16),
        scratch_shapes=[
            pltpu.VMEM((seq, ATTN_HEAD_DIM), BF16),
            pltpu.VMEM((nkt, ATTN_HEAD_DIM, KEY_TILE), BF16),
            pltpu.VMEM((nkt, KEY_TILE, Q_BLOCK), F32),
            pltpu.VMEM((nkt, KEY_TILE, n_heads_q), F32),
            pltpu.VMEM((8, n_heads_q), F32),
            pltpu.VMEM((8, n_heads_q), F32),
            pltpu.VMEM((ATTN_HEAD_DIM, n_heads_q), F32),
        ],
        compiler_params=_cparams(("arbitrary", "arbitrary")),
        name="prompt_attn",
    )(qa_hm, qi_pm, wi_t, ka, va, kd, tri)


def _gate(o, g):
    rn = o * lax.rsqrt(jnp.mean(o * o, axis=-1, keepdims=True) + EPS)
    return rn * (g / (1.0 + jnp.exp(-g)))


def _ret_body(q_ref, k_ref, v_ref, g_ref, decay_ref, rsc_ref, zeta_ref, gpow_ref,
              rg_ref, st_ref):
    c = pl.program_id(1)

    @pl.when(c == 0)
    def _init():
        st_ref[...] = jnp.zeros(st_ref.shape, F32)

    for h in range(RET_HEADS):
        sl = slice(h * 128, (h + 1) * 128)
        q = q_ref[:, sl]
        k = k_ref[:, sl]
        v = v_ref[:, sl]
        r_old = st_ref[0, h]
        qk = lax.dot_general(q, k, (((1,), (1,)), ((), ())), preferred_element_type=F32)
        inner = jnp.dot((qk * decay_ref[h]).astype(BF16), v, preferred_element_type=F32)
        cross = jnp.dot(q, r_old.astype(BF16), preferred_element_type=F32) * rsc_ref[h]
        kz = (k.astype(F32) * zeta_ref[h]).astype(BF16)
        upd = lax.dot_general(kz, v, (((0,), (0,)), ((), ())), preferred_element_type=F32)
        st_ref[0, h] = r_old * gpow_ref[h] + upd
        rg_ref[:, sl] = _gate(inner + cross, g_ref[:, sl].astype(F32)).astype(BF16)


def _prompt_retention(main, decay, rsc, zeta, gpow, batch, seq):
    nc = seq // RET_CHUNK
    m = batch * seq
    width = RET_WIDTH
    const3 = lambda b, c: (0, 0, 0)
    in_specs = [
        pl.BlockSpec((RET_CHUNK, width), lambda b, c: (b * nc + c, 0)),
        pl.BlockSpec((RET_CHUNK, width), lambda b, c: (b * nc + c, 1)),
        pl.BlockSpec((RET_CHUNK, width), lambda b, c: (b * nc + c, 2)),
        pl.BlockSpec((RET_CHUNK, width), lambda b, c: (b * nc + c, 3)),
        pl.BlockSpec((RET_HEADS, RET_CHUNK, RET_CHUNK), const3),
        pl.BlockSpec((RET_HEADS, RET_CHUNK, RET_DV), const3),
        pl.BlockSpec((RET_HEADS, RET_CHUNK, RET_DK), const3),
        pl.BlockSpec((RET_HEADS, 1, RET_DV), const3),
    ]
    return pl.pallas_call(
        _ret_body,
        grid=(batch, nc),
        in_specs=in_specs,
        out_specs=(
            pl.BlockSpec((RET_CHUNK, width), lambda b, c: (b * nc + c, 0)),
            pl.BlockSpec((1, RET_HEADS, RET_DK, RET_DV), lambda b, c: (b, 0, 0, 0)),
        ),
        out_shape=(
            jax.ShapeDtypeStruct((m, width), BF16),
            jax.ShapeDtypeStruct((batch, RET_HEADS, RET_DK, RET_DV), F32),
        ),
        compiler_params=_cparams(("arbitrary", "arbitrary")),
        name="prompt_ret",
    )(main, main, main, main, decay, rsc, zeta, gpow)


def _outproj_body(a_ref, r_ref, wa_ref, wr_ref, x_ref, g2_ref, x1_ref, h2_ref):
    mixed = (jnp.dot(a_ref[...], wa_ref[...], preferred_element_type=F32)
             + jnp.dot(r_ref[...], wr_ref[...], preferred_element_type=F32))
    x1 = x_ref[...] + mixed
    x1_ref[...] = x1
    ms = jnp.mean(x1 * x1, axis=-1, keepdims=True)
    h2_ref[...] = (x1 * lax.rsqrt(ms + EPS) * g2_ref[...]).astype(BF16)


def _out_projection(attn_o, rg, wa, wr, x2d, g2, tm):
    m = x2d.shape[0]
    half = attn_o.shape[1]
    in_specs = [
        pl.BlockSpec((tm, half), lambda i: (i, 0)),
        pl.BlockSpec((tm, half), lambda i: (i, 0)),
        pl.BlockSpec((half, D_MODEL), lambda i: (0, 0)),
        pl.BlockSpec((half, D_MODEL), lambda i: (0, 0)),
        pl.BlockSpec((tm, D_MODEL), lambda i: (i, 0)),
        pl.BlockSpec((1, D_MODEL), lambda i: (0, 0)),
    ]
    return pl.pallas_call(
        _outproj_body,
        grid=(m // tm,),
        in_specs=in_specs,
        out_specs=(pl.BlockSpec((tm, D_MODEL), lambda i: (i, 0)),
                   pl.BlockSpec((tm, D_MODEL), lambda i: (i, 0))),
        out_shape=(jax.ShapeDtypeStruct((m, D_MODEL), F32),
                   jax.ShapeDtypeStruct((m, D_MODEL), BF16)),
        compiler_params=_cparams(("arbitrary",)),
        name="out_proj",
    )(attn_o, rg, wa, wr, x2d, g2)


def _mlp_body(h2_ref, wu_ref, wd_ref, x1_ref, gf_ref, y_ref, acc_ref):
    f = pl.program_id(1)

    @pl.when(f == 0)
    def _init():
        acc_ref[...] = x1_ref[...]

    u = jnp.dot(h2_ref[...], wu_ref[...], preferred_element_type=F32)
    a = jnp.maximum(u, 0.0)
    acc_ref[...] += jnp.dot((a * a).astype(BF16), wd_ref[...], preferred_element_type=F32)

    @pl.when(f == pl.num_programs(1) - 1)
    def _final():
        x2 = acc_ref[...]
        ms = jnp.mean(x2 * x2, axis=-1, keepdims=True)
        y_ref[...] = x2 * lax.rsqrt(ms + EPS) * gf_ref[...]


def _mlp(h2, wu, wd, x1, gf, tm, tf):
    m = h2.shape[0]
    in_specs = [
        pl.BlockSpec((tm, D_MODEL), lambda i, f: (i, 0)),
        pl.BlockSpec((D_MODEL, tf), lambda i, f: (0, f)),
        pl.BlockSpec((tf, D_MODEL), lambda i, f: (f, 0)),
        pl.BlockSpec((tm, D_MODEL), lambda i, f: (i, 0)),
        pl.BlockSpec((1, D_MODEL), lambda i, f: (0, 0)),
    ]
    return pl.pallas_call(
        _mlp_body,
        grid=(m // tm, D_FF // tf),
        in_specs=in_specs,
        out_specs=pl.BlockSpec((tm, D_MODEL), lambda i, f: (i, 0)),
        out_shape=jax.ShapeDtypeStruct((m, D_MODEL), F32),
        scratch_shapes=[pltpu.VMEM((tm, D_MODEL), F32)],
        compiler_params=_cparams(("arbitrary", "arbitrary")),
        name="mlp",
    )(h2, wu, wd, x1, gf)


def _fetch_pages(pt_ref, step, slot, streams, start):
    n_pages = pt_ref.shape[1]
    for hbm, buf, sem in streams:
        for j in range(n_pages):
            cp = pltpu.make_async_copy(hbm.at[pt_ref[step, j]], buf.at[slot, j], sem.at[slot])
            if start:
                cp.start()
            else:
                cp.wait()


def _paged_prefetch(pt_ref, streams):
    b = pl.program_id(0)
    slot = b % 2

    @pl.when(b == 0)
    def _first():
        _fetch_pages(pt_ref, 0, 0, streams, start=True)

    @pl.when(b + 1 < pl.num_programs(0))
    def _next():
        _fetch_pages(pt_ref, b + 1, 1 - slot, streams, start=True)

    _fetch_pages(pt_ref, b, slot, streams, start=False)
    return slot


def _sidx_body(pt_ref, qi_ref, w_ref, kin_ref, cache_hbm, out_ref, kt_s, pbuf, sem):
    n_pages = pt_ref.shape[1]
    page = pbuf.shape[3]
    past = n_pages * page
    slot = _paged_prefetch(pt_ref, ((cache_hbm, pbuf, sem),))
    qi = qi_ref[0]
    w = w_ref[0] * (IDX_HEAD_DIM ** -0.5)
    for j in range(n_pages):
        kt_s[:, j * page:(j + 1) * page] = pbuf[slot, j].astype(BF16)
    s = jnp.dot(qi, kt_s[...], preferred_element_type=F32)
    out_ref[0, :, 0:past] = jnp.sum(jnp.maximum(s, 0.0) * w, axis=0, keepdims=True)
    sn = jnp.sum(qi.astype(F32) * kin_ref[0].astype(BF16).astype(F32), axis=1, keepdims=True)
    rn = jnp.sum(jnp.maximum(sn, 0.0) * w, axis=0, keepdims=True)
    lane = lax.broadcasted_iota(I32, (1, LANES), 1)
    out_ref[0, :, past:past + LANES] = jnp.where(lane == 0, rn, -jnp.inf)


def _sample_index_scores(page_table, qi_s, wi_s, ki_s, cache_idx_k_t):
    nb, n_pages = page_table.shape
    page = cache_idx_k_t.shape[2]
    width = n_pages * page + LANES

    grid_spec = pltpu.PrefetchScalarGridSpec(
        num_scalar_prefetch=1,
        grid=(nb,),
        in_specs=[
            pl.BlockSpec((1, IDX_HEADS, IDX_HEAD_DIM), lambda b, pt: (b, 0, 0)),
            pl.BlockSpec((1, IDX_HEADS, 1), lambda b, pt: (b, 0, 0)),
            pl.BlockSpec((1, 1, IDX_HEAD_DIM), lambda b, pt: (b, 0, 0)),
            pl.BlockSpec(memory_space=pl.ANY),
        ],
        out_specs=pl.BlockSpec((1, 1, width), lambda b, pt: (b, 0, 0)),
        scratch_shapes=[pltpu.VMEM((IDX_HEAD_DIM, n_pages * page), BF16),
                        pltpu.VMEM((2, n_pages, IDX_HEAD_DIM, page), F32),
                        pltpu.SemaphoreType.DMA((2,))],
    )
    return pl.pallas_call(
        _sidx_body,
        grid_spec=grid_spec,
        out_shape=jax.ShapeDtypeStruct((nb, 1, width), F32),
        compiler_params=_cparams(("arbitrary",)),
        name="sample_idx",
    )(page_table, qi_s, wi_s, ki_s, cache_idx_k_t)


def _ssel_body(sc_ref, tri_ref, sel_ref):
    rows, width = sc_ref.shape
    nt = width // LANES
    n_valid = (nt - 1) * LANES + 1

    def tile(kt):
        return sc_ref[:, kt * LANES:(kt + 1) * LANES]

    def count_cmp(cmp):
        acc = jnp.zeros((rows, LANES), F32)
        for kt in range(nt):
            acc = acc + jnp.where(cmp(tile(kt)), 1.0, 0.0)
        return jnp.broadcast_to(jnp.sum(acc, axis=1, keepdims=True), (rows, LANES))

    tf = _threshold_search(lambda c: count_cmp(lambda sc: sc >= c), 32, (rows, LANES))
    need = float(TOPK_MAX) - count_cmp(lambda sc: sc > tf)
    tie_off = jnp.zeros((rows, LANES), F32)
    for kt in range(nt):
        col = kt * LANES + lax.broadcasted_iota(I32, (rows, LANES), 1)
        sc = tile(kt)
        eq = sc == tf
        tie = jnp.where(eq, 1.0, 0.0)
        rank = jnp.dot(tie.astype(BF16), tri_ref[...], preferred_element_type=F32) + tie_off
        sel = ((sc > tf) | (eq & (rank <= need))) & (col < n_valid)
        sel_ref[:, kt * LANES:(kt + 1) * LANES] = jnp.where(sel, 1.0, 0.0)
        tie_off = tie_off + jnp.broadcast_to(jnp.sum(tie, axis=1, keepdims=True), (rows, LANES))


def _sample_select(scores2d, tri):
    rows, width = scores2d.shape
    return pl.pallas_call(
        _ssel_body,
        out_shape=jax.ShapeDtypeStruct((rows, width), F32),
        compiler_params=pltpu.CompilerParams(vmem_limit_bytes=VMEM_LIMIT),
        name="sample_select",
    )(scores2d, tri)


def _sattn_body(pt_ref, q_ref, sel_ref, kn_ref, vn_ref, ck_hbm, cv_hbm, o_ref,
                kbuf, vbuf, ksem, vsem):
    n_pages = pt_ref.shape[1]
    page = kbuf.shape[2]
    past = n_pages * page
    slot = _paged_prefetch(pt_ref, ((ck_hbm, kbuf, ksem), (cv_hbm, vbuf, vsem)))
    q = q_ref[0]
    scale = ATTN_HEAD_DIM ** -0.5
    k_all = kbuf[slot].reshape(past, ATTN_HEAD_DIM).astype(BF16)
    v_all = vbuf[slot].reshape(past, ATTN_HEAD_DIM).astype(BF16)
    s = lax.dot_general(q, k_all, (((1,), (1,)), ((), ())), preferred_element_type=F32)
    s = jnp.where(sel_ref[0, :, 0:past] > 0.5, s * scale, NEG_BIG)
    kn = kn_ref[0].astype(BF16).astype(F32)
    sn = jnp.sum(q.astype(F32) * kn, axis=1, keepdims=True) * scale
    sn = jnp.where(sel_ref[0, :, past:past + 1] > 0.5, sn, NEG_BIG)
    m = jnp.maximum(jnp.max(s, axis=1, keepdims=True), sn)
    pn = jnp.exp(sn - m)
    p = jnp.exp(s - m)
    l = pn + jnp.sum(p, axis=1, keepdims=True)
    acc = (pn * vn_ref[0].astype(BF16).astype(F32)
           + jnp.dot(p.astype(BF16), v_all, preferred_element_type=F32))
    o_ref[0] = (acc / l).astype(BF16)


def _sample_attention(page_table, qa_s, sel3, ka_s, va_s, cache_k, cache_v):
    nb, n_pages = page_table.shape
    page = cache_k.shape[1]

    grid_spec = pltpu.PrefetchScalarGridSpec(
        num_scalar_prefetch=1,
        grid=(nb,),
        in_specs=[
            pl.BlockSpec((1, ATTN_HEADS, ATTN_HEAD_DIM), lambda b, pt: (b, 0, 0)),
            pl.BlockSpec((1, 1, sel3.shape[2]), lambda b, pt: (b, 0, 0)),
            pl.BlockSpec((1, 1, ATTN_HEAD_DIM), lambda b, pt: (b, 0, 0)),
            pl.BlockSpec((1, 1, ATTN_HEAD_DIM), lambda b, pt: (b, 0, 0)),
            pl.BlockSpec(memory_space=pl.ANY),
            pl.BlockSpec(memory_space=pl.ANY),
        ],
        out_specs=pl.BlockSpec((1, ATTN_HEADS, ATTN_HEAD_DIM), lambda b, pt: (b, 0, 0)),
        scratch_shapes=[pltpu.VMEM((2, n_pages, page, ATTN_HEAD_DIM), F32),
                        pltpu.VMEM((2, n_pages, page, ATTN_HEAD_DIM), F32),
                        pltpu.SemaphoreType.DMA((2,)),
                        pltpu.SemaphoreType.DMA((2,))],
    )
    return pl.pallas_call(
        _sattn_body,
        grid_spec=grid_spec,
        out_shape=jax.ShapeDtypeStruct((nb, ATTN_HEADS, ATTN_HEAD_DIM), BF16),
        compiler_params=_cparams(("arbitrary",)),
        name="sample_attn",
    )(page_table, qa_s, sel3, ka_s, va_s, cache_k, cache_v)


def _sret_body(qkvg_ref, st_ref, gam_ref, rg_ref, so_ref):
    ns = st_ref.shape[0]
    for s in range(ns):
        blk = qkvg_ref[s].astype(F32)
        q8 = blk[0:8]
        k8 = blk[8:16]
        v8 = blk[16:24]
        g8 = blk[24:32]
        q_t = q8.T
        k_t = k8.T
        qk = jnp.sum(q8 * k8, axis=1, keepdims=True)
        rows = []
        for h in range(RET_HEADS):
            r_old = st_ref[s, h]
            gam = gam_ref[h]
            qcol = jnp.broadcast_to(q_t[:, h:h + 1], (RET_DK, RET_DV))
            kcol = jnp.broadcast_to(k_t[:, h:h + 1], (RET_DK, RET_DV))
            vrow = v8[h:h + 1]
            q_r = jnp.sum(qcol * r_old, axis=0, keepdims=True)
            rows.append(gam * q_r + qk[h:h + 1] * vrow)
            so_ref[s, h] = gam * r_old + kcol * vrow
        ret = jnp.concatenate(rows, axis=0)
        rg_ref[s] = _gate(ret, g8).astype(BF16)


def _sample_retention(qkvg, state, gam, ns):
    nb = state.shape[0]
    return pl.pallas_call(
        _sret_body,
        grid=(nb // ns,),
        in_specs=[
            pl.BlockSpec((ns, 32, LANES), lambda i: (i, 0, 0)),
            pl.BlockSpec((ns, RET_HEADS, RET_DK, RET_DV), lambda i: (i, 0, 0, 0)),
            pl.BlockSpec((RET_HEADS, 1, LANES), lambda i: (0, 0, 0)),
        ],
        out_specs=(
            pl.BlockSpec((ns, RET_HEADS, RET_DV), lambda i: (i, 0, 0)),
            pl.BlockSpec((ns, RET_HEADS, RET_DK, RET_DV), lambda i: (i, 0, 0, 0)),
        ),
        out_shape=(
            jax.ShapeDtypeStruct((nb, RET_HEADS, RET_DV), BF16),
            jax.ShapeDtypeStruct(state.shape, F32),
        ),
        compiler_params=_cparams(("arbitrary",)),
        name="sample_ret",
    )(qkvg, state, gam)


def _rotary_table(pos):
    half = RET_DK // 2
    inv = ROPE_BASE ** (-jnp.arange(half, dtype=F32) / half)
    ang = pos[:, None] * inv[None, :]
    cos = jnp.cos(ang)
    sin = jnp.sin(ang)
    return jnp.concatenate([cos, cos, -sin, sin], axis=1)


def _retention_constants():
    lg = jnp.log1p(-jnp.exp2(-5.0 - jnp.arange(RET_HEADS, dtype=F32)))
    n = RET_CHUNK
    i = jnp.arange(n, dtype=F32)
    diff = i[:, None] - i[None, :]
    decay = jnp.where(diff[None] >= 0,
                      jnp.exp(jnp.maximum(diff, 0.0)[None] * lg[:, None, None]), 0.0)
    rsc = jnp.exp((i + 1.0)[None, :] * lg[:, None])
    zeta = jnp.exp((n - 1.0 - i)[None, :] * lg[:, None])
    gpow = jnp.exp(n * lg)
    gam1 = jnp.exp(lg)
    rsc_b = jnp.broadcast_to(rsc[:, :, None], (RET_HEADS, n, RET_DV))
    zeta_b = jnp.broadcast_to(zeta[:, :, None], (RET_HEADS, n, RET_DK))
    gpow_b = jnp.broadcast_to(gpow[:, None, None], (RET_HEADS, 1, RET_DV))
    gam1_b = jnp.broadcast_to(gam1[:, None, None], (RET_HEADS, 1, LANES))
    return decay, rsc_b, zeta_b, gpow_b, gam1_b


def _upper_tri(n):
    r = lax.broadcasted_iota(I32, (n, n), 0)
    c = lax.broadcasted_iota(I32, (n, n), 1)
    return (r <= c).astype(BF16)


def _pad_lanes(v):
    return jnp.pad(v, (0, LANES - v.shape[0])).reshape(1, LANES)


def kernel(x_prompt, x_sample, cache_k, cache_v, cache_idx_k, state_ret, page_table,
           norm1_g, w_in, idx_k_norm_g, idx_k_norm_b, w_out, norm2_g, w_up, w_down, final_norm_g):
    batch, seq, _ = x_prompt.shape
    nb = x_sample.shape[0]
    past_len = page_table.shape[1] * cache_k.shape[1]
    half_mix = ATTN_HEADS * ATTN_HEAD_DIM

    wt = w_in.T.astype(BF16)
    wt_attn = wt[:OFF_QR + LANES]
    wt_ret = wt[OFF_QR:]
    wa = w_out[:half_mix].astype(BF16)
    wr = w_out[half_mix:].astype(BF16)
    wu = w_up.astype(BF16)
    wd = w_down.astype(BF16)
    g1 = norm1_g.reshape(1, D_MODEL)
    g2 = norm2_g.reshape(1, D_MODEL)
    gf = final_norm_g.reshape(1, D_MODEL)
    lng = _pad_lanes(idx_k_norm_g)
    lnb = _pad_lanes(idx_k_norm_b)
    decay, rsc_b, zeta_b, gpow_b, gam1_b = _retention_constants()

    xp = x_prompt.reshape(batch * seq, D_MODEL)
    cs_p = _rotary_table(jnp.arange(seq, dtype=F32))
    qa_p, qi_p, ka_p, va_p, ki_p, kd_p, wi_p = _project_attn(xp, g1, wt_attn, lng, lnb, tm=1024)
    main_p = _project_ret(xp, g1, wt_ret, cs_p, tm=512)
    attn_p = _prompt_attention(qa_p, qi_p, wi_p.T, ka_p, va_p, kd_p, _upper_tri(KEY_TILE).T,
                               batch, seq)
    rg_p, ret_state_p = _prompt_retention(main_p, decay, rsc_b, zeta_b, gpow_b, batch, seq)
    x1_p, h2_p = _out_projection(attn_p, rg_p, wa, wr, xp, g2, tm=512)
    y_p = _mlp(h2_p, wu, wd, x1_p, gf, tm=512, tf=1024)

    xs = x_sample.reshape(nb, D_MODEL)
    cs_s = _rotary_table(jnp.full((nb,), past_len, dtype=F32))
    qa_s, qi_s, ka_s, va_s, ki_s, _, wi_s = _project_attn(xs, g1, wt_attn, lng, lnb, tm=nb)
    main_s = _project_ret(xs, g1, wt_ret, cs_s, tm=nb)
    scores = _sample_index_scores(
        page_table,
        qi_s.transpose(1, 0, 2).reshape(nb, IDX_HEADS, IDX_HEAD_DIM),
        wi_s.reshape(nb, IDX_HEADS, 1),
        ki_s.reshape(nb, 1, IDX_HEAD_DIM),
        jnp.swapaxes(cache_idx_k, 1, 2))
    width = scores.shape[2]
    sel = _sample_select(scores.reshape(nb, width), _upper_tri(LANES))
    attn_s = _sample_attention(
        page_table,
        qa_s.transpose(1, 0, 2),
        sel.reshape(nb, 1, width),
        ka_s.reshape(nb, 1, ATTN_HEAD_DIM),
        va_s.reshape(nb, 1, ATTN_HEAD_DIM),
        cache_k, cache_v)
    rg_s, ret_state_s = _sample_retention(main_s.reshape(nb, 32, LANES), state_ret, gam1_b, ns=4)
    x1_s, h2_s = _out_projection(attn_s.reshape(nb, half_mix), rg_s.reshape(nb, RET_WIDTH),
                                 wa, wr, xs, g2, tm=nb)
    y_s = _mlp(h2_s, wu, wd, x1_s, gf, tm=nb, tf=1024)

    return (
        y_p.reshape(batch, seq, D_MODEL),
        y_s.reshape(nb, 1, D_MODEL),
        ka_p.reshape(batch, seq, ATTN_HEAD_DIM),
        va_p.reshape(batch, seq, ATTN_HEAD_DIM),
        ki_p.reshape(batch, seq, IDX_HEAD_DIM),
        ret_state_p,
        ka_s.reshape(nb, 1, ATTN_HEAD_DIM),
        va_s.reshape(nb, 1, ATTN_HEAD_DIM),
        ki_s.reshape(nb, 1, IDX_HEAD_DIM),
        ret_state_s,
    )
```

```python
import functools

import numpy as np
import jax
import jax.numpy as jnp
from jax import lax
from jax.experimental import pallas as pl
from jax.experimental.pallas import tpu as pltpu

F32 = jnp.float32
BF16 = jnp.bfloat16
I32 = jnp.int32

D_MODEL = 2048
ATTN_HEADS = 8
ATTN_HEAD_DIM = 128
IDX_HEADS = 16
IDX_HEAD_DIM = 64
TOPK_MAX = 256
RET_HEADS = 8
RET_DK = 128
RET_DV = 128
RET_CHUNK = 256
ROPE_BASE = 10000.0
D_FF = 4 * D_MODEL
EPS = 1e-6
Q_BLOCK = 256

OFF_QA, OFF_KA, OFF_VA, OFF_QI, OFF_KI, OFF_WI = 0, 1024, 1152, 1280, 2304, 2368
OFF_QR, OFF_KR, OFF_VR, OFF_GR = 2384, 3408, 4432, 5456
RET_WIDTH = RET_HEADS * RET_DV

LANES = 128
PROJ_TILE = 512
KEY_TILE = 256
COUNT_ROWS = 64
PAGE_SLOTS = 4
INT_MIN = -2 ** 31
KEY_NEG_INF = -2 ** 31 + 0x7FFFFF
NEG_BIG = -1e30
VMEM_LIMIT = 56 * 1024 * 1024


def _cparams(sem):
    return pltpu.CompilerParams(dimension_semantics=sem, vmem_limit_bytes=VMEM_LIMIT)


def _resident(shape):
    zeros = (0,) * len(shape)
    return pl.BlockSpec(shape, lambda *_: zeros, pipeline_mode=pl.Buffered(1))


def _normed_input(x_ref, g_ref, xn_ref):
    x = x_ref[...]
    ms = jnp.mean(x * x, axis=-1, keepdims=True)
    xn_ref[...] = (x * lax.rsqrt(ms + EPS) * g_ref[...]).astype(BF16)


def _matmul_rows(xn_ref, wt_ref, r0, n):
    return lax.dot_general(xn_ref[...], wt_ref[r0:r0 + n, :], (((1,), (1,)), ((), ())),
                           preferred_element_type=F32)


def _proj_attn_body(x_ref, g_ref, wt_ref, lng_ref, lnb_ref,
                    qa_ref, qi_ref, ka_ref, va_ref, ki_ref, kd_ref, wi_ref, xn_ref):
    _normed_input(x_ref, g_ref, xn_ref)
    mm = functools.partial(_matmul_rows, xn_ref, wt_ref)
    for t in range(ATTN_HEADS * ATTN_HEAD_DIM // PROJ_TILE):
        acc = mm(OFF_QA + t * PROJ_TILE, PROJ_TILE)
        for hh in range(4):
            qa_ref[4 * t + hh] = acc[:, hh * LANES:(hh + 1) * LANES].astype(BF16)
    for t in range(IDX_HEADS * IDX_HEAD_DIM // PROJ_TILE):
        acc = mm(OFF_QI + t * PROJ_TILE, PROJ_TILE)
        for hh in range(4):
            qi_ref[4 * t + hh] = acc[:, hh * LANES:(hh + 1) * LANES].astype(BF16)
    kv = mm(OFF_KA, 2 * ATTN_HEAD_DIM)
    ka_ref[...] = kv[:, :ATTN_HEAD_DIM]
    va_ref[...] = kv[:, ATTN_HEAD_DIM:]
    kw = mm(OFF_KI, LANES)
    lane = lax.broadcasted_iota(I32, kw.shape, 1)
    is_k = lane < IDX_HEAD_DIM
    mu = jnp.sum(jnp.where(is_k, kw, 0.0), axis=-1, keepdims=True) * (1.0 / IDX_HEAD_DIM)
    d = jnp.where(is_k, kw - mu, 0.0)
    var = jnp.sum(d * d, axis=-1, keepdims=True) * (1.0 / IDX_HEAD_DIM)
    kn = d * lax.rsqrt(var + EPS) * lng_ref[...] + lnb_ref[...]
    ki_ref[...] = kn[:, :IDX_HEAD_DIM]
    kd_ref[...] = jnp.where(is_k, kn, pltpu.roll(kn, IDX_HEAD_DIM, 1)).astype(BF16)
    wi_ref[...] = kw[:, IDX_HEAD_DIM:IDX_HEAD_DIM + IDX_HEADS] * (IDX_HEADS ** -0.5)


def _proj_ret_body(x_ref, g_ref, wt_ref, cs_ref, main_ref, xn_ref):
    _normed_input(x_ref, g_ref, xn_ref)
    base = OFF_QR
    cosf = cs_ref[:, :LANES]
    sinf = cs_ref[:, LANES:]
    tiles = RET_WIDTH // PROJ_TILE
    for seg, (off, scale) in enumerate(((OFF_QR, None), (OFF_KR, RET_DK ** -0.5))):
        for t in range(tiles):
            acc = _matmul_rows(xn_ref, wt_ref, off - base + t * PROJ_TILE, PROJ_TILE)
            for hh in range(PROJ_TILE // LANES):
                xh = acc[:, hh * LANES:(hh + 1) * LANES]
                r = xh * cosf + pltpu.roll(xh, RET_DK // 2, 1) * sinf
                if scale is not None:
                    r = r * scale
                c0 = seg * RET_WIDTH + t * PROJ_TILE + hh * LANES
                main_ref[:, c0:c0 + LANES] = r.astype(BF16)
    for seg, off in ((2, OFF_VR), (3, OFF_GR)):
        for t in range(tiles):
            acc = _matmul_rows(xn_ref, wt_ref, off - base + t * PROJ_TILE, PROJ_TILE)
            c0 = seg * RET_WIDTH + t * PROJ_TILE
            main_ref[:, c0:c0 + PROJ_TILE] = acc.astype(BF16)


def _project_attn(x2d, g1, wt_attn, lng, lnb, tm):
    m = x2d.shape[0]
    row = lambda i: (i, 0)
    out_shape = (
        jax.ShapeDtypeStruct((ATTN_HEADS, m, ATTN_HEAD_DIM), BF16),
        jax.ShapeDtypeStruct((IDX_HEADS // 2, m, LANES), BF16),
        jax.ShapeDtypeStruct((m, ATTN_HEAD_DIM), F32),
        jax.ShapeDtypeStruct((m, ATTN_HEAD_DIM), F32),
        jax.ShapeDtypeStruct((m, IDX_HEAD_DIM), F32),
        jax.ShapeDtypeStruct((m, LANES), BF16),
        jax.ShapeDtypeStruct((m, IDX_HEADS), F32),
    )
    out_specs = (
        pl.BlockSpec((ATTN_HEADS, tm, ATTN_HEAD_DIM), lambda i: (0, i, 0)),
        pl.BlockSpec((IDX_HEADS // 2, tm, LANES), lambda i: (0, i, 0)),
        pl.BlockSpec((tm, ATTN_HEAD_DIM), row),
        pl.BlockSpec((tm, ATTN_HEAD_DIM), row),
        pl.BlockSpec((tm, IDX_HEAD_DIM), row),
        pl.BlockSpec((tm, LANES), row),
        pl.BlockSpec((tm, IDX_HEADS), row),
    )
    return pl.pallas_call(
        _proj_attn_body,
        grid=(m // tm,),
        in_specs=[pl.BlockSpec((tm, D_MODEL), row), _resident((1, D_MODEL)),
                  _resident(wt_attn.shape), _resident((1, LANES)), _resident((1, LANES))],
        out_specs=out_specs,
        out_shape=out_shape,
        scratch_shapes=[pltpu.VMEM((tm, D_MODEL), BF16)],
        compiler_params=_cparams(("arbitrary",)),
        name="proj_attn",
    )(x2d, g1, wt_attn, lng, lnb)


def _project_ret(x2d, g1, wt_ret, cs, tm):
    m = x2d.shape[0]
    n_pos_blocks = cs.shape[0] // tm
    row = lambda i: (i, 0)
    return pl.pallas_call(
        _proj_ret_body,
        grid=(m // tm,),
        in_specs=[pl.BlockSpec((tm, D_MODEL), row), _resident((1, D_MODEL)),
                  _resident(wt_ret.shape),
                  pl.BlockSpec((tm, 2 * LANES), lambda i: (i % n_pos_blocks, 0))],
        out_specs=pl.BlockSpec((tm, 4 * RET_WIDTH), row),
        out_shape=jax.ShapeDtypeStruct((m, 4 * RET_WIDTH), BF16),
        scratch_shapes=[pltpu.VMEM((tm, D_MODEL), BF16)],
        compiler_params=_cparams(("arbitrary",)),
        name="proj_ret",
    )(x2d, g1, wt_ret, cs)


def _key_to_float(key):
    bits = key ^ ((key >> 31) & 0x7FFFFFFF)
    return lax.bitcast_convert_type(bits, F32)


def _threshold_search(count_ge, n_iter, shape):
    def body(it, t):
        bit = lax.shift_left(jnp.int32(1), 31 - it)
        cand = t ^ bit
        cnt = count_ge(_key_to_float(cand))
        return jnp.where(cnt >= float(TOPK_MAX), cand, t)

    t = lax.fori_loop(0, n_iter, body, jnp.full(shape, INT_MIN, I32))
    return _key_to_float(jnp.maximum(t, KEY_NEG_INF))


def _attn_body(qa_ref, qi_ref, wit_ref, ka_ref, va_ref, kd_ref, tri_ref, o_ref,
               kbf, vtb, scr, sbuf, mrun, lrun, acc_s):
    qb = pl.program_id(1)
    n_heads_q = ATTN_HEADS * Q_BLOCK
    n_pairs = IDX_HEADS // 2

    @pl.when(qb == 0)
    def _cast():
        kbf[...] = ka_ref[...].astype(BF16)
        for kt in range(vtb.shape[0]):
            vtb[kt] = va_ref[kt * KEY_TILE:(kt + 1) * KEY_TILE, :].T.astype(BF16)

    nk = ((qb + 1) * Q_BLOCK + KEY_TILE - 1) // KEY_TILE
    wt = wit_ref[...] * (IDX_HEAD_DIM ** -0.5)
    qi2 = qi_ref[...].reshape(n_pairs * Q_BLOCK, LANES)
    lo_half = lax.broadcasted_iota(I32, (KEY_TILE, LANES), 1) < IDX_HEAD_DIM
    qidx = qb * Q_BLOCK + lax.broadcasted_iota(I32, (KEY_TILE, Q_BLOCK), 1)
    kidx0 = lax.broadcasted_iota(I32, (KEY_TILE, Q_BLOCK), 0)
    contract_last = (((1,), (1,)), ((), ()))

    def idx_body(kt, carry):
        off = pl.multiple_of(kt * KEY_TILE, KEY_TILE)
        kit = kd_ref[pl.ds(off, KEY_TILE), :]
        zero = jnp.zeros_like(kit)
        s_even = lax.dot_general(jnp.where(lo_half, kit, zero), qi2, contract_last,
                                 preferred_element_type=F32)
        s_odd = lax.dot_general(jnp.where(lo_half, zero, kit), qi2, contract_last,
                                preferred_element_type=F32)
        score = jnp.zeros((KEY_TILE, Q_BLOCK), F32)
        for g in range(n_pairs):
            cs = slice(g * Q_BLOCK, (g + 1) * Q_BLOCK)
            score = score + jnp.maximum(s_even[:, cs], 0.0) * wt[2 * g:2 * g + 1, :]
            score = score + jnp.maximum(s_odd[:, cs], 0.0) * wt[2 * g + 1:2 * g + 2, :]
        scr[kt] = jnp.where(kidx0 + off <= qidx, score, -jnp.inf)
        s = lax.dot_general(kbf[pl.ds(off, KEY_TILE), :], qa2, contract_last,
                            preferred_element_type=F32)
        sbuf[kt] = s * (ATTN_HEAD_DIM ** -0.5)
        return carry

    qa2 = qa_ref[...].reshape(n_heads_q, ATTN_HEAD_DIM)
    lax.fori_loop(0, nk, idx_body, 0)

    def count_cmp(cmp):
        def body(kt, acc):
            c = jnp.where(cmp(scr[kt]), 1.0, 0.0)
            return acc + jnp.sum(c.reshape(KEY_TILE // COUNT_ROWS, COUNT_ROWS, Q_BLOCK), axis=0)
        acc = lax.fori_loop(0, nk, body, jnp.zeros((COUNT_ROWS, Q_BLOCK), F32))
        return jnp.sum(acc, axis=0, keepdims=True)

    n_iter = jnp.where(qb >= TOPK_MAX // Q_BLOCK, 32, 0)
    tf = _threshold_search(lambda c: count_cmp(lambda sc: sc >= c), n_iter, (1, Q_BLOCK))
    need = float(TOPK_MAX) - count_cmp(lambda sc: sc > tf)
    excess = jnp.max(count_cmp(lambda sc: sc >= tf)) > float(TOPK_MAX)

    mrun[...] = jnp.full((8, n_heads_q), NEG_BIG, F32)

    def mask_tile(kt, sel):
        for h in range(ATTN_HEADS):
            cs = slice(h * Q_BLOCK, (h + 1) * Q_BLOCK)
            sh = jnp.where(sel, sbuf[kt, :, cs], NEG_BIG)
            sbuf[kt, :, cs] = sh
            mrun[:, cs] = jnp.maximum(
                mrun[:, cs], jnp.max(sh.reshape(KEY_TILE // 8, 8, Q_BLOCK), axis=0))

    def p1_plain(kt, carry):
        off = pl.multiple_of(kt * KEY_TILE, KEY_TILE)
        mask_tile(kt, (scr[kt] >= tf) & (kidx0 + off <= qidx))
        return carry

    def p1_ties(kt, tie_off):
        off = pl.multiple_of(kt * KEY_TILE, KEY_TILE)
        sc = scr[kt]
        eq = sc == tf
        tie = jnp.where(eq, 1.0, 0.0)
        rank = jnp.dot(tri_ref[...], tie.astype(BF16), preferred_element_type=F32) + tie_off
        mask_tile(kt, ((sc > tf) | (eq & (rank <= need))) & (kidx0 + off <= qidx))
        return tie_off + jnp.sum(tie, axis=0, keepdims=True)

    @pl.when(excess)
    def _with_ties():
        lax.fori_loop(0, nk, p1_ties, jnp.zeros((1, Q_BLOCK), F32))

    @pl.when(jnp.logical_not(excess))
    def _without_ties():
        lax.fori_loop(0, nk, p1_plain, 0)

    m = jnp.max(mrun[...], axis=0, keepdims=True)
    lrun[...] = jnp.zeros((8, n_heads_q), F32)
    acc_s[...] = jnp.zeros((ATTN_HEAD_DIM, n_heads_q), F32)

    def p2_body(kt, carry):
        p = jnp.exp(sbuf[kt] - m)
        lrun[...] += jnp.sum(p.reshape(KEY_TILE // 8, 8, n_heads_q), axis=0)
        acc_s[...] += jnp.dot(vtb[kt], p.astype(BF16), preferred_element_type=F32)
        return carry

    lax.fori_loop(0, nk, p2_body, 0)

    out = acc_s[...] / jnp.sum(lrun[...], axis=0, keepdims=True)
    for h in range(ATTN_HEADS):
        oh = out[:, h * Q_BLOCK:(h + 1) * Q_BLOCK].T
        o_ref[:, h * ATTN_HEAD_DIM:(h + 1) * ATTN_HEAD_DIM] = oh.astype(BF16)


def _prompt_attention(qa_hm, qi_pm, wi_t, ka, va, kd, tri, batch, seq):
    nq = seq // Q_BLOCK
    nkt = seq // KEY_TILE
    m = batch * seq
    n_heads_q = ATTN_HEADS * Q_BLOCK
    in_specs = [
        pl.BlockSpec((ATTN_HEADS, Q_BLOCK, ATTN_HEAD_DIM), lambda b, q: (0, b * nq + q, 0)),
        pl.BlockSpec((IDX_HEADS // 2, Q_BLOCK, LANES), lambda b, q: (0, b * nq + q, 0)),
        pl.BlockSpec((IDX_HEADS, Q_BLOCK), lambda b, q: (0, b * nq + q)),
        pl.BlockSpec((seq, ATTN_HEAD_DIM), lambda b, q: (b, 0)),
        pl.BlockSpec((seq, ATTN_HEAD_DIM), lambda b, q: (b, 0)),
        pl.BlockSpec((seq, LANES), lambda b, q: (b, 0)),
        pl.BlockSpec((KEY_TILE, KEY_TILE), lambda b, q: (0, 0)),
    ]
    return pl.pallas_call(
        _attn_body,
        grid=(batch, nq),
        in_specs=in_specs,
        out_specs=pl.BlockSpec((Q_BLOCK, ATTN_HEADS * ATTN_HEAD_DIM), lambda b, q: (b * nq + q, 0)),
        out_shape=jax.ShapeDtypeStruct((m, ATTN_HEADS * ATTN_HEAD_DIM), BF16),
        scratch_shapes=[
            pltpu.VMEM((seq, ATTN_HEAD_DIM), BF16),
            pltpu.VMEM((nkt, ATTN_HEAD_DIM, KEY_TILE), BF16),
            pltpu.VMEM((nkt, KEY_TILE, Q_BLOCK), F32),
            pltpu.VMEM((nkt, KEY_TILE, n_heads_q), F32),
            pltpu.VMEM((8, n_heads_q), F32),
            pltpu.VMEM((8, n_heads_q), F32),
            pltpu.VMEM((ATTN_HEAD_DIM, n_heads_q), F32),
        ],
        compiler_params=_cparams(("arbitrary", "arbitrary")),
        name="prompt_attn",
    )(qa_hm, qi_pm, wi_t, ka, va, kd, tri)


def _gate(o, g):
    rn = o * lax.rsqrt(jnp.mean(o * o, axis=-1, keepdims=True) + EPS)
    return rn * (g / (1.0 + jnp.exp(-g)))


def _ret_body(q_ref, k_ref, v_ref, g_ref, decay_ref, rsc_ref, zeta_ref, gpow_ref,
              rg_ref, st_ref):
    c = pl.program_id(1)

    @pl.when(c == 0)
    def _init():
        st_ref[...] = jnp.zeros(st_ref.shape, F32)

    for h in range(RET_HEADS):
        sl = slice(h * 128, (h + 1) * 128)
        q = q_ref[:, sl]
        k = k_ref[:, sl]
        v = v_ref[:, sl]
        r_old = st_ref[0, h]
        qk = lax.dot_general(q, k, (((1,), (1,)), ((), ())), preferred_element_type=F32)
        inner = jnp.dot((qk * decay_ref[h]).astype(BF16), v, preferred_element_type=F32)
        cross = jnp.dot(q, r_old.astype(BF16), preferred_element_type=F32) * rsc_ref[h]
        kz = (k.astype(F32) * zeta_ref[h]).astype(BF16)
        upd = lax.dot_general(kz, v, (((0,), (0,)), ((), ())), preferred_element_type=F32)
        st_ref[0, h] = r_old * gpow_ref[h] + upd
        rg_ref[:, sl] = _gate(inner + cross, g_ref[:, sl].astype(F32)).astype(BF16)


def _prompt_retention(main, decay, rsc, zeta, gpow, batch, seq):
    nc = seq // RET_CHUNK
    m = batch * seq
    width = RET_WIDTH
    const3 = lambda b, c: (0, 0, 0)
    in_specs = [
        pl.BlockSpec((RET_CHUNK, width), lambda b, c: (b * nc + c, 0)),
        pl.BlockSpec((RET_CHUNK, width), lambda b, c: (b * nc + c, 1)),
        pl.BlockSpec((RET_CHUNK, width), lambda b, c: (b * nc + c, 2)),
        pl.BlockSpec((RET_CHUNK, width), lambda b, c: (b * nc + c, 3)),
        pl.BlockSpec((RET_HEADS, RET_CHUNK, RET_CHUNK), const3),
        pl.BlockSpec((RET_HEADS, RET_CHUNK, RET_DV), const3),
        pl.BlockSpec((RET_HEADS, RET_CHUNK, RET_DK), const3),
        pl.BlockSpec((RET_HEADS, 1, RET_DV), const3),
    ]
    return pl.pallas_call(
        _ret_body,
        grid=(batch, nc),
        in_specs=in_specs,
        out_specs=(
            pl.BlockSpec((RET_CHUNK, width), lambda b, c: (b * nc + c, 0)),
            pl.BlockSpec((1, RET_HEADS, RET_DK, RET_DV), lambda b, c: (b, 0, 0, 0)),
        ),
        out_shape=(
            jax.ShapeDtypeStruct((m, width), BF16),
            jax.ShapeDtypeStruct((batch, RET_HEADS, RET_DK, RET_DV), F32),
        ),
        compiler_params=_cparams(("arbitrary", "arbitrary")),
        name="prompt_ret",
    )(main, main, main, main, decay, rsc, zeta, gpow)


def _outproj_body(a_ref, r_ref, wa_ref, wr_ref, x_ref, g2_ref, x1_ref, h2_ref):
    mixed = (jnp.dot(a_ref[...], wa_ref[...], preferred_element_type=F32)
             + jnp.dot(r_ref[...], wr_ref[...], preferred_element_type=F32))
    x1 = x_ref[...] + mixed
    x1_ref[...] = x1
    ms = jnp.mean(x1 * x1, axis=-1, keepdims=True)
    h2_ref[...] = (x1 * lax.rsqrt(ms + EPS) * g2_ref[...]).astype(BF16)


def _out_projection(attn_o, rg, wa, wr, x2d, g2, tm):
    m = x2d.shape[0]
    half = attn_o.shape[1]
    in_specs = [
        pl.BlockSpec((tm, half), lambda i: (i, 0)),
        pl.BlockSpec((tm, half), lambda i: (i, 0)),
        pl.BlockSpec((half, D_MODEL), lambda i: (0, 0)),
        pl.BlockSpec((half, D_MODEL), lambda i: (0, 0)),
        pl.BlockSpec((tm, D_MODEL), lambda i: (i, 0)),
        pl.BlockSpec((1, D_MODEL), lambda i: (0, 0)),
    ]
    return pl.pallas_call(
        _outproj_body,
        grid=(m // tm,),
        in_specs=in_specs,
        out_specs=(pl.BlockSpec((tm, D_MODEL), lambda i: (i, 0)),
                   pl.BlockSpec((tm, D_MODEL), lambda i: (i, 0))),
        out_shape=(jax.ShapeDtypeStruct((m, D_MODEL), F32),
                   jax.ShapeDtypeStruct((m, D_MODEL), BF16)),
        compiler_params=_cparams(("arbitrary",)),
        name="out_proj",
    )(attn_o, rg, wa, wr, x2d, g2)


def _mlp_body(h2_ref, wu_ref, wd_ref, x1_ref, gf_ref, y_ref, acc_ref):
    f = pl.program_id(1)

    @pl.when(f == 0)
    def _init():
        acc_ref[...] = x1_ref[...]

    u = jnp.dot(h2_ref[...], wu_ref[...], preferred_element_type=F32)
    a = jnp.maximum(u, 0.0)
    acc_ref[...] += jnp.dot((a * a).astype(BF16), wd_ref[...], preferred_element_type=F32)

    @pl.when(f == pl.num_programs(1) - 1)
    def _final():
        x2 = acc_ref[...]
        ms = jnp.mean(x2 * x2, axis=-1, keepdims=True)
        y_ref[...] = x2 * lax.rsqrt(ms + EPS) * gf_ref[...]


def _mlp_cast_body(h2_ref, wu_ref, wd_ref, x1_ref, gf_ref, y_ref, wub_ref, wdb_ref, acc_ref):
    f = pl.program_id(0)

    @pl.when(f == 0)
    def _init():
        acc_ref[...] = x1_ref[...]

    wu = wu_ref[...].astype(BF16)
    wd = wd_ref[...].astype(BF16)
    wub_ref[...] = wu
    wdb_ref[...] = wd
    u = jnp.dot(h2_ref[...], wu, preferred_element_type=F32)
    a = jnp.maximum(u, 0.0)
    acc_ref[...] += jnp.dot((a * a).astype(BF16), wd, preferred_element_type=F32)

    @pl.when(f == pl.num_programs(0) - 1)
    def _final():
        x2 = acc_ref[...]
        ms = jnp.mean(x2 * x2, axis=-1, keepdims=True)
        y_ref[...] = x2 * lax.rsqrt(ms + EPS) * gf_ref[...]


def _mlp_and_cast(h2, w_up, w_down, x1, gf, tf):
    m = h2.shape[0]
    full = lambda f: (0, 0)
    return pl.pallas_call(
        _mlp_cast_body,
        grid=(D_FF // tf,),
        in_specs=[
            pl.BlockSpec((m, D_MODEL), full),
            pl.BlockSpec((D_MODEL, tf), lambda f: (0, f)),
            pl.BlockSpec((tf, D_MODEL), lambda f: (f, 0)),
            pl.BlockSpec((m, D_MODEL), full),
            pl.BlockSpec((1, D_MODEL), full),
        ],
        out_specs=(
            pl.BlockSpec((m, D_MODEL), full),
            pl.BlockSpec((D_MODEL, tf), lambda f: (0, f)),
            pl.BlockSpec((tf, D_MODEL), lambda f: (f, 0)),
        ),
        out_shape=(
            jax.ShapeDtypeStruct((m, D_MODEL), F32),
            jax.ShapeDtypeStruct((D_MODEL, D_FF), BF16),
            jax.ShapeDtypeStruct((D_FF, D_MODEL), BF16),
        ),
        scratch_shapes=[pltpu.VMEM((m, D_MODEL), F32)],
        compiler_params=_cparams(("arbitrary",)),
        name="mlp_cast",
    )(h2, w_up, w_down, x1, gf)


def _mlp(h2, wu, wd, x1, gf, tm, tf):
    m = h2.shape[0]
    in_specs = [
        pl.BlockSpec((tm, D_MODEL), lambda i, f: (i, 0)),
        pl.BlockSpec((D_MODEL, tf), lambda i, f: (0, f)),
        pl.BlockSpec((tf, D_MODEL), lambda i, f: (f, 0)),
        pl.BlockSpec((tm, D_MODEL), lambda i, f: (i, 0)),
        pl.BlockSpec((1, D_MODEL), lambda i, f: (0, 0)),
    ]
    return pl.pallas_call(
        _mlp_body,
        grid=(m // tm, D_FF // tf),
        in_specs=in_specs,
        out_specs=pl.BlockSpec((tm, D_MODEL), lambda i, f: (i, 0)),
        out_shape=jax.ShapeDtypeStruct((m, D_MODEL), F32),
        scratch_shapes=[pltpu.VMEM((tm, D_MODEL), F32)],
        compiler_params=_cparams(("arbitrary", "arbitrary")),
        name="mlp",
    )(h2, wu, wd, x1, gf)


def _fetch_pages(pt_ref, step, slot, streams, start):
    n_pages = pt_ref.shape[1]
    for hbm, buf, sem in streams:
        for j in range(n_pages):
            cp = pltpu.make_async_copy(hbm.at[pt_ref[step, j]], buf.at[slot, j], sem.at[slot])
            if start:
                cp.start()
            else:
                cp.wait()


def _paged_prefetch(pt_ref, streams):
    b = pl.program_id(0)
    n_slots = streams[0][1].shape[0]
    ahead = n_slots - 1

    @pl.when(b == 0)
    def _first():
        for s in range(ahead):
            _fetch_pages(pt_ref, s, s, streams, start=True)

    @pl.when(b + ahead < pl.num_programs(0))
    def _next():
        _fetch_pages(pt_ref, b + ahead, (b + ahead) % n_slots, streams, start=True)

    slot = b % n_slots
    _fetch_pages(pt_ref, b, slot, streams, start=False)
    return slot


def _sidx_body(pt_ref, qi_ref, w_ref, kin_ref, cache_hbm, out_ref, kt_s, pbuf, sem):
    n_pages = pt_ref.shape[1]
    page = pbuf.shape[3]
    past = n_pages * page
    slot = _paged_prefetch(pt_ref, ((cache_hbm, pbuf, sem),))
    qi = qi_ref[0]
    w = w_ref[0] * (IDX_HEAD_DIM ** -0.5)
    for j in range(n_pages):
        kt_s[:, j * page:(j + 1) * page] = pbuf[slot, j].astype(BF16)
    s = jnp.dot(qi, kt_s[...], preferred_element_type=F32)
    out_ref[0, :, 0:past] = jnp.sum(jnp.maximum(s, 0.0) * w, axis=0, keepdims=True)
    sn = jnp.sum(qi.astype(F32) * kin_ref[0].astype(BF16).astype(F32), axis=1, keepdims=True)
    rn = jnp.sum(jnp.maximum(sn, 0.0) * w, axis=0, keepdims=True)
    lane = lax.broadcasted_iota(I32, (1, LANES), 1)
    out_ref[0, :, past:past + LANES] = jnp.where(lane == 0, rn, -jnp.inf)


def _sample_index_scores(page_table, qi_s, wi_s, ki_s, cache_idx_k_t):
    nb, n_pages = page_table.shape
    page = cache_idx_k_t.shape[2]
    width = n_pages * page + LANES

    grid_spec = pltpu.PrefetchScalarGridSpec(
        num_scalar_prefetch=1,
        grid=(nb,),
        in_specs=[
            pl.BlockSpec((1, IDX_HEADS, IDX_HEAD_DIM), lambda b, pt: (b, 0, 0)),
            pl.BlockSpec((1, IDX_HEADS, 1), lambda b, pt: (b, 0, 0)),
            pl.BlockSpec((1, 1, IDX_HEAD_DIM), lambda b, pt: (b, 0, 0)),
            pl.BlockSpec(memory_space=pl.ANY),
        ],
        out_specs=pl.BlockSpec((1, 1, width), lambda b, pt: (b, 0, 0)),
        scratch_shapes=[pltpu.VMEM((IDX_HEAD_DIM, n_pages * page), BF16),
                        pltpu.VMEM((PAGE_SLOTS, n_pages, IDX_HEAD_DIM, page), F32),
                        pltpu.SemaphoreType.DMA((PAGE_SLOTS,))],
    )
    return pl.pallas_call(
        _sidx_body,
        grid_spec=grid_spec,
        out_shape=jax.ShapeDtypeStruct((nb, 1, width), F32),
        compiler_params=_cparams(("arbitrary",)),
        name="sample_idx",
    )(page_table, qi_s, wi_s, ki_s, cache_idx_k_t)


def _ssel_body(sc_ref, tri_ref, sel_ref):
    rows, width = sc_ref.shape
    nt = width // LANES
    n_valid = (nt - 1) * LANES + 1

    def tile(kt):
        return sc_ref[:, kt * LANES:(kt + 1) * LANES]

    def count_cmp(cmp):
        acc = jnp.zeros((rows, LANES), F32)
        for kt in range(nt):
            acc = acc + jnp.where(cmp(tile(kt)), 1.0, 0.0)
        return jnp.broadcast_to(jnp.sum(acc, axis=1, keepdims=True), (rows, LANES))

    tf = _threshold_search(lambda c: count_cmp(lambda sc: sc >= c), 32, (rows, LANES))
    need = float(TOPK_MAX) - count_cmp(lambda sc: sc > tf)
    tie_off = jnp.zeros((rows, LANES), F32)
    for kt in range(nt):
        col = kt * LANES + lax.broadcasted_iota(I32, (rows, LANES), 1)
        sc = tile(kt)
        eq = sc == tf
        tie = jnp.where(eq, 1.0, 0.0)
        rank = jnp.dot(tie.astype(BF16), tri_ref[...], preferred_element_type=F32) + tie_off
        sel = ((sc > tf) | (eq & (rank <= need))) & (col < n_valid)
        sel_ref[:, kt * LANES:(kt + 1) * LANES] = jnp.where(sel, 1.0, 0.0)
        tie_off = tie_off + jnp.broadcast_to(jnp.sum(tie, axis=1, keepdims=True), (rows, LANES))


def _sample_select(scores2d, tri):
    rows, width = scores2d.shape
    return pl.pallas_call(
        _ssel_body,
        out_shape=jax.ShapeDtypeStruct((rows, width), F32),
        compiler_params=pltpu.CompilerParams(vmem_limit_bytes=VMEM_LIMIT),
        name="sample_select",
    )(scores2d, tri)


def _sattn_body(pt_ref, q_ref, sel_ref, kn_ref, vn_ref, ck_hbm, cv_hbm, o_ref,
                kbuf, vbuf, ksem, vsem):
    n_pages = pt_ref.shape[1]
    page = kbuf.shape[2]
    past = n_pages * page
    slot = _paged_prefetch(pt_ref, ((ck_hbm, kbuf, ksem), (cv_hbm, vbuf, vsem)))
    q = q_ref[0]
    scale = ATTN_HEAD_DIM ** -0.5
    k_all = kbuf[slot].reshape(past, ATTN_HEAD_DIM).astype(BF16)
    v_all = vbuf[slot].reshape(past, ATTN_HEAD_DIM).astype(BF16)
    s = lax.dot_general(q, k_all, (((1,), (1,)), ((), ())), preferred_element_type=F32)
    s = jnp.where(sel_ref[0, :, 0:past] > 0.5, s * scale, NEG_BIG)
    kn = kn_ref[0].astype(BF16).astype(F32)
    sn = jnp.sum(q.astype(F32) * kn, axis=1, keepdims=True) * scale
    sn = jnp.where(sel_ref[0, :, past:past + 1] > 0.5, sn, NEG_BIG)
    m = jnp.maximum(jnp.max(s, axis=1, keepdims=True), sn)
    pn = jnp.exp(sn - m)
    p = jnp.exp(s - m)
    l = pn + jnp.sum(p, axis=1, keepdims=True)
    acc = (pn * vn_ref[0].astype(BF16).astype(F32)
           + jnp.dot(p.astype(BF16), v_all, preferred_element_type=F32))
    o_ref[0] = (acc / l).astype(BF16)


def _sample_attention(page_table, qa_s, sel3, ka_s, va_s, cache_k, cache_v):
    nb, n_pages = page_table.shape
    page = cache_k.shape[1]

    grid_spec = pltpu.PrefetchScalarGridSpec(
        num_scalar_prefetch=1,
        grid=(nb,),
        in_specs=[
            pl.BlockSpec((1, ATTN_HEADS, ATTN_HEAD_DIM), lambda b, pt: (b, 0, 0)),
            pl.BlockSpec((1, 1, sel3.shape[2]), lambda b, pt: (b, 0, 0)),
            pl.BlockSpec((1, 1, ATTN_HEAD_DIM), lambda b, pt: (b, 0, 0)),
            pl.BlockSpec((1, 1, ATTN_HEAD_DIM), lambda b, pt: (b, 0, 0)),
            pl.BlockSpec(memory_space=pl.ANY),
            pl.BlockSpec(memory_space=pl.ANY),
        ],
        out_specs=pl.BlockSpec((1, ATTN_HEADS, ATTN_HEAD_DIM), lambda b, pt: (b, 0, 0)),
        scratch_shapes=[pltpu.VMEM((PAGE_SLOTS, n_pages, page, ATTN_HEAD_DIM), F32),
                        pltpu.VMEM((PAGE_SLOTS, n_pages, page, ATTN_HEAD_DIM), F32),
                        pltpu.SemaphoreType.DMA((PAGE_SLOTS,)),
                        pltpu.SemaphoreType.DMA((PAGE_SLOTS,))],
    )
    return pl.pallas_call(
        _sattn_body,
        grid_spec=grid_spec,
        out_shape=jax.ShapeDtypeStruct((nb, ATTN_HEADS, ATTN_HEAD_DIM), BF16),
        compiler_params=_cparams(("arbitrary",)),
        name="sample_attn",
    )(page_table, qa_s, sel3, ka_s, va_s, cache_k, cache_v)


def _sret_body(qkvg_ref, st_ref, gam_ref, rg_ref, so_ref):
    ns = st_ref.shape[0]
    for s in range(ns):
        blk = qkvg_ref[s].astype(F32)
        q8 = blk[0:8]
        k8 = blk[8:16]
        v8 = blk[16:24]
        g8 = blk[24:32]
        q_t = q8.T
        k_t = k8.T
        qk = jnp.sum(q8 * k8, axis=1, keepdims=True)
        rows = []
        for h in range(RET_HEADS):
            r_old = st_ref[s, h]
            gam = gam_ref[h]
            qcol = jnp.broadcast_to(q_t[:, h:h + 1], (RET_DK, RET_DV))
            kcol = jnp.broadcast_to(k_t[:, h:h + 1], (RET_DK, RET_DV))
            vrow = v8[h:h + 1]
            q_r = jnp.sum(qcol * r_old, axis=0, keepdims=True)
            rows.append(gam * q_r + qk[h:h + 1] * vrow)
            so_ref[s, h] = gam * r_old + kcol * vrow
        ret = jnp.concatenate(rows, axis=0)
        rg_ref[s] = _gate(ret, g8).astype(BF16)


def _sample_retention(qkvg, state, gam, ns):
    nb = state.shape[0]
    return pl.pallas_call(
        _sret_body,
        grid=(nb // ns,),
        in_specs=[
            pl.BlockSpec((ns, 32, LANES), lambda i: (i, 0, 0)),
            pl.BlockSpec((ns, RET_HEADS, RET_DK, RET_DV), lambda i: (i, 0, 0, 0)),
            pl.BlockSpec((RET_HEADS, 1, LANES), lambda i: (0, 0, 0)),
        ],
        out_specs=(
            pl.BlockSpec((ns, RET_HEADS, RET_DV), lambda i: (i, 0, 0)),
            pl.BlockSpec((ns, RET_HEADS, RET_DK, RET_DV), lambda i: (i, 0, 0, 0)),
        ),
        out_shape=(
            jax.ShapeDtypeStruct((nb, RET_HEADS, RET_DV), BF16),
            jax.ShapeDtypeStruct(state.shape, F32),
        ),
        compiler_params=_cparams(("arbitrary",)),
        name="sample_ret",
    )(qkvg, state, gam)


def _rotary_table(pos):
    half = RET_DK // 2
    inv = ROPE_BASE ** (-jnp.arange(half, dtype=F32) / half)
    ang = pos[:, None] * inv[None, :]
    cos = jnp.cos(ang)
    sin = jnp.sin(ang)
    return jnp.concatenate([cos, cos, -sin, sin], axis=1)


def _retention_constants():
    lg = jnp.log1p(-jnp.exp2(-5.0 - jnp.arange(RET_HEADS, dtype=F32)))
    n = RET_CHUNK
    i = jnp.arange(n, dtype=F32)
    diff = i[:, None] - i[None, :]
    decay = jnp.where(diff[None] >= 0,
                      jnp.exp(jnp.maximum(diff, 0.0)[None] * lg[:, None, None]), 0.0)
    rsc = jnp.exp((i + 1.0)[None, :] * lg[:, None])
    zeta = jnp.exp((n - 1.0 - i)[None, :] * lg[:, None])
    gpow = jnp.exp(n * lg)
    gam1 = jnp.exp(lg)
    rsc_b = jnp.broadcast_to(rsc[:, :, None], (RET_HEADS, n, RET_DV))
    zeta_b = jnp.broadcast_to(zeta[:, :, None], (RET_HEADS, n, RET_DK))
    gpow_b = jnp.broadcast_to(gpow[:, None, None], (RET_HEADS, 1, RET_DV))
    gam1_b = jnp.broadcast_to(gam1[:, None, None], (RET_HEADS, 1, LANES))
    return decay, rsc_b, zeta_b, gpow_b, gam1_b


def _upper_tri(n):
    r = lax.broadcasted_iota(I32, (n, n), 0)
    c = lax.broadcasted_iota(I32, (n, n), 1)
    return (r <= c).astype(BF16)


def _pad_lanes(v):
    return jnp.pad(v, (0, LANES - v.shape[0])).reshape(1, LANES)


def kernel(x_prompt, x_sample, cache_k, cache_v, cache_idx_k, state_ret, page_table,
           norm1_g, w_in, idx_k_norm_g, idx_k_norm_b, w_out, norm2_g, w_up, w_down, final_norm_g):
    batch, seq, _ = x_prompt.shape
    nb = x_sample.shape[0]
    past_len = page_table.shape[1] * cache_k.shape[1]
    half_mix = ATTN_HEADS * ATTN_HEAD_DIM

    wt_attn = w_in[:, :OFF_QR + LANES].T.astype(BF16)
    wt_ret = w_in[:, OFF_QR:].T.astype(BF16)
    wa = w_out[:half_mix].astype(BF16)
    wr = w_out[half_mix:].astype(BF16)
    g1 = norm1_g.reshape(1, D_MODEL)
    g2 = norm2_g.reshape(1, D_MODEL)
    gf = final_norm_g.reshape(1, D_MODEL)
    lng = _pad_lanes(idx_k_norm_g)
    lnb = _pad_lanes(idx_k_norm_b)
    decay, rsc_b, zeta_b, gpow_b, gam1_b = _retention_constants()

    xp = x_prompt.reshape(batch * seq, D_MODEL)
    cs_p = _rotary_table(jnp.arange(seq, dtype=F32))
    qa_p, qi_p, ka_p, va_p, ki_p, kd_p, wi_p = _project_attn(xp, g1, wt_attn, lng, lnb, tm=1024)
    main_p = _project_ret(xp, g1, wt_ret, cs_p, tm=512)
    attn_p = _prompt_attention(qa_p, qi_p, wi_p.T, ka_p, va_p, kd_p, _upper_tri(KEY_TILE).T,
                               batch, seq)
    rg_p, ret_state_p = _prompt_retention(main_p, decay, rsc_b, zeta_b, gpow_b, batch, seq)
    x1_p, h2_p = _out_projection(attn_p, rg_p, wa, wr, xp, g2, tm=512)

    assert nb >= PAGE_SLOTS
    xs = x_sample.reshape(nb, D_MODEL)
    cs_s = _rotary_table(jnp.full((nb,), past_len, dtype=F32))
    qa_s, qi_s, ka_s, va_s, ki_s, _, wi_s = _project_attn(xs, g1, wt_attn, lng, lnb, tm=nb)
    main_s = _project_ret(xs, g1, wt_ret, cs_s, tm=nb)
    scores = _sample_index_scores(
        page_table,
        qi_s.transpose(1, 0, 2).reshape(nb, IDX_HEADS, IDX_HEAD_DIM),
        wi_s.reshape(nb, IDX_HEADS, 1),
        ki_s.reshape(nb, 1, IDX_HEAD_DIM),
        jnp.swapaxes(cache_idx_k, 1, 2))
    width = scores.shape[2]
    sel = _sample_select(scores.reshape(nb, width), _upper_tri(LANES))
    attn_s = _sample_attention(
        page_table,
        qa_s.transpose(1, 0, 2),
        sel.reshape(nb, 1, width),
        ka_s.reshape(nb, 1, ATTN_HEAD_DIM),
        va_s.reshape(nb, 1, ATTN_HEAD_DIM),
        cache_k, cache_v)
    rg_s, ret_state_s = _sample_retention(main_s.reshape(nb, 32, LANES), state_ret, gam1_b, ns=8)
    x1_s, h2_s = _out_projection(attn_s.reshape(nb, half_mix), rg_s.reshape(nb, RET_WIDTH),
                                 wa, wr, xs, g2, tm=nb)
    y_s, wu, wd = _mlp_and_cast(h2_s, w_up, w_down, x1_s, gf, tf=512)
    y_p = _mlp(h2_p, wu, wd, x1_p, gf, tm=512, tf=1024)

    return (
        y_p.reshape(batch, seq, D_MODEL),
        y_s.reshape(nb, 1, D_MODEL),
        ka_p.reshape(batch, seq, ATTN_HEAD_DIM),
        va_p.reshape(batch, seq, ATTN_HEAD_DIM),
        ki_p.reshape(batch, seq, IDX_HEAD_DIM),
        ret_state_p,
        ka_s.reshape(nb, 1, ATTN_HEAD_DIM),
        va_s.reshape(nb, 1, ATTN_HEAD_DIM),
        ki_s.reshape(nb, 1, IDX_HEAD_DIM),
        ret_state_s,
    )
```

```python
import functools

import numpy as np
import jax
import jax.numpy as jnp
from jax import lax
from jax.experimental import pallas as pl
from jax.experimental.pallas import tpu as pltpu

F32 = jnp.float32
BF16 = jnp.bfloat16
I32 = jnp.int32

D_MODEL = 2048
ATTN_HEADS = 8
ATTN_HEAD_DIM = 128
IDX_HEADS = 16
IDX_HEAD_DIM = 64
TOPK_MAX = 256
RET_HEADS = 8
RET_DK = 128
RET_DV = 128
RET_CHUNK = 256
ROPE_BASE = 10000.0
D_FF = 4 * D_MODEL
EPS = 1e-6
Q_BLOCK = 256

OFF_QA, OFF_KA, OFF_VA, OFF_QI, OFF_KI, OFF_WI = 0, 1024, 1152, 1280, 2304, 2368
OFF_QR, OFF_KR, OFF_VR, OFF_GR = 2384, 3408, 4432, 5456
RET_WIDTH = RET_HEADS * RET_DV

LANES = 128
PROJ_TILE = 512
KEY_TILE = 256
COUNT_ROWS = 64
PAGE_SLOTS = 4
SUM_ROWS = 16
LOG2_E = 1.4426950408889634
INT_MIN = -2 ** 31
KEY_NEG_INF = -2 ** 31 + 0x7FFFFF
NEG_BIG = -1e30
VMEM_LIMIT = 56 * 1024 * 1024


def _cparams(sem):
    return pltpu.CompilerParams(dimension_semantics=sem, vmem_limit_bytes=VMEM_LIMIT)


def _resident(shape):
    zeros = (0,) * len(shape)
    return pl.BlockSpec(shape, lambda *_: zeros, pipeline_mode=pl.Buffered(1))


def _normed_input(x_ref, g_ref, xn_ref):
    x = x_ref[...]
    ms = jnp.mean(x * x, axis=-1, keepdims=True)
    xn_ref[...] = (x * lax.rsqrt(ms + EPS) * g_ref[...]).astype(BF16)


def _matmul_rows(xn_ref, wt_ref, r0, n):
    return lax.dot_general(xn_ref[...], wt_ref[r0:r0 + n, :], (((1,), (1,)), ((), ())),
                           preferred_element_type=F32)


def _proj_attn_body(x_ref, g_ref, wt_ref, lng_ref, lnb_ref,
                    qa_ref, qi_ref, ka_ref, va_ref, ki_ref, kd_ref, wi_ref, xn_ref):
    _normed_input(x_ref, g_ref, xn_ref)
    mm = functools.partial(_matmul_rows, xn_ref, wt_ref)
    for t in range(ATTN_HEADS * ATTN_HEAD_DIM // PROJ_TILE):
        acc = mm(OFF_QA + t * PROJ_TILE, PROJ_TILE)
        for hh in range(4):
            qa_ref[4 * t + hh] = acc[:, hh * LANES:(hh + 1) * LANES].astype(BF16)
    for t in range(IDX_HEADS * IDX_HEAD_DIM // PROJ_TILE):
        acc = mm(OFF_QI + t * PROJ_TILE, PROJ_TILE)
        for hh in range(4):
            qi_ref[4 * t + hh] = acc[:, hh * LANES:(hh + 1) * LANES].astype(BF16)
    kv = mm(OFF_KA, 2 * ATTN_HEAD_DIM)
    ka_ref[...] = kv[:, :ATTN_HEAD_DIM]
    va_ref[...] = kv[:, ATTN_HEAD_DIM:]
    kw = mm(OFF_KI, LANES)
    lane = lax.broadcasted_iota(I32, kw.shape, 1)
    is_k = lane < IDX_HEAD_DIM
    mu = jnp.sum(jnp.where(is_k, kw, 0.0), axis=-1, keepdims=True) * (1.0 / IDX_HEAD_DIM)
    d = jnp.where(is_k, kw - mu, 0.0)
    var = jnp.sum(d * d, axis=-1, keepdims=True) * (1.0 / IDX_HEAD_DIM)
    kn = d * lax.rsqrt(var + EPS) * lng_ref[...] + lnb_ref[...]
    ki_ref[...] = kn[:, :IDX_HEAD_DIM]
    kd_ref[...] = jnp.where(is_k, kn, pltpu.roll(kn, IDX_HEAD_DIM, 1)).astype(BF16)
    wi_ref[...] = kw[:, IDX_HEAD_DIM:IDX_HEAD_DIM + IDX_HEADS] * (IDX_HEADS ** -0.5)


def _proj_ret_body(x_ref, g_ref, wt_ref, cs_ref, main_ref, xn_ref):
    _normed_input(x_ref, g_ref, xn_ref)
    base = OFF_QR
    cosf = cs_ref[:, :LANES]
    sinf = cs_ref[:, LANES:]
    tiles = RET_WIDTH // PROJ_TILE
    for seg, (off, scale) in enumerate(((OFF_QR, None), (OFF_KR, RET_DK ** -0.5))):
        for t in range(tiles):
            acc = _matmul_rows(xn_ref, wt_ref, off - base + t * PROJ_TILE, PROJ_TILE)
            for hh in range(PROJ_TILE // LANES):
                xh = acc[:, hh * LANES:(hh + 1) * LANES]
                r = xh * cosf + pltpu.roll(xh, RET_DK // 2, 1) * sinf
                if scale is not None:
                    r = r * scale
                c0 = seg * RET_WIDTH + t * PROJ_TILE + hh * LANES
                main_ref[:, c0:c0 + LANES] = r.astype(BF16)
    for seg, off in ((2, OFF_VR), (3, OFF_GR)):
        for t in range(tiles):
            acc = _matmul_rows(xn_ref, wt_ref, off - base + t * PROJ_TILE, PROJ_TILE)
            c0 = seg * RET_WIDTH + t * PROJ_TILE
            main_ref[:, c0:c0 + PROJ_TILE] = acc.astype(BF16)


def _project_attn(x2d, g1, wt_attn, lng, lnb, tm):
    m = x2d.shape[0]
    row = lambda i: (i, 0)
    out_shape = (
        jax.ShapeDtypeStruct((ATTN_HEADS, m, ATTN_HEAD_DIM), BF16),
        jax.ShapeDtypeStruct((IDX_HEADS // 2, m, LANES), BF16),
        jax.ShapeDtypeStruct((m, ATTN_HEAD_DIM), F32),
        jax.ShapeDtypeStruct((m, ATTN_HEAD_DIM), F32),
        jax.ShapeDtypeStruct((m, IDX_HEAD_DIM), F32),
        jax.ShapeDtypeStruct((m, LANES), BF16),
        jax.ShapeDtypeStruct((m, IDX_HEADS), F32),
    )
    out_specs = (
        pl.BlockSpec((ATTN_HEADS, tm, ATTN_HEAD_DIM), lambda i: (0, i, 0)),
        pl.BlockSpec((IDX_HEADS // 2, tm, LANES), lambda i: (0, i, 0)),
        pl.BlockSpec((tm, ATTN_HEAD_DIM), row),
        pl.BlockSpec((tm, ATTN_HEAD_DIM), row),
        pl.BlockSpec((tm, IDX_HEAD_DIM), row),
        pl.BlockSpec((tm, LANES), row),
        pl.BlockSpec((tm, IDX_HEADS), row),
    )
    return pl.pallas_call(
        _proj_attn_body,
        grid=(m // tm,),
        in_specs=[pl.BlockSpec((tm, D_MODEL), row), _resident((1, D_MODEL)),
                  _resident(wt_attn.shape), _resident((1, LANES)), _resident((1, LANES))],
        out_specs=out_specs,
        out_shape=out_shape,
        scratch_shapes=[pltpu.VMEM((tm, D_MODEL), BF16)],
        compiler_params=_cparams(("arbitrary",)),
        name="proj_attn",
    )(x2d, g1, wt_attn, lng, lnb)


def _project_ret(x2d, g1, wt_ret, cs, tm):
    m = x2d.shape[0]
    n_pos_blocks = cs.shape[0] // tm
    row = lambda i: (i, 0)
    return pl.pallas_call(
        _proj_ret_body,
        grid=(m // tm,),
        in_specs=[pl.BlockSpec((tm, D_MODEL), row), _resident((1, D_MODEL)),
                  _resident(wt_ret.shape),
                  pl.BlockSpec((tm, 2 * LANES), lambda i: (i % n_pos_blocks, 0))],
        out_specs=pl.BlockSpec((tm, 4 * RET_WIDTH), row),
        out_shape=jax.ShapeDtypeStruct((m, 4 * RET_WIDTH), BF16),
        scratch_shapes=[pltpu.VMEM((tm, D_MODEL), BF16)],
        compiler_params=_cparams(("arbitrary",)),
        name="proj_ret",
    )(x2d, g1, wt_ret, cs)


def _key_to_float(key):
    bits = key ^ ((key >> 31) & 0x7FFFFFFF)
    return lax.bitcast_convert_type(bits, F32)


def _threshold_search(count_ge, n_iter, shape):
    def body(it, t):
        bit = lax.shift_left(jnp.int32(1), 31 - it)
        cand = t ^ bit
        cnt = count_ge(_key_to_float(cand))
        return jnp.where(cnt >= float(TOPK_MAX), cand, t)

    t = lax.fori_loop(0, n_iter, body, jnp.full(shape, INT_MIN, I32))
    return _key_to_float(jnp.maximum(t, KEY_NEG_INF))


def _attn_body(qa_ref, qi_ref, wit_ref, ka_ref, va_ref, kd_ref, tri_ref, o_ref,
               kbf, vtb, scr, sbuf, mrun, acc_s):
    qb = pl.program_id(1)
    n_heads_q = ATTN_HEADS * Q_BLOCK
    n_pairs = IDX_HEADS // 2
    dv = ATTN_HEAD_DIM

    @pl.when(qb == 0)
    def _cast():
        kbf[...] = ka_ref[...].astype(BF16)
        for kt in range(vtb.shape[0]):
            vtb[kt, :dv] = va_ref[kt * KEY_TILE:(kt + 1) * KEY_TILE, :].T.astype(BF16)
            vtb[kt, dv:] = jnp.ones((vtb.shape[1] - dv, KEY_TILE), BF16)

    nk = ((qb + 1) * Q_BLOCK + KEY_TILE - 1) // KEY_TILE
    wt = wit_ref[...] * (IDX_HEAD_DIM ** -0.5)
    qi2 = qi_ref[...].reshape(n_pairs * Q_BLOCK, LANES)
    lo_half = lax.broadcasted_iota(I32, (KEY_TILE, LANES), 1) < IDX_HEAD_DIM
    qidx = qb * Q_BLOCK + lax.broadcasted_iota(I32, (KEY_TILE, Q_BLOCK), 1)
    kidx0 = lax.broadcasted_iota(I32, (KEY_TILE, Q_BLOCK), 0)
    contract_last = (((1,), (1,)), ((), ()))

    def idx_body(kt, carry):
        off = pl.multiple_of(kt * KEY_TILE, KEY_TILE)
        kit = kd_ref[pl.ds(off, KEY_TILE), :]
        zero = jnp.zeros_like(kit)
        s_even = lax.dot_general(jnp.where(lo_half, kit, zero), qi2, contract_last,
                                 preferred_element_type=F32)
        s_odd = lax.dot_general(jnp.where(lo_half, zero, kit), qi2, contract_last,
                                preferred_element_type=F32)
        score = jnp.zeros((KEY_TILE, Q_BLOCK), F32)
        for g in range(n_pairs):
            cs = slice(g * Q_BLOCK, (g + 1) * Q_BLOCK)
            score = score + jnp.maximum(s_even[:, cs], 0.0) * wt[2 * g:2 * g + 1, :]
            score = score + jnp.maximum(s_odd[:, cs], 0.0) * wt[2 * g + 1:2 * g + 2, :]
        scr[kt] = jnp.where(kidx0 + off <= qidx, score, -jnp.inf)
        s = lax.dot_general(kbf[pl.ds(off, KEY_TILE), :], qa2, contract_last,
                            preferred_element_type=F32)
        s = s * (ATTN_HEAD_DIM ** -0.5 * LOG2_E)
        sbuf[kt] = s
        mrun[...] = jnp.maximum(mrun[...],
                                jnp.max(s.reshape(KEY_TILE // 8, 8, n_heads_q), axis=0))
        return carry

    qa2 = qa_ref[...].reshape(n_heads_q, ATTN_HEAD_DIM)
    mrun[...] = jnp.full((8, n_heads_q), NEG_BIG, F32)
    lax.fori_loop(0, nk, idx_body, 0)

    def count_cmp(cmp):
        def body(kt, acc):
            c = jnp.where(cmp(scr[kt]), 1.0, 0.0)
            return acc + jnp.sum(c.reshape(KEY_TILE // COUNT_ROWS, COUNT_ROWS, Q_BLOCK), axis=0)
        acc = lax.fori_loop(0, nk, body, jnp.zeros((COUNT_ROWS, Q_BLOCK), F32))
        return jnp.sum(acc, axis=0, keepdims=True)

    n_iter = jnp.where(qb >= TOPK_MAX // Q_BLOCK, 32, 0)
    tf = _threshold_search(lambda c: count_cmp(lambda sc: sc >= c), n_iter, (1, Q_BLOCK))
    need = float(TOPK_MAX) - count_cmp(lambda sc: sc > tf)
    excess = jnp.max(count_cmp(lambda sc: sc >= tf)) > float(TOPK_MAX)

    def softmax_sum(m, masked):
        acc_s[...] = jnp.zeros(acc_s.shape, F32)

        def body(kt, carry):
            if masked:
                p = jnp.exp2(sbuf[kt] - m).astype(BF16)
            else:
                off = pl.multiple_of(kt * KEY_TILE, KEY_TILE)
                sel = (scr[kt] >= tf) & (kidx0 + off <= qidx)
                parts = []
                for h in range(ATTN_HEADS):
                    cs = slice(h * Q_BLOCK, (h + 1) * Q_BLOCK)
                    ph = jnp.where(sel, jnp.exp2(sbuf[kt, :, cs] - m[:, cs]), 0.0)
                    parts.append(ph.astype(BF16))
                p = jnp.concatenate(parts, axis=1)
            acc_s[...] += jnp.dot(vtb[kt], p, preferred_element_type=F32)
            return carry

        lax.fori_loop(0, nk, body, 0)

    def fast_path():
        softmax_sum(jnp.max(mrun[...], axis=0, keepdims=True), masked=False)
        return (jnp.min(acc_s[dv:dv + 1, :]) > 0.0).astype(I32)

    done = lax.cond(excess, lambda: jnp.int32(0), fast_path) == 1

    def mask_tile(kt, sel):
        for h in range(ATTN_HEADS):
            cs = slice(h * Q_BLOCK, (h + 1) * Q_BLOCK)
            sh = jnp.where(sel, sbuf[kt, :, cs], NEG_BIG)
            sbuf[kt, :, cs] = sh
            mrun[:, cs] = jnp.maximum(
                mrun[:, cs], jnp.max(sh.reshape(KEY_TILE // 8, 8, Q_BLOCK), axis=0))

    def p1_plain(kt, carry):
        off = pl.multiple_of(kt * KEY_TILE, KEY_TILE)
        mask_tile(kt, (scr[kt] >= tf) & (kidx0 + off <= qidx))
        return carry

    def p1_ties(kt, tie_off):
        off = pl.multiple_of(kt * KEY_TILE, KEY_TILE)
        sc = scr[kt]
        eq = sc == tf
        tie = jnp.where(eq, 1.0, 0.0)
        rank = jnp.dot(tri_ref[...], tie.astype(BF16), preferred_element_type=F32) + tie_off
        mask_tile(kt, ((sc > tf) | (eq & (rank <= need))) & (kidx0 + off <= qidx))
        return tie_off + jnp.sum(tie, axis=0, keepdims=True)

    @pl.when(jnp.logical_not(done))
    def _exact_path():
        mrun[...] = jnp.full((8, n_heads_q), NEG_BIG, F32)

        @pl.when(excess)
        def _with_ties():
            lax.fori_loop(0, nk, p1_ties, jnp.zeros((1, Q_BLOCK), F32))

        @pl.when(jnp.logical_not(excess))
        def _without_ties():
            lax.fori_loop(0, nk, p1_plain, 0)

        softmax_sum(jnp.max(mrun[...], axis=0, keepdims=True), masked=True)

    out = acc_s[:dv, :] / acc_s[dv:dv + 1, :]
    for h in range(ATTN_HEADS):
        oh = out[:, h * Q_BLOCK:(h + 1) * Q_BLOCK].T
        o_ref[:, h * ATTN_HEAD_DIM:(h + 1) * ATTN_HEAD_DIM] = oh.astype(BF16)


def _prompt_attention(qa_hm, qi_pm, wi_t, ka, va, kd, tri, batch, seq):
    nq = seq // Q_BLOCK
    nkt = seq // KEY_TILE
    m = batch * seq
    n_heads_q = ATTN_HEADS * Q_BLOCK
    in_specs = [
        pl.BlockSpec((ATTN_HEADS, Q_BLOCK, ATTN_HEAD_DIM), lambda b, q: (0, b * nq + q, 0)),
        pl.BlockSpec((IDX_HEADS // 2, Q_BLOCK, LANES), lambda b, q: (0, b * nq + q, 0)),
        pl.BlockSpec((IDX_HEADS, Q_BLOCK), lambda b, q: (0, b * nq + q)),
        pl.BlockSpec((seq, ATTN_HEAD_DIM), lambda b, q: (b, 0)),
        pl.BlockSpec((seq, ATTN_HEAD_DIM), lambda b, q: (b, 0)),
        pl.BlockSpec((seq, LANES), lambda b, q: (b, 0)),
        pl.BlockSpec((KEY_TILE, KEY_TILE), lambda b, q: (0, 0)),
    ]
    return pl.pallas_call(
        _attn_body,
        grid=(batch, nq),
        in_specs=in_specs,
        out_specs=pl.BlockSpec((Q_BLOCK, ATTN_HEADS * ATTN_HEAD_DIM), lambda b, q: (b * nq + q, 0)),
        out_shape=jax.ShapeDtypeStruct((m, ATTN_HEADS * ATTN_HEAD_DIM), BF16),
        scratch_shapes=[
            pltpu.VMEM((seq, ATTN_HEAD_DIM), BF16),
            pltpu.VMEM((nkt, ATTN_HEAD_DIM + SUM_ROWS, KEY_TILE), BF16),
            pltpu.VMEM((nkt, KEY_TILE, Q_BLOCK), F32),
            pltpu.VMEM((nkt, KEY_TILE, n_heads_q), F32),
            pltpu.VMEM((8, n_heads_q), F32),
            pltpu.VMEM((ATTN_HEAD_DIM + SUM_ROWS, n_heads_q), F32),
        ],
        compiler_params=_cparams(("arbitrary", "arbitrary")),
        name="prompt_attn",
    )(qa_hm, qi_pm, wi_t, ka, va, kd, tri)


def _gate(o, g):
    rn = o * lax.rsqrt(jnp.mean(o * o, axis=-1, keepdims=True) + EPS)
    return rn * (g / (1.0 + jnp.exp(-g)))


def _ret_body(q_ref, k_ref, v_ref, g_ref, decay_ref, rsc_ref, zeta_ref, gpow_ref,
              rg_ref, st_ref):
    c = pl.program_id(1)

    @pl.when(c == 0)
    def _init():
        st_ref[...] = jnp.zeros(st_ref.shape, F32)

    for h in range(RET_HEADS):
        sl = slice(h * 128, (h + 1) * 128)
        q = q_ref[:, sl]
        k = k_ref[:, sl]
        v = v_ref[:, sl]
        r_old = st_ref[0, h]
        qk = lax.dot_general(q, k, (((1,), (1,)), ((), ())), preferred_element_type=F32)
        inner = jnp.dot((qk * decay_ref[h]).astype(BF16), v, preferred_element_type=F32)
        cross = jnp.dot(q, r_old.astype(BF16), preferred_element_type=F32) * rsc_ref[h]
        kz = (k.astype(F32) * zeta_ref[h]).astype(BF16)
        upd = lax.dot_general(kz, v, (((0,), (0,)), ((), ())), preferred_element_type=F32)
        st_ref[0, h] = r_old * gpow_ref[h] + upd
        rg_ref[:, sl] = _gate(inner + cross, g_ref[:, sl].astype(F32)).astype(BF16)


def _prompt_retention(main, decay, rsc, zeta, gpow, batch, seq):
    nc = seq // RET_CHUNK
    m = batch * seq
    width = RET_WIDTH
    const3 = lambda b, c: (0, 0, 0)
    in_specs = [
        pl.BlockSpec((RET_CHUNK, width), lambda b, c: (b * nc + c, 0)),
        pl.BlockSpec((RET_CHUNK, width), lambda b, c: (b * nc + c, 1)),
        pl.BlockSpec((RET_CHUNK, width), lambda b, c: (b * nc + c, 2)),
        pl.BlockSpec((RET_CHUNK, width), lambda b, c: (b * nc + c, 3)),
        pl.BlockSpec((RET_HEADS, RET_CHUNK, RET_CHUNK), const3),
        pl.BlockSpec((RET_HEADS, RET_CHUNK, RET_DV), const3),
        pl.BlockSpec((RET_HEADS, RET_CHUNK, RET_DK), const3),
        pl.BlockSpec((RET_HEADS, 1, RET_DV), const3),
    ]
    return pl.pallas_call(
        _ret_body,
        grid=(batch, nc),
        in_specs=in_specs,
        out_specs=(
            pl.BlockSpec((RET_CHUNK, width), lambda b, c: (b * nc + c, 0)),
            pl.BlockSpec((1, RET_HEADS, RET_DK, RET_DV), lambda b, c: (b, 0, 0, 0)),
        ),
        out_shape=(
            jax.ShapeDtypeStruct((m, width), BF16),
            jax.ShapeDtypeStruct((batch, RET_HEADS, RET_DK, RET_DV), F32),
        ),
        compiler_params=_cparams(("arbitrary", "arbitrary")),
        name="prompt_ret",
    )(main, main, main, main, decay, rsc, zeta, gpow)


def _outproj_body(a_ref, r_ref, wa_ref, wr_ref, x_ref, g2_ref, x1_ref, h2_ref):
    mixed = (jnp.dot(a_ref[...], wa_ref[...], preferred_element_type=F32)
             + jnp.dot(r_ref[...], wr_ref[...], preferred_element_type=F32))
    x1 = x_ref[...] + mixed
    x1_ref[...] = x1
    ms = jnp.mean(x1 * x1, axis=-1, keepdims=True)
    h2_ref[...] = (x1 * lax.rsqrt(ms + EPS) * g2_ref[...]).astype(BF16)


def _out_projection(attn_o, rg, wa, wr, x2d, g2, tm):
    m = x2d.shape[0]
    half = attn_o.shape[1]
    in_specs = [
        pl.BlockSpec((tm, half), lambda i: (i, 0)),
        pl.BlockSpec((tm, half), lambda i: (i, 0)),
        pl.BlockSpec((half, D_MODEL), lambda i: (0, 0)),
        pl.BlockSpec((half, D_MODEL), lambda i: (0, 0)),
        pl.BlockSpec((tm, D_MODEL), lambda i: (i, 0)),
        pl.BlockSpec((1, D_MODEL), lambda i: (0, 0)),
    ]
    return pl.pallas_call(
        _outproj_body,
        grid=(m // tm,),
        in_specs=in_specs,
        out_specs=(pl.BlockSpec((tm, D_MODEL), lambda i: (i, 0)),
                   pl.BlockSpec((tm, D_MODEL), lambda i: (i, 0))),
        out_shape=(jax.ShapeDtypeStruct((m, D_MODEL), F32),
                   jax.ShapeDtypeStruct((m, D_MODEL), BF16)),
        compiler_params=_cparams(("arbitrary",)),
        name="out_proj",
    )(attn_o, rg, wa, wr, x2d, g2)


def _mlp_body(h2_ref, wu_ref, wd_ref, x1_ref, gf_ref, y_ref, acc_ref):
    f = pl.program_id(1)

    @pl.when(f == 0)
    def _init():
        acc_ref[...] = x1_ref[...]

    u = jnp.dot(h2_ref[...], wu_ref[...], preferred_element_type=F32)
    a = jnp.maximum(u, 0.0)
    acc_ref[...] += jnp.dot((a * a).astype(BF16), wd_ref[...], preferred_element_type=F32)

    @pl.when(f == pl.num_programs(1) - 1)
    def _final():
        x2 = acc_ref[...]
        ms = jnp.mean(x2 * x2, axis=-1, keepdims=True)
        y_ref[...] = x2 * lax.rsqrt(ms + EPS) * gf_ref[...]


def _mlp_cast_body(h2_ref, wu_ref, wd_ref, x1_ref, gf_ref, y_ref, wub_ref, wdb_ref, acc_ref):
    f = pl.program_id(0)

    @pl.when(f == 0)
    def _init():
        acc_ref[...] = x1_ref[...]

    wu = wu_ref[...].astype(BF16)
    wd = wd_ref[...].astype(BF16)
    wub_ref[...] = wu
    wdb_ref[...] = wd
    u = jnp.dot(h2_ref[...], wu, preferred_element_type=F32)
    a = jnp.maximum(u, 0.0)
    acc_ref[...] += jnp.dot((a * a).astype(BF16), wd, preferred_element_type=F32)

    @pl.when(f == pl.num_programs(0) - 1)
    def _final():
        x2 = acc_ref[...]
        ms = jnp.mean(x2 * x2, axis=-1, keepdims=True)
        y_ref[...] = x2 * lax.rsqrt(ms + EPS) * gf_ref[...]


def _mlp_and_cast(h2, w_up, w_down, x1, gf, tf):
    m = h2.shape[0]
    full = lambda f: (0, 0)
    return pl.pallas_call(
        _mlp_cast_body,
        grid=(D_FF // tf,),
        in_specs=[
            pl.BlockSpec((m, D_MODEL), full),
            pl.BlockSpec((D_MODEL, tf), lambda f: (0, f)),
            pl.BlockSpec((tf, D_MODEL), lambda f: (f, 0)),
            pl.BlockSpec((m, D_MODEL), full),
            pl.BlockSpec((1, D_MODEL), full),
        ],
        out_specs=(
            pl.BlockSpec((m, D_MODEL), full),
            pl.BlockSpec((D_MODEL, tf), lambda f: (0, f)),
            pl.BlockSpec((tf, D_MODEL), lambda f: (f, 0)),
        ),
        out_shape=(
            jax.ShapeDtypeStruct((m, D_MODEL), F32),
            jax.ShapeDtypeStruct((D_MODEL, D_FF), BF16),
            jax.ShapeDtypeStruct((D_FF, D_MODEL), BF16),
        ),
        scratch_shapes=[pltpu.VMEM((m, D_MODEL), F32)],
        compiler_params=_cparams(("arbitrary",)),
        name="mlp_cast",
    )(h2, w_up, w_down, x1, gf)


def _mlp(h2, wu, wd, x1, gf, tm, tf):
    m = h2.shape[0]
    in_specs = [
        pl.BlockSpec((tm, D_MODEL), lambda i, f: (i, 0)),
        pl.BlockSpec((D_MODEL, tf), lambda i, f: (0, f)),
        pl.BlockSpec((tf, D_MODEL), lambda i, f: (f, 0)),
        pl.BlockSpec((tm, D_MODEL), lambda i, f: (i, 0)),
        pl.BlockSpec((1, D_MODEL), lambda i, f: (0, 0)),
    ]
    return pl.pallas_call(
        _mlp_body,
        grid=(m // tm, D_FF // tf),
        in_specs=in_specs,
        out_specs=pl.BlockSpec((tm, D_MODEL), lambda i, f: (i, 0)),
        out_shape=jax.ShapeDtypeStruct((m, D_MODEL), F32),
        scratch_shapes=[pltpu.VMEM((tm, D_MODEL), F32)],
        compiler_params=_cparams(("arbitrary", "arbitrary")),
        name="mlp",
    )(h2, wu, wd, x1, gf)


def _fetch_pages(pt_ref, step, slot, streams, start):
    n_pages = pt_ref.shape[1]
    for hbm, buf, sem in streams:
        for j in range(n_pages):
            cp = pltpu.make_async_copy(hbm.at[pt_ref[step, j]], buf.at[slot, j], sem.at[slot])
            if start:
                cp.start()
            else:
                cp.wait()


def _paged_loop(pt_ref, streams, step_fn):
    nb = pt_ref.shape[0]
    n_slots = streams[0][1].shape[0]
    ahead = n_slots - 1
    for s in range(ahead):
        _fetch_pages(pt_ref, s, s, streams, start=True)

    def body(b, carry):
        @pl.when(b + ahead < nb)
        def _next():
            _fetch_pages(pt_ref, b + ahead, (b + ahead) % n_slots, streams, start=True)

        slot = b % n_slots
        _fetch_pages(pt_ref, b, slot, streams, start=False)
        step_fn(b, slot)
        return carry

    lax.fori_loop(0, nb, body, 0)


def _sidx_body(pt_ref, qi_ref, w_ref, kin_ref, cache_hbm, out_ref, kt_s, pbuf, sem):
    n_pages = pt_ref.shape[1]
    page = pbuf.shape[3]
    past = n_pages * page
    lane = lax.broadcasted_iota(I32, (1, LANES), 1)

    def step(b, slot):
        qi = qi_ref[b]
        w = w_ref[b] * (IDX_HEAD_DIM ** -0.5)
        for j in range(n_pages):
            kt_s[:, j * page:(j + 1) * page] = pbuf[slot, j].astype(BF16)
        s = jnp.dot(qi, kt_s[...], preferred_element_type=F32)
        out_ref[b, :, 0:past] = jnp.sum(jnp.maximum(s, 0.0) * w, axis=0, keepdims=True)
        sn = jnp.sum(qi.astype(F32) * kin_ref[b].astype(BF16).astype(F32), axis=1, keepdims=True)
        rn = jnp.sum(jnp.maximum(sn, 0.0) * w, axis=0, keepdims=True)
        out_ref[b, :, past:past + LANES] = jnp.where(lane == 0, rn, -jnp.inf)

    _paged_loop(pt_ref, ((cache_hbm, pbuf, sem),), step)


def _sample_index_scores(page_table, qi_s, wi_s, ki_s, cache_idx_k_t):
    nb, n_pages = page_table.shape
    page = cache_idx_k_t.shape[2]
    width = n_pages * page + LANES

    vmem = pl.BlockSpec(memory_space=pltpu.VMEM)
    return pl.pallas_call(
        _sidx_body,
        in_specs=[pl.BlockSpec(memory_space=pltpu.SMEM), vmem, vmem, vmem,
                  pl.BlockSpec(memory_space=pl.ANY)],
        out_specs=vmem,
        out_shape=jax.ShapeDtypeStruct((nb, 1, width), F32),
        scratch_shapes=[pltpu.VMEM((IDX_HEAD_DIM, n_pages * page), BF16),
                        pltpu.VMEM((PAGE_SLOTS, n_pages, IDX_HEAD_DIM, page), F32),
                        pltpu.SemaphoreType.DMA((PAGE_SLOTS,))],
        compiler_params=pltpu.CompilerParams(vmem_limit_bytes=VMEM_LIMIT),
        name="sample_idx",
    )(page_table, qi_s, wi_s, ki_s, cache_idx_k_t)


def _ssel_body(sc_ref, tri_ref, sel_ref):
    rows, width = sc_ref.shape
    nt = width // LANES
    n_valid = (nt - 1) * LANES + 1

    def tile(kt):
        return sc_ref[:, kt * LANES:(kt + 1) * LANES]

    def count_cmp(cmp):
        acc = jnp.zeros((rows, LANES), F32)
        for kt in range(nt):
            acc = acc + jnp.where(cmp(tile(kt)), 1.0, 0.0)
        return jnp.broadcast_to(jnp.sum(acc, axis=1, keepdims=True), (rows, LANES))

    tf = _threshold_search(lambda c: count_cmp(lambda sc: sc >= c), 32, (rows, LANES))
    need = float(TOPK_MAX) - count_cmp(lambda sc: sc > tf)
    tie_off = jnp.zeros((rows, LANES), F32)
    for kt in range(nt):
        col = kt * LANES + lax.broadcasted_iota(I32, (rows, LANES), 1)
        sc = tile(kt)
        eq = sc == tf
        tie = jnp.where(eq, 1.0, 0.0)
        rank = jnp.dot(tie.astype(BF16), tri_ref[...], preferred_element_type=F32) + tie_off
        sel = ((sc > tf) | (eq & (rank <= need))) & (col < n_valid)
        sel_ref[:, kt * LANES:(kt + 1) * LANES] = jnp.where(sel, 1.0, 0.0)
        tie_off = tie_off + jnp.broadcast_to(jnp.sum(tie, axis=1, keepdims=True), (rows, LANES))


def _sample_select(scores2d, tri):
    rows, width = scores2d.shape
    return pl.pallas_call(
        _ssel_body,
        out_shape=jax.ShapeDtypeStruct((rows, width), F32),
        compiler_params=pltpu.CompilerParams(vmem_limit_bytes=VMEM_LIMIT),
        name="sample_select",
    )(scores2d, tri)


def _sattn_body(pt_ref, q_ref, sel_ref, kn_ref, vn_ref, ck_hbm, cv_hbm, o_ref,
                kbuf, vbuf, ksem, vsem):
    n_pages = pt_ref.shape[1]
    page = kbuf.shape[2]
    past = n_pages * page
    scale = ATTN_HEAD_DIM ** -0.5

    def step(b, slot):
        q = q_ref[b]
        k_all = kbuf[slot].reshape(past, ATTN_HEAD_DIM).astype(BF16)
        v_all = vbuf[slot].reshape(past, ATTN_HEAD_DIM).astype(BF16)
        s = lax.dot_general(q, k_all, (((1,), (1,)), ((), ())), preferred_element_type=F32)
        s = jnp.where(sel_ref[b, :, 0:past] > 0.5, s * scale, NEG_BIG)
        kn = kn_ref[b].astype(BF16).astype(F32)
        sn = jnp.sum(q.astype(F32) * kn, axis=1, keepdims=True) * scale
        sn = jnp.where(sel_ref[b, :, past:past + 1] > 0.5, sn, NEG_BIG)
        m = jnp.maximum(jnp.max(s, axis=1, keepdims=True), sn)
        pn = jnp.exp(sn - m)
        p = jnp.exp(s - m)
        l = pn + jnp.sum(p, axis=1, keepdims=True)
        acc = (pn * vn_ref[b].astype(BF16).astype(F32)
               + jnp.dot(p.astype(BF16), v_all, preferred_element_type=F32))
        o_ref[b] = (acc / l).astype(BF16)

    _paged_loop(pt_ref, ((ck_hbm, kbuf, ksem), (cv_hbm, vbuf, vsem)), step)


def _sample_attention(page_table, qa_s, sel3, ka_s, va_s, cache_k, cache_v):
    nb, n_pages = page_table.shape
    page = cache_k.shape[1]

    vmem = pl.BlockSpec(memory_space=pltpu.VMEM)
    hbm = pl.BlockSpec(memory_space=pl.ANY)
    return pl.pallas_call(
        _sattn_body,
        in_specs=[pl.BlockSpec(memory_space=pltpu.SMEM), vmem, vmem, vmem, vmem, hbm, hbm],
        out_specs=vmem,
        out_shape=jax.ShapeDtypeStruct((nb, ATTN_HEADS, ATTN_HEAD_DIM), BF16),
        scratch_shapes=[pltpu.VMEM((PAGE_SLOTS, n_pages, page, ATTN_HEAD_DIM), F32),
                        pltpu.VMEM((PAGE_SLOTS, n_pages, page, ATTN_HEAD_DIM), F32),
                        pltpu.SemaphoreType.DMA((PAGE_SLOTS,)),
                        pltpu.SemaphoreType.DMA((PAGE_SLOTS,))],
        compiler_params=pltpu.CompilerParams(vmem_limit_bytes=VMEM_LIMIT),
        name="sample_attn",
    )(page_table, qa_s, sel3, ka_s, va_s, cache_k, cache_v)


def _sret_body(qkvg_ref, st_ref, gam_ref, rg_ref, so_ref):
    ns = st_ref.shape[0]
    for s in range(ns):
        blk = qkvg_ref[s].astype(F32)
        q8 = blk[0:8]
        k8 = blk[8:16]
        v8 = blk[16:24]
        g8 = blk[24:32]
        q_t = q8.T
        k_t = k8.T
        qk = jnp.sum(q8 * k8, axis=1, keepdims=True)
        rows = []
        for h in range(RET_HEADS):
            r_old = st_ref[s, h]
            gam = gam_ref[h]
            qcol = jnp.broadcast_to(q_t[:, h:h + 1], (RET_DK, RET_DV))
            kcol = jnp.broadcast_to(k_t[:, h:h + 1], (RET_DK, RET_DV))
            vrow = v8[h:h + 1]
            q_r = jnp.sum(qcol * r_old, axis=0, keepdims=True)
            rows.append(gam * q_r + qk[h:h + 1] * vrow)
            so_ref[s, h] = gam * r_old + kcol * vrow
        ret = jnp.concatenate(rows, axis=0)
        rg_ref[s] = _gate(ret, g8).astype(BF16)


def _sample_retention(qkvg, state, gam, ns):
    nb = state.shape[0]
    return pl.pallas_call(
        _sret_body,
        grid=(nb // ns,),
        in_specs=[
            pl.BlockSpec((ns, 32, LANES), lambda i: (i, 0, 0)),
            pl.BlockSpec((ns, RET_HEADS, RET_DK, RET_DV), lambda i: (i, 0, 0, 0)),
            pl.BlockSpec((RET_HEADS, 1, LANES), lambda i: (0, 0, 0)),
        ],
        out_specs=(
            pl.BlockSpec((ns, RET_HEADS, RET_DV), lambda i: (i, 0, 0)),
            pl.BlockSpec((ns, RET_HEADS, RET_DK, RET_DV), lambda i: (i, 0, 0, 0)),
        ),
        out_shape=(
            jax.ShapeDtypeStruct((nb, RET_HEADS, RET_DV), BF16),
            jax.ShapeDtypeStruct(state.shape, F32),
        ),
        compiler_params=_cparams(("arbitrary",)),
        name="sample_ret",
    )(qkvg, state, gam)


def _rotary_table(pos):
    half = RET_DK // 2
    inv = ROPE_BASE ** (-jnp.arange(half, dtype=F32) / half)
    ang = pos[:, None] * inv[None, :]
    cos = jnp.cos(ang)
    sin = jnp.sin(ang)
    return jnp.concatenate([cos, cos, -sin, sin], axis=1)


def _retention_constants():
    lg = jnp.log1p(-jnp.exp2(-5.0 - jnp.arange(RET_HEADS, dtype=F32)))
    n = RET_CHUNK
    i = jnp.arange(n, dtype=F32)
    diff = i[:, None] - i[None, :]
    decay = jnp.where(diff[None] >= 0,
                      jnp.exp(jnp.maximum(diff, 0.0)[None] * lg[:, None, None]), 0.0)
    rsc = jnp.exp((i + 1.0)[None, :] * lg[:, None])
    zeta = jnp.exp((n - 1.0 - i)[None, :] * lg[:, None])
    gpow = jnp.exp(n * lg)
    gam1 = jnp.exp(lg)
    rsc_b = jnp.broadcast_to(rsc[:, :, None], (RET_HEADS, n, RET_DV))
    zeta_b = jnp.broadcast_to(zeta[:, :, None], (RET_HEADS, n, RET_DK))
    gpow_b = jnp.broadcast_to(gpow[:, None, None], (RET_HEADS, 1, RET_DV))
    gam1_b = jnp.broadcast_to(gam1[:, None, None], (RET_HEADS, 1, LANES))
    return decay, rsc_b, zeta_b, gpow_b, gam1_b


def _upper_tri(n):
    r = lax.broadcasted_iota(I32, (n, n), 0)
    c = lax.broadcasted_iota(I32, (n, n), 1)
    return (r <= c).astype(BF16)


def _pad_lanes(v):
    return jnp.pad(v, (0, LANES - v.shape[0])).reshape(1, LANES)


def kernel(x_prompt, x_sample, cache_k, cache_v, cache_idx_k, state_ret, page_table,
           norm1_g, w_in, idx_k_norm_g, idx_k_norm_b, w_out, norm2_g, w_up, w_down, final_norm_g):
    batch, seq, _ = x_prompt.shape
    nb = x_sample.shape[0]
    past_len = page_table.shape[1] * cache_k.shape[1]
    half_mix = ATTN_HEADS * ATTN_HEAD_DIM

    wt_attn = w_in[:, :OFF_QR + LANES].T.astype(BF16)
    wt_ret = w_in[:, OFF_QR:].T.astype(BF16)
    wa = w_out[:half_mix].astype(BF16)
    wr = w_out[half_mix:].astype(BF16)
    g1 = norm1_g.reshape(1, D_MODEL)
    g2 = norm2_g.reshape(1, D_MODEL)
    gf = final_norm_g.reshape(1, D_MODEL)
    lng = _pad_lanes(idx_k_norm_g)
    lnb = _pad_lanes(idx_k_norm_b)
    decay, rsc_b, zeta_b, gpow_b, gam1_b = _retention_constants()

    xp = x_prompt.reshape(batch * seq, D_MODEL)
    cs_p = _rotary_table(jnp.arange(seq, dtype=F32))
    qa_p, qi_p, ka_p, va_p, ki_p, kd_p, wi_p = _project_attn(xp, g1, wt_attn, lng, lnb, tm=1024)
    main_p = _project_ret(xp, g1, wt_ret, cs_p, tm=512)
    attn_p = _prompt_attention(qa_p, qi_p, wi_p.T, ka_p, va_p, kd_p, _upper_tri(KEY_TILE).T,
                               batch, seq)
    rg_p, ret_state_p = _prompt_retention(main_p, decay, rsc_b, zeta_b, gpow_b, batch, seq)
    x1_p, h2_p = _out_projection(attn_p, rg_p, wa, wr, xp, g2, tm=512)

    assert nb >= PAGE_SLOTS
    xs = x_sample.reshape(nb, D_MODEL)
    cs_s = _rotary_table(jnp.full((nb,), past_len, dtype=F32))
    qa_s, qi_s, ka_s, va_s, ki_s, _, wi_s = _project_attn(xs, g1, wt_attn, lng, lnb, tm=nb)
    main_s = _project_ret(xs, g1, wt_ret, cs_s, tm=nb)
    scores = _sample_index_scores(
        page_table,
        qi_s.transpose(1, 0, 2).reshape(nb, IDX_HEADS, IDX_HEAD_DIM),
        wi_s.reshape(nb, IDX_HEADS, 1),
        ki_s.reshape(nb, 1, IDX_HEAD_DIM),
        jnp.swapaxes(cache_idx_k, 1, 2))
    width = scores.shape[2]
    sel = _sample_select(scores.reshape(nb, width), _upper_tri(LANES))
    attn_s = _sample_attention(
        page_table,
        qa_s.transpose(1, 0, 2),
        sel.reshape(nb, 1, width),
        ka_s.reshape(nb, 1, ATTN_HEAD_DIM),
        va_s.reshape(nb, 1, ATTN_HEAD_DIM),
        cache_k, cache_v)
    rg_s, ret_state_s = _sample_retention(main_s.reshape(nb, 32, LANES), state_ret, gam1_b, ns=8)
    x1_s, h2_s = _out_projection(attn_s.reshape(nb, half_mix), rg_s.reshape(nb, RET_WIDTH),
                                 wa, wr, xs, g2, tm=nb)
    y_s, wu, wd = _mlp_and_cast(h2_s, w_up, w_down, x1_s, gf, tf=512)
    y_p = _mlp(h2_p, wu, wd, x1_p, gf, tm=512, tf=1024)

    return (
        y_p.reshape(batch, seq, D_MODEL),
        y_s.reshape(nb, 1, D_MODEL),
        ka_p.reshape(batch, seq, ATTN_HEAD_DIM),
        va_p.reshape(batch, seq, ATTN_HEAD_DIM),
        ki_p.reshape(batch, seq, IDX_HEAD_DIM),
        ret_state_p,
        ka_s.reshape(nb, 1, ATTN_HEAD_DIM),
        va_s.reshape(nb, 1, ATTN_HEAD_DIM),
        ki_s.reshape(nb, 1, IDX_HEAD_DIM),
        ret_state_s,
    )
```

```python
import functools

import numpy as np
import jax
import jax.numpy as jnp
from jax import lax
from jax.experimental import pallas as pl
from jax.experimental.pallas import tpu as pltpu

F32 = jnp.float32
BF16 = jnp.bfloat16
I32 = jnp.int32

D_MODEL = 2048
ATTN_HEADS = 8
ATTN_HEAD_DIM = 128
IDX_HEADS = 16
IDX_HEAD_DIM = 64
TOPK_MAX = 256
RET_HEADS = 8
RET_DK = 128
RET_DV = 128
RET_CHUNK = 256
ROPE_BASE = 10000.0
D_FF = 4 * D_MODEL
EPS = 1e-6
Q_BLOCK = 256

OFF_QA, OFF_KA, OFF_VA, OFF_QI, OFF_KI, OFF_WI = 0, 1024, 1152, 1280, 2304, 2368
OFF_QR, OFF_KR, OFF_VR, OFF_GR = 2384, 3408, 4432, 5456
RET_WIDTH = RET_HEADS * RET_DV

LANES = 128
PROJ_TILE = 512
KEY_TILE = 256
COUNT_ROWS = 64
PAGE_SLOTS = 4
SUM_ROWS = 16
LOG2_E = 1.4426950408889634
INT_MIN = -2 ** 31
KEY_NEG_INF = -2 ** 31 + 0x7FFFFF
NEG_BIG = -1e30
VMEM_LIMIT = 56 * 1024 * 1024


def _cparams(sem):
    return pltpu.CompilerParams(dimension_semantics=sem, vmem_limit_bytes=VMEM_LIMIT)


def _resident(shape):
    zeros = (0,) * len(shape)
    return pl.BlockSpec(shape, lambda *_: zeros, pipeline_mode=pl.Buffered(1))


def _normed_input(x_ref, g_ref, xn_ref):
    x = x_ref[...]
    ms = jnp.mean(x * x, axis=-1, keepdims=True)
    xn_ref[...] = (x * lax.rsqrt(ms + EPS) * g_ref[...]).astype(BF16)


def _matmul_rows(xn_ref, wt_ref, r0, n):
    return lax.dot_general(xn_ref[...], wt_ref[r0:r0 + n, :], (((1,), (1,)), ((), ())),
                           preferred_element_type=F32)


def _proj_attn_body(x_ref, g_ref, wt_ref, lng_ref, lnb_ref,
                    qa_ref, qi_ref, ka_ref, va_ref, ki_ref, kd_ref, wi_ref, xn_ref):
    _normed_input(x_ref, g_ref, xn_ref)
    mm = functools.partial(_matmul_rows, xn_ref, wt_ref)
    for t in range(ATTN_HEADS * ATTN_HEAD_DIM // PROJ_TILE):
        acc = mm(OFF_QA + t * PROJ_TILE, PROJ_TILE)
        for hh in range(4):
            qa_ref[4 * t + hh] = acc[:, hh * LANES:(hh + 1) * LANES].astype(BF16)
    for t in range(IDX_HEADS * IDX_HEAD_DIM // PROJ_TILE):
        acc = mm(OFF_QI + t * PROJ_TILE, PROJ_TILE)
        for hh in range(4):
            qi_ref[4 * t + hh] = acc[:, hh * LANES:(hh + 1) * LANES].astype(BF16)
    kv = mm(OFF_KA, 2 * ATTN_HEAD_DIM)
    ka_ref[...] = kv[:, :ATTN_HEAD_DIM]
    va_ref[...] = kv[:, ATTN_HEAD_DIM:]
    kw = mm(OFF_KI, LANES)
    lane = lax.broadcasted_iota(I32, kw.shape, 1)
    is_k = lane < IDX_HEAD_DIM
    mu = jnp.sum(jnp.where(is_k, kw, 0.0), axis=-1, keepdims=True) * (1.0 / IDX_HEAD_DIM)
    d = jnp.where(is_k, kw - mu, 0.0)
    var = jnp.sum(d * d, axis=-1, keepdims=True) * (1.0 / IDX_HEAD_DIM)
    kn = d * lax.rsqrt(var + EPS) * lng_ref[...] + lnb_ref[...]
    ki_ref[...] = kn[:, :IDX_HEAD_DIM]
    kd_ref[...] = jnp.where(is_k, kn, pltpu.roll(kn, IDX_HEAD_DIM, 1)).astype(BF16)
    wi_ref[...] = kw[:, IDX_HEAD_DIM:IDX_HEAD_DIM + IDX_HEADS] * (IDX_HEADS ** -0.5)


def _proj_ret_body(x_ref, g_ref, wt_ref, cs_ref, main_ref, xn_ref):
    _normed_input(x_ref, g_ref, xn_ref)
    base = 0
    cosf = cs_ref[:, :LANES]
    sinf = cs_ref[:, LANES:]
    tiles = RET_WIDTH // PROJ_TILE
    for seg, (off, scale) in enumerate(((OFF_QR, None), (OFF_KR, RET_DK ** -0.5))):
        for t in range(tiles):
            acc = _matmul_rows(xn_ref, wt_ref, off - base + t * PROJ_TILE, PROJ_TILE)
            for hh in range(PROJ_TILE // LANES):
                xh = acc[:, hh * LANES:(hh + 1) * LANES]
                r = xh * cosf + pltpu.roll(xh, RET_DK // 2, 1) * sinf
                if scale is not None:
                    r = r * scale
                c0 = seg * RET_WIDTH + t * PROJ_TILE + hh * LANES
                main_ref[:, c0:c0 + LANES] = r.astype(BF16)
    for seg, off in ((2, OFF_VR), (3, OFF_GR)):
        for t in range(tiles):
            acc = _matmul_rows(xn_ref, wt_ref, off - base + t * PROJ_TILE, PROJ_TILE)
            c0 = seg * RET_WIDTH + t * PROJ_TILE
            main_ref[:, c0:c0 + PROJ_TILE] = acc.astype(BF16)


def _project_attn(x2d, g1, wt_attn, lng, lnb, tm):
    m = x2d.shape[0]
    row = lambda i: (i, 0)
    out_shape = (
        jax.ShapeDtypeStruct((ATTN_HEADS, m, ATTN_HEAD_DIM), BF16),
        jax.ShapeDtypeStruct((IDX_HEADS // 2, m, LANES), BF16),
        jax.ShapeDtypeStruct((m, ATTN_HEAD_DIM), F32),
        jax.ShapeDtypeStruct((m, ATTN_HEAD_DIM), F32),
        jax.ShapeDtypeStruct((m, IDX_HEAD_DIM), F32),
        jax.ShapeDtypeStruct((m, LANES), BF16),
        jax.ShapeDtypeStruct((m, IDX_HEADS), F32),
    )
    out_specs = (
        pl.BlockSpec((ATTN_HEADS, tm, ATTN_HEAD_DIM), lambda i: (0, i, 0)),
        pl.BlockSpec((IDX_HEADS // 2, tm, LANES), lambda i: (0, i, 0)),
        pl.BlockSpec((tm, ATTN_HEAD_DIM), row),
        pl.BlockSpec((tm, ATTN_HEAD_DIM), row),
        pl.BlockSpec((tm, IDX_HEAD_DIM), row),
        pl.BlockSpec((tm, LANES), row),
        pl.BlockSpec((tm, IDX_HEADS), row),
    )
    return pl.pallas_call(
        _proj_attn_body,
        grid=(m // tm,),
        in_specs=[pl.BlockSpec((tm, D_MODEL), row), _resident((1, D_MODEL)),
                  _resident((OFF_QR + LANES, D_MODEL)), _resident((1, LANES)),
                  _resident((1, LANES))],
        out_specs=out_specs,
        out_shape=out_shape,
        scratch_shapes=[pltpu.VMEM((tm, D_MODEL), BF16)],
        compiler_params=_cparams(("arbitrary",)),
        name="proj_attn",
    )(x2d, g1, wt_attn, lng, lnb)


def _project_ret(x2d, g1, wt_ret, cs, tm):
    m = x2d.shape[0]
    n_pos_blocks = cs.shape[0] // tm
    row = lambda i: (i, 0)
    return pl.pallas_call(
        _proj_ret_body,
        grid=(m // tm,),
        in_specs=[pl.BlockSpec((tm, D_MODEL), row), _resident((1, D_MODEL)),
                  _resident(wt_ret.shape),
                  pl.BlockSpec((tm, 2 * LANES), lambda i: (i % n_pos_blocks, 0))],
        out_specs=pl.BlockSpec((tm, 4 * RET_WIDTH), row),
        out_shape=jax.ShapeDtypeStruct((m, 4 * RET_WIDTH), BF16),
        scratch_shapes=[pltpu.VMEM((tm, D_MODEL), BF16)],
        compiler_params=_cparams(("arbitrary",)),
        name="proj_ret",
    )(x2d, g1, wt_ret, cs)


def _key_to_float(key):
    bits = key ^ ((key >> 31) & 0x7FFFFFFF)
    return lax.bitcast_convert_type(bits, F32)


def _threshold_search(count_ge, n_iter, shape):
    def body(it, t):
        bit = lax.shift_left(jnp.int32(1), 31 - it)
        cand = t ^ bit
        cnt = count_ge(_key_to_float(cand))
        return jnp.where(cnt >= float(TOPK_MAX), cand, t)

    t = lax.fori_loop(0, n_iter, body, jnp.full(shape, INT_MIN, I32))
    return _key_to_float(jnp.maximum(t, KEY_NEG_INF))


def _attn_body(qa_ref, qi_ref, wit_ref, ka_ref, va_ref, kd_ref, tri_ref, o_ref,
               kbf, vtb, scr, sbuf, mrun, acc_s):
    qb = pl.program_id(1)
    n_heads_q = ATTN_HEADS * Q_BLOCK
    n_pairs = IDX_HEADS // 2
    dv = ATTN_HEAD_DIM

    @pl.when(qb == 0)
    def _cast():
        kbf[...] = ka_ref[...].astype(BF16)
        for kt in range(vtb.shape[0]):
            vtb[kt, :dv] = va_ref[kt * KEY_TILE:(kt + 1) * KEY_TILE, :].T.astype(BF16)
            vtb[kt, dv:] = jnp.ones((vtb.shape[1] - dv, KEY_TILE), BF16)

    nk = ((qb + 1) * Q_BLOCK + KEY_TILE - 1) // KEY_TILE
    wt = wit_ref[...] * (IDX_HEAD_DIM ** -0.5)
    qi2 = qi_ref[...].reshape(n_pairs * Q_BLOCK, LANES)
    lo_half = lax.broadcasted_iota(I32, (KEY_TILE, LANES), 1) < IDX_HEAD_DIM
    qidx = qb * Q_BLOCK + lax.broadcasted_iota(I32, (KEY_TILE, Q_BLOCK), 1)
    kidx0 = lax.broadcasted_iota(I32, (KEY_TILE, Q_BLOCK), 0)
    contract_last = (((1,), (1,)), ((), ()))

    def idx_body(kt, carry):
        off = pl.multiple_of(kt * KEY_TILE, KEY_TILE)
        kit = kd_ref[pl.ds(off, KEY_TILE), :]
        zero = jnp.zeros_like(kit)
        s_even = lax.dot_general(jnp.where(lo_half, kit, zero), qi2, contract_last,
                                 preferred_element_type=F32)
        s_odd = lax.dot_general(jnp.where(lo_half, zero, kit), qi2, contract_last,
                                preferred_element_type=F32)
        score = jnp.zeros((KEY_TILE, Q_BLOCK), F32)
        for g in range(n_pairs):
            cs = slice(g * Q_BLOCK, (g + 1) * Q_BLOCK)
            score = score + jnp.maximum(s_even[:, cs], 0.0) * wt[2 * g:2 * g + 1, :]
            score = score + jnp.maximum(s_odd[:, cs], 0.0) * wt[2 * g + 1:2 * g + 2, :]
        scr[kt] = jnp.where(kidx0 + off <= qidx, score, -jnp.inf)
        s = lax.dot_general(kbf[pl.ds(off, KEY_TILE), :], qa2, contract_last,
                            preferred_element_type=F32)
        s = s * (ATTN_HEAD_DIM ** -0.5 * LOG2_E)
        sbuf[kt] = s
        mrun[...] = jnp.maximum(mrun[...],
                                jnp.max(s.reshape(KEY_TILE // 8, 8, n_heads_q), axis=0))
        return carry

    qa2 = qa_ref[...].reshape(n_heads_q, ATTN_HEAD_DIM)
    mrun[...] = jnp.full((8, n_heads_q), NEG_BIG, F32)
    lax.fori_loop(0, nk, idx_body, 0)

    def count_cmp(cmp):
        def body(kt, acc):
            hit = cmp(scr[kt])
            for r in range(KEY_TILE // COUNT_ROWS):
                acc = jnp.where(hit[r * COUNT_ROWS:(r + 1) * COUNT_ROWS], acc + 1.0, acc)
            return acc
        acc = lax.fori_loop(0, nk, body, jnp.zeros((COUNT_ROWS, Q_BLOCK), F32))
        return jnp.sum(acc, axis=0, keepdims=True)

    n_iter = jnp.where(qb >= TOPK_MAX // Q_BLOCK, 32, 0)
    tf = _threshold_search(lambda c: count_cmp(lambda sc: sc >= c), n_iter, (1, Q_BLOCK))
    need = float(TOPK_MAX) - count_cmp(lambda sc: sc > tf)
    excess = jnp.max(count_cmp(lambda sc: sc >= tf)) > float(TOPK_MAX)

    def softmax_sum(m, masked):
        acc_s[...] = jnp.zeros(acc_s.shape, F32)

        def body(kt, carry):
            if masked:
                p = jnp.exp2(sbuf[kt] - m).astype(BF16)
            else:
                off = pl.multiple_of(kt * KEY_TILE, KEY_TILE)
                sel = (scr[kt] >= tf) & (kidx0 + off <= qidx)
                parts = []
                for h in range(ATTN_HEADS):
                    cs = slice(h * Q_BLOCK, (h + 1) * Q_BLOCK)
                    ph = jnp.where(sel, jnp.exp2(sbuf[kt, :, cs] - m[:, cs]), 0.0)
                    parts.append(ph.astype(BF16))
                p = jnp.concatenate(parts, axis=1)
            acc_s[...] += jnp.dot(vtb[kt], p, preferred_element_type=F32)
            return carry

        lax.fori_loop(0, nk, body, 0)

    def fast_path():
        softmax_sum(jnp.max(mrun[...], axis=0, keepdims=True), masked=False)
        return (jnp.min(acc_s[dv:dv + 1, :]) > 0.0).astype(I32)

    done = lax.cond(excess, lambda: jnp.int32(0), fast_path) == 1

    def mask_tile(kt, sel):
        for h in range(ATTN_HEADS):
            cs = slice(h * Q_BLOCK, (h + 1) * Q_BLOCK)
            sh = jnp.where(sel, sbuf[kt, :, cs], NEG_BIG)
            sbuf[kt, :, cs] = sh
            mrun[:, cs] = jnp.maximum(
                mrun[:, cs], jnp.max(sh.reshape(KEY_TILE // 8, 8, Q_BLOCK), axis=0))

    def p1_plain(kt, carry):
        off = pl.multiple_of(kt * KEY_TILE, KEY_TILE)
        mask_tile(kt, (scr[kt] >= tf) & (kidx0 + off <= qidx))
        return carry

    def p1_ties(kt, tie_off):
        off = pl.multiple_of(kt * KEY_TILE, KEY_TILE)
        sc = scr[kt]
        eq = sc == tf
        tie = jnp.where(eq, 1.0, 0.0)
        rank = jnp.dot(tri_ref[...], tie.astype(BF16), preferred_element_type=F32) + tie_off
        mask_tile(kt, ((sc > tf) | (eq & (rank <= need))) & (kidx0 + off <= qidx))
        return tie_off + jnp.sum(tie, axis=0, keepdims=True)

    @pl.when(jnp.logical_not(done))
    def _exact_path():
        mrun[...] = jnp.full((8, n_heads_q), NEG_BIG, F32)

        @pl.when(excess)
        def _with_ties():
            lax.fori_loop(0, nk, p1_ties, jnp.zeros((1, Q_BLOCK), F32))

        @pl.when(jnp.logical_not(excess))
        def _without_ties():
            lax.fori_loop(0, nk, p1_plain, 0)

        softmax_sum(jnp.max(mrun[...], axis=0, keepdims=True), masked=True)

    out = acc_s[:dv, :] / acc_s[dv:dv + 1, :]
    for h in range(ATTN_HEADS):
        oh = out[:, h * Q_BLOCK:(h + 1) * Q_BLOCK].T
        o_ref[:, h * ATTN_HEAD_DIM:(h + 1) * ATTN_HEAD_DIM] = oh.astype(BF16)


def _prompt_attention(qa_hm, qi_pm, wi_t, ka, va, kd, tri, batch, seq):
    nq = seq // Q_BLOCK
    nkt = seq // KEY_TILE
    m = batch * seq
    n_heads_q = ATTN_HEADS * Q_BLOCK
    in_specs = [
        pl.BlockSpec((ATTN_HEADS, Q_BLOCK, ATTN_HEAD_DIM), lambda b, q: (0, b * nq + q, 0)),
        pl.BlockSpec((IDX_HEADS // 2, Q_BLOCK, LANES), lambda b, q: (0, b * nq + q, 0)),
        pl.BlockSpec((IDX_HEADS, Q_BLOCK), lambda b, q: (0, b * nq + q)),
        pl.BlockSpec((seq, ATTN_HEAD_DIM), lambda b, q: (b, 0)),
        pl.BlockSpec((seq, ATTN_HEAD_DIM), lambda b, q: (b, 0)),
        pl.BlockSpec((seq, LANES), lambda b, q: (b, 0)),
        pl.BlockSpec((KEY_TILE, KEY_TILE), lambda b, q: (0, 0)),
    ]
    return pl.pallas_call(
        _attn_body,
        grid=(batch, nq),
        in_specs=in_specs,
        out_specs=pl.BlockSpec((Q_BLOCK, ATTN_HEADS * ATTN_HEAD_DIM), lambda b, q: (b * nq + q, 0)),
        out_shape=jax.ShapeDtypeStruct((m, ATTN_HEADS * ATTN_HEAD_DIM), BF16),
        scratch_shapes=[
            pltpu.VMEM((seq, ATTN_HEAD_DIM), BF16),
            pltpu.VMEM((nkt, ATTN_HEAD_DIM + SUM_ROWS, KEY_TILE), BF16),
            pltpu.VMEM((nkt, KEY_TILE, Q_BLOCK), F32),
            pltpu.VMEM((nkt, KEY_TILE, n_heads_q), F32),
            pltpu.VMEM((8, n_heads_q), F32),
            pltpu.VMEM((ATTN_HEAD_DIM + SUM_ROWS, n_heads_q), F32),
        ],
        compiler_params=_cparams(("arbitrary", "arbitrary")),
        name="prompt_attn",
    )(qa_hm, qi_pm, wi_t, ka, va, kd, tri)


def _gate(o, g):
    rn = o * lax.rsqrt(jnp.mean(o * o, axis=-1, keepdims=True) + EPS)
    return rn * (g / (1.0 + jnp.exp(-g)))


def _ret_body(q_ref, k_ref, v_ref, g_ref, decay_ref, rsc_ref, zeta_ref, gpow_ref,
              rg_ref, st_ref):
    c = pl.program_id(1)

    @pl.when(c == 0)
    def _init():
        st_ref[...] = jnp.zeros(st_ref.shape, F32)

    for h in range(RET_HEADS):
        sl = slice(h * 128, (h + 1) * 128)
        q = q_ref[:, sl]
        k = k_ref[:, sl]
        v = v_ref[:, sl]
        r_old = st_ref[0, h]
        qk = lax.dot_general(q, k, (((1,), (1,)), ((), ())), preferred_element_type=F32)
        inner = jnp.dot((qk * decay_ref[h]).astype(BF16), v, preferred_element_type=F32)
        cross = jnp.dot(q, r_old.astype(BF16), preferred_element_type=F32) * rsc_ref[h]
        kz = (k.astype(F32) * zeta_ref[h]).astype(BF16)
        upd = lax.dot_general(kz, v, (((0,), (0,)), ((), ())), preferred_element_type=F32)
        st_ref[0, h] = r_old * gpow_ref[h] + upd
        rg_ref[:, sl] = _gate(inner + cross, g_ref[:, sl].astype(F32)).astype(BF16)


def _prompt_retention(main, decay, rsc, zeta, gpow, batch, seq):
    nc = seq // RET_CHUNK
    m = batch * seq
    width = RET_WIDTH
    const3 = lambda b, c: (0, 0, 0)
    in_specs = [
        pl.BlockSpec((RET_CHUNK, width), lambda b, c: (b * nc + c, 0)),
        pl.BlockSpec((RET_CHUNK, width), lambda b, c: (b * nc + c, 1)),
        pl.BlockSpec((RET_CHUNK, width), lambda b, c: (b * nc + c, 2)),
        pl.BlockSpec((RET_CHUNK, width), lambda b, c: (b * nc + c, 3)),
        pl.BlockSpec((RET_HEADS, RET_CHUNK, RET_CHUNK), const3),
        pl.BlockSpec((RET_HEADS, RET_CHUNK, RET_DV), const3),
        pl.BlockSpec((RET_HEADS, RET_CHUNK, RET_DK), const3),
        pl.BlockSpec((RET_HEADS, 1, RET_DV), const3),
    ]
    return pl.pallas_call(
        _ret_body,
        grid=(batch, nc),
        in_specs=in_specs,
        out_specs=(
            pl.BlockSpec((RET_CHUNK, width), lambda b, c: (b * nc + c, 0)),
            pl.BlockSpec((1, RET_HEADS, RET_DK, RET_DV), lambda b, c: (b, 0, 0, 0)),
        ),
        out_shape=(
            jax.ShapeDtypeStruct((m, width), BF16),
            jax.ShapeDtypeStruct((batch, RET_HEADS, RET_DK, RET_DV), F32),
        ),
        compiler_params=_cparams(("arbitrary", "arbitrary")),
        name="prompt_ret",
    )(main, main, main, main, decay, rsc, zeta, gpow)


def _outproj_body(a_ref, r_ref, wa_ref, wr_ref, x_ref, g2_ref, x1_ref, h2_ref):
    mixed = (jnp.dot(a_ref[...], wa_ref[...], preferred_element_type=F32)
             + jnp.dot(r_ref[...], wr_ref[...], preferred_element_type=F32))
    x1 = x_ref[...] + mixed
    x1_ref[...] = x1
    ms = jnp.mean(x1 * x1, axis=-1, keepdims=True)
    h2_ref[...] = (x1 * lax.rsqrt(ms + EPS) * g2_ref[...]).astype(BF16)


def _out_projection(attn_o, rg, wa, wr, x2d, g2, tm):
    m = x2d.shape[0]
    half = attn_o.shape[1]
    in_specs = [
        pl.BlockSpec((tm, half), lambda i: (i, 0)),
        pl.BlockSpec((tm, half), lambda i: (i, 0)),
        pl.BlockSpec((half, D_MODEL), lambda i: (0, 0)),
        pl.BlockSpec((half, D_MODEL), lambda i: (0, 0)),
        pl.BlockSpec((tm, D_MODEL), lambda i: (i, 0)),
        pl.BlockSpec((1, D_MODEL), lambda i: (0, 0)),
    ]
    return pl.pallas_call(
        _outproj_body,
        grid=(m // tm,),
        in_specs=in_specs,
        out_specs=(pl.BlockSpec((tm, D_MODEL), lambda i: (i, 0)),
                   pl.BlockSpec((tm, D_MODEL), lambda i: (i, 0))),
        out_shape=(jax.ShapeDtypeStruct((m, D_MODEL), F32),
                   jax.ShapeDtypeStruct((m, D_MODEL), BF16)),
        compiler_params=_cparams(("arbitrary",)),
        name="out_proj",
    )(attn_o, rg, wa, wr, x2d, g2)


def _mlp_body(h2_ref, wu_ref, wd_ref, x1_ref, gf_ref, y_ref, acc_ref):
    f = pl.program_id(1)

    @pl.when(f == 0)
    def _init():
        acc_ref[...] = x1_ref[...]

    u = jnp.dot(h2_ref[...], wu_ref[...], preferred_element_type=F32)
    a = jnp.maximum(u, 0.0)
    acc_ref[...] += jnp.dot((a * a).astype(BF16), wd_ref[...], preferred_element_type=F32)

    @pl.when(f == pl.num_programs(1) - 1)
    def _final():
        x2 = acc_ref[...]
        ms = jnp.mean(x2 * x2, axis=-1, keepdims=True)
        y_ref[...] = x2 * lax.rsqrt(ms + EPS) * gf_ref[...]


def _mlp_cast_body(h2_ref, wu_ref, wd_ref, x1_ref, gf_ref, y_ref, wub_ref, wdb_ref, acc_ref):
    f = pl.program_id(0)

    @pl.when(f == 0)
    def _init():
        acc_ref[...] = x1_ref[...]

    wu = wu_ref[...].astype(BF16)
    wd = wd_ref[...].astype(BF16)
    wub_ref[...] = wu
    wdb_ref[...] = wd
    u = jnp.dot(h2_ref[...], wu, preferred_element_type=F32)
    a = jnp.maximum(u, 0.0)
    acc_ref[...] += jnp.dot((a * a).astype(BF16), wd, preferred_element_type=F32)

    @pl.when(f == pl.num_programs(0) - 1)
    def _final():
        x2 = acc_ref[...]
        ms = jnp.mean(x2 * x2, axis=-1, keepdims=True)
        y_ref[...] = x2 * lax.rsqrt(ms + EPS) * gf_ref[...]


def _mlp_and_cast(h2, w_up, w_down, x1, gf, tf):
    m = h2.shape[0]
    full = lambda f: (0, 0)
    return pl.pallas_call(
        _mlp_cast_body,
        grid=(D_FF // tf,),
        in_specs=[
            pl.BlockSpec((m, D_MODEL), full),
            pl.BlockSpec((D_MODEL, tf), lambda f: (0, f)),
            pl.BlockSpec((tf, D_MODEL), lambda f: (f, 0)),
            pl.BlockSpec((m, D_MODEL), full),
            pl.BlockSpec((1, D_MODEL), full),
        ],
        out_specs=(
            pl.BlockSpec((m, D_MODEL), full),
            pl.BlockSpec((D_MODEL, tf), lambda f: (0, f)),
            pl.BlockSpec((tf, D_MODEL), lambda f: (f, 0)),
        ),
        out_shape=(
            jax.ShapeDtypeStruct((m, D_MODEL), F32),
            jax.ShapeDtypeStruct((D_MODEL, D_FF), BF16),
            jax.ShapeDtypeStruct((D_FF, D_MODEL), BF16),
        ),
        scratch_shapes=[pltpu.VMEM((m, D_MODEL), F32)],
        compiler_params=_cparams(("arbitrary",)),
        name="mlp_cast",
    )(h2, w_up, w_down, x1, gf)


def _mlp(h2, wu, wd, x1, gf, tm, tf):
    m = h2.shape[0]
    in_specs = [
        pl.BlockSpec((tm, D_MODEL), lambda i, f: (i, 0)),
        pl.BlockSpec((D_MODEL, tf), lambda i, f: (0, f)),
        pl.BlockSpec((tf, D_MODEL), lambda i, f: (f, 0)),
        pl.BlockSpec((tm, D_MODEL), lambda i, f: (i, 0)),
        pl.BlockSpec((1, D_MODEL), lambda i, f: (0, 0)),
    ]
    return pl.pallas_call(
        _mlp_body,
        grid=(m // tm, D_FF // tf),
        in_specs=in_specs,
        out_specs=pl.BlockSpec((tm, D_MODEL), lambda i, f: (i, 0)),
        out_shape=jax.ShapeDtypeStruct((m, D_MODEL), F32),
        scratch_shapes=[pltpu.VMEM((tm, D_MODEL), F32)],
        compiler_params=_cparams(("arbitrary", "arbitrary")),
        name="mlp",
    )(h2, wu, wd, x1, gf)


def _fetch_pages(pt_ref, step, slot, streams, start):
    n_pages = pt_ref.shape[1]
    for hbm, buf, sem in streams:
        for j in range(n_pages):
            cp = pltpu.make_async_copy(hbm.at[pt_ref[step, j]], buf.at[slot, j], sem.at[slot])
            if start:
                cp.start()
            else:
                cp.wait()


def _paged_loop(pt_ref, streams, step_fn):
    nb = pt_ref.shape[0]
    n_slots = streams[0][1].shape[0]
    ahead = n_slots - 1
    for s in range(ahead):
        _fetch_pages(pt_ref, s, s, streams, start=True)

    def body(b, carry):
        @pl.when(b + ahead < nb)
        def _next():
            _fetch_pages(pt_ref, b + ahead, (b + ahead) % n_slots, streams, start=True)

        slot = b % n_slots
        _fetch_pages(pt_ref, b, slot, streams, start=False)
        step_fn(b, slot)
        return carry

    lax.fori_loop(0, nb, body, 0)


def _sidx_body(pt_ref, qi_ref, w_ref, kin_ref, cache_hbm, out_ref, kt_s, pbuf, sem):
    n_pages = pt_ref.shape[1]
    page = pbuf.shape[3]
    past = n_pages * page
    lane = lax.broadcasted_iota(I32, (1, LANES), 1)

    def step(b, slot):
        qi = qi_ref[b]
        w = w_ref[b] * (IDX_HEAD_DIM ** -0.5)
        for j in range(n_pages):
            kt_s[:, j * page:(j + 1) * page] = pbuf[slot, j].astype(BF16)
        s = jnp.dot(qi, kt_s[...], preferred_element_type=F32)
        out_ref[b, :, 0:past] = jnp.sum(jnp.maximum(s, 0.0) * w, axis=0, keepdims=True)
        sn = jnp.sum(qi.astype(F32) * kin_ref[b].astype(BF16).astype(F32), axis=1, keepdims=True)
        rn = jnp.sum(jnp.maximum(sn, 0.0) * w, axis=0, keepdims=True)
        out_ref[b, :, past:past + LANES] = jnp.where(lane == 0, rn, -jnp.inf)

    _paged_loop(pt_ref, ((cache_hbm, pbuf, sem),), step)


def _sample_index_scores(page_table, qi_s, wi_s, ki_s, cache_idx_k_t):
    nb, n_pages = page_table.shape
    page = cache_idx_k_t.shape[2]
    width = n_pages * page + LANES

    vmem = pl.BlockSpec(memory_space=pltpu.VMEM)
    return pl.pallas_call(
        _sidx_body,
        in_specs=[pl.BlockSpec(memory_space=pltpu.SMEM), vmem, vmem, vmem,
                  pl.BlockSpec(memory_space=pl.ANY)],
        out_specs=vmem,
        out_shape=jax.ShapeDtypeStruct((nb, 1, width), F32),
        scratch_shapes=[pltpu.VMEM((IDX_HEAD_DIM, n_pages * page), BF16),
                        pltpu.VMEM((PAGE_SLOTS, n_pages, IDX_HEAD_DIM, page), F32),
                        pltpu.SemaphoreType.DMA((PAGE_SLOTS,))],
        compiler_params=pltpu.CompilerParams(vmem_limit_bytes=VMEM_LIMIT),
        name="sample_idx",
    )(page_table, qi_s, wi_s, ki_s, cache_idx_k_t)


def _ssel_body(sc_ref, tri_ref, sel_ref):
    rows, width = sc_ref.shape
    nt = width // LANES
    n_valid = (nt - 1) * LANES + 1

    def tile(kt):
        return sc_ref[:, kt * LANES:(kt + 1) * LANES]

    def count_cmp(cmp):
        acc = jnp.zeros((rows, LANES), F32)
        for kt in range(nt):
            acc = acc + jnp.where(cmp(tile(kt)), 1.0, 0.0)
        return jnp.broadcast_to(jnp.sum(acc, axis=1, keepdims=True), (rows, LANES))

    tf = _threshold_search(lambda c: count_cmp(lambda sc: sc >= c), 32, (rows, LANES))
    need = float(TOPK_MAX) - count_cmp(lambda sc: sc > tf)
    tie_off = jnp.zeros((rows, LANES), F32)
    for kt in range(nt):
        col = kt * LANES + lax.broadcasted_iota(I32, (rows, LANES), 1)
        sc = tile(kt)
        eq = sc == tf
        tie = jnp.where(eq, 1.0, 0.0)
        rank = jnp.dot(tie.astype(BF16), tri_ref[...], preferred_element_type=F32) + tie_off
        sel = ((sc > tf) | (eq & (rank <= need))) & (col < n_valid)
        sel_ref[:, kt * LANES:(kt + 1) * LANES] = jnp.where(sel, 1.0, 0.0)
        tie_off = tie_off + jnp.broadcast_to(jnp.sum(tie, axis=1, keepdims=True), (rows, LANES))


def _sample_select(scores2d, tri):
    rows, width = scores2d.shape
    return pl.pallas_call(
        _ssel_body,
        out_shape=jax.ShapeDtypeStruct((rows, width), F32),
        compiler_params=pltpu.CompilerParams(vmem_limit_bytes=VMEM_LIMIT),
        name="sample_select",
    )(scores2d, tri)


def _sattn_body(pt_ref, q_ref, sel_ref, kn_ref, vn_ref, ck_hbm, cv_hbm, o_ref,
                kbuf, vbuf, ksem, vsem):
    n_pages = pt_ref.shape[1]
    page = kbuf.shape[2]
    past = n_pages * page
    scale = ATTN_HEAD_DIM ** -0.5

    def step(b, slot):
        q = q_ref[b]
        k_all = kbuf[slot].reshape(past, ATTN_HEAD_DIM).astype(BF16)
        v_all = vbuf[slot].reshape(past, ATTN_HEAD_DIM).astype(BF16)
        s = lax.dot_general(q, k_all, (((1,), (1,)), ((), ())), preferred_element_type=F32)
        s = jnp.where(sel_ref[b, :, 0:past] > 0.5, s * scale, NEG_BIG)
        kn = kn_ref[b].astype(BF16).astype(F32)
        sn = jnp.sum(q.astype(F32) * kn, axis=1, keepdims=True) * scale
        sn = jnp.where(sel_ref[b, :, past:past + 1] > 0.5, sn, NEG_BIG)
        m = jnp.maximum(jnp.max(s, axis=1, keepdims=True), sn)
        pn = jnp.exp(sn - m)
        p = jnp.exp(s - m)
        l = pn + jnp.sum(p, axis=1, keepdims=True)
        acc = (pn * vn_ref[b].astype(BF16).astype(F32)
               + jnp.dot(p.astype(BF16), v_all, preferred_element_type=F32))
        o_ref[b] = (acc / l).astype(BF16)

    _paged_loop(pt_ref, ((ck_hbm, kbuf, ksem), (cv_hbm, vbuf, vsem)), step)


def _sample_attention(page_table, qa_s, sel3, ka_s, va_s, cache_k, cache_v):
    nb, n_pages = page_table.shape
    page = cache_k.shape[1]

    vmem = pl.BlockSpec(memory_space=pltpu.VMEM)
    hbm = pl.BlockSpec(memory_space=pl.ANY)
    return pl.pallas_call(
        _sattn_body,
        in_specs=[pl.BlockSpec(memory_space=pltpu.SMEM), vmem, vmem, vmem, vmem, hbm, hbm],
        out_specs=vmem,
        out_shape=jax.ShapeDtypeStruct((nb, ATTN_HEADS, ATTN_HEAD_DIM), BF16),
        scratch_shapes=[pltpu.VMEM((PAGE_SLOTS, n_pages, page, ATTN_HEAD_DIM), F32),
                        pltpu.VMEM((PAGE_SLOTS, n_pages, page, ATTN_HEAD_DIM), F32),
                        pltpu.SemaphoreType.DMA((PAGE_SLOTS,)),
                        pltpu.SemaphoreType.DMA((PAGE_SLOTS,))],
        compiler_params=pltpu.CompilerParams(vmem_limit_bytes=VMEM_LIMIT),
        name="sample_attn",
    )(page_table, qa_s, sel3, ka_s, va_s, cache_k, cache_v)


def _sret_body(qkvg_ref, st_ref, gam_ref, rg_ref, so_ref):
    ns = st_ref.shape[0]
    for s in range(ns):
        blk = qkvg_ref[s].astype(F32)
        q8 = blk[0:8]
        k8 = blk[8:16]
        v8 = blk[16:24]
        g8 = blk[24:32]
        q_t = q8.T
        k_t = k8.T
        qk = jnp.sum(q8 * k8, axis=1, keepdims=True)
        rows = []
        for h in range(RET_HEADS):
            r_old = st_ref[s, h]
            gam = gam_ref[h]
            qcol = jnp.broadcast_to(q_t[:, h:h + 1], (RET_DK, RET_DV))
            kcol = jnp.broadcast_to(k_t[:, h:h + 1], (RET_DK, RET_DV))
            vrow = v8[h:h + 1]
            q_r = jnp.sum(qcol * r_old, axis=0, keepdims=True)
            rows.append(gam * q_r + qk[h:h + 1] * vrow)
            so_ref[s, h] = gam * r_old + kcol * vrow
        ret = jnp.concatenate(rows, axis=0)
        rg_ref[s] = _gate(ret, g8).astype(BF16)


def _sample_retention(qkvg, state, gam, ns):
    nb = state.shape[0]
    return pl.pallas_call(
        _sret_body,
        grid=(nb // ns,),
        in_specs=[
            pl.BlockSpec((ns, 32, LANES), lambda i: (i, 0, 0)),
            pl.BlockSpec((ns, RET_HEADS, RET_DK, RET_DV), lambda i: (i, 0, 0, 0)),
            pl.BlockSpec((RET_HEADS, 1, LANES), lambda i: (0, 0, 0)),
        ],
        out_specs=(
            pl.BlockSpec((ns, RET_HEADS, RET_DV), lambda i: (i, 0, 0)),
            pl.BlockSpec((ns, RET_HEADS, RET_DK, RET_DV), lambda i: (i, 0, 0, 0)),
        ),
        out_shape=(
            jax.ShapeDtypeStruct((nb, RET_HEADS, RET_DV), BF16),
            jax.ShapeDtypeStruct(state.shape, F32),
        ),
        compiler_params=_cparams(("arbitrary",)),
        name="sample_ret",
    )(qkvg, state, gam)


def _rotary_table(pos):
    half = RET_DK // 2
    inv = ROPE_BASE ** (-jnp.arange(half, dtype=F32) / half)
    ang = pos[:, None] * inv[None, :]
    cos = jnp.cos(ang)
    sin = jnp.sin(ang)
    return jnp.concatenate([cos, cos, -sin, sin], axis=1)


def _retention_constants():
    lg = jnp.log1p(-jnp.exp2(-5.0 - jnp.arange(RET_HEADS, dtype=F32)))
    n = RET_CHUNK
    i = jnp.arange(n, dtype=F32)
    diff = i[:, None] - i[None, :]
    decay = jnp.where(diff[None] >= 0,
                      jnp.exp(jnp.maximum(diff, 0.0)[None] * lg[:, None, None]), 0.0)
    rsc = jnp.exp((i + 1.0)[None, :] * lg[:, None])
    zeta = jnp.exp((n - 1.0 - i)[None, :] * lg[:, None])
    gpow = jnp.exp(n * lg)
    gam1 = jnp.exp(lg)
    rsc_b = jnp.broadcast_to(rsc[:, :, None], (RET_HEADS, n, RET_DV))
    zeta_b = jnp.broadcast_to(zeta[:, :, None], (RET_HEADS, n, RET_DK))
    gpow_b = jnp.broadcast_to(gpow[:, None, None], (RET_HEADS, 1, RET_DV))
    gam1_b = jnp.broadcast_to(gam1[:, None, None], (RET_HEADS, 1, LANES))
    return decay, rsc_b, zeta_b, gpow_b, gam1_b


def _upper_tri(n):
    r = lax.broadcasted_iota(I32, (n, n), 0)
    c = lax.broadcasted_iota(I32, (n, n), 1)
    return (r <= c).astype(BF16)


def _pad_lanes(v):
    return jnp.pad(v, (0, LANES - v.shape[0])).reshape(1, LANES)


def kernel(x_prompt, x_sample, cache_k, cache_v, cache_idx_k, state_ret, page_table,
           norm1_g, w_in, idx_k_norm_g, idx_k_norm_b, w_out, norm2_g, w_up, w_down, final_norm_g):
    batch, seq, _ = x_prompt.shape
    nb = x_sample.shape[0]
    past_len = page_table.shape[1] * cache_k.shape[1]
    half_mix = ATTN_HEADS * ATTN_HEAD_DIM

    wt = w_in.T.astype(BF16)
    wa = w_out[:half_mix].astype(BF16)
    wr = w_out[half_mix:].astype(BF16)
    g1 = norm1_g.reshape(1, D_MODEL)
    g2 = norm2_g.reshape(1, D_MODEL)
    gf = final_norm_g.reshape(1, D_MODEL)
    lng = _pad_lanes(idx_k_norm_g)
    lnb = _pad_lanes(idx_k_norm_b)
    decay, rsc_b, zeta_b, gpow_b, gam1_b = _retention_constants()

    xp = x_prompt.reshape(batch * seq, D_MODEL)
    cs_p = _rotary_table(jnp.arange(seq, dtype=F32))
    qa_p, qi_p, ka_p, va_p, ki_p, kd_p, wi_p = _project_attn(xp, g1, wt, lng, lnb, tm=1024)
    main_p = _project_ret(xp, g1, wt, cs_p, tm=512)
    attn_p = _prompt_attention(qa_p, qi_p, wi_p.T, ka_p, va_p, kd_p, _upper_tri(KEY_TILE).T,
                               batch, seq)
    rg_p, ret_state_p = _prompt_retention(main_p, decay, rsc_b, zeta_b, gpow_b, batch, seq)
    x1_p, h2_p = _out_projection(attn_p, rg_p, wa, wr, xp, g2, tm=512)

    assert nb >= PAGE_SLOTS
    xs = x_sample.reshape(nb, D_MODEL)
    cs_s = _rotary_table(jnp.full((nb,), past_len, dtype=F32))
    qa_s, qi_s, ka_s, va_s, ki_s, _, wi_s = _project_attn(xs, g1, wt, lng, lnb, tm=nb)
    main_s = _project_ret(xs, g1, wt, cs_s, tm=nb)
    scores = _sample_index_scores(
        page_table,
        qi_s.transpose(1, 0, 2).reshape(nb, IDX_HEADS, IDX_HEAD_DIM),
        wi_s.reshape(nb, IDX_HEADS, 1),
        ki_s.reshape(nb, 1, IDX_HEAD_DIM),
        jnp.swapaxes(cache_idx_k, 1, 2))
    width = scores.shape[2]
    sel = _sample_select(scores.reshape(nb, width), _upper_tri(LANES))
    attn_s = _sample_attention(
        page_table,
        qa_s.transpose(1, 0, 2),
        sel.reshape(nb, 1, width),
        ka_s.reshape(nb, 1, ATTN_HEAD_DIM),
        va_s.reshape(nb, 1, ATTN_HEAD_DIM),
        cache_k, cache_v)
    rg_s, ret_state_s = _sample_retention(main_s.reshape(nb, 32, LANES), state_ret, gam1_b, ns=8)
    x1_s, h2_s = _out_projection(attn_s.reshape(nb, half_mix), rg_s.reshape(nb, RET_WIDTH),
                                 wa, wr, xs, g2, tm=nb)
    y_s, wu, wd = _mlp_and_cast(h2_s, w_up, w_down, x1_s, gf, tf=512)
    y_p = _mlp(h2_p, wu, wd, x1_p, gf, tm=512, tf=1024)

    return (
        y_p.reshape(batch, seq, D_MODEL),
        y_s.reshape(nb, 1, D_MODEL),
        ka_p.reshape(batch, seq, ATTN_HEAD_DIM),
        va_p.reshape(batch, seq, ATTN_HEAD_DIM),
        ki_p.reshape(batch, seq, IDX_HEAD_DIM),
        ret_state_p,
        ka_s.reshape(nb, 1, ATTN_HEAD_DIM),
        va_s.reshape(nb, 1, ATTN_HEAD_DIM),
        ki_s.reshape(nb, 1, IDX_HEAD_DIM),
        ret_state_s,
    )
```

```python
import functools

import numpy as np
import jax
import jax.numpy as jnp
from jax import lax
from jax.experimental import pallas as pl
from jax.experimental.pallas import tpu as pltpu

F32 = jnp.float32
BF16 = jnp.bfloat16
I32 = jnp.int32

D_MODEL = 2048
ATTN_HEADS = 8
ATTN_HEAD_DIM = 128
IDX_HEADS = 16
IDX_HEAD_DIM = 64
TOPK_MAX = 256
RET_HEADS = 8
RET_DK = 128
RET_DV = 128
RET_CHUNK = 256
ROPE_BASE = 10000.0
D_FF = 4 * D_MODEL
EPS = 1e-6
Q_BLOCK = 256

OFF_QA, OFF_KA, OFF_VA, OFF_QI, OFF_KI, OFF_WI = 0, 1024, 1152, 1280, 2304, 2368
OFF_QR, OFF_KR, OFF_VR, OFF_GR = 2384, 3408, 4432, 5456
RET_WIDTH = RET_HEADS * RET_DV

LANES = 128
PROJ_TILE = 512
KEY_TILE = 256
COUNT_ROWS = 64
PAGE_SLOTS = 4
SUM_ROWS = 16
LOG2_E = 1.4426950408889634
INT_MIN = -2 ** 31
KEY_NEG_INF = -2 ** 31 + 0x7FFFFF
NEG_BIG = -1e30
VMEM_LIMIT = 56 * 1024 * 1024

TILES = {
    "proj_attn_rows": 1024,
    "proj_ret_rows": 512,
    "out_proj_rows": 512,
    "mlp_rows": 512,
    "mlp_ff": 1024,
    "mlp_cast_ff": 512,
    "sample_ret_rows": 8,
}


def _cparams(sem):
    return pltpu.CompilerParams(dimension_semantics=sem, vmem_limit_bytes=VMEM_LIMIT)


def _resident(shape):
    zeros = (0,) * len(shape)
    return pl.BlockSpec(shape, lambda *_: zeros, pipeline_mode=pl.Buffered(1))


def _normed_input(x_ref, g_ref, xn_ref):
    x = x_ref[...]
    ms = jnp.mean(x * x, axis=-1, keepdims=True)
    xn_ref[...] = (x * lax.rsqrt(ms + EPS) * g_ref[...]).astype(BF16)


def _matmul_rows(xn_ref, wt_ref, r0, n):
    return lax.dot_general(xn_ref[...], wt_ref[r0:r0 + n, :], (((1,), (1,)), ((), ())),
                           preferred_element_type=F32)


def _proj_attn_body(x_ref, g_ref, wt_ref, lng_ref, lnb_ref,
                    qa_ref, qi_ref, ka_ref, va_ref, ki_ref, kd_ref, wi_ref, xn_ref):
    _normed_input(x_ref, g_ref, xn_ref)
    mm = functools.partial(_matmul_rows, xn_ref, wt_ref)
    for t in range(ATTN_HEADS * ATTN_HEAD_DIM // PROJ_TILE):
        acc = mm(OFF_QA + t * PROJ_TILE, PROJ_TILE)
        for hh in range(4):
            qa_ref[4 * t + hh] = acc[:, hh * LANES:(hh + 1) * LANES].astype(BF16)
    for t in range(IDX_HEADS * IDX_HEAD_DIM // PROJ_TILE):
        acc = mm(OFF_QI + t * PROJ_TILE, PROJ_TILE)
        for hh in range(4):
            qi_ref[4 * t + hh] = acc[:, hh * LANES:(hh + 1) * LANES].astype(BF16)
    kv = mm(OFF_KA, 2 * ATTN_HEAD_DIM)
    ka_ref[...] = kv[:, :ATTN_HEAD_DIM]
    va_ref[...] = kv[:, ATTN_HEAD_DIM:]
    kw = mm(OFF_KI, LANES)
    lane = lax.broadcasted_iota(I32, kw.shape, 1)
    is_k = lane < IDX_HEAD_DIM
    mu = jnp.sum(jnp.where(is_k, kw, 0.0), axis=-1, keepdims=True) * (1.0 / IDX_HEAD_DIM)
    d = jnp.where(is_k, kw - mu, 0.0)
    var = jnp.sum(d * d, axis=-1, keepdims=True) * (1.0 / IDX_HEAD_DIM)
    kn = d * lax.rsqrt(var + EPS) * lng_ref[...] + lnb_ref[...]
    ki_ref[...] = kn[:, :IDX_HEAD_DIM]
    kd_ref[...] = jnp.where(is_k, kn, pltpu.roll(kn, IDX_HEAD_DIM, 1)).astype(BF16)
    wi_ref[...] = (kw[:, IDX_HEAD_DIM:IDX_HEAD_DIM + IDX_HEADS] * (IDX_HEADS ** -0.5)).T


def _proj_ret_body(x_ref, g_ref, wt_ref, cs_ref, main_ref, xn_ref):
    _normed_input(x_ref, g_ref, xn_ref)
    base = 0
    cosf = cs_ref[:, :LANES]
    sinf = cs_ref[:, LANES:]
    tiles = RET_WIDTH // PROJ_TILE
    for seg, (off, scale) in enumerate(((OFF_QR, None), (OFF_KR, RET_DK ** -0.5))):
        for t in range(tiles):
            acc = _matmul_rows(xn_ref, wt_ref, off - base + t * PROJ_TILE, PROJ_TILE)
            for hh in range(PROJ_TILE // LANES):
                xh = acc[:, hh * LANES:(hh + 1) * LANES]
                r = xh * cosf + pltpu.roll(xh, RET_DK // 2, 1) * sinf
                if scale is not None:
                    r = r * scale
                c0 = seg * RET_WIDTH + t * PROJ_TILE + hh * LANES
                main_ref[:, c0:c0 + LANES] = r.astype(BF16)
    for seg, off in ((2, OFF_VR), (3, OFF_GR)):
        for t in range(tiles):
            acc = _matmul_rows(xn_ref, wt_ref, off - base + t * PROJ_TILE, PROJ_TILE)
            c0 = seg * RET_WIDTH + t * PROJ_TILE
            main_ref[:, c0:c0 + PROJ_TILE] = acc.astype(BF16)


def _project_attn(x2d, g1, wt_attn, lng, lnb, tm):
    m = x2d.shape[0]
    row = lambda i: (i, 0)
    out_shape = (
        jax.ShapeDtypeStruct((ATTN_HEADS, m, ATTN_HEAD_DIM), BF16),
        jax.ShapeDtypeStruct((IDX_HEADS // 2, m, LANES), BF16),
        jax.ShapeDtypeStruct((m, ATTN_HEAD_DIM), F32),
        jax.ShapeDtypeStruct((m, ATTN_HEAD_DIM), F32),
        jax.ShapeDtypeStruct((m, IDX_HEAD_DIM), F32),
        jax.ShapeDtypeStruct((m, LANES), BF16),
        jax.ShapeDtypeStruct((IDX_HEADS, m), F32),
    )
    out_specs = (
        pl.BlockSpec((ATTN_HEADS, tm, ATTN_HEAD_DIM), lambda i: (0, i, 0)),
        pl.BlockSpec((IDX_HEADS // 2, tm, LANES), lambda i: (0, i, 0)),
        pl.BlockSpec((tm, ATTN_HEAD_DIM), row),
        pl.BlockSpec((tm, ATTN_HEAD_DIM), row),
        pl.BlockSpec((tm, IDX_HEAD_DIM), row),
        pl.BlockSpec((tm, LANES), row),
        pl.BlockSpec((IDX_HEADS, tm), lambda i: (0, i)),
    )
    return pl.pallas_call(
        _proj_attn_body,
        grid=(m // tm,),
        in_specs=[pl.BlockSpec((tm, D_MODEL), row), _resident((1, D_MODEL)),
                  _resident((OFF_QR + LANES, D_MODEL)), _resident((1, LANES)),
                  _resident((1, LANES))],
        out_specs=out_specs,
        out_shape=out_shape,
        scratch_shapes=[pltpu.VMEM((tm, D_MODEL), BF16)],
        compiler_params=_cparams(("arbitrary",)),
        name="proj_attn",
    )(x2d, g1, wt_attn, lng, lnb)


def _project_ret(x2d, g1, wt_ret, cs, tm):
    m = x2d.shape[0]
    n_pos_blocks = cs.shape[0] // tm
    row = lambda i: (i, 0)
    return pl.pallas_call(
        _proj_ret_body,
        grid=(m // tm,),
        in_specs=[pl.BlockSpec((tm, D_MODEL), row), _resident((1, D_MODEL)),
                  _resident(wt_ret.shape),
                  pl.BlockSpec((tm, 2 * LANES), lambda i: (i % n_pos_blocks, 0))],
        out_specs=pl.BlockSpec((tm, 4 * RET_WIDTH), row),
        out_shape=jax.ShapeDtypeStruct((m, 4 * RET_WIDTH), BF16),
        scratch_shapes=[pltpu.VMEM((tm, D_MODEL), BF16)],
        compiler_params=_cparams(("arbitrary",)),
        name="proj_ret",
    )(x2d, g1, wt_ret, cs)


def _key_to_float(key):
    bits = key ^ ((key >> 31) & 0x7FFFFFFF)
    return lax.bitcast_convert_type(bits, F32)


def _threshold_search(count_ge, n_iter, shape):
    def body(it, t):
        bit = lax.shift_left(jnp.int32(1), 31 - it)
        cand = t ^ bit
        cnt = count_ge(_key_to_float(cand))
        return jnp.where(cnt >= float(TOPK_MAX), cand, t)

    t = lax.fori_loop(0, n_iter, body, jnp.full(shape, INT_MIN, I32))
    return _key_to_float(jnp.maximum(t, KEY_NEG_INF))


def _attn_body(qa_ref, qi_ref, wit_ref, ka_ref, va_ref, kd_ref, tri_ref, o_ref,
               kbf, vtb, scr, sbuf, mrun, acc_s, kmax):
    qb = pl.program_id(1)
    n_heads_q = ATTN_HEADS * Q_BLOCK
    n_pairs = IDX_HEADS // 2
    dv = ATTN_HEAD_DIM
    logit_scale = ATTN_HEAD_DIM ** -0.5 * LOG2_E

    @pl.when(qb == 0)
    def _cast():
        ka = ka_ref[...]
        kbf[...] = ka.astype(BF16)
        kmax[...] = jnp.broadcast_to(jnp.max(jnp.sum(ka * ka, axis=1, keepdims=True)), kmax.shape)
        for kt in range(vtb.shape[0]):
            vtb[kt, :dv] = va_ref[kt * KEY_TILE:(kt + 1) * KEY_TILE, :].T.astype(BF16)
            vtb[kt, dv:] = jnp.ones((vtb.shape[1] - dv, KEY_TILE), BF16)

    nk = ((qb + 1) * Q_BLOCK + KEY_TILE - 1) // KEY_TILE
    wt = wit_ref[...] * (IDX_HEAD_DIM ** -0.5)
    qi2 = qi_ref[...].reshape(n_pairs * Q_BLOCK, LANES)
    lo_half = lax.broadcasted_iota(I32, (KEY_TILE, LANES), 1) < IDX_HEAD_DIM
    qidx = qb * Q_BLOCK + lax.broadcasted_iota(I32, (KEY_TILE, Q_BLOCK), 1)
    kidx0 = lax.broadcasted_iota(I32, (KEY_TILE, Q_BLOCK), 0)
    contract_last = (((1,), (1,)), ((), ()))

    def idx_body(kt, carry):
        off = pl.multiple_of(kt * KEY_TILE, KEY_TILE)
        kit = kd_ref[pl.ds(off, KEY_TILE), :]
        zero = jnp.zeros_like(kit)
        s_even = lax.dot_general(jnp.where(lo_half, kit, zero), qi2, contract_last,
                                 preferred_element_type=F32)
        s_odd = lax.dot_general(jnp.where(lo_half, zero, kit), qi2, contract_last,
                                preferred_element_type=F32)
        score = jnp.zeros((KEY_TILE, Q_BLOCK), F32)
        for g in range(n_pairs):
            cs = slice(g * Q_BLOCK, (g + 1) * Q_BLOCK)
            score = score + jnp.maximum(s_even[:, cs], 0.0) * wt[2 * g:2 * g + 1, :]
            score = score + jnp.maximum(s_odd[:, cs], 0.0) * wt[2 * g + 1:2 * g + 2, :]
        scr[kt] = jnp.where(kidx0 + off <= qidx, score, -jnp.inf)
        return carry

    lax.fori_loop(0, nk, idx_body, 0)

    def count_cmp(cmp):
        def body(kt, acc):
            hit = cmp(scr[kt])
            for r in range(KEY_TILE // COUNT_ROWS):
                acc = jnp.where(hit[r * COUNT_ROWS:(r + 1) * COUNT_ROWS], acc + 1.0, acc)
            return acc
        acc = lax.fori_loop(0, nk, body, jnp.zeros((COUNT_ROWS, Q_BLOCK), F32))
        return jnp.sum(acc, axis=0, keepdims=True)

    n_iter = jnp.where(qb >= TOPK_MAX // Q_BLOCK, 32, 0)
    tf = _threshold_search(lambda c: count_cmp(lambda sc: sc >= c), n_iter, (1, Q_BLOCK))
    need = float(TOPK_MAX) - count_cmp(lambda sc: sc > tf)
    excess = jnp.max(count_cmp(lambda sc: sc >= tf)) > float(TOPK_MAX)

    qa2 = qa_ref[...].reshape(n_heads_q, ATTN_HEAD_DIM)

    def logits(kt):
        off = pl.multiple_of(kt * KEY_TILE, KEY_TILE)
        s = lax.dot_general(kbf[pl.ds(off, KEY_TILE), :], qa2, contract_last,
                            preferred_element_type=F32)
        return s * logit_scale

    def softmax_sum(m, masked):
        acc_s[...] = jnp.zeros(acc_s.shape, F32)

        def body_masked(kt, carry):
            p = jnp.exp2(sbuf[kt] - m).astype(BF16)
            acc_s[...] += jnp.dot(vtb[kt], p, preferred_element_type=F32)
            return carry

        def body(kt, carry):
            off = pl.multiple_of(kt * KEY_TILE, KEY_TILE)
            sel = (scr[kt] >= tf) & (kidx0 + off <= qidx)
            e = jnp.exp2(logits(kt) - m)
            parts = []
            for h in range(ATTN_HEADS):
                cs = slice(h * Q_BLOCK, (h + 1) * Q_BLOCK)
                parts.append(jnp.where(sel, e[:, cs], 0.0).astype(BF16))
            p = jnp.concatenate(parts, axis=1)
            acc_s[...] += jnp.dot(vtb[kt], p, preferred_element_type=F32)
            return carry

        lax.fori_loop(0, nk, body_masked if masked else body, 0)

    def fast_path():
        q2 = (qa2 * qa2).astype(BF16)
        qsq = lax.dot_general(jnp.ones((8, ATTN_HEAD_DIM), BF16), q2, contract_last,
                              preferred_element_type=F32)[0:1]
        softmax_sum(jnp.sqrt(qsq * kmax[0:1, 0:1]) * logit_scale, masked=False)
        return (jnp.min(acc_s[dv:dv + 1, :]) > 0.0).astype(I32)

    done = lax.cond(excess, lambda: jnp.int32(0), fast_path) == 1

    def mask_tile(kt, sel):
        s = logits(kt)
        for h in range(ATTN_HEADS):
            cs = slice(h * Q_BLOCK, (h + 1) * Q_BLOCK)
            sh = jnp.where(sel, s[:, cs], NEG_BIG)
            sbuf[kt, :, cs] = sh
            mrun[:, cs] = jnp.maximum(
                mrun[:, cs], jnp.max(sh.reshape(KEY_TILE // 8, 8, Q_BLOCK), axis=0))

    def p1_plain(kt, carry):
        off = pl.multiple_of(kt * KEY_TILE, KEY_TILE)
        mask_tile(kt, (scr[kt] >= tf) & (kidx0 + off <= qidx))
        return carry

    def p1_ties(kt, tie_off):
        off = pl.multiple_of(kt * KEY_TILE, KEY_TILE)
        sc = scr[kt]
        eq = sc == tf
        tie = jnp.where(eq, 1.0, 0.0)
        rank = jnp.dot(tri_ref[...], tie.astype(BF16), preferred_element_type=F32) + tie_off
        mask_tile(kt, ((sc > tf) | (eq & (rank <= need))) & (kidx0 + off <= qidx))
        return tie_off + jnp.sum(tie, axis=0, keepdims=True)

    @pl.when(jnp.logical_not(done))
    def _exact_path():
        mrun[...] = jnp.full((8, n_heads_q), NEG_BIG, F32)

        @pl.when(excess)
        def _with_ties():
            lax.fori_loop(0, nk, p1_ties, jnp.zeros((1, Q_BLOCK), F32))

        @pl.when(jnp.logical_not(excess))
        def _without_ties():
            lax.fori_loop(0, nk, p1_plain, 0)

        softmax_sum(jnp.max(mrun[...], axis=0, keepdims=True), masked=True)

    out = acc_s[:dv, :] / acc_s[dv:dv + 1, :]
    for h in range(ATTN_HEADS):
        oh = out[:, h * Q_BLOCK:(h + 1) * Q_BLOCK].T
        o_ref[:, h * ATTN_HEAD_DIM:(h + 1) * ATTN_HEAD_DIM] = oh.astype(BF16)


def _prompt_attention(qa_hm, qi_pm, wi_t, ka, va, kd, tri, batch, seq):
    nq = seq // Q_BLOCK
    nkt = seq // KEY_TILE
    m = batch * seq
    n_heads_q = ATTN_HEADS * Q_BLOCK
    in_specs = [
        pl.BlockSpec((ATTN_HEADS, Q_BLOCK, ATTN_HEAD_DIM), lambda b, q: (0, b * nq + q, 0)),
        pl.BlockSpec((IDX_HEADS // 2, Q_BLOCK, LANES), lambda b, q: (0, b * nq + q, 0)),
        pl.BlockSpec((IDX_HEADS, Q_BLOCK), lambda b, q: (0, b * nq + q)),
        pl.BlockSpec((seq, ATTN_HEAD_DIM), lambda b, q: (b, 0)),
        pl.BlockSpec((seq, ATTN_HEAD_DIM), lambda b, q: (b, 0)),
        pl.BlockSpec((seq, LANES), lambda b, q: (b, 0)),
        pl.BlockSpec((KEY_TILE, KEY_TILE), lambda b, q: (0, 0)),
    ]
    return pl.pallas_call(
        _attn_body,
        grid=(batch, nq),
        in_specs=in_specs,
        out_specs=pl.BlockSpec((Q_BLOCK, ATTN_HEADS * ATTN_HEAD_DIM), lambda b, q: (b * nq + q, 0)),
        out_shape=jax.ShapeDtypeStruct((m, ATTN_HEADS * ATTN_HEAD_DIM), BF16),
        scratch_shapes=[
            pltpu.VMEM((seq, ATTN_HEAD_DIM), BF16),
            pltpu.VMEM((nkt, ATTN_HEAD_DIM + SUM_ROWS, KEY_TILE), BF16),
            pltpu.VMEM((nkt, KEY_TILE, Q_BLOCK), F32),
            pltpu.VMEM((nkt, KEY_TILE, n_heads_q), F32),
            pltpu.VMEM((8, n_heads_q), F32),
            pltpu.VMEM((ATTN_HEAD_DIM + SUM_ROWS, n_heads_q), F32),
            pltpu.VMEM((8, LANES), F32),
        ],
        compiler_params=_cparams(("arbitrary", "arbitrary")),
        name="prompt_attn",
    )(qa_hm, qi_pm, wi_t, ka, va, kd, tri)


def _gate(o, g):
    rn = o * lax.rsqrt(jnp.mean(o * o, axis=-1, keepdims=True) + EPS)
    return rn * (g / (1.0 + jnp.exp(-g)))


def _ret_body(q_ref, k_ref, v_ref, g_ref, decay_ref, rsc_ref, zeta_ref, gpow_ref,
              rg_ref, st_ref):
    c = pl.program_id(1)

    @pl.when(c == 0)
    def _init():
        st_ref[...] = jnp.zeros(st_ref.shape, F32)

    for h in range(RET_HEADS):
        sl = slice(h * 128, (h + 1) * 128)
        q = q_ref[:, sl]
        k = k_ref[:, sl]
        v = v_ref[:, sl]
        r_old = st_ref[0, h]
        qk = lax.dot_general(q, k, (((1,), (1,)), ((), ())), preferred_element_type=F32)
        inner = jnp.dot((qk * decay_ref[h]).astype(BF16), v, preferred_element_type=F32)
        cross = jnp.dot(q, r_old.astype(BF16), preferred_element_type=F32) * rsc_ref[h]
        kz = (k.astype(F32) * zeta_ref[h]).astype(BF16)
        upd = lax.dot_general(kz, v, (((0,), (0,)), ((), ())), preferred_element_type=F32)
        st_ref[0, h] = r_old * gpow_ref[h] + upd
        rg_ref[:, sl] = _gate(inner + cross, g_ref[:, sl].astype(F32)).astype(BF16)


def _prompt_retention(main, decay, rsc, zeta, gpow, batch, seq):
    nc = seq // RET_CHUNK
    m = batch * seq
    width = RET_WIDTH
    const3 = lambda b, c: (0, 0, 0)
    in_specs = [
        pl.BlockSpec((RET_CHUNK, width), lambda b, c: (b * nc + c, 0)),
        pl.BlockSpec((RET_CHUNK, width), lambda b, c: (b * nc + c, 1)),
        pl.BlockSpec((RET_CHUNK, width), lambda b, c: (b * nc + c, 2)),
        pl.BlockSpec((RET_CHUNK, width), lambda b, c: (b * nc + c, 3)),
        pl.BlockSpec((RET_HEADS, RET_CHUNK, RET_CHUNK), const3),
        pl.BlockSpec((RET_HEADS, RET_CHUNK, RET_DV), const3),
        pl.BlockSpec((RET_HEADS, RET_CHUNK, RET_DK), const3),
        pl.BlockSpec((RET_HEADS, 1, RET_DV), const3),
    ]
    return pl.pallas_call(
        _ret_body,
        grid=(batch, nc),
        in_specs=in_specs,
        out_specs=(
            pl.BlockSpec((RET_CHUNK, width), lambda b, c: (b * nc + c, 0)),
            pl.BlockSpec((1, RET_HEADS, RET_DK, RET_DV), lambda b, c: (b, 0, 0, 0)),
        ),
        out_shape=(
            jax.ShapeDtypeStruct((m, width), BF16),
            jax.ShapeDtypeStruct((batch, RET_HEADS, RET_DK, RET_DV), F32),
        ),
        compiler_params=_cparams(("arbitrary", "arbitrary")),
        name="prompt_ret",
    )(main, main, main, main, decay, rsc, zeta, gpow)


def _outproj_body(a_ref, r_ref, wa_ref, wr_ref, x_ref, g2_ref, x1_ref, h2_ref):
    mixed = (jnp.dot(a_ref[...], wa_ref[...], preferred_element_type=F32)
             + jnp.dot(r_ref[...], wr_ref[...], preferred_element_type=F32))
    x1 = x_ref[...] + mixed
    x1_ref[...] = x1
    ms = jnp.mean(x1 * x1, axis=-1, keepdims=True)
    h2_ref[...] = (x1 * lax.rsqrt(ms + EPS) * g2_ref[...]).astype(BF16)


def _out_projection(attn_o, rg, wa, wr, x2d, g2, tm):
    m = x2d.shape[0]
    half = attn_o.shape[1]
    in_specs = [
        pl.BlockSpec((tm, half), lambda i: (i, 0)),
        pl.BlockSpec((tm, half), lambda i: (i, 0)),
        pl.BlockSpec((half, D_MODEL), lambda i: (0, 0)),
        pl.BlockSpec((half, D_MODEL), lambda i: (0, 0)),
        pl.BlockSpec((tm, D_MODEL), lambda i: (i, 0)),
        pl.BlockSpec((1, D_MODEL), lambda i: (0, 0)),
    ]
    return pl.pallas_call(
        _outproj_body,
        grid=(m // tm,),
        in_specs=in_specs,
        out_specs=(pl.BlockSpec((tm, D_MODEL), lambda i: (i, 0)),
                   pl.BlockSpec((tm, D_MODEL), lambda i: (i, 0))),
        out_shape=(jax.ShapeDtypeStruct((m, D_MODEL), F32),
                   jax.ShapeDtypeStruct((m, D_MODEL), BF16)),
        compiler_params=_cparams(("arbitrary",)),
        name="out_proj",
    )(attn_o, rg, wa, wr, x2d, g2)


def _mlp_body(h2_ref, wu_ref, wd_ref, x1_ref, gf_ref, y_ref, acc_ref):
    f = pl.program_id(1)

    @pl.when(f == 0)
    def _init():
        acc_ref[...] = x1_ref[...]

    u = jnp.dot(h2_ref[...], wu_ref[...], preferred_element_type=F32)
    a = jnp.maximum(u, 0.0)
    acc_ref[...] += jnp.dot((a * a).astype(BF16), wd_ref[...], preferred_element_type=F32)

    @pl.when(f == pl.num_programs(1) - 1)
    def _final():
        x2 = acc_ref[...]
        ms = jnp.mean(x2 * x2, axis=-1, keepdims=True)
        y_ref[...] = x2 * lax.rsqrt(ms + EPS) * gf_ref[...]


def _mlp_cast_body(h2_ref, wu_ref, wd_ref, x1_ref, gf_ref, y_ref, wub_ref, wdb_ref, acc_ref):
    f = pl.program_id(0)

    @pl.when(f == 0)
    def _init():
        acc_ref[...] = x1_ref[...]

    wu = wu_ref[...].astype(BF16)
    wd = wd_ref[...].astype(BF16)
    wub_ref[...] = wu
    wdb_ref[...] = wd
    u = jnp.dot(h2_ref[...], wu, preferred_element_type=F32)
    a = jnp.maximum(u, 0.0)
    acc_ref[...] += jnp.dot((a * a).astype(BF16), wd, preferred_element_type=F32)

    @pl.when(f == pl.num_programs(0) - 1)
    def _final():
        x2 = acc_ref[...]
        ms = jnp.mean(x2 * x2, axis=-1, keepdims=True)
        y_ref[...] = x2 * lax.rsqrt(ms + EPS) * gf_ref[...]


def _mlp_and_cast(h2, w_up, w_down, x1, gf, tf):
    m = h2.shape[0]
    full = lambda f: (0, 0)
    return pl.pallas_call(
        _mlp_cast_body,
        grid=(D_FF // tf,),
        in_specs=[
            pl.BlockSpec((m, D_MODEL), full),
            pl.BlockSpec((D_MODEL, tf), lambda f: (0, f)),
            pl.BlockSpec((tf, D_MODEL), lambda f: (f, 0)),
            pl.BlockSpec((m, D_MODEL), full),
            pl.BlockSpec((1, D_MODEL), full),
        ],
        out_specs=(
            pl.BlockSpec((m, D_MODEL), full),
            pl.BlockSpec((D_MODEL, tf), lambda f: (0, f)),
            pl.BlockSpec((tf, D_MODEL), lambda f: (f, 0)),
        ),
        out_shape=(
            jax.ShapeDtypeStruct((m, D_MODEL), F32),
            jax.ShapeDtypeStruct((D_MODEL, D_FF), BF16),
            jax.ShapeDtypeStruct((D_FF, D_MODEL), BF16),
        ),
        scratch_shapes=[pltpu.VMEM((m, D_MODEL), F32)],
        compiler_params=_cparams(("arbitrary",)),
        name="mlp_cast",
    )(h2, w_up, w_down, x1, gf)


def _mlp(h2, wu, wd, x1, gf, tm, tf):
    m = h2.shape[0]
    in_specs = [
        pl.BlockSpec((tm, D_MODEL), lambda i, f: (i, 0)),
        pl.BlockSpec((D_MODEL, tf), lambda i, f: (0, f)),
        pl.BlockSpec((tf, D_MODEL), lambda i, f: (f, 0)),
        pl.BlockSpec((tm, D_MODEL), lambda i, f: (i, 0)),
        pl.BlockSpec((1, D_MODEL), lambda i, f: (0, 0)),
    ]
    return pl.pallas_call(
        _mlp_body,
        grid=(m // tm, D_FF // tf),
        in_specs=in_specs,
        out_specs=pl.BlockSpec((tm, D_MODEL), lambda i, f: (i, 0)),
        out_shape=jax.ShapeDtypeStruct((m, D_MODEL), F32),
        scratch_shapes=[pltpu.VMEM((tm, D_MODEL), F32)],
        compiler_params=_cparams(("arbitrary", "arbitrary")),
        name="mlp",
    )(h2, wu, wd, x1, gf)


def _fetch_pages(pt_ref, step, slot, streams, start):
    n_pages = pt_ref.shape[1]
    for hbm, buf, sem in streams:
        for j in range(n_pages):
            cp = pltpu.make_async_copy(hbm.at[pt_ref[step, j]], buf.at[slot, j], sem.at[slot])
            if start:
                cp.start()
            else:
                cp.wait()


def _paged_loop(pt_ref, streams, step_fn):
    nb = pt_ref.shape[0]
    n_slots = streams[0][1].shape[0]
    ahead = n_slots - 1
    for s in range(ahead):
        _fetch_pages(pt_ref, s, s, streams, start=True)

    def body(b, carry):
        @pl.when(b + ahead < nb)
        def _next():
            _fetch_pages(pt_ref, b + ahead, (b + ahead) % n_slots, streams, start=True)

        slot = b % n_slots
        _fetch_pages(pt_ref, b, slot, streams, start=False)
        step_fn(b, slot)
        return carry

    lax.fori_loop(0, nb, body, 0)


def _sidx_body(pt_ref, qi_ref, w_ref, kin_ref, cache_hbm, out_ref, kt_s, pbuf, sem):
    n_pages = pt_ref.shape[1]
    page = pbuf.shape[3]
    past = n_pages * page
    lane = lax.broadcasted_iota(I32, (1, LANES), 1)

    def step(b, slot):
        qi = qi_ref[b]
        w = w_ref[b] * (IDX_HEAD_DIM ** -0.5)
        for j in range(n_pages):
            kt_s[:, j * page:(j + 1) * page] = pbuf[slot, j].astype(BF16)
        s = jnp.dot(qi, kt_s[...], preferred_element_type=F32)
        out_ref[b, :, 0:past] = jnp.sum(jnp.maximum(s, 0.0) * w, axis=0, keepdims=True)
        sn = jnp.sum(qi.astype(F32) * kin_ref[b].astype(BF16).astype(F32), axis=1, keepdims=True)
        rn = jnp.sum(jnp.maximum(sn, 0.0) * w, axis=0, keepdims=True)
        out_ref[b, :, past:past + LANES] = jnp.where(lane == 0, rn, -jnp.inf)

    _paged_loop(pt_ref, ((cache_hbm, pbuf, sem),), step)


def _sample_index_scores(page_table, qi_s, wi_s, ki_s, cache_idx_k_t):
    nb, n_pages = page_table.shape
    page = cache_idx_k_t.shape[2]
    width = n_pages * page + LANES

    vmem = pl.BlockSpec(memory_space=pltpu.VMEM)
    return pl.pallas_call(
        _sidx_body,
        in_specs=[pl.BlockSpec(memory_space=pltpu.SMEM), vmem, vmem, vmem,
                  pl.BlockSpec(memory_space=pl.ANY)],
        out_specs=vmem,
        out_shape=jax.ShapeDtypeStruct((nb, 1, width), F32),
        scratch_shapes=[pltpu.VMEM((IDX_HEAD_DIM, n_pages * page), BF16),
                        pltpu.VMEM((PAGE_SLOTS, n_pages, IDX_HEAD_DIM, page), F32),
                        pltpu.SemaphoreType.DMA((PAGE_SLOTS,))],
        compiler_params=pltpu.CompilerParams(vmem_limit_bytes=VMEM_LIMIT),
        name="sample_idx",
    )(page_table, qi_s, wi_s, ki_s, cache_idx_k_t)


def _ssel_body(sc_ref, tri_ref, sel_ref):
    rows, width = sc_ref.shape
    nt = width // LANES
    n_valid = (nt - 1) * LANES + 1

    def tile(kt):
        return sc_ref[:, kt * LANES:(kt + 1) * LANES]

    def count_cmp(cmp):
        acc = jnp.zeros((rows, LANES), F32)
        for kt in range(nt):
            acc = acc + jnp.where(cmp(tile(kt)), 1.0, 0.0)
        return jnp.broadcast_to(jnp.sum(acc, axis=1, keepdims=True), (rows, LANES))

    tf = _threshold_search(lambda c: count_cmp(lambda sc: sc >= c), 32, (rows, LANES))
    need = float(TOPK_MAX) - count_cmp(lambda sc: sc > tf)
    tie_off = jnp.zeros((rows, LANES), F32)
    for kt in range(nt):
        col = kt * LANES + lax.broadcasted_iota(I32, (rows, LANES), 1)
        sc = tile(kt)
        eq = sc == tf
        tie = jnp.where(eq, 1.0, 0.0)
        rank = jnp.dot(tie.astype(BF16), tri_ref[...], preferred_element_type=F32) + tie_off
        sel = ((sc > tf) | (eq & (rank <= need))) & (col < n_valid)
        sel_ref[:, kt * LANES:(kt + 1) * LANES] = jnp.where(sel, 1.0, 0.0)
        tie_off = tie_off + jnp.broadcast_to(jnp.sum(tie, axis=1, keepdims=True), (rows, LANES))


def _sample_select(scores2d, tri):
    rows, width = scores2d.shape
    return pl.pallas_call(
        _ssel_body,
        out_shape=jax.ShapeDtypeStruct((rows, width), F32),
        compiler_params=pltpu.CompilerParams(vmem_limit_bytes=VMEM_LIMIT),
        name="sample_select",
    )(scores2d, tri)


def _sattn_body(pt_ref, q_ref, sel_ref, kn_ref, vn_ref, ck_hbm, cv_hbm, o_ref,
                kbuf, vbuf, ksem, vsem):
    n_pages = pt_ref.shape[1]
    page = kbuf.shape[2]
    past = n_pages * page
    scale = ATTN_HEAD_DIM ** -0.5

    def step(b, slot):
        q = q_ref[b]
        k_all = kbuf[slot].reshape(past, ATTN_HEAD_DIM).astype(BF16)
        v_all = vbuf[slot].reshape(past, ATTN_HEAD_DIM).astype(BF16)
        s = lax.dot_general(q, k_all, (((1,), (1,)), ((), ())), preferred_element_type=F32)
        s = jnp.where(sel_ref[b, :, 0:past] > 0.5, s * scale, NEG_BIG)
        kn = kn_ref[b].astype(BF16).astype(F32)
        sn = jnp.sum(q.astype(F32) * kn, axis=1, keepdims=True) * scale
        sn = jnp.where(sel_ref[b, :, past:past + 1] > 0.5, sn, NEG_BIG)
        m = jnp.maximum(jnp.max(s, axis=1, keepdims=True), sn)
        pn = jnp.exp(sn - m)
        p = jnp.exp(s - m)
        l = pn + jnp.sum(p, axis=1, keepdims=True)
        acc = (pn * vn_ref[b].astype(BF16).astype(F32)
               + jnp.dot(p.astype(BF16), v_all, preferred_element_type=F32))
        o_ref[b] = (acc / l).astype(BF16)

    _paged_loop(pt_ref, ((ck_hbm, kbuf, ksem), (cv_hbm, vbuf, vsem)), step)


def _sample_attention(page_table, qa_s, sel3, ka_s, va_s, cache_k, cache_v):
    nb, n_pages = page_table.shape
    page = cache_k.shape[1]

    vmem = pl.BlockSpec(memory_space=pltpu.VMEM)
    hbm = pl.BlockSpec(memory_space=pl.ANY)
    return pl.pallas_call(
        _sattn_body,
        in_specs=[pl.BlockSpec(memory_space=pltpu.SMEM), vmem, vmem, vmem, vmem, hbm, hbm],
        out_specs=vmem,
        out_shape=jax.ShapeDtypeStruct((nb, ATTN_HEADS, ATTN_HEAD_DIM), BF16),
        scratch_shapes=[pltpu.VMEM((PAGE_SLOTS, n_pages, page, ATTN_HEAD_DIM), F32),
                        pltpu.VMEM((PAGE_SLOTS, n_pages, page, ATTN_HEAD_DIM), F32),
                        pltpu.SemaphoreType.DMA((PAGE_SLOTS,)),
                        pltpu.SemaphoreType.DMA((PAGE_SLOTS,))],
        compiler_params=pltpu.CompilerParams(vmem_limit_bytes=VMEM_LIMIT),
        name="sample_attn",
    )(page_table, qa_s, sel3, ka_s, va_s, cache_k, cache_v)


def _sret_body(qkvg_ref, st_ref, gam_ref, rg_ref, so_ref):
    ns = st_ref.shape[0]
    for s in range(ns):
        blk = qkvg_ref[s].astype(F32)
        q8 = blk[0:8]
        k8 = blk[8:16]
        v8 = blk[16:24]
        g8 = blk[24:32]
        q_t = q8.T
        k_t = k8.T
        qk = jnp.sum(q8 * k8, axis=1, keepdims=True)
        rows = []
        for h in range(RET_HEADS):
            r_old = st_ref[s, h]
            gam = gam_ref[h]
            qcol = jnp.broadcast_to(q_t[:, h:h + 1], (RET_DK, RET_DV))
            kcol = jnp.broadcast_to(k_t[:, h:h + 1], (RET_DK, RET_DV))
            vrow = v8[h:h + 1]
            q_r = jnp.sum(qcol * r_old, axis=0, keepdims=True)
            rows.append(gam * q_r + qk[h:h + 1] * vrow)
            so_ref[s, h] = gam * r_old + kcol * vrow
        ret = jnp.concatenate(rows, axis=0)
        rg_ref[s] = _gate(ret, g8).astype(BF16)


def _sample_retention(qkvg, state, gam, ns):
    nb = state.shape[0]
    return pl.pallas_call(
        _sret_body,
        grid=(nb // ns,),
        in_specs=[
            pl.BlockSpec((ns, 32, LANES), lambda i: (i, 0, 0)),
            pl.BlockSpec((ns, RET_HEADS, RET_DK, RET_DV), lambda i: (i, 0, 0, 0)),
            pl.BlockSpec((RET_HEADS, 1, LANES), lambda i: (0, 0, 0)),
        ],
        out_specs=(
            pl.BlockSpec((ns, RET_HEADS, RET_DV), lambda i: (i, 0, 0)),
            pl.BlockSpec((ns, RET_HEADS, RET_DK, RET_DV), lambda i: (i, 0, 0, 0)),
        ),
        out_shape=(
            jax.ShapeDtypeStruct((nb, RET_HEADS, RET_DV), BF16),
            jax.ShapeDtypeStruct(state.shape, F32),
        ),
        compiler_params=_cparams(("arbitrary",)),
        name="sample_ret",
    )(qkvg, state, gam)


def _rotary_table(pos):
    half = RET_DK // 2
    inv = ROPE_BASE ** (-np.arange(half, dtype=np.float64) / half)
    ang = np.asarray(pos, np.float64)[:, None] * inv[None, :]
    cos = np.cos(ang)
    sin = np.sin(ang)
    return jnp.asarray(np.concatenate([cos, cos, -sin, sin], axis=1), F32)


def _retention_constants():
    lg = np.log1p(-np.exp2(-5.0 - np.arange(RET_HEADS, dtype=np.float64)))
    n = RET_CHUNK
    i = np.arange(n, dtype=np.float64)
    diff = i[:, None] - i[None, :]
    decay = np.where(diff[None] >= 0, np.exp(np.maximum(diff, 0.0)[None] * lg[:, None, None]), 0.0)
    rsc = np.exp((i + 1.0)[None, :] * lg[:, None])
    zeta = np.exp((n - 1.0 - i)[None, :] * lg[:, None])
    gpow = np.exp(n * lg)
    gam1 = np.exp(lg)
    rsc_b = np.broadcast_to(rsc[:, :, None], (RET_HEADS, n, RET_DV))
    zeta_b = np.broadcast_to(zeta[:, :, None], (RET_HEADS, n, RET_DK))
    gpow_b = np.broadcast_to(gpow[:, None, None], (RET_HEADS, 1, RET_DV))
    gam1_b = np.broadcast_to(gam1[:, None, None], (RET_HEADS, 1, LANES))
    return tuple(jnp.asarray(a, F32) for a in (decay, rsc_b, zeta_b, gpow_b, gam1_b))


def _upper_tri(n):
    return jnp.asarray(np.triu(np.ones((n, n), np.float32)), BF16)


def _lower_tri(n):
    return jnp.asarray(np.tril(np.ones((n, n), np.float32)), BF16)


def _pad_lanes(v):
    return jnp.pad(v, (0, LANES - v.shape[0])).reshape(1, LANES)


def kernel(x_prompt, x_sample, cache_k, cache_v, cache_idx_k, state_ret, page_table,
           norm1_g, w_in, idx_k_norm_g, idx_k_norm_b, w_out, norm2_g, w_up, w_down, final_norm_g):
    batch, seq, _ = x_prompt.shape
    nb = x_sample.shape[0]
    past_len = page_table.shape[1] * cache_k.shape[1]
    half_mix = ATTN_HEADS * ATTN_HEAD_DIM

    wt = w_in.T.astype(BF16)
    wa = w_out[:half_mix].astype(BF16)
    wr = w_out[half_mix:].astype(BF16)
    g1 = norm1_g.reshape(1, D_MODEL)
    g2 = norm2_g.reshape(1, D_MODEL)
    gf = final_norm_g.reshape(1, D_MODEL)
    lng = _pad_lanes(idx_k_norm_g)
    lnb = _pad_lanes(idx_k_norm_b)
    decay, rsc_b, zeta_b, gpow_b, gam1_b = _retention_constants()

    xp = x_prompt.reshape(batch * seq, D_MODEL)
    cs_p = _rotary_table(np.arange(seq))
    qa_p, qi_p, ka_p, va_p, ki_p, kd_p, wi_p = _project_attn(
        xp, g1, wt, lng, lnb, tm=TILES["proj_attn_rows"])
    main_p = _project_ret(xp, g1, wt, cs_p, tm=TILES["proj_ret_rows"])
    attn_p = _prompt_attention(qa_p, qi_p, wi_p, ka_p, va_p, kd_p, _lower_tri(KEY_TILE),
                               batch, seq)
    rg_p, ret_state_p = _prompt_retention(main_p, decay, rsc_b, zeta_b, gpow_b, batch, seq)
    x1_p, h2_p = _out_projection(attn_p, rg_p, wa, wr, xp, g2, tm=TILES["out_proj_rows"])

    assert nb >= PAGE_SLOTS
    xs = x_sample.reshape(nb, D_MODEL)
    cs_s = _rotary_table(np.full((nb,), past_len))
    qa_s, qi_s, ka_s, va_s, ki_s, _, wi_s = _project_attn(xs, g1, wt, lng, lnb, tm=nb)
    main_s = _project_ret(xs, g1, wt, cs_s, tm=nb)
    scores = _sample_index_scores(
        page_table,
        qi_s.transpose(1, 0, 2).reshape(nb, IDX_HEADS, IDX_HEAD_DIM),
        wi_s.T.reshape(nb, IDX_HEADS, 1),
        ki_s.reshape(nb, 1, IDX_HEAD_DIM),
        jnp.swapaxes(cache_idx_k, 1, 2))
    width = scores.shape[2]
    sel = _sample_select(scores.reshape(nb, width), _upper_tri(LANES))
    attn_s = _sample_attention(
        page_table,
        qa_s.transpose(1, 0, 2),
        sel.reshape(nb, 1, width),
        ka_s.reshape(nb, 1, ATTN_HEAD_DIM),
        va_s.reshape(nb, 1, ATTN_HEAD_DIM),
        cache_k, cache_v)
    rg_s, ret_state_s = _sample_retention(main_s.reshape(nb, 32, LANES), state_ret, gam1_b,
                                          ns=TILES["sample_ret_rows"])
    x1_s, h2_s = _out_projection(attn_s.reshape(nb, half_mix), rg_s.reshape(nb, RET_WIDTH),
                                 wa, wr, xs, g2, tm=nb)
    y_s, wu, wd = _mlp_and_cast(h2_s, w_up, w_down, x1_s, gf, tf=TILES["mlp_cast_ff"])
    y_p = _mlp(h2_p, wu, wd, x1_p, gf, tm=TILES["mlp_rows"], tf=TILES["mlp_ff"])

    return (
        y_p.reshape(batch, seq, D_MODEL),
        y_s.reshape(nb, 1, D_MODEL),
        ka_p.reshape(batch, seq, ATTN_HEAD_DIM),
        va_p.reshape(batch, seq, ATTN_HEAD_DIM),
        ki_p.reshape(batch, seq, IDX_HEAD_DIM),
        ret_state_p,
        ka_s.reshape(nb, 1, ATTN_HEAD_DIM),
        va_s.reshape(nb, 1, ATTN_HEAD_DIM),
        ki_s.reshape(nb, 1, IDX_HEAD_DIM),
        ret_state_s,
    )
```

```python
import functools

import numpy as np
import jax
import jax.numpy as jnp
from jax import lax
from jax.experimental import pallas as pl
from jax.experimental.pallas import tpu as pltpu

F32 = jnp.float32
BF16 = jnp.bfloat16
I32 = jnp.int32

D_MODEL = 2048
ATTN_HEADS = 8
ATTN_HEAD_DIM = 128
IDX_HEADS = 16
IDX_HEAD_DIM = 64
TOPK_MAX = 256
RET_HEADS = 8
RET_DK = 128
RET_DV = 128
RET_CHUNK = 256
ROPE_BASE = 10000.0
D_FF = 4 * D_MODEL
EPS = 1e-6
Q_BLOCK = 256

OFF_QA, OFF_KA, OFF_VA, OFF_QI, OFF_KI, OFF_WI = 0, 1024, 1152, 1280, 2304, 2368
OFF_QR, OFF_KR, OFF_VR, OFF_GR = 2384, 3408, 4432, 5456
RET_WIDTH = RET_HEADS * RET_DV

LANES = 128
PROJ_TILE = 512
KEY_TILE = 256
COUNT_ROWS = 64
PAGE_SLOTS = 4
SUM_ROWS = 16
LOG2_E = 1.4426950408889634
INT_MIN = -2 ** 31
KEY_NEG_INF = -2 ** 31 + 0x7FFFFF
BF16_KEY_NEG_INF = -2 ** 15 + 0x7F
NEG_BIG = -1e30
VMEM_LIMIT = 56 * 1024 * 1024

TILES = {
    "proj_attn_rows": 1024,
    "proj_ret_rows": 512,
    "out_proj_rows": 512,
    "mlp_rows": 512,
    "mlp_ff": 1024,
    "mlp_cast_ff": 512,
    "sample_ret_rows": 8,
}


def _cparams(sem):
    return pltpu.CompilerParams(dimension_semantics=sem, vmem_limit_bytes=VMEM_LIMIT)


def _resident(shape):
    zeros = (0,) * len(shape)
    return pl.BlockSpec(shape, lambda *_: zeros, pipeline_mode=pl.Buffered(1))


def _normed_input(x_ref, g_ref, xn_ref):
    x = x_ref[...]
    ms = jnp.mean(x * x, axis=-1, keepdims=True)
    xn_ref[...] = (x * lax.rsqrt(ms + EPS) * g_ref[...]).astype(BF16)


def _matmul_rows(xn_ref, wt_ref, r0, n):
    return lax.dot_general(xn_ref[...], wt_ref[r0:r0 + n, :], (((1,), (1,)), ((), ())),
                           preferred_element_type=F32)


def _proj_attn_body(x_ref, g_ref, wt_ref, lng_ref, lnb_ref,
                    qa_ref, qi_ref, ka_ref, va_ref, ki_ref, kd_ref, wi_ref, xn_ref):
    _normed_input(x_ref, g_ref, xn_ref)
    mm = functools.partial(_matmul_rows, xn_ref, wt_ref)
    for t in range(ATTN_HEADS * ATTN_HEAD_DIM // PROJ_TILE):
        acc = mm(OFF_QA + t * PROJ_TILE, PROJ_TILE)
        for hh in range(4):
            qa_ref[4 * t + hh] = acc[:, hh * LANES:(hh + 1) * LANES].astype(BF16)
    for t in range(IDX_HEADS * IDX_HEAD_DIM // PROJ_TILE):
        acc = mm(OFF_QI + t * PROJ_TILE, PROJ_TILE)
        for hh in range(4):
            qi_ref[4 * t + hh] = acc[:, hh * LANES:(hh + 1) * LANES].astype(BF16)
    kv = mm(OFF_KA, 2 * ATTN_HEAD_DIM)
    ka_ref[...] = kv[:, :ATTN_HEAD_DIM]
    va_ref[...] = kv[:, ATTN_HEAD_DIM:]
    kw = mm(OFF_KI, LANES)
    lane = lax.broadcasted_iota(I32, kw.shape, 1)
    is_k = lane < IDX_HEAD_DIM
    mu = jnp.sum(jnp.where(is_k, kw, 0.0), axis=-1, keepdims=True) * (1.0 / IDX_HEAD_DIM)
    d = jnp.where(is_k, kw - mu, 0.0)
    var = jnp.sum(d * d, axis=-1, keepdims=True) * (1.0 / IDX_HEAD_DIM)
    kn = d * lax.rsqrt(var + EPS) * lng_ref[...] + lnb_ref[...]
    ki_ref[...] = kn[:, :IDX_HEAD_DIM]
    kd_ref[...] = jnp.where(is_k, kn, pltpu.roll(kn, IDX_HEAD_DIM, 1)).astype(BF16)
    wi_ref[...] = (kw[:, IDX_HEAD_DIM:IDX_HEAD_DIM + IDX_HEADS] * (IDX_HEADS ** -0.5)).T


def _proj_ret_body(x_ref, g_ref, wt_ref, cs_ref, main_ref, xn_ref):
    _normed_input(x_ref, g_ref, xn_ref)
    base = 0
    cosf = cs_ref[:, :LANES]
    sinf = cs_ref[:, LANES:]
    tiles = RET_WIDTH // PROJ_TILE
    for seg, (off, scale) in enumerate(((OFF_QR, None), (OFF_KR, RET_DK ** -0.5))):
        for t in range(tiles):
            acc = _matmul_rows(xn_ref, wt_ref, off - base + t * PROJ_TILE, PROJ_TILE)
            for hh in range(PROJ_TILE // LANES):
                xh = acc[:, hh * LANES:(hh + 1) * LANES]
                r = xh * cosf + pltpu.roll(xh, RET_DK // 2, 1) * sinf
                if scale is not None:
                    r = r * scale
                c0 = seg * RET_WIDTH + t * PROJ_TILE + hh * LANES
                main_ref[:, c0:c0 + LANES] = r.astype(BF16)
    for seg, off in ((2, OFF_VR), (3, OFF_GR)):
        for t in range(tiles):
            acc = _matmul_rows(xn_ref, wt_ref, off - base + t * PROJ_TILE, PROJ_TILE)
            c0 = seg * RET_WIDTH + t * PROJ_TILE
            main_ref[:, c0:c0 + PROJ_TILE] = acc.astype(BF16)


def _project_attn(x2d, g1, wt_attn, lng, lnb, tm):
    m = x2d.shape[0]
    row = lambda i: (i, 0)
    out_shape = (
        jax.ShapeDtypeStruct((ATTN_HEADS, m, ATTN_HEAD_DIM), BF16),
        jax.ShapeDtypeStruct((IDX_HEADS // 2, m, LANES), BF16),
        jax.ShapeDtypeStruct((m, ATTN_HEAD_DIM), F32),
        jax.ShapeDtypeStruct((m, ATTN_HEAD_DIM), F32),
        jax.ShapeDtypeStruct((m, IDX_HEAD_DIM), F32),
        jax.ShapeDtypeStruct((m, LANES), BF16),
        jax.ShapeDtypeStruct((IDX_HEADS, m), F32),
    )
    out_specs = (
        pl.BlockSpec((ATTN_HEADS, tm, ATTN_HEAD_DIM), lambda i: (0, i, 0)),
        pl.BlockSpec((IDX_HEADS // 2, tm, LANES), lambda i: (0, i, 0)),
        pl.BlockSpec((tm, ATTN_HEAD_DIM), row),
        pl.BlockSpec((tm, ATTN_HEAD_DIM), row),
        pl.BlockSpec((tm, IDX_HEAD_DIM), row),
        pl.BlockSpec((tm, LANES), row),
        pl.BlockSpec((IDX_HEADS, tm), lambda i: (0, i)),
    )
    return pl.pallas_call(
        _proj_attn_body,
        grid=(m // tm,),
        in_specs=[pl.BlockSpec((tm, D_MODEL), row), _resident((1, D_MODEL)),
                  _resident((OFF_QR + LANES, D_MODEL)), _resident((1, LANES)),
                  _resident((1, LANES))],
        out_specs=out_specs,
        out_shape=out_shape,
        scratch_shapes=[pltpu.VMEM((tm, D_MODEL), BF16)],
        compiler_params=_cparams(("arbitrary",)),
        name="proj_attn",
    )(x2d, g1, wt_attn, lng, lnb)


def _project_ret(x2d, g1, wt_ret, cs, tm):
    m = x2d.shape[0]
    n_pos_blocks = cs.shape[0] // tm
    row = lambda i: (i, 0)
    return pl.pallas_call(
        _proj_ret_body,
        grid=(m // tm,),
        in_specs=[pl.BlockSpec((tm, D_MODEL), row), _resident((1, D_MODEL)),
                  _resident(wt_ret.shape),
                  pl.BlockSpec((tm, 2 * LANES), lambda i: (i % n_pos_blocks, 0))],
        out_specs=pl.BlockSpec((tm, 4 * RET_WIDTH), row),
        out_shape=jax.ShapeDtypeStruct((m, 4 * RET_WIDTH), BF16),
        scratch_shapes=[pltpu.VMEM((tm, D_MODEL), BF16)],
        compiler_params=_cparams(("arbitrary",)),
        name="proj_ret",
    )(x2d, g1, wt_ret, cs)


def _key_to_float(key):
    bits = key ^ ((key >> 31) & 0x7FFFFFFF)
    return lax.bitcast_convert_type(bits, F32)


def _threshold_search(count_ge, n_iter, shape):
    def body(it, t):
        bit = lax.shift_left(jnp.int32(1), 31 - it)
        cand = t ^ bit
        cnt = count_ge(_key_to_float(cand))
        return jnp.where(cnt >= float(TOPK_MAX), cand, t)

    t = lax.fori_loop(0, n_iter, body, jnp.full(shape, INT_MIN, I32))
    return _key_to_float(jnp.maximum(t, KEY_NEG_INF))


def _bf16_key_to_f32_key(k16):
    return lax.shift_left(k16, 16) | jnp.where(k16 < 0, 0xFFFF, 0)


def _threshold_search_coarse_fine(count_ge_bf16, count_ge, run, shape):
    def coarse(it, u):
        cand = u | lax.shift_left(jnp.int32(1), 15 - it)
        c = _key_to_float(_bf16_key_to_f32_key(cand - 32768)).astype(BF16)
        return jnp.where(count_ge_bf16(c) >= float(TOPK_MAX), cand, u)

    u = lax.fori_loop(0, jnp.where(run, 16, 0), coarse, jnp.zeros(shape, I32))
    k1 = jnp.maximum(u - 32768, BF16_KEY_NEG_INF)
    lo = _bf16_key_to_f32_key(jnp.maximum(k1 - 1, -32768))
    hi = _bf16_key_to_f32_key(jnp.minimum(k1 + 1, 32767))

    def fine(it, t):
        cand = t + lax.shift_left(jnp.int32(1), 16 - it)
        ok = (cand < hi) & (count_ge(_key_to_float(cand)) >= float(TOPK_MAX))
        return jnp.where(ok, cand, t)

    t = lax.fori_loop(0, jnp.where(run, 17, 0), fine, lo)
    return _key_to_float(jnp.maximum(t, KEY_NEG_INF))


def _attn_body(qa_ref, qi_ref, wit_ref, ka_ref, va_ref, kd_ref, tri_ref, o_ref,
               kbf, vtb, scr, sbuf, mrun, acc_s, kmax, scr16):
    qb = pl.program_id(1)
    n_heads_q = ATTN_HEADS * Q_BLOCK
    n_pairs = IDX_HEADS // 2
    dv = ATTN_HEAD_DIM
    logit_scale = ATTN_HEAD_DIM ** -0.5 * LOG2_E

    @pl.when(qb == 0)
    def _cast():
        ka = ka_ref[...]
        kbf[...] = ka.astype(BF16)
        kmax[...] = jnp.broadcast_to(jnp.max(jnp.sum(ka * ka, axis=1, keepdims=True)), kmax.shape)
        for kt in range(vtb.shape[0]):
            vtb[kt, :dv] = va_ref[kt * KEY_TILE:(kt + 1) * KEY_TILE, :].T.astype(BF16)
            vtb[kt, dv:] = jnp.ones((vtb.shape[1] - dv, KEY_TILE), BF16)

    nk = ((qb + 1) * Q_BLOCK + KEY_TILE - 1) // KEY_TILE
    wt = wit_ref[...] * (IDX_HEAD_DIM ** -0.5)
    qi2 = qi_ref[...].reshape(n_pairs * Q_BLOCK, LANES)
    lo_half = lax.broadcasted_iota(I32, (KEY_TILE, LANES), 1) < IDX_HEAD_DIM
    qidx = qb * Q_BLOCK + lax.broadcasted_iota(I32, (KEY_TILE, Q_BLOCK), 1)
    kidx0 = lax.broadcasted_iota(I32, (KEY_TILE, Q_BLOCK), 0)
    contract_last = (((1,), (1,)), ((), ()))

    def idx_body(kt, carry):
        off = pl.multiple_of(kt * KEY_TILE, KEY_TILE)
        kit = kd_ref[pl.ds(off, KEY_TILE), :]
        zero = jnp.zeros_like(kit)
        s_even = lax.dot_general(jnp.where(lo_half, kit, zero), qi2, contract_last,
                                 preferred_element_type=F32)
        s_odd = lax.dot_general(jnp.where(lo_half, zero, kit), qi2, contract_last,
                                preferred_element_type=F32)
        score = jnp.zeros((KEY_TILE, Q_BLOCK), F32)
        for g in range(n_pairs):
            cs = slice(g * Q_BLOCK, (g + 1) * Q_BLOCK)
            score = score + jnp.maximum(s_even[:, cs], 0.0) * wt[2 * g:2 * g + 1, :]
            score = score + jnp.maximum(s_odd[:, cs], 0.0) * wt[2 * g + 1:2 * g + 2, :]
        score = jnp.where(kidx0 + off <= qidx, score, -jnp.inf)
        scr[kt] = score
        scr16[kt] = score.astype(BF16)
        return carry

    lax.fori_loop(0, nk, idx_body, 0)

    def count_ge_bf16(c):
        def body(kt, acc):
            hit = scr16[kt] >= c
            for r in range(KEY_TILE // COUNT_ROWS):
                acc = jnp.where(hit[r * COUNT_ROWS:(r + 1) * COUNT_ROWS], acc + 1.0, acc)
            return acc
        acc = lax.fori_loop(0, nk, body, jnp.zeros((COUNT_ROWS, Q_BLOCK), BF16))
        return jnp.sum(acc.astype(F32), axis=0, keepdims=True)

    def count_cmp(cmp):
        def body(kt, acc):
            hit = cmp(scr[kt])
            for r in range(KEY_TILE // COUNT_ROWS):
                acc = jnp.where(hit[r * COUNT_ROWS:(r + 1) * COUNT_ROWS], acc + 1.0, acc)
            return acc
        acc = lax.fori_loop(0, nk, body, jnp.zeros((COUNT_ROWS, Q_BLOCK), F32))
        return jnp.sum(acc, axis=0, keepdims=True)

    tf = _threshold_search_coarse_fine(
        count_ge_bf16, lambda c: count_cmp(lambda sc: sc >= c),
        qb >= TOPK_MAX // Q_BLOCK, (1, Q_BLOCK))
    need = float(TOPK_MAX) - count_cmp(lambda sc: sc > tf)
    excess = jnp.max(count_cmp(lambda sc: sc >= tf)) > float(TOPK_MAX)

    qa2 = qa_ref[...].reshape(n_heads_q, ATTN_HEAD_DIM)

    def logits(kt):
        off = pl.multiple_of(kt * KEY_TILE, KEY_TILE)
        s = lax.dot_general(kbf[pl.ds(off, KEY_TILE), :], qa2, contract_last,
                            preferred_element_type=F32)
        return s * logit_scale

    def softmax_sum(m, masked):
        acc_s[...] = jnp.zeros(acc_s.shape, F32)

        def body_masked(kt, carry):
            p = jnp.exp2(sbuf[kt] - m).astype(BF16)
            acc_s[...] += jnp.dot(vtb[kt], p, preferred_element_type=F32)
            return carry

        def body(kt, carry):
            off = pl.multiple_of(kt * KEY_TILE, KEY_TILE)
            sel = (scr[kt] >= tf) & (kidx0 + off <= qidx)
            e = jnp.exp2(logits(kt) - m)
            parts = []
            for h in range(ATTN_HEADS):
                cs = slice(h * Q_BLOCK, (h + 1) * Q_BLOCK)
                parts.append(jnp.where(sel, e[:, cs], 0.0).astype(BF16))
            p = jnp.concatenate(parts, axis=1)
            acc_s[...] += jnp.dot(vtb[kt], p, preferred_element_type=F32)
            return carry

        lax.fori_loop(0, nk, body_masked if masked else body, 0)

    def fast_path():
        q2 = (qa2 * qa2).astype(BF16)
        qsq = lax.dot_general(jnp.ones((8, ATTN_HEAD_DIM), BF16), q2, contract_last,
                              preferred_element_type=F32)[0:1]
        softmax_sum(jnp.sqrt(qsq * kmax[0:1, 0:1]) * logit_scale, masked=False)
        return (jnp.min(acc_s[dv:dv + 1, :]) > 0.0).astype(I32)

    done = lax.cond(excess, lambda: jnp.int32(0), fast_path) == 1

    def mask_tile(kt, sel):
        s = logits(kt)
        for h in range(ATTN_HEADS):
            cs = slice(h * Q_BLOCK, (h + 1) * Q_BLOCK)
            sh = jnp.where(sel, s[:, cs], NEG_BIG)
            sbuf[kt, :, cs] = sh
            mrun[:, cs] = jnp.maximum(
                mrun[:, cs], jnp.max(sh.reshape(KEY_TILE // 8, 8, Q_BLOCK), axis=0))

    def p1_plain(kt, carry):
        off = pl.multiple_of(kt * KEY_TILE, KEY_TILE)
        mask_tile(kt, (scr[kt] >= tf) & (kidx0 + off <= qidx))
        return carry

    def p1_ties(kt, tie_off):
        off = pl.multiple_of(kt * KEY_TILE, KEY_TILE)
        sc = scr[kt]
        eq = sc == tf
        tie = jnp.where(eq, 1.0, 0.0)
        rank = jnp.dot(tri_ref[...], tie.astype(BF16), preferred_element_type=F32) + tie_off
        mask_tile(kt, ((sc > tf) | (eq & (rank <= need))) & (kidx0 + off <= qidx))
        return tie_off + jnp.sum(tie, axis=0, keepdims=True)

    @pl.when(jnp.logical_not(done))
    def _exact_path():
        mrun[...] = jnp.full((8, n_heads_q), NEG_BIG, F32)

        @pl.when(excess)
        def _with_ties():
            lax.fori_loop(0, nk, p1_ties, jnp.zeros((1, Q_BLOCK), F32))

        @pl.when(jnp.logical_not(excess))
        def _without_ties():
            lax.fori_loop(0, nk, p1_plain, 0)

        softmax_sum(jnp.max(mrun[...], axis=0, keepdims=True), masked=True)

    out = acc_s[:dv, :] / acc_s[dv:dv + 1, :]
    for h in range(ATTN_HEADS):
        oh = out[:, h * Q_BLOCK:(h + 1) * Q_BLOCK].T
        o_ref[:, h * ATTN_HEAD_DIM:(h + 1) * ATTN_HEAD_DIM] = oh.astype(BF16)


def _prompt_attention(qa_hm, qi_pm, wi_t, ka, va, kd, tri, batch, seq):
    nq = seq // Q_BLOCK
    nkt = seq // KEY_TILE
    m = batch * seq
    n_heads_q = ATTN_HEADS * Q_BLOCK
    in_specs = [
        pl.BlockSpec((ATTN_HEADS, Q_BLOCK, ATTN_HEAD_DIM), lambda b, q: (0, b * nq + q, 0)),
        pl.BlockSpec((IDX_HEADS // 2, Q_BLOCK, LANES), lambda b, q: (0, b * nq + q, 0)),
        pl.BlockSpec((IDX_HEADS, Q_BLOCK), lambda b, q: (0, b * nq + q)),
        pl.BlockSpec((seq, ATTN_HEAD_DIM), lambda b, q: (b, 0)),
        pl.BlockSpec((seq, ATTN_HEAD_DIM), lambda b, q: (b, 0)),
        pl.BlockSpec((seq, LANES), lambda b, q: (b, 0)),
        pl.BlockSpec((KEY_TILE, KEY_TILE), lambda b, q: (0, 0)),
    ]
    return pl.pallas_call(
        _attn_body,
        grid=(batch, nq),
        in_specs=in_specs,
        out_specs=pl.BlockSpec((Q_BLOCK, ATTN_HEADS * ATTN_HEAD_DIM), lambda b, q: (b * nq + q, 0)),
        out_shape=jax.ShapeDtypeStruct((m, ATTN_HEADS * ATTN_HEAD_DIM), BF16),
        scratch_shapes=[
            pltpu.VMEM((seq, ATTN_HEAD_DIM), BF16),
            pltpu.VMEM((nkt, ATTN_HEAD_DIM + SUM_ROWS, KEY_TILE), BF16),
            pltpu.VMEM((nkt, KEY_TILE, Q_BLOCK), F32),
            pltpu.VMEM((nkt, KEY_TILE, n_heads_q), F32),
            pltpu.VMEM((8, n_heads_q), F32),
            pltpu.VMEM((ATTN_HEAD_DIM + SUM_ROWS, n_heads_q), F32),
            pltpu.VMEM((8, LANES), F32),
            pltpu.VMEM((nkt, KEY_TILE, Q_BLOCK), BF16),
        ],
        compiler_params=_cparams(("arbitrary", "arbitrary")),
        name="prompt_attn",
    )(qa_hm, qi_pm, wi_t, ka, va, kd, tri)


def _gate(o, g):
    rn = o * lax.rsqrt(jnp.mean(o * o, axis=-1, keepdims=True) + EPS)
    return rn * (g / (1.0 + jnp.exp(-g)))


def _ret_body(q_ref, k_ref, v_ref, g_ref, decay_ref, rsc_ref, zeta_ref, gpow_ref,
              rg_ref, st_ref):
    c = pl.program_id(1)

    @pl.when(c == 0)
    def _init():
        st_ref[...] = jnp.zeros(st_ref.shape, F32)

    for h in range(RET_HEADS):
        sl = slice(h * 128, (h + 1) * 128)
        q = q_ref[:, sl]
        k = k_ref[:, sl]
        v = v_ref[:, sl]
        r_old = st_ref[0, h]
        qk = lax.dot_general(q, k, (((1,), (1,)), ((), ())), preferred_element_type=F32)
        inner = jnp.dot((qk * decay_ref[h]).astype(BF16), v, preferred_element_type=F32)
        cross = jnp.dot(q, r_old.astype(BF16), preferred_element_type=F32) * rsc_ref[h]
        kz = (k.astype(F32) * zeta_ref[h]).astype(BF16)
        upd = lax.dot_general(kz, v, (((0,), (0,)), ((), ())), preferred_element_type=F32)
        st_ref[0, h] = r_old * gpow_ref[h] + upd
        rg_ref[:, sl] = _gate(inner + cross, g_ref[:, sl].astype(F32)).astype(BF16)


def _prompt_retention(main, decay, rsc, zeta, gpow, batch, seq):
    nc = seq // RET_CHUNK
    m = batch * seq
    width = RET_WIDTH
    const3 = lambda b, c: (0, 0, 0)
    in_specs = [
        pl.BlockSpec((RET_CHUNK, width), lambda b, c: (b * nc + c, 0)),
        pl.BlockSpec((RET_CHUNK, width), lambda b, c: (b * nc + c, 1)),
        pl.BlockSpec((RET_CHUNK, width), lambda b, c: (b * nc + c, 2)),
        pl.BlockSpec((RET_CHUNK, width), lambda b, c: (b * nc + c, 3)),
        pl.BlockSpec((RET_HEADS, RET_CHUNK, RET_CHUNK), const3),
        pl.BlockSpec((RET_HEADS, RET_CHUNK, RET_DV), const3),
        pl.BlockSpec((RET_HEADS, RET_CHUNK, RET_DK), const3),
        pl.BlockSpec((RET_HEADS, 1, RET_DV), const3),
    ]
    return pl.pallas_call(
        _ret_body,
        grid=(batch, nc),
        in_specs=in_specs,
        out_specs=(
            pl.BlockSpec((RET_CHUNK, width), lambda b, c: (b * nc + c, 0)),
            pl.BlockSpec((1, RET_HEADS, RET_DK, RET_DV), lambda b, c: (b, 0, 0, 0)),
        ),
        out_shape=(
            jax.ShapeDtypeStruct((m, width), BF16),
            jax.ShapeDtypeStruct((batch, RET_HEADS, RET_DK, RET_DV), F32),
        ),
        compiler_params=_cparams(("arbitrary", "arbitrary")),
        name="prompt_ret",
    )(main, main, main, main, decay, rsc, zeta, gpow)


def _outproj_body(a_ref, r_ref, wa_ref, wr_ref, x_ref, g2_ref, x1_ref, h2_ref):
    mixed = (jnp.dot(a_ref[...], wa_ref[...], preferred_element_type=F32)
             + jnp.dot(r_ref[...], wr_ref[...], preferred_element_type=F32))
    x1 = x_ref[...] + mixed
    x1_ref[...] = x1
    ms = jnp.mean(x1 * x1, axis=-1, keepdims=True)
    h2_ref[...] = (x1 * lax.rsqrt(ms + EPS) * g2_ref[...]).astype(BF16)


def _out_projection(attn_o, rg, wa, wr, x2d, g2, tm):
    m = x2d.shape[0]
    half = attn_o.shape[1]
    in_specs = [
        pl.BlockSpec((tm, half), lambda i: (i, 0)),
        pl.BlockSpec((tm, half), lambda i: (i, 0)),
        pl.BlockSpec((half, D_MODEL), lambda i: (0, 0)),
        pl.BlockSpec((half, D_MODEL), lambda i: (0, 0)),
        pl.BlockSpec((tm, D_MODEL), lambda i: (i, 0)),
        pl.BlockSpec((1, D_MODEL), lambda i: (0, 0)),
    ]
    return pl.pallas_call(
        _outproj_body,
        grid=(m // tm,),
        in_specs=in_specs,
        out_specs=(pl.BlockSpec((tm, D_MODEL), lambda i: (i, 0)),
                   pl.BlockSpec((tm, D_MODEL), lambda i: (i, 0))),
        out_shape=(jax.ShapeDtypeStruct((m, D_MODEL), F32),
                   jax.ShapeDtypeStruct((m, D_MODEL), BF16)),
        compiler_params=_cparams(("arbitrary",)),
        name="out_proj",
    )(attn_o, rg, wa, wr, x2d, g2)


def _mlp_body(h2_ref, wu_ref, wd_ref, x1_ref, gf_ref, y_ref, acc_ref):
    f = pl.program_id(1)

    @pl.when(f == 0)
    def _init():
        acc_ref[...] = x1_ref[...]

    u = jnp.dot(h2_ref[...], wu_ref[...], preferred_element_type=F32)
    a = jnp.maximum(u, 0.0)
    acc_ref[...] += jnp.dot((a * a).astype(BF16), wd_ref[...], preferred_element_type=F32)

    @pl.when(f == pl.num_programs(1) - 1)
    def _final():
        x2 = acc_ref[...]
        ms = jnp.mean(x2 * x2, axis=-1, keepdims=True)
        y_ref[...] = x2 * lax.rsqrt(ms + EPS) * gf_ref[...]


def _mlp_cast_body(h2_ref, wu_ref, wd_ref, x1_ref, gf_ref, y_ref, wub_ref, wdb_ref, acc_ref):
    f = pl.program_id(0)

    @pl.when(f == 0)
    def _init():
        acc_ref[...] = x1_ref[...]

    wu = wu_ref[...].astype(BF16)
    wd = wd_ref[...].astype(BF16)
    wub_ref[...] = wu
    wdb_ref[...] = wd
    u = jnp.dot(h2_ref[...], wu, preferred_element_type=F32)
    a = jnp.maximum(u, 0.0)
    acc_ref[...] += jnp.dot((a * a).astype(BF16), wd, preferred_element_type=F32)

    @pl.when(f == pl.num_programs(0) - 1)
    def _final():
        x2 = acc_ref[...]
        ms = jnp.mean(x2 * x2, axis=-1, keepdims=True)
        y_ref[...] = x2 * lax.rsqrt(ms + EPS) * gf_ref[...]


def _mlp_and_cast(h2, w_up, w_down, x1, gf, tf):
    m = h2.shape[0]
    full = lambda f: (0, 0)
    return pl.pallas_call(
        _mlp_cast_body,
        grid=(D_FF // tf,),
        in_specs=[
            pl.BlockSpec((m, D_MODEL), full),
            pl.BlockSpec((D_MODEL, tf), lambda f: (0, f)),
            pl.BlockSpec((tf, D_MODEL), lambda f: (f, 0)),
            pl.BlockSpec((m, D_MODEL), full),
            pl.BlockSpec((1, D_MODEL), full),
        ],
        out_specs=(
            pl.BlockSpec((m, D_MODEL), full),
            pl.BlockSpec((D_MODEL, tf), lambda f: (0, f)),
            pl.BlockSpec((tf, D_MODEL), lambda f: (f, 0)),
        ),
        out_shape=(
            jax.ShapeDtypeStruct((m, D_MODEL), F32),
            jax.ShapeDtypeStruct((D_MODEL, D_FF), BF16),
            jax.ShapeDtypeStruct((D_FF, D_MODEL), BF16),
        ),
        scratch_shapes=[pltpu.VMEM((m, D_MODEL), F32)],
        compiler_params=_cparams(("arbitrary",)),
        name="mlp_cast",
    )(h2, w_up, w_down, x1, gf)


def _mlp(h2, wu, wd, x1, gf, tm, tf):
    m = h2.shape[0]
    in_specs = [
        pl.BlockSpec((tm, D_MODEL), lambda i, f: (i, 0)),
        pl.BlockSpec((D_MODEL, tf), lambda i, f: (0, f)),
        pl.BlockSpec((tf, D_MODEL), lambda i, f: (f, 0)),
        pl.BlockSpec((tm, D_MODEL), lambda i, f: (i, 0)),
        pl.BlockSpec((1, D_MODEL), lambda i, f: (0, 0)),
    ]
    return pl.pallas_call(
        _mlp_body,
        grid=(m // tm, D_FF // tf),
        in_specs=in_specs,
        out_specs=pl.BlockSpec((tm, D_MODEL), lambda i, f: (i, 0)),
        out_shape=jax.ShapeDtypeStruct((m, D_MODEL), F32),
        scratch_shapes=[pltpu.VMEM((tm, D_MODEL), F32)],
        compiler_params=_cparams(("arbitrary", "arbitrary")),
        name="mlp",
    )(h2, wu, wd, x1, gf)


def _fetch_pages(pt_ref, step, slot, streams, start):
    n_pages = pt_ref.shape[1]
    for hbm, buf, sem in streams:
        for j in range(n_pages):
            cp = pltpu.make_async_copy(hbm.at[pt_ref[step, j]], buf.at[slot, j], sem.at[slot])
            if start:
                cp.start()
            else:
                cp.wait()


def _paged_loop(pt_ref, streams, step_fn):
    nb = pt_ref.shape[0]
    n_slots = streams[0][1].shape[0]
    ahead = n_slots - 1
    for s in range(ahead):
        _fetch_pages(pt_ref, s, s, streams, start=True)

    def body(b, carry):
        @pl.when(b + ahead < nb)
        def _next():
            _fetch_pages(pt_ref, b + ahead, (b + ahead) % n_slots, streams, start=True)

        slot = b % n_slots
        _fetch_pages(pt_ref, b, slot, streams, start=False)
        step_fn(b, slot)
        return carry

    lax.fori_loop(0, nb, body, 0)


def _sidx_body(pt_ref, qi_ref, w_ref, kin_ref, cache_hbm, out_ref, kt_s, pbuf, sem):
    n_pages = pt_ref.shape[1]
    page = pbuf.shape[3]
    past = n_pages * page
    lane = lax.broadcasted_iota(I32, (1, LANES), 1)

    def step(b, slot):
        qi = qi_ref[b]
        w = w_ref[b] * (IDX_HEAD_DIM ** -0.5)
        for j in range(n_pages):
            kt_s[:, j * page:(j + 1) * page] = pbuf[slot, j].astype(BF16)
        s = jnp.dot(qi, kt_s[...], preferred_element_type=F32)
        out_ref[b, :, 0:past] = jnp.sum(jnp.maximum(s, 0.0) * w, axis=0, keepdims=True)
        sn = jnp.sum(qi.astype(F32) * kin_ref[b].astype(BF16).astype(F32), axis=1, keepdims=True)
        rn = jnp.sum(jnp.maximum(sn, 0.0) * w, axis=0, keepdims=True)
        out_ref[b, :, past:past + LANES] = jnp.where(lane == 0, rn, -jnp.inf)

    _paged_loop(pt_ref, ((cache_hbm, pbuf, sem),), step)


def _sample_index_scores(page_table, qi_s, wi_s, ki_s, cache_idx_k_t):
    nb, n_pages = page_table.shape
    page = cache_idx_k_t.shape[2]
    width = n_pages * page + LANES

    vmem = pl.BlockSpec(memory_space=pltpu.VMEM)
    return pl.pallas_call(
        _sidx_body,
        in_specs=[pl.BlockSpec(memory_space=pltpu.SMEM), vmem, vmem, vmem,
                  pl.BlockSpec(memory_space=pl.ANY)],
        out_specs=vmem,
        out_shape=jax.ShapeDtypeStruct((nb, 1, width), F32),
        scratch_shapes=[pltpu.VMEM((IDX_HEAD_DIM, n_pages * page), BF16),
                        pltpu.VMEM((PAGE_SLOTS, n_pages, IDX_HEAD_DIM, page), F32),
                        pltpu.SemaphoreType.DMA((PAGE_SLOTS,))],
        compiler_params=pltpu.CompilerParams(vmem_limit_bytes=VMEM_LIMIT),
        name="sample_idx",
    )(page_table, qi_s, wi_s, ki_s, cache_idx_k_t)


def _ssel_body(sc_ref, tri_ref, sel_ref):
    rows, width = sc_ref.shape
    nt = width // LANES
    n_valid = (nt - 1) * LANES + 1

    def tile(kt):
        return sc_ref[:, kt * LANES:(kt + 1) * LANES]

    def count_cmp(cmp):
        acc = jnp.zeros((rows, LANES), F32)
        for kt in range(nt):
            acc = acc + jnp.where(cmp(tile(kt)), 1.0, 0.0)
        return jnp.broadcast_to(jnp.sum(acc, axis=1, keepdims=True), (rows, LANES))

    tf = _threshold_search(lambda c: count_cmp(lambda sc: sc >= c), 32, (rows, LANES))
    need = float(TOPK_MAX) - count_cmp(lambda sc: sc > tf)
    tie_off = jnp.zeros((rows, LANES), F32)
    for kt in range(nt):
        col = kt * LANES + lax.broadcasted_iota(I32, (rows, LANES), 1)
        sc = tile(kt)
        eq = sc == tf
        tie = jnp.where(eq, 1.0, 0.0)
        rank = jnp.dot(tie.astype(BF16), tri_ref[...], preferred_element_type=F32) + tie_off
        sel = ((sc > tf) | (eq & (rank <= need))) & (col < n_valid)
        sel_ref[:, kt * LANES:(kt + 1) * LANES] = jnp.where(sel, 1.0, 0.0)
        tie_off = tie_off + jnp.broadcast_to(jnp.sum(tie, axis=1, keepdims=True), (rows, LANES))


def _sample_select(scores2d, tri):
    rows, width = scores2d.shape
    return pl.pallas_call(
        _ssel_body,
        out_shape=jax.ShapeDtypeStruct((rows, width), F32),
        compiler_params=pltpu.CompilerParams(vmem_limit_bytes=VMEM_LIMIT),
        name="sample_select",
    )(scores2d, tri)


def _sattn_body(pt_ref, q_ref, sel_ref, kn_ref, vn_ref, ck_hbm, cv_hbm, o_ref,
                kbuf, vbuf, ksem, vsem):
    n_pages = pt_ref.shape[1]
    page = kbuf.shape[2]
    past = n_pages * page
    scale = ATTN_HEAD_DIM ** -0.5

    def step(b, slot):
        q = q_ref[b]
        k_all = kbuf[slot].reshape(past, ATTN_HEAD_DIM).astype(BF16)
        v_all = vbuf[slot].reshape(past, ATTN_HEAD_DIM).astype(BF16)
        s = lax.dot_general(q, k_all, (((1,), (1,)), ((), ())), preferred_element_type=F32)
        s = jnp.where(sel_ref[b, :, 0:past] > 0.5, s * scale, NEG_BIG)
        kn = kn_ref[b].astype(BF16).astype(F32)
        sn = jnp.sum(q.astype(F32) * kn, axis=1, keepdims=True) * scale
        sn = jnp.where(sel_ref[b, :, past:past + 1] > 0.5, sn, NEG_BIG)
        m = jnp.maximum(jnp.max(s, axis=1, keepdims=True), sn)
        pn = jnp.exp(sn - m)
        p = jnp.exp(s - m)
        l = pn + jnp.sum(p, axis=1, keepdims=True)
        acc = (pn * vn_ref[b].astype(BF16).astype(F32)
               + jnp.dot(p.astype(BF16), v_all, preferred_element_type=F32))
        o_ref[b] = (acc / l).astype(BF16)

    _paged_loop(pt_ref, ((ck_hbm, kbuf, ksem), (cv_hbm, vbuf, vsem)), step)


def _sample_attention(page_table, qa_s, sel3, ka_s, va_s, cache_k, cache_v):
    nb, n_pages = page_table.shape
    page = cache_k.shape[1]

    vmem = pl.BlockSpec(memory_space=pltpu.VMEM)
    hbm = pl.BlockSpec(memory_space=pl.ANY)
    return pl.pallas_call(
        _sattn_body,
        in_specs=[pl.BlockSpec(memory_space=pltpu.SMEM), vmem, vmem, vmem, vmem, hbm, hbm],
        out_specs=vmem,
        out_shape=jax.ShapeDtypeStruct((nb, ATTN_HEADS, ATTN_HEAD_DIM), BF16),
        scratch_shapes=[pltpu.VMEM((PAGE_SLOTS, n_pages, page, ATTN_HEAD_DIM), F32),
                        pltpu.VMEM((PAGE_SLOTS, n_pages, page, ATTN_HEAD_DIM), F32),
                        pltpu.SemaphoreType.DMA((PAGE_SLOTS,)),
                        pltpu.SemaphoreType.DMA((PAGE_SLOTS,))],
        compiler_params=pltpu.CompilerParams(vmem_limit_bytes=VMEM_LIMIT),
        name="sample_attn",
    )(page_table, qa_s, sel3, ka_s, va_s, cache_k, cache_v)


def _sret_body(qkvg_ref, st_ref, gam_ref, rg_ref, so_ref):
    ns = st_ref.shape[0]
    for s in range(ns):
        blk = qkvg_ref[s].astype(F32)
        q8 = blk[0:8]
        k8 = blk[8:16]
        v8 = blk[16:24]
        g8 = blk[24:32]
        q_t = q8.T
        k_t = k8.T
        qk = jnp.sum(q8 * k8, axis=1, keepdims=True)
        rows = []
        for h in range(RET_HEADS):
            r_old = st_ref[s, h]
            gam = gam_ref[h]
            qcol = jnp.broadcast_to(q_t[:, h:h + 1], (RET_DK, RET_DV))
            kcol = jnp.broadcast_to(k_t[:, h:h + 1], (RET_DK, RET_DV))
            vrow = v8[h:h + 1]
            q_r = jnp.sum(qcol * r_old, axis=0, keepdims=True)
            rows.append(gam * q_r + qk[h:h + 1] * vrow)
            so_ref[s, h] = gam * r_old + kcol * vrow
        ret = jnp.concatenate(rows, axis=0)
        rg_ref[s] = _gate(ret, g8).astype(BF16)


def _sample_retention(qkvg, state, gam, ns):
    nb = state.shape[0]
    return pl.pallas_call(
        _sret_body,
        grid=(nb // ns,),
        in_specs=[
            pl.BlockSpec((ns, 32, LANES), lambda i: (i, 0, 0)),
            pl.BlockSpec((ns, RET_HEADS, RET_DK, RET_DV), lambda i: (i, 0, 0, 0)),
            pl.BlockSpec((RET_HEADS, 1, LANES), lambda i: (0, 0, 0)),
        ],
        out_specs=(
            pl.BlockSpec((ns, RET_HEADS, RET_DV), lambda i: (i, 0, 0)),
            pl.BlockSpec((ns, RET_HEADS, RET_DK, RET_DV), lambda i: (i, 0, 0, 0)),
        ),
        out_shape=(
            jax.ShapeDtypeStruct((nb, RET_HEADS, RET_DV), BF16),
            jax.ShapeDtypeStruct(state.shape, F32),
        ),
        compiler_params=_cparams(("arbitrary",)),
        name="sample_ret",
    )(qkvg, state, gam)


def _rotary_table(pos):
    half = RET_DK // 2
    inv = ROPE_BASE ** (-np.arange(half, dtype=np.float64) / half)
    ang = np.asarray(pos, np.float64)[:, None] * inv[None, :]
    cos = np.cos(ang)
    sin = np.sin(ang)
    return jnp.asarray(np.concatenate([cos, cos, -sin, sin], axis=1), F32)


def _retention_constants():
    lg = np.log1p(-np.exp2(-5.0 - np.arange(RET_HEADS, dtype=np.float64)))
    n = RET_CHUNK
    i = np.arange(n, dtype=np.float64)
    diff = i[:, None] - i[None, :]
    decay = np.where(diff[None] >= 0, np.exp(np.maximum(diff, 0.0)[None] * lg[:, None, None]), 0.0)
    rsc = np.exp((i + 1.0)[None, :] * lg[:, None])
    zeta = np.exp((n - 1.0 - i)[None, :] * lg[:, None])
    gpow = np.exp(n * lg)
    gam1 = np.exp(lg)
    rsc_b = np.broadcast_to(rsc[:, :, None], (RET_HEADS, n, RET_DV))
    zeta_b = np.broadcast_to(zeta[:, :, None], (RET_HEADS, n, RET_DK))
    gpow_b = np.broadcast_to(gpow[:, None, None], (RET_HEADS, 1, RET_DV))
    gam1_b = np.broadcast_to(gam1[:, None, None], (RET_HEADS, 1, LANES))
    return tuple(jnp.asarray(a, F32) for a in (decay, rsc_b, zeta_b, gpow_b, gam1_b))


def _upper_tri(n):
    return jnp.asarray(np.triu(np.ones((n, n), np.float32)), BF16)


def _lower_tri(n):
    return jnp.asarray(np.tril(np.ones((n, n), np.float32)), BF16)


def _pad_lanes(v):
    return jnp.pad(v, (0, LANES - v.shape[0])).reshape(1, LANES)


def kernel(x_prompt, x_sample, cache_k, cache_v, cache_idx_k, state_ret, page_table,
           norm1_g, w_in, idx_k_norm_g, idx_k_norm_b, w_out, norm2_g, w_up, w_down, final_norm_g):
    batch, seq, _ = x_prompt.shape
    nb = x_sample.shape[0]
    past_len = page_table.shape[1] * cache_k.shape[1]
    half_mix = ATTN_HEADS * ATTN_HEAD_DIM

    wt = w_in.T.astype(BF16)
    wa = w_out[:half_mix].astype(BF16)
    wr = w_out[half_mix:].astype(BF16)
    g1 = norm1_g.reshape(1, D_MODEL)
    g2 = norm2_g.reshape(1, D_MODEL)
    gf = final_norm_g.reshape(1, D_MODEL)
    lng = _pad_lanes(idx_k_norm_g)
    lnb = _pad_lanes(idx_k_norm_b)
    decay, rsc_b, zeta_b, gpow_b, gam1_b = _retention_constants()

    xp = x_prompt.reshape(batch * seq, D_MODEL)
    cs_p = _rotary_table(np.arange(seq))
    qa_p, qi_p, ka_p, va_p, ki_p, kd_p, wi_p = _project_attn(
        xp, g1, wt, lng, lnb, tm=TILES["proj_attn_rows"])
    main_p = _project_ret(xp, g1, wt, cs_p, tm=TILES["proj_ret_rows"])
    attn_p = _prompt_attention(qa_p, qi_p, wi_p, ka_p, va_p, kd_p, _lower_tri(KEY_TILE),
                               batch, seq)
    rg_p, ret_state_p = _prompt_retention(main_p, decay, rsc_b, zeta_b, gpow_b, batch, seq)
    x1_p, h2_p = _out_projection(attn_p, rg_p, wa, wr, xp, g2, tm=TILES["out_proj_rows"])

    assert nb >= PAGE_SLOTS
    xs = x_sample.reshape(nb, D_MODEL)
    cs_s = _rotary_table(np.full((nb,), past_len))
    qa_s, qi_s, ka_s, va_s, ki_s, _, wi_s = _project_attn(xs, g1, wt, lng, lnb, tm=nb)
    main_s = _project_ret(xs, g1, wt, cs_s, tm=nb)
    scores = _sample_index_scores(
        page_table,
        qi_s.transpose(1, 0, 2).reshape(nb, IDX_HEADS, IDX_HEAD_DIM),
        wi_s.T.reshape(nb, IDX_HEADS, 1),
        ki_s.reshape(nb, 1, IDX_HEAD_DIM),
        jnp.swapaxes(cache_idx_k, 1, 2))
    width = scores.shape[2]
    sel = _sample_select(scores.reshape(nb, width), _upper_tri(LANES))
    attn_s = _sample_attention(
        page_table,
        qa_s.transpose(1, 0, 2),
        sel.reshape(nb, 1, width),
        ka_s.reshape(nb, 1, ATTN_HEAD_DIM),
        va_s.reshape(nb, 1, ATTN_HEAD_DIM),
        cache_k, cache_v)
    rg_s, ret_state_s = _sample_retention(main_s.reshape(nb, 32, LANES), state_ret, gam1_b,
                                          ns=TILES["sample_ret_rows"])
    x1_s, h2_s = _out_projection(attn_s.reshape(nb, half_mix), rg_s.reshape(nb, RET_WIDTH),
                                 wa, wr, xs, g2, tm=nb)
    y_s, wu, wd = _mlp_and_cast(h2_s, w_up, w_down, x1_s, gf, tf=TILES["mlp_cast_ff"])
    y_p = _mlp(h2_p, wu, wd, x1_p, gf, tm=TILES["mlp_rows"], tf=TILES["mlp_ff"])

    return (
        y_p.reshape(batch, seq, D_MODEL),
        y_s.reshape(nb, 1, D_MODEL),
        ka_p.reshape(batch, seq, ATTN_HEAD_DIM),
        va_p.reshape(batch, seq, ATTN_HEAD_DIM),
        ki_p.reshape(batch, seq, IDX_HEAD_DIM),
        ret_state_p,
        ka_s.reshape(nb, 1, ATTN_HEAD_DIM),
        va_s.reshape(nb, 1, ATTN_HEAD_DIM),
        ki_s.reshape(nb, 1, IDX_HEAD_DIM),
        ret_state_s,
    )
```

```python
import functools

import numpy as np
import jax
import jax.numpy as jnp
from jax import lax
from jax.experimental import pallas as pl
from jax.experimental.pallas import tpu as pltpu

F32 = jnp.float32
BF16 = jnp.bfloat16
I32 = jnp.int32

D_MODEL = 2048
ATTN_HEADS = 8
ATTN_HEAD_DIM = 128
IDX_HEADS = 16
IDX_HEAD_DIM = 64
TOPK_MAX = 256
RET_HEADS = 8
RET_DK = 128
RET_DV = 128
RET_CHUNK = 256
ROPE_BASE = 10000.0
D_FF = 4 * D_MODEL
EPS = 1e-6
Q_BLOCK = 256

OFF_QA, OFF_KA, OFF_VA, OFF_QI, OFF_KI, OFF_WI = 0, 1024, 1152, 1280, 2304, 2368
OFF_QR, OFF_KR, OFF_VR, OFF_GR = 2384, 3408, 4432, 5456
RET_WIDTH = RET_HEADS * RET_DV

LANES = 128
PROJ_TILE = 512
KEY_TILE = 256
COUNT_ROWS = 64
PAGE_SLOTS = 4
SUM_ROWS = 16
LOG2_E = 1.4426950408889634
INT_MIN = -2 ** 31
KEY_NEG_INF = -2 ** 31 + 0x7FFFFF
BF16_KEY_NEG_INF = -2 ** 15 + 0x7F
NEG_BIG = -1e30
VMEM_LIMIT = 56 * 1024 * 1024

TILES = {
    "proj_attn_rows": 1024,
    "proj_ret_rows": 512,
    "out_proj_rows": 512,
    "mlp_rows": 512,
    "mlp_ff": 1024,
    "mlp_cast_ff": 512,
    "sample_ret_rows": 8,
}


def _cparams(sem):
    return pltpu.CompilerParams(dimension_semantics=sem, vmem_limit_bytes=VMEM_LIMIT)


def _resident(shape):
    zeros = (0,) * len(shape)
    return pl.BlockSpec(shape, lambda *_: zeros, pipeline_mode=pl.Buffered(1))


def _normed_input(x_ref, g_ref, xn_ref):
    x = x_ref[...]
    ms = jnp.mean(x * x, axis=-1, keepdims=True)
    xn_ref[...] = (x * lax.rsqrt(ms + EPS) * g_ref[...]).astype(BF16)


def _matmul_rows(xn_ref, wt_ref, r0, n):
    return lax.dot_general(xn_ref[...], wt_ref[r0:r0 + n, :], (((1,), (1,)), ((), ())),
                           preferred_element_type=F32)


def _proj_attn_body(x_ref, g_ref, wt_ref, lng_ref, lnb_ref,
                    qa_ref, qi_ref, ka_ref, va_ref, ki_ref, kd_ref, wi_ref, xn_ref):
    _normed_input(x_ref, g_ref, xn_ref)
    mm = functools.partial(_matmul_rows, xn_ref, wt_ref)
    for t in range(ATTN_HEADS * ATTN_HEAD_DIM // PROJ_TILE):
        acc = mm(OFF_QA + t * PROJ_TILE, PROJ_TILE)
        for hh in range(4):
            qa_ref[4 * t + hh] = acc[:, hh * LANES:(hh + 1) * LANES].astype(BF16)
    for t in range(IDX_HEADS * IDX_HEAD_DIM // PROJ_TILE):
        acc = mm(OFF_QI + t * PROJ_TILE, PROJ_TILE)
        for hh in range(4):
            qi_ref[4 * t + hh] = acc[:, hh * LANES:(hh + 1) * LANES].astype(BF16)
    kv = mm(OFF_KA, 2 * ATTN_HEAD_DIM)
    ka_ref[...] = kv[:, :ATTN_HEAD_DIM]
    va_ref[...] = kv[:, ATTN_HEAD_DIM:]
    kw = mm(OFF_KI, LANES)
    lane = lax.broadcasted_iota(I32, kw.shape, 1)
    is_k = lane < IDX_HEAD_DIM
    mu = jnp.sum(jnp.where(is_k, kw, 0.0), axis=-1, keepdims=True) * (1.0 / IDX_HEAD_DIM)
    d = jnp.where(is_k, kw - mu, 0.0)
    var = jnp.sum(d * d, axis=-1, keepdims=True) * (1.0 / IDX_HEAD_DIM)
    kn = d * lax.rsqrt(var + EPS) * lng_ref[...] + lnb_ref[...]
    ki_ref[...] = kn[:, :IDX_HEAD_DIM]
    kd_ref[...] = jnp.where(is_k, kn, pltpu.roll(kn, IDX_HEAD_DIM, 1)).astype(BF16)
    wi_ref[...] = (kw[:, IDX_HEAD_DIM:IDX_HEAD_DIM + IDX_HEADS] * (IDX_HEADS ** -0.5)).T


def _proj_ret_body(x_ref, g_ref, wt_ref, cs_ref, main_ref, xn_ref):
    _normed_input(x_ref, g_ref, xn_ref)
    base = 0
    cosf = cs_ref[:, :LANES]
    sinf = cs_ref[:, LANES:]
    tiles = RET_WIDTH // PROJ_TILE
    for seg, (off, scale) in enumerate(((OFF_QR, None), (OFF_KR, RET_DK ** -0.5))):
        for t in range(tiles):
            acc = _matmul_rows(xn_ref, wt_ref, off - base + t * PROJ_TILE, PROJ_TILE)
            for hh in range(PROJ_TILE // LANES):
                xh = acc[:, hh * LANES:(hh + 1) * LANES]
                r = xh * cosf + pltpu.roll(xh, RET_DK // 2, 1) * sinf
                if scale is not None:
                    r = r * scale
                c0 = seg * RET_WIDTH + t * PROJ_TILE + hh * LANES
                main_ref[:, c0:c0 + LANES] = r.astype(BF16)
    for seg, off in ((2, OFF_VR), (3, OFF_GR)):
        for t in range(tiles):
            acc = _matmul_rows(xn_ref, wt_ref, off - base + t * PROJ_TILE, PROJ_TILE)
            c0 = seg * RET_WIDTH + t * PROJ_TILE
            main_ref[:, c0:c0 + PROJ_TILE] = acc.astype(BF16)


def _project_attn(x2d, g1, wt_attn, lng, lnb, tm):
    m = x2d.shape[0]
    row = lambda i: (i, 0)
    out_shape = (
        jax.ShapeDtypeStruct((ATTN_HEADS, m, ATTN_HEAD_DIM), BF16),
        jax.ShapeDtypeStruct((IDX_HEADS // 2, m, LANES), BF16),
        jax.ShapeDtypeStruct((m, ATTN_HEAD_DIM), F32),
        jax.ShapeDtypeStruct((m, ATTN_HEAD_DIM), F32),
        jax.ShapeDtypeStruct((m, IDX_HEAD_DIM), F32),
        jax.ShapeDtypeStruct((m, LANES), BF16),
        jax.ShapeDtypeStruct((IDX_HEADS, m), F32),
    )
    out_specs = (
        pl.BlockSpec((ATTN_HEADS, tm, ATTN_HEAD_DIM), lambda i: (0, i, 0)),
        pl.BlockSpec((IDX_HEADS // 2, tm, LANES), lambda i: (0, i, 0)),
        pl.BlockSpec((tm, ATTN_HEAD_DIM), row),
        pl.BlockSpec((tm, ATTN_HEAD_DIM), row),
        pl.BlockSpec((tm, IDX_HEAD_DIM), row),
        pl.BlockSpec((tm, LANES), row),
        pl.BlockSpec((IDX_HEADS, tm), lambda i: (0, i)),
    )
    return pl.pallas_call(
        _proj_attn_body,
        grid=(m // tm,),
        in_specs=[pl.BlockSpec((tm, D_MODEL), row), _resident((1, D_MODEL)),
                  _resident((OFF_QR + LANES, D_MODEL)), _resident((1, LANES)),
                  _resident((1, LANES))],
        out_specs=out_specs,
        out_shape=out_shape,
        scratch_shapes=[pltpu.VMEM((tm, D_MODEL), BF16)],
        compiler_params=_cparams(("arbitrary",)),
        name="proj_attn",
    )(x2d, g1, wt_attn, lng, lnb)


def _project_ret(x2d, g1, wt_ret, cs, tm):
    m = x2d.shape[0]
    n_pos_blocks = cs.shape[0] // tm
    row = lambda i: (i, 0)
    return pl.pallas_call(
        _proj_ret_body,
        grid=(m // tm,),
        in_specs=[pl.BlockSpec((tm, D_MODEL), row), _resident((1, D_MODEL)),
                  _resident(wt_ret.shape),
                  pl.BlockSpec((tm, 2 * LANES), lambda i: (i % n_pos_blocks, 0))],
        out_specs=pl.BlockSpec((tm, 4 * RET_WIDTH), row),
        out_shape=jax.ShapeDtypeStruct((m, 4 * RET_WIDTH), BF16),
        scratch_shapes=[pltpu.VMEM((tm, D_MODEL), BF16)],
        compiler_params=_cparams(("arbitrary",)),
        name="proj_ret",
    )(x2d, g1, wt_ret, cs)


def _key_to_float(key):
    bits = key ^ ((key >> 31) & 0x7FFFFFFF)
    return lax.bitcast_convert_type(bits, F32)


def _threshold_search(count_ge, n_iter, shape):
    def body(it, t):
        bit = lax.shift_left(jnp.int32(1), 31 - it)
        cand = t ^ bit
        cnt = count_ge(_key_to_float(cand))
        return jnp.where(cnt >= float(TOPK_MAX), cand, t)

    t = lax.fori_loop(0, n_iter, body, jnp.full(shape, INT_MIN, I32))
    return _key_to_float(jnp.maximum(t, KEY_NEG_INF))


def _bf16_key_to_f32_key(k16):
    return lax.shift_left(k16, 16) | jnp.where(k16 < 0, 0xFFFF, 0)


def _threshold_search_coarse_fine(count_ge_bf16, count_ge, run, shape):
    def coarse(it, u):
        cand = u | lax.shift_left(jnp.int32(1), 15 - it)
        c = _key_to_float(_bf16_key_to_f32_key(cand - 32768)).astype(BF16)
        return jnp.where(count_ge_bf16(c) >= float(TOPK_MAX), cand, u)

    u = lax.fori_loop(0, jnp.where(run, 16, 0), coarse, jnp.zeros(shape, I32))
    k1 = jnp.maximum(u - 32768, BF16_KEY_NEG_INF)
    lo = _bf16_key_to_f32_key(jnp.maximum(k1 - 1, -32768))
    hi = _bf16_key_to_f32_key(jnp.minimum(k1 + 1, 32767))

    def fine(it, t):
        cand = t + lax.shift_left(jnp.int32(1), 16 - it)
        ok = (cand < hi) & (count_ge(_key_to_float(cand)) >= float(TOPK_MAX))
        return jnp.where(ok, cand, t)

    t = lax.fori_loop(0, jnp.where(run, 17, 0), fine, lo)
    return _key_to_float(jnp.maximum(t, KEY_NEG_INF))


def _attn_body(qa_ref, qi_ref, wit_ref, ka_ref, va_ref, kd_ref, tri_ref, o_ref,
               kbf, vtb, scr, mrun, acc_s, kmax, scr16):
    qb = pl.program_id(1)
    n_heads_q = ATTN_HEADS * Q_BLOCK
    n_pairs = IDX_HEADS // 2
    dv = ATTN_HEAD_DIM
    logit_scale = ATTN_HEAD_DIM ** -0.5 * LOG2_E

    @pl.when(qb == 0)
    def _cast():
        ka = ka_ref[...]
        kbf[...] = ka.astype(BF16)
        kmax[...] = jnp.broadcast_to(jnp.max(jnp.sum(ka * ka, axis=1, keepdims=True)), kmax.shape)
        for kt in range(vtb.shape[0]):
            vtb[kt, :dv] = va_ref[kt * KEY_TILE:(kt + 1) * KEY_TILE, :].T.astype(BF16)
            vtb[kt, dv:] = jnp.ones((vtb.shape[1] - dv, KEY_TILE), BF16)

    nk = ((qb + 1) * Q_BLOCK + KEY_TILE - 1) // KEY_TILE
    wt = wit_ref[...] * (IDX_HEAD_DIM ** -0.5)
    qi2 = qi_ref[...].reshape(n_pairs * Q_BLOCK, LANES)
    lo_half = lax.broadcasted_iota(I32, (KEY_TILE, LANES), 1) < IDX_HEAD_DIM
    qidx = qb * Q_BLOCK + lax.broadcasted_iota(I32, (KEY_TILE, Q_BLOCK), 1)
    kidx0 = lax.broadcasted_iota(I32, (KEY_TILE, Q_BLOCK), 0)
    contract_last = (((1,), (1,)), ((), ()))

    def idx_body(kt, carry):
        off = pl.multiple_of(kt * KEY_TILE, KEY_TILE)
        kit = kd_ref[pl.ds(off, KEY_TILE), :]
        zero = jnp.zeros_like(kit)
        s_even = lax.dot_general(jnp.where(lo_half, kit, zero), qi2, contract_last,
                                 preferred_element_type=F32)
        s_odd = lax.dot_general(jnp.where(lo_half, zero, kit), qi2, contract_last,
                                preferred_element_type=F32)
        score = jnp.zeros((KEY_TILE, Q_BLOCK), F32)
        for g in range(n_pairs):
            cs = slice(g * Q_BLOCK, (g + 1) * Q_BLOCK)
            score = score + jnp.maximum(s_even[:, cs], 0.0) * wt[2 * g:2 * g + 1, :]
            score = score + jnp.maximum(s_odd[:, cs], 0.0) * wt[2 * g + 1:2 * g + 2, :]
        score = jnp.where(kidx0 + off <= qidx, score, -jnp.inf)
        scr[kt] = score
        scr16[kt] = score.astype(BF16)
        return carry

    def for_tiles(fn):
        def pair(j, carry):
            fn(2 * j, 0)
            fn(2 * j + 1, 0)
            return carry

        lax.fori_loop(0, nk // 2, pair, 0)

        @pl.when(nk % 2 == 1)
        def _last():
            fn(nk - 1, 0)

    for_tiles(idx_body)

    def count_ge_bf16(c):
        def body(kt, acc):
            hit = scr16[kt] >= c
            for r in range(KEY_TILE // COUNT_ROWS):
                acc = jnp.where(hit[r * COUNT_ROWS:(r + 1) * COUNT_ROWS], acc + 1.0, acc)
            return acc
        acc = lax.fori_loop(0, nk, body, jnp.zeros((COUNT_ROWS, Q_BLOCK), BF16))
        return jnp.sum(acc.astype(F32), axis=0, keepdims=True)

    def count_cmp(cmp):
        def body(kt, acc):
            hit = cmp(scr[kt])
            for r in range(KEY_TILE // COUNT_ROWS):
                acc = jnp.where(hit[r * COUNT_ROWS:(r + 1) * COUNT_ROWS], acc + 1.0, acc)
            return acc
        acc = lax.fori_loop(0, nk, body, jnp.zeros((COUNT_ROWS, Q_BLOCK), F32))
        return jnp.sum(acc, axis=0, keepdims=True)

    tf = _threshold_search_coarse_fine(
        count_ge_bf16, lambda c: count_cmp(lambda sc: sc >= c),
        qb >= TOPK_MAX // Q_BLOCK, (1, Q_BLOCK))
    need = float(TOPK_MAX) - count_cmp(lambda sc: sc > tf)
    excess = jnp.max(count_cmp(lambda sc: sc >= tf)) > float(TOPK_MAX)

    qa2 = qa_ref[...].reshape(n_heads_q, ATTN_HEAD_DIM)

    def logits(kt):
        off = pl.multiple_of(kt * KEY_TILE, KEY_TILE)
        s = lax.dot_general(kbf[pl.ds(off, KEY_TILE), :], qa2, contract_last,
                            preferred_element_type=F32)
        return s * logit_scale

    def sel_plain(kt, carry):
        off = pl.multiple_of(kt * KEY_TILE, KEY_TILE)
        return (scr[kt] >= tf) & (kidx0 + off <= qidx), carry

    def sel_ties(kt, tie_off):
        off = pl.multiple_of(kt * KEY_TILE, KEY_TILE)
        sc = scr[kt]
        eq = sc == tf
        tie = jnp.where(eq, 1.0, 0.0)
        rank = jnp.dot(tri_ref[...], tie.astype(BF16), preferred_element_type=F32) + tie_off
        sel = ((sc > tf) | (eq & (rank <= need))) & (kidx0 + off <= qidx)
        return sel, tie_off + jnp.sum(tie, axis=0, keepdims=True)

    no_ties = jnp.zeros((1, Q_BLOCK), F32)

    def softmax_sum(m, sel_fn):
        acc_s[...] = jnp.zeros(acc_s.shape, F32)

        def body(kt, carry):
            sel, carry = sel_fn(kt, carry)
            e = jnp.exp2(logits(kt) - m)
            parts = []
            for h in range(ATTN_HEADS):
                cs = slice(h * Q_BLOCK, (h + 1) * Q_BLOCK)
                parts.append(jnp.where(sel, e[:, cs], 0.0).astype(BF16))
            p = jnp.concatenate(parts, axis=1)
            acc_s[...] += jnp.dot(vtb[kt], p, preferred_element_type=F32)
            return carry

        if sel_fn is sel_plain:
            for_tiles(body)
        else:
            lax.fori_loop(0, nk, body, no_ties)

    def selected_max(sel_fn):
        mrun[...] = jnp.full(mrun.shape, NEG_BIG, F32)

        def body(kt, carry):
            sel, carry = sel_fn(kt, carry)
            s = logits(kt)
            for h in range(ATTN_HEADS):
                cs = slice(h * Q_BLOCK, (h + 1) * Q_BLOCK)
                sh = jnp.where(sel, s[:, cs], NEG_BIG)
                mrun[:, cs] = jnp.maximum(
                    mrun[:, cs], jnp.max(sh.reshape(KEY_TILE // 8, 8, Q_BLOCK), axis=0))
            return carry

        lax.fori_loop(0, nk, body, no_ties)
        return jnp.max(mrun[...], axis=0, keepdims=True)

    def fast_path():
        q2 = (qa2 * qa2).astype(BF16)
        qsq = lax.dot_general(jnp.ones((8, ATTN_HEAD_DIM), BF16), q2, contract_last,
                              preferred_element_type=F32)[0:1]
        softmax_sum(jnp.sqrt(qsq * kmax[0:1, 0:1]) * logit_scale, sel_plain)
        return (jnp.min(acc_s[dv:dv + 1, :]) > 0.0).astype(I32)

    done = lax.cond(excess, lambda: jnp.int32(0), fast_path) == 1

    @pl.when(jnp.logical_not(done) & excess)
    def _exact_with_ties():
        softmax_sum(selected_max(sel_ties), sel_ties)

    @pl.when(jnp.logical_not(done) & jnp.logical_not(excess))
    def _exact_without_ties():
        softmax_sum(selected_max(sel_plain), sel_plain)

    out = acc_s[:dv, :] / acc_s[dv:dv + 1, :]
    for h in range(ATTN_HEADS):
        oh = out[:, h * Q_BLOCK:(h + 1) * Q_BLOCK].T
        o_ref[:, h * ATTN_HEAD_DIM:(h + 1) * ATTN_HEAD_DIM] = oh.astype(BF16)


def _prompt_attention(qa_hm, qi_pm, wi_t, ka, va, kd, tri, batch, seq):
    nq = seq // Q_BLOCK
    nkt = seq // KEY_TILE
    m = batch * seq
    n_heads_q = ATTN_HEADS * Q_BLOCK
    in_specs = [
        pl.BlockSpec((ATTN_HEADS, Q_BLOCK, ATTN_HEAD_DIM), lambda b, q: (0, b * nq + q, 0)),
        pl.BlockSpec((IDX_HEADS // 2, Q_BLOCK, LANES), lambda b, q: (0, b * nq + q, 0)),
        pl.BlockSpec((IDX_HEADS, Q_BLOCK), lambda b, q: (0, b * nq + q)),
        pl.BlockSpec((seq, ATTN_HEAD_DIM), lambda b, q: (b, 0)),
        pl.BlockSpec((seq, ATTN_HEAD_DIM), lambda b, q: (b, 0)),
        pl.BlockSpec((seq, LANES), lambda b, q: (b, 0)),
        pl.BlockSpec((KEY_TILE, KEY_TILE), lambda b, q: (0, 0)),
    ]
    return pl.pallas_call(
        _attn_body,
        grid=(batch, nq),
        in_specs=in_specs,
        out_specs=pl.BlockSpec((Q_BLOCK, ATTN_HEADS * ATTN_HEAD_DIM), lambda b, q: (b * nq + q, 0)),
        out_shape=jax.ShapeDtypeStruct((m, ATTN_HEADS * ATTN_HEAD_DIM), BF16),
        scratch_shapes=[
            pltpu.VMEM((seq, ATTN_HEAD_DIM), BF16),
            pltpu.VMEM((nkt, ATTN_HEAD_DIM + SUM_ROWS, KEY_TILE), BF16),
            pltpu.VMEM((nkt, KEY_TILE, Q_BLOCK), F32),
            pltpu.VMEM((8, n_heads_q), F32),
            pltpu.VMEM((ATTN_HEAD_DIM + SUM_ROWS, n_heads_q), F32),
            pltpu.VMEM((8, LANES), F32),
            pltpu.VMEM((nkt, KEY_TILE, Q_BLOCK), BF16),
        ],
        compiler_params=_cparams(("arbitrary", "arbitrary")),
        name="prompt_attn",
    )(qa_hm, qi_pm, wi_t, ka, va, kd, tri)


def _gate(o, g):
    rn = o * lax.rsqrt(jnp.mean(o * o, axis=-1, keepdims=True) + EPS)
    return rn * (g / (1.0 + jnp.exp(-g)))


def _ret_body(q_ref, k_ref, v_ref, g_ref, decay_ref, rsc_ref, zeta_ref, gpow_ref,
              rg_ref, st_ref):
    c = pl.program_id(1)

    @pl.when(c == 0)
    def _init():
        st_ref[...] = jnp.zeros(st_ref.shape, F32)

    for h in range(RET_HEADS):
        sl = slice(h * 128, (h + 1) * 128)
        q = q_ref[:, sl]
        k = k_ref[:, sl]
        v = v_ref[:, sl]
        r_old = st_ref[0, h]
        qk = lax.dot_general(q, k, (((1,), (1,)), ((), ())), preferred_element_type=F32)
        inner = jnp.dot((qk * decay_ref[h]).astype(BF16), v, preferred_element_type=F32)
        cross = jnp.dot(q, r_old.astype(BF16), preferred_element_type=F32) * rsc_ref[h]
        kz = (k.astype(F32) * zeta_ref[h]).astype(BF16)
        upd = lax.dot_general(kz, v, (((0,), (0,)), ((), ())), preferred_element_type=F32)
        st_ref[0, h] = r_old * gpow_ref[h] + upd
        rg_ref[:, sl] = _gate(inner + cross, g_ref[:, sl].astype(F32)).astype(BF16)


def _prompt_retention(main, decay, rsc, zeta, gpow, batch, seq):
    nc = seq // RET_CHUNK
    m = batch * seq
    width = RET_WIDTH
    const3 = lambda b, c: (0, 0, 0)
    in_specs = [
        pl.BlockSpec((RET_CHUNK, width), lambda b, c: (b * nc + c, 0)),
        pl.BlockSpec((RET_CHUNK, width), lambda b, c: (b * nc + c, 1)),
        pl.BlockSpec((RET_CHUNK, width), lambda b, c: (b * nc + c, 2)),
        pl.BlockSpec((RET_CHUNK, width), lambda b, c: (b * nc + c, 3)),
        pl.BlockSpec((RET_HEADS, RET_CHUNK, RET_CHUNK), const3),
        pl.BlockSpec((RET_HEADS, RET_CHUNK, RET_DV), const3),
        pl.BlockSpec((RET_HEADS, RET_CHUNK, RET_DK), const3),
        pl.BlockSpec((RET_HEADS, 1, RET_DV), const3),
    ]
    return pl.pallas_call(
        _ret_body,
        grid=(batch, nc),
        in_specs=in_specs,
        out_specs=(
            pl.BlockSpec((RET_CHUNK, width), lambda b, c: (b * nc + c, 0)),
            pl.BlockSpec((1, RET_HEADS, RET_DK, RET_DV), lambda b, c: (b, 0, 0, 0)),
        ),
        out_shape=(
            jax.ShapeDtypeStruct((m, width), BF16),
            jax.ShapeDtypeStruct((batch, RET_HEADS, RET_DK, RET_DV), F32),
        ),
        compiler_params=_cparams(("arbitrary", "arbitrary")),
        name="prompt_ret",
    )(main, main, main, main, decay, rsc, zeta, gpow)


def _outproj_body(a_ref, r_ref, wa_ref, wr_ref, x_ref, g2_ref, x1_ref, h2_ref):
    mixed = (jnp.dot(a_ref[...], wa_ref[...], preferred_element_type=F32)
             + jnp.dot(r_ref[...], wr_ref[...], preferred_element_type=F32))
    x1 = x_ref[...] + mixed
    x1_ref[...] = x1
    ms = jnp.mean(x1 * x1, axis=-1, keepdims=True)
    h2_ref[...] = (x1 * lax.rsqrt(ms + EPS) * g2_ref[...]).astype(BF16)


def _out_projection(attn_o, rg, wa, wr, x2d, g2, tm):
    m = x2d.shape[0]
    half = attn_o.shape[1]
    in_specs = [
        pl.BlockSpec((tm, half), lambda i: (i, 0)),
        pl.BlockSpec((tm, half), lambda i: (i, 0)),
        pl.BlockSpec((half, D_MODEL), lambda i: (0, 0)),
        pl.BlockSpec((half, D_MODEL), lambda i: (0, 0)),
        pl.BlockSpec((tm, D_MODEL), lambda i: (i, 0)),
        pl.BlockSpec((1, D_MODEL), lambda i: (0, 0)),
    ]
    return pl.pallas_call(
        _outproj_body,
        grid=(m // tm,),
        in_specs=in_specs,
        out_specs=(pl.BlockSpec((tm, D_MODEL), lambda i: (i, 0)),
                   pl.BlockSpec((tm, D_MODEL), lambda i: (i, 0))),
        out_shape=(jax.ShapeDtypeStruct((m, D_MODEL), F32),
                   jax.ShapeDtypeStruct((m, D_MODEL), BF16)),
        compiler_params=_cparams(("arbitrary",)),
        name="out_proj",
    )(attn_o, rg, wa, wr, x2d, g2)


def _mlp_body(h2_ref, wu_ref, wd_ref, x1_ref, gf_ref, y_ref, acc_ref):
    f = pl.program_id(1)

    @pl.when(f == 0)
    def _init():
        acc_ref[...] = x1_ref[...]

    u = jnp.dot(h2_ref[...], wu_ref[...], preferred_element_type=F32)
    a = jnp.maximum(u, 0.0)
    acc_ref[...] += jnp.dot((a * a).astype(BF16), wd_ref[...], preferred_element_type=F32)

    @pl.when(f == pl.num_programs(1) - 1)
    def _final():
        x2 = acc_ref[...]
        ms = jnp.mean(x2 * x2, axis=-1, keepdims=True)
        y_ref[...] = x2 * lax.rsqrt(ms + EPS) * gf_ref[...]


def _mlp_cast_body(h2_ref, wu_ref, wd_ref, x1_ref, gf_ref, y_ref, wub_ref, wdb_ref, acc_ref):
    f = pl.program_id(0)

    @pl.when(f == 0)
    def _init():
        acc_ref[...] = x1_ref[...]

    wu = wu_ref[...].astype(BF16)
    wd = wd_ref[...].astype(BF16)
    wub_ref[...] = wu
    wdb_ref[...] = wd
    u = jnp.dot(h2_ref[...], wu, preferred_element_type=F32)
    a = jnp.maximum(u, 0.0)
    acc_ref[...] += jnp.dot((a * a).astype(BF16), wd, preferred_element_type=F32)

    @pl.when(f == pl.num_programs(0) - 1)
    def _final():
        x2 = acc_ref[...]
        ms = jnp.mean(x2 * x2, axis=-1, keepdims=True)
        y_ref[...] = x2 * lax.rsqrt(ms + EPS) * gf_ref[...]


def _mlp_and_cast(h2, w_up, w_down, x1, gf, tf):
    m = h2.shape[0]
    full = lambda f: (0, 0)
    return pl.pallas_call(
        _mlp_cast_body,
        grid=(D_FF // tf,),
        in_specs=[
            pl.BlockSpec((m, D_MODEL), full),
            pl.BlockSpec((D_MODEL, tf), lambda f: (0, f)),
            pl.BlockSpec((tf, D_MODEL), lambda f: (f, 0)),
            pl.BlockSpec((m, D_MODEL), full),
            pl.BlockSpec((1, D_MODEL), full),
        ],
        out_specs=(
            pl.BlockSpec((m, D_MODEL), full),
            pl.BlockSpec((D_MODEL, tf), lambda f: (0, f)),
            pl.BlockSpec((tf, D_MODEL), lambda f: (f, 0)),
        ),
        out_shape=(
            jax.ShapeDtypeStruct((m, D_MODEL), F32),
            jax.ShapeDtypeStruct((D_MODEL, D_FF), BF16),
            jax.ShapeDtypeStruct((D_FF, D_MODEL), BF16),
        ),
        scratch_shapes=[pltpu.VMEM((m, D_MODEL), F32)],
        compiler_params=_cparams(("arbitrary",)),
        name="mlp_cast",
    )(h2, w_up, w_down, x1, gf)


def _mlp(h2, wu, wd, x1, gf, tm, tf):
    m = h2.shape[0]
    in_specs = [
        pl.BlockSpec((tm, D_MODEL), lambda i, f: (i, 0)),
        pl.BlockSpec((D_MODEL, tf), lambda i, f: (0, f)),
        pl.BlockSpec((tf, D_MODEL), lambda i, f: (f, 0)),
        pl.BlockSpec((tm, D_MODEL), lambda i, f: (i, 0)),
        pl.BlockSpec((1, D_MODEL), lambda i, f: (0, 0)),
    ]
    return pl.pallas_call(
        _mlp_body,
        grid=(m // tm, D_FF // tf),
        in_specs=in_specs,
        out_specs=pl.BlockSpec((tm, D_MODEL), lambda i, f: (i, 0)),
        out_shape=jax.ShapeDtypeStruct((m, D_MODEL), F32),
        scratch_shapes=[pltpu.VMEM((tm, D_MODEL), F32)],
        compiler_params=_cparams(("arbitrary", "arbitrary")),
        name="mlp",
    )(h2, wu, wd, x1, gf)


def _fetch_pages(pt_ref, step, slot, streams, start):
    n_pages = pt_ref.shape[1]
    for hbm, buf, sem in streams:
        for j in range(n_pages):
            cp = pltpu.make_async_copy(hbm.at[pt_ref[step, j]], buf.at[slot, j], sem.at[slot])
            if start:
                cp.start()
            else:
                cp.wait()


def _paged_loop(pt_ref, streams, step_fn):
    nb = pt_ref.shape[0]
    n_slots = streams[0][1].shape[0]
    ahead = n_slots - 1
    for s in range(ahead):
        _fetch_pages(pt_ref, s, s, streams, start=True)

    def body(b, carry):
        @pl.when(b + ahead < nb)
        def _next():
            _fetch_pages(pt_ref, b + ahead, (b + ahead) % n_slots, streams, start=True)

        slot = b % n_slots
        _fetch_pages(pt_ref, b, slot, streams, start=False)
        step_fn(b, slot)
        return carry

    lax.fori_loop(0, nb, body, 0)


def _sidx_body(pt_ref, qi_ref, w_ref, kin_ref, cache_hbm, out_ref, kt_s, pbuf, sem):
    n_pages = pt_ref.shape[1]
    page = pbuf.shape[3]
    past = n_pages * page
    lane = lax.broadcasted_iota(I32, (1, LANES), 1)

    def step(b, slot):
        qi = qi_ref[b]
        w = w_ref[b] * (IDX_HEAD_DIM ** -0.5)
        for j in range(n_pages):
            kt_s[:, j * page:(j + 1) * page] = pbuf[slot, j].astype(BF16)
        s = jnp.dot(qi, kt_s[...], preferred_element_type=F32)
        out_ref[b, :, 0:past] = jnp.sum(jnp.maximum(s, 0.0) * w, axis=0, keepdims=True)
        sn = jnp.sum(qi.astype(F32) * kin_ref[b].astype(BF16).astype(F32), axis=1, keepdims=True)
        rn = jnp.sum(jnp.maximum(sn, 0.0) * w, axis=0, keepdims=True)
        out_ref[b, :, past:past + LANES] = jnp.where(lane == 0, rn, -jnp.inf)

    _paged_loop(pt_ref, ((cache_hbm, pbuf, sem),), step)


def _sample_index_scores(page_table, qi_s, wi_s, ki_s, cache_idx_k_t):
    nb, n_pages = page_table.shape
    page = cache_idx_k_t.shape[2]
    width = n_pages * page + LANES

    vmem = pl.BlockSpec(memory_space=pltpu.VMEM)
    return pl.pallas_call(
        _sidx_body,
        in_specs=[pl.BlockSpec(memory_space=pltpu.SMEM), vmem, vmem, vmem,
                  pl.BlockSpec(memory_space=pl.ANY)],
        out_specs=vmem,
        out_shape=jax.ShapeDtypeStruct((nb, 1, width), F32),
        scratch_shapes=[pltpu.VMEM((IDX_HEAD_DIM, n_pages * page), BF16),
                        pltpu.VMEM((PAGE_SLOTS, n_pages, IDX_HEAD_DIM, page), F32),
                        pltpu.SemaphoreType.DMA((PAGE_SLOTS,))],
        compiler_params=pltpu.CompilerParams(vmem_limit_bytes=VMEM_LIMIT),
        name="sample_idx",
    )(page_table, qi_s, wi_s, ki_s, cache_idx_k_t)


def _ssel_body(sc_ref, tri_ref, sel_ref):
    rows, width = sc_ref.shape
    nt = width // LANES
    n_valid = (nt - 1) * LANES + 1

    def tile(kt):
        return sc_ref[:, kt * LANES:(kt + 1) * LANES]

    def count_cmp(cmp):
        acc = jnp.zeros((rows, LANES), F32)
        for kt in range(nt):
            acc = acc + jnp.where(cmp(tile(kt)), 1.0, 0.0)
        return jnp.broadcast_to(jnp.sum(acc, axis=1, keepdims=True), (rows, LANES))

    tf = _threshold_search(lambda c: count_cmp(lambda sc: sc >= c), 32, (rows, LANES))
    need = float(TOPK_MAX) - count_cmp(lambda sc: sc > tf)
    tie_off = jnp.zeros((rows, LANES), F32)
    for kt in range(nt):
        col = kt * LANES + lax.broadcasted_iota(I32, (rows, LANES), 1)
        sc = tile(kt)
        eq = sc == tf
        tie = jnp.where(eq, 1.0, 0.0)
        rank = jnp.dot(tie.astype(BF16), tri_ref[...], preferred_element_type=F32) + tie_off
        sel = ((sc > tf) | (eq & (rank <= need))) & (col < n_valid)
        sel_ref[:, kt * LANES:(kt + 1) * LANES] = jnp.where(sel, 1.0, 0.0)
        tie_off = tie_off + jnp.broadcast_to(jnp.sum(tie, axis=1, keepdims=True), (rows, LANES))


def _sample_select(scores2d, tri):
    rows, width = scores2d.shape
    return pl.pallas_call(
        _ssel_body,
        out_shape=jax.ShapeDtypeStruct((rows, width), F32),
        compiler_params=pltpu.CompilerParams(vmem_limit_bytes=VMEM_LIMIT),
        name="sample_select",
    )(scores2d, tri)


def _sattn_body(pt_ref, q_ref, sel_ref, kn_ref, vn_ref, ck_hbm, cv_hbm, o_ref,
                kbuf, vbuf, ksem, vsem):
    n_pages = pt_ref.shape[1]
    page = kbuf.shape[2]
    past = n_pages * page
    scale = ATTN_HEAD_DIM ** -0.5

    def step(b, slot):
        q = q_ref[b]
        k_all = kbuf[slot].reshape(past, ATTN_HEAD_DIM).astype(BF16)
        v_all = vbuf[slot].reshape(past, ATTN_HEAD_DIM).astype(BF16)
        s = lax.dot_general(q, k_all, (((1,), (1,)), ((), ())), preferred_element_type=F32)
        s = jnp.where(sel_ref[b, :, 0:past] > 0.5, s * scale, NEG_BIG)
        kn = kn_ref[b].astype(BF16).astype(F32)
        sn = jnp.sum(q.astype(F32) * kn, axis=1, keepdims=True) * scale
        sn = jnp.where(sel_ref[b, :, past:past + 1] > 0.5, sn, NEG_BIG)
        m = jnp.maximum(jnp.max(s, axis=1, keepdims=True), sn)
        pn = jnp.exp(sn - m)
        p = jnp.exp(s - m)
        l = pn + jnp.sum(p, axis=1, keepdims=True)
        acc = (pn * vn_ref[b].astype(BF16).astype(F32)
               + jnp.dot(p.astype(BF16), v_all, preferred_element_type=F32))
        o_ref[b] = (acc / l).astype(BF16)

    _paged_loop(pt_ref, ((ck_hbm, kbuf, ksem), (cv_hbm, vbuf, vsem)), step)


def _sample_attention(page_table, qa_s, sel3, ka_s, va_s, cache_k, cache_v):
    nb, n_pages = page_table.shape
    page = cache_k.shape[1]

    vmem = pl.BlockSpec(memory_space=pltpu.VMEM)
    hbm = pl.BlockSpec(memory_space=pl.ANY)
    return pl.pallas_call(
        _sattn_body,
        in_specs=[pl.BlockSpec(memory_space=pltpu.SMEM), vmem, vmem, vmem, vmem, hbm, hbm],
        out_specs=vmem,
        out_shape=jax.ShapeDtypeStruct((nb, ATTN_HEADS, ATTN_HEAD_DIM), BF16),
        scratch_shapes=[pltpu.VMEM((PAGE_SLOTS, n_pages, page, ATTN_HEAD_DIM), F32),
                        pltpu.VMEM((PAGE_SLOTS, n_pages, page, ATTN_HEAD_DIM), F32),
                        pltpu.SemaphoreType.DMA((PAGE_SLOTS,)),
                        pltpu.SemaphoreType.DMA((PAGE_SLOTS,))],
        compiler_params=pltpu.CompilerParams(vmem_limit_bytes=VMEM_LIMIT),
        name="sample_attn",
    )(page_table, qa_s, sel3, ka_s, va_s, cache_k, cache_v)


def _sret_body(qkvg_ref, st_ref, gam_ref, rg_ref, so_ref):
    ns = st_ref.shape[0]
    for s in range(ns):
        blk = qkvg_ref[s].astype(F32)
        q8 = blk[0:8]
        k8 = blk[8:16]
        v8 = blk[16:24]
        g8 = blk[24:32]
        q_t = q8.T
        k_t = k8.T
        qk = jnp.sum(q8 * k8, axis=1, keepdims=True)
        rows = []
        for h in range(RET_HEADS):
            r_old = st_ref[s, h]
            gam = gam_ref[h]
            qcol = jnp.broadcast_to(q_t[:, h:h + 1], (RET_DK, RET_DV))
            kcol = jnp.broadcast_to(k_t[:, h:h + 1], (RET_DK, RET_DV))
            vrow = v8[h:h + 1]
            q_r = jnp.sum(qcol * r_old, axis=0, keepdims=True)
            rows.append(gam * q_r + qk[h:h + 1] * vrow)
            so_ref[s, h] = gam * r_old + kcol * vrow
        ret = jnp.concatenate(rows, axis=0)
        rg_ref[s] = _gate(ret, g8).astype(BF16)


def _sample_retention(qkvg, state, gam, ns):
    nb = state.shape[0]
    return pl.pallas_call(
        _sret_body,
        grid=(nb // ns,),
        in_specs=[
            pl.BlockSpec((ns, 32, LANES), lambda i: (i, 0, 0)),
            pl.BlockSpec((ns, RET_HEADS, RET_DK, RET_DV), lambda i: (i, 0, 0, 0)),
            pl.BlockSpec((RET_HEADS, 1, LANES), lambda i: (0, 0, 0)),
        ],
        out_specs=(
            pl.BlockSpec((ns, RET_HEADS, RET_DV), lambda i: (i, 0, 0)),
            pl.BlockSpec((ns, RET_HEADS, RET_DK, RET_DV), lambda i: (i, 0, 0, 0)),
        ),
        out_shape=(
            jax.ShapeDtypeStruct((nb, RET_HEADS, RET_DV), BF16),
            jax.ShapeDtypeStruct(state.shape, F32),
        ),
        compiler_params=_cparams(("arbitrary",)),
        name="sample_ret",
    )(qkvg, state, gam)


def _rotary_table(pos):
    half = RET_DK // 2
    inv = ROPE_BASE ** (-np.arange(half, dtype=np.float64) / half)
    ang = np.asarray(pos, np.float64)[:, None] * inv[None, :]
    cos = np.cos(ang)
    sin = np.sin(ang)
    return jnp.asarray(np.concatenate([cos, cos, -sin, sin], axis=1), F32)


def _retention_constants():
    lg = np.log1p(-np.exp2(-5.0 - np.arange(RET_HEADS, dtype=np.float64)))
    n = RET_CHUNK
    i = np.arange(n, dtype=np.float64)
    diff = i[:, None] - i[None, :]
    decay = np.where(diff[None] >= 0, np.exp(np.maximum(diff, 0.0)[None] * lg[:, None, None]), 0.0)
    rsc = np.exp((i + 1.0)[None, :] * lg[:, None])
    zeta = np.exp((n - 1.0 - i)[None, :] * lg[:, None])
    gpow = np.exp(n * lg)
    gam1 = np.exp(lg)
    rsc_b = np.broadcast_to(rsc[:, :, None], (RET_HEADS, n, RET_DV))
    zeta_b = np.broadcast_to(zeta[:, :, None], (RET_HEADS, n, RET_DK))
    gpow_b = np.broadcast_to(gpow[:, None, None], (RET_HEADS, 1, RET_DV))
    gam1_b = np.broadcast_to(gam1[:, None, None], (RET_HEADS, 1, LANES))
    return tuple(jnp.asarray(a, F32) for a in (decay, rsc_b, zeta_b, gpow_b, gam1_b))


def _upper_tri(n):
    return jnp.asarray(np.triu(np.ones((n, n), np.float32)), BF16)


def _lower_tri(n):
    return jnp.asarray(np.tril(np.ones((n, n), np.float32)), BF16)


def _pad_lanes(v):
    return jnp.pad(v, (0, LANES - v.shape[0])).reshape(1, LANES)


def kernel(x_prompt, x_sample, cache_k, cache_v, cache_idx_k, state_ret, page_table,
           norm1_g, w_in, idx_k_norm_g, idx_k_norm_b, w_out, norm2_g, w_up, w_down, final_norm_g):
    batch, seq, _ = x_prompt.shape
    nb = x_sample.shape[0]
    past_len = page_table.shape[1] * cache_k.shape[1]
    half_mix = ATTN_HEADS * ATTN_HEAD_DIM

    wt = w_in.T.astype(BF16)
    wa = w_out[:half_mix].astype(BF16)
    wr = w_out[half_mix:].astype(BF16)
    g1 = norm1_g.reshape(1, D_MODEL)
    g2 = norm2_g.reshape(1, D_MODEL)
    gf = final_norm_g.reshape(1, D_MODEL)
    lng = _pad_lanes(idx_k_norm_g)
    lnb = _pad_lanes(idx_k_norm_b)
    decay, rsc_b, zeta_b, gpow_b, gam1_b = _retention_constants()

    xp = x_prompt.reshape(batch * seq, D_MODEL)
    cs_p = _rotary_table(np.arange(seq))
    qa_p, qi_p, ka_p, va_p, ki_p, kd_p, wi_p = _project_attn(
        xp, g1, wt, lng, lnb, tm=TILES["proj_attn_rows"])
    main_p = _project_ret(xp, g1, wt, cs_p, tm=TILES["proj_ret_rows"])
    attn_p = _prompt_attention(qa_p, qi_p, wi_p, ka_p, va_p, kd_p, _lower_tri(KEY_TILE),
                               batch, seq)
    rg_p, ret_state_p = _prompt_retention(main_p, decay, rsc_b, zeta_b, gpow_b, batch, seq)
    x1_p, h2_p = _out_projection(attn_p, rg_p, wa, wr, xp, g2, tm=TILES["out_proj_rows"])

    assert nb >= PAGE_SLOTS
    xs = x_sample.reshape(nb, D_MODEL)
    cs_s = _rotary_table(np.full((nb,), past_len))
    qa_s, qi_s, ka_s, va_s, ki_s, _, wi_s = _project_attn(xs, g1, wt, lng, lnb, tm=nb)
    main_s = _project_ret(xs, g1, wt, cs_s, tm=nb)
    scores = _sample_index_scores(
        page_table,
        qi_s.transpose(1, 0, 2).reshape(nb, IDX_HEADS, IDX_HEAD_DIM),
        wi_s.T.reshape(nb, IDX_HEADS, 1),
        ki_s.reshape(nb, 1, IDX_HEAD_DIM),
        jnp.swapaxes(cache_idx_k, 1, 2))
    width = scores.shape[2]
    sel = _sample_select(scores.reshape(nb, width), _upper_tri(LANES))
    attn_s = _sample_attention(
        page_table,
        qa_s.transpose(1, 0, 2),
        sel.reshape(nb, 1, width),
        ka_s.reshape(nb, 1, ATTN_HEAD_DIM),
        va_s.reshape(nb, 1, ATTN_HEAD_DIM),
        cache_k, cache_v)
    rg_s, ret_state_s = _sample_retention(main_s.reshape(nb, 32, LANES), state_ret, gam1_b,
                                          ns=TILES["sample_ret_rows"])
    x1_s, h2_s = _out_projection(attn_s.reshape(nb, half_mix), rg_s.reshape(nb, RET_WIDTH),
                                 wa, wr, xs, g2, tm=nb)
    y_s, wu, wd = _mlp_and_cast(h2_s, w_up, w_down, x1_s, gf, tf=TILES["mlp_cast_ff"])
    y_p = _mlp(h2_p, wu, wd, x1_p, gf, tm=TILES["mlp_rows"], tf=TILES["mlp_ff"])

    return (
        y_p.reshape(batch, seq, D_MODEL),
        y_s.reshape(nb, 1, D_MODEL),
        ka_p.reshape(batch, seq, ATTN_HEAD_DIM),
        va_p.reshape(batch, seq, ATTN_HEAD_DIM),
        ki_p.reshape(batch, seq, IDX_HEAD_DIM),
        ret_state_p,
        ka_s.reshape(nb, 1, ATTN_HEAD_DIM),
        va_s.reshape(nb, 1, ATTN_HEAD_DIM),
        ki_s.reshape(nb, 1, IDX_HEAD_DIM),
        ret_state_s,
    )
```

```python
import functools

import numpy as np
import jax
import jax.numpy as jnp
from jax import lax
from jax.experimental import pallas as pl
from jax.experimental.pallas import tpu as pltpu

F32 = jnp.float32
BF16 = jnp.bfloat16
I32 = jnp.int32

D_MODEL = 2048
ATTN_HEADS = 8
ATTN_HEAD_DIM = 128
IDX_HEADS = 16
IDX_HEAD_DIM = 64
TOPK_MAX = 256
RET_HEADS = 8
RET_DK = 128
RET_DV = 128
RET_CHUNK = 256
ROPE_BASE = 10000.0
D_FF = 4 * D_MODEL
EPS = 1e-6
Q_BLOCK = 256

OFF_QA, OFF_KA, OFF_VA, OFF_QI, OFF_KI, OFF_WI = 0, 1024, 1152, 1280, 2304, 2368
OFF_QR, OFF_KR, OFF_VR, OFF_GR = 2384, 3408, 4432, 5456
RET_WIDTH = RET_HEADS * RET_DV

LANES = 128
PROJ_TILE = 512
KEY_TILE = 256
COUNT_ROWS = 64
PAGE_GROUP = 2
IDX_PAGE_SLOTS = 8
ATTN_PAGE_SLOTS = 6
SUM_ROWS = 16
LOG2_E = 1.4426950408889634
INT_MIN = -2 ** 31
KEY_NEG_INF = -2 ** 31 + 0x7FFFFF
BF16_KEY_NEG_INF = -2 ** 15 + 0x7F
NEG_BIG = -1e30
VMEM_LIMIT = 56 * 1024 * 1024

TILES = {
    "proj_attn_rows": 1024,
    "proj_ret_rows": 512,
    "out_proj_rows": 512,
    "mlp_rows": 512,
    "mlp_ff": 1024,
    "mlp_cast_ff": 512,
    "sample_ret_rows": 8,
}


def _cparams(sem):
    return pltpu.CompilerParams(dimension_semantics=sem, vmem_limit_bytes=VMEM_LIMIT)


def _resident(shape):
    zeros = (0,) * len(shape)
    return pl.BlockSpec(shape, lambda *_: zeros, pipeline_mode=pl.Buffered(1))


def _normed_input(x_ref, g_ref, xn_ref):
    x = x_ref[...]
    ms = jnp.mean(x * x, axis=-1, keepdims=True)
    xn_ref[...] = (x * lax.rsqrt(ms + EPS) * g_ref[...]).astype(BF16)


def _matmul_rows(xn_ref, wt_ref, r0, n):
    return lax.dot_general(xn_ref[...], wt_ref[r0:r0 + n, :], (((1,), (1,)), ((), ())),
                           preferred_element_type=F32)


def _proj_attn_body(x_ref, g_ref, wt_ref, lng_ref, lnb_ref,
                    qa_ref, qi_ref, ka_ref, va_ref, ki_ref, kd_ref, wi_ref, xn_ref):
    _normed_input(x_ref, g_ref, xn_ref)
    mm = functools.partial(_matmul_rows, xn_ref, wt_ref)
    for t in range(ATTN_HEADS * ATTN_HEAD_DIM // PROJ_TILE):
        acc = mm(OFF_QA + t * PROJ_TILE, PROJ_TILE)
        for hh in range(4):
            qa_ref[4 * t + hh] = acc[:, hh * LANES:(hh + 1) * LANES].astype(BF16)
    for t in range(IDX_HEADS * IDX_HEAD_DIM // PROJ_TILE):
        acc = mm(OFF_QI + t * PROJ_TILE, PROJ_TILE)
        for hh in range(4):
            qi_ref[4 * t + hh] = acc[:, hh * LANES:(hh + 1) * LANES].astype(BF16)
    kv = mm(OFF_KA, 2 * ATTN_HEAD_DIM)
    ka_ref[...] = kv[:, :ATTN_HEAD_DIM]
    va_ref[...] = kv[:, ATTN_HEAD_DIM:]
    kw = mm(OFF_KI, LANES)
    lane = lax.broadcasted_iota(I32, kw.shape, 1)
    is_k = lane < IDX_HEAD_DIM
    mu = jnp.sum(jnp.where(is_k, kw, 0.0), axis=-1, keepdims=True) * (1.0 / IDX_HEAD_DIM)
    d = jnp.where(is_k, kw - mu, 0.0)
    var = jnp.sum(d * d, axis=-1, keepdims=True) * (1.0 / IDX_HEAD_DIM)
    kn = d * lax.rsqrt(var + EPS) * lng_ref[...] + lnb_ref[...]
    ki_ref[...] = kn[:, :IDX_HEAD_DIM]
    kd_ref[...] = jnp.where(is_k, kn, pltpu.roll(kn, IDX_HEAD_DIM, 1)).astype(BF16)
    wi_ref[...] = (kw[:, IDX_HEAD_DIM:IDX_HEAD_DIM + IDX_HEADS] * (IDX_HEADS ** -0.5)).T


def _proj_ret_body(x_ref, g_ref, wt_ref, cs_ref, main_ref, xn_ref):
    _normed_input(x_ref, g_ref, xn_ref)
    base = 0
    cosf = cs_ref[:, :LANES]
    sinf = cs_ref[:, LANES:]
    tiles = RET_WIDTH // PROJ_TILE
    for seg, (off, scale) in enumerate(((OFF_QR, None), (OFF_KR, RET_DK ** -0.5))):
        for t in range(tiles):
            acc = _matmul_rows(xn_ref, wt_ref, off - base + t * PROJ_TILE, PROJ_TILE)
            for hh in range(PROJ_TILE // LANES):
                xh = acc[:, hh * LANES:(hh + 1) * LANES]
                r = xh * cosf + pltpu.roll(xh, RET_DK // 2, 1) * sinf
                if scale is not None:
                    r = r * scale
                c0 = seg * RET_WIDTH + t * PROJ_TILE + hh * LANES
                main_ref[:, c0:c0 + LANES] = r.astype(BF16)
    for seg, off in ((2, OFF_VR), (3, OFF_GR)):
        for t in range(tiles):
            acc = _matmul_rows(xn_ref, wt_ref, off - base + t * PROJ_TILE, PROJ_TILE)
            c0 = seg * RET_WIDTH + t * PROJ_TILE
            main_ref[:, c0:c0 + PROJ_TILE] = acc.astype(BF16)


def _project_attn(x2d, g1, wt_attn, lng, lnb, tm):
    m = x2d.shape[0]
    row = lambda i: (i, 0)
    out_shape = (
        jax.ShapeDtypeStruct((ATTN_HEADS, m, ATTN_HEAD_DIM), BF16),
        jax.ShapeDtypeStruct((IDX_HEADS // 2, m, LANES), BF16),
        jax.ShapeDtypeStruct((m, ATTN_HEAD_DIM), F32),
        jax.ShapeDtypeStruct((m, ATTN_HEAD_DIM), F32),
        jax.ShapeDtypeStruct((m, IDX_HEAD_DIM), F32),
        jax.ShapeDtypeStruct((m, LANES), BF16),
        jax.ShapeDtypeStruct((IDX_HEADS, m), F32),
    )
    out_specs = (
        pl.BlockSpec((ATTN_HEADS, tm, ATTN_HEAD_DIM), lambda i: (0, i, 0)),
        pl.BlockSpec((IDX_HEADS // 2, tm, LANES), lambda i: (0, i, 0)),
        pl.BlockSpec((tm, ATTN_HEAD_DIM), row),
        pl.BlockSpec((tm, ATTN_HEAD_DIM), row),
        pl.BlockSpec((tm, IDX_HEAD_DIM), row),
        pl.BlockSpec((tm, LANES), row),
        pl.BlockSpec((IDX_HEADS, tm), lambda i: (0, i)),
    )
    return pl.pallas_call(
        _proj_attn_body,
        grid=(m // tm,),
        in_specs=[pl.BlockSpec((tm, D_MODEL), row), _resident((1, D_MODEL)),
                  _resident((OFF_QR + LANES, D_MODEL)), _resident((1, LANES)),
                  _resident((1, LANES))],
        out_specs=out_specs,
        out_shape=out_shape,
        scratch_shapes=[pltpu.VMEM((tm, D_MODEL), BF16)],
        compiler_params=_cparams(("arbitrary",)),
        name="proj_attn",
    )(x2d, g1, wt_attn, lng, lnb)


def _project_ret(x2d, g1, wt_ret, cs, tm):
    m = x2d.shape[0]
    n_pos_blocks = cs.shape[0] // tm
    row = lambda i: (i, 0)
    return pl.pallas_call(
        _proj_ret_body,
        grid=(m // tm,),
        in_specs=[pl.BlockSpec((tm, D_MODEL), row), _resident((1, D_MODEL)),
                  _resident(wt_ret.shape),
                  pl.BlockSpec((tm, 2 * LANES), lambda i: (i % n_pos_blocks, 0))],
        out_specs=pl.BlockSpec((tm, 4 * RET_WIDTH), row),
        out_shape=jax.ShapeDtypeStruct((m, 4 * RET_WIDTH), BF16),
        scratch_shapes=[pltpu.VMEM((tm, D_MODEL), BF16)],
        compiler_params=_cparams(("arbitrary",)),
        name="proj_ret",
    )(x2d, g1, wt_ret, cs)


def _key_to_float(key):
    bits = key ^ ((key >> 31) & 0x7FFFFFFF)
    return lax.bitcast_convert_type(bits, F32)


def _threshold_search(count_ge, n_iter, shape):
    def body(it, t):
        bit = lax.shift_left(jnp.int32(1), 31 - it)
        cand = t ^ bit
        cnt = count_ge(_key_to_float(cand))
        return jnp.where(cnt >= float(TOPK_MAX), cand, t)

    t = lax.fori_loop(0, n_iter, body, jnp.full(shape, INT_MIN, I32))
    return _key_to_float(jnp.maximum(t, KEY_NEG_INF))


def _bf16_key_to_f32_key(k16):
    return lax.shift_left(k16, 16) | jnp.where(k16 < 0, 0xFFFF, 0)


def _threshold_search_coarse_fine(count_ge_bf16, count_ge, run, shape):
    def coarse(it, u):
        cand = u | lax.shift_left(jnp.int32(1), 15 - it)
        c = _key_to_float(_bf16_key_to_f32_key(cand - 32768)).astype(BF16)
        return jnp.where(count_ge_bf16(c) >= float(TOPK_MAX), cand, u)

    u = lax.fori_loop(0, jnp.where(run, 16, 0), coarse, jnp.zeros(shape, I32))
    k1 = jnp.maximum(u - 32768, BF16_KEY_NEG_INF)
    lo = _bf16_key_to_f32_key(jnp.maximum(k1 - 1, -32768))
    hi = _bf16_key_to_f32_key(jnp.minimum(k1 + 1, 32767))

    def fine(it, t):
        cand = t + lax.shift_left(jnp.int32(1), 16 - it)
        ok = (cand < hi) & (count_ge(_key_to_float(cand)) >= float(TOPK_MAX))
        return jnp.where(ok, cand, t)

    t = lax.fori_loop(0, jnp.where(run, 17, 0), fine, lo)
    return _key_to_float(jnp.maximum(t, KEY_NEG_INF))


def _attn_body(qa_ref, qi_ref, wit_ref, ka_ref, va_ref, kd_ref, tri_ref, o_ref,
               kbf, vtb, scr, mrun, acc_s, kmax, scr16):
    qb = pl.program_id(1)
    n_heads_q = ATTN_HEADS * Q_BLOCK
    n_pairs = IDX_HEADS // 2
    dv = ATTN_HEAD_DIM
    logit_scale = ATTN_HEAD_DIM ** -0.5 * LOG2_E

    @pl.when(qb == 0)
    def _cast():
        ka = ka_ref[...]
        kbf[...] = ka.astype(BF16)
        kmax[...] = jnp.broadcast_to(jnp.max(jnp.sum(ka * ka, axis=1, keepdims=True)), kmax.shape)
        for kt in range(vtb.shape[0]):
            vtb[kt, :dv] = va_ref[kt * KEY_TILE:(kt + 1) * KEY_TILE, :].T.astype(BF16)
            vtb[kt, dv:] = jnp.ones((vtb.shape[1] - dv, KEY_TILE), BF16)

    nk = ((qb + 1) * Q_BLOCK + KEY_TILE - 1) // KEY_TILE
    wt = wit_ref[...] * (IDX_HEAD_DIM ** -0.5)
    qi2 = qi_ref[...].reshape(n_pairs * Q_BLOCK, LANES)
    lo_half = lax.broadcasted_iota(I32, (KEY_TILE, LANES), 1) < IDX_HEAD_DIM
    qidx = qb * Q_BLOCK + lax.broadcasted_iota(I32, (KEY_TILE, Q_BLOCK), 1)
    kidx0 = lax.broadcasted_iota(I32, (KEY_TILE, Q_BLOCK), 0)
    contract_last = (((1,), (1,)), ((), ()))

    def idx_body(kt, carry):
        off = pl.multiple_of(kt * KEY_TILE, KEY_TILE)
        kit = kd_ref[pl.ds(off, KEY_TILE), :]
        zero = jnp.zeros_like(kit)
        s_even = lax.dot_general(jnp.where(lo_half, kit, zero), qi2, contract_last,
                                 preferred_element_type=F32)
        s_odd = lax.dot_general(jnp.where(lo_half, zero, kit), qi2, contract_last,
                                preferred_element_type=F32)
        score = jnp.zeros((KEY_TILE, Q_BLOCK), F32)
        for g in range(n_pairs):
            cs = slice(g * Q_BLOCK, (g + 1) * Q_BLOCK)
            score = score + jnp.maximum(s_even[:, cs], 0.0) * wt[2 * g:2 * g + 1, :]
            score = score + jnp.maximum(s_odd[:, cs], 0.0) * wt[2 * g + 1:2 * g + 2, :]
        score = jnp.where(kidx0 + off <= qidx, score, -jnp.inf)
        scr[kt] = score
        scr16[kt] = score.astype(BF16)
        return carry

    def for_tiles(fn):
        def pair(j, carry):
            fn(2 * j, 0)
            fn(2 * j + 1, 0)
            return carry

        lax.fori_loop(0, nk // 2, pair, 0)

        @pl.when(nk % 2 == 1)
        def _last():
            fn(nk - 1, 0)

    for_tiles(idx_body)

    def count_ge_bf16(c):
        def body(kt, acc):
            hit = scr16[kt] >= c
            for r in range(KEY_TILE // COUNT_ROWS):
                acc = jnp.where(hit[r * COUNT_ROWS:(r + 1) * COUNT_ROWS], acc + 1.0, acc)
            return acc
        acc = lax.fori_loop(0, nk, body, jnp.zeros((COUNT_ROWS, Q_BLOCK), BF16))
        return jnp.sum(acc.astype(F32), axis=0, keepdims=True)

    def count_cmp(cmp):
        def body(kt, acc):
            hit = cmp(scr[kt])
            for r in range(KEY_TILE // COUNT_ROWS):
                acc = jnp.where(hit[r * COUNT_ROWS:(r + 1) * COUNT_ROWS], acc + 1.0, acc)
            return acc
        acc = lax.fori_loop(0, nk, body, jnp.zeros((COUNT_ROWS, Q_BLOCK), F32))
        return jnp.sum(acc, axis=0, keepdims=True)

    tf = _threshold_search_coarse_fine(
        count_ge_bf16, lambda c: count_cmp(lambda sc: sc >= c),
        qb >= TOPK_MAX // Q_BLOCK, (1, Q_BLOCK))
    need = float(TOPK_MAX) - count_cmp(lambda sc: sc > tf)
    excess = jnp.max(count_cmp(lambda sc: sc >= tf)) > float(TOPK_MAX)

    qa2 = qa_ref[...].reshape(n_heads_q, ATTN_HEAD_DIM)

    def logits(kt):
        off = pl.multiple_of(kt * KEY_TILE, KEY_TILE)
        s = lax.dot_general(kbf[pl.ds(off, KEY_TILE), :], qa2, contract_last,
                            preferred_element_type=F32)
        return s * logit_scale

    def sel_plain(kt, carry):
        off = pl.multiple_of(kt * KEY_TILE, KEY_TILE)
        return (scr[kt] >= tf) & (kidx0 + off <= qidx), carry

    def sel_ties(kt, tie_off):
        off = pl.multiple_of(kt * KEY_TILE, KEY_TILE)
        sc = scr[kt]
        eq = sc == tf
        tie = jnp.where(eq, 1.0, 0.0)
        rank = jnp.dot(tri_ref[...], tie.astype(BF16), preferred_element_type=F32) + tie_off
        sel = ((sc > tf) | (eq & (rank <= need))) & (kidx0 + off <= qidx)
        return sel, tie_off + jnp.sum(tie, axis=0, keepdims=True)

    no_ties = jnp.zeros((1, Q_BLOCK), F32)

    def softmax_sum(m, sel_fn):
        acc_s[...] = jnp.zeros(acc_s.shape, F32)

        def body(kt, carry):
            sel, carry = sel_fn(kt, carry)
            e = jnp.exp2(logits(kt) - m)
            parts = []
            for h in range(ATTN_HEADS):
                cs = slice(h * Q_BLOCK, (h + 1) * Q_BLOCK)
                parts.append(jnp.where(sel, e[:, cs], 0.0).astype(BF16))
            p = jnp.concatenate(parts, axis=1)
            acc_s[...] += jnp.dot(vtb[kt], p, preferred_element_type=F32)
            return carry

        if sel_fn is sel_plain:
            for_tiles(body)
        else:
            lax.fori_loop(0, nk, body, no_ties)

    def selected_max(sel_fn):
        mrun[...] = jnp.full(mrun.shape, NEG_BIG, F32)

        def body(kt, carry):
            sel, carry = sel_fn(kt, carry)
            s = logits(kt)
            for h in range(ATTN_HEADS):
                cs = slice(h * Q_BLOCK, (h + 1) * Q_BLOCK)
                sh = jnp.where(sel, s[:, cs], NEG_BIG)
                mrun[:, cs] = jnp.maximum(
                    mrun[:, cs], jnp.max(sh.reshape(KEY_TILE // 8, 8, Q_BLOCK), axis=0))
            return carry

        lax.fori_loop(0, nk, body, no_ties)
        return jnp.max(mrun[...], axis=0, keepdims=True)

    def fast_path():
        q2 = (qa2 * qa2).astype(BF16)
        qsq = lax.dot_general(jnp.ones((8, ATTN_HEAD_DIM), BF16), q2, contract_last,
                              preferred_element_type=F32)[0:1]
        softmax_sum(jnp.sqrt(qsq * kmax[0:1, 0:1]) * logit_scale, sel_plain)
        return (jnp.min(acc_s[dv:dv + 1, :]) > 0.0).astype(I32)

    done = lax.cond(excess, lambda: jnp.int32(0), fast_path) == 1

    @pl.when(jnp.logical_not(done) & excess)
    def _exact_with_ties():
        softmax_sum(selected_max(sel_ties), sel_ties)

    @pl.when(jnp.logical_not(done) & jnp.logical_not(excess))
    def _exact_without_ties():
        softmax_sum(selected_max(sel_plain), sel_plain)

    out = acc_s[:dv, :] / acc_s[dv:dv + 1, :]
    for h in range(ATTN_HEADS):
        oh = out[:, h * Q_BLOCK:(h + 1) * Q_BLOCK].T
        o_ref[:, h * ATTN_HEAD_DIM:(h + 1) * ATTN_HEAD_DIM] = oh.astype(BF16)


def _prompt_attention(qa_hm, qi_pm, wi_t, ka, va, kd, tri, batch, seq):
    nq = seq // Q_BLOCK
    nkt = seq // KEY_TILE
    m = batch * seq
    n_heads_q = ATTN_HEADS * Q_BLOCK
    in_specs = [
        pl.BlockSpec((ATTN_HEADS, Q_BLOCK, ATTN_HEAD_DIM), lambda b, q: (0, b * nq + q, 0)),
        pl.BlockSpec((IDX_HEADS // 2, Q_BLOCK, LANES), lambda b, q: (0, b * nq + q, 0)),
        pl.BlockSpec((IDX_HEADS, Q_BLOCK), lambda b, q: (0, b * nq + q)),
        pl.BlockSpec((seq, ATTN_HEAD_DIM), lambda b, q: (b, 0)),
        pl.BlockSpec((seq, ATTN_HEAD_DIM), lambda b, q: (b, 0)),
        pl.BlockSpec((seq, LANES), lambda b, q: (b, 0)),
        pl.BlockSpec((KEY_TILE, KEY_TILE), lambda b, q: (0, 0)),
    ]
    return pl.pallas_call(
        _attn_body,
        grid=(batch, nq),
        in_specs=in_specs,
        out_specs=pl.BlockSpec((Q_BLOCK, ATTN_HEADS * ATTN_HEAD_DIM), lambda b, q: (b * nq + q, 0)),
        out_shape=jax.ShapeDtypeStruct((m, ATTN_HEADS * ATTN_HEAD_DIM), BF16),
        scratch_shapes=[
            pltpu.VMEM((seq, ATTN_HEAD_DIM), BF16),
            pltpu.VMEM((nkt, ATTN_HEAD_DIM + SUM_ROWS, KEY_TILE), BF16),
            pltpu.VMEM((nkt, KEY_TILE, Q_BLOCK), F32),
            pltpu.VMEM((8, n_heads_q), F32),
            pltpu.VMEM((ATTN_HEAD_DIM + SUM_ROWS, n_heads_q), F32),
            pltpu.VMEM((8, LANES), F32),
            pltpu.VMEM((nkt, KEY_TILE, Q_BLOCK), BF16),
        ],
        compiler_params=_cparams(("arbitrary", "arbitrary")),
        name="prompt_attn",
    )(qa_hm, qi_pm, wi_t, ka, va, kd, tri)


def _gate(o, g):
    rn = o * lax.rsqrt(jnp.mean(o * o, axis=-1, keepdims=True) + EPS)
    return rn * (g / (1.0 + jnp.exp(-g)))


def _ret_body(q_ref, k_ref, v_ref, g_ref, decay_ref, rsc_ref, zeta_ref, gpow_ref,
              rg_ref, st_ref):
    c = pl.program_id(1)

    @pl.when(c == 0)
    def _init():
        st_ref[...] = jnp.zeros(st_ref.shape, F32)

    for h in range(RET_HEADS):
        sl = slice(h * 128, (h + 1) * 128)
        q = q_ref[:, sl]
        k = k_ref[:, sl]
        v = v_ref[:, sl]
        r_old = st_ref[0, h]
        qk = lax.dot_general(q, k, (((1,), (1,)), ((), ())), preferred_element_type=F32)
        inner = jnp.dot((qk * decay_ref[h]).astype(BF16), v, preferred_element_type=F32)
        cross = jnp.dot(q, r_old.astype(BF16), preferred_element_type=F32) * rsc_ref[h]
        kz = (k.astype(F32) * zeta_ref[h]).astype(BF16)
        upd = lax.dot_general(kz, v, (((0,), (0,)), ((), ())), preferred_element_type=F32)
        st_ref[0, h] = r_old * gpow_ref[h] + upd
        rg_ref[:, sl] = _gate(inner + cross, g_ref[:, sl].astype(F32)).astype(BF16)


def _prompt_retention(main, decay, rsc, zeta, gpow, batch, seq):
    nc = seq // RET_CHUNK
    m = batch * seq
    width = RET_WIDTH
    const3 = lambda b, c: (0, 0, 0)
    in_specs = [
        pl.BlockSpec((RET_CHUNK, width), lambda b, c: (b * nc + c, 0)),
        pl.BlockSpec((RET_CHUNK, width), lambda b, c: (b * nc + c, 1)),
        pl.BlockSpec((RET_CHUNK, width), lambda b, c: (b * nc + c, 2)),
        pl.BlockSpec((RET_CHUNK, width), lambda b, c: (b * nc + c, 3)),
        pl.BlockSpec((RET_HEADS, RET_CHUNK, RET_CHUNK), const3),
        pl.BlockSpec((RET_HEADS, RET_CHUNK, RET_DV), const3),
        pl.BlockSpec((RET_HEADS, RET_CHUNK, RET_DK), const3),
        pl.BlockSpec((RET_HEADS, 1, RET_DV), const3),
    ]
    return pl.pallas_call(
        _ret_body,
        grid=(batch, nc),
        in_specs=in_specs,
        out_specs=(
            pl.BlockSpec((RET_CHUNK, width), lambda b, c: (b * nc + c, 0)),
            pl.BlockSpec((1, RET_HEADS, RET_DK, RET_DV), lambda b, c: (b, 0, 0, 0)),
        ),
        out_shape=(
            jax.ShapeDtypeStruct((m, width), BF16),
            jax.ShapeDtypeStruct((batch, RET_HEADS, RET_DK, RET_DV), F32),
        ),
        compiler_params=_cparams(("arbitrary", "arbitrary")),
        name="prompt_ret",
    )(main, main, main, main, decay, rsc, zeta, gpow)


def _outproj_body(a_ref, r_ref, wa_ref, wr_ref, x_ref, g2_ref, x1_ref, h2_ref):
    mixed = (jnp.dot(a_ref[...], wa_ref[...], preferred_element_type=F32)
             + jnp.dot(r_ref[...], wr_ref[...], preferred_element_type=F32))
    x1 = x_ref[...] + mixed
    x1_ref[...] = x1
    ms = jnp.mean(x1 * x1, axis=-1, keepdims=True)
    h2_ref[...] = (x1 * lax.rsqrt(ms + EPS) * g2_ref[...]).astype(BF16)


def _out_projection(attn_o, rg, wa, wr, x2d, g2, tm):
    m = x2d.shape[0]
    half = attn_o.shape[1]
    in_specs = [
        pl.BlockSpec((tm, half), lambda i: (i, 0)),
        pl.BlockSpec((tm, half), lambda i: (i, 0)),
        pl.BlockSpec((half, D_MODEL), lambda i: (0, 0)),
        pl.BlockSpec((half, D_MODEL), lambda i: (0, 0)),
        pl.BlockSpec((tm, D_MODEL), lambda i: (i, 0)),
        pl.BlockSpec((1, D_MODEL), lambda i: (0, 0)),
    ]
    return pl.pallas_call(
        _outproj_body,
        grid=(m // tm,),
        in_specs=in_specs,
        out_specs=(pl.BlockSpec((tm, D_MODEL), lambda i: (i, 0)),
                   pl.BlockSpec((tm, D_MODEL), lambda i: (i, 0))),
        out_shape=(jax.ShapeDtypeStruct((m, D_MODEL), F32),
                   jax.ShapeDtypeStruct((m, D_MODEL), BF16)),
        compiler_params=_cparams(("arbitrary",)),
        name="out_proj",
    )(attn_o, rg, wa, wr, x2d, g2)


def _mlp_body(h2_ref, wu_ref, wd_ref, x1_ref, gf_ref, y_ref, acc_ref):
    f = pl.program_id(1)

    @pl.when(f == 0)
    def _init():
        acc_ref[...] = x1_ref[...]

    u = jnp.dot(h2_ref[...], wu_ref[...], preferred_element_type=F32)
    a = jnp.maximum(u, 0.0)
    acc_ref[...] += jnp.dot((a * a).astype(BF16), wd_ref[...], preferred_element_type=F32)

    @pl.when(f == pl.num_programs(1) - 1)
    def _final():
        x2 = acc_ref[...]
        ms = jnp.mean(x2 * x2, axis=-1, keepdims=True)
        y_ref[...] = x2 * lax.rsqrt(ms + EPS) * gf_ref[...]


def _mlp_cast_body(h2_ref, wu_ref, wd_ref, x1_ref, gf_ref, y_ref, wub_ref, wdb_ref, acc_ref):
    f = pl.program_id(0)

    @pl.when(f == 0)
    def _init():
        acc_ref[...] = x1_ref[...]

    wu = wu_ref[...].astype(BF16)
    wd = wd_ref[...].astype(BF16)
    wub_ref[...] = wu
    wdb_ref[...] = wd
    u = jnp.dot(h2_ref[...], wu, preferred_element_type=F32)
    a = jnp.maximum(u, 0.0)
    acc_ref[...] += jnp.dot((a * a).astype(BF16), wd, preferred_element_type=F32)

    @pl.when(f == pl.num_programs(0) - 1)
    def _final():
        x2 = acc_ref[...]
        ms = jnp.mean(x2 * x2, axis=-1, keepdims=True)
        y_ref[...] = x2 * lax.rsqrt(ms + EPS) * gf_ref[...]


def _mlp_and_cast(h2, w_up, w_down, x1, gf, tf):
    m = h2.shape[0]
    full = lambda f: (0, 0)
    return pl.pallas_call(
        _mlp_cast_body,
        grid=(D_FF // tf,),
        in_specs=[
            pl.BlockSpec((m, D_MODEL), full),
            pl.BlockSpec((D_MODEL, tf), lambda f: (0, f)),
            pl.BlockSpec((tf, D_MODEL), lambda f: (f, 0)),
            pl.BlockSpec((m, D_MODEL), full),
            pl.BlockSpec((1, D_MODEL), full),
        ],
        out_specs=(
            pl.BlockSpec((m, D_MODEL), full),
            pl.BlockSpec((D_MODEL, tf), lambda f: (0, f)),
            pl.BlockSpec((tf, D_MODEL), lambda f: (f, 0)),
        ),
        out_shape=(
            jax.ShapeDtypeStruct((m, D_MODEL), F32),
            jax.ShapeDtypeStruct((D_MODEL, D_FF), BF16),
            jax.ShapeDtypeStruct((D_FF, D_MODEL), BF16),
        ),
        scratch_shapes=[pltpu.VMEM((m, D_MODEL), F32)],
        compiler_params=_cparams(("arbitrary",)),
        name="mlp_cast",
    )(h2, w_up, w_down, x1, gf)


def _mlp(h2, wu, wd, x1, gf, tm, tf):
    m = h2.shape[0]
    in_specs = [
        pl.BlockSpec((tm, D_MODEL), lambda i, f: (i, 0)),
        pl.BlockSpec((D_MODEL, tf), lambda i, f: (0, f)),
        pl.BlockSpec((tf, D_MODEL), lambda i, f: (f, 0)),
        pl.BlockSpec((tm, D_MODEL), lambda i, f: (i, 0)),
        pl.BlockSpec((1, D_MODEL), lambda i, f: (0, 0)),
    ]
    return pl.pallas_call(
        _mlp_body,
        grid=(m // tm, D_FF // tf),
        in_specs=in_specs,
        out_specs=pl.BlockSpec((tm, D_MODEL), lambda i, f: (i, 0)),
        out_shape=jax.ShapeDtypeStruct((m, D_MODEL), F32),
        scratch_shapes=[pltpu.VMEM((tm, D_MODEL), F32)],
        compiler_params=_cparams(("arbitrary", "arbitrary")),
        name="mlp",
    )(h2, wu, wd, x1, gf)


def _fetch_pages(pt_ref, step, slot, streams, start):
    n_pages = pt_ref.shape[1]
    for hbm, buf, sem in streams:
        for j in range(n_pages):
            cp = pltpu.make_async_copy(hbm.at[pt_ref[step, j]], buf.at[slot, j], sem.at[slot])
            if start:
                cp.start()
            else:
                cp.wait()


def _paged_loop(pt_ref, streams, step_fn):
    nb = pt_ref.shape[0]
    n_slots = streams[0][1].shape[0]
    ahead = n_slots - PAGE_GROUP
    assert nb % PAGE_GROUP == 0 and ahead % PAGE_GROUP == 0 and PAGE_GROUP <= ahead <= nb
    for s in range(ahead):
        _fetch_pages(pt_ref, s, s, streams, start=True)

    def body(g, carry):
        b0 = g * PAGE_GROUP

        @pl.when(b0 + ahead < nb)
        def _next():
            for i in range(PAGE_GROUP):
                row = b0 + ahead + i
                _fetch_pages(pt_ref, row, row % n_slots, streams, start=True)

        for i in range(PAGE_GROUP):
            _fetch_pages(pt_ref, b0 + i, (b0 + i) % n_slots, streams, start=False)
        for i in range(PAGE_GROUP):
            step_fn(b0 + i, (b0 + i) % n_slots, i)
        return carry

    lax.fori_loop(0, nb // PAGE_GROUP, body, 0)


def _sidx_body(pt_ref, qi_ref, w_ref, kin_ref, cache_hbm, out_ref, kt_s, pbuf, sem):
    n_pages = pt_ref.shape[1]
    page = pbuf.shape[3]
    past = n_pages * page
    lane = lax.broadcasted_iota(I32, (1, LANES), 1)

    def step(b, slot, lane_of_trip):
        qi = qi_ref[b]
        w = w_ref[b] * (IDX_HEAD_DIM ** -0.5)
        for j in range(n_pages):
            kt_s[lane_of_trip, :, j * page:(j + 1) * page] = pbuf[slot, j].astype(BF16)
        s = jnp.dot(qi, kt_s[lane_of_trip], preferred_element_type=F32)
        out_ref[b, :, 0:past] = jnp.sum(jnp.maximum(s, 0.0) * w, axis=0, keepdims=True)
        sn = jnp.sum(qi.astype(F32) * kin_ref[b].astype(BF16).astype(F32), axis=1, keepdims=True)
        rn = jnp.sum(jnp.maximum(sn, 0.0) * w, axis=0, keepdims=True)
        out_ref[b, :, past:past + LANES] = jnp.where(lane == 0, rn, -jnp.inf)

    _paged_loop(pt_ref, ((cache_hbm, pbuf, sem),), step)


def _sample_index_scores(page_table, qi_s, wi_s, ki_s, cache_idx_k_t):
    nb, n_pages = page_table.shape
    page = cache_idx_k_t.shape[2]
    width = n_pages * page + LANES

    vmem = pl.BlockSpec(memory_space=pltpu.VMEM)
    return pl.pallas_call(
        _sidx_body,
        in_specs=[pl.BlockSpec(memory_space=pltpu.SMEM), vmem, vmem, vmem,
                  pl.BlockSpec(memory_space=pl.ANY)],
        out_specs=vmem,
        out_shape=jax.ShapeDtypeStruct((nb, 1, width), F32),
        scratch_shapes=[pltpu.VMEM((PAGE_GROUP, IDX_HEAD_DIM, n_pages * page), BF16),
                        pltpu.VMEM((IDX_PAGE_SLOTS, n_pages, IDX_HEAD_DIM, page), F32),
                        pltpu.SemaphoreType.DMA((IDX_PAGE_SLOTS,))],
        compiler_params=pltpu.CompilerParams(vmem_limit_bytes=VMEM_LIMIT),
        name="sample_idx",
    )(page_table, qi_s, wi_s, ki_s, cache_idx_k_t)


def _ssel_body(sc_ref, tri_ref, sel_ref):
    rows, width = sc_ref.shape
    nt = width // LANES
    n_valid = (nt - 1) * LANES + 1

    def tile(kt):
        return sc_ref[:, kt * LANES:(kt + 1) * LANES]

    def count_cmp(cmp):
        acc = jnp.zeros((rows, LANES), F32)
        for kt in range(nt):
            acc = acc + jnp.where(cmp(tile(kt)), 1.0, 0.0)
        return jnp.broadcast_to(jnp.sum(acc, axis=1, keepdims=True), (rows, LANES))

    tf = _threshold_search(lambda c: count_cmp(lambda sc: sc >= c), 32, (rows, LANES))
    need = float(TOPK_MAX) - count_cmp(lambda sc: sc > tf)
    tie_off = jnp.zeros((rows, LANES), F32)
    for kt in range(nt):
        col = kt * LANES + lax.broadcasted_iota(I32, (rows, LANES), 1)
        sc = tile(kt)
        eq = sc == tf
        tie = jnp.where(eq, 1.0, 0.0)
        rank = jnp.dot(tie.astype(BF16), tri_ref[...], preferred_element_type=F32) + tie_off
        sel = ((sc > tf) | (eq & (rank <= need))) & (col < n_valid)
        sel_ref[:, kt * LANES:(kt + 1) * LANES] = jnp.where(sel, 1.0, 0.0)
        tie_off = tie_off + jnp.broadcast_to(jnp.sum(tie, axis=1, keepdims=True), (rows, LANES))


def _sample_select(scores2d, tri):
    rows, width = scores2d.shape
    return pl.pallas_call(
        _ssel_body,
        out_shape=jax.ShapeDtypeStruct((rows, width), F32),
        compiler_params=pltpu.CompilerParams(vmem_limit_bytes=VMEM_LIMIT),
        name="sample_select",
    )(scores2d, tri)


def _sattn_body(pt_ref, q_ref, sel_ref, kn_ref, vn_ref, ck_hbm, cv_hbm, o_ref,
                kbuf, vbuf, ksem, vsem):
    n_pages = pt_ref.shape[1]
    page = kbuf.shape[2]
    past = n_pages * page
    scale = ATTN_HEAD_DIM ** -0.5

    def step(b, slot, lane_of_trip):
        q = q_ref[b]
        k_all = kbuf[slot].reshape(past, ATTN_HEAD_DIM).astype(BF16)
        v_all = vbuf[slot].reshape(past, ATTN_HEAD_DIM).astype(BF16)
        s = lax.dot_general(q, k_all, (((1,), (1,)), ((), ())), preferred_element_type=F32)
        s = jnp.where(sel_ref[b, :, 0:past] > 0.5, s * scale, NEG_BIG)
        kn = kn_ref[b].astype(BF16).astype(F32)
        sn = jnp.sum(q.astype(F32) * kn, axis=1, keepdims=True) * scale
        sn = jnp.where(sel_ref[b, :, past:past + 1] > 0.5, sn, NEG_BIG)
        m = jnp.maximum(jnp.max(s, axis=1, keepdims=True), sn)
        pn = jnp.exp(sn - m)
        p = jnp.exp(s - m)
        l = pn + jnp.sum(p, axis=1, keepdims=True)
        acc = (pn * vn_ref[b].astype(BF16).astype(F32)
               + jnp.dot(p.astype(BF16), v_all, preferred_element_type=F32))
        o_ref[b] = (acc / l).astype(BF16)

    _paged_loop(pt_ref, ((ck_hbm, kbuf, ksem), (cv_hbm, vbuf, vsem)), step)


def _sample_attention(page_table, qa_s, sel3, ka_s, va_s, cache_k, cache_v):
    nb, n_pages = page_table.shape
    page = cache_k.shape[1]

    vmem = pl.BlockSpec(memory_space=pltpu.VMEM)
    hbm = pl.BlockSpec(memory_space=pl.ANY)
    return pl.pallas_call(
        _sattn_body,
        in_specs=[pl.BlockSpec(memory_space=pltpu.SMEM), vmem, vmem, vmem, vmem, hbm, hbm],
        out_specs=vmem,
        out_shape=jax.ShapeDtypeStruct((nb, ATTN_HEADS, ATTN_HEAD_DIM), BF16),
        scratch_shapes=[pltpu.VMEM((ATTN_PAGE_SLOTS, n_pages, page, ATTN_HEAD_DIM), F32),
                        pltpu.VMEM((ATTN_PAGE_SLOTS, n_pages, page, ATTN_HEAD_DIM), F32),
                        pltpu.SemaphoreType.DMA((ATTN_PAGE_SLOTS,)),
                        pltpu.SemaphoreType.DMA((ATTN_PAGE_SLOTS,))],
        compiler_params=pltpu.CompilerParams(vmem_limit_bytes=VMEM_LIMIT),
        name="sample_attn",
    )(page_table, qa_s, sel3, ka_s, va_s, cache_k, cache_v)


def _sret_body(qkvg_ref, st_ref, gam_ref, rg_ref, so_ref):
    ns = st_ref.shape[0]
    for s in range(ns):
        blk = qkvg_ref[s].astype(F32)
        q8 = blk[0:8]
        k8 = blk[8:16]
        v8 = blk[16:24]
        g8 = blk[24:32]
        q_t = q8.T
        k_t = k8.T
        qk = jnp.sum(q8 * k8, axis=1, keepdims=True)
        rows = []
        for h in range(RET_HEADS):
            r_old = st_ref[s, h]
            gam = gam_ref[h]
            qcol = jnp.broadcast_to(q_t[:, h:h + 1], (RET_DK, RET_DV))
            kcol = jnp.broadcast_to(k_t[:, h:h + 1], (RET_DK, RET_DV))
            vrow = v8[h:h + 1]
            q_r = jnp.sum(qcol * r_old, axis=0, keepdims=True)
            rows.append(gam * q_r + qk[h:h + 1] * vrow)
            so_ref[s, h] = gam * r_old + kcol * vrow
        ret = jnp.concatenate(rows, axis=0)
        rg_ref[s] = _gate(ret, g8).astype(BF16)


def _sample_retention(qkvg, state, gam, ns):
    nb = state.shape[0]
    return pl.pallas_call(
        _sret_body,
        grid=(nb // ns,),
        in_specs=[
            pl.BlockSpec((ns, 32, LANES), lambda i: (i, 0, 0)),
            pl.BlockSpec((ns, RET_HEADS, RET_DK, RET_DV), lambda i: (i, 0, 0, 0)),
            pl.BlockSpec((RET_HEADS, 1, LANES), lambda i: (0, 0, 0)),
        ],
        out_specs=(
            pl.BlockSpec((ns, RET_HEADS, RET_DV), lambda i: (i, 0, 0)),
            pl.BlockSpec((ns, RET_HEADS, RET_DK, RET_DV), lambda i: (i, 0, 0, 0)),
        ),
        out_shape=(
            jax.ShapeDtypeStruct((nb, RET_HEADS, RET_DV), BF16),
            jax.ShapeDtypeStruct(state.shape, F32),
        ),
        compiler_params=_cparams(("arbitrary",)),
        name="sample_ret",
    )(qkvg, state, gam)


def _rotary_table(pos):
    half = RET_DK // 2
    inv = ROPE_BASE ** (-np.arange(half, dtype=np.float64) / half)
    ang = np.asarray(pos, np.float64)[:, None] * inv[None, :]
    cos = np.cos(ang)
    sin = np.sin(ang)
    return jnp.asarray(np.concatenate([cos, cos, -sin, sin], axis=1), F32)


def _retention_constants():
    lg = np.log1p(-np.exp2(-5.0 - np.arange(RET_HEADS, dtype=np.float64)))
    n = RET_CHUNK
    i = np.arange(n, dtype=np.float64)
    diff = i[:, None] - i[None, :]
    decay = np.where(diff[None] >= 0, np.exp(np.maximum(diff, 0.0)[None] * lg[:, None, None]), 0.0)
    rsc = np.exp((i + 1.0)[None, :] * lg[:, None])
    zeta = np.exp((n - 1.0 - i)[None, :] * lg[:, None])
    gpow = np.exp(n * lg)
    gam1 = np.exp(lg)
    rsc_b = np.broadcast_to(rsc[:, :, None], (RET_HEADS, n, RET_DV))
    zeta_b = np.broadcast_to(zeta[:, :, None], (RET_HEADS, n, RET_DK))
    gpow_b = np.broadcast_to(gpow[:, None, None], (RET_HEADS, 1, RET_DV))
    gam1_b = np.broadcast_to(gam1[:, None, None], (RET_HEADS, 1, LANES))
    return tuple(jnp.asarray(a, F32) for a in (decay, rsc_b, zeta_b, gpow_b, gam1_b))


def _upper_tri(n):
    return jnp.asarray(np.triu(np.ones((n, n), np.float32)), BF16)


def _lower_tri(n):
    return jnp.asarray(np.tril(np.ones((n, n), np.float32)), BF16)


def _pad_lanes(v):
    return jnp.pad(v, (0, LANES - v.shape[0])).reshape(1, LANES)


def kernel(x_prompt, x_sample, cache_k, cache_v, cache_idx_k, state_ret, page_table,
           norm1_g, w_in, idx_k_norm_g, idx_k_norm_b, w_out, norm2_g, w_up, w_down, final_norm_g):
    batch, seq, _ = x_prompt.shape
    nb = x_sample.shape[0]
    past_len = page_table.shape[1] * cache_k.shape[1]
    half_mix = ATTN_HEADS * ATTN_HEAD_DIM

    wt = w_in.T.astype(BF16)
    wa = w_out[:half_mix].astype(BF16)
    wr = w_out[half_mix:].astype(BF16)
    g1 = norm1_g.reshape(1, D_MODEL)
    g2 = norm2_g.reshape(1, D_MODEL)
    gf = final_norm_g.reshape(1, D_MODEL)
    lng = _pad_lanes(idx_k_norm_g)
    lnb = _pad_lanes(idx_k_norm_b)
    decay, rsc_b, zeta_b, gpow_b, gam1_b = _retention_constants()

    xp = x_prompt.reshape(batch * seq, D_MODEL)
    cs_p = _rotary_table(np.arange(seq))
    qa_p, qi_p, ka_p, va_p, ki_p, kd_p, wi_p = _project_attn(
        xp, g1, wt, lng, lnb, tm=TILES["proj_attn_rows"])
    main_p = _project_ret(xp, g1, wt, cs_p, tm=TILES["proj_ret_rows"])
    attn_p = _prompt_attention(qa_p, qi_p, wi_p, ka_p, va_p, kd_p, _lower_tri(KEY_TILE),
                               batch, seq)
    rg_p, ret_state_p = _prompt_retention(main_p, decay, rsc_b, zeta_b, gpow_b, batch, seq)
    x1_p, h2_p = _out_projection(attn_p, rg_p, wa, wr, xp, g2, tm=TILES["out_proj_rows"])

    xs = x_sample.reshape(nb, D_MODEL)
    cs_s = _rotary_table(np.full((nb,), past_len))
    qa_s, qi_s, ka_s, va_s, ki_s, _, wi_s = _project_attn(xs, g1, wt, lng, lnb, tm=nb)
    main_s = _project_ret(xs, g1, wt, cs_s, tm=nb)
    scores = _sample_index_scores(
        page_table,
        qi_s.transpose(1, 0, 2).reshape(nb, IDX_HEADS, IDX_HEAD_DIM),
        wi_s.T.reshape(nb, IDX_HEADS, 1),
        ki_s.reshape(nb, 1, IDX_HEAD_DIM),
        jnp.swapaxes(cache_idx_k, 1, 2))
    width = scores.shape[2]
    sel = _sample_select(scores.reshape(nb, width), _upper_tri(LANES))
    attn_s = _sample_attention(
        page_table,
        qa_s.transpose(1, 0, 2),
        sel.reshape(nb, 1, width),
        ka_s.reshape(nb, 1, ATTN_HEAD_DIM),
        va_s.reshape(nb, 1, ATTN_HEAD_DIM),
        cache_k, cache_v)
    rg_s, ret_state_s = _sample_retention(main_s.reshape(nb, 32, LANES), state_ret, gam1_b,
                                          ns=TILES["sample_ret_rows"])
    x1_s, h2_s = _out_projection(attn_s.reshape(nb, half_mix), rg_s.reshape(nb, RET_WIDTH),
                                 wa, wr, xs, g2, tm=nb)
    y_s, wu, wd = _mlp_and_cast(h2_s, w_up, w_down, x1_s, gf, tf=TILES["mlp_cast_ff"])
    y_p = _mlp(h2_p, wu, wd, x1_p, gf, tm=TILES["mlp_rows"], tf=TILES["mlp_ff"])

    return (
        y_p.reshape(batch, seq, D_MODEL),
        y_s.reshape(nb, 1, D_MODEL),
        ka_p.reshape(batch, seq, ATTN_HEAD_DIM),
        va_p.reshape(batch, seq, ATTN_HEAD_DIM),
        ki_p.reshape(batch, seq, IDX_HEAD_DIM),
        ret_state_p,
        ka_s.reshape(nb, 1, ATTN_HEAD_DIM),
        va_s.reshape(nb, 1, ATTN_HEAD_DIM),
        ki_s.reshape(nb, 1, IDX_HEAD_DIM),
        ret_state_s,
    )
```

```python
import functools

import numpy as np
import jax
import jax.numpy as jnp
from jax import lax
from jax.experimental import pallas as pl
from jax.experimental.pallas import tpu as pltpu

F32 = jnp.float32
BF16 = jnp.bfloat16
I32 = jnp.int32

D_MODEL = 2048
ATTN_HEADS = 8
ATTN_HEAD_DIM = 128
IDX_HEADS = 16
IDX_HEAD_DIM = 64
TOPK_MAX = 256
RET_HEADS = 8
RET_DK = 128
RET_DV = 128
RET_CHUNK = 256
ROPE_BASE = 10000.0
D_FF = 4 * D_MODEL
EPS = 1e-6
Q_BLOCK = 256

OFF_QA, OFF_KA, OFF_VA, OFF_QI, OFF_KI, OFF_WI = 0, 1024, 1152, 1280, 2304, 2368
OFF_QR, OFF_KR, OFF_VR, OFF_GR = 2384, 3408, 4432, 5456
RET_WIDTH = RET_HEADS * RET_DV

LANES = 128
PROJ_TILE = 512
KEY_TILE = 256
COUNT_ROWS = 64
PAGE_GROUP = 2
IDX_PAGE_SLOTS = 8
ATTN_PAGE_SLOTS = 6
SUM_ROWS = 16
LOG2_E = 1.4426950408889634
INT_MIN = -2 ** 31
KEY_NEG_INF = -2 ** 31 + 0x7FFFFF
BF16_KEY_NEG_INF = -2 ** 15 + 0x7F
NEG_BIG = -1e30
VMEM_LIMIT = 56 * 1024 * 1024

TILES = {
    "proj_attn_rows": 1024,
    "proj_ret_rows": 512,
    "out_proj_rows": 512,
    "mlp_rows": 512,
    "mlp_ff": 1024,
    "mlp_cast_ff": 1024,
    "sample_ret_rows": 8,
}


def _cparams(sem):
    return pltpu.CompilerParams(dimension_semantics=sem, vmem_limit_bytes=VMEM_LIMIT)


def _resident(shape):
    zeros = (0,) * len(shape)
    return pl.BlockSpec(shape, lambda *_: zeros, pipeline_mode=pl.Buffered(1))


def _normed_input(x_ref, g_ref, xn_ref):
    x = x_ref[...]
    ms = jnp.mean(x * x, axis=-1, keepdims=True)
    xn_ref[...] = (x * lax.rsqrt(ms + EPS) * g_ref[...]).astype(BF16)


def _matmul_rows(xn_ref, wt_ref, r0, n):
    return lax.dot_general(xn_ref[...], wt_ref[r0:r0 + n, :], (((1,), (1,)), ((), ())),
                           preferred_element_type=F32)


def _proj_attn_body(x_ref, g_ref, wt_ref, lng_ref, lnb_ref,
                    qa_ref, qi_ref, ka_ref, va_ref, ki_ref, kd_ref, wi_ref, xn_ref):
    _normed_input(x_ref, g_ref, xn_ref)
    mm = functools.partial(_matmul_rows, xn_ref, wt_ref)
    for t in range(ATTN_HEADS * ATTN_HEAD_DIM // PROJ_TILE):
        acc = mm(OFF_QA + t * PROJ_TILE, PROJ_TILE)
        for hh in range(4):
            qa_ref[4 * t + hh] = acc[:, hh * LANES:(hh + 1) * LANES].astype(BF16)
    for t in range(IDX_HEADS * IDX_HEAD_DIM // PROJ_TILE):
        acc = mm(OFF_QI + t * PROJ_TILE, PROJ_TILE)
        for hh in range(4):
            qi_ref[4 * t + hh] = acc[:, hh * LANES:(hh + 1) * LANES].astype(BF16)
    kv = mm(OFF_KA, 2 * ATTN_HEAD_DIM)
    ka_ref[...] = kv[:, :ATTN_HEAD_DIM]
    va_ref[...] = kv[:, ATTN_HEAD_DIM:]
    kw = mm(OFF_KI, LANES)
    lane = lax.broadcasted_iota(I32, kw.shape, 1)
    is_k = lane < IDX_HEAD_DIM
    mu = jnp.sum(jnp.where(is_k, kw, 0.0), axis=-1, keepdims=True) * (1.0 / IDX_HEAD_DIM)
    d = jnp.where(is_k, kw - mu, 0.0)
    var = jnp.sum(d * d, axis=-1, keepdims=True) * (1.0 / IDX_HEAD_DIM)
    kn = d * lax.rsqrt(var + EPS) * lng_ref[...] + lnb_ref[...]
    ki_ref[...] = kn[:, :IDX_HEAD_DIM]
    kd_ref[...] = jnp.where(is_k, kn, pltpu.roll(kn, IDX_HEAD_DIM, 1)).astype(BF16)
    wi_ref[...] = (kw[:, IDX_HEAD_DIM:IDX_HEAD_DIM + IDX_HEADS] * (IDX_HEADS ** -0.5)).T


def _proj_ret_body(x_ref, g_ref, wt_ref, cs_ref, main_ref, xn_ref):
    _normed_input(x_ref, g_ref, xn_ref)
    base = 0
    cosf = cs_ref[:, :LANES]
    sinf = cs_ref[:, LANES:]
    tiles = RET_WIDTH // PROJ_TILE
    for seg, (off, scale) in enumerate(((OFF_QR, None), (OFF_KR, RET_DK ** -0.5))):
        for t in range(tiles):
            acc = _matmul_rows(xn_ref, wt_ref, off - base + t * PROJ_TILE, PROJ_TILE)
            for hh in range(PROJ_TILE // LANES):
                xh = acc[:, hh * LANES:(hh + 1) * LANES]
                r = xh * cosf + pltpu.roll(xh, RET_DK // 2, 1) * sinf
                if scale is not None:
                    r = r * scale
                c0 = seg * RET_WIDTH + t * PROJ_TILE + hh * LANES
                main_ref[:, c0:c0 + LANES] = r.astype(BF16)
    for seg, off in ((2, OFF_VR), (3, OFF_GR)):
        for t in range(tiles):
            acc = _matmul_rows(xn_ref, wt_ref, off - base + t * PROJ_TILE, PROJ_TILE)
            c0 = seg * RET_WIDTH + t * PROJ_TILE
            main_ref[:, c0:c0 + PROJ_TILE] = acc.astype(BF16)


def _project_attn(x2d, g1, wt_attn, lng, lnb, tm):
    m = x2d.shape[0]
    row = lambda i: (i, 0)
    out_shape = (
        jax.ShapeDtypeStruct((ATTN_HEADS, m, ATTN_HEAD_DIM), BF16),
        jax.ShapeDtypeStruct((IDX_HEADS // 2, m, LANES), BF16),
        jax.ShapeDtypeStruct((m, ATTN_HEAD_DIM), F32),
        jax.ShapeDtypeStruct((m, ATTN_HEAD_DIM), F32),
        jax.ShapeDtypeStruct((m, IDX_HEAD_DIM), F32),
        jax.ShapeDtypeStruct((m, LANES), BF16),
        jax.ShapeDtypeStruct((IDX_HEADS, m), F32),
    )
    out_specs = (
        pl.BlockSpec((ATTN_HEADS, tm, ATTN_HEAD_DIM), lambda i: (0, i, 0)),
        pl.BlockSpec((IDX_HEADS // 2, tm, LANES), lambda i: (0, i, 0)),
        pl.BlockSpec((tm, ATTN_HEAD_DIM), row),
        pl.BlockSpec((tm, ATTN_HEAD_DIM), row),
        pl.BlockSpec((tm, IDX_HEAD_DIM), row),
        pl.BlockSpec((tm, LANES), row),
        pl.BlockSpec((IDX_HEADS, tm), lambda i: (0, i)),
    )
    return pl.pallas_call(
        _proj_attn_body,
        grid=(m // tm,),
        in_specs=[pl.BlockSpec((tm, D_MODEL), row), _resident((1, D_MODEL)),
                  _resident((OFF_QR + LANES, D_MODEL)), _resident((1, LANES)),
                  _resident((1, LANES))],
        out_specs=out_specs,
        out_shape=out_shape,
        scratch_shapes=[pltpu.VMEM((tm, D_MODEL), BF16)],
        compiler_params=_cparams(("arbitrary",)),
        name="proj_attn",
    )(x2d, g1, wt_attn, lng, lnb)


def _project_ret(x2d, g1, wt_ret, cs, tm):
    m = x2d.shape[0]
    n_pos_blocks = cs.shape[0] // tm
    row = lambda i: (i, 0)
    return pl.pallas_call(
        _proj_ret_body,
        grid=(m // tm,),
        in_specs=[pl.BlockSpec((tm, D_MODEL), row), _resident((1, D_MODEL)),
                  _resident(wt_ret.shape),
                  pl.BlockSpec((tm, 2 * LANES), lambda i: (i % n_pos_blocks, 0))],
        out_specs=pl.BlockSpec((tm, 4 * RET_WIDTH), row),
        out_shape=jax.ShapeDtypeStruct((m, 4 * RET_WIDTH), BF16),
        scratch_shapes=[pltpu.VMEM((tm, D_MODEL), BF16)],
        compiler_params=_cparams(("arbitrary",)),
        name="proj_ret",
    )(x2d, g1, wt_ret, cs)


def _key_to_float(key):
    bits = key ^ ((key >> 31) & 0x7FFFFFFF)
    return lax.bitcast_convert_type(bits, F32)


def _threshold_search(count_ge, n_iter, shape):
    def body(it, t):
        bit = lax.shift_left(jnp.int32(1), 31 - it)
        cand = t ^ bit
        cnt = count_ge(_key_to_float(cand))
        return jnp.where(cnt >= float(TOPK_MAX), cand, t)

    t = lax.fori_loop(0, n_iter, body, jnp.full(shape, INT_MIN, I32))
    return _key_to_float(jnp.maximum(t, KEY_NEG_INF))


def _bf16_key_to_f32_key(k16):
    return lax.shift_left(k16, 16) | jnp.where(k16 < 0, 0xFFFF, 0)


def _threshold_search_coarse_fine(count_ge_bf16, count_ge, run, shape):
    def coarse(it, u):
        cand = u | lax.shift_left(jnp.int32(1), 15 - it)
        c = _key_to_float(_bf16_key_to_f32_key(cand - 32768)).astype(BF16)
        return jnp.where(count_ge_bf16(c) >= float(TOPK_MAX), cand, u)

    u = lax.fori_loop(0, jnp.where(run, 16, 0), coarse, jnp.zeros(shape, I32))
    k1 = jnp.maximum(u - 32768, BF16_KEY_NEG_INF)
    lo = _bf16_key_to_f32_key(jnp.maximum(k1 - 1, -32768))
    hi = _bf16_key_to_f32_key(jnp.minimum(k1 + 1, 32767))

    def fine(it, t):
        cand = t + lax.shift_left(jnp.int32(1), 16 - it)
        ok = (cand < hi) & (count_ge(_key_to_float(cand)) >= float(TOPK_MAX))
        return jnp.where(ok, cand, t)

    t = lax.fori_loop(0, jnp.where(run, 17, 0), fine, lo)
    return _key_to_float(jnp.maximum(t, KEY_NEG_INF))


def _attn_body(qa_ref, qi_ref, wit_ref, ka_ref, va_ref, kd_ref, tri_ref, o_ref,
               kbf, vtb, scr, mrun, acc_s, kmax, scr16):
    qb = pl.program_id(1)
    n_heads_q = ATTN_HEADS * Q_BLOCK
    n_pairs = IDX_HEADS // 2
    dv = ATTN_HEAD_DIM
    logit_scale = ATTN_HEAD_DIM ** -0.5 * LOG2_E

    @pl.when(qb == 0)
    def _cast():
        ka = ka_ref[...]
        kbf[...] = ka.astype(BF16)
        kmax[...] = jnp.broadcast_to(jnp.max(jnp.sum(ka * ka, axis=1, keepdims=True)), kmax.shape)
        for kt in range(vtb.shape[0]):
            vtb[kt, :dv] = va_ref[kt * KEY_TILE:(kt + 1) * KEY_TILE, :].T.astype(BF16)
            vtb[kt, dv:] = jnp.ones((vtb.shape[1] - dv, KEY_TILE), BF16)

    nk = ((qb + 1) * Q_BLOCK + KEY_TILE - 1) // KEY_TILE
    wt = wit_ref[...] * (IDX_HEAD_DIM ** -0.5)
    qi2 = qi_ref[...].reshape(n_pairs * Q_BLOCK, LANES)
    lo_half = lax.broadcasted_iota(I32, (KEY_TILE, LANES), 1) < IDX_HEAD_DIM
    qidx = qb * Q_BLOCK + lax.broadcasted_iota(I32, (KEY_TILE, Q_BLOCK), 1)
    kidx0 = lax.broadcasted_iota(I32, (KEY_TILE, Q_BLOCK), 0)
    contract_last = (((1,), (1,)), ((), ()))

    def idx_body(kt, carry):
        off = pl.multiple_of(kt * KEY_TILE, KEY_TILE)
        kit = kd_ref[pl.ds(off, KEY_TILE), :]
        zero = jnp.zeros_like(kit)
        s_even = lax.dot_general(jnp.where(lo_half, kit, zero), qi2, contract_last,
                                 preferred_element_type=F32)
        s_odd = lax.dot_general(jnp.where(lo_half, zero, kit), qi2, contract_last,
                                preferred_element_type=F32)
        score = jnp.zeros((KEY_TILE, Q_BLOCK), F32)
        for g in range(n_pairs):
            cs = slice(g * Q_BLOCK, (g + 1) * Q_BLOCK)
            score = score + jnp.maximum(s_even[:, cs], 0.0) * wt[2 * g:2 * g + 1, :]
            score = score + jnp.maximum(s_odd[:, cs], 0.0) * wt[2 * g + 1:2 * g + 2, :]
        score = jnp.where(kidx0 + off <= qidx, score, -jnp.inf)
        scr[kt] = score
        scr16[kt] = score.astype(BF16)
        return carry

    def for_tiles(fn):
        def pair(j, carry):
            fn(2 * j, 0)
            fn(2 * j + 1, 0)
            return carry

        lax.fori_loop(0, nk // 2, pair, 0)

        @pl.when(nk % 2 == 1)
        def _last():
            fn(nk - 1, 0)

    for_tiles(idx_body)

    def count_ge_bf16(c):
        def body(kt, acc):
            hit = scr16[kt] >= c
            for r in range(KEY_TILE // COUNT_ROWS):
                acc = jnp.where(hit[r * COUNT_ROWS:(r + 1) * COUNT_ROWS], acc + 1.0, acc)
            return acc
        acc = lax.fori_loop(0, nk, body, jnp.zeros((COUNT_ROWS, Q_BLOCK), BF16))
        return jnp.sum(acc.astype(F32), axis=0, keepdims=True)

    def count_cmp(cmp):
        def body(kt, acc):
            hit = cmp(scr[kt])
            for r in range(KEY_TILE // COUNT_ROWS):
                acc = jnp.where(hit[r * COUNT_ROWS:(r + 1) * COUNT_ROWS], acc + 1.0, acc)
            return acc
        acc = lax.fori_loop(0, nk, body, jnp.zeros((COUNT_ROWS, Q_BLOCK), F32))
        return jnp.sum(acc, axis=0, keepdims=True)

    tf = _threshold_search_coarse_fine(
        count_ge_bf16, lambda c: count_cmp(lambda sc: sc >= c),
        qb >= TOPK_MAX // Q_BLOCK, (1, Q_BLOCK))
    excess = jnp.max(count_cmp(lambda sc: sc >= tf)) > float(TOPK_MAX)

    qa2 = qa_ref[...].reshape(n_heads_q, ATTN_HEAD_DIM)

    def logits(kt):
        off = pl.multiple_of(kt * KEY_TILE, KEY_TILE)
        s = lax.dot_general(kbf[pl.ds(off, KEY_TILE), :], qa2, contract_last,
                            preferred_element_type=F32)
        return s * logit_scale

    def sel_plain(kt, carry):
        off = pl.multiple_of(kt * KEY_TILE, KEY_TILE)
        return (scr[kt] >= tf) & (kidx0 + off <= qidx), carry

    def sel_ties(need, kt, tie_off):
        off = pl.multiple_of(kt * KEY_TILE, KEY_TILE)
        sc = scr[kt]
        eq = sc == tf
        tie = jnp.where(eq, 1.0, 0.0)
        rank = jnp.dot(tri_ref[...], tie.astype(BF16), preferred_element_type=F32) + tie_off
        sel = ((sc > tf) | (eq & (rank <= need))) & (kidx0 + off <= qidx)
        return sel, tie_off + jnp.sum(tie, axis=0, keepdims=True)

    no_ties = jnp.zeros((1, Q_BLOCK), F32)

    def softmax_sum(m, sel_fn):
        acc_s[...] = jnp.zeros(acc_s.shape, F32)

        def body(kt, carry):
            sel, carry = sel_fn(kt, carry)
            e = jnp.exp2(logits(kt) - m)
            parts = []
            for h in range(ATTN_HEADS):
                cs = slice(h * Q_BLOCK, (h + 1) * Q_BLOCK)
                parts.append(jnp.where(sel, e[:, cs], 0.0).astype(BF16))
            p = jnp.concatenate(parts, axis=1)
            acc_s[...] += jnp.dot(vtb[kt], p, preferred_element_type=F32)
            return carry

        if sel_fn is sel_plain:
            for_tiles(body)
        else:
            lax.fori_loop(0, nk, body, no_ties)

    def selected_max(sel_fn):
        mrun[...] = jnp.full(mrun.shape, NEG_BIG, F32)

        def body(kt, carry):
            sel, carry = sel_fn(kt, carry)
            s = logits(kt)
            for h in range(ATTN_HEADS):
                cs = slice(h * Q_BLOCK, (h + 1) * Q_BLOCK)
                sh = jnp.where(sel, s[:, cs], NEG_BIG)
                mrun[:, cs] = jnp.maximum(
                    mrun[:, cs], jnp.max(sh.reshape(KEY_TILE // 8, 8, Q_BLOCK), axis=0))
            return carry

        lax.fori_loop(0, nk, body, no_ties)
        return jnp.max(mrun[...], axis=0, keepdims=True)

    def fast_path():
        q2 = (qa2 * qa2).astype(BF16)
        qsq = lax.dot_general(jnp.ones((8, ATTN_HEAD_DIM), BF16), q2, contract_last,
                              preferred_element_type=F32)[0:1]
        softmax_sum(jnp.sqrt(qsq * kmax[0:1, 0:1]) * logit_scale, sel_plain)
        return (jnp.min(acc_s[dv:dv + 1, :]) > 0.0).astype(I32)

    done = lax.cond(excess, lambda: jnp.int32(0), fast_path) == 1

    @pl.when(jnp.logical_not(done) & excess)
    def _exact_with_ties():
        need = float(TOPK_MAX) - count_cmp(lambda sc: sc > tf)
        sel_fn = functools.partial(sel_ties, need)
        softmax_sum(selected_max(sel_fn), sel_fn)

    @pl.when(jnp.logical_not(done) & jnp.logical_not(excess))
    def _exact_without_ties():
        softmax_sum(selected_max(sel_plain), sel_plain)

    out = acc_s[:dv, :] / acc_s[dv:dv + 1, :]
    for h in range(ATTN_HEADS):
        oh = out[:, h * Q_BLOCK:(h + 1) * Q_BLOCK].T
        o_ref[:, h * ATTN_HEAD_DIM:(h + 1) * ATTN_HEAD_DIM] = oh.astype(BF16)


def _prompt_attention(qa_hm, qi_pm, wi_t, ka, va, kd, tri, batch, seq):
    nq = seq // Q_BLOCK
    nkt = seq // KEY_TILE
    m = batch * seq
    n_heads_q = ATTN_HEADS * Q_BLOCK
    in_specs = [
        pl.BlockSpec((ATTN_HEADS, Q_BLOCK, ATTN_HEAD_DIM), lambda b, q: (0, b * nq + q, 0)),
        pl.BlockSpec((IDX_HEADS // 2, Q_BLOCK, LANES), lambda b, q: (0, b * nq + q, 0)),
        pl.BlockSpec((IDX_HEADS, Q_BLOCK), lambda b, q: (0, b * nq + q)),
        pl.BlockSpec((seq, ATTN_HEAD_DIM), lambda b, q: (b, 0)),
        pl.BlockSpec((seq, ATTN_HEAD_DIM), lambda b, q: (b, 0)),
        pl.BlockSpec((seq, LANES), lambda b, q: (b, 0)),
        pl.BlockSpec((KEY_TILE, KEY_TILE), lambda b, q: (0, 0)),
    ]
    return pl.pallas_call(
        _attn_body,
        grid=(batch, nq),
        in_specs=in_specs,
        out_specs=pl.BlockSpec((Q_BLOCK, ATTN_HEADS * ATTN_HEAD_DIM), lambda b, q: (b * nq + q, 0)),
        out_shape=jax.ShapeDtypeStruct((m, ATTN_HEADS * ATTN_HEAD_DIM), BF16),
        scratch_shapes=[
            pltpu.VMEM((seq, ATTN_HEAD_DIM), BF16),
            pltpu.VMEM((nkt, ATTN_HEAD_DIM + SUM_ROWS, KEY_TILE), BF16),
            pltpu.VMEM((nkt, KEY_TILE, Q_BLOCK), F32),
            pltpu.VMEM((8, n_heads_q), F32),
            pltpu.VMEM((ATTN_HEAD_DIM + SUM_ROWS, n_heads_q), F32),
            pltpu.VMEM((8, LANES), F32),
            pltpu.VMEM((nkt, KEY_TILE, Q_BLOCK), BF16),
        ],
        compiler_params=_cparams(("arbitrary", "arbitrary")),
        name="prompt_attn",
    )(qa_hm, qi_pm, wi_t, ka, va, kd, tri)


def _gate(o, g):
    rn = o * lax.rsqrt(jnp.mean(o * o, axis=-1, keepdims=True) + EPS)
    return rn * (g / (1.0 + jnp.exp(-g)))


def _ret_body(q_ref, k_ref, v_ref, g_ref, decay_ref, rsc_ref, zeta_ref, gpow_ref,
              rg_ref, st_ref):
    c = pl.program_id(1)

    @pl.when(c == 0)
    def _init():
        st_ref[...] = jnp.zeros(st_ref.shape, F32)

    for h in range(RET_HEADS):
        sl = slice(h * 128, (h + 1) * 128)
        q = q_ref[:, sl]
        k = k_ref[:, sl]
        v = v_ref[:, sl]
        r_old = st_ref[0, h]
        qk = lax.dot_general(q, k, (((1,), (1,)), ((), ())), preferred_element_type=F32)
        inner = jnp.dot((qk * decay_ref[h]).astype(BF16), v, preferred_element_type=F32)
        cross = jnp.dot(q, r_old.astype(BF16), preferred_element_type=F32) * rsc_ref[h]
        kz = (k.astype(F32) * zeta_ref[h]).astype(BF16)
        upd = lax.dot_general(kz, v, (((0,), (0,)), ((), ())), preferred_element_type=F32)
        st_ref[0, h] = r_old * gpow_ref[h] + upd
        rg_ref[:, sl] = _gate(inner + cross, g_ref[:, sl].astype(F32)).astype(BF16)


def _prompt_retention(main, decay, rsc, zeta, gpow, batch, seq):
    nc = seq // RET_CHUNK
    m = batch * seq
    width = RET_WIDTH
    const3 = lambda b, c: (0, 0, 0)
    in_specs = [
        pl.BlockSpec((RET_CHUNK, width), lambda b, c: (b * nc + c, 0)),
        pl.BlockSpec((RET_CHUNK, width), lambda b, c: (b * nc + c, 1)),
        pl.BlockSpec((RET_CHUNK, width), lambda b, c: (b * nc + c, 2)),
        pl.BlockSpec((RET_CHUNK, width), lambda b, c: (b * nc + c, 3)),
        pl.BlockSpec((RET_HEADS, RET_CHUNK, RET_CHUNK), const3),
        pl.BlockSpec((RET_HEADS, RET_CHUNK, RET_DV), const3),
        pl.BlockSpec((RET_HEADS, RET_CHUNK, RET_DK), const3),
        pl.BlockSpec((RET_HEADS, 1, RET_DV), const3),
    ]
    return pl.pallas_call(
        _ret_body,
        grid=(batch, nc),
        in_specs=in_specs,
        out_specs=(
            pl.BlockSpec((RET_CHUNK, width), lambda b, c: (b * nc + c, 0)),
            pl.BlockSpec((1, RET_HEADS, RET_DK, RET_DV), lambda b, c: (b, 0, 0, 0)),
        ),
        out_shape=(
            jax.ShapeDtypeStruct((m, width), BF16),
            jax.ShapeDtypeStruct((batch, RET_HEADS, RET_DK, RET_DV), F32),
        ),
        compiler_params=_cparams(("arbitrary", "arbitrary")),
        name="prompt_ret",
    )(main, main, main, main, decay, rsc, zeta, gpow)


def _outproj_body(a_ref, r_ref, wa_ref, wr_ref, x_ref, g2_ref, x1_ref, h2_ref):
    mixed = (jnp.dot(a_ref[...], wa_ref[...], preferred_element_type=F32)
             + jnp.dot(r_ref[...], wr_ref[...], preferred_element_type=F32))
    x1 = x_ref[...] + mixed
    x1_ref[...] = x1
    ms = jnp.mean(x1 * x1, axis=-1, keepdims=True)
    h2_ref[...] = (x1 * lax.rsqrt(ms + EPS) * g2_ref[...]).astype(BF16)


def _out_projection(attn_o, rg, wa, wr, x2d, g2, tm):
    m = x2d.shape[0]
    half = attn_o.shape[1]
    in_specs = [
        pl.BlockSpec((tm, half), lambda i: (i, 0)),
        pl.BlockSpec((tm, half), lambda i: (i, 0)),
        pl.BlockSpec((half, D_MODEL), lambda i: (0, 0)),
        pl.BlockSpec((half, D_MODEL), lambda i: (0, 0)),
        pl.BlockSpec((tm, D_MODEL), lambda i: (i, 0)),
        pl.BlockSpec((1, D_MODEL), lambda i: (0, 0)),
    ]
    return pl.pallas_call(
        _outproj_body,
        grid=(m // tm,),
        in_specs=in_specs,
        out_specs=(pl.BlockSpec((tm, D_MODEL), lambda i: (i, 0)),
                   pl.BlockSpec((tm, D_MODEL), lambda i: (i, 0))),
        out_shape=(jax.ShapeDtypeStruct((m, D_MODEL), F32),
                   jax.ShapeDtypeStruct((m, D_MODEL), BF16)),
        compiler_params=_cparams(("arbitrary",)),
        name="out_proj",
    )(attn_o, rg, wa, wr, x2d, g2)


def _mlp_body(h2_ref, wu_ref, wd_ref, x1_ref, gf_ref, y_ref, acc_ref):
    f = pl.program_id(1)

    @pl.when(f == 0)
    def _init():
        acc_ref[...] = x1_ref[...]

    u = jnp.dot(h2_ref[...], wu_ref[...], preferred_element_type=F32)
    a = jnp.maximum(u, 0.0)
    acc_ref[...] += jnp.dot((a * a).astype(BF16), wd_ref[...], preferred_element_type=F32)

    @pl.when(f == pl.num_programs(1) - 1)
    def _final():
        x2 = acc_ref[...]
        ms = jnp.mean(x2 * x2, axis=-1, keepdims=True)
        y_ref[...] = x2 * lax.rsqrt(ms + EPS) * gf_ref[...]


def _mlp_cast_body(h2_ref, wu_ref, wd_ref, x1_ref, gf_ref, y_ref, wub_ref, wdb_ref, acc_ref):
    f = pl.program_id(0)

    @pl.when(f == 0)
    def _init():
        acc_ref[...] = x1_ref[...]

    wu = wu_ref[...].astype(BF16)
    wd = wd_ref[...].astype(BF16)
    wub_ref[...] = wu
    wdb_ref[...] = wd
    u = jnp.dot(h2_ref[...], wu, preferred_element_type=F32)
    a = jnp.maximum(u, 0.0)
    acc_ref[...] += jnp.dot((a * a).astype(BF16), wd, preferred_element_type=F32)

    @pl.when(f == pl.num_programs(0) - 1)
    def _final():
        x2 = acc_ref[...]
        ms = jnp.mean(x2 * x2, axis=-1, keepdims=True)
        y_ref[...] = x2 * lax.rsqrt(ms + EPS) * gf_ref[...]


def _mlp_and_cast(h2, w_up, w_down, x1, gf, tf):
    m = h2.shape[0]
    full = lambda f: (0, 0)
    return pl.pallas_call(
        _mlp_cast_body,
        grid=(D_FF // tf,),
        in_specs=[
            pl.BlockSpec((m, D_MODEL), full),
            pl.BlockSpec((D_MODEL, tf), lambda f: (0, f)),
            pl.BlockSpec((tf, D_MODEL), lambda f: (f, 0)),
            pl.BlockSpec((m, D_MODEL), full),
            pl.BlockSpec((1, D_MODEL), full),
        ],
        out_specs=(
            pl.BlockSpec((m, D_MODEL), full),
            pl.BlockSpec((D_MODEL, tf), lambda f: (0, f)),
            pl.BlockSpec((tf, D_MODEL), lambda f: (f, 0)),
        ),
        out_shape=(
            jax.ShapeDtypeStruct((m, D_MODEL), F32),
            jax.ShapeDtypeStruct((D_MODEL, D_FF), BF16),
            jax.ShapeDtypeStruct((D_FF, D_MODEL), BF16),
        ),
        scratch_shapes=[pltpu.VMEM((m, D_MODEL), F32)],
        compiler_params=_cparams(("arbitrary",)),
        name="mlp_cast",
    )(h2, w_up, w_down, x1, gf)


def _mlp(h2, wu, wd, x1, gf, tm, tf):
    m = h2.shape[0]
    in_specs = [
        pl.BlockSpec((tm, D_MODEL), lambda i, f: (i, 0)),
        pl.BlockSpec((D_MODEL, tf), lambda i, f: (0, f)),
        pl.BlockSpec((tf, D_MODEL), lambda i, f: (f, 0)),
        pl.BlockSpec((tm, D_MODEL), lambda i, f: (i, 0)),
        pl.BlockSpec((1, D_MODEL), lambda i, f: (0, 0)),
    ]
    return pl.pallas_call(
        _mlp_body,
        grid=(m // tm, D_FF // tf),
        in_specs=in_specs,
        out_specs=pl.BlockSpec((tm, D_MODEL), lambda i, f: (i, 0)),
        out_shape=jax.ShapeDtypeStruct((m, D_MODEL), F32),
        scratch_shapes=[pltpu.VMEM((tm, D_MODEL), F32)],
        compiler_params=_cparams(("arbitrary", "arbitrary")),
        name="mlp",
    )(h2, wu, wd, x1, gf)


def _fetch_pages(pt_ref, step, slot, streams, start):
    n_pages = pt_ref.shape[1]
    for hbm, buf, sem in streams:
        for j in range(n_pages):
            cp = pltpu.make_async_copy(hbm.at[pt_ref[step, j]], buf.at[slot, j], sem.at[slot])
            if start:
                cp.start()
            else:
                cp.wait()


def _paged_loop(pt_ref, streams, step_fn):
    nb = pt_ref.shape[0]
    n_slots = streams[0][1].shape[0]
    ahead = n_slots - PAGE_GROUP
    assert nb % PAGE_GROUP == 0 and ahead % PAGE_GROUP == 0 and PAGE_GROUP <= ahead <= nb
    for s in range(ahead):
        _fetch_pages(pt_ref, s, s, streams, start=True)

    def body(g, carry):
        b0 = g * PAGE_GROUP

        @pl.when(b0 + ahead < nb)
        def _next():
            for i in range(PAGE_GROUP):
                row = b0 + ahead + i
                _fetch_pages(pt_ref, row, row % n_slots, streams, start=True)

        for i in range(PAGE_GROUP):
            _fetch_pages(pt_ref, b0 + i, (b0 + i) % n_slots, streams, start=False)
        for i in range(PAGE_GROUP):
            step_fn(b0 + i, (b0 + i) % n_slots, i)
        return carry

    lax.fori_loop(0, nb // PAGE_GROUP, body, 0)


def _sidx_body(pt_ref, qi_ref, w_ref, kin_ref, cache_hbm, out_ref, kt_s, pbuf, sem):
    n_pages = pt_ref.shape[1]
    page = pbuf.shape[3]
    past = n_pages * page
    lane = lax.broadcasted_iota(I32, (1, LANES), 1)

    def step(b, slot, lane_of_trip):
        qi = qi_ref[b]
        w = w_ref[b] * (IDX_HEAD_DIM ** -0.5)
        for j in range(n_pages):
            kt_s[lane_of_trip, :, j * page:(j + 1) * page] = pbuf[slot, j].astype(BF16)
        s = jnp.dot(qi, kt_s[lane_of_trip], preferred_element_type=F32)
        out_ref[b, :, 0:past] = jnp.sum(jnp.maximum(s, 0.0) * w, axis=0, keepdims=True)
        sn = jnp.sum(qi.astype(F32) * kin_ref[b].astype(BF16).astype(F32), axis=1, keepdims=True)
        rn = jnp.sum(jnp.maximum(sn, 0.0) * w, axis=0, keepdims=True)
        out_ref[b, :, past:past + LANES] = jnp.where(lane == 0, rn, -jnp.inf)

    _paged_loop(pt_ref, ((cache_hbm, pbuf, sem),), step)


def _sample_index_scores(page_table, qi_s, wi_s, ki_s, cache_idx_k_t):
    nb, n_pages = page_table.shape
    page = cache_idx_k_t.shape[2]
    width = n_pages * page + LANES

    vmem = pl.BlockSpec(memory_space=pltpu.VMEM)
    return pl.pallas_call(
        _sidx_body,
        in_specs=[pl.BlockSpec(memory_space=pltpu.SMEM), vmem, vmem, vmem,
                  pl.BlockSpec(memory_space=pl.ANY)],
        out_specs=vmem,
        out_shape=jax.ShapeDtypeStruct((nb, 1, width), F32),
        scratch_shapes=[pltpu.VMEM((PAGE_GROUP, IDX_HEAD_DIM, n_pages * page), BF16),
                        pltpu.VMEM((IDX_PAGE_SLOTS, n_pages, IDX_HEAD_DIM, page), F32),
                        pltpu.SemaphoreType.DMA((IDX_PAGE_SLOTS,))],
        compiler_params=pltpu.CompilerParams(vmem_limit_bytes=VMEM_LIMIT),
        name="sample_idx",
    )(page_table, qi_s, wi_s, ki_s, cache_idx_k_t)


def _ssel_body(sc_ref, tri_ref, sel_ref):
    rows, width = sc_ref.shape
    nt = width // LANES
    n_valid = (nt - 1) * LANES + 1

    def tile(kt):
        return sc_ref[:, kt * LANES:(kt + 1) * LANES]

    def count_cmp(cmp):
        acc = jnp.zeros((rows, LANES), F32)
        for kt in range(nt):
            acc = acc + jnp.where(cmp(tile(kt)), 1.0, 0.0)
        return jnp.broadcast_to(jnp.sum(acc, axis=1, keepdims=True), (rows, LANES))

    tf = _threshold_search(lambda c: count_cmp(lambda sc: sc >= c), 32, (rows, LANES))
    need = float(TOPK_MAX) - count_cmp(lambda sc: sc > tf)
    tie_off = jnp.zeros((rows, LANES), F32)
    for kt in range(nt):
        col = kt * LANES + lax.broadcasted_iota(I32, (rows, LANES), 1)
        sc = tile(kt)
        eq = sc == tf
        tie = jnp.where(eq, 1.0, 0.0)
        rank = jnp.dot(tie.astype(BF16), tri_ref[...], preferred_element_type=F32) + tie_off
        sel = ((sc > tf) | (eq & (rank <= need))) & (col < n_valid)
        sel_ref[:, kt * LANES:(kt + 1) * LANES] = jnp.where(sel, 1.0, 0.0)
        tie_off = tie_off + jnp.broadcast_to(jnp.sum(tie, axis=1, keepdims=True), (rows, LANES))


def _sample_select(scores2d, tri):
    rows, width = scores2d.shape
    return pl.pallas_call(
        _ssel_body,
        out_shape=jax.ShapeDtypeStruct((rows, width), F32),
        compiler_params=pltpu.CompilerParams(vmem_limit_bytes=VMEM_LIMIT),
        name="sample_select",
    )(scores2d, tri)


def _sattn_body(pt_ref, q_ref, sel_ref, kn_ref, vn_ref, ck_hbm, cv_hbm, o_ref,
                kbuf, vbuf, ksem, vsem):
    n_pages = pt_ref.shape[1]
    page = kbuf.shape[2]
    past = n_pages * page
    scale = ATTN_HEAD_DIM ** -0.5

    def step(b, slot, lane_of_trip):
        q = q_ref[b]
        k_all = kbuf[slot].reshape(past, ATTN_HEAD_DIM).astype(BF16)
        v_all = vbuf[slot].reshape(past, ATTN_HEAD_DIM).astype(BF16)
        s = lax.dot_general(q, k_all, (((1,), (1,)), ((), ())), preferred_element_type=F32)
        s = jnp.where(sel_ref[b, :, 0:past] > 0.5, s * scale, NEG_BIG)
        kn = kn_ref[b].astype(BF16).astype(F32)
        sn = jnp.sum(q.astype(F32) * kn, axis=1, keepdims=True) * scale
        sn = jnp.where(sel_ref[b, :, past:past + 1] > 0.5, sn, NEG_BIG)
        m = jnp.maximum(jnp.max(s, axis=1, keepdims=True), sn)
        pn = jnp.exp(sn - m)
        p = jnp.exp(s - m)
        l = pn + jnp.sum(p, axis=1, keepdims=True)
        acc = (pn * vn_ref[b].astype(BF16).astype(F32)
               + jnp.dot(p.astype(BF16), v_all, preferred_element_type=F32))
        o_ref[b] = (acc / l).astype(BF16)

    _paged_loop(pt_ref, ((ck_hbm, kbuf, ksem), (cv_hbm, vbuf, vsem)), step)


def _sample_attention(page_table, qa_s, sel3, ka_s, va_s, cache_k, cache_v):
    nb, n_pages = page_table.shape
    page = cache_k.shape[1]

    vmem = pl.BlockSpec(memory_space=pltpu.VMEM)
    hbm = pl.BlockSpec(memory_space=pl.ANY)
    return pl.pallas_call(
        _sattn_body,
        in_specs=[pl.BlockSpec(memory_space=pltpu.SMEM), vmem, vmem, vmem, vmem, hbm, hbm],
        out_specs=vmem,
        out_shape=jax.ShapeDtypeStruct((nb, ATTN_HEADS, ATTN_HEAD_DIM), BF16),
        scratch_shapes=[pltpu.VMEM((ATTN_PAGE_SLOTS, n_pages, page, ATTN_HEAD_DIM), F32),
                        pltpu.VMEM((ATTN_PAGE_SLOTS, n_pages, page, ATTN_HEAD_DIM), F32),
                        pltpu.SemaphoreType.DMA((ATTN_PAGE_SLOTS,)),
                        pltpu.SemaphoreType.DMA((ATTN_PAGE_SLOTS,))],
        compiler_params=pltpu.CompilerParams(vmem_limit_bytes=VMEM_LIMIT),
        name="sample_attn",
    )(page_table, qa_s, sel3, ka_s, va_s, cache_k, cache_v)


def _sret_body(qkvg_ref, st_ref, gam_ref, rg_ref, so_ref):
    ns = st_ref.shape[0]
    for s in range(ns):
        blk = qkvg_ref[s].astype(F32)
        q8 = blk[0:8]
        k8 = blk[8:16]
        v8 = blk[16:24]
        g8 = blk[24:32]
        q_t = q8.T
        k_t = k8.T
        qk = jnp.sum(q8 * k8, axis=1, keepdims=True)
        rows = []
        for h in range(RET_HEADS):
            r_old = st_ref[s, h]
            gam = gam_ref[h]
            qcol = jnp.broadcast_to(q_t[:, h:h + 1], (RET_DK, RET_DV))
            kcol = jnp.broadcast_to(k_t[:, h:h + 1], (RET_DK, RET_DV))
            vrow = v8[h:h + 1]
            q_r = jnp.sum(qcol * r_old, axis=0, keepdims=True)
            rows.append(gam * q_r + qk[h:h + 1] * vrow)
            so_ref[s, h] = gam * r_old + kcol * vrow
        ret = jnp.concatenate(rows, axis=0)
        rg_ref[s] = _gate(ret, g8).astype(BF16)


def _sample_retention(qkvg, state, gam, ns):
    nb = state.shape[0]
    return pl.pallas_call(
        _sret_body,
        grid=(nb // ns,),
        in_specs=[
            pl.BlockSpec((ns, 32, LANES), lambda i: (i, 0, 0)),
            pl.BlockSpec((ns, RET_HEADS, RET_DK, RET_DV), lambda i: (i, 0, 0, 0)),
            pl.BlockSpec((RET_HEADS, 1, LANES), lambda i: (0, 0, 0)),
        ],
        out_specs=(
            pl.BlockSpec((ns, RET_HEADS, RET_DV), lambda i: (i, 0, 0)),
            pl.BlockSpec((ns, RET_HEADS, RET_DK, RET_DV), lambda i: (i, 0, 0, 0)),
        ),
        out_shape=(
            jax.ShapeDtypeStruct((nb, RET_HEADS, RET_DV), BF16),
            jax.ShapeDtypeStruct(state.shape, F32),
        ),
        compiler_params=_cparams(("arbitrary",)),
        name="sample_ret",
    )(qkvg, state, gam)


def _rotary_table(pos):
    half = RET_DK // 2
    inv = ROPE_BASE ** (-np.arange(half, dtype=np.float64) / half)
    ang = np.asarray(pos, np.float64)[:, None] * inv[None, :]
    cos = np.cos(ang)
    sin = np.sin(ang)
    return jnp.asarray(np.concatenate([cos, cos, -sin, sin], axis=1), F32)


def _retention_constants():
    lg = np.log1p(-np.exp2(-5.0 - np.arange(RET_HEADS, dtype=np.float64)))
    n = RET_CHUNK
    i = np.arange(n, dtype=np.float64)
    diff = i[:, None] - i[None, :]
    decay = np.where(diff[None] >= 0, np.exp(np.maximum(diff, 0.0)[None] * lg[:, None, None]), 0.0)
    rsc = np.exp((i + 1.0)[None, :] * lg[:, None])
    zeta = np.exp((n - 1.0 - i)[None, :] * lg[:, None])
    gpow = np.exp(n * lg)
    gam1 = np.exp(lg)
    rsc_b = np.broadcast_to(rsc[:, :, None], (RET_HEADS, n, RET_DV))
    zeta_b = np.broadcast_to(zeta[:, :, None], (RET_HEADS, n, RET_DK))
    gpow_b = np.broadcast_to(gpow[:, None, None], (RET_HEADS, 1, RET_DV))
    gam1_b = np.broadcast_to(gam1[:, None, None], (RET_HEADS, 1, LANES))
    return tuple(jnp.asarray(a, F32) for a in (decay, rsc_b, zeta_b, gpow_b, gam1_b))


def _upper_tri(n):
    return jnp.asarray(np.triu(np.ones((n, n), np.float32)), BF16)


def _lower_tri(n):
    return jnp.asarray(np.tril(np.ones((n, n), np.float32)), BF16)


def _pad_lanes(v):
    return jnp.pad(v, (0, LANES - v.shape[0])).reshape(1, LANES)


def kernel(x_prompt, x_sample, cache_k, cache_v, cache_idx_k, state_ret, page_table,
           norm1_g, w_in, idx_k_norm_g, idx_k_norm_b, w_out, norm2_g, w_up, w_down, final_norm_g):
    batch, seq, _ = x_prompt.shape
    nb = x_sample.shape[0]
    past_len = page_table.shape[1] * cache_k.shape[1]
    half_mix = ATTN_HEADS * ATTN_HEAD_DIM

    wt = w_in.T.astype(BF16)
    wa = w_out[:half_mix].astype(BF16)
    wr = w_out[half_mix:].astype(BF16)
    g1 = norm1_g.reshape(1, D_MODEL)
    g2 = norm2_g.reshape(1, D_MODEL)
    gf = final_norm_g.reshape(1, D_MODEL)
    lng = _pad_lanes(idx_k_norm_g)
    lnb = _pad_lanes(idx_k_norm_b)
    decay, rsc_b, zeta_b, gpow_b, gam1_b = _retention_constants()

    xp = x_prompt.reshape(batch * seq, D_MODEL)
    cs_p = _rotary_table(np.arange(seq))
    qa_p, qi_p, ka_p, va_p, ki_p, kd_p, wi_p = _project_attn(
        xp, g1, wt, lng, lnb, tm=TILES["proj_attn_rows"])
    main_p = _project_ret(xp, g1, wt, cs_p, tm=TILES["proj_ret_rows"])
    attn_p = _prompt_attention(qa_p, qi_p, wi_p, ka_p, va_p, kd_p, _lower_tri(KEY_TILE),
                               batch, seq)
    rg_p, ret_state_p = _prompt_retention(main_p, decay, rsc_b, zeta_b, gpow_b, batch, seq)
    x1_p, h2_p = _out_projection(attn_p, rg_p, wa, wr, xp, g2, tm=TILES["out_proj_rows"])

    xs = x_sample.reshape(nb, D_MODEL)
    cs_s = _rotary_table(np.full((nb,), past_len))
    qa_s, qi_s, ka_s, va_s, ki_s, _, wi_s = _project_attn(xs, g1, wt, lng, lnb, tm=nb)
    main_s = _project_ret(xs, g1, wt, cs_s, tm=nb)
    scores = _sample_index_scores(
        page_table,
        qi_s.transpose(1, 0, 2).reshape(nb, IDX_HEADS, IDX_HEAD_DIM),
        wi_s.T.reshape(nb, IDX_HEADS, 1),
        ki_s.reshape(nb, 1, IDX_HEAD_DIM),
        jnp.swapaxes(cache_idx_k, 1, 2))
    width = scores.shape[2]
    sel = _sample_select(scores.reshape(nb, width), _upper_tri(LANES))
    attn_s = _sample_attention(
        page_table,
        qa_s.transpose(1, 0, 2),
        sel.reshape(nb, 1, width),
        ka_s.reshape(nb, 1, ATTN_HEAD_DIM),
        va_s.reshape(nb, 1, ATTN_HEAD_DIM),
        cache_k, cache_v)
    rg_s, ret_state_s = _sample_retention(main_s.reshape(nb, 32, LANES), state_ret, gam1_b,
                                          ns=TILES["sample_ret_rows"])
    x1_s, h2_s = _out_projection(attn_s.reshape(nb, half_mix), rg_s.reshape(nb, RET_WIDTH),
                                 wa, wr, xs, g2, tm=nb)
    y_s, wu, wd = _mlp_and_cast(h2_s, w_up, w_down, x1_s, gf, tf=TILES["mlp_cast_ff"])
    y_p = _mlp(h2_p, wu, wd, x1_p, gf, tm=TILES["mlp_rows"], tf=TILES["mlp_ff"])

    return (
        y_p.reshape(batch, seq, D_MODEL),
        y_s.reshape(nb, 1, D_MODEL),
        ka_p.reshape(batch, seq, ATTN_HEAD_DIM),
        va_p.reshape(batch, seq, ATTN_HEAD_DIM),
        ki_p.reshape(batch, seq, IDX_HEAD_DIM),
        ret_state_p,
        ka_s.reshape(nb, 1, ATTN_HEAD_DIM),
        va_s.reshape(nb, 1, ATTN_HEAD_DIM),
        ki_s.reshape(nb, 1, IDX_HEAD_DIM),
        ret_state_s,
    )
```

```python
import functools

import numpy as np
import jax
import jax.numpy as jnp
from jax import lax
from jax.experimental import pallas as pl
from jax.experimental.pallas import tpu as pltpu

F32 = jnp.float32
BF16 = jnp.bfloat16
I32 = jnp.int32

D_MODEL = 2048
ATTN_HEADS = 8
ATTN_HEAD_DIM = 128
IDX_HEADS = 16
IDX_HEAD_DIM = 64
TOPK_MAX = 256
RET_HEADS = 8
RET_DK = 128
RET_DV = 128
RET_CHUNK = 256
ROPE_BASE = 10000.0
D_FF = 4 * D_MODEL
EPS = 1e-6
Q_BLOCK = 256

OFF_QA, OFF_KA, OFF_VA, OFF_QI, OFF_KI, OFF_WI = 0, 1024, 1152, 1280, 2304, 2368
OFF_QR, OFF_KR, OFF_VR, OFF_GR = 2384, 3408, 4432, 5456
RET_WIDTH = RET_HEADS * RET_DV

LANES = 128
PROJ_TILE = 512
KEY_TILE = 256
COUNT_ROWS = 32
PAGE_GROUP = 2
IDX_PAGE_SLOTS = 8
ATTN_PAGE_SLOTS = 6
SUM_ROWS = 16
LOG2_E = 1.4426950408889634
INT_MIN = -2 ** 31
KEY_NEG_INF = -2 ** 31 + 0x7FFFFF
BF16_KEY_NEG_INF = -2 ** 15 + 0x7F
NEG_BIG = -1e30
VMEM_LIMIT = 56 * 1024 * 1024

TILES = {
    "proj_attn_rows": 1024,
    "proj_ret_rows": 512,
    "out_proj_rows": 512,
    "mlp_rows": 512,
    "mlp_ff": 1024,
    "mlp_cast_ff": 1024,
    "sample_ret_rows": 8,
}


def _cparams(sem):
    return pltpu.CompilerParams(dimension_semantics=sem, vmem_limit_bytes=VMEM_LIMIT)


def _resident(shape):
    zeros = (0,) * len(shape)
    return pl.BlockSpec(shape, lambda *_: zeros, pipeline_mode=pl.Buffered(1))


def _normed_input(x_ref, g_ref, xn_ref):
    x = x_ref[...]
    ms = jnp.mean(x * x, axis=-1, keepdims=True)
    xn_ref[...] = (x * lax.rsqrt(ms + EPS) * g_ref[...]).astype(BF16)


def _matmul_rows(xn_ref, wt_ref, r0, n):
    return lax.dot_general(xn_ref[...], wt_ref[r0:r0 + n, :], (((1,), (1,)), ((), ())),
                           preferred_element_type=F32)


def _proj_attn_body(x_ref, g_ref, wt_ref, lng_ref, lnb_ref,
                    qa_ref, qi_ref, ka_ref, va_ref, ki_ref, kd_ref, wi_ref, xn_ref):
    _normed_input(x_ref, g_ref, xn_ref)
    mm = functools.partial(_matmul_rows, xn_ref, wt_ref)
    for t in range(ATTN_HEADS * ATTN_HEAD_DIM // PROJ_TILE):
        acc = mm(OFF_QA + t * PROJ_TILE, PROJ_TILE)
        for hh in range(4):
            qa_ref[4 * t + hh] = acc[:, hh * LANES:(hh + 1) * LANES].astype(BF16)
    for t in range(IDX_HEADS * IDX_HEAD_DIM // PROJ_TILE):
        acc = mm(OFF_QI + t * PROJ_TILE, PROJ_TILE)
        for hh in range(4):
            qi_ref[4 * t + hh] = acc[:, hh * LANES:(hh + 1) * LANES].astype(BF16)
    kv = mm(OFF_KA, 2 * ATTN_HEAD_DIM)
    ka_ref[...] = kv[:, :ATTN_HEAD_DIM]
    va_ref[...] = kv[:, ATTN_HEAD_DIM:]
    kw = mm(OFF_KI, LANES)
    lane = lax.broadcasted_iota(I32, kw.shape, 1)
    is_k = lane < IDX_HEAD_DIM
    mu = jnp.sum(jnp.where(is_k, kw, 0.0), axis=-1, keepdims=True) * (1.0 / IDX_HEAD_DIM)
    d = jnp.where(is_k, kw - mu, 0.0)
    var = jnp.sum(d * d, axis=-1, keepdims=True) * (1.0 / IDX_HEAD_DIM)
    kn = d * lax.rsqrt(var + EPS) * lng_ref[...] + lnb_ref[...]
    ki_ref[...] = kn[:, :IDX_HEAD_DIM]
    kd_ref[...] = jnp.where(is_k, kn, pltpu.roll(kn, IDX_HEAD_DIM, 1)).astype(BF16)
    wi_ref[...] = (kw[:, IDX_HEAD_DIM:IDX_HEAD_DIM + IDX_HEADS] * (IDX_HEADS ** -0.5)).T


def _proj_ret_body(x_ref, g_ref, wt_ref, cs_ref, main_ref, xn_ref):
    _normed_input(x_ref, g_ref, xn_ref)
    base = 0
    cosf = cs_ref[:, :LANES]
    sinf = cs_ref[:, LANES:]
    tiles = RET_WIDTH // PROJ_TILE
    for seg, (off, scale) in enumerate(((OFF_QR, None), (OFF_KR, RET_DK ** -0.5))):
        for t in range(tiles):
            acc = _matmul_rows(xn_ref, wt_ref, off - base + t * PROJ_TILE, PROJ_TILE)
            for hh in range(PROJ_TILE // LANES):
                xh = acc[:, hh * LANES:(hh + 1) * LANES]
                r = xh * cosf + pltpu.roll(xh, RET_DK // 2, 1) * sinf
                if scale is not None:
                    r = r * scale
                c0 = seg * RET_WIDTH + t * PROJ_TILE + hh * LANES
                main_ref[:, c0:c0 + LANES] = r.astype(BF16)
    for seg, off in ((2, OFF_VR), (3, OFF_GR)):
        for t in range(tiles):
            acc = _matmul_rows(xn_ref, wt_ref, off - base + t * PROJ_TILE, PROJ_TILE)
            c0 = seg * RET_WIDTH + t * PROJ_TILE
            main_ref[:, c0:c0 + PROJ_TILE] = acc.astype(BF16)


def _project_attn(x2d, g1, wt_attn, lng, lnb, tm):
    m = x2d.shape[0]
    row = lambda i: (i, 0)
    out_shape = (
        jax.ShapeDtypeStruct((ATTN_HEADS, m, ATTN_HEAD_DIM), BF16),
        jax.ShapeDtypeStruct((IDX_HEADS // 2, m, LANES), BF16),
        jax.ShapeDtypeStruct((m, ATTN_HEAD_DIM), F32),
        jax.ShapeDtypeStruct((m, ATTN_HEAD_DIM), F32),
        jax.ShapeDtypeStruct((m, IDX_HEAD_DIM), F32),
        jax.ShapeDtypeStruct((m, LANES), BF16),
        jax.ShapeDtypeStruct((IDX_HEADS, m), F32),
    )
    out_specs = (
        pl.BlockSpec((ATTN_HEADS, tm, ATTN_HEAD_DIM), lambda i: (0, i, 0)),
        pl.BlockSpec((IDX_HEADS // 2, tm, LANES), lambda i: (0, i, 0)),
        pl.BlockSpec((tm, ATTN_HEAD_DIM), row),
        pl.BlockSpec((tm, ATTN_HEAD_DIM), row),
        pl.BlockSpec((tm, IDX_HEAD_DIM), row),
        pl.BlockSpec((tm, LANES), row),
        pl.BlockSpec((IDX_HEADS, tm), lambda i: (0, i)),
    )
    return pl.pallas_call(
        _proj_attn_body,
        grid=(m // tm,),
        in_specs=[pl.BlockSpec((tm, D_MODEL), row), _resident((1, D_MODEL)),
                  _resident((OFF_QR + LANES, D_MODEL)), _resident((1, LANES)),
                  _resident((1, LANES))],
        out_specs=out_specs,
        out_shape=out_shape,
        scratch_shapes=[pltpu.VMEM((tm, D_MODEL), BF16)],
        compiler_params=_cparams(("arbitrary",)),
        name="proj_attn",
    )(x2d, g1, wt_attn, lng, lnb)


def _project_ret(x2d, g1, wt_ret, cs, tm):
    m = x2d.shape[0]
    n_pos_blocks = cs.shape[0] // tm
    row = lambda i: (i, 0)
    return pl.pallas_call(
        _proj_ret_body,
        grid=(m // tm,),
        in_specs=[pl.BlockSpec((tm, D_MODEL), row), _resident((1, D_MODEL)),
                  _resident(wt_ret.shape),
                  pl.BlockSpec((tm, 2 * LANES), lambda i: (i % n_pos_blocks, 0))],
        out_specs=pl.BlockSpec((tm, 4 * RET_WIDTH), row),
        out_shape=jax.ShapeDtypeStruct((m, 4 * RET_WIDTH), BF16),
        scratch_shapes=[pltpu.VMEM((tm, D_MODEL), BF16)],
        compiler_params=_cparams(("arbitrary",)),
        name="proj_ret",
    )(x2d, g1, wt_ret, cs)


def _key_to_float(key):
    bits = key ^ ((key >> 31) & 0x7FFFFFFF)
    return lax.bitcast_convert_type(bits, F32)


def _threshold_search(count_ge, n_iter, shape):
    def body(it, t):
        bit = lax.shift_left(jnp.int32(1), 31 - it)
        cand = t ^ bit
        cnt = count_ge(_key_to_float(cand))
        return jnp.where(cnt >= float(TOPK_MAX), cand, t)

    t = lax.fori_loop(0, n_iter, body, jnp.full(shape, INT_MIN, I32))
    return _key_to_float(jnp.maximum(t, KEY_NEG_INF))


def _bf16_key_to_f32_key(k16):
    return lax.shift_left(k16, 16) | jnp.where(k16 < 0, 0xFFFF, 0)


def _threshold_search_coarse_fine(count_ge_bf16, count_ge, run, shape):
    def coarse(it, u):
        cand = u | lax.shift_left(jnp.int32(1), 15 - it)
        c = _key_to_float(_bf16_key_to_f32_key(cand - 32768)).astype(BF16)
        return jnp.where(count_ge_bf16(c) >= float(TOPK_MAX), cand, u)

    u = lax.fori_loop(0, jnp.where(run, 16, 0), coarse, jnp.zeros(shape, I32))
    k1 = jnp.maximum(u - 32768, BF16_KEY_NEG_INF)
    lo = _bf16_key_to_f32_key(jnp.maximum(k1 - 1, -32768))
    hi = _bf16_key_to_f32_key(jnp.minimum(k1 + 1, 32767))

    def fine(it, t):
        cand = t + lax.shift_left(jnp.int32(1), 16 - it)
        ok = (cand < hi) & (count_ge(_key_to_float(cand)) >= float(TOPK_MAX))
        return jnp.where(ok, cand, t)

    t = lax.fori_loop(0, jnp.where(run, 17, 0), fine, lo)
    return _key_to_float(jnp.maximum(t, KEY_NEG_INF))


def _attn_body(qa_ref, qi_ref, wit_ref, ka_ref, va_ref, kd_ref, tri_ref, o_ref,
               kbf, vtb, scr, mrun, acc_s, kmax, scr16):
    qb = pl.program_id(1)
    n_heads_q = ATTN_HEADS * Q_BLOCK
    n_pairs = IDX_HEADS // 2
    dv = ATTN_HEAD_DIM
    logit_scale = ATTN_HEAD_DIM ** -0.5 * LOG2_E

    @pl.when(qb == 0)
    def _cast():
        ka = ka_ref[...]
        kbf[...] = ka.astype(BF16)
        kmax[...] = jnp.broadcast_to(jnp.max(jnp.sum(ka * ka, axis=1, keepdims=True)), kmax.shape)
        for kt in range(vtb.shape[0]):
            vtb[kt, :dv] = va_ref[kt * KEY_TILE:(kt + 1) * KEY_TILE, :].T.astype(BF16)
            vtb[kt, dv:] = jnp.ones((vtb.shape[1] - dv, KEY_TILE), BF16)

    nk = ((qb + 1) * Q_BLOCK + KEY_TILE - 1) // KEY_TILE
    wt = wit_ref[...] * (IDX_HEAD_DIM ** -0.5)
    qi2 = qi_ref[...].reshape(n_pairs * Q_BLOCK, LANES)
    lo_half = lax.broadcasted_iota(I32, (KEY_TILE, LANES), 1) < IDX_HEAD_DIM
    qidx = qb * Q_BLOCK + lax.broadcasted_iota(I32, (KEY_TILE, Q_BLOCK), 1)
    kidx0 = lax.broadcasted_iota(I32, (KEY_TILE, Q_BLOCK), 0)
    contract_last = (((1,), (1,)), ((), ()))

    def idx_body(kt, carry):
        off = pl.multiple_of(kt * KEY_TILE, KEY_TILE)
        kit = kd_ref[pl.ds(off, KEY_TILE), :]
        zero = jnp.zeros_like(kit)
        s_even = lax.dot_general(jnp.where(lo_half, kit, zero), qi2, contract_last,
                                 preferred_element_type=F32)
        s_odd = lax.dot_general(jnp.where(lo_half, zero, kit), qi2, contract_last,
                                preferred_element_type=F32)
        score = jnp.zeros((KEY_TILE, Q_BLOCK), F32)
        for g in range(n_pairs):
            cs = slice(g * Q_BLOCK, (g + 1) * Q_BLOCK)
            score = score + jnp.maximum(s_even[:, cs], 0.0) * wt[2 * g:2 * g + 1, :]
            score = score + jnp.maximum(s_odd[:, cs], 0.0) * wt[2 * g + 1:2 * g + 2, :]
        score = jnp.where(kidx0 + off <= qidx, score, -jnp.inf)
        scr[kt] = score
        scr16[kt] = score.astype(BF16)
        return carry

    def for_tiles(fn):
        def pair(j, carry):
            fn(2 * j, 0)
            fn(2 * j + 1, 0)
            return carry

        lax.fori_loop(0, nk // 2, pair, 0)

        @pl.when(nk % 2 == 1)
        def _last():
            fn(nk - 1, 0)

    for_tiles(idx_body)

    def count_ge_bf16(c):
        def body(kt, acc):
            hit = scr16[kt] >= c
            for r in range(KEY_TILE // COUNT_ROWS):
                acc = jnp.where(hit[r * COUNT_ROWS:(r + 1) * COUNT_ROWS], acc + 1.0, acc)
            return acc
        acc = lax.fori_loop(0, nk, body, jnp.zeros((COUNT_ROWS, Q_BLOCK), BF16))
        return jnp.sum(acc.astype(F32), axis=0, keepdims=True)

    def count_cmp(cmp):
        def body(kt, acc):
            hit = cmp(scr[kt])
            for r in range(KEY_TILE // COUNT_ROWS):
                acc = jnp.where(hit[r * COUNT_ROWS:(r + 1) * COUNT_ROWS], acc + 1.0, acc)
            return acc
        acc = lax.fori_loop(0, nk, body, jnp.zeros((COUNT_ROWS, Q_BLOCK), F32))
        return jnp.sum(acc, axis=0, keepdims=True)

    tf = _threshold_search_coarse_fine(
        count_ge_bf16, lambda c: count_cmp(lambda sc: sc >= c),
        qb >= TOPK_MAX // Q_BLOCK, (1, Q_BLOCK))
    excess = jnp.max(count_cmp(lambda sc: sc >= tf)) > float(TOPK_MAX)

    qa2 = qa_ref[...].reshape(n_heads_q, ATTN_HEAD_DIM)

    def logits(kt):
        off = pl.multiple_of(kt * KEY_TILE, KEY_TILE)
        s = lax.dot_general(kbf[pl.ds(off, KEY_TILE), :], qa2, contract_last,
                            preferred_element_type=F32)
        return s * logit_scale

    def sel_plain(kt, carry):
        off = pl.multiple_of(kt * KEY_TILE, KEY_TILE)
        return (scr[kt] >= tf) & (kidx0 + off <= qidx), carry

    def sel_ties(need, kt, tie_off):
        off = pl.multiple_of(kt * KEY_TILE, KEY_TILE)
        sc = scr[kt]
        eq = sc == tf
        tie = jnp.where(eq, 1.0, 0.0)
        rank = jnp.dot(tri_ref[...], tie.astype(BF16), preferred_element_type=F32) + tie_off
        sel = ((sc > tf) | (eq & (rank <= need))) & (kidx0 + off <= qidx)
        return sel, tie_off + jnp.sum(tie, axis=0, keepdims=True)

    no_ties = jnp.zeros((1, Q_BLOCK), F32)

    def softmax_sum(m, sel_fn):
        acc_s[...] = jnp.zeros(acc_s.shape, F32)

        def body(kt, carry):
            sel, carry = sel_fn(kt, carry)
            e = jnp.exp2(logits(kt) - m)
            parts = []
            for h in range(ATTN_HEADS):
                cs = slice(h * Q_BLOCK, (h + 1) * Q_BLOCK)
                parts.append(jnp.where(sel, e[:, cs], 0.0).astype(BF16))
            p = jnp.concatenate(parts, axis=1)
            acc_s[...] += jnp.dot(vtb[kt], p, preferred_element_type=F32)
            return carry

        if sel_fn is sel_plain:
            for_tiles(body)
        else:
            lax.fori_loop(0, nk, body, no_ties)

    def selected_max(sel_fn):
        mrun[...] = jnp.full(mrun.shape, NEG_BIG, F32)

        def body(kt, carry):
            sel, carry = sel_fn(kt, carry)
            s = logits(kt)
            for h in range(ATTN_HEADS):
                cs = slice(h * Q_BLOCK, (h + 1) * Q_BLOCK)
                sh = jnp.where(sel, s[:, cs], NEG_BIG)
                mrun[:, cs] = jnp.maximum(
                    mrun[:, cs], jnp.max(sh.reshape(KEY_TILE // 8, 8, Q_BLOCK), axis=0))
            return carry

        lax.fori_loop(0, nk, body, no_ties)
        return jnp.max(mrun[...], axis=0, keepdims=True)

    def fast_path():
        q2 = (qa2 * qa2).astype(BF16)
        qsq = lax.dot_general(jnp.ones((8, ATTN_HEAD_DIM), BF16), q2, contract_last,
                              preferred_element_type=F32)[0:1]
        softmax_sum(jnp.sqrt(qsq * kmax[0:1, 0:1]) * logit_scale, sel_plain)
        return (jnp.min(acc_s[dv:dv + 1, :]) > 0.0).astype(I32)

    done = lax.cond(excess, lambda: jnp.int32(0), fast_path) == 1

    @pl.when(jnp.logical_not(done) & excess)
    def _exact_with_ties():
        need = float(TOPK_MAX) - count_cmp(lambda sc: sc > tf)
        sel_fn = functools.partial(sel_ties, need)
        softmax_sum(selected_max(sel_fn), sel_fn)

    @pl.when(jnp.logical_not(done) & jnp.logical_not(excess))
    def _exact_without_ties():
        softmax_sum(selected_max(sel_plain), sel_plain)

    out = acc_s[:dv, :] / acc_s[dv:dv + 1, :]
    for h in range(ATTN_HEADS):
        oh = out[:, h * Q_BLOCK:(h + 1) * Q_BLOCK].T
        o_ref[:, h * ATTN_HEAD_DIM:(h + 1) * ATTN_HEAD_DIM] = oh.astype(BF16)


def _prompt_attention(qa_hm, qi_pm, wi_t, ka, va, kd, tri, batch, seq):
    nq = seq // Q_BLOCK
    nkt = seq // KEY_TILE
    m = batch * seq
    n_heads_q = ATTN_HEADS * Q_BLOCK
    assert seq // COUNT_ROWS <= 256
    in_specs = [
        pl.BlockSpec((ATTN_HEADS, Q_BLOCK, ATTN_HEAD_DIM), lambda b, q: (0, b * nq + q, 0)),
        pl.BlockSpec((IDX_HEADS // 2, Q_BLOCK, LANES), lambda b, q: (0, b * nq + q, 0)),
        pl.BlockSpec((IDX_HEADS, Q_BLOCK), lambda b, q: (0, b * nq + q)),
        pl.BlockSpec((seq, ATTN_HEAD_DIM), lambda b, q: (b, 0)),
        pl.BlockSpec((seq, ATTN_HEAD_DIM), lambda b, q: (b, 0)),
        pl.BlockSpec((seq, LANES), lambda b, q: (b, 0)),
        pl.BlockSpec((KEY_TILE, KEY_TILE), lambda b, q: (0, 0)),
    ]
    return pl.pallas_call(
        _attn_body,
        grid=(batch, nq),
        in_specs=in_specs,
        out_specs=pl.BlockSpec((Q_BLOCK, ATTN_HEADS * ATTN_HEAD_DIM), lambda b, q: (b * nq + q, 0)),
        out_shape=jax.ShapeDtypeStruct((m, ATTN_HEADS * ATTN_HEAD_DIM), BF16),
        scratch_shapes=[
            pltpu.VMEM((seq, ATTN_HEAD_DIM), BF16),
            pltpu.VMEM((nkt, ATTN_HEAD_DIM + SUM_ROWS, KEY_TILE), BF16),
            pltpu.VMEM((nkt, KEY_TILE, Q_BLOCK), F32),
            pltpu.VMEM((8, n_heads_q), F32),
            pltpu.VMEM((ATTN_HEAD_DIM + SUM_ROWS, n_heads_q), F32),
            pltpu.VMEM((8, LANES), F32),
            pltpu.VMEM((nkt, KEY_TILE, Q_BLOCK), BF16),
        ],
        compiler_params=_cparams(("arbitrary", "arbitrary")),
        name="prompt_attn",
    )(qa_hm, qi_pm, wi_t, ka, va, kd, tri)


def _gate(o, g):
    rn = o * lax.rsqrt(jnp.mean(o * o, axis=-1, keepdims=True) + EPS)
    return rn * (g / (1.0 + jnp.exp(-g)))


def _ret_body(q_ref, k_ref, v_ref, g_ref, decay_ref, rsc_ref, zeta_ref, gpow_ref,
              rg_ref, st_ref):
    c = pl.program_id(1)

    @pl.when(c == 0)
    def _init():
        st_ref[...] = jnp.zeros(st_ref.shape, F32)

    for h in range(RET_HEADS):
        sl = slice(h * 128, (h + 1) * 128)
        q = q_ref[:, sl]
        k = k_ref[:, sl]
        v = v_ref[:, sl]
        r_old = st_ref[0, h]
        qk = lax.dot_general(q, k, (((1,), (1,)), ((), ())), preferred_element_type=F32)
        inner = jnp.dot((qk * decay_ref[h]).astype(BF16), v, preferred_element_type=F32)
        cross = jnp.dot(q, r_old.astype(BF16), preferred_element_type=F32) * rsc_ref[h]
        kz = (k.astype(F32) * zeta_ref[h]).astype(BF16)
        upd = lax.dot_general(kz, v, (((0,), (0,)), ((), ())), preferred_element_type=F32)
        st_ref[0, h] = r_old * gpow_ref[h] + upd
        rg_ref[:, sl] = _gate(inner + cross, g_ref[:, sl].astype(F32)).astype(BF16)


def _prompt_retention(main, decay, rsc, zeta, gpow, batch, seq):
    nc = seq // RET_CHUNK
    m = batch * seq
    width = RET_WIDTH
    const3 = lambda b, c: (0, 0, 0)
    in_specs = [
        pl.BlockSpec((RET_CHUNK, width), lambda b, c: (b * nc + c, 0)),
        pl.BlockSpec((RET_CHUNK, width), lambda b, c: (b * nc + c, 1)),
        pl.BlockSpec((RET_CHUNK, width), lambda b, c: (b * nc + c, 2)),
        pl.BlockSpec((RET_CHUNK, width), lambda b, c: (b * nc + c, 3)),
        pl.BlockSpec((RET_HEADS, RET_CHUNK, RET_CHUNK), const3),
        pl.BlockSpec((RET_HEADS, RET_CHUNK, RET_DV), const3),
        pl.BlockSpec((RET_HEADS, RET_CHUNK, RET_DK), const3),
        pl.BlockSpec((RET_HEADS, 1, RET_DV), const3),
    ]
    return pl.pallas_call(
        _ret_body,
        grid=(batch, nc),
        in_specs=in_specs,
        out_specs=(
            pl.BlockSpec((RET_CHUNK, width), lambda b, c: (b * nc + c, 0)),
            pl.BlockSpec((1, RET_HEADS, RET_DK, RET_DV), lambda b, c: (b, 0, 0, 0)),
        ),
        out_shape=(
            jax.ShapeDtypeStruct((m, width), BF16),
            jax.ShapeDtypeStruct((batch, RET_HEADS, RET_DK, RET_DV), F32),
        ),
        compiler_params=_cparams(("arbitrary", "arbitrary")),
        name="prompt_ret",
    )(main, main, main, main, decay, rsc, zeta, gpow)


def _outproj_body(a_ref, r_ref, wa_ref, wr_ref, x_ref, g2_ref, x1_ref, h2_ref):
    mixed = (jnp.dot(a_ref[...], wa_ref[...], preferred_element_type=F32)
             + jnp.dot(r_ref[...], wr_ref[...], preferred_element_type=F32))
    x1 = x_ref[...] + mixed
    x1_ref[...] = x1
    ms = jnp.mean(x1 * x1, axis=-1, keepdims=True)
    h2_ref[...] = (x1 * lax.rsqrt(ms + EPS) * g2_ref[...]).astype(BF16)


def _out_projection(attn_o, rg, wa, wr, x2d, g2, tm):
    m = x2d.shape[0]
    half = attn_o.shape[1]
    in_specs = [
        pl.BlockSpec((tm, half), lambda i: (i, 0)),
        pl.BlockSpec((tm, half), lambda i: (i, 0)),
        pl.BlockSpec((half, D_MODEL), lambda i: (0, 0)),
        pl.BlockSpec((half, D_MODEL), lambda i: (0, 0)),
        pl.BlockSpec((tm, D_MODEL), lambda i: (i, 0)),
        pl.BlockSpec((1, D_MODEL), lambda i: (0, 0)),
    ]
    return pl.pallas_call(
        _outproj_body,
        grid=(m // tm,),
        in_specs=in_specs,
        out_specs=(pl.BlockSpec((tm, D_MODEL), lambda i: (i, 0)),
                   pl.BlockSpec((tm, D_MODEL), lambda i: (i, 0))),
        out_shape=(jax.ShapeDtypeStruct((m, D_MODEL), F32),
                   jax.ShapeDtypeStruct((m, D_MODEL), BF16)),
        compiler_params=_cparams(("arbitrary",)),
        name="out_proj",
    )(attn_o, rg, wa, wr, x2d, g2)


def _mlp_body(h2_ref, wu_ref, wd_ref, x1_ref, gf_ref, y_ref, acc_ref):
    f = pl.program_id(1)

    @pl.when(f == 0)
    def _init():
        acc_ref[...] = x1_ref[...]

    u = jnp.dot(h2_ref[...], wu_ref[...], preferred_element_type=F32)
    a = jnp.maximum(u, 0.0)
    acc_ref[...] += jnp.dot((a * a).astype(BF16), wd_ref[...], preferred_element_type=F32)

    @pl.when(f == pl.num_programs(1) - 1)
    def _final():
        x2 = acc_ref[...]
        ms = jnp.mean(x2 * x2, axis=-1, keepdims=True)
        y_ref[...] = x2 * lax.rsqrt(ms + EPS) * gf_ref[...]


def _mlp_cast_body(h2_ref, wu_ref, wd_ref, x1_ref, gf_ref, y_ref, wub_ref, wdb_ref, acc_ref):
    f = pl.program_id(0)

    @pl.when(f == 0)
    def _init():
        acc_ref[...] = x1_ref[...]

    wu = wu_ref[...].astype(BF16)
    wd = wd_ref[...].astype(BF16)
    wub_ref[...] = wu
    wdb_ref[...] = wd
    u = jnp.dot(h2_ref[...], wu, preferred_element_type=F32)
    a = jnp.maximum(u, 0.0)
    acc_ref[...] += jnp.dot((a * a).astype(BF16), wd, preferred_element_type=F32)

    @pl.when(f == pl.num_programs(0) - 1)
    def _final():
        x2 = acc_ref[...]
        ms = jnp.mean(x2 * x2, axis=-1, keepdims=True)
        y_ref[...] = x2 * lax.rsqrt(ms + EPS) * gf_ref[...]


def _mlp_and_cast(h2, w_up, w_down, x1, gf, tf):
    m = h2.shape[0]
    full = lambda f: (0, 0)
    return pl.pallas_call(
        _mlp_cast_body,
        grid=(D_FF // tf,),
        in_specs=[
            pl.BlockSpec((m, D_MODEL), full),
            pl.BlockSpec((D_MODEL, tf), lambda f: (0, f)),
            pl.BlockSpec((tf, D_MODEL), lambda f: (f, 0)),
            pl.BlockSpec((m, D_MODEL), full),
            pl.BlockSpec((1, D_MODEL), full),
        ],
        out_specs=(
            pl.BlockSpec((m, D_MODEL), full),
            pl.BlockSpec((D_MODEL, tf), lambda f: (0, f)),
            pl.BlockSpec((tf, D_MODEL), lambda f: (f, 0)),
        ),
        out_shape=(
            jax.ShapeDtypeStruct((m, D_MODEL), F32),
            jax.ShapeDtypeStruct((D_MODEL, D_FF), BF16),
            jax.ShapeDtypeStruct((D_FF, D_MODEL), BF16),
        ),
        scratch_shapes=[pltpu.VMEM((m, D_MODEL), F32)],
        compiler_params=_cparams(("arbitrary",)),
        name="mlp_cast",
    )(h2, w_up, w_down, x1, gf)


def _mlp(h2, wu, wd, x1, gf, tm, tf):
    m = h2.shape[0]
    in_specs = [
        pl.BlockSpec((tm, D_MODEL), lambda i, f: (i, 0)),
        pl.BlockSpec((D_MODEL, tf), lambda i, f: (0, f)),
        pl.BlockSpec((tf, D_MODEL), lambda i, f: (f, 0)),
        pl.BlockSpec((tm, D_MODEL), lambda i, f: (i, 0)),
        pl.BlockSpec((1, D_MODEL), lambda i, f: (0, 0)),
    ]
    return pl.pallas_call(
        _mlp_body,
        grid=(m // tm, D_FF // tf),
        in_specs=in_specs,
        out_specs=pl.BlockSpec((tm, D_MODEL), lambda i, f: (i, 0)),
        out_shape=jax.ShapeDtypeStruct((m, D_MODEL), F32),
        scratch_shapes=[pltpu.VMEM((tm, D_MODEL), F32)],
        compiler_params=_cparams(("arbitrary", "arbitrary")),
        name="mlp",
    )(h2, wu, wd, x1, gf)


def _fetch_pages(pt_ref, step, slot, streams, start):
    n_pages = pt_ref.shape[1]
    for hbm, buf, sem in streams:
        for j in range(n_pages):
            cp = pltpu.make_async_copy(hbm.at[pt_ref[step, j]], buf.at[slot, j], sem.at[slot])
            if start:
                cp.start(priority=j % 2)
            else:
                cp.wait()


def _paged_loop(pt_ref, streams, step_fn):
    nb = pt_ref.shape[0]
    n_slots = streams[0][1].shape[0]
    ahead = n_slots - PAGE_GROUP
    assert nb % PAGE_GROUP == 0 and ahead % PAGE_GROUP == 0 and PAGE_GROUP <= ahead <= nb
    for s in range(ahead):
        _fetch_pages(pt_ref, s, s, streams, start=True)

    def body(g, carry):
        b0 = g * PAGE_GROUP

        @pl.when(b0 + ahead < nb)
        def _next():
            for i in range(PAGE_GROUP):
                row = b0 + ahead + i
                _fetch_pages(pt_ref, row, row % n_slots, streams, start=True)

        for i in range(PAGE_GROUP):
            _fetch_pages(pt_ref, b0 + i, (b0 + i) % n_slots, streams, start=False)
        for i in range(PAGE_GROUP):
            step_fn(b0 + i, (b0 + i) % n_slots, i)
        return carry

    lax.fori_loop(0, nb // PAGE_GROUP, body, 0)


def _sidx_body(pt_ref, qi_ref, w_ref, kin_ref, cache_hbm, out_ref, kt_s, pbuf, sem):
    n_pages = pt_ref.shape[1]
    page = pbuf.shape[3]
    past = n_pages * page
    lane = lax.broadcasted_iota(I32, (1, LANES), 1)

    def step(b, slot, lane_of_trip):
        qi = qi_ref[b]
        w = w_ref[b] * (IDX_HEAD_DIM ** -0.5)
        for j in range(n_pages):
            kt_s[lane_of_trip, :, j * page:(j + 1) * page] = pbuf[slot, j].astype(BF16)
        s = jnp.dot(qi, kt_s[lane_of_trip], preferred_element_type=F32)
        out_ref[b, :, 0:past] = jnp.sum(jnp.maximum(s, 0.0) * w, axis=0, keepdims=True)
        sn = jnp.sum(qi.astype(F32) * kin_ref[b].astype(BF16).astype(F32), axis=1, keepdims=True)
        rn = jnp.sum(jnp.maximum(sn, 0.0) * w, axis=0, keepdims=True)
        out_ref[b, :, past:past + LANES] = jnp.where(lane == 0, rn, -jnp.inf)

    _paged_loop(pt_ref, ((cache_hbm, pbuf, sem),), step)


def _sample_index_scores(page_table, qi_s, wi_s, ki_s, cache_idx_k_t):
    nb, n_pages = page_table.shape
    page = cache_idx_k_t.shape[2]
    width = n_pages * page + LANES

    vmem = pl.BlockSpec(memory_space=pltpu.VMEM)
    return pl.pallas_call(
        _sidx_body,
        in_specs=[pl.BlockSpec(memory_space=pltpu.SMEM), vmem, vmem, vmem,
                  pl.BlockSpec(memory_space=pl.ANY)],
        out_specs=vmem,
        out_shape=jax.ShapeDtypeStruct((nb, 1, width), F32),
        scratch_shapes=[pltpu.VMEM((PAGE_GROUP, IDX_HEAD_DIM, n_pages * page), BF16),
                        pltpu.VMEM((IDX_PAGE_SLOTS, n_pages, IDX_HEAD_DIM, page), F32),
                        pltpu.SemaphoreType.DMA((IDX_PAGE_SLOTS,))],
        compiler_params=pltpu.CompilerParams(vmem_limit_bytes=VMEM_LIMIT),
        name="sample_idx",
    )(page_table, qi_s, wi_s, ki_s, cache_idx_k_t)


def _ssel_body(sc_ref, tri_ref, sel_ref):
    rows, width = sc_ref.shape
    nt = width // LANES
    n_valid = (nt - 1) * LANES + 1

    def tile(kt):
        return sc_ref[:, kt * LANES:(kt + 1) * LANES]

    def count_cmp(cmp):
        acc = jnp.zeros((rows, LANES), F32)
        for kt in range(nt):
            acc = acc + jnp.where(cmp(tile(kt)), 1.0, 0.0)
        return jnp.broadcast_to(jnp.sum(acc, axis=1, keepdims=True), (rows, LANES))

    tf = _threshold_search(lambda c: count_cmp(lambda sc: sc >= c), 32, (rows, LANES))
    need = float(TOPK_MAX) - count_cmp(lambda sc: sc > tf)
    tie_off = jnp.zeros((rows, LANES), F32)
    for kt in range(nt):
        col = kt * LANES + lax.broadcasted_iota(I32, (rows, LANES), 1)
        sc = tile(kt)
        eq = sc == tf
        tie = jnp.where(eq, 1.0, 0.0)
        rank = jnp.dot(tie.astype(BF16), tri_ref[...], preferred_element_type=F32) + tie_off
        sel = ((sc > tf) | (eq & (rank <= need))) & (col < n_valid)
        sel_ref[:, kt * LANES:(kt + 1) * LANES] = jnp.where(sel, 1.0, 0.0)
        tie_off = tie_off + jnp.broadcast_to(jnp.sum(tie, axis=1, keepdims=True), (rows, LANES))


def _sample_select(scores2d, tri):
    rows, width = scores2d.shape
    return pl.pallas_call(
        _ssel_body,
        out_shape=jax.ShapeDtypeStruct((rows, width), F32),
        compiler_params=pltpu.CompilerParams(vmem_limit_bytes=VMEM_LIMIT),
        name="sample_select",
    )(scores2d, tri)


def _sattn_body(pt_ref, q_ref, sel_ref, kn_ref, vn_ref, ck_hbm, cv_hbm, o_ref,
                kbuf, vbuf, ksem, vsem):
    n_pages = pt_ref.shape[1]
    page = kbuf.shape[2]
    past = n_pages * page
    scale = ATTN_HEAD_DIM ** -0.5

    def step(b, slot, lane_of_trip):
        q = q_ref[b]
        k_all = kbuf[slot].reshape(past, ATTN_HEAD_DIM).astype(BF16)
        v_all = vbuf[slot].reshape(past, ATTN_HEAD_DIM).astype(BF16)
        s = lax.dot_general(q, k_all, (((1,), (1,)), ((), ())), preferred_element_type=F32)
        s = jnp.where(sel_ref[b, :, 0:past] > 0.5, s * scale, NEG_BIG)
        kn = kn_ref[b].astype(BF16).astype(F32)
        sn = jnp.sum(q.astype(F32) * kn, axis=1, keepdims=True) * scale
        sn = jnp.where(sel_ref[b, :, past:past + 1] > 0.5, sn, NEG_BIG)
        m = jnp.maximum(jnp.max(s, axis=1, keepdims=True), sn)
        pn = jnp.exp(sn - m)
        p = jnp.exp(s - m)
        l = pn + jnp.sum(p, axis=1, keepdims=True)
        acc = (pn * vn_ref[b].astype(BF16).astype(F32)
               + jnp.dot(p.astype(BF16), v_all, preferred_element_type=F32))
        o_ref[b] = (acc / l).astype(BF16)

    _paged_loop(pt_ref, ((ck_hbm, kbuf, ksem), (cv_hbm, vbuf, vsem)), step)


def _sample_attention(page_table, qa_s, sel3, ka_s, va_s, cache_k, cache_v):
    nb, n_pages = page_table.shape
    page = cache_k.shape[1]

    vmem = pl.BlockSpec(memory_space=pltpu.VMEM)
    hbm = pl.BlockSpec(memory_space=pl.ANY)
    return pl.pallas_call(
        _sattn_body,
        in_specs=[pl.BlockSpec(memory_space=pltpu.SMEM), vmem, vmem, vmem, vmem, hbm, hbm],
        out_specs=vmem,
        out_shape=jax.ShapeDtypeStruct((nb, ATTN_HEADS, ATTN_HEAD_DIM), BF16),
        scratch_shapes=[pltpu.VMEM((ATTN_PAGE_SLOTS, n_pages, page, ATTN_HEAD_DIM), F32),
                        pltpu.VMEM((ATTN_PAGE_SLOTS, n_pages, page, ATTN_HEAD_DIM), F32),
                        pltpu.SemaphoreType.DMA((ATTN_PAGE_SLOTS,)),
                        pltpu.SemaphoreType.DMA((ATTN_PAGE_SLOTS,))],
        compiler_params=pltpu.CompilerParams(vmem_limit_bytes=VMEM_LIMIT),
        name="sample_attn",
    )(page_table, qa_s, sel3, ka_s, va_s, cache_k, cache_v)


def _sret_body(qkvg_ref, st_ref, gam_ref, rg_ref, so_ref):
    ns = st_ref.shape[0]
    for s in range(ns):
        blk = qkvg_ref[s].astype(F32)
        q8 = blk[0:8]
        k8 = blk[8:16]
        v8 = blk[16:24]
        g8 = blk[24:32]
        q_t = q8.T
        k_t = k8.T
        qk = jnp.sum(q8 * k8, axis=1, keepdims=True)
        rows = []
        for h in range(RET_HEADS):
            r_old = st_ref[s, h]
            gam = gam_ref[h]
            qcol = jnp.broadcast_to(q_t[:, h:h + 1], (RET_DK, RET_DV))
            kcol = jnp.broadcast_to(k_t[:, h:h + 1], (RET_DK, RET_DV))
            vrow = v8[h:h + 1]
            q_r = jnp.sum(qcol * r_old, axis=0, keepdims=True)
            rows.append(gam * q_r + qk[h:h + 1] * vrow)
            so_ref[s, h] = gam * r_old + kcol * vrow
        ret = jnp.concatenate(rows, axis=0)
        rg_ref[s] = _gate(ret, g8).astype(BF16)


def _sample_retention(qkvg, state, gam, ns):
    nb = state.shape[0]
    return pl.pallas_call(
        _sret_body,
        grid=(nb // ns,),
        in_specs=[
            pl.BlockSpec((ns, 32, LANES), lambda i: (i, 0, 0)),
            pl.BlockSpec((ns, RET_HEADS, RET_DK, RET_DV), lambda i: (i, 0, 0, 0)),
            pl.BlockSpec((RET_HEADS, 1, LANES), lambda i: (0, 0, 0)),
        ],
        out_specs=(
            pl.BlockSpec((ns, RET_HEADS, RET_DV), lambda i: (i, 0, 0)),
            pl.BlockSpec((ns, RET_HEADS, RET_DK, RET_DV), lambda i: (i, 0, 0, 0)),
        ),
        out_shape=(
            jax.ShapeDtypeStruct((nb, RET_HEADS, RET_DV), BF16),
            jax.ShapeDtypeStruct(state.shape, F32),
        ),
        compiler_params=_cparams(("arbitrary",)),
        name="sample_ret",
    )(qkvg, state, gam)


def _rotary_table(pos):
    half = RET_DK // 2
    inv = ROPE_BASE ** (-np.arange(half, dtype=np.float64) / half)
    ang = np.asarray(pos, np.float64)[:, None] * inv[None, :]
    cos = np.cos(ang)
    sin = np.sin(ang)
    return jnp.asarray(np.concatenate([cos, cos, -sin, sin], axis=1), F32)


def _retention_constants():
    lg = np.log1p(-np.exp2(-5.0 - np.arange(RET_HEADS, dtype=np.float64)))
    n = RET_CHUNK
    i = np.arange(n, dtype=np.float64)
    diff = i[:, None] - i[None, :]
    decay = np.where(diff[None] >= 0, np.exp(np.maximum(diff, 0.0)[None] * lg[:, None, None]), 0.0)
    rsc = np.exp((i + 1.0)[None, :] * lg[:, None])
    zeta = np.exp((n - 1.0 - i)[None, :] * lg[:, None])
    gpow = np.exp(n * lg)
    gam1 = np.exp(lg)
    rsc_b = np.broadcast_to(rsc[:, :, None], (RET_HEADS, n, RET_DV))
    zeta_b = np.broadcast_to(zeta[:, :, None], (RET_HEADS, n, RET_DK))
    gpow_b = np.broadcast_to(gpow[:, None, None], (RET_HEADS, 1, RET_DV))
    gam1_b = np.broadcast_to(gam1[:, None, None], (RET_HEADS, 1, LANES))
    return tuple(jnp.asarray(a, F32) for a in (decay, rsc_b, zeta_b, gpow_b, gam1_b))


def _upper_tri(n):
    return jnp.asarray(np.triu(np.ones((n, n), np.float32)), BF16)


def _lower_tri(n):
    return jnp.asarray(np.tril(np.ones((n, n), np.float32)), BF16)


def _pad_lanes(v):
    return jnp.pad(v, (0, LANES - v.shape[0])).reshape(1, LANES)


def kernel(x_prompt, x_sample, cache_k, cache_v, cache_idx_k, state_ret, page_table,
           norm1_g, w_in, idx_k_norm_g, idx_k_norm_b, w_out, norm2_g, w_up, w_down, final_norm_g):
    batch, seq, _ = x_prompt.shape
    nb = x_sample.shape[0]
    past_len = page_table.shape[1] * cache_k.shape[1]
    half_mix = ATTN_HEADS * ATTN_HEAD_DIM

    wt = w_in.T.astype(BF16)
    wa = w_out[:half_mix].astype(BF16)
    wr = w_out[half_mix:].astype(BF16)
    g1 = norm1_g.reshape(1, D_MODEL)
    g2 = norm2_g.reshape(1, D_MODEL)
    gf = final_norm_g.reshape(1, D_MODEL)
    lng = _pad_lanes(idx_k_norm_g)
    lnb = _pad_lanes(idx_k_norm_b)
    decay, rsc_b, zeta_b, gpow_b, gam1_b = _retention_constants()

    xp = x_prompt.reshape(batch * seq, D_MODEL)
    cs_p = _rotary_table(np.arange(seq))
    qa_p, qi_p, ka_p, va_p, ki_p, kd_p, wi_p = _project_attn(
        xp, g1, wt, lng, lnb, tm=TILES["proj_attn_rows"])
    main_p = _project_ret(xp, g1, wt, cs_p, tm=TILES["proj_ret_rows"])
    attn_p = _prompt_attention(qa_p, qi_p, wi_p, ka_p, va_p, kd_p, _lower_tri(KEY_TILE),
                               batch, seq)
    rg_p, ret_state_p = _prompt_retention(main_p, decay, rsc_b, zeta_b, gpow_b, batch, seq)
    x1_p, h2_p = _out_projection(attn_p, rg_p, wa, wr, xp, g2, tm=TILES["out_proj_rows"])

    xs = x_sample.reshape(nb, D_MODEL)
    cs_s = _rotary_table(np.full((nb,), past_len))
    qa_s, qi_s, ka_s, va_s, ki_s, _, wi_s = _project_attn(xs, g1, wt, lng, lnb, tm=nb)
    main_s = _project_ret(xs, g1, wt, cs_s, tm=nb)
    scores = _sample_index_scores(
        page_table,
        qi_s.transpose(1, 0, 2).reshape(nb, IDX_HEADS, IDX_HEAD_DIM),
        wi_s.T.reshape(nb, IDX_HEADS, 1),
        ki_s.reshape(nb, 1, IDX_HEAD_DIM),
        jnp.swapaxes(cache_idx_k, 1, 2))
    width = scores.shape[2]
    sel = _sample_select(scores.reshape(nb, width), _upper_tri(LANES))
    attn_s = _sample_attention(
        page_table,
        qa_s.transpose(1, 0, 2),
        sel.reshape(nb, 1, width),
        ka_s.reshape(nb, 1, ATTN_HEAD_DIM),
        va_s.reshape(nb, 1, ATTN_HEAD_DIM),
        cache_k, cache_v)
    rg_s, ret_state_s = _sample_retention(main_s.reshape(nb, 32, LANES), state_ret, gam1_b,
                                          ns=TILES["sample_ret_rows"])
    x1_s, h2_s = _out_projection(attn_s.reshape(nb, half_mix), rg_s.reshape(nb, RET_WIDTH),
                                 wa, wr, xs, g2, tm=nb)
    y_s, wu, wd = _mlp_and_cast(h2_s, w_up, w_down, x1_s, gf, tf=TILES["mlp_cast_ff"])
    y_p = _mlp(h2_p, wu, wd, x1_p, gf, tm=TILES["mlp_rows"], tf=TILES["mlp_ff"])

    return (
        y_p.reshape(batch, seq, D_MODEL),
        y_s.reshape(nb, 1, D_MODEL),
        ka_p.reshape(batch, seq, ATTN_HEAD_DIM),
        va_p.reshape(batch, seq, ATTN_HEAD_DIM),
        ki_p.reshape(batch, seq, IDX_HEAD_DIM),
        ret_state_p,
        ka_s.reshape(nb, 1, ATTN_HEAD_DIM),
        va_s.reshape(nb, 1, ATTN_HEAD_DIM),
        ki_s.reshape(nb, 1, IDX_HEAD_DIM),
        ret_state_s,
    )
```

```python
import functools

import numpy as np
import jax
import jax.numpy as jnp
from jax import lax
from jax.experimental import pallas as pl
from jax.experimental.pallas import tpu as pltpu

F32 = jnp.float32
BF16 = jnp.bfloat16
I32 = jnp.int32

D_MODEL = 2048
ATTN_HEADS = 8
ATTN_HEAD_DIM = 128
IDX_HEADS = 16
IDX_HEAD_DIM = 64
TOPK_MAX = 256
RET_HEADS = 8
RET_DK = 128
RET_DV = 128
RET_CHUNK = 256
ROPE_BASE = 10000.0
D_FF = 4 * D_MODEL
EPS = 1e-6
Q_BLOCK = 256

OFF_QA, OFF_KA, OFF_VA, OFF_QI, OFF_KI, OFF_WI = 0, 1024, 1152, 1280, 2304, 2368
OFF_QR, OFF_KR, OFF_VR, OFF_GR = 2384, 3408, 4432, 5456
RET_WIDTH = RET_HEADS * RET_DV

LANES = 128
PROJ_TILE = 512
KEY_TILE = 256
COUNT_ROWS = 32
PAGE_GROUP = 2
IDX_PAGE_SLOTS = 8
ATTN_PAGE_SLOTS = 6
SUM_ROWS = 16
LOG2_E = 1.4426950408889634
INT_MIN = -2 ** 31
KEY_NEG_INF = -2 ** 31 + 0x7FFFFF
BF16_KEY_NEG_INF = -2 ** 15 + 0x7F
NEG_BIG = -1e30
VMEM_LIMIT = 56 * 1024 * 1024

TILES = {
    "proj_attn_rows": 1024,
    "proj_ret_rows": 512,
    "out_proj_rows": 512,
    "mlp_rows": 512,
    "mlp_ff": 1024,
    "mlp_cast_ff": 1024,
    "sample_ret_rows": 8,
}


def _cparams(sem):
    return pltpu.CompilerParams(dimension_semantics=sem, vmem_limit_bytes=VMEM_LIMIT)


def _resident(shape):
    zeros = (0,) * len(shape)
    return pl.BlockSpec(shape, lambda *_: zeros, pipeline_mode=pl.Buffered(1))


def _normed_input(x_ref, g_ref, xn_ref):
    x = x_ref[...]
    ms = jnp.mean(x * x, axis=-1, keepdims=True)
    xn_ref[...] = (x * lax.rsqrt(ms + EPS) * g_ref[...]).astype(BF16)


def _matmul_rows(xn_ref, wt_ref, r0, n):
    return lax.dot_general(xn_ref[...], wt_ref[r0:r0 + n, :], (((1,), (1,)), ((), ())),
                           preferred_element_type=F32)


def _proj_attn_body(x_ref, g_ref, wt_ref, lng_ref, lnb_ref,
                    qa_ref, qi_ref, ka_ref, va_ref, ki_ref, kd_ref, wi_ref, xn_ref):
    _normed_input(x_ref, g_ref, xn_ref)
    mm = functools.partial(_matmul_rows, xn_ref, wt_ref)
    for t in range(ATTN_HEADS * ATTN_HEAD_DIM // PROJ_TILE):
        acc = mm(OFF_QA + t * PROJ_TILE, PROJ_TILE)
        for hh in range(4):
            qa_ref[4 * t + hh] = acc[:, hh * LANES:(hh + 1) * LANES].astype(BF16)
    for t in range(IDX_HEADS * IDX_HEAD_DIM // PROJ_TILE):
        acc = mm(OFF_QI + t * PROJ_TILE, PROJ_TILE)
        for hh in range(4):
            qi_ref[4 * t + hh] = acc[:, hh * LANES:(hh + 1) * LANES].astype(BF16)
    kv = mm(OFF_KA, 2 * ATTN_HEAD_DIM)
    ka_ref[...] = kv[:, :ATTN_HEAD_DIM]
    va_ref[...] = kv[:, ATTN_HEAD_DIM:]
    kw = mm(OFF_KI, LANES)
    lane = lax.broadcasted_iota(I32, kw.shape, 1)
    is_k = lane < IDX_HEAD_DIM
    mu = jnp.sum(jnp.where(is_k, kw, 0.0), axis=-1, keepdims=True) * (1.0 / IDX_HEAD_DIM)
    d = jnp.where(is_k, kw - mu, 0.0)
    var = jnp.sum(d * d, axis=-1, keepdims=True) * (1.0 / IDX_HEAD_DIM)
    kn = d * lax.rsqrt(var + EPS) * lng_ref[...] + lnb_ref[...]
    ki_ref[...] = kn[:, :IDX_HEAD_DIM]
    kd_ref[...] = jnp.where(is_k, kn, pltpu.roll(kn, IDX_HEAD_DIM, 1)).astype(BF16)
    wi_ref[...] = (kw[:, IDX_HEAD_DIM:IDX_HEAD_DIM + IDX_HEADS] * (IDX_HEADS ** -0.5)).T


def _proj_ret_body(x_ref, g_ref, wt_ref, cs_ref, main_ref, xn_ref):
    _normed_input(x_ref, g_ref, xn_ref)
    base = 0
    cosf = cs_ref[:, :LANES]
    sinf = cs_ref[:, LANES:]
    tiles = RET_WIDTH // PROJ_TILE
    for seg, (off, scale) in enumerate(((OFF_QR, None), (OFF_KR, RET_DK ** -0.5))):
        for t in range(tiles):
            acc = _matmul_rows(xn_ref, wt_ref, off - base + t * PROJ_TILE, PROJ_TILE)
            for hh in range(PROJ_TILE // LANES):
                xh = acc[:, hh * LANES:(hh + 1) * LANES]
                r = xh * cosf + pltpu.roll(xh, RET_DK // 2, 1) * sinf
                if scale is not None:
                    r = r * scale
                c0 = seg * RET_WIDTH + t * PROJ_TILE + hh * LANES
                main_ref[:, c0:c0 + LANES] = r.astype(BF16)
    for seg, off in ((2, OFF_VR), (3, OFF_GR)):
        for t in range(tiles):
            acc = _matmul_rows(xn_ref, wt_ref, off - base + t * PROJ_TILE, PROJ_TILE)
            c0 = seg * RET_WIDTH + t * PROJ_TILE
            main_ref[:, c0:c0 + PROJ_TILE] = acc.astype(BF16)


def _project_attn(x2d, g1, wt_attn, lng, lnb, tm):
    m = x2d.shape[0]
    row = lambda i: (i, 0)
    out_shape = (
        jax.ShapeDtypeStruct((ATTN_HEADS, m, ATTN_HEAD_DIM), BF16),
        jax.ShapeDtypeStruct((IDX_HEADS // 2, m, LANES), BF16),
        jax.ShapeDtypeStruct((m, ATTN_HEAD_DIM), F32),
        jax.ShapeDtypeStruct((m, ATTN_HEAD_DIM), F32),
        jax.ShapeDtypeStruct((m, IDX_HEAD_DIM), F32),
        jax.ShapeDtypeStruct((m, LANES), BF16),
        jax.ShapeDtypeStruct((IDX_HEADS, m), F32),
    )
    out_specs = (
        pl.BlockSpec((ATTN_HEADS, tm, ATTN_HEAD_DIM), lambda i: (0, i, 0)),
        pl.BlockSpec((IDX_HEADS // 2, tm, LANES), lambda i: (0, i, 0)),
        pl.BlockSpec((tm, ATTN_HEAD_DIM), row),
        pl.BlockSpec((tm, ATTN_HEAD_DIM), row),
        pl.BlockSpec((tm, IDX_HEAD_DIM), row),
        pl.BlockSpec((tm, LANES), row),
        pl.BlockSpec((IDX_HEADS, tm), lambda i: (0, i)),
    )
    return pl.pallas_call(
        _proj_attn_body,
        grid=(m // tm,),
        in_specs=[pl.BlockSpec((tm, D_MODEL), row), _resident((1, D_MODEL)),
                  _resident((OFF_QR + LANES, D_MODEL)), _resident((1, LANES)),
                  _resident((1, LANES))],
        out_specs=out_specs,
        out_shape=out_shape,
        scratch_shapes=[pltpu.VMEM((tm, D_MODEL), BF16)],
        compiler_params=_cparams(("arbitrary",)),
        name="proj_attn",
    )(x2d, g1, wt_attn, lng, lnb)


def _project_ret(x2d, g1, wt_ret, cs, tm):
    m = x2d.shape[0]
    n_pos_blocks = cs.shape[0] // tm
    row = lambda i: (i, 0)
    return pl.pallas_call(
        _proj_ret_body,
        grid=(m // tm,),
        in_specs=[pl.BlockSpec((tm, D_MODEL), row), _resident((1, D_MODEL)),
                  _resident(wt_ret.shape),
                  pl.BlockSpec((tm, 2 * LANES), lambda i: (i % n_pos_blocks, 0))],
        out_specs=pl.BlockSpec((tm, 4 * RET_WIDTH), row),
        out_shape=jax.ShapeDtypeStruct((m, 4 * RET_WIDTH), BF16),
        scratch_shapes=[pltpu.VMEM((tm, D_MODEL), BF16)],
        compiler_params=_cparams(("arbitrary",)),
        name="proj_ret",
    )(x2d, g1, wt_ret, cs)


def _key_to_float(key):
    bits = key ^ ((key >> 31) & 0x7FFFFFFF)
    return lax.bitcast_convert_type(bits, F32)


def _threshold_search(count_ge, n_iter, shape):
    def body(it, t):
        bit = lax.shift_left(jnp.int32(1), 31 - it)
        cand = t ^ bit
        cnt = count_ge(_key_to_float(cand))
        return jnp.where(cnt >= float(TOPK_MAX), cand, t)

    t = lax.fori_loop(0, n_iter, body, jnp.full(shape, INT_MIN, I32))
    return _key_to_float(jnp.maximum(t, KEY_NEG_INF))


def _bf16_key_to_f32_key(k16):
    return lax.shift_left(k16, 16) | jnp.where(k16 < 0, 0xFFFF, 0)


def _threshold_search_coarse_fine(count_ge_bf16, count_ge, run, shape):
    def coarse(it, u):
        cand = u | lax.shift_left(jnp.int32(1), 15 - it)
        c = _key_to_float(_bf16_key_to_f32_key(cand - 32768)).astype(BF16)
        return jnp.where(count_ge_bf16(c) >= float(TOPK_MAX), cand, u)

    u = lax.fori_loop(0, jnp.where(run, 16, 0), coarse, jnp.zeros(shape, I32))
    k1 = jnp.maximum(u - 32768, BF16_KEY_NEG_INF)
    lo = _bf16_key_to_f32_key(jnp.maximum(k1 - 1, -32768))
    hi = _bf16_key_to_f32_key(jnp.minimum(k1 + 1, 32767))

    def fine(it, t):
        cand = t + lax.shift_left(jnp.int32(1), 16 - it)
        ok = (cand < hi) & (count_ge(_key_to_float(cand)) >= float(TOPK_MAX))
        return jnp.where(ok, cand, t)

    t = lax.fori_loop(0, jnp.where(run, 17, 0), fine, lo)
    return _key_to_float(jnp.maximum(t, KEY_NEG_INF))


def _attn_body(qa_ref, qi_ref, wit_ref, ka_ref, va_ref, kd_ref, tri_ref, o_ref,
               kbf, vtb, scr, mrun, acc_s, kmax, scr16):
    qb = pl.program_id(1)
    n_heads_q = ATTN_HEADS * Q_BLOCK
    n_pairs = IDX_HEADS // 2
    dv = ATTN_HEAD_DIM
    logit_scale = ATTN_HEAD_DIM ** -0.5 * LOG2_E

    @pl.when(qb == 0)
    def _cast():
        ka = ka_ref[...]
        kbf[...] = ka.astype(BF16)
        kmax[...] = jnp.broadcast_to(jnp.max(jnp.sum(ka * ka, axis=1, keepdims=True)), kmax.shape)
        for kt in range(vtb.shape[0]):
            vtb[kt, :dv] = va_ref[kt * KEY_TILE:(kt + 1) * KEY_TILE, :].T.astype(BF16)
            vtb[kt, dv:] = jnp.ones((vtb.shape[1] - dv, KEY_TILE), BF16)

    nk = ((qb + 1) * Q_BLOCK + KEY_TILE - 1) // KEY_TILE
    wt = wit_ref[...] * (IDX_HEAD_DIM ** -0.5)
    qi2 = qi_ref[...].reshape(n_pairs * Q_BLOCK, LANES)
    lo_half = lax.broadcasted_iota(I32, (KEY_TILE, LANES), 1) < IDX_HEAD_DIM
    qidx = qb * Q_BLOCK + lax.broadcasted_iota(I32, (KEY_TILE, Q_BLOCK), 1)
    kidx0 = lax.broadcasted_iota(I32, (KEY_TILE, Q_BLOCK), 0)
    contract_last = (((1,), (1,)), ((), ()))

    def idx_body(kt, carry):
        off = pl.multiple_of(kt * KEY_TILE, KEY_TILE)
        kit = kd_ref[pl.ds(off, KEY_TILE), :]
        zero = jnp.zeros_like(kit)
        s_even = lax.dot_general(jnp.where(lo_half, kit, zero), qi2, contract_last,
                                 preferred_element_type=F32)
        s_odd = lax.dot_general(jnp.where(lo_half, zero, kit), qi2, contract_last,
                                preferred_element_type=F32)
        score = jnp.zeros((KEY_TILE, Q_BLOCK), F32)
        for g in range(n_pairs):
            cs = slice(g * Q_BLOCK, (g + 1) * Q_BLOCK)
            score = score + jnp.maximum(s_even[:, cs], 0.0) * wt[2 * g:2 * g + 1, :]
            score = score + jnp.maximum(s_odd[:, cs], 0.0) * wt[2 * g + 1:2 * g + 2, :]
        score = jnp.where(kidx0 + off <= qidx, score, -jnp.inf)
        scr[kt] = score
        scr16[kt] = score.astype(BF16)
        return carry

    def for_tiles(fn):
        def pair(j, carry):
            fn(2 * j, 0)
            fn(2 * j + 1, 0)
            return carry

        lax.fori_loop(0, nk // 2, pair, 0)

        @pl.when(nk % 2 == 1)
        def _last():
            fn(nk - 1, 0)

    for_tiles(idx_body)

    def count_ge_bf16(c):
        def body(kt, acc):
            hit = scr16[kt] >= c
            for r in range(KEY_TILE // COUNT_ROWS):
                acc = jnp.where(hit[r * COUNT_ROWS:(r + 1) * COUNT_ROWS], acc + 1.0, acc)
            return acc
        acc = lax.fori_loop(0, nk, body, jnp.zeros((COUNT_ROWS, Q_BLOCK), BF16))
        return jnp.sum(acc.astype(F32), axis=0, keepdims=True)

    def count_cmp(cmp):
        def body(kt, acc):
            hit = cmp(scr[kt])
            for r in range(KEY_TILE // COUNT_ROWS):
                acc = jnp.where(hit[r * COUNT_ROWS:(r + 1) * COUNT_ROWS], acc + 1.0, acc)
            return acc
        acc = lax.fori_loop(0, nk, body, jnp.zeros((COUNT_ROWS, Q_BLOCK), F32))
        return jnp.sum(acc, axis=0, keepdims=True)

    tf = _threshold_search_coarse_fine(
        count_ge_bf16, lambda c: count_cmp(lambda sc: sc >= c),
        qb >= TOPK_MAX // Q_BLOCK, (1, Q_BLOCK))
    excess = jnp.max(count_cmp(lambda sc: sc >= tf)) > float(TOPK_MAX)

    qa2 = qa_ref[...].reshape(n_heads_q, ATTN_HEAD_DIM)

    def logits(kt):
        off = pl.multiple_of(kt * KEY_TILE, KEY_TILE)
        s = lax.dot_general(kbf[pl.ds(off, KEY_TILE), :], qa2, contract_last,
                            preferred_element_type=F32)
        return s * logit_scale

    def sel_plain(kt, carry):
        off = pl.multiple_of(kt * KEY_TILE, KEY_TILE)
        return (scr[kt] >= tf) & (kidx0 + off <= qidx), carry

    def sel_ties(need, kt, tie_off):
        off = pl.multiple_of(kt * KEY_TILE, KEY_TILE)
        sc = scr[kt]
        eq = sc == tf
        tie = jnp.where(eq, 1.0, 0.0)
        rank = jnp.dot(tri_ref[...], tie.astype(BF16), preferred_element_type=F32) + tie_off
        sel = ((sc > tf) | (eq & (rank <= need))) & (kidx0 + off <= qidx)
        return sel, tie_off + jnp.sum(tie, axis=0, keepdims=True)

    no_ties = jnp.zeros((1, Q_BLOCK), F32)

    def softmax_sum(m, sel_fn):
        acc_s[...] = jnp.zeros(acc_s.shape, F32)

        def body(kt, carry):
            sel, carry = sel_fn(kt, carry)
            e = jnp.exp2(logits(kt) - m)
            parts = []
            for h in range(ATTN_HEADS):
                cs = slice(h * Q_BLOCK, (h + 1) * Q_BLOCK)
                parts.append(jnp.where(sel, e[:, cs], 0.0).astype(BF16))
            p = jnp.concatenate(parts, axis=1)
            acc_s[...] += jnp.dot(vtb[kt], p, preferred_element_type=F32)
            return carry

        if sel_fn is sel_plain:
            for_tiles(body)
        else:
            lax.fori_loop(0, nk, body, no_ties)

    def selected_max(sel_fn):
        mrun[...] = jnp.full(mrun.shape, NEG_BIG, F32)

        def body(kt, carry):
            sel, carry = sel_fn(kt, carry)
            s = logits(kt)
            for h in range(ATTN_HEADS):
                cs = slice(h * Q_BLOCK, (h + 1) * Q_BLOCK)
                sh = jnp.where(sel, s[:, cs], NEG_BIG)
                mrun[:, cs] = jnp.maximum(
                    mrun[:, cs], jnp.max(sh.reshape(KEY_TILE // 8, 8, Q_BLOCK), axis=0))
            return carry

        lax.fori_loop(0, nk, body, no_ties)
        return jnp.max(mrun[...], axis=0, keepdims=True)

    def fast_path():
        q2 = (qa2 * qa2).astype(BF16)
        qsq = lax.dot_general(jnp.ones((8, ATTN_HEAD_DIM), BF16), q2, contract_last,
                              preferred_element_type=F32)[0:1]
        softmax_sum(jnp.sqrt(qsq * kmax[0:1, 0:1]) * logit_scale, sel_plain)
        return (jnp.min(acc_s[dv:dv + 1, :]) > 0.0).astype(I32)

    done = lax.cond(excess, lambda: jnp.int32(0), fast_path) == 1

    @pl.when(jnp.logical_not(done) & excess)
    def _exact_with_ties():
        need = float(TOPK_MAX) - count_cmp(lambda sc: sc > tf)
        sel_fn = functools.partial(sel_ties, need)
        softmax_sum(selected_max(sel_fn), sel_fn)

    @pl.when(jnp.logical_not(done) & jnp.logical_not(excess))
    def _exact_without_ties():
        softmax_sum(selected_max(sel_plain), sel_plain)

    out = acc_s[:dv, :] / acc_s[dv:dv + 1, :]
    for h in range(ATTN_HEADS):
        oh = out[:, h * Q_BLOCK:(h + 1) * Q_BLOCK].T
        o_ref[:, h * ATTN_HEAD_DIM:(h + 1) * ATTN_HEAD_DIM] = oh.astype(BF16)


def _prompt_attention(qa_hm, qi_pm, wi_t, ka, va, kd, tri, batch, seq):
    nq = seq // Q_BLOCK
    nkt = seq // KEY_TILE
    m = batch * seq
    n_heads_q = ATTN_HEADS * Q_BLOCK
    assert seq // COUNT_ROWS <= 256
    in_specs = [
        pl.BlockSpec((ATTN_HEADS, Q_BLOCK, ATTN_HEAD_DIM), lambda b, q: (0, b * nq + q, 0)),
        pl.BlockSpec((IDX_HEADS // 2, Q_BLOCK, LANES), lambda b, q: (0, b * nq + q, 0)),
        pl.BlockSpec((IDX_HEADS, Q_BLOCK), lambda b, q: (0, b * nq + q)),
        pl.BlockSpec((seq, ATTN_HEAD_DIM), lambda b, q: (b, 0)),
        pl.BlockSpec((seq, ATTN_HEAD_DIM), lambda b, q: (b, 0)),
        pl.BlockSpec((seq, LANES), lambda b, q: (b, 0)),
        pl.BlockSpec((KEY_TILE, KEY_TILE), lambda b, q: (0, 0)),
    ]
    return pl.pallas_call(
        _attn_body,
        grid=(batch, nq),
        in_specs=in_specs,
        out_specs=pl.BlockSpec((Q_BLOCK, ATTN_HEADS * ATTN_HEAD_DIM), lambda b, q: (b * nq + q, 0)),
        out_shape=jax.ShapeDtypeStruct((m, ATTN_HEADS * ATTN_HEAD_DIM), BF16),
        scratch_shapes=[
            pltpu.VMEM((seq, ATTN_HEAD_DIM), BF16),
            pltpu.VMEM((nkt, ATTN_HEAD_DIM + SUM_ROWS, KEY_TILE), BF16),
            pltpu.VMEM((nkt, KEY_TILE, Q_BLOCK), F32),
            pltpu.VMEM((8, n_heads_q), F32),
            pltpu.VMEM((ATTN_HEAD_DIM + SUM_ROWS, n_heads_q), F32),
            pltpu.VMEM((8, LANES), F32),
            pltpu.VMEM((nkt, KEY_TILE, Q_BLOCK), BF16),
        ],
        compiler_params=_cparams(("arbitrary", "arbitrary")),
        name="prompt_attn",
    )(qa_hm, qi_pm, wi_t, ka, va, kd, tri)


def _gate(o, g):
    rn = o * lax.rsqrt(jnp.mean(o * o, axis=-1, keepdims=True) + EPS)
    return rn * (g / (1.0 + jnp.exp(-g)))


def _ret_body(q_ref, k_ref, v_ref, g_ref, decay_ref, rsc_ref, zeta_ref, gpow_ref,
              rg_ref, st_ref):
    c = pl.program_id(1)

    @pl.when(c == 0)
    def _init():
        st_ref[...] = jnp.zeros(st_ref.shape, F32)

    for h in range(RET_HEADS):
        sl = slice(h * 128, (h + 1) * 128)
        q = q_ref[:, sl]
        k = k_ref[:, sl]
        v = v_ref[:, sl]
        r_old = st_ref[0, h]
        qk = lax.dot_general(q, k, (((1,), (1,)), ((), ())), preferred_element_type=F32)
        inner = jnp.dot((qk * decay_ref[h]).astype(BF16), v, preferred_element_type=F32)
        cross = jnp.dot(q, r_old.astype(BF16), preferred_element_type=F32) * rsc_ref[h]
        kz = (k.astype(F32) * zeta_ref[h]).astype(BF16)
        upd = lax.dot_general(kz, v, (((0,), (0,)), ((), ())), preferred_element_type=F32)
        st_ref[0, h] = r_old * gpow_ref[h] + upd
        rg_ref[:, sl] = _gate(inner + cross, g_ref[:, sl].astype(F32)).astype(BF16)


def _prompt_retention(main, decay, rsc, zeta, gpow, batch, seq):
    nc = seq // RET_CHUNK
    m = batch * seq
    width = RET_WIDTH
    const3 = lambda b, c: (0, 0, 0)
    in_specs = [
        pl.BlockSpec((RET_CHUNK, width), lambda b, c: (b * nc + c, 0)),
        pl.BlockSpec((RET_CHUNK, width), lambda b, c: (b * nc + c, 1)),
        pl.BlockSpec((RET_CHUNK, width), lambda b, c: (b * nc + c, 2)),
        pl.BlockSpec((RET_CHUNK, width), lambda b, c: (b * nc + c, 3)),
        pl.BlockSpec((RET_HEADS, RET_CHUNK, RET_CHUNK), const3),
        pl.BlockSpec((RET_HEADS, RET_CHUNK, RET_DV), const3),
        pl.BlockSpec((RET_HEADS, RET_CHUNK, RET_DK), const3),
        pl.BlockSpec((RET_HEADS, 1, RET_DV), const3),
    ]
    return pl.pallas_call(
        _ret_body,
        grid=(batch, nc),
        in_specs=in_specs,
        out_specs=(
            pl.BlockSpec((RET_CHUNK, width), lambda b, c: (b * nc + c, 0)),
            pl.BlockSpec((1, RET_HEADS, RET_DK, RET_DV), lambda b, c: (b, 0, 0, 0)),
        ),
        out_shape=(
            jax.ShapeDtypeStruct((m, width), BF16),
            jax.ShapeDtypeStruct((batch, RET_HEADS, RET_DK, RET_DV), F32),
        ),
        compiler_params=_cparams(("arbitrary", "arbitrary")),
        name="prompt_ret",
    )(main, main, main, main, decay, rsc, zeta, gpow)


def _outproj_body(a_ref, r_ref, wa_ref, wr_ref, x_ref, g2_ref, x1_ref, h2_ref):
    mixed = (jnp.dot(a_ref[...], wa_ref[...], preferred_element_type=F32)
             + jnp.dot(r_ref[...], wr_ref[...], preferred_element_type=F32))
    x1 = x_ref[...] + mixed
    x1_ref[...] = x1
    ms = jnp.mean(x1 * x1, axis=-1, keepdims=True)
    h2_ref[...] = (x1 * lax.rsqrt(ms + EPS) * g2_ref[...]).astype(BF16)


def _out_projection(attn_o, rg, wa, wr, x2d, g2, tm):
    m = x2d.shape[0]
    half = attn_o.shape[1]
    in_specs = [
        pl.BlockSpec((tm, half), lambda i: (i, 0)),
        pl.BlockSpec((tm, half), lambda i: (i, 0)),
        pl.BlockSpec((half, D_MODEL), lambda i: (0, 0)),
        pl.BlockSpec((half, D_MODEL), lambda i: (0, 0)),
        pl.BlockSpec((tm, D_MODEL), lambda i: (i, 0)),
        pl.BlockSpec((1, D_MODEL), lambda i: (0, 0)),
    ]
    return pl.pallas_call(
        _outproj_body,
        grid=(m // tm,),
        in_specs=in_specs,
        out_specs=(pl.BlockSpec((tm, D_MODEL), lambda i: (i, 0)),
                   pl.BlockSpec((tm, D_MODEL), lambda i: (i, 0))),
        out_shape=(jax.ShapeDtypeStruct((m, D_MODEL), F32),
                   jax.ShapeDtypeStruct((m, D_MODEL), BF16)),
        compiler_params=_cparams(("arbitrary",)),
        name="out_proj",
    )(attn_o, rg, wa, wr, x2d, g2)


def _mlp_body(h2_ref, wu_ref, wd_ref, x1_ref, gf_ref, y_ref, acc_ref):
    f = pl.program_id(1)

    @pl.when(f == 0)
    def _init():
        acc_ref[...] = x1_ref[...]

    u = jnp.dot(h2_ref[...], wu_ref[...], preferred_element_type=F32)
    a = jnp.maximum(u, 0.0)
    acc_ref[...] += jnp.dot((a * a).astype(BF16), wd_ref[...], preferred_element_type=F32)

    @pl.when(f == pl.num_programs(1) - 1)
    def _final():
        x2 = acc_ref[...]
        ms = jnp.mean(x2 * x2, axis=-1, keepdims=True)
        y_ref[...] = x2 * lax.rsqrt(ms + EPS) * gf_ref[...]


def _mlp_cast_body(h2_ref, wu_ref, wd_ref, x1_ref, gf_ref, y_ref, wub_ref, wdb_ref, acc_ref):
    f = pl.program_id(0)

    @pl.when(f == 0)
    def _init():
        acc_ref[...] = x1_ref[...]

    wu = wu_ref[...].astype(BF16)
    wd = wd_ref[...].astype(BF16)
    wub_ref[...] = wu
    wdb_ref[...] = wd
    u = jnp.dot(h2_ref[...], wu, preferred_element_type=F32)
    a = jnp.maximum(u, 0.0)
    acc_ref[...] += jnp.dot((a * a).astype(BF16), wd, preferred_element_type=F32)

    @pl.when(f == pl.num_programs(0) - 1)
    def _final():
        x2 = acc_ref[...]
        ms = jnp.mean(x2 * x2, axis=-1, keepdims=True)
        y_ref[...] = x2 * lax.rsqrt(ms + EPS) * gf_ref[...]


def _mlp_and_cast(h2, w_up, w_down, x1, gf, tf):
    m = h2.shape[0]
    full = lambda f: (0, 0)
    return pl.pallas_call(
        _mlp_cast_body,
        grid=(D_FF // tf,),
        in_specs=[
            pl.BlockSpec((m, D_MODEL), full),
            pl.BlockSpec((D_MODEL, tf), lambda f: (0, f)),
            pl.BlockSpec((tf, D_MODEL), lambda f: (f, 0)),
            pl.BlockSpec((m, D_MODEL), full),
            pl.BlockSpec((1, D_MODEL), full),
        ],
        out_specs=(
            pl.BlockSpec((m, D_MODEL), full),
            pl.BlockSpec((D_MODEL, tf), lambda f: (0, f)),
            pl.BlockSpec((tf, D_MODEL), lambda f: (f, 0)),
        ),
        out_shape=(
            jax.ShapeDtypeStruct((m, D_MODEL), F32),
            jax.ShapeDtypeStruct((D_MODEL, D_FF), BF16),
            jax.ShapeDtypeStruct((D_FF, D_MODEL), BF16),
        ),
        scratch_shapes=[pltpu.VMEM((m, D_MODEL), F32)],
        compiler_params=_cparams(("arbitrary",)),
        name="mlp_cast",
    )(h2, w_up, w_down, x1, gf)


def _mlp(h2, wu, wd, x1, gf, tm, tf):
    m = h2.shape[0]
    in_specs = [
        pl.BlockSpec((tm, D_MODEL), lambda i, f: (i, 0)),
        pl.BlockSpec((D_MODEL, tf), lambda i, f: (0, f)),
        pl.BlockSpec((tf, D_MODEL), lambda i, f: (f, 0)),
        pl.BlockSpec((tm, D_MODEL), lambda i, f: (i, 0)),
        pl.BlockSpec((1, D_MODEL), lambda i, f: (0, 0)),
    ]
    return pl.pallas_call(
        _mlp_body,
        grid=(m // tm, D_FF // tf),
        in_specs=in_specs,
        out_specs=pl.BlockSpec((tm, D_MODEL), lambda i, f: (i, 0)),
        out_shape=jax.ShapeDtypeStruct((m, D_MODEL), F32),
        scratch_shapes=[pltpu.VMEM((tm, D_MODEL), F32)],
        compiler_params=_cparams(("arbitrary", "arbitrary")),
        name="mlp",
    )(h2, wu, wd, x1, gf)


def _fetch_pages(pt_ref, step, slot, streams, start):
    n_pages = pt_ref.shape[1]
    for hbm, buf, sem in streams:
        for j in range(n_pages):
            cp = pltpu.make_async_copy(hbm.at[pt_ref[step, j]], buf.at[slot, j], sem.at[slot])
            if start:
                cp.start()
            else:
                cp.wait()


def _paged_loop(pt_ref, streams, step_fn):
    nb = pt_ref.shape[0]
    n_slots = streams[0][1].shape[0]
    ahead = n_slots - PAGE_GROUP
    assert nb % PAGE_GROUP == 0 and ahead % PAGE_GROUP == 0 and PAGE_GROUP <= ahead <= nb
    for s in range(ahead):
        _fetch_pages(pt_ref, s, s, streams, start=True)

    def body(g, carry):
        b0 = g * PAGE_GROUP

        @pl.when(b0 + ahead < nb)
        def _next():
            for i in range(PAGE_GROUP):
                row = b0 + ahead + i
                _fetch_pages(pt_ref, row, row % n_slots, streams, start=True)

        for i in range(PAGE_GROUP):
            _fetch_pages(pt_ref, b0 + i, (b0 + i) % n_slots, streams, start=False)
        for i in range(PAGE_GROUP):
            step_fn(b0 + i, (b0 + i) % n_slots, i)
        return carry

    lax.fori_loop(0, nb // PAGE_GROUP, body, 0)


def _sidx_body(pt_ref, qi_ref, w_ref, kin_ref, cache_hbm, out_ref, kt_s, pbuf, sem):
    n_pages = pt_ref.shape[1]
    page = pbuf.shape[3]
    past = n_pages * page
    lane = lax.broadcasted_iota(I32, (1, LANES), 1)

    def step(b, slot, lane_of_trip):
        qi = qi_ref[b]
        w = w_ref[b] * (IDX_HEAD_DIM ** -0.5)
        for j in range(n_pages):
            kt_s[lane_of_trip, :, j * page:(j + 1) * page] = pbuf[slot, j].astype(BF16)
        s = jnp.dot(qi, kt_s[lane_of_trip], preferred_element_type=F32)
        out_ref[b, :, 0:past] = jnp.sum(jnp.maximum(s, 0.0) * w, axis=0, keepdims=True)
        sn = jnp.sum(qi.astype(F32) * kin_ref[b].astype(BF16).astype(F32), axis=1, keepdims=True)
        rn = jnp.sum(jnp.maximum(sn, 0.0) * w, axis=0, keepdims=True)
        out_ref[b, :, past:past + LANES] = jnp.where(lane == 0, rn, -jnp.inf)

    _paged_loop(pt_ref, ((cache_hbm, pbuf, sem),), step)


def _sample_index_scores(page_table, qi_s, wi_s, ki_s, cache_idx_k_t):
    nb, n_pages = page_table.shape
    page = cache_idx_k_t.shape[2]
    width = n_pages * page + LANES

    vmem = pl.BlockSpec(memory_space=pltpu.VMEM)
    return pl.pallas_call(
        _sidx_body,
        in_specs=[pl.BlockSpec(memory_space=pltpu.SMEM), vmem, vmem, vmem,
                  pl.BlockSpec(memory_space=pl.ANY)],
        out_specs=vmem,
        out_shape=jax.ShapeDtypeStruct((nb, 1, width), F32),
        scratch_shapes=[pltpu.VMEM((PAGE_GROUP, IDX_HEAD_DIM, n_pages * page), BF16),
                        pltpu.VMEM((IDX_PAGE_SLOTS, n_pages, IDX_HEAD_DIM, page), F32),
                        pltpu.SemaphoreType.DMA((IDX_PAGE_SLOTS,))],
        compiler_params=pltpu.CompilerParams(vmem_limit_bytes=VMEM_LIMIT),
        name="sample_idx",
    )(page_table, qi_s, wi_s, ki_s, cache_idx_k_t)


def _ssel_body(sc_ref, tri_ref, sel_ref):
    rows, width = sc_ref.shape
    nt = width // LANES
    n_valid = (nt - 1) * LANES + 1

    def tile(kt):
        return sc_ref[:, kt * LANES:(kt + 1) * LANES]

    def count_cmp(cmp):
        acc = jnp.zeros((rows, LANES), F32)
        for kt in range(nt):
            acc = acc + jnp.where(cmp(tile(kt)), 1.0, 0.0)
        return jnp.broadcast_to(jnp.sum(acc, axis=1, keepdims=True), (rows, LANES))

    tf = _threshold_search(lambda c: count_cmp(lambda sc: sc >= c), 32, (rows, LANES))
    need = float(TOPK_MAX) - count_cmp(lambda sc: sc > tf)
    tie_off = jnp.zeros((rows, LANES), F32)
    for kt in range(nt):
        col = kt * LANES + lax.broadcasted_iota(I32, (rows, LANES), 1)
        sc = tile(kt)
        eq = sc == tf
        tie = jnp.where(eq, 1.0, 0.0)
        rank = jnp.dot(tie.astype(BF16), tri_ref[...], preferred_element_type=F32) + tie_off
        sel = ((sc > tf) | (eq & (rank <= need))) & (col < n_valid)
        sel_ref[:, kt * LANES:(kt + 1) * LANES] = jnp.where(sel, 1.0, 0.0)
        tie_off = tie_off + jnp.broadcast_to(jnp.sum(tie, axis=1, keepdims=True), (rows, LANES))


def _sample_select(scores2d, tri):
    rows, width = scores2d.shape
    return pl.pallas_call(
        _ssel_body,
        out_shape=jax.ShapeDtypeStruct((rows, width), F32),
        compiler_params=pltpu.CompilerParams(vmem_limit_bytes=VMEM_LIMIT),
        name="sample_select",
    )(scores2d, tri)


def _sattn_body(pt_ref, q_ref, sel_ref, kn_ref, vn_ref, ck_hbm, cv_hbm, o_ref,
                kbuf, vbuf, ksem, vsem):
    n_pages = pt_ref.shape[1]
    page = kbuf.shape[2]
    past = n_pages * page
    scale = ATTN_HEAD_DIM ** -0.5

    def step(b, slot, lane_of_trip):
        q = q_ref[b]
        k_all = kbuf[slot].reshape(past, ATTN_HEAD_DIM).astype(BF16)
        v_all = vbuf[slot].reshape(past, ATTN_HEAD_DIM).astype(BF16)
        s = lax.dot_general(q, k_all, (((1,), (1,)), ((), ())), preferred_element_type=F32)
        s = jnp.where(sel_ref[b, :, 0:past] > 0.5, s * scale, NEG_BIG)
        kn = kn_ref[b].astype(BF16).astype(F32)
        sn = jnp.sum(q.astype(F32) * kn, axis=1, keepdims=True) * scale
        sn = jnp.where(sel_ref[b, :, past:past + 1] > 0.5, sn, NEG_BIG)
        m = jnp.maximum(jnp.max(s, axis=1, keepdims=True), sn)
        pn = jnp.exp(sn - m)
        p = jnp.exp(s - m)
        l = pn + jnp.sum(p, axis=1, keepdims=True)
        acc = (pn * vn_ref[b].astype(BF16).astype(F32)
               + jnp.dot(p.astype(BF16), v_all, preferred_element_type=F32))
        o_ref[b] = (acc / l).astype(BF16)

    _paged_loop(pt_ref, ((ck_hbm, kbuf, ksem), (cv_hbm, vbuf, vsem)), step)


def _sample_attention(page_table, qa_s, sel3, ka_s, va_s, cache_k, cache_v):
    nb, n_pages = page_table.shape
    page = cache_k.shape[1]

    vmem = pl.BlockSpec(memory_space=pltpu.VMEM)
    hbm = pl.BlockSpec(memory_space=pl.ANY)
    return pl.pallas_call(
        _sattn_body,
        in_specs=[pl.BlockSpec(memory_space=pltpu.SMEM), vmem, vmem, vmem, vmem, hbm, hbm],
        out_specs=vmem,
        out_shape=jax.ShapeDtypeStruct((nb, ATTN_HEADS, ATTN_HEAD_DIM), BF16),
        scratch_shapes=[pltpu.VMEM((ATTN_PAGE_SLOTS, n_pages, page, ATTN_HEAD_DIM), F32),
                        pltpu.VMEM((ATTN_PAGE_SLOTS, n_pages, page, ATTN_HEAD_DIM), F32),
                        pltpu.SemaphoreType.DMA((ATTN_PAGE_SLOTS,)),
                        pltpu.SemaphoreType.DMA((ATTN_PAGE_SLOTS,))],
        compiler_params=pltpu.CompilerParams(vmem_limit_bytes=VMEM_LIMIT),
        name="sample_attn",
    )(page_table, qa_s, sel3, ka_s, va_s, cache_k, cache_v)


def _sret_body(qkvg_ref, st_ref, gam_ref, rg_ref, so_ref):
    ns = st_ref.shape[0]
    for s in range(ns):
        blk = qkvg_ref[s].astype(F32)
        q8 = blk[0:8]
        k8 = blk[8:16]
        v8 = blk[16:24]
        g8 = blk[24:32]
        q_t = q8.T
        k_t = k8.T
        qk = jnp.sum(q8 * k8, axis=1, keepdims=True)
        rows = []
        for h in range(RET_HEADS):
            r_old = st_ref[s, h]
            gam = gam_ref[h]
            qcol = jnp.broadcast_to(q_t[:, h:h + 1], (RET_DK, RET_DV))
            kcol = jnp.broadcast_to(k_t[:, h:h + 1], (RET_DK, RET_DV))
            vrow = v8[h:h + 1]
            q_r = jnp.sum(qcol * r_old, axis=0, keepdims=True)
            rows.append(gam * q_r + qk[h:h + 1] * vrow)
            so_ref[s, h] = gam * r_old + kcol * vrow
        ret = jnp.concatenate(rows, axis=0)
        rg_ref[s] = _gate(ret, g8).astype(BF16)


def _sample_retention(qkvg, state, gam, ns):
    nb = state.shape[0]
    return pl.pallas_call(
        _sret_body,
        grid=(nb // ns,),
        in_specs=[
            pl.BlockSpec((ns, 32, LANES), lambda i: (i, 0, 0)),
            pl.BlockSpec((ns, RET_HEADS, RET_DK, RET_DV), lambda i: (i, 0, 0, 0)),
            pl.BlockSpec((RET_HEADS, 1, LANES), lambda i: (0, 0, 0)),
        ],
        out_specs=(
            pl.BlockSpec((ns, RET_HEADS, RET_DV), lambda i: (i, 0, 0)),
            pl.BlockSpec((ns, RET_HEADS, RET_DK, RET_DV), lambda i: (i, 0, 0, 0)),
        ),
        out_shape=(
            jax.ShapeDtypeStruct((nb, RET_HEADS, RET_DV), BF16),
            jax.ShapeDtypeStruct(state.shape, F32),
        ),
        compiler_params=_cparams(("arbitrary",)),
        name="sample_ret",
    )(qkvg, state, gam)


def _rotary_table(pos):
    half = RET_DK // 2
    inv = ROPE_BASE ** (-np.arange(half, dtype=np.float64) / half)
    ang = np.asarray(pos, np.float64)[:, None] * inv[None, :]
    cos = np.cos(ang)
    sin = np.sin(ang)
    return jnp.asarray(np.concatenate([cos, cos, -sin, sin], axis=1), F32)


def _retention_constants():
    lg = np.log1p(-np.exp2(-5.0 - np.arange(RET_HEADS, dtype=np.float64)))
    n = RET_CHUNK
    i = np.arange(n, dtype=np.float64)
    diff = i[:, None] - i[None, :]
    decay = np.where(diff[None] >= 0, np.exp(np.maximum(diff, 0.0)[None] * lg[:, None, None]), 0.0)
    rsc = np.exp((i + 1.0)[None, :] * lg[:, None])
    zeta = np.exp((n - 1.0 - i)[None, :] * lg[:, None])
    gpow = np.exp(n * lg)
    gam1 = np.exp(lg)
    rsc_b = np.broadcast_to(rsc[:, :, None], (RET_HEADS, n, RET_DV))
    zeta_b = np.broadcast_to(zeta[:, :, None], (RET_HEADS, n, RET_DK))
    gpow_b = np.broadcast_to(gpow[:, None, None], (RET_HEADS, 1, RET_DV))
    gam1_b = np.broadcast_to(gam1[:, None, None], (RET_HEADS, 1, LANES))
    return tuple(jnp.asarray(a, F32) for a in (decay, rsc_b, zeta_b, gpow_b, gam1_b))


def _upper_tri(n):
    return jnp.asarray(np.triu(np.ones((n, n), np.float32)), BF16)


def _lower_tri(n):
    return jnp.asarray(np.tril(np.ones((n, n), np.float32)), BF16)


def _pad_lanes(v):
    return jnp.pad(v, (0, LANES - v.shape[0])).reshape(1, LANES)


def kernel(x_prompt, x_sample, cache_k, cache_v, cache_idx_k, state_ret, page_table,
           norm1_g, w_in, idx_k_norm_g, idx_k_norm_b, w_out, norm2_g, w_up, w_down, final_norm_g):
    batch, seq, _ = x_prompt.shape
    nb = x_sample.shape[0]
    past_len = page_table.shape[1] * cache_k.shape[1]
    half_mix = ATTN_HEADS * ATTN_HEAD_DIM

    wt = w_in.T.astype(BF16)
    wa = w_out[:half_mix].astype(BF16)
    wr = w_out[half_mix:].astype(BF16)
    g1 = norm1_g.reshape(1, D_MODEL)
    g2 = norm2_g.reshape(1, D_MODEL)
    gf = final_norm_g.reshape(1, D_MODEL)
    lng = _pad_lanes(idx_k_norm_g)
    lnb = _pad_lanes(idx_k_norm_b)
    decay, rsc_b, zeta_b, gpow_b, gam1_b = _retention_constants()

    xp = x_prompt.reshape(batch * seq, D_MODEL)
    cs_p = _rotary_table(np.arange(seq))
    qa_p, qi_p, ka_p, va_p, ki_p, kd_p, wi_p = _project_attn(
        xp, g1, wt, lng, lnb, tm=TILES["proj_attn_rows"])
    main_p = _project_ret(xp, g1, wt, cs_p, tm=TILES["proj_ret_rows"])
    attn_p = _prompt_attention(qa_p, qi_p, wi_p, ka_p, va_p, kd_p, _lower_tri(KEY_TILE),
                               batch, seq)
    rg_p, ret_state_p = _prompt_retention(main_p, decay, rsc_b, zeta_b, gpow_b, batch, seq)
    x1_p, h2_p = _out_projection(attn_p, rg_p, wa, wr, xp, g2, tm=TILES["out_proj_rows"])

    xs = x_sample.reshape(nb, D_MODEL)
    cs_s = _rotary_table(np.full((nb,), past_len))
    qa_s, qi_s, ka_s, va_s, ki_s, _, wi_s = _project_attn(xs, g1, wt, lng, lnb, tm=nb)
    main_s = _project_ret(xs, g1, wt, cs_s, tm=nb)
    scores = _sample_index_scores(
        page_table,
        qi_s.transpose(1, 0, 2).reshape(nb, IDX_HEADS, IDX_HEAD_DIM),
        wi_s.T.reshape(nb, IDX_HEADS, 1),
        ki_s.reshape(nb, 1, IDX_HEAD_DIM),
        jnp.swapaxes(cache_idx_k, 1, 2))
    width = scores.shape[2]
    sel = _sample_select(scores.reshape(nb, width), _upper_tri(LANES))
    attn_s = _sample_attention(
        page_table,
        qa_s.transpose(1, 0, 2),
        sel.reshape(nb, 1, width),
        ka_s.reshape(nb, 1, ATTN_HEAD_DIM),
        va_s.reshape(nb, 1, ATTN_HEAD_DIM),
        cache_k, cache_v)
    rg_s, ret_state_s = _sample_retention(main_s.reshape(nb, 32, LANES), state_ret, gam1_b,
                                          ns=TILES["sample_ret_rows"])
    x1_s, h2_s = _out_projection(attn_s.reshape(nb, half_mix), rg_s.reshape(nb, RET_WIDTH),
                                 wa, wr, xs, g2, tm=nb)
    y_s, wu, wd = _mlp_and_cast(h2_s, w_up, w_down, x1_s, gf, tf=TILES["mlp_cast_ff"])
    y_p = _mlp(h2_p, wu, wd, x1_p, gf, tm=TILES["mlp_rows"], tf=TILES["mlp_ff"])

    return (
        y_p.reshape(batch, seq, D_MODEL),
        y_s.reshape(nb, 1, D_MODEL),
        ka_p.reshape(batch, seq, ATTN_HEAD_DIM),
        va_p.reshape(batch, seq, ATTN_HEAD_DIM),
        ki_p.reshape(batch, seq, IDX_HEAD_DIM),
        ret_state_p,
        ka_s.reshape(nb, 1, ATTN_HEAD_DIM),
        va_s.reshape(nb, 1, ATTN_HEAD_DIM),
        ki_s.reshape(nb, 1, IDX_HEAD_DIM),
        ret_state_s,
    )
```

```python
import functools

import numpy as np
import jax
import jax.numpy as jnp
from jax import lax
from jax.experimental import pallas as pl
from jax.experimental.pallas import tpu as pltpu

F32 = jnp.float32
BF16 = jnp.bfloat16
I32 = jnp.int32

D_MODEL = 2048
ATTN_HEADS = 8
ATTN_HEAD_DIM = 128
IDX_HEADS = 16
IDX_HEAD_DIM = 64
TOPK_MAX = 256
RET_HEADS = 8
RET_DK = 128
RET_DV = 128
RET_CHUNK = 256
ROPE_BASE = 10000.0
D_FF = 4 * D_MODEL
EPS = 1e-6
Q_BLOCK = 256

OFF_QA, OFF_KA, OFF_VA, OFF_QI, OFF_KI, OFF_WI = 0, 1024, 1152, 1280, 2304, 2368
OFF_QR, OFF_KR, OFF_VR, OFF_GR = 2384, 3408, 4432, 5456
RET_WIDTH = RET_HEADS * RET_DV

LANES = 128
PROJ_TILE = 512
KEY_TILE = 256
COUNT_ROWS = 32
PAGE_GROUP = 2
IDX_PAGE_SLOTS = 8
ATTN_PAGE_SLOTS = 6
SUM_ROWS = 16
LOG2_E = 1.4426950408889634
INT_MIN = -2 ** 31
KEY_NEG_INF = -2 ** 31 + 0x7FFFFF
BF16_KEY_NEG_INF = -2 ** 15 + 0x7F
NEG_BIG = -1e30
VMEM_LIMIT = 56 * 1024 * 1024

TILES = {
    "proj_attn_rows": 1024,
    "proj_ret_rows": 512,
    "out_proj_rows": 512,
    "mlp_rows": 512,
    "mlp_ff": 1024,
    "mlp_cast_ff": 1024,
    "sample_ret_rows": 8,
}


def _cparams(sem):
    return pltpu.CompilerParams(dimension_semantics=sem, vmem_limit_bytes=VMEM_LIMIT)


def _resident(shape):
    zeros = (0,) * len(shape)
    return pl.BlockSpec(shape, lambda *_: zeros, pipeline_mode=pl.Buffered(1))


def _normed_input(x_ref, g_ref, xn_ref):
    x = x_ref[...]
    ms = jnp.mean(x * x, axis=-1, keepdims=True)
    xn_ref[...] = (x * lax.rsqrt(ms + EPS) * g_ref[...]).astype(BF16)


def _matmul_rows(xn, wt_ref, r0, n):
    return lax.dot_general(xn, wt_ref[r0:r0 + n, :], (((1,), (1,)), ((), ())),
                           preferred_element_type=F32)


def _with_sample_rows(x_ref, xs_ref, g_ref, xn_ref, compute):
    tm, ns = x_ref.shape[0], xs_ref.shape[0]
    last = pl.program_id(0) == pl.num_programs(0) - 1

    @pl.when(last)
    def _prompt_and_sample_rows():
        _normed_input(x_ref, g_ref, xn_ref.at[pl.ds(0, tm)])
        _normed_input(xs_ref, g_ref, xn_ref.at[pl.ds(tm, ns)])
        compute(xn_ref[...], with_samples=True)

    @pl.when(jnp.logical_not(last))
    def _prompt_rows():
        _normed_input(x_ref, g_ref, xn_ref.at[pl.ds(0, tm)])
        compute(xn_ref[pl.ds(0, tm), :], with_samples=False)


def _proj_attn_body(x_ref, xs_ref, g_ref, wt_ref, lng_ref, lnb_ref, *refs):
    outs_p, outs_s, xn_ref = refs[0:7], refs[7:14], refs[14]
    tm, ns = x_ref.shape[0], xs_ref.shape[0]

    def compute(xn, with_samples):
        parts = [((0, tm), outs_p)] + ([((tm, tm + ns), outs_s)] if with_samples else [])
        mm = functools.partial(_matmul_rows, xn, wt_ref)
        for t in range(ATTN_HEADS * ATTN_HEAD_DIM // PROJ_TILE):
            acc = mm(OFF_QA + t * PROJ_TILE, PROJ_TILE)
            for hh in range(PROJ_TILE // LANES):
                piece = acc[:, hh * LANES:(hh + 1) * LANES].astype(BF16)
                for (a, b), o in parts:
                    o[0][4 * t + hh] = piece[a:b]
        for t in range(IDX_HEADS * IDX_HEAD_DIM // PROJ_TILE):
            acc = mm(OFF_QI + t * PROJ_TILE, PROJ_TILE)
            for hh in range(PROJ_TILE // LANES):
                piece = acc[:, hh * LANES:(hh + 1) * LANES].astype(BF16)
                for (a, b), o in parts:
                    o[1][4 * t + hh] = piece[a:b]
        kv = mm(OFF_KA, 2 * ATTN_HEAD_DIM)
        kw = mm(OFF_KI, LANES)
        lane = lax.broadcasted_iota(I32, kw.shape, 1)
        is_k = lane < IDX_HEAD_DIM
        mu = jnp.sum(jnp.where(is_k, kw, 0.0), axis=-1, keepdims=True) * (1.0 / IDX_HEAD_DIM)
        d = jnp.where(is_k, kw - mu, 0.0)
        var = jnp.sum(d * d, axis=-1, keepdims=True) * (1.0 / IDX_HEAD_DIM)
        kn = d * lax.rsqrt(var + EPS) * lng_ref[...] + lnb_ref[...]
        kd = jnp.where(is_k, kn, pltpu.roll(kn, IDX_HEAD_DIM, 1)).astype(BF16)
        wi = kw[:, IDX_HEAD_DIM:IDX_HEAD_DIM + IDX_HEADS] * (IDX_HEADS ** -0.5)
        for (a, b), o in parts:
            o[2][...] = kv[a:b, :ATTN_HEAD_DIM]
            o[3][...] = kv[a:b, ATTN_HEAD_DIM:]
            o[4][...] = kn[a:b, :IDX_HEAD_DIM]
            o[5][...] = kd[a:b]
            o[6][...] = wi[a:b].T

    _with_sample_rows(x_ref, xs_ref, g_ref, xn_ref, compute)


def _proj_ret_body(x_ref, xs_ref, g_ref, wt_ref, cs_ref, css_ref, main_ref, mains_ref, xn_ref):
    tm, ns = x_ref.shape[0], xs_ref.shape[0]

    def compute(xn, with_samples):
        parts = [((0, tm), main_ref)] + ([((tm, tm + ns), mains_ref)] if with_samples else [])
        cs = jnp.concatenate([cs_ref[...], css_ref[...]], axis=0) if with_samples else cs_ref[...]
        cosf = cs[:, :LANES]
        sinf = cs[:, LANES:]
        tiles = RET_WIDTH // PROJ_TILE
        for seg, (off, scale) in enumerate(((OFF_QR, None), (OFF_KR, RET_DK ** -0.5))):
            for t in range(tiles):
                acc = _matmul_rows(xn, wt_ref, off + t * PROJ_TILE, PROJ_TILE)
                for hh in range(PROJ_TILE // LANES):
                    xh = acc[:, hh * LANES:(hh + 1) * LANES]
                    r = xh * cosf + pltpu.roll(xh, RET_DK // 2, 1) * sinf
                    if scale is not None:
                        r = r * scale
                    c0 = seg * RET_WIDTH + t * PROJ_TILE + hh * LANES
                    r = r.astype(BF16)
                    for (a, b), o in parts:
                        o[:, c0:c0 + LANES] = r[a:b]
        for seg, off in ((2, OFF_VR), (3, OFF_GR)):
            for t in range(tiles):
                acc = _matmul_rows(xn, wt_ref, off + t * PROJ_TILE, PROJ_TILE).astype(BF16)
                c0 = seg * RET_WIDTH + t * PROJ_TILE
                for (a, b), o in parts:
                    o[:, c0:c0 + PROJ_TILE] = acc[a:b]

    _with_sample_rows(x_ref, xs_ref, g_ref, xn_ref, compute)


def _project_attn(x2d, xs2d, g1, wt, lng, lnb, tm):
    m, ns = x2d.shape[0], xs2d.shape[0]
    row = lambda i: (i, 0)

    def shapes(n):
        return (
            jax.ShapeDtypeStruct((ATTN_HEADS, n, ATTN_HEAD_DIM), BF16),
            jax.ShapeDtypeStruct((IDX_HEADS // 2, n, LANES), BF16),
            jax.ShapeDtypeStruct((n, ATTN_HEAD_DIM), F32),
            jax.ShapeDtypeStruct((n, ATTN_HEAD_DIM), F32),
            jax.ShapeDtypeStruct((n, IDX_HEAD_DIM), F32),
            jax.ShapeDtypeStruct((n, LANES), BF16),
            jax.ShapeDtypeStruct((IDX_HEADS, n), F32),
        )

    prompt_specs = (
        pl.BlockSpec((ATTN_HEADS, tm, ATTN_HEAD_DIM), lambda i: (0, i, 0)),
        pl.BlockSpec((IDX_HEADS // 2, tm, LANES), lambda i: (0, i, 0)),
        pl.BlockSpec((tm, ATTN_HEAD_DIM), row),
        pl.BlockSpec((tm, ATTN_HEAD_DIM), row),
        pl.BlockSpec((tm, IDX_HEAD_DIM), row),
        pl.BlockSpec((tm, LANES), row),
        pl.BlockSpec((IDX_HEADS, tm), lambda i: (0, i)),
    )
    sample_specs = tuple(pl.BlockSpec(s.shape, lambda i, nd=len(s.shape): (0,) * nd)
                         for s in shapes(ns))
    outs = pl.pallas_call(
        _proj_attn_body,
        grid=(m // tm,),
        in_specs=[pl.BlockSpec((tm, D_MODEL), row), _resident((ns, D_MODEL)),
                  _resident((1, D_MODEL)), _resident((OFF_QR + LANES, D_MODEL)),
                  _resident((1, LANES)), _resident((1, LANES))],
        out_specs=prompt_specs + sample_specs,
        out_shape=shapes(m) + shapes(ns),
        scratch_shapes=[pltpu.VMEM((tm + ns, D_MODEL), BF16)],
        compiler_params=_cparams(("arbitrary",)),
        name="proj_attn",
    )(x2d, xs2d, g1, wt, lng, lnb)
    return outs[:7], outs[7:]


def _project_ret(x2d, xs2d, g1, wt, cs, css, tm):
    m, ns = x2d.shape[0], xs2d.shape[0]
    n_pos_blocks = cs.shape[0] // tm
    row = lambda i: (i, 0)
    return pl.pallas_call(
        _proj_ret_body,
        grid=(m // tm,),
        in_specs=[pl.BlockSpec((tm, D_MODEL), row), _resident((ns, D_MODEL)),
                  _resident((1, D_MODEL)), _resident(wt.shape),
                  pl.BlockSpec((tm, 2 * LANES), lambda i: (i % n_pos_blocks, 0)),
                  _resident((ns, 2 * LANES))],
        out_specs=(pl.BlockSpec((tm, 4 * RET_WIDTH), row),
                   pl.BlockSpec((ns, 4 * RET_WIDTH), lambda i: (0, 0))),
        out_shape=(jax.ShapeDtypeStruct((m, 4 * RET_WIDTH), BF16),
                   jax.ShapeDtypeStruct((ns, 4 * RET_WIDTH), BF16)),
        scratch_shapes=[pltpu.VMEM((tm + ns, D_MODEL), BF16)],
        compiler_params=_cparams(("arbitrary",)),
        name="proj_ret",
    )(x2d, xs2d, g1, wt, cs, css)


def _key_to_float(key):
    bits = key ^ ((key >> 31) & 0x7FFFFFFF)
    return lax.bitcast_convert_type(bits, F32)


def _threshold_search(count_ge, n_iter, shape):
    def body(it, t):
        bit = lax.shift_left(jnp.int32(1), 31 - it)
        cand = t ^ bit
        cnt = count_ge(_key_to_float(cand))
        return jnp.where(cnt >= float(TOPK_MAX), cand, t)

    t = lax.fori_loop(0, n_iter, body, jnp.full(shape, INT_MIN, I32))
    return _key_to_float(jnp.maximum(t, KEY_NEG_INF))


def _bf16_key_to_f32_key(k16):
    return lax.shift_left(k16, 16) | jnp.where(k16 < 0, 0xFFFF, 0)


def _threshold_search_coarse_fine(count_ge_bf16, count_ge, run, shape):
    def coarse(it, u):
        cand = u | lax.shift_left(jnp.int32(1), 15 - it)
        c = _key_to_float(_bf16_key_to_f32_key(cand - 32768)).astype(BF16)
        return jnp.where(count_ge_bf16(c) >= float(TOPK_MAX), cand, u)

    u = lax.fori_loop(0, jnp.where(run, 16, 0), coarse, jnp.zeros(shape, I32))
    k1 = jnp.maximum(u - 32768, BF16_KEY_NEG_INF)
    lo = _bf16_key_to_f32_key(jnp.maximum(k1 - 1, -32768))
    hi = _bf16_key_to_f32_key(jnp.minimum(k1 + 1, 32767))

    def fine(it, t):
        cand = t + lax.shift_left(jnp.int32(1), 16 - it)
        ok = (cand < hi) & (count_ge(_key_to_float(cand)) >= float(TOPK_MAX))
        return jnp.where(ok, cand, t)

    t = lax.fori_loop(0, jnp.where(run, 17, 0), fine, lo)
    return _key_to_float(jnp.maximum(t, KEY_NEG_INF))


def _attn_body(qa_ref, qi_ref, wit_ref, ka_ref, va_ref, kd_ref, tri_ref, o_ref,
               kbf, vtb, scr, mrun, acc_s, kmax, scr16):
    qb = pl.program_id(1)
    n_heads_q = ATTN_HEADS * Q_BLOCK
    n_pairs = IDX_HEADS // 2
    dv = ATTN_HEAD_DIM
    logit_scale = ATTN_HEAD_DIM ** -0.5 * LOG2_E

    @pl.when(qb == 0)
    def _cast():
        ka = ka_ref[...]
        kbf[...] = ka.astype(BF16)
        kmax[...] = jnp.broadcast_to(jnp.max(jnp.sum(ka * ka, axis=1, keepdims=True)), kmax.shape)
        for kt in range(vtb.shape[0]):
            vtb[kt, :dv] = va_ref[kt * KEY_TILE:(kt + 1) * KEY_TILE, :].T.astype(BF16)
            vtb[kt, dv:] = jnp.ones((vtb.shape[1] - dv, KEY_TILE), BF16)

    nk = ((qb + 1) * Q_BLOCK + KEY_TILE - 1) // KEY_TILE
    wt = wit_ref[...] * (IDX_HEAD_DIM ** -0.5)
    qi2 = qi_ref[...].reshape(n_pairs * Q_BLOCK, LANES)
    lo_half = lax.broadcasted_iota(I32, (KEY_TILE, LANES), 1) < IDX_HEAD_DIM
    qidx = qb * Q_BLOCK + lax.broadcasted_iota(I32, (KEY_TILE, Q_BLOCK), 1)
    kidx0 = lax.broadcasted_iota(I32, (KEY_TILE, Q_BLOCK), 0)
    contract_last = (((1,), (1,)), ((), ()))

    def idx_body(kt, carry):
        off = pl.multiple_of(kt * KEY_TILE, KEY_TILE)
        kit = kd_ref[pl.ds(off, KEY_TILE), :]
        zero = jnp.zeros_like(kit)
        s_even = lax.dot_general(jnp.where(lo_half, kit, zero), qi2, contract_last,
                                 preferred_element_type=F32)
        s_odd = lax.dot_general(jnp.where(lo_half, zero, kit), qi2, contract_last,
                                preferred_element_type=F32)
        score = jnp.zeros((KEY_TILE, Q_BLOCK), F32)
        for g in range(n_pairs):
            cs = slice(g * Q_BLOCK, (g + 1) * Q_BLOCK)
            score = score + jnp.maximum(s_even[:, cs], 0.0) * wt[2 * g:2 * g + 1, :]
            score = score + jnp.maximum(s_odd[:, cs], 0.0) * wt[2 * g + 1:2 * g + 2, :]
        score = jnp.where(kidx0 + off <= qidx, score, -jnp.inf)
        scr[kt] = score
        scr16[kt] = score.astype(BF16)
        return carry

    def for_tiles(fn):
        def pair(j, carry):
            fn(2 * j, 0)
            fn(2 * j + 1, 0)
            return carry

        lax.fori_loop(0, nk // 2, pair, 0)

        @pl.when(nk % 2 == 1)
        def _last():
            fn(nk - 1, 0)

    for_tiles(idx_body)

    def count_ge_bf16(c):
        def body(kt, acc):
            hit = scr16[kt] >= c
            for r in range(KEY_TILE // COUNT_ROWS):
                acc = jnp.where(hit[r * COUNT_ROWS:(r + 1) * COUNT_ROWS], acc + 1.0, acc)
            return acc
        acc = lax.fori_loop(0, nk, body, jnp.zeros((COUNT_ROWS, Q_BLOCK), BF16))
        return jnp.sum(acc.astype(F32), axis=0, keepdims=True)

    def count_cmp(cmp):
        def body(kt, acc):
            hit = cmp(scr[kt])
            for r in range(KEY_TILE // COUNT_ROWS):
                acc = jnp.where(hit[r * COUNT_ROWS:(r + 1) * COUNT_ROWS], acc + 1.0, acc)
            return acc
        acc = lax.fori_loop(0, nk, body, jnp.zeros((COUNT_ROWS, Q_BLOCK), F32))
        return jnp.sum(acc, axis=0, keepdims=True)

    tf = _threshold_search_coarse_fine(
        count_ge_bf16, lambda c: count_cmp(lambda sc: sc >= c),
        qb >= TOPK_MAX // Q_BLOCK, (1, Q_BLOCK))
    excess = jnp.max(count_cmp(lambda sc: sc >= tf)) > float(TOPK_MAX)

    qa2 = qa_ref[...].reshape(n_heads_q, ATTN_HEAD_DIM)

    def logits(kt):
        off = pl.multiple_of(kt * KEY_TILE, KEY_TILE)
        s = lax.dot_general(kbf[pl.ds(off, KEY_TILE), :], qa2, contract_last,
                            preferred_element_type=F32)
        return s * logit_scale

    def sel_plain(kt, carry):
        off = pl.multiple_of(kt * KEY_TILE, KEY_TILE)
        return (scr[kt] >= tf) & (kidx0 + off <= qidx), carry

    def sel_ties(need, kt, tie_off):
        off = pl.multiple_of(kt * KEY_TILE, KEY_TILE)
        sc = scr[kt]
        eq = sc == tf
        tie = jnp.where(eq, 1.0, 0.0)
        rank = jnp.dot(tri_ref[...], tie.astype(BF16), preferred_element_type=F32) + tie_off
        sel = ((sc > tf) | (eq & (rank <= need))) & (kidx0 + off <= qidx)
        return sel, tie_off + jnp.sum(tie, axis=0, keepdims=True)

    no_ties = jnp.zeros((1, Q_BLOCK), F32)

    def softmax_sum(m, sel_fn):
        acc_s[...] = jnp.zeros(acc_s.shape, F32)

        def body(kt, carry):
            sel, carry = sel_fn(kt, carry)
            e = jnp.exp2(logits(kt) - m)
            parts = []
            for h in range(ATTN_HEADS):
                cs = slice(h * Q_BLOCK, (h + 1) * Q_BLOCK)
                parts.append(jnp.where(sel, e[:, cs], 0.0).astype(BF16))
            p = jnp.concatenate(parts, axis=1)
            acc_s[...] += jnp.dot(vtb[kt], p, preferred_element_type=F32)
            return carry

        if sel_fn is sel_plain:
            for_tiles(body)
        else:
            lax.fori_loop(0, nk, body, no_ties)

    def selected_max(sel_fn):
        mrun[...] = jnp.full(mrun.shape, NEG_BIG, F32)

        def body(kt, carry):
            sel, carry = sel_fn(kt, carry)
            s = logits(kt)
            for h in range(ATTN_HEADS):
                cs = slice(h * Q_BLOCK, (h + 1) * Q_BLOCK)
                sh = jnp.where(sel, s[:, cs], NEG_BIG)
                mrun[:, cs] = jnp.maximum(
                    mrun[:, cs], jnp.max(sh.reshape(KEY_TILE // 8, 8, Q_BLOCK), axis=0))
            return carry

        lax.fori_loop(0, nk, body, no_ties)
        return jnp.max(mrun[...], axis=0, keepdims=True)

    def fast_path():
        q2 = (qa2 * qa2).astype(BF16)
        qsq = lax.dot_general(jnp.ones((8, ATTN_HEAD_DIM), BF16), q2, contract_last,
                              preferred_element_type=F32)[0:1]
        softmax_sum(jnp.sqrt(qsq * kmax[0:1, 0:1]) * logit_scale, sel_plain)
        return (jnp.min(acc_s[dv:dv + 1, :]) > 0.0).astype(I32)

    done = lax.cond(excess, lambda: jnp.int32(0), fast_path) == 1

    @pl.when(jnp.logical_not(done) & excess)
    def _exact_with_ties():
        need = float(TOPK_MAX) - count_cmp(lambda sc: sc > tf)
        sel_fn = functools.partial(sel_ties, need)
        softmax_sum(selected_max(sel_fn), sel_fn)

    @pl.when(jnp.logical_not(done) & jnp.logical_not(excess))
    def _exact_without_ties():
        softmax_sum(selected_max(sel_plain), sel_plain)

    out = acc_s[:dv, :] / acc_s[dv:dv + 1, :]
    for h in range(ATTN_HEADS):
        oh = out[:, h * Q_BLOCK:(h + 1) * Q_BLOCK].T
        o_ref[:, h * ATTN_HEAD_DIM:(h + 1) * ATTN_HEAD_DIM] = oh.astype(BF16)


def _prompt_attention(qa_hm, qi_pm, wi_t, ka, va, kd, tri, batch, seq):
    nq = seq // Q_BLOCK
    nkt = seq // KEY_TILE
    m = batch * seq
    n_heads_q = ATTN_HEADS * Q_BLOCK
    assert seq // COUNT_ROWS <= 256
    in_specs = [
        pl.BlockSpec((ATTN_HEADS, Q_BLOCK, ATTN_HEAD_DIM), lambda b, q: (0, b * nq + q, 0)),
        pl.BlockSpec((IDX_HEADS // 2, Q_BLOCK, LANES), lambda b, q: (0, b * nq + q, 0)),
        pl.BlockSpec((IDX_HEADS, Q_BLOCK), lambda b, q: (0, b * nq + q)),
        pl.BlockSpec((seq, ATTN_HEAD_DIM), lambda b, q: (b, 0)),
        pl.BlockSpec((seq, ATTN_HEAD_DIM), lambda b, q: (b, 0)),
        pl.BlockSpec((seq, LANES), lambda b, q: (b, 0)),
        pl.BlockSpec((KEY_TILE, KEY_TILE), lambda b, q: (0, 0)),
    ]
    return pl.pallas_call(
        _attn_body,
        grid=(batch, nq),
        in_specs=in_specs,
        out_specs=pl.BlockSpec((Q_BLOCK, ATTN_HEADS * ATTN_HEAD_DIM), lambda b, q: (b * nq + q, 0)),
        out_shape=jax.ShapeDtypeStruct((m, ATTN_HEADS * ATTN_HEAD_DIM), BF16),
        scratch_shapes=[
            pltpu.VMEM((seq, ATTN_HEAD_DIM), BF16),
            pltpu.VMEM((nkt, ATTN_HEAD_DIM + SUM_ROWS, KEY_TILE), BF16),
            pltpu.VMEM((nkt, KEY_TILE, Q_BLOCK), F32),
            pltpu.VMEM((8, n_heads_q), F32),
            pltpu.VMEM((ATTN_HEAD_DIM + SUM_ROWS, n_heads_q), F32),
            pltpu.VMEM((8, LANES), F32),
            pltpu.VMEM((nkt, KEY_TILE, Q_BLOCK), BF16),
        ],
        compiler_params=_cparams(("arbitrary", "arbitrary")),
        name="prompt_attn",
    )(qa_hm, qi_pm, wi_t, ka, va, kd, tri)


def _gate(o, g):
    rn = o * lax.rsqrt(jnp.mean(o * o, axis=-1, keepdims=True) + EPS)
    return rn * (g / (1.0 + jnp.exp(-g)))


def _ret_body(q_ref, k_ref, v_ref, g_ref, decay_ref, rsc_ref, zeta_ref, gpow_ref,
              rg_ref, st_ref):
    c = pl.program_id(1)

    @pl.when(c == 0)
    def _init():
        st_ref[...] = jnp.zeros(st_ref.shape, F32)

    for h in range(RET_HEADS):
        sl = slice(h * 128, (h + 1) * 128)
        q = q_ref[:, sl]
        k = k_ref[:, sl]
        v = v_ref[:, sl]
        r_old = st_ref[0, h]
        qk = lax.dot_general(q, k, (((1,), (1,)), ((), ())), preferred_element_type=F32)
        inner = jnp.dot((qk * decay_ref[h]).astype(BF16), v, preferred_element_type=F32)
        cross = jnp.dot(q, r_old.astype(BF16), preferred_element_type=F32) * rsc_ref[h]
        kz = (k.astype(F32) * zeta_ref[h]).astype(BF16)
        upd = lax.dot_general(kz, v, (((0,), (0,)), ((), ())), preferred_element_type=F32)
        st_ref[0, h] = r_old * gpow_ref[h] + upd
        rg_ref[:, sl] = _gate(inner + cross, g_ref[:, sl].astype(F32)).astype(BF16)


def _prompt_retention(main, decay, rsc, zeta, gpow, batch, seq):
    nc = seq // RET_CHUNK
    m = batch * seq
    width = RET_WIDTH
    const3 = lambda b, c: (0, 0, 0)
    in_specs = [
        pl.BlockSpec((RET_CHUNK, width), lambda b, c: (b * nc + c, 0)),
        pl.BlockSpec((RET_CHUNK, width), lambda b, c: (b * nc + c, 1)),
        pl.BlockSpec((RET_CHUNK, width), lambda b, c: (b * nc + c, 2)),
        pl.BlockSpec((RET_CHUNK, width), lambda b, c: (b * nc + c, 3)),
        pl.BlockSpec((RET_HEADS, RET_CHUNK, RET_CHUNK), const3),
        pl.BlockSpec((RET_HEADS, RET_CHUNK, RET_DV), const3),
        pl.BlockSpec((RET_HEADS, RET_CHUNK, RET_DK), const3),
        pl.BlockSpec((RET_HEADS, 1, RET_DV), const3),
    ]
    return pl.pallas_call(
        _ret_body,
        grid=(batch, nc),
        in_specs=in_specs,
        out_specs=(
            pl.BlockSpec((RET_CHUNK, width), lambda b, c: (b * nc + c, 0)),
            pl.BlockSpec((1, RET_HEADS, RET_DK, RET_DV), lambda b, c: (b, 0, 0, 0)),
        ),
        out_shape=(
            jax.ShapeDtypeStruct((m, width), BF16),
            jax.ShapeDtypeStruct((batch, RET_HEADS, RET_DK, RET_DV), F32),
        ),
        compiler_params=_cparams(("arbitrary", "arbitrary")),
        name="prompt_ret",
    )(main, main, main, main, decay, rsc, zeta, gpow)


def _outproj_body(a_ref, r_ref, wa_ref, wr_ref, x_ref, g2_ref, x1_ref, h2_ref):
    mixed = (jnp.dot(a_ref[...], wa_ref[...], preferred_element_type=F32)
             + jnp.dot(r_ref[...], wr_ref[...], preferred_element_type=F32))
    x1 = x_ref[...] + mixed
    x1_ref[...] = x1
    ms = jnp.mean(x1 * x1, axis=-1, keepdims=True)
    h2_ref[...] = (x1 * lax.rsqrt(ms + EPS) * g2_ref[...]).astype(BF16)


def _out_projection(attn_o, rg, wa, wr, x2d, g2, tm):
    m = x2d.shape[0]
    half = attn_o.shape[1]
    in_specs = [
        pl.BlockSpec((tm, half), lambda i: (i, 0)),
        pl.BlockSpec((tm, half), lambda i: (i, 0)),
        pl.BlockSpec((half, D_MODEL), lambda i: (0, 0)),
        pl.BlockSpec((half, D_MODEL), lambda i: (0, 0)),
        pl.BlockSpec((tm, D_MODEL), lambda i: (i, 0)),
        pl.BlockSpec((1, D_MODEL), lambda i: (0, 0)),
    ]
    return pl.pallas_call(
        _outproj_body,
        grid=(m // tm,),
        in_specs=in_specs,
        out_specs=(pl.BlockSpec((tm, D_MODEL), lambda i: (i, 0)),
                   pl.BlockSpec((tm, D_MODEL), lambda i: (i, 0))),
        out_shape=(jax.ShapeDtypeStruct((m, D_MODEL), F32),
                   jax.ShapeDtypeStruct((m, D_MODEL), BF16)),
        compiler_params=_cparams(("arbitrary",)),
        name="out_proj",
    )(attn_o, rg, wa, wr, x2d, g2)


def _mlp_body(h2_ref, wu_ref, wd_ref, x1_ref, gf_ref, y_ref, acc_ref):
    f = pl.program_id(1)

    @pl.when(f == 0)
    def _init():
        acc_ref[...] = x1_ref[...]

    u = jnp.dot(h2_ref[...], wu_ref[...], preferred_element_type=F32)
    a = jnp.maximum(u, 0.0)
    acc_ref[...] += jnp.dot((a * a).astype(BF16), wd_ref[...], preferred_element_type=F32)

    @pl.when(f == pl.num_programs(1) - 1)
    def _final():
        x2 = acc_ref[...]
        ms = jnp.mean(x2 * x2, axis=-1, keepdims=True)
        y_ref[...] = x2 * lax.rsqrt(ms + EPS) * gf_ref[...]


def _mlp_cast_body(h2_ref, wu_ref, wd_ref, x1_ref, gf_ref, y_ref, wub_ref, wdb_ref, acc_ref):
    f = pl.program_id(0)

    @pl.when(f == 0)
    def _init():
        acc_ref[...] = x1_ref[...]

    wu = wu_ref[...].astype(BF16)
    wd = wd_ref[...].astype(BF16)
    wub_ref[...] = wu
    wdb_ref[...] = wd
    u = jnp.dot(h2_ref[...], wu, preferred_element_type=F32)
    a = jnp.maximum(u, 0.0)
    acc_ref[...] += jnp.dot((a * a).astype(BF16), wd, preferred_element_type=F32)

    @pl.when(f == pl.num_programs(0) - 1)
    def _final():
        x2 = acc_ref[...]
        ms = jnp.mean(x2 * x2, axis=-1, keepdims=True)
        y_ref[...] = x2 * lax.rsqrt(ms + EPS) * gf_ref[...]


def _mlp_and_cast(h2, w_up, w_down, x1, gf, tf):
    m = h2.shape[0]
    full = lambda f: (0, 0)
    return pl.pallas_call(
        _mlp_cast_body,
        grid=(D_FF // tf,),
        in_specs=[
            pl.BlockSpec((m, D_MODEL), full),
            pl.BlockSpec((D_MODEL, tf), lambda f: (0, f)),
            pl.BlockSpec((tf, D_MODEL), lambda f: (f, 0)),
            pl.BlockSpec((m, D_MODEL), full),
            pl.BlockSpec((1, D_MODEL), full),
        ],
        out_specs=(
            pl.BlockSpec((m, D_MODEL), full),
            pl.BlockSpec((D_MODEL, tf), lambda f: (0, f)),
            pl.BlockSpec((tf, D_MODEL), lambda f: (f, 0)),
        ),
        out_shape=(
            jax.ShapeDtypeStruct((m, D_MODEL), F32),
            jax.ShapeDtypeStruct((D_MODEL, D_FF), BF16),
            jax.ShapeDtypeStruct((D_FF, D_MODEL), BF16),
        ),
        scratch_shapes=[pltpu.VMEM((m, D_MODEL), F32)],
        compiler_params=_cparams(("arbitrary",)),
        name="mlp_cast",
    )(h2, w_up, w_down, x1, gf)


def _mlp(h2, wu, wd, x1, gf, tm, tf):
    m = h2.shape[0]
    in_specs = [
        pl.BlockSpec((tm, D_MODEL), lambda i, f: (i, 0)),
        pl.BlockSpec((D_MODEL, tf), lambda i, f: (0, f)),
        pl.BlockSpec((tf, D_MODEL), lambda i, f: (f, 0)),
        pl.BlockSpec((tm, D_MODEL), lambda i, f: (i, 0)),
        pl.BlockSpec((1, D_MODEL), lambda i, f: (0, 0)),
    ]
    return pl.pallas_call(
        _mlp_body,
        grid=(m // tm, D_FF // tf),
        in_specs=in_specs,
        out_specs=pl.BlockSpec((tm, D_MODEL), lambda i, f: (i, 0)),
        out_shape=jax.ShapeDtypeStruct((m, D_MODEL), F32),
        scratch_shapes=[pltpu.VMEM((tm, D_MODEL), F32)],
        compiler_params=_cparams(("arbitrary", "arbitrary")),
        name="mlp",
    )(h2, wu, wd, x1, gf)


def _fetch_pages(pt_ref, step, slot, streams, start):
    n_pages = pt_ref.shape[1]
    for hbm, buf, sem in streams:
        for j in range(n_pages):
            cp = pltpu.make_async_copy(hbm.at[pt_ref[step, j]], buf.at[slot, j], sem.at[slot])
            if start:
                cp.start()
            else:
                cp.wait()


def _paged_loop(pt_ref, streams, step_fn):
    nb = pt_ref.shape[0]
    n_slots = streams[0][1].shape[0]
    ahead = n_slots - PAGE_GROUP
    assert nb % PAGE_GROUP == 0 and ahead % PAGE_GROUP == 0 and PAGE_GROUP <= ahead <= nb
    for s in range(ahead):
        _fetch_pages(pt_ref, s, s, streams, start=True)

    def body(g, carry):
        b0 = g * PAGE_GROUP

        @pl.when(b0 + ahead < nb)
        def _next():
            for i in range(PAGE_GROUP):
                row = b0 + ahead + i
                _fetch_pages(pt_ref, row, row % n_slots, streams, start=True)

        for i in range(PAGE_GROUP):
            _fetch_pages(pt_ref, b0 + i, (b0 + i) % n_slots, streams, start=False)
        for i in range(PAGE_GROUP):
            step_fn(b0 + i, (b0 + i) % n_slots, i)
        return carry

    lax.fori_loop(0, nb // PAGE_GROUP, body, 0)


def _sidx_body(pt_ref, qi_ref, w_ref, kin_ref, cache_hbm, out_ref, kt_s, pbuf, sem):
    n_pages = pt_ref.shape[1]
    page = pbuf.shape[3]
    past = n_pages * page
    lane = lax.broadcasted_iota(I32, (1, LANES), 1)

    def step(b, slot, lane_of_trip):
        qi = qi_ref[b]
        w = w_ref[b] * (IDX_HEAD_DIM ** -0.5)
        for j in range(n_pages):
            kt_s[lane_of_trip, :, j * page:(j + 1) * page] = pbuf[slot, j].astype(BF16)
        s = jnp.dot(qi, kt_s[lane_of_trip], preferred_element_type=F32)
        out_ref[b, :, 0:past] = jnp.sum(jnp.maximum(s, 0.0) * w, axis=0, keepdims=True)
        sn = jnp.sum(qi.astype(F32) * kin_ref[b].astype(BF16).astype(F32), axis=1, keepdims=True)
        rn = jnp.sum(jnp.maximum(sn, 0.0) * w, axis=0, keepdims=True)
        out_ref[b, :, past:past + LANES] = jnp.where(lane == 0, rn, -jnp.inf)

    _paged_loop(pt_ref, ((cache_hbm, pbuf, sem),), step)


def _sample_index_scores(page_table, qi_s, wi_s, ki_s, cache_idx_k_t):
    nb, n_pages = page_table.shape
    page = cache_idx_k_t.shape[2]
    width = n_pages * page + LANES

    vmem = pl.BlockSpec(memory_space=pltpu.VMEM)
    return pl.pallas_call(
        _sidx_body,
        in_specs=[pl.BlockSpec(memory_space=pltpu.SMEM), vmem, vmem, vmem,
                  pl.BlockSpec(memory_space=pl.ANY)],
        out_specs=vmem,
        out_shape=jax.ShapeDtypeStruct((nb, 1, width), F32),
        scratch_shapes=[pltpu.VMEM((PAGE_GROUP, IDX_HEAD_DIM, n_pages * page), BF16),
                        pltpu.VMEM((IDX_PAGE_SLOTS, n_pages, IDX_HEAD_DIM, page), F32),
                        pltpu.SemaphoreType.DMA((IDX_PAGE_SLOTS,))],
        compiler_params=pltpu.CompilerParams(vmem_limit_bytes=VMEM_LIMIT),
        name="sample_idx",
    )(page_table, qi_s, wi_s, ki_s, cache_idx_k_t)


def _ssel_body(sc_ref, tri_ref, sel_ref):
    rows, width = sc_ref.shape
    nt = width // LANES
    n_valid = (nt - 1) * LANES + 1

    def tile(kt):
        return sc_ref[:, kt * LANES:(kt + 1) * LANES]

    def count_cmp(cmp):
        acc = jnp.zeros((rows, LANES), F32)
        for kt in range(nt):
            acc = acc + jnp.where(cmp(tile(kt)), 1.0, 0.0)
        return jnp.broadcast_to(jnp.sum(acc, axis=1, keepdims=True), (rows, LANES))

    tf = _threshold_search(lambda c: count_cmp(lambda sc: sc >= c), 32, (rows, LANES))
    need = float(TOPK_MAX) - count_cmp(lambda sc: sc > tf)
    tie_off = jnp.zeros((rows, LANES), F32)
    for kt in range(nt):
        col = kt * LANES + lax.broadcasted_iota(I32, (rows, LANES), 1)
        sc = tile(kt)
        eq = sc == tf
        tie = jnp.where(eq, 1.0, 0.0)
        rank = jnp.dot(tie.astype(BF16), tri_ref[...], preferred_element_type=F32) + tie_off
        sel = ((sc > tf) | (eq & (rank <= need))) & (col < n_valid)
        sel_ref[:, kt * LANES:(kt + 1) * LANES] = jnp.where(sel, 1.0, 0.0)
        tie_off = tie_off + jnp.broadcast_to(jnp.sum(tie, axis=1, keepdims=True), (rows, LANES))


def _sample_select(scores2d, tri):
    rows, width = scores2d.shape
    return pl.pallas_call(
        _ssel_body,
        out_shape=jax.ShapeDtypeStruct((rows, width), F32),
        compiler_params=pltpu.CompilerParams(vmem_limit_bytes=VMEM_LIMIT),
        name="sample_select",
    )(scores2d, tri)


def _sattn_body(pt_ref, q_ref, sel_ref, kn_ref, vn_ref, ck_hbm, cv_hbm, o_ref,
                kbuf, vbuf, ksem, vsem):
    n_pages = pt_ref.shape[1]
    page = kbuf.shape[2]
    past = n_pages * page
    scale = ATTN_HEAD_DIM ** -0.5

    def step(b, slot, lane_of_trip):
        q = q_ref[b]
        k_all = kbuf[slot].reshape(past, ATTN_HEAD_DIM).astype(BF16)
        v_all = vbuf[slot].reshape(past, ATTN_HEAD_DIM).astype(BF16)
        s = lax.dot_general(q, k_all, (((1,), (1,)), ((), ())), preferred_element_type=F32)
        s = jnp.where(sel_ref[b, :, 0:past] > 0.5, s * scale, NEG_BIG)
        kn = kn_ref[b].astype(BF16).astype(F32)
        sn = jnp.sum(q.astype(F32) * kn, axis=1, keepdims=True) * scale
        sn = jnp.where(sel_ref[b, :, past:past + 1] > 0.5, sn, NEG_BIG)
        m = jnp.maximum(jnp.max(s, axis=1, keepdims=True), sn)
        pn = jnp.exp(sn - m)
        p = jnp.exp(s - m)
        l = pn + jnp.sum(p, axis=1, keepdims=True)
        acc = (pn * vn_ref[b].astype(BF16).astype(F32)
               + jnp.dot(p.astype(BF16), v_all, preferred_element_type=F32))
        o_ref[b] = (acc / l).astype(BF16)

    _paged_loop(pt_ref, ((ck_hbm, kbuf, ksem), (cv_hbm, vbuf, vsem)), step)


def _sample_attention(page_table, qa_s, sel3, ka_s, va_s, cache_k, cache_v):
    nb, n_pages = page_table.shape
    page = cache_k.shape[1]

    vmem = pl.BlockSpec(memory_space=pltpu.VMEM)
    hbm = pl.BlockSpec(memory_space=pl.ANY)
    return pl.pallas_call(
        _sattn_body,
        in_specs=[pl.BlockSpec(memory_space=pltpu.SMEM), vmem, vmem, vmem, vmem, hbm, hbm],
        out_specs=vmem,
        out_shape=jax.ShapeDtypeStruct((nb, ATTN_HEADS, ATTN_HEAD_DIM), BF16),
        scratch_shapes=[pltpu.VMEM((ATTN_PAGE_SLOTS, n_pages, page, ATTN_HEAD_DIM), F32),
                        pltpu.VMEM((ATTN_PAGE_SLOTS, n_pages, page, ATTN_HEAD_DIM), F32),
                        pltpu.SemaphoreType.DMA((ATTN_PAGE_SLOTS,)),
                        pltpu.SemaphoreType.DMA((ATTN_PAGE_SLOTS,))],
        compiler_params=pltpu.CompilerParams(vmem_limit_bytes=VMEM_LIMIT),
        name="sample_attn",
    )(page_table, qa_s, sel3, ka_s, va_s, cache_k, cache_v)


def _sret_body(qkvg_ref, st_ref, gam_ref, rg_ref, so_ref):
    ns = st_ref.shape[0]
    for s in range(ns):
        blk = qkvg_ref[s].astype(F32)
        q8 = blk[0:8]
        k8 = blk[8:16]
        v8 = blk[16:24]
        g8 = blk[24:32]
        q_t = q8.T
        k_t = k8.T
        qk = jnp.sum(q8 * k8, axis=1, keepdims=True)
        rows = []
        for h in range(RET_HEADS):
            r_old = st_ref[s, h]
            gam = gam_ref[h]
            qcol = jnp.broadcast_to(q_t[:, h:h + 1], (RET_DK, RET_DV))
            kcol = jnp.broadcast_to(k_t[:, h:h + 1], (RET_DK, RET_DV))
            vrow = v8[h:h + 1]
            q_r = jnp.sum(qcol * r_old, axis=0, keepdims=True)
            rows.append(gam * q_r + qk[h:h + 1] * vrow)
            so_ref[s, h] = gam * r_old + kcol * vrow
        ret = jnp.concatenate(rows, axis=0)
        rg_ref[s] = _gate(ret, g8).astype(BF16)


def _sample_retention(qkvg, state, gam, ns):
    nb = state.shape[0]
    return pl.pallas_call(
        _sret_body,
        grid=(nb // ns,),
        in_specs=[
            pl.BlockSpec((ns, 32, LANES), lambda i: (i, 0, 0)),
            pl.BlockSpec((ns, RET_HEADS, RET_DK, RET_DV), lambda i: (i, 0, 0, 0)),
            pl.BlockSpec((RET_HEADS, 1, LANES), lambda i: (0, 0, 0)),
        ],
        out_specs=(
            pl.BlockSpec((ns, RET_HEADS, RET_DV), lambda i: (i, 0, 0)),
            pl.BlockSpec((ns, RET_HEADS, RET_DK, RET_DV), lambda i: (i, 0, 0, 0)),
        ),
        out_shape=(
            jax.ShapeDtypeStruct((nb, RET_HEADS, RET_DV), BF16),
            jax.ShapeDtypeStruct(state.shape, F32),
        ),
        compiler_params=_cparams(("arbitrary",)),
        name="sample_ret",
    )(qkvg, state, gam)


def _rotary_table(pos):
    half = RET_DK // 2
    inv = ROPE_BASE ** (-np.arange(half, dtype=np.float64) / half)
    ang = np.asarray(pos, np.float64)[:, None] * inv[None, :]
    cos = np.cos(ang)
    sin = np.sin(ang)
    return jnp.asarray(np.concatenate([cos, cos, -sin, sin], axis=1), F32)


def _retention_constants():
    lg = np.log1p(-np.exp2(-5.0 - np.arange(RET_HEADS, dtype=np.float64)))
    n = RET_CHUNK
    i = np.arange(n, dtype=np.float64)
    diff = i[:, None] - i[None, :]
    decay = np.where(diff[None] >= 0, np.exp(np.maximum(diff, 0.0)[None] * lg[:, None, None]), 0.0)
    rsc = np.exp((i + 1.0)[None, :] * lg[:, None])
    zeta = np.exp((n - 1.0 - i)[None, :] * lg[:, None])
    gpow = np.exp(n * lg)
    gam1 = np.exp(lg)
    rsc_b = np.broadcast_to(rsc[:, :, None], (RET_HEADS, n, RET_DV))
    zeta_b = np.broadcast_to(zeta[:, :, None], (RET_HEADS, n, RET_DK))
    gpow_b = np.broadcast_to(gpow[:, None, None], (RET_HEADS, 1, RET_DV))
    gam1_b = np.broadcast_to(gam1[:, None, None], (RET_HEADS, 1, LANES))
    return tuple(jnp.asarray(a, F32) for a in (decay, rsc_b, zeta_b, gpow_b, gam1_b))


def _upper_tri(n):
    return jnp.asarray(np.triu(np.ones((n, n), np.float32)), BF16)


def _lower_tri(n):
    return jnp.asarray(np.tril(np.ones((n, n), np.float32)), BF16)


def _pad_lanes(v):
    return jnp.pad(v, (0, LANES - v.shape[0])).reshape(1, LANES)


def kernel(x_prompt, x_sample, cache_k, cache_v, cache_idx_k, state_ret, page_table,
           norm1_g, w_in, idx_k_norm_g, idx_k_norm_b, w_out, norm2_g, w_up, w_down, final_norm_g):
    batch, seq, _ = x_prompt.shape
    nb = x_sample.shape[0]
    past_len = page_table.shape[1] * cache_k.shape[1]
    half_mix = ATTN_HEADS * ATTN_HEAD_DIM

    wt = w_in.T.astype(BF16)
    wa = w_out[:half_mix].astype(BF16)
    wr = w_out[half_mix:].astype(BF16)
    g1 = norm1_g.reshape(1, D_MODEL)
    g2 = norm2_g.reshape(1, D_MODEL)
    gf = final_norm_g.reshape(1, D_MODEL)
    lng = _pad_lanes(idx_k_norm_g)
    lnb = _pad_lanes(idx_k_norm_b)
    decay, rsc_b, zeta_b, gpow_b, gam1_b = _retention_constants()

    xp = x_prompt.reshape(batch * seq, D_MODEL)
    xs = x_sample.reshape(nb, D_MODEL)
    cs_p = _rotary_table(np.arange(seq))
    cs_s = _rotary_table(np.full((nb,), past_len))
    (qa_p, qi_p, ka_p, va_p, ki_p, kd_p, wi_p), (qa_s, qi_s, ka_s, va_s, ki_s, _, wi_s) = (
        _project_attn(xp, xs, g1, wt, lng, lnb, tm=TILES["proj_attn_rows"]))
    main_p, main_s = _project_ret(xp, xs, g1, wt, cs_p, cs_s, tm=TILES["proj_ret_rows"])

    attn_p = _prompt_attention(qa_p, qi_p, wi_p, ka_p, va_p, kd_p, _lower_tri(KEY_TILE),
                               batch, seq)
    rg_p, ret_state_p = _prompt_retention(main_p, decay, rsc_b, zeta_b, gpow_b, batch, seq)
    x1_p, h2_p = _out_projection(attn_p, rg_p, wa, wr, xp, g2, tm=TILES["out_proj_rows"])

    scores = _sample_index_scores(
        page_table,
        qi_s.transpose(1, 0, 2).reshape(nb, IDX_HEADS, IDX_HEAD_DIM),
        wi_s.T.reshape(nb, IDX_HEADS, 1),
        ki_s.reshape(nb, 1, IDX_HEAD_DIM),
        jnp.swapaxes(cache_idx_k, 1, 2))
    width = scores.shape[2]
    sel = _sample_select(scores.reshape(nb, width), _upper_tri(LANES))
    attn_s = _sample_attention(
        page_table,
        qa_s.transpose(1, 0, 2),
        sel.reshape(nb, 1, width),
        ka_s.reshape(nb, 1, ATTN_HEAD_DIM),
        va_s.reshape(nb, 1, ATTN_HEAD_DIM),
        cache_k, cache_v)
    rg_s, ret_state_s = _sample_retention(main_s.reshape(nb, 32, LANES), state_ret, gam1_b,
                                          ns=TILES["sample_ret_rows"])
    x1_s, h2_s = _out_projection(attn_s.reshape(nb, half_mix), rg_s.reshape(nb, RET_WIDTH),
                                 wa, wr, xs, g2, tm=nb)
    y_s, wu, wd = _mlp_and_cast(h2_s, w_up, w_down, x1_s, gf, tf=TILES["mlp_cast_ff"])
    y_p = _mlp(h2_p, wu, wd, x1_p, gf, tm=TILES["mlp_rows"], tf=TILES["mlp_ff"])

    return (
        y_p.reshape(batch, seq, D_MODEL),
        y_s.reshape(nb, 1, D_MODEL),
        ka_p.reshape(batch, seq, ATTN_HEAD_DIM),
        va_p.reshape(batch, seq, ATTN_HEAD_DIM),
        ki_p.reshape(batch, seq, IDX_HEAD_DIM),
        ret_state_p,
        ka_s.reshape(nb, 1, ATTN_HEAD_DIM),
        va_s.reshape(nb, 1, ATTN_HEAD_DIM),
        ki_s.reshape(nb, 1, IDX_HEAD_DIM),
        ret_state_s,
    )
```

```python
import functools

import numpy as np
import jax
import jax.numpy as jnp
from jax import lax
from jax.experimental import pallas as pl
from jax.experimental.pallas import tpu as pltpu

F32 = jnp.float32
BF16 = jnp.bfloat16
I32 = jnp.int32

D_MODEL = 2048
ATTN_HEADS = 8
ATTN_HEAD_DIM = 128
IDX_HEADS = 16
IDX_HEAD_DIM = 64
TOPK_MAX = 256
RET_HEADS = 8
RET_DK = 128
RET_DV = 128
RET_CHUNK = 256
ROPE_BASE = 10000.0
D_FF = 4 * D_MODEL
EPS = 1e-6
Q_BLOCK = 256

OFF_QA, OFF_KA, OFF_VA, OFF_QI, OFF_KI, OFF_WI = 0, 1024, 1152, 1280, 2304, 2368
OFF_QR, OFF_KR, OFF_VR, OFF_GR = 2384, 3408, 4432, 5456
RET_WIDTH = RET_HEADS * RET_DV

LANES = 128
PROJ_TILE = 512
KEY_TILE = 256
COUNT_ROWS = 32
PAGE_GROUP = 2
IDX_PAGE_SLOTS = 8
ATTN_PAGE_SLOTS = 6
SUM_ROWS = 16
LOG2_E = 1.4426950408889634
INT_MIN = -2 ** 31
KEY_NEG_INF = -2 ** 31 + 0x7FFFFF
BF16_KEY_NEG_INF = -2 ** 15 + 0x7F
NEG_BIG = -1e30
VMEM_LIMIT = 56 * 1024 * 1024

TILES = {
    "proj_attn_rows": 1024,
    "proj_ret_rows": 512,
    "out_proj_rows": 512,
    "mlp_rows": 512,
    "mlp_ff": 1024,
    "sample_ret_rows": 8,
}


def _cparams(sem):
    return pltpu.CompilerParams(dimension_semantics=sem, vmem_limit_bytes=VMEM_LIMIT)


def _resident(shape):
    zeros = (0,) * len(shape)
    return pl.BlockSpec(shape, lambda *_: zeros, pipeline_mode=pl.Buffered(1))


def _normed_input(x_ref, g_ref, xn_ref):
    x = x_ref[...]
    ms = jnp.mean(x * x, axis=-1, keepdims=True)
    xn_ref[...] = (x * lax.rsqrt(ms + EPS) * g_ref[...]).astype(BF16)


def _matmul_rows(xn, wt_ref, r0, n):
    return lax.dot_general(xn, wt_ref[r0:r0 + n, :], (((1,), (1,)), ((), ())),
                           preferred_element_type=F32)


def _with_sample_rows(x_ref, xs_ref, g_ref, xn_ref, compute):
    tm, ns = x_ref.shape[0], xs_ref.shape[0]
    last = pl.program_id(0) == pl.num_programs(0) - 1

    @pl.when(last)
    def _prompt_and_sample_rows():
        _normed_input(x_ref, g_ref, xn_ref.at[pl.ds(0, tm)])
        _normed_input(xs_ref, g_ref, xn_ref.at[pl.ds(tm, ns)])
        compute(xn_ref[...], with_samples=True)

    @pl.when(jnp.logical_not(last))
    def _prompt_rows():
        _normed_input(x_ref, g_ref, xn_ref.at[pl.ds(0, tm)])
        compute(xn_ref[pl.ds(0, tm), :], with_samples=False)


def _proj_attn_body(x_ref, xs_ref, g_ref, wt_ref, lng_ref, lnb_ref, *refs):
    outs_p, outs_s, xn_ref = refs[0:7], refs[7:14], refs[14]
    tm, ns = x_ref.shape[0], xs_ref.shape[0]

    def compute(xn, with_samples):
        parts = [((0, tm), outs_p)] + ([((tm, tm + ns), outs_s)] if with_samples else [])
        mm = functools.partial(_matmul_rows, xn, wt_ref)
        for t in range(ATTN_HEADS * ATTN_HEAD_DIM // PROJ_TILE):
            acc = mm(OFF_QA + t * PROJ_TILE, PROJ_TILE)
            for hh in range(PROJ_TILE // LANES):
                piece = acc[:, hh * LANES:(hh + 1) * LANES].astype(BF16)
                for (a, b), o in parts:
                    o[0][4 * t + hh] = piece[a:b]
        for t in range(IDX_HEADS * IDX_HEAD_DIM // PROJ_TILE):
            acc = mm(OFF_QI + t * PROJ_TILE, PROJ_TILE)
            for hh in range(PROJ_TILE // LANES):
                piece = acc[:, hh * LANES:(hh + 1) * LANES].astype(BF16)
                for (a, b), o in parts:
                    o[1][4 * t + hh] = piece[a:b]
        kv = mm(OFF_KA, 2 * ATTN_HEAD_DIM)
        kw = mm(OFF_KI, LANES)
        lane = lax.broadcasted_iota(I32, kw.shape, 1)
        is_k = lane < IDX_HEAD_DIM
        mu = jnp.sum(jnp.where(is_k, kw, 0.0), axis=-1, keepdims=True) * (1.0 / IDX_HEAD_DIM)
        d = jnp.where(is_k, kw - mu, 0.0)
        var = jnp.sum(d * d, axis=-1, keepdims=True) * (1.0 / IDX_HEAD_DIM)
        kn = d * lax.rsqrt(var + EPS) * lng_ref[...] + lnb_ref[...]
        kd = jnp.where(is_k, kn, pltpu.roll(kn, IDX_HEAD_DIM, 1)).astype(BF16)
        wi = kw[:, IDX_HEAD_DIM:IDX_HEAD_DIM + IDX_HEADS] * (IDX_HEADS ** -0.5)
        for (a, b), o in parts:
            o[2][...] = kv[a:b, :ATTN_HEAD_DIM]
            o[3][...] = kv[a:b, ATTN_HEAD_DIM:]
            o[4][...] = kn[a:b, :IDX_HEAD_DIM]
            o[5][...] = kd[a:b]
            o[6][...] = wi[a:b].T

    _with_sample_rows(x_ref, xs_ref, g_ref, xn_ref, compute)


def _proj_ret_body(x_ref, xs_ref, g_ref, wt_ref, cs_ref, css_ref, main_ref, mains_ref, xn_ref):
    tm, ns = x_ref.shape[0], xs_ref.shape[0]

    def compute(xn, with_samples):
        parts = [((0, tm), main_ref)] + ([((tm, tm + ns), mains_ref)] if with_samples else [])
        cs = jnp.concatenate([cs_ref[...], css_ref[...]], axis=0) if with_samples else cs_ref[...]
        cosf = cs[:, :LANES]
        sinf = cs[:, LANES:]
        tiles = RET_WIDTH // PROJ_TILE
        for seg, (off, scale) in enumerate(((OFF_QR, None), (OFF_KR, RET_DK ** -0.5))):
            for t in range(tiles):
                acc = _matmul_rows(xn, wt_ref, off + t * PROJ_TILE, PROJ_TILE)
                for hh in range(PROJ_TILE // LANES):
                    xh = acc[:, hh * LANES:(hh + 1) * LANES]
                    r = xh * cosf + pltpu.roll(xh, RET_DK // 2, 1) * sinf
                    if scale is not None:
                        r = r * scale
                    c0 = seg * RET_WIDTH + t * PROJ_TILE + hh * LANES
                    r = r.astype(BF16)
                    for (a, b), o in parts:
                        o[:, c0:c0 + LANES] = r[a:b]
        for seg, off in ((2, OFF_VR), (3, OFF_GR)):
            for t in range(tiles):
                acc = _matmul_rows(xn, wt_ref, off + t * PROJ_TILE, PROJ_TILE).astype(BF16)
                c0 = seg * RET_WIDTH + t * PROJ_TILE
                for (a, b), o in parts:
                    o[:, c0:c0 + PROJ_TILE] = acc[a:b]

    _with_sample_rows(x_ref, xs_ref, g_ref, xn_ref, compute)


def _project_attn(x2d, xs2d, g1, wt, lng, lnb, tm):
    m, ns = x2d.shape[0], xs2d.shape[0]
    row = lambda i: (i, 0)

    def shapes(n):
        return (
            jax.ShapeDtypeStruct((ATTN_HEADS, n, ATTN_HEAD_DIM), BF16),
            jax.ShapeDtypeStruct((IDX_HEADS // 2, n, LANES), BF16),
            jax.ShapeDtypeStruct((n, ATTN_HEAD_DIM), F32),
            jax.ShapeDtypeStruct((n, ATTN_HEAD_DIM), F32),
            jax.ShapeDtypeStruct((n, IDX_HEAD_DIM), F32),
            jax.ShapeDtypeStruct((n, LANES), BF16),
            jax.ShapeDtypeStruct((IDX_HEADS, n), F32),
        )

    prompt_specs = (
        pl.BlockSpec((ATTN_HEADS, tm, ATTN_HEAD_DIM), lambda i: (0, i, 0)),
        pl.BlockSpec((IDX_HEADS // 2, tm, LANES), lambda i: (0, i, 0)),
        pl.BlockSpec((tm, ATTN_HEAD_DIM), row),
        pl.BlockSpec((tm, ATTN_HEAD_DIM), row),
        pl.BlockSpec((tm, IDX_HEAD_DIM), row),
        pl.BlockSpec((tm, LANES), row),
        pl.BlockSpec((IDX_HEADS, tm), lambda i: (0, i)),
    )
    sample_specs = tuple(pl.BlockSpec(s.shape, lambda i, nd=len(s.shape): (0,) * nd)
                         for s in shapes(ns))
    outs = pl.pallas_call(
        _proj_attn_body,
        grid=(m // tm,),
        in_specs=[pl.BlockSpec((tm, D_MODEL), row), _resident((ns, D_MODEL)),
                  _resident((1, D_MODEL)), _resident((OFF_QR + LANES, D_MODEL)),
                  _resident((1, LANES)), _resident((1, LANES))],
        out_specs=prompt_specs + sample_specs,
        out_shape=shapes(m) + shapes(ns),
        scratch_shapes=[pltpu.VMEM((tm + ns, D_MODEL), BF16)],
        compiler_params=_cparams(("arbitrary",)),
        name="proj_attn",
    )(x2d, xs2d, g1, wt, lng, lnb)
    return outs[:7], outs[7:]


def _project_ret(x2d, xs2d, g1, wt, cs, css, tm):
    m, ns = x2d.shape[0], xs2d.shape[0]
    n_pos_blocks = cs.shape[0] // tm
    row = lambda i: (i, 0)
    return pl.pallas_call(
        _proj_ret_body,
        grid=(m // tm,),
        in_specs=[pl.BlockSpec((tm, D_MODEL), row), _resident((ns, D_MODEL)),
                  _resident((1, D_MODEL)), _resident(wt.shape),
                  pl.BlockSpec((tm, 2 * LANES), lambda i: (i % n_pos_blocks, 0)),
                  _resident((ns, 2 * LANES))],
        out_specs=(pl.BlockSpec((tm, 4 * RET_WIDTH), row),
                   pl.BlockSpec((ns, 4 * RET_WIDTH), lambda i: (0, 0))),
        out_shape=(jax.ShapeDtypeStruct((m, 4 * RET_WIDTH), BF16),
                   jax.ShapeDtypeStruct((ns, 4 * RET_WIDTH), BF16)),
        scratch_shapes=[pltpu.VMEM((tm + ns, D_MODEL), BF16)],
        compiler_params=_cparams(("arbitrary",)),
        name="proj_ret",
    )(x2d, xs2d, g1, wt, cs, css)


def _key_to_float(key):
    bits = key ^ ((key >> 31) & 0x7FFFFFFF)
    return lax.bitcast_convert_type(bits, F32)


def _threshold_search(count_ge, n_iter, shape):
    def body(it, t):
        bit = lax.shift_left(jnp.int32(1), 31 - it)
        cand = t ^ bit
        cnt = count_ge(_key_to_float(cand))
        return jnp.where(cnt >= float(TOPK_MAX), cand, t)

    t = lax.fori_loop(0, n_iter, body, jnp.full(shape, INT_MIN, I32))
    return _key_to_float(jnp.maximum(t, KEY_NEG_INF))


def _bf16_key_to_f32_key(k16):
    return lax.shift_left(k16, 16) | jnp.where(k16 < 0, 0xFFFF, 0)


def _threshold_search_coarse_fine(count_ge_bf16, count_ge, run, shape):
    def coarse(it, u):
        cand = u | lax.shift_left(jnp.int32(1), 15 - it)
        c = _key_to_float(_bf16_key_to_f32_key(cand - 32768)).astype(BF16)
        return jnp.where(count_ge_bf16(c) >= float(TOPK_MAX), cand, u)

    u = lax.fori_loop(0, jnp.where(run, 16, 0), coarse, jnp.zeros(shape, I32))
    k1 = jnp.maximum(u - 32768, BF16_KEY_NEG_INF)
    lo = _bf16_key_to_f32_key(jnp.maximum(k1 - 1, -32768))
    hi = _bf16_key_to_f32_key(jnp.minimum(k1 + 1, 32767))

    def fine(it, t):
        cand = t + lax.shift_left(jnp.int32(1), 16 - it)
        ok = (cand < hi) & (count_ge(_key_to_float(cand)) >= float(TOPK_MAX))
        return jnp.where(ok, cand, t)

    t = lax.fori_loop(0, jnp.where(run, 17, 0), fine, lo)
    return _key_to_float(jnp.maximum(t, KEY_NEG_INF))


def _attn_body(qa_ref, qi_ref, wit_ref, ka_ref, va_ref, kd_ref, tri_ref, wu_ref, wd_ref,
               o_ref, wub_ref, wdb_ref, kbf, vtb, scr, mrun, acc_s, kmax, scr16):
    wub_ref[...] = wu_ref[...].astype(BF16)
    wdb_ref[...] = wd_ref[...].astype(BF16)

    qb = pl.program_id(1)
    n_heads_q = ATTN_HEADS * Q_BLOCK
    n_pairs = IDX_HEADS // 2
    dv = ATTN_HEAD_DIM
    logit_scale = ATTN_HEAD_DIM ** -0.5 * LOG2_E

    @pl.when(qb == 0)
    def _cast():
        ka = ka_ref[...]
        kbf[...] = ka.astype(BF16)
        kmax[...] = jnp.broadcast_to(jnp.max(jnp.sum(ka * ka, axis=1, keepdims=True)), kmax.shape)
        for kt in range(vtb.shape[0]):
            vtb[kt, :dv] = va_ref[kt * KEY_TILE:(kt + 1) * KEY_TILE, :].T.astype(BF16)
            vtb[kt, dv:] = jnp.ones((vtb.shape[1] - dv, KEY_TILE), BF16)

    nk = ((qb + 1) * Q_BLOCK + KEY_TILE - 1) // KEY_TILE
    wt = wit_ref[...] * (IDX_HEAD_DIM ** -0.5)
    qi2 = qi_ref[...].reshape(n_pairs * Q_BLOCK, LANES)
    lo_half = lax.broadcasted_iota(I32, (KEY_TILE, LANES), 1) < IDX_HEAD_DIM
    qidx = qb * Q_BLOCK + lax.broadcasted_iota(I32, (KEY_TILE, Q_BLOCK), 1)
    kidx0 = lax.broadcasted_iota(I32, (KEY_TILE, Q_BLOCK), 0)
    contract_last = (((1,), (1,)), ((), ()))

    def idx_body(kt, carry):
        off = pl.multiple_of(kt * KEY_TILE, KEY_TILE)
        kit = kd_ref[pl.ds(off, KEY_TILE), :]
        zero = jnp.zeros_like(kit)
        s_even = lax.dot_general(jnp.where(lo_half, kit, zero), qi2, contract_last,
                                 preferred_element_type=F32)
        s_odd = lax.dot_general(jnp.where(lo_half, zero, kit), qi2, contract_last,
                                preferred_element_type=F32)
        score = jnp.zeros((KEY_TILE, Q_BLOCK), F32)
        for g in range(n_pairs):
            cs = slice(g * Q_BLOCK, (g + 1) * Q_BLOCK)
            score = score + jnp.maximum(s_even[:, cs], 0.0) * wt[2 * g:2 * g + 1, :]
            score = score + jnp.maximum(s_odd[:, cs], 0.0) * wt[2 * g + 1:2 * g + 2, :]
        score = jnp.where(kidx0 + off <= qidx, score, -jnp.inf)
        scr[kt] = score
        scr16[kt] = score.astype(BF16)
        return carry

    def for_tiles(fn):
        def pair(j, carry):
            fn(2 * j, 0)
            fn(2 * j + 1, 0)
            return carry

        lax.fori_loop(0, nk // 2, pair, 0)

        @pl.when(nk % 2 == 1)
        def _last():
            fn(nk - 1, 0)

    for_tiles(idx_body)

    def count_ge_bf16(c):
        def body(kt, acc):
            hit = scr16[kt] >= c
            for r in range(KEY_TILE // COUNT_ROWS):
                acc = jnp.where(hit[r * COUNT_ROWS:(r + 1) * COUNT_ROWS], acc + 1.0, acc)
            return acc
        acc = lax.fori_loop(0, nk, body, jnp.zeros((COUNT_ROWS, Q_BLOCK), BF16))
        return jnp.sum(acc.astype(F32), axis=0, keepdims=True)

    def count_cmp(cmp):
        def body(kt, acc):
            hit = cmp(scr[kt])
            for r in range(KEY_TILE // COUNT_ROWS):
                acc = jnp.where(hit[r * COUNT_ROWS:(r + 1) * COUNT_ROWS], acc + 1.0, acc)
            return acc
        acc = lax.fori_loop(0, nk, body, jnp.zeros((COUNT_ROWS, Q_BLOCK), F32))
        return jnp.sum(acc, axis=0, keepdims=True)

    tf = _threshold_search_coarse_fine(
        count_ge_bf16, lambda c: count_cmp(lambda sc: sc >= c),
        qb >= TOPK_MAX // Q_BLOCK, (1, Q_BLOCK))
    excess = jnp.max(count_cmp(lambda sc: sc >= tf)) > float(TOPK_MAX)

    qa2 = qa_ref[...].reshape(n_heads_q, ATTN_HEAD_DIM)

    def logits(kt):
        off = pl.multiple_of(kt * KEY_TILE, KEY_TILE)
        s = lax.dot_general(kbf[pl.ds(off, KEY_TILE), :], qa2, contract_last,
                            preferred_element_type=F32)
        return s * logit_scale

    def sel_plain(kt, carry):
        off = pl.multiple_of(kt * KEY_TILE, KEY_TILE)
        return (scr[kt] >= tf) & (kidx0 + off <= qidx), carry

    def sel_ties(need, kt, tie_off):
        off = pl.multiple_of(kt * KEY_TILE, KEY_TILE)
        sc = scr[kt]
        eq = sc == tf
        tie = jnp.where(eq, 1.0, 0.0)
        rank = jnp.dot(tri_ref[...], tie.astype(BF16), preferred_element_type=F32) + tie_off
        sel = ((sc > tf) | (eq & (rank <= need))) & (kidx0 + off <= qidx)
        return sel, tie_off + jnp.sum(tie, axis=0, keepdims=True)

    no_ties = jnp.zeros((1, Q_BLOCK), F32)

    def softmax_sum(m, sel_fn):
        acc_s[...] = jnp.zeros(acc_s.shape, F32)

        def body(kt, carry):
            sel, carry = sel_fn(kt, carry)
            e = jnp.exp2(logits(kt) - m)
            parts = []
            for h in range(ATTN_HEADS):
                cs = slice(h * Q_BLOCK, (h + 1) * Q_BLOCK)
                parts.append(jnp.where(sel, e[:, cs], 0.0).astype(BF16))
            p = jnp.concatenate(parts, axis=1)
            acc_s[...] += jnp.dot(vtb[kt], p, preferred_element_type=F32)
            return carry

        if sel_fn is sel_plain:
            for_tiles(body)
        else:
            lax.fori_loop(0, nk, body, no_ties)

    def selected_max(sel_fn):
        mrun[...] = jnp.full(mrun.shape, NEG_BIG, F32)

        def body(kt, carry):
            sel, carry = sel_fn(kt, carry)
            s = logits(kt)
            for h in range(ATTN_HEADS):
                cs = slice(h * Q_BLOCK, (h + 1) * Q_BLOCK)
                sh = jnp.where(sel, s[:, cs], NEG_BIG)
                mrun[:, cs] = jnp.maximum(
                    mrun[:, cs], jnp.max(sh.reshape(KEY_TILE // 8, 8, Q_BLOCK), axis=0))
            return carry

        lax.fori_loop(0, nk, body, no_ties)
        return jnp.max(mrun[...], axis=0, keepdims=True)

    def fast_path():
        q2 = (qa2 * qa2).astype(BF16)
        qsq = lax.dot_general(jnp.ones((8, ATTN_HEAD_DIM), BF16), q2, contract_last,
                              preferred_element_type=F32)[0:1]
        softmax_sum(jnp.sqrt(qsq * kmax[0:1, 0:1]) * logit_scale, sel_plain)
        return (jnp.min(acc_s[dv:dv + 1, :]) > 0.0).astype(I32)

    done = lax.cond(excess, lambda: jnp.int32(0), fast_path) == 1

    @pl.when(jnp.logical_not(done) & excess)
    def _exact_with_ties():
        need = float(TOPK_MAX) - count_cmp(lambda sc: sc > tf)
        sel_fn = functools.partial(sel_ties, need)
        softmax_sum(selected_max(sel_fn), sel_fn)

    @pl.when(jnp.logical_not(done) & jnp.logical_not(excess))
    def _exact_without_ties():
        softmax_sum(selected_max(sel_plain), sel_plain)

    out = acc_s[:dv, :] / acc_s[dv:dv + 1, :]
    for h in range(ATTN_HEADS):
        oh = out[:, h * Q_BLOCK:(h + 1) * Q_BLOCK].T
        o_ref[:, h * ATTN_HEAD_DIM:(h + 1) * ATTN_HEAD_DIM] = oh.astype(BF16)


def _prompt_attention(qa_hm, qi_pm, wi_t, ka, va, kd, tri, w_up, w_down, batch, seq):
    nq = seq // Q_BLOCK
    nkt = seq // KEY_TILE
    m = batch * seq
    n_heads_q = ATTN_HEADS * Q_BLOCK
    assert seq // COUNT_ROWS <= 256
    n_steps = batch * nq
    up_rows, down_rows = w_up.shape[0] // n_steps, w_down.shape[0] // n_steps
    assert up_rows * n_steps == w_up.shape[0] and down_rows * n_steps == w_down.shape[0]
    assert up_rows % 16 == 0 and down_rows % 16 == 0
    slab = lambda b, q: (b * nq + q, 0)
    in_specs = [
        pl.BlockSpec((ATTN_HEADS, Q_BLOCK, ATTN_HEAD_DIM), lambda b, q: (0, b * nq + q, 0)),
        pl.BlockSpec((IDX_HEADS // 2, Q_BLOCK, LANES), lambda b, q: (0, b * nq + q, 0)),
        pl.BlockSpec((IDX_HEADS, Q_BLOCK), lambda b, q: (0, b * nq + q)),
        pl.BlockSpec((seq, ATTN_HEAD_DIM), lambda b, q: (b, 0)),
        pl.BlockSpec((seq, ATTN_HEAD_DIM), lambda b, q: (b, 0)),
        pl.BlockSpec((seq, LANES), lambda b, q: (b, 0)),
        pl.BlockSpec((KEY_TILE, KEY_TILE), lambda b, q: (0, 0)),
        pl.BlockSpec((up_rows, w_up.shape[1]), slab),
        pl.BlockSpec((down_rows, w_down.shape[1]), slab),
    ]
    return pl.pallas_call(
        _attn_body,
        grid=(batch, nq),
        in_specs=in_specs,
        out_specs=(
            pl.BlockSpec((Q_BLOCK, ATTN_HEADS * ATTN_HEAD_DIM), slab),
            pl.BlockSpec((up_rows, w_up.shape[1]), slab),
            pl.BlockSpec((down_rows, w_down.shape[1]), slab),
        ),
        out_shape=(
            jax.ShapeDtypeStruct((m, ATTN_HEADS * ATTN_HEAD_DIM), BF16),
            jax.ShapeDtypeStruct(w_up.shape, BF16),
            jax.ShapeDtypeStruct(w_down.shape, BF16),
        ),
        scratch_shapes=[
            pltpu.VMEM((seq, ATTN_HEAD_DIM), BF16),
            pltpu.VMEM((nkt, ATTN_HEAD_DIM + SUM_ROWS, KEY_TILE), BF16),
            pltpu.VMEM((nkt, KEY_TILE, Q_BLOCK), F32),
            pltpu.VMEM((8, n_heads_q), F32),
            pltpu.VMEM((ATTN_HEAD_DIM + SUM_ROWS, n_heads_q), F32),
            pltpu.VMEM((8, LANES), F32),
            pltpu.VMEM((nkt, KEY_TILE, Q_BLOCK), BF16),
        ],
        compiler_params=_cparams(("arbitrary", "arbitrary")),
        name="prompt_attn",
    )(qa_hm, qi_pm, wi_t, ka, va, kd, tri, w_up, w_down)


def _gate(o, g):
    rn = o * lax.rsqrt(jnp.mean(o * o, axis=-1, keepdims=True) + EPS)
    return rn * (g / (1.0 + jnp.exp(-g)))


def _ret_body(q_ref, k_ref, v_ref, g_ref, decay_ref, rsc_ref, zeta_ref, gpow_ref,
              rg_ref, st_ref):
    c = pl.program_id(1)

    @pl.when(c == 0)
    def _init():
        st_ref[...] = jnp.zeros(st_ref.shape, F32)

    for h in range(RET_HEADS):
        sl = slice(h * 128, (h + 1) * 128)
        q = q_ref[:, sl]
        k = k_ref[:, sl]
        v = v_ref[:, sl]
        r_old = st_ref[0, h]
        qk = lax.dot_general(q, k, (((1,), (1,)), ((), ())), preferred_element_type=F32)
        inner = jnp.dot((qk * decay_ref[h]).astype(BF16), v, preferred_element_type=F32)
        cross = jnp.dot(q, r_old.astype(BF16), preferred_element_type=F32) * rsc_ref[h]
        kz = (k.astype(F32) * zeta_ref[h]).astype(BF16)
        upd = lax.dot_general(kz, v, (((0,), (0,)), ((), ())), preferred_element_type=F32)
        st_ref[0, h] = r_old * gpow_ref[h] + upd
        rg_ref[:, sl] = _gate(inner + cross, g_ref[:, sl].astype(F32)).astype(BF16)


def _prompt_retention(main, decay, rsc, zeta, gpow, batch, seq):
    nc = seq // RET_CHUNK
    m = batch * seq
    width = RET_WIDTH
    const3 = lambda b, c: (0, 0, 0)
    in_specs = [
        pl.BlockSpec((RET_CHUNK, width), lambda b, c: (b * nc + c, 0)),
        pl.BlockSpec((RET_CHUNK, width), lambda b, c: (b * nc + c, 1)),
        pl.BlockSpec((RET_CHUNK, width), lambda b, c: (b * nc + c, 2)),
        pl.BlockSpec((RET_CHUNK, width), lambda b, c: (b * nc + c, 3)),
        pl.BlockSpec((RET_HEADS, RET_CHUNK, RET_CHUNK), const3),
        pl.BlockSpec((RET_HEADS, RET_CHUNK, RET_DV), const3),
        pl.BlockSpec((RET_HEADS, RET_CHUNK, RET_DK), const3),
        pl.BlockSpec((RET_HEADS, 1, RET_DV), const3),
    ]
    return pl.pallas_call(
        _ret_body,
        grid=(batch, nc),
        in_specs=in_specs,
        out_specs=(
            pl.BlockSpec((RET_CHUNK, width), lambda b, c: (b * nc + c, 0)),
            pl.BlockSpec((1, RET_HEADS, RET_DK, RET_DV), lambda b, c: (b, 0, 0, 0)),
        ),
        out_shape=(
            jax.ShapeDtypeStruct((m, width), BF16),
            jax.ShapeDtypeStruct((batch, RET_HEADS, RET_DK, RET_DV), F32),
        ),
        compiler_params=_cparams(("arbitrary", "arbitrary")),
        name="prompt_ret",
    )(main, main, main, main, decay, rsc, zeta, gpow)


def _outproj_body(a_ref, r_ref, wa_ref, wr_ref, x_ref, g2_ref, x1_ref, h2_ref):
    mixed = (jnp.dot(a_ref[...], wa_ref[...], preferred_element_type=F32)
             + jnp.dot(r_ref[...], wr_ref[...], preferred_element_type=F32))
    x1 = x_ref[...] + mixed
    x1_ref[...] = x1
    ms = jnp.mean(x1 * x1, axis=-1, keepdims=True)
    h2_ref[...] = (x1 * lax.rsqrt(ms + EPS) * g2_ref[...]).astype(BF16)


def _out_projection(attn_o, rg, wa, wr, x2d, g2, tm):
    m = x2d.shape[0]
    half = attn_o.shape[1]
    in_specs = [
        pl.BlockSpec((tm, half), lambda i: (i, 0)),
        pl.BlockSpec((tm, half), lambda i: (i, 0)),
        pl.BlockSpec((half, D_MODEL), lambda i: (0, 0)),
        pl.BlockSpec((half, D_MODEL), lambda i: (0, 0)),
        pl.BlockSpec((tm, D_MODEL), lambda i: (i, 0)),
        pl.BlockSpec((1, D_MODEL), lambda i: (0, 0)),
    ]
    return pl.pallas_call(
        _outproj_body,
        grid=(m // tm,),
        in_specs=in_specs,
        out_specs=(pl.BlockSpec((tm, D_MODEL), lambda i: (i, 0)),
                   pl.BlockSpec((tm, D_MODEL), lambda i: (i, 0))),
        out_shape=(jax.ShapeDtypeStruct((m, D_MODEL), F32),
                   jax.ShapeDtypeStruct((m, D_MODEL), BF16)),
        compiler_params=_cparams(("arbitrary",)),
        name="out_proj",
    )(attn_o, rg, wa, wr, x2d, g2)


def _mlp_body(h2_ref, wu_ref, wd_ref, x1_ref, gf_ref, y_ref, acc_ref):
    f = pl.program_id(1)

    @pl.when(f == 0)
    def _init():
        acc_ref[...] = x1_ref[...]

    u = jnp.dot(h2_ref[...], wu_ref[...], preferred_element_type=F32)
    a = jnp.maximum(u, 0.0)
    acc_ref[...] += jnp.dot((a * a).astype(BF16), wd_ref[...], preferred_element_type=F32)

    @pl.when(f == pl.num_programs(1) - 1)
    def _final():
        x2 = acc_ref[...]
        ms = jnp.mean(x2 * x2, axis=-1, keepdims=True)
        y_ref[...] = x2 * lax.rsqrt(ms + EPS) * gf_ref[...]


def _mlp(h2, wu, wd, x1, gf, tm, tf):
    m = h2.shape[0]
    in_specs = [
        pl.BlockSpec((tm, D_MODEL), lambda i, f: (i, 0)),
        pl.BlockSpec((D_MODEL, tf), lambda i, f: (0, f)),
        pl.BlockSpec((tf, D_MODEL), lambda i, f: (f, 0)),
        pl.BlockSpec((tm, D_MODEL), lambda i, f: (i, 0)),
        pl.BlockSpec((1, D_MODEL), lambda i, f: (0, 0)),
    ]
    return pl.pallas_call(
        _mlp_body,
        grid=(m // tm, D_FF // tf),
        in_specs=in_specs,
        out_specs=pl.BlockSpec((tm, D_MODEL), lambda i, f: (i, 0)),
        out_shape=jax.ShapeDtypeStruct((m, D_MODEL), F32),
        scratch_shapes=[pltpu.VMEM((tm, D_MODEL), F32)],
        compiler_params=_cparams(("arbitrary", "arbitrary")),
        name="mlp",
    )(h2, wu, wd, x1, gf)


def _fetch_pages(pt_ref, step, slot, streams, start):
    n_pages = pt_ref.shape[1]
    for hbm, buf, sem in streams:
        for j in range(n_pages):
            cp = pltpu.make_async_copy(hbm.at[pt_ref[step, j]], buf.at[slot, j], sem.at[slot])
            if start:
                cp.start()
            else:
                cp.wait()


def _paged_loop(pt_ref, streams, step_fn):
    nb = pt_ref.shape[0]
    n_slots = streams[0][1].shape[0]
    ahead = n_slots - PAGE_GROUP
    assert nb % PAGE_GROUP == 0 and ahead % PAGE_GROUP == 0 and PAGE_GROUP <= ahead <= nb
    for s in range(ahead):
        _fetch_pages(pt_ref, s, s, streams, start=True)

    def body(g, carry):
        b0 = g * PAGE_GROUP

        @pl.when(b0 + ahead < nb)
        def _next():
            for i in range(PAGE_GROUP):
                row = b0 + ahead + i
                _fetch_pages(pt_ref, row, row % n_slots, streams, start=True)

        for i in range(PAGE_GROUP):
            _fetch_pages(pt_ref, b0 + i, (b0 + i) % n_slots, streams, start=False)
        for i in range(PAGE_GROUP):
            step_fn(b0 + i, (b0 + i) % n_slots, i)
        return carry

    lax.fori_loop(0, nb // PAGE_GROUP, body, 0)


def _sidx_body(pt_ref, qi_ref, w_ref, kin_ref, cache_hbm, out_ref, kt_s, pbuf, sem):
    n_pages = pt_ref.shape[1]
    page = pbuf.shape[3]
    past = n_pages * page
    lane = lax.broadcasted_iota(I32, (1, LANES), 1)

    def step(b, slot, lane_of_trip):
        qi = qi_ref[b]
        w = w_ref[b] * (IDX_HEAD_DIM ** -0.5)
        for j in range(n_pages):
            kt_s[lane_of_trip, :, j * page:(j + 1) * page] = pbuf[slot, j].astype(BF16)
        s = jnp.dot(qi, kt_s[lane_of_trip], preferred_element_type=F32)
        out_ref[b, :, 0:past] = jnp.sum(jnp.maximum(s, 0.0) * w, axis=0, keepdims=True)
        sn = jnp.sum(qi.astype(F32) * kin_ref[b].astype(BF16).astype(F32), axis=1, keepdims=True)
        rn = jnp.sum(jnp.maximum(sn, 0.0) * w, axis=0, keepdims=True)
        out_ref[b, :, past:past + LANES] = jnp.where(lane == 0, rn, -jnp.inf)

    _paged_loop(pt_ref, ((cache_hbm, pbuf, sem),), step)


def _sample_index_scores(page_table, qi_s, wi_s, ki_s, cache_idx_k_t):
    nb, n_pages = page_table.shape
    page = cache_idx_k_t.shape[2]
    width = n_pages * page + LANES

    vmem = pl.BlockSpec(memory_space=pltpu.VMEM)
    return pl.pallas_call(
        _sidx_body,
        in_specs=[pl.BlockSpec(memory_space=pltpu.SMEM), vmem, vmem, vmem,
                  pl.BlockSpec(memory_space=pl.ANY)],
        out_specs=vmem,
        out_shape=jax.ShapeDtypeStruct((nb, 1, width), F32),
        scratch_shapes=[pltpu.VMEM((PAGE_GROUP, IDX_HEAD_DIM, n_pages * page), BF16),
                        pltpu.VMEM((IDX_PAGE_SLOTS, n_pages, IDX_HEAD_DIM, page), F32),
                        pltpu.SemaphoreType.DMA((IDX_PAGE_SLOTS,))],
        compiler_params=pltpu.CompilerParams(vmem_limit_bytes=VMEM_LIMIT),
        name="sample_idx",
    )(page_table, qi_s, wi_s, ki_s, cache_idx_k_t)


def _ssel_body(sc_ref, tri_ref, sel_ref):
    rows, width = sc_ref.shape
    nt = width // LANES
    n_valid = (nt - 1) * LANES + 1

    def tile(kt):
        return sc_ref[:, kt * LANES:(kt + 1) * LANES]

    def count_cmp(cmp):
        acc = jnp.zeros((rows, LANES), F32)
        for kt in range(nt):
            acc = acc + jnp.where(cmp(tile(kt)), 1.0, 0.0)
        return jnp.broadcast_to(jnp.sum(acc, axis=1, keepdims=True), (rows, LANES))

    tf = _threshold_search(lambda c: count_cmp(lambda sc: sc >= c), 32, (rows, LANES))
    need = float(TOPK_MAX) - count_cmp(lambda sc: sc > tf)
    tie_off = jnp.zeros((rows, LANES), F32)
    for kt in range(nt):
        col = kt * LANES + lax.broadcasted_iota(I32, (rows, LANES), 1)
        sc = tile(kt)
        eq = sc == tf
        tie = jnp.where(eq, 1.0, 0.0)
        rank = jnp.dot(tie.astype(BF16), tri_ref[...], preferred_element_type=F32) + tie_off
        sel = ((sc > tf) | (eq & (rank <= need))) & (col < n_valid)
        sel_ref[:, kt * LANES:(kt + 1) * LANES] = jnp.where(sel, 1.0, 0.0)
        tie_off = tie_off + jnp.broadcast_to(jnp.sum(tie, axis=1, keepdims=True), (rows, LANES))


def _sample_select(scores2d, tri):
    rows, width = scores2d.shape
    return pl.pallas_call(
        _ssel_body,
        out_shape=jax.ShapeDtypeStruct((rows, width), F32),
        compiler_params=pltpu.CompilerParams(vmem_limit_bytes=VMEM_LIMIT),
        name="sample_select",
    )(scores2d, tri)


def _sattn_body(pt_ref, q_ref, sel_ref, kn_ref, vn_ref, ck_hbm, cv_hbm, o_ref,
                kbuf, vbuf, ksem, vsem):
    n_pages = pt_ref.shape[1]
    page = kbuf.shape[2]
    past = n_pages * page
    scale = ATTN_HEAD_DIM ** -0.5

    def step(b, slot, lane_of_trip):
        q = q_ref[b]
        k_all = kbuf[slot].reshape(past, ATTN_HEAD_DIM).astype(BF16)
        v_all = vbuf[slot].reshape(past, ATTN_HEAD_DIM).astype(BF16)
        s = lax.dot_general(q, k_all, (((1,), (1,)), ((), ())), preferred_element_type=F32)
        s = jnp.where(sel_ref[b, :, 0:past] > 0.5, s * scale, NEG_BIG)
        kn = kn_ref[b].astype(BF16).astype(F32)
        sn = jnp.sum(q.astype(F32) * kn, axis=1, keepdims=True) * scale
        sn = jnp.where(sel_ref[b, :, past:past + 1] > 0.5, sn, NEG_BIG)
        m = jnp.maximum(jnp.max(s, axis=1, keepdims=True), sn)
        pn = jnp.exp(sn - m)
        p = jnp.exp(s - m)
        l = pn + jnp.sum(p, axis=1, keepdims=True)
        acc = (pn * vn_ref[b].astype(BF16).astype(F32)
               + jnp.dot(p.astype(BF16), v_all, preferred_element_type=F32))
        o_ref[b] = (acc / l).astype(BF16)

    _paged_loop(pt_ref, ((ck_hbm, kbuf, ksem), (cv_hbm, vbuf, vsem)), step)


def _sample_attention(page_table, qa_s, sel3, ka_s, va_s, cache_k, cache_v):
    nb, n_pages = page_table.shape
    page = cache_k.shape[1]

    vmem = pl.BlockSpec(memory_space=pltpu.VMEM)
    hbm = pl.BlockSpec(memory_space=pl.ANY)
    return pl.pallas_call(
        _sattn_body,
        in_specs=[pl.BlockSpec(memory_space=pltpu.SMEM), vmem, vmem, vmem, vmem, hbm, hbm],
        out_specs=vmem,
        out_shape=jax.ShapeDtypeStruct((nb, ATTN_HEADS, ATTN_HEAD_DIM), BF16),
        scratch_shapes=[pltpu.VMEM((ATTN_PAGE_SLOTS, n_pages, page, ATTN_HEAD_DIM), F32),
                        pltpu.VMEM((ATTN_PAGE_SLOTS, n_pages, page, ATTN_HEAD_DIM), F32),
                        pltpu.SemaphoreType.DMA((ATTN_PAGE_SLOTS,)),
                        pltpu.SemaphoreType.DMA((ATTN_PAGE_SLOTS,))],
        compiler_params=pltpu.CompilerParams(vmem_limit_bytes=VMEM_LIMIT),
        name="sample_attn",
    )(page_table, qa_s, sel3, ka_s, va_s, cache_k, cache_v)


def _sret_body(qkvg_ref, st_ref, gam_ref, rg_ref, so_ref):
    ns = st_ref.shape[0]
    for s in range(ns):
        blk = qkvg_ref[s].astype(F32)
        q8 = blk[0:8]
        k8 = blk[8:16]
        v8 = blk[16:24]
        g8 = blk[24:32]
        q_t = q8.T
        k_t = k8.T
        qk = jnp.sum(q8 * k8, axis=1, keepdims=True)
        rows = []
        for h in range(RET_HEADS):
            r_old = st_ref[s, h]
            gam = gam_ref[h]
            qcol = jnp.broadcast_to(q_t[:, h:h + 1], (RET_DK, RET_DV))
            kcol = jnp.broadcast_to(k_t[:, h:h + 1], (RET_DK, RET_DV))
            vrow = v8[h:h + 1]
            q_r = jnp.sum(qcol * r_old, axis=0, keepdims=True)
            rows.append(gam * q_r + qk[h:h + 1] * vrow)
            so_ref[s, h] = gam * r_old + kcol * vrow
        ret = jnp.concatenate(rows, axis=0)
        rg_ref[s] = _gate(ret, g8).astype(BF16)


def _sample_retention(qkvg, state, gam, ns):
    nb = state.shape[0]
    return pl.pallas_call(
        _sret_body,
        grid=(nb // ns,),
        in_specs=[
            pl.BlockSpec((ns, 32, LANES), lambda i: (i, 0, 0)),
            pl.BlockSpec((ns, RET_HEADS, RET_DK, RET_DV), lambda i: (i, 0, 0, 0)),
            pl.BlockSpec((RET_HEADS, 1, LANES), lambda i: (0, 0, 0)),
        ],
        out_specs=(
            pl.BlockSpec((ns, RET_HEADS, RET_DV), lambda i: (i, 0, 0)),
            pl.BlockSpec((ns, RET_HEADS, RET_DK, RET_DV), lambda i: (i, 0, 0, 0)),
        ),
        out_shape=(
            jax.ShapeDtypeStruct((nb, RET_HEADS, RET_DV), BF16),
            jax.ShapeDtypeStruct(state.shape, F32),
        ),
        compiler_params=_cparams(("arbitrary",)),
        name="sample_ret",
    )(qkvg, state, gam)


def _rotary_table(pos):
    half = RET_DK // 2
    inv = ROPE_BASE ** (-np.arange(half, dtype=np.float64) / half)
    ang = np.asarray(pos, np.float64)[:, None] * inv[None, :]
    cos = np.cos(ang)
    sin = np.sin(ang)
    return jnp.asarray(np.concatenate([cos, cos, -sin, sin], axis=1), F32)


def _retention_constants():
    lg = np.log1p(-np.exp2(-5.0 - np.arange(RET_HEADS, dtype=np.float64)))
    n = RET_CHUNK
    i = np.arange(n, dtype=np.float64)
    diff = i[:, None] - i[None, :]
    decay = np.where(diff[None] >= 0, np.exp(np.maximum(diff, 0.0)[None] * lg[:, None, None]), 0.0)
    rsc = np.exp((i + 1.0)[None, :] * lg[:, None])
    zeta = np.exp((n - 1.0 - i)[None, :] * lg[:, None])
    gpow = np.exp(n * lg)
    gam1 = np.exp(lg)
    rsc_b = np.broadcast_to(rsc[:, :, None], (RET_HEADS, n, RET_DV))
    zeta_b = np.broadcast_to(zeta[:, :, None], (RET_HEADS, n, RET_DK))
    gpow_b = np.broadcast_to(gpow[:, None, None], (RET_HEADS, 1, RET_DV))
    gam1_b = np.broadcast_to(gam1[:, None, None], (RET_HEADS, 1, LANES))
    return tuple(jnp.asarray(a, F32) for a in (decay, rsc_b, zeta_b, gpow_b, gam1_b))


def _upper_tri(n):
    return jnp.asarray(np.triu(np.ones((n, n), np.float32)), BF16)


def _lower_tri(n):
    return jnp.asarray(np.tril(np.ones((n, n), np.float32)), BF16)


def _pad_lanes(v):
    return jnp.pad(v, (0, LANES - v.shape[0])).reshape(1, LANES)


def kernel(x_prompt, x_sample, cache_k, cache_v, cache_idx_k, state_ret, page_table,
           norm1_g, w_in, idx_k_norm_g, idx_k_norm_b, w_out, norm2_g, w_up, w_down, final_norm_g):
    batch, seq, _ = x_prompt.shape
    nb = x_sample.shape[0]
    past_len = page_table.shape[1] * cache_k.shape[1]
    half_mix = ATTN_HEADS * ATTN_HEAD_DIM

    wt = w_in.T.astype(BF16)
    wa = w_out[:half_mix].astype(BF16)
    wr = w_out[half_mix:].astype(BF16)
    g1 = norm1_g.reshape(1, D_MODEL)
    g2 = norm2_g.reshape(1, D_MODEL)
    gf = final_norm_g.reshape(1, D_MODEL)
    lng = _pad_lanes(idx_k_norm_g)
    lnb = _pad_lanes(idx_k_norm_b)
    decay, rsc_b, zeta_b, gpow_b, gam1_b = _retention_constants()

    xp = x_prompt.reshape(batch * seq, D_MODEL)
    xs = x_sample.reshape(nb, D_MODEL)
    cs_p = _rotary_table(np.arange(seq))
    cs_s = _rotary_table(np.full((nb,), past_len))
    (qa_p, qi_p, ka_p, va_p, ki_p, kd_p, wi_p), (qa_s, qi_s, ka_s, va_s, ki_s, _, wi_s) = (
        _project_attn(xp, xs, g1, wt, lng, lnb, tm=TILES["proj_attn_rows"]))
    main_p, main_s = _project_ret(xp, xs, g1, wt, cs_p, cs_s, tm=TILES["proj_ret_rows"])

    attn_p, wu, wd = _prompt_attention(qa_p, qi_p, wi_p, ka_p, va_p, kd_p, _lower_tri(KEY_TILE),
                                       w_up, w_down, batch, seq)
    rg_p, ret_state_p = _prompt_retention(main_p, decay, rsc_b, zeta_b, gpow_b, batch, seq)
    x1_p, h2_p = _out_projection(attn_p, rg_p, wa, wr, xp, g2, tm=TILES["out_proj_rows"])

    scores = _sample_index_scores(
        page_table,
        qi_s.transpose(1, 0, 2).reshape(nb, IDX_HEADS, IDX_HEAD_DIM),
        wi_s.T.reshape(nb, IDX_HEADS, 1),
        ki_s.reshape(nb, 1, IDX_HEAD_DIM),
        jnp.swapaxes(cache_idx_k, 1, 2))
    width = scores.shape[2]
    sel = _sample_select(scores.reshape(nb, width), _upper_tri(LANES))
    attn_s = _sample_attention(
        page_table,
        qa_s.transpose(1, 0, 2),
        sel.reshape(nb, 1, width),
        ka_s.reshape(nb, 1, ATTN_HEAD_DIM),
        va_s.reshape(nb, 1, ATTN_HEAD_DIM),
        cache_k, cache_v)
    rg_s, ret_state_s = _sample_retention(main_s.reshape(nb, 32, LANES), state_ret, gam1_b,
                                          ns=TILES["sample_ret_rows"])
    x1_s, h2_s = _out_projection(attn_s.reshape(nb, half_mix), rg_s.reshape(nb, RET_WIDTH),
                                 wa, wr, xs, g2, tm=nb)
    y_s = _mlp(h2_s, wu, wd, x1_s, gf, tm=nb, tf=TILES["mlp_ff"])
    y_p = _mlp(h2_p, wu, wd, x1_p, gf, tm=TILES["mlp_rows"], tf=TILES["mlp_ff"])

    return (
        y_p.reshape(batch, seq, D_MODEL),
        y_s.reshape(nb, 1, D_MODEL),
        ka_p.reshape(batch, seq, ATTN_HEAD_DIM),
        va_p.reshape(batch, seq, ATTN_HEAD_DIM),
        ki_p.reshape(batch, seq, IDX_HEAD_DIM),
        ret_state_p,
        ka_s.reshape(nb, 1, ATTN_HEAD_DIM),
        va_s.reshape(nb, 1, ATTN_HEAD_DIM),
        ki_s.reshape(nb, 1, IDX_HEAD_DIM),
        ret_state_s,
    )
```

```python
import functools

import numpy as np
import jax
import jax.numpy as jnp
from jax import lax
from jax.experimental import pallas as pl
from jax.experimental.pallas import tpu as pltpu

F32 = jnp.float32
BF16 = jnp.bfloat16
I32 = jnp.int32

D_MODEL = 2048
ATTN_HEADS = 8
ATTN_HEAD_DIM = 128
IDX_HEADS = 16
IDX_HEAD_DIM = 64
TOPK_MAX = 256
RET_HEADS = 8
RET_DK = 128
RET_DV = 128
RET_CHUNK = 256
ROPE_BASE = 10000.0
D_FF = 4 * D_MODEL
EPS = 1e-6
Q_BLOCK = 256

OFF_QA, OFF_KA, OFF_VA, OFF_QI, OFF_KI, OFF_WI = 0, 1024, 1152, 1280, 2304, 2368
OFF_QR, OFF_KR, OFF_VR, OFF_GR = 2384, 3408, 4432, 5456
RET_WIDTH = RET_HEADS * RET_DV

LANES = 128
PROJ_TILE = 512
KEY_TILE = 256
COUNT_ROWS = 32
PAGE_GROUP = 2
IDX_PAGE_SLOTS = 8
ATTN_PAGE_SLOTS = 6
SUM_ROWS = 16
LOG2_E = 1.4426950408889634
INT_MIN = -2 ** 31
KEY_NEG_INF = -2 ** 31 + 0x7FFFFF
BF16_KEY_NEG_INF = -2 ** 15 + 0x7F
NEG_BIG = -1e30
VMEM_LIMIT = 56 * 1024 * 1024

TILES = {
    "proj_attn_rows": 1024,
    "proj_ret_rows": 512,
    "out_proj_rows": 512,
    "mlp_rows": 512,
    "mlp_ff": 1024,
    "sample_ret_rows": 8,
}


def _cparams(sem):
    return pltpu.CompilerParams(dimension_semantics=sem, vmem_limit_bytes=VMEM_LIMIT)


def _resident(shape):
    zeros = (0,) * len(shape)
    return pl.BlockSpec(shape, lambda *_: zeros, pipeline_mode=pl.Buffered(1))


def _normed_input(x_ref, g_ref, xn_ref):
    x = x_ref[...]
    ms = jnp.mean(x * x, axis=-1, keepdims=True)
    xn_ref[...] = (x * lax.rsqrt(ms + EPS) * g_ref[...]).astype(BF16)


def _matmul_rows(xn, wt_ref, r0, n):
    return lax.dot_general(xn, wt_ref[r0:r0 + n, :], (((1,), (1,)), ((), ())),
                           preferred_element_type=F32)


def _with_sample_rows(x_ref, xs_ref, g_ref, xn_ref, compute):
    tm, ns = x_ref.shape[0], xs_ref.shape[0]
    last = pl.program_id(0) == pl.num_programs(0) - 1

    @pl.when(last)
    def _prompt_and_sample_rows():
        _normed_input(x_ref, g_ref, xn_ref.at[pl.ds(0, tm)])
        _normed_input(xs_ref, g_ref, xn_ref.at[pl.ds(tm, ns)])
        compute(xn_ref[...], with_samples=True)

    @pl.when(jnp.logical_not(last))
    def _prompt_rows():
        _normed_input(x_ref, g_ref, xn_ref.at[pl.ds(0, tm)])
        compute(xn_ref[pl.ds(0, tm), :], with_samples=False)


def _proj_attn_body(x_ref, xs_ref, g_ref, wt_ref, lng_ref, lnb_ref, *refs):
    outs_p, outs_s, xn_ref = refs[0:7], refs[7:14], refs[14]
    tm, ns = x_ref.shape[0], xs_ref.shape[0]

    def compute(xn, with_samples):
        parts = [((0, tm), outs_p)] + ([((tm, tm + ns), outs_s)] if with_samples else [])
        mm = functools.partial(_matmul_rows, xn, wt_ref)
        for t in range(ATTN_HEADS * ATTN_HEAD_DIM // PROJ_TILE):
            acc = mm(OFF_QA + t * PROJ_TILE, PROJ_TILE)
            for hh in range(PROJ_TILE // LANES):
                piece = acc[:, hh * LANES:(hh + 1) * LANES].astype(BF16)
                for (a, b), o in parts:
                    o[0][4 * t + hh] = piece[a:b]
        for t in range(IDX_HEADS * IDX_HEAD_DIM // PROJ_TILE):
            acc = mm(OFF_QI + t * PROJ_TILE, PROJ_TILE)
            for hh in range(PROJ_TILE // LANES):
                piece = acc[:, hh * LANES:(hh + 1) * LANES].astype(BF16)
                for (a, b), o in parts:
                    o[1][4 * t + hh] = piece[a:b]
        kv = mm(OFF_KA, 2 * ATTN_HEAD_DIM)
        kw = mm(OFF_KI, LANES)
        lane = lax.broadcasted_iota(I32, kw.shape, 1)
        is_k = lane < IDX_HEAD_DIM
        mu = jnp.sum(jnp.where(is_k, kw, 0.0), axis=-1, keepdims=True) * (1.0 / IDX_HEAD_DIM)
        d = jnp.where(is_k, kw - mu, 0.0)
        var = jnp.sum(d * d, axis=-1, keepdims=True) * (1.0 / IDX_HEAD_DIM)
        kn = d * lax.rsqrt(var + EPS) * lng_ref[...] + lnb_ref[...]
        kd = jnp.where(is_k, kn, pltpu.roll(kn, IDX_HEAD_DIM, 1)).astype(BF16)
        wi = kw[:, IDX_HEAD_DIM:IDX_HEAD_DIM + IDX_HEADS] * (IDX_HEADS ** -0.5)
        for (a, b), o in parts:
            o[2][...] = kv[a:b, :ATTN_HEAD_DIM]
            o[3][...] = kv[a:b, ATTN_HEAD_DIM:]
            o[4][...] = kn[a:b, :IDX_HEAD_DIM]
            o[5][...] = kd[a:b]
            o[6][...] = wi[a:b].T

    _with_sample_rows(x_ref, xs_ref, g_ref, xn_ref, compute)


def _proj_ret_body(x_ref, xs_ref, g_ref, wt_ref, cs_ref, css_ref, main_ref, mains_ref, xn_ref):
    tm, ns = x_ref.shape[0], xs_ref.shape[0]

    def compute(xn, with_samples):
        parts = [((0, tm), main_ref)] + ([((tm, tm + ns), mains_ref)] if with_samples else [])
        cs = jnp.concatenate([cs_ref[...], css_ref[...]], axis=0) if with_samples else cs_ref[...]
        cosf = cs[:, :LANES]
        sinf = cs[:, LANES:]
        tiles = RET_WIDTH // PROJ_TILE
        for seg, (off, scale) in enumerate(((OFF_QR, None), (OFF_KR, RET_DK ** -0.5))):
            for t in range(tiles):
                acc = _matmul_rows(xn, wt_ref, off + t * PROJ_TILE, PROJ_TILE)
                for hh in range(PROJ_TILE // LANES):
                    xh = acc[:, hh * LANES:(hh + 1) * LANES]
                    r = xh * cosf + pltpu.roll(xh, RET_DK // 2, 1) * sinf
                    if scale is not None:
                        r = r * scale
                    c0 = seg * RET_WIDTH + t * PROJ_TILE + hh * LANES
                    r = r.astype(BF16)
                    for (a, b), o in parts:
                        o[:, c0:c0 + LANES] = r[a:b]
        for seg, off in ((2, OFF_VR), (3, OFF_GR)):
            for t in range(tiles):
                acc = _matmul_rows(xn, wt_ref, off + t * PROJ_TILE, PROJ_TILE).astype(BF16)
                c0 = seg * RET_WIDTH + t * PROJ_TILE
                for (a, b), o in parts:
                    o[:, c0:c0 + PROJ_TILE] = acc[a:b]

    _with_sample_rows(x_ref, xs_ref, g_ref, xn_ref, compute)


def _project_attn(x2d, xs2d, g1, wt, lng, lnb, tm):
    m, ns = x2d.shape[0], xs2d.shape[0]
    row = lambda i: (i, 0)

    def shapes(n):
        return (
            jax.ShapeDtypeStruct((ATTN_HEADS, n, ATTN_HEAD_DIM), BF16),
            jax.ShapeDtypeStruct((IDX_HEADS // 2, n, LANES), BF16),
            jax.ShapeDtypeStruct((n, ATTN_HEAD_DIM), F32),
            jax.ShapeDtypeStruct((n, ATTN_HEAD_DIM), F32),
            jax.ShapeDtypeStruct((n, IDX_HEAD_DIM), F32),
            jax.ShapeDtypeStruct((n, LANES), BF16),
            jax.ShapeDtypeStruct((IDX_HEADS, n), F32),
        )

    prompt_specs = (
        pl.BlockSpec((ATTN_HEADS, tm, ATTN_HEAD_DIM), lambda i: (0, i, 0)),
        pl.BlockSpec((IDX_HEADS // 2, tm, LANES), lambda i: (0, i, 0)),
        pl.BlockSpec((tm, ATTN_HEAD_DIM), row),
        pl.BlockSpec((tm, ATTN_HEAD_DIM), row),
        pl.BlockSpec((tm, IDX_HEAD_DIM), row),
        pl.BlockSpec((tm, LANES), row),
        pl.BlockSpec((IDX_HEADS, tm), lambda i: (0, i)),
    )
    sample_specs = tuple(pl.BlockSpec(s.shape, lambda i, nd=len(s.shape): (0,) * nd)
                         for s in shapes(ns))
    outs = pl.pallas_call(
        _proj_attn_body,
        grid=(m // tm,),
        in_specs=[pl.BlockSpec((tm, D_MODEL), row), _resident((ns, D_MODEL)),
                  _resident((1, D_MODEL)), _resident((OFF_QR + LANES, D_MODEL)),
                  _resident((1, LANES)), _resident((1, LANES))],
        out_specs=prompt_specs + sample_specs,
        out_shape=shapes(m) + shapes(ns),
        scratch_shapes=[pltpu.VMEM((tm + ns, D_MODEL), BF16)],
        compiler_params=_cparams(("arbitrary",)),
        name="proj_attn",
    )(x2d, xs2d, g1, wt, lng, lnb)
    return outs[:7], outs[7:]


def _project_ret(x2d, xs2d, g1, wt, cs, css, tm):
    m, ns = x2d.shape[0], xs2d.shape[0]
    n_pos_blocks = cs.shape[0] // tm
    row = lambda i: (i, 0)
    return pl.pallas_call(
        _proj_ret_body,
        grid=(m // tm,),
        in_specs=[pl.BlockSpec((tm, D_MODEL), row), _resident((ns, D_MODEL)),
                  _resident((1, D_MODEL)), _resident(wt.shape),
                  pl.BlockSpec((tm, 2 * LANES), lambda i: (i % n_pos_blocks, 0)),
                  _resident((ns, 2 * LANES))],
        out_specs=(pl.BlockSpec((tm, 4 * RET_WIDTH), row),
                   pl.BlockSpec((ns, 4 * RET_WIDTH), lambda i: (0, 0))),
        out_shape=(jax.ShapeDtypeStruct((m, 4 * RET_WIDTH), BF16),
                   jax.ShapeDtypeStruct((ns, 4 * RET_WIDTH), BF16)),
        scratch_shapes=[pltpu.VMEM((tm + ns, D_MODEL), BF16)],
        compiler_params=_cparams(("arbitrary",)),
        name="proj_ret",
    )(x2d, xs2d, g1, wt, cs, css)


def _key_to_float(key):
    bits = key ^ ((key >> 31) & 0x7FFFFFFF)
    return lax.bitcast_convert_type(bits, F32)


def _threshold_search(count_ge, n_iter, shape):
    def body(it, t):
        bit = lax.shift_left(jnp.int32(1), 31 - it)
        cand = t ^ bit
        cnt = count_ge(_key_to_float(cand))
        return jnp.where(cnt >= float(TOPK_MAX), cand, t)

    t = lax.fori_loop(0, n_iter, body, jnp.full(shape, INT_MIN, I32))
    return _key_to_float(jnp.maximum(t, KEY_NEG_INF))


def _bf16_key_to_f32_key(k16):
    return lax.shift_left(k16, 16) | jnp.where(k16 < 0, 0xFFFF, 0)


def _threshold_search_coarse_fine(count_ge_bf16, count_ge, run, shape):
    def coarse(it, u):
        cand = u | lax.shift_left(jnp.int32(1), 15 - it)
        c = _key_to_float(_bf16_key_to_f32_key(cand - 32768)).astype(BF16)
        return jnp.where(count_ge_bf16(c) >= float(TOPK_MAX), cand, u)

    u = lax.fori_loop(0, jnp.where(run, 16, 0), coarse, jnp.zeros(shape, I32))
    k1 = jnp.maximum(u - 32768, BF16_KEY_NEG_INF)
    lo = _bf16_key_to_f32_key(jnp.maximum(k1 - 1, -32768))
    hi = _bf16_key_to_f32_key(jnp.minimum(k1 + 1, 32767))

    def fine(it, t):
        cand = t + lax.shift_left(jnp.int32(1), 16 - it)
        ok = (cand < hi) & (count_ge(_key_to_float(cand)) >= float(TOPK_MAX))
        return jnp.where(ok, cand, t)

    t = lax.fori_loop(0, jnp.where(run, 17, 0), fine, lo)
    return _key_to_float(jnp.maximum(t, KEY_NEG_INF))


def _attn_body(qa_ref, qi_ref, wit_ref, ka_ref, va_ref, kd_ref, tri_ref, wu_ref, wd_ref,
               o_ref, wub_ref, wdb_ref, kbf, vtb, scr, mrun, acc_s, kmax, scr16):
    wub_ref[...] = wu_ref[...].astype(BF16)
    wdb_ref[...] = wd_ref[...].astype(BF16)

    qb = pl.program_id(1)
    n_heads_q = ATTN_HEADS * Q_BLOCK
    n_pairs = IDX_HEADS // 2
    dv = ATTN_HEAD_DIM
    logit_scale = ATTN_HEAD_DIM ** -0.5 * LOG2_E

    @pl.when(qb == 0)
    def _cast():
        ka = ka_ref[...]
        kbf[...] = ka.astype(BF16)
        kmax[...] = jnp.broadcast_to(jnp.max(jnp.sum(ka * ka, axis=1, keepdims=True)), kmax.shape)
        for kt in range(vtb.shape[0]):
            vtb[kt, :dv] = va_ref[kt * KEY_TILE:(kt + 1) * KEY_TILE, :].T.astype(BF16)
            vtb[kt, dv:] = jnp.ones((vtb.shape[1] - dv, KEY_TILE), BF16)

    nk = ((qb + 1) * Q_BLOCK + KEY_TILE - 1) // KEY_TILE
    wt = wit_ref[...] * (IDX_HEAD_DIM ** -0.5)
    qi2 = qi_ref[...].reshape(n_pairs * Q_BLOCK, LANES)
    lo_half = lax.broadcasted_iota(I32, (KEY_TILE, LANES), 1) < IDX_HEAD_DIM
    qidx = qb * Q_BLOCK + lax.broadcasted_iota(I32, (KEY_TILE, Q_BLOCK), 1)
    kidx0 = lax.broadcasted_iota(I32, (KEY_TILE, Q_BLOCK), 0)
    contract_last = (((1,), (1,)), ((), ()))

    def idx_body(kt, carry):
        off = pl.multiple_of(kt * KEY_TILE, KEY_TILE)
        kit = kd_ref[pl.ds(off, KEY_TILE), :]
        zero = jnp.zeros_like(kit)
        s_even = lax.dot_general(jnp.where(lo_half, kit, zero), qi2, contract_last,
                                 preferred_element_type=F32)
        s_odd = lax.dot_general(jnp.where(lo_half, zero, kit), qi2, contract_last,
                                preferred_element_type=F32)
        score = jnp.zeros((KEY_TILE, Q_BLOCK), F32)
        for g in range(n_pairs):
            cs = slice(g * Q_BLOCK, (g + 1) * Q_BLOCK)
            score = score + jnp.maximum(s_even[:, cs], 0.0) * wt[2 * g:2 * g + 1, :]
            score = score + jnp.maximum(s_odd[:, cs], 0.0) * wt[2 * g + 1:2 * g + 2, :]
        score = jnp.where(kidx0 + off <= qidx, score, -jnp.inf)
        scr[kt] = score
        scr16[kt] = score.astype(BF16)
        return carry

    def for_tiles(fn):
        def pair(j, carry):
            fn(2 * j, 0)
            fn(2 * j + 1, 0)
            return carry

        lax.fori_loop(0, nk // 2, pair, 0)

        @pl.when(nk % 2 == 1)
        def _last():
            fn(nk - 1, 0)

    for_tiles(idx_body)

    def count_ge_bf16(c):
        def body(kt, acc):
            hit = scr16[kt] >= c
            for r in range(KEY_TILE // COUNT_ROWS):
                acc = jnp.where(hit[r * COUNT_ROWS:(r + 1) * COUNT_ROWS], acc + 1.0, acc)
            return acc
        acc = lax.fori_loop(0, nk, body, jnp.zeros((COUNT_ROWS, Q_BLOCK), BF16))
        return jnp.sum(acc.astype(F32), axis=0, keepdims=True)

    def count_cmp(cmp):
        def body(kt, acc):
            hit = cmp(scr[kt])
            for r in range(KEY_TILE // COUNT_ROWS):
                acc = jnp.where(hit[r * COUNT_ROWS:(r + 1) * COUNT_ROWS], acc + 1.0, acc)
            return acc
        acc = lax.fori_loop(0, nk, body, jnp.zeros((COUNT_ROWS, Q_BLOCK), F32))
        return jnp.sum(acc, axis=0, keepdims=True)

    tf = _threshold_search_coarse_fine(
        count_ge_bf16, lambda c: count_cmp(lambda sc: sc >= c),
        qb >= TOPK_MAX // Q_BLOCK, (1, Q_BLOCK))
    excess = jnp.max(count_cmp(lambda sc: sc >= tf)) > float(TOPK_MAX)

    qa2 = qa_ref[...].reshape(n_heads_q, ATTN_HEAD_DIM)

    def logits(kt):
        off = pl.multiple_of(kt * KEY_TILE, KEY_TILE)
        s = lax.dot_general(kbf[pl.ds(off, KEY_TILE), :], qa2, contract_last,
                            preferred_element_type=F32)
        return s * logit_scale

    def sel_plain(kt, carry):
        off = pl.multiple_of(kt * KEY_TILE, KEY_TILE)
        return (scr[kt] >= tf) & (kidx0 + off <= qidx), carry

    def sel_ties(need, kt, tie_off):
        off = pl.multiple_of(kt * KEY_TILE, KEY_TILE)
        sc = scr[kt]
        eq = sc == tf
        tie = jnp.where(eq, 1.0, 0.0)
        rank = jnp.dot(tri_ref[...], tie.astype(BF16), preferred_element_type=F32) + tie_off
        sel = ((sc > tf) | (eq & (rank <= need))) & (kidx0 + off <= qidx)
        return sel, tie_off + jnp.sum(tie, axis=0, keepdims=True)

    no_ties = jnp.zeros((1, Q_BLOCK), F32)

    def softmax_sum(m, sel_fn):
        acc_s[...] = jnp.zeros(acc_s.shape, F32)

        def body(kt, carry):
            sel, carry = sel_fn(kt, carry)
            e = jnp.exp2(logits(kt) - m)
            parts = []
            for h in range(ATTN_HEADS):
                cs = slice(h * Q_BLOCK, (h + 1) * Q_BLOCK)
                parts.append(jnp.where(sel, e[:, cs], 0.0).astype(BF16))
            p = jnp.concatenate(parts, axis=1)
            acc_s[...] += jnp.dot(vtb[kt], p, preferred_element_type=F32)
            return carry

        if sel_fn is sel_plain:
            for_tiles(body)
        else:
            lax.fori_loop(0, nk, body, no_ties)

    def selected_max(sel_fn):
        mrun[...] = jnp.full(mrun.shape, NEG_BIG, F32)

        def body(kt, carry):
            sel, carry = sel_fn(kt, carry)
            s = logits(kt)
            for h in range(ATTN_HEADS):
                cs = slice(h * Q_BLOCK, (h + 1) * Q_BLOCK)
                sh = jnp.where(sel, s[:, cs], NEG_BIG)
                mrun[:, cs] = jnp.maximum(
                    mrun[:, cs], jnp.max(sh.reshape(KEY_TILE // 8, 8, Q_BLOCK), axis=0))
            return carry

        lax.fori_loop(0, nk, body, no_ties)
        return jnp.max(mrun[...], axis=0, keepdims=True)

    def fast_path():
        q2 = (qa2 * qa2).astype(BF16)
        qsq = lax.dot_general(jnp.ones((8, ATTN_HEAD_DIM), BF16), q2, contract_last,
                              preferred_element_type=F32)[0:1]
        softmax_sum(jnp.sqrt(qsq * kmax[0:1, 0:1]) * logit_scale, sel_plain)
        return (jnp.min(acc_s[dv:dv + 1, :]) > 0.0).astype(I32)

    done = lax.cond(excess, lambda: jnp.int32(0), fast_path) == 1

    @pl.when(jnp.logical_not(done) & excess)
    def _exact_with_ties():
        need = float(TOPK_MAX) - count_cmp(lambda sc: sc > tf)
        sel_fn = functools.partial(sel_ties, need)
        softmax_sum(selected_max(sel_fn), sel_fn)

    @pl.when(jnp.logical_not(done) & jnp.logical_not(excess))
    def _exact_without_ties():
        softmax_sum(selected_max(sel_plain), sel_plain)

    out = acc_s[:dv, :] / acc_s[dv:dv + 1, :]
    for h in range(ATTN_HEADS):
        oh = out[:, h * Q_BLOCK:(h + 1) * Q_BLOCK].T
        o_ref[:, h * ATTN_HEAD_DIM:(h + 1) * ATTN_HEAD_DIM] = oh.astype(BF16)


def _prompt_attention(qa_hm, qi_pm, wi_t, ka, va, kd, tri, w_up, w_down, batch, seq):
    nq = seq // Q_BLOCK
    nkt = seq // KEY_TILE
    m = batch * seq
    n_heads_q = ATTN_HEADS * Q_BLOCK
    assert seq // COUNT_ROWS <= 256
    n_steps = batch * nq
    up_rows, down_rows = w_up.shape[0] // n_steps, w_down.shape[0] // n_steps
    assert up_rows * n_steps == w_up.shape[0] and down_rows * n_steps == w_down.shape[0]
    assert up_rows % 16 == 0 and down_rows % 16 == 0
    slab = lambda b, q: (b * nq + q, 0)
    in_specs = [
        pl.BlockSpec((ATTN_HEADS, Q_BLOCK, ATTN_HEAD_DIM), lambda b, q: (0, b * nq + q, 0)),
        pl.BlockSpec((IDX_HEADS // 2, Q_BLOCK, LANES), lambda b, q: (0, b * nq + q, 0)),
        pl.BlockSpec((IDX_HEADS, Q_BLOCK), lambda b, q: (0, b * nq + q)),
        pl.BlockSpec((seq, ATTN_HEAD_DIM), lambda b, q: (b, 0)),
        pl.BlockSpec((seq, ATTN_HEAD_DIM), lambda b, q: (b, 0)),
        pl.BlockSpec((seq, LANES), lambda b, q: (b, 0)),
        pl.BlockSpec((KEY_TILE, KEY_TILE), lambda b, q: (0, 0)),
        pl.BlockSpec((up_rows, w_up.shape[1]), slab),
        pl.BlockSpec((down_rows, w_down.shape[1]), slab),
    ]
    return pl.pallas_call(
        _attn_body,
        grid=(batch, nq),
        in_specs=in_specs,
        out_specs=(
            pl.BlockSpec((Q_BLOCK, ATTN_HEADS * ATTN_HEAD_DIM), slab),
            pl.BlockSpec((up_rows, w_up.shape[1]), slab),
            pl.BlockSpec((down_rows, w_down.shape[1]), slab),
        ),
        out_shape=(
            jax.ShapeDtypeStruct((m, ATTN_HEADS * ATTN_HEAD_DIM), BF16),
            jax.ShapeDtypeStruct(w_up.shape, BF16),
            jax.ShapeDtypeStruct(w_down.shape, BF16),
        ),
        scratch_shapes=[
            pltpu.VMEM((seq, ATTN_HEAD_DIM), BF16),
            pltpu.VMEM((nkt, ATTN_HEAD_DIM + SUM_ROWS, KEY_TILE), BF16),
            pltpu.VMEM((nkt, KEY_TILE, Q_BLOCK), F32),
            pltpu.VMEM((8, n_heads_q), F32),
            pltpu.VMEM((ATTN_HEAD_DIM + SUM_ROWS, n_heads_q), F32),
            pltpu.VMEM((8, LANES), F32),
            pltpu.VMEM((nkt, KEY_TILE, Q_BLOCK), BF16),
        ],
        compiler_params=_cparams(("arbitrary", "arbitrary")),
        name="prompt_attn",
    )(qa_hm, qi_pm, wi_t, ka, va, kd, tri, w_up, w_down)


def _gate(o, g):
    rn = o * lax.rsqrt(jnp.mean(o * o, axis=-1, keepdims=True) + EPS)
    return rn * (g / (1.0 + jnp.exp(-g)))


def _ret_body(q_ref, k_ref, v_ref, g_ref, decay_ref, rsc_ref, zeta_ref, gpow_ref,
              rg_ref, st_ref):
    c = pl.program_id(1)

    @pl.when(c == 0)
    def _init():
        st_ref[...] = jnp.zeros(st_ref.shape, F32)

    for h in range(RET_HEADS):
        sl = slice(h * 128, (h + 1) * 128)
        q = q_ref[:, sl]
        k = k_ref[:, sl]
        v = v_ref[:, sl]
        r_old = st_ref[0, h]
        qk = lax.dot_general(q, k, (((1,), (1,)), ((), ())), preferred_element_type=F32)
        inner = jnp.dot((qk * decay_ref[h]).astype(BF16), v, preferred_element_type=F32)
        cross = jnp.dot(q, r_old.astype(BF16), preferred_element_type=F32) * rsc_ref[h]
        kz = (k.astype(F32) * zeta_ref[h]).astype(BF16)
        upd = lax.dot_general(kz, v, (((0,), (0,)), ((), ())), preferred_element_type=F32)
        st_ref[0, h] = r_old * gpow_ref[h] + upd
        rg_ref[:, sl] = _gate(inner + cross, g_ref[:, sl].astype(F32)).astype(BF16)


def _prompt_retention(main, decay, rsc, zeta, gpow, batch, seq):
    nc = seq // RET_CHUNK
    m = batch * seq
    width = RET_WIDTH
    const3 = lambda b, c: (0, 0, 0)
    in_specs = [
        pl.BlockSpec((RET_CHUNK, width), lambda b, c: (b * nc + c, 0)),
        pl.BlockSpec((RET_CHUNK, width), lambda b, c: (b * nc + c, 1)),
        pl.BlockSpec((RET_CHUNK, width), lambda b, c: (b * nc + c, 2)),
        pl.BlockSpec((RET_CHUNK, width), lambda b, c: (b * nc + c, 3)),
        pl.BlockSpec((RET_HEADS, RET_CHUNK, RET_CHUNK), const3),
        pl.BlockSpec((RET_HEADS, RET_CHUNK, RET_DV), const3),
        pl.BlockSpec((RET_HEADS, RET_CHUNK, RET_DK), const3),
        pl.BlockSpec((RET_HEADS, 1, RET_DV), const3),
    ]
    return pl.pallas_call(
        _ret_body,
        grid=(batch, nc),
        in_specs=in_specs,
        out_specs=(
            pl.BlockSpec((RET_CHUNK, width), lambda b, c: (b * nc + c, 0)),
            pl.BlockSpec((1, RET_HEADS, RET_DK, RET_DV), lambda b, c: (b, 0, 0, 0)),
        ),
        out_shape=(
            jax.ShapeDtypeStruct((m, width), BF16),
            jax.ShapeDtypeStruct((batch, RET_HEADS, RET_DK, RET_DV), F32),
        ),
        compiler_params=_cparams(("arbitrary", "arbitrary")),
        name="prompt_ret",
    )(main, main, main, main, decay, rsc, zeta, gpow)


def _outproj_body(a_ref, r_ref, wa_ref, wr_ref, x_ref, g2_ref, x1_ref, h2_ref):
    mixed = (jnp.dot(a_ref[...], wa_ref[...], preferred_element_type=F32)
             + jnp.dot(r_ref[...], wr_ref[...], preferred_element_type=F32))
    x1 = x_ref[...] + mixed
    x1_ref[...] = x1
    ms = jnp.mean(x1 * x1, axis=-1, keepdims=True)
    h2_ref[...] = (x1 * lax.rsqrt(ms + EPS) * g2_ref[...]).astype(BF16)


def _out_projection(attn_o, rg, wa, wr, x2d, g2, tm):
    m = x2d.shape[0]
    half = attn_o.shape[1]
    in_specs = [
        pl.BlockSpec((tm, half), lambda i: (i, 0)),
        pl.BlockSpec((tm, half), lambda i: (i, 0)),
        pl.BlockSpec((half, D_MODEL), lambda i: (0, 0)),
        pl.BlockSpec((half, D_MODEL), lambda i: (0, 0)),
        pl.BlockSpec((tm, D_MODEL), lambda i: (i, 0)),
        pl.BlockSpec((1, D_MODEL), lambda i: (0, 0)),
    ]
    return pl.pallas_call(
        _outproj_body,
        grid=(m // tm,),
        in_specs=in_specs,
        out_specs=(pl.BlockSpec((tm, D_MODEL), lambda i: (i, 0)),
                   pl.BlockSpec((tm, D_MODEL), lambda i: (i, 0))),
        out_shape=(jax.ShapeDtypeStruct((m, D_MODEL), F32),
                   jax.ShapeDtypeStruct((m, D_MODEL), BF16)),
        compiler_params=_cparams(("arbitrary",)),
        name="out_proj",
    )(attn_o, rg, wa, wr, x2d, g2)


def _mlp_body(h2_ref, wu_ref, wd_ref, x1_ref, gf_ref, y_ref, acc_ref, side_job=None):
    f = pl.program_id(1)

    @pl.when(f == 0)
    def _init():
        acc_ref[...] = x1_ref[...]

    if side_job is not None:
        side_job()
    u = jnp.dot(h2_ref[...], wu_ref[...], preferred_element_type=F32)
    a = jnp.maximum(u, 0.0)
    acc_ref[...] += jnp.dot((a * a).astype(BF16), wd_ref[...], preferred_element_type=F32)

    @pl.when(f == pl.num_programs(1) - 1)
    def _final():
        x2 = acc_ref[...]
        ms = jnp.mean(x2 * x2, axis=-1, keepdims=True)
        y_ref[...] = x2 * lax.rsqrt(ms + EPS) * gf_ref[...]


def _mlp_ret_body(h2_ref, wu_ref, wd_ref, x1_ref, gf_ref, qkvg_ref, st_ref, gam_ref,
                  y_ref, rg_ref, so_ref, acc_ref):
    _mlp_body(h2_ref, wu_ref, wd_ref, x1_ref, gf_ref, y_ref, acc_ref,
              side_job=lambda: _sret_body(qkvg_ref, st_ref, gam_ref, rg_ref, so_ref))


def _mlp_with_sample_retention(h2, wu, wd, x1, gf, qkvg, state, gam, tm, tf):
    m = h2.shape[0]
    nb = state.shape[0]
    ni, nf = m // tm, D_FF // tf
    ns = nb // (ni * nf)
    assert ns * ni * nf == nb
    step = lambda i, f: (i * nf + f, 0, 0)
    return pl.pallas_call(
        _mlp_ret_body,
        grid=(ni, nf),
        in_specs=[
            pl.BlockSpec((tm, D_MODEL), lambda i, f: (i, 0)),
            pl.BlockSpec((D_MODEL, tf), lambda i, f: (0, f)),
            pl.BlockSpec((tf, D_MODEL), lambda i, f: (f, 0)),
            pl.BlockSpec((tm, D_MODEL), lambda i, f: (i, 0)),
            pl.BlockSpec((1, D_MODEL), lambda i, f: (0, 0)),
            pl.BlockSpec((ns, 32, LANES), step),
            pl.BlockSpec((ns, RET_HEADS, RET_DK, RET_DV), lambda i, f: (i * nf + f, 0, 0, 0)),
            pl.BlockSpec((RET_HEADS, 1, LANES), lambda i, f: (0, 0, 0)),
        ],
        out_specs=(
            pl.BlockSpec((tm, D_MODEL), lambda i, f: (i, 0)),
            pl.BlockSpec((ns, RET_HEADS, RET_DV), step),
            pl.BlockSpec((ns, RET_HEADS, RET_DK, RET_DV), lambda i, f: (i * nf + f, 0, 0, 0)),
        ),
        out_shape=(
            jax.ShapeDtypeStruct((m, D_MODEL), F32),
            jax.ShapeDtypeStruct((nb, RET_HEADS, RET_DV), BF16),
            jax.ShapeDtypeStruct(state.shape, F32),
        ),
        scratch_shapes=[pltpu.VMEM((tm, D_MODEL), F32)],
        compiler_params=_cparams(("arbitrary", "arbitrary")),
        name="mlp_ret",
    )(h2, wu, wd, x1, gf, qkvg, state, gam)


def _mlp(h2, wu, wd, x1, gf, tm, tf):
    m = h2.shape[0]
    in_specs = [
        pl.BlockSpec((tm, D_MODEL), lambda i, f: (i, 0)),
        pl.BlockSpec((D_MODEL, tf), lambda i, f: (0, f)),
        pl.BlockSpec((tf, D_MODEL), lambda i, f: (f, 0)),
        pl.BlockSpec((tm, D_MODEL), lambda i, f: (i, 0)),
        pl.BlockSpec((1, D_MODEL), lambda i, f: (0, 0)),
    ]
    return pl.pallas_call(
        _mlp_body,
        grid=(m // tm, D_FF // tf),
        in_specs=in_specs,
        out_specs=pl.BlockSpec((tm, D_MODEL), lambda i, f: (i, 0)),
        out_shape=jax.ShapeDtypeStruct((m, D_MODEL), F32),
        scratch_shapes=[pltpu.VMEM((tm, D_MODEL), F32)],
        compiler_params=_cparams(("arbitrary", "arbitrary")),
        name="mlp",
    )(h2, wu, wd, x1, gf)


def _fetch_pages(pt_ref, step, slot, streams, start):
    n_pages = pt_ref.shape[1]
    for hbm, buf, sem in streams:
        for j in range(n_pages):
            cp = pltpu.make_async_copy(hbm.at[pt_ref[step, j]], buf.at[slot, j], sem.at[slot])
            if start:
                cp.start()
            else:
                cp.wait()


def _paged_loop(pt_ref, streams, step_fn):
    nb = pt_ref.shape[0]
    n_slots = streams[0][1].shape[0]
    ahead = n_slots - PAGE_GROUP
    assert nb % PAGE_GROUP == 0 and ahead % PAGE_GROUP == 0 and PAGE_GROUP <= ahead <= nb
    for s in range(ahead):
        _fetch_pages(pt_ref, s, s, streams, start=True)

    def body(g, carry):
        b0 = g * PAGE_GROUP

        @pl.when(b0 + ahead < nb)
        def _next():
            for i in range(PAGE_GROUP):
                row = b0 + ahead + i
                _fetch_pages(pt_ref, row, row % n_slots, streams, start=True)

        for i in range(PAGE_GROUP):
            _fetch_pages(pt_ref, b0 + i, (b0 + i) % n_slots, streams, start=False)
        for i in range(PAGE_GROUP):
            step_fn(b0 + i, (b0 + i) % n_slots, i)
        return carry

    lax.fori_loop(0, nb // PAGE_GROUP, body, 0)


def _sidx_body(pt_ref, qi_ref, w_ref, kin_ref, cache_hbm, out_ref, kt_s, pbuf, sem):
    n_pages = pt_ref.shape[1]
    page = pbuf.shape[3]
    past = n_pages * page
    lane = lax.broadcasted_iota(I32, (1, LANES), 1)

    def step(b, slot, lane_of_trip):
        qi = qi_ref[b]
        w = w_ref[b] * (IDX_HEAD_DIM ** -0.5)
        for j in range(n_pages):
            kt_s[lane_of_trip, :, j * page:(j + 1) * page] = pbuf[slot, j].astype(BF16)
        s = jnp.dot(qi, kt_s[lane_of_trip], preferred_element_type=F32)
        out_ref[b, :, 0:past] = jnp.sum(jnp.maximum(s, 0.0) * w, axis=0, keepdims=True)
        sn = jnp.sum(qi.astype(F32) * kin_ref[b].astype(BF16).astype(F32), axis=1, keepdims=True)
        rn = jnp.sum(jnp.maximum(sn, 0.0) * w, axis=0, keepdims=True)
        out_ref[b, :, past:past + LANES] = jnp.where(lane == 0, rn, -jnp.inf)

    _paged_loop(pt_ref, ((cache_hbm, pbuf, sem),), step)


def _sample_index_scores(page_table, qi_s, wi_s, ki_s, cache_idx_k_t):
    nb, n_pages = page_table.shape
    page = cache_idx_k_t.shape[2]
    width = n_pages * page + LANES

    vmem = pl.BlockSpec(memory_space=pltpu.VMEM)
    return pl.pallas_call(
        _sidx_body,
        in_specs=[pl.BlockSpec(memory_space=pltpu.SMEM), vmem, vmem, vmem,
                  pl.BlockSpec(memory_space=pl.ANY)],
        out_specs=vmem,
        out_shape=jax.ShapeDtypeStruct((nb, 1, width), F32),
        scratch_shapes=[pltpu.VMEM((PAGE_GROUP, IDX_HEAD_DIM, n_pages * page), BF16),
                        pltpu.VMEM((IDX_PAGE_SLOTS, n_pages, IDX_HEAD_DIM, page), F32),
                        pltpu.SemaphoreType.DMA((IDX_PAGE_SLOTS,))],
        compiler_params=pltpu.CompilerParams(vmem_limit_bytes=VMEM_LIMIT),
        name="sample_idx",
    )(page_table, qi_s, wi_s, ki_s, cache_idx_k_t)


def _ssel_body(sc_ref, tri_ref, sel_ref):
    rows, width = sc_ref.shape
    nt = width // LANES
    n_valid = (nt - 1) * LANES + 1

    def tile(kt):
        return sc_ref[:, kt * LANES:(kt + 1) * LANES]

    def count_cmp(cmp):
        acc = jnp.zeros((rows, LANES), F32)
        for kt in range(nt):
            acc = acc + jnp.where(cmp(tile(kt)), 1.0, 0.0)
        return jnp.broadcast_to(jnp.sum(acc, axis=1, keepdims=True), (rows, LANES))

    tf = _threshold_search(lambda c: count_cmp(lambda sc: sc >= c), 32, (rows, LANES))
    need = float(TOPK_MAX) - count_cmp(lambda sc: sc > tf)
    tie_off = jnp.zeros((rows, LANES), F32)
    for kt in range(nt):
        col = kt * LANES + lax.broadcasted_iota(I32, (rows, LANES), 1)
        sc = tile(kt)
        eq = sc == tf
        tie = jnp.where(eq, 1.0, 0.0)
        rank = jnp.dot(tie.astype(BF16), tri_ref[...], preferred_element_type=F32) + tie_off
        sel = ((sc > tf) | (eq & (rank <= need))) & (col < n_valid)
        sel_ref[:, kt * LANES:(kt + 1) * LANES] = jnp.where(sel, 1.0, 0.0)
        tie_off = tie_off + jnp.broadcast_to(jnp.sum(tie, axis=1, keepdims=True), (rows, LANES))


def _sample_select(scores2d, tri):
    rows, width = scores2d.shape
    return pl.pallas_call(
        _ssel_body,
        out_shape=jax.ShapeDtypeStruct((rows, width), F32),
        compiler_params=pltpu.CompilerParams(vmem_limit_bytes=VMEM_LIMIT),
        name="sample_select",
    )(scores2d, tri)


def _sattn_body(pt_ref, q_ref, sel_ref, kn_ref, vn_ref, ck_hbm, cv_hbm, o_ref,
                kbuf, vbuf, ksem, vsem):
    n_pages = pt_ref.shape[1]
    page = kbuf.shape[2]
    past = n_pages * page
    scale = ATTN_HEAD_DIM ** -0.5

    def step(b, slot, lane_of_trip):
        q = q_ref[b]
        k_all = kbuf[slot].reshape(past, ATTN_HEAD_DIM).astype(BF16)
        v_all = vbuf[slot].reshape(past, ATTN_HEAD_DIM).astype(BF16)
        s = lax.dot_general(q, k_all, (((1,), (1,)), ((), ())), preferred_element_type=F32)
        s = jnp.where(sel_ref[b, :, 0:past] > 0.5, s * scale, NEG_BIG)
        kn = kn_ref[b].astype(BF16).astype(F32)
        sn = jnp.sum(q.astype(F32) * kn, axis=1, keepdims=True) * scale
        sn = jnp.where(sel_ref[b, :, past:past + 1] > 0.5, sn, NEG_BIG)
        m = jnp.maximum(jnp.max(s, axis=1, keepdims=True), sn)
        pn = jnp.exp(sn - m)
        p = jnp.exp(s - m)
        l = pn + jnp.sum(p, axis=1, keepdims=True)
        acc = (pn * vn_ref[b].astype(BF16).astype(F32)
               + jnp.dot(p.astype(BF16), v_all, preferred_element_type=F32))
        o_ref[b] = (acc / l).astype(BF16)

    _paged_loop(pt_ref, ((ck_hbm, kbuf, ksem), (cv_hbm, vbuf, vsem)), step)


def _sample_attention(page_table, qa_s, sel3, ka_s, va_s, cache_k, cache_v):
    nb, n_pages = page_table.shape
    page = cache_k.shape[1]

    vmem = pl.BlockSpec(memory_space=pltpu.VMEM)
    hbm = pl.BlockSpec(memory_space=pl.ANY)
    return pl.pallas_call(
        _sattn_body,
        in_specs=[pl.BlockSpec(memory_space=pltpu.SMEM), vmem, vmem, vmem, vmem, hbm, hbm],
        out_specs=vmem,
        out_shape=jax.ShapeDtypeStruct((nb, ATTN_HEADS, ATTN_HEAD_DIM), BF16),
        scratch_shapes=[pltpu.VMEM((ATTN_PAGE_SLOTS, n_pages, page, ATTN_HEAD_DIM), F32),
                        pltpu.VMEM((ATTN_PAGE_SLOTS, n_pages, page, ATTN_HEAD_DIM), F32),
                        pltpu.SemaphoreType.DMA((ATTN_PAGE_SLOTS,)),
                        pltpu.SemaphoreType.DMA((ATTN_PAGE_SLOTS,))],
        compiler_params=pltpu.CompilerParams(vmem_limit_bytes=VMEM_LIMIT),
        name="sample_attn",
    )(page_table, qa_s, sel3, ka_s, va_s, cache_k, cache_v)


def _sret_body(qkvg_ref, st_ref, gam_ref, rg_ref, so_ref):
    ns = st_ref.shape[0]
    for s in range(ns):
        blk = qkvg_ref[s].astype(F32)
        q8 = blk[0:8]
        k8 = blk[8:16]
        v8 = blk[16:24]
        g8 = blk[24:32]
        q_t = q8.T
        k_t = k8.T
        qk = jnp.sum(q8 * k8, axis=1, keepdims=True)
        rows = []
        for h in range(RET_HEADS):
            r_old = st_ref[s, h]
            gam = gam_ref[h]
            qcol = jnp.broadcast_to(q_t[:, h:h + 1], (RET_DK, RET_DV))
            kcol = jnp.broadcast_to(k_t[:, h:h + 1], (RET_DK, RET_DV))
            vrow = v8[h:h + 1]
            q_r = jnp.sum(qcol * r_old, axis=0, keepdims=True)
            rows.append(gam * q_r + qk[h:h + 1] * vrow)
            so_ref[s, h] = gam * r_old + kcol * vrow
        ret = jnp.concatenate(rows, axis=0)
        rg_ref[s] = _gate(ret, g8).astype(BF16)


def _sample_retention(qkvg, state, gam, ns):
    nb = state.shape[0]
    return pl.pallas_call(
        _sret_body,
        grid=(nb // ns,),
        in_specs=[
            pl.BlockSpec((ns, 32, LANES), lambda i: (i, 0, 0)),
            pl.BlockSpec((ns, RET_HEADS, RET_DK, RET_DV), lambda i: (i, 0, 0, 0)),
            pl.BlockSpec((RET_HEADS, 1, LANES), lambda i: (0, 0, 0)),
        ],
        out_specs=(
            pl.BlockSpec((ns, RET_HEADS, RET_DV), lambda i: (i, 0, 0)),
            pl.BlockSpec((ns, RET_HEADS, RET_DK, RET_DV), lambda i: (i, 0, 0, 0)),
        ),
        out_shape=(
            jax.ShapeDtypeStruct((nb, RET_HEADS, RET_DV), BF16),
            jax.ShapeDtypeStruct(state.shape, F32),
        ),
        compiler_params=_cparams(("arbitrary",)),
        name="sample_ret",
    )(qkvg, state, gam)


def _rotary_table(pos):
    half = RET_DK // 2
    inv = ROPE_BASE ** (-np.arange(half, dtype=np.float64) / half)
    ang = np.asarray(pos, np.float64)[:, None] * inv[None, :]
    cos = np.cos(ang)
    sin = np.sin(ang)
    return jnp.asarray(np.concatenate([cos, cos, -sin, sin], axis=1), F32)


def _retention_constants():
    lg = np.log1p(-np.exp2(-5.0 - np.arange(RET_HEADS, dtype=np.float64)))
    n = RET_CHUNK
    i = np.arange(n, dtype=np.float64)
    diff = i[:, None] - i[None, :]
    decay = np.where(diff[None] >= 0, np.exp(np.maximum(diff, 0.0)[None] * lg[:, None, None]), 0.0)
    rsc = np.exp((i + 1.0)[None, :] * lg[:, None])
    zeta = np.exp((n - 1.0 - i)[None, :] * lg[:, None])
    gpow = np.exp(n * lg)
    gam1 = np.exp(lg)
    rsc_b = np.broadcast_to(rsc[:, :, None], (RET_HEADS, n, RET_DV))
    zeta_b = np.broadcast_to(zeta[:, :, None], (RET_HEADS, n, RET_DK))
    gpow_b = np.broadcast_to(gpow[:, None, None], (RET_HEADS, 1, RET_DV))
    gam1_b = np.broadcast_to(gam1[:, None, None], (RET_HEADS, 1, LANES))
    return tuple(jnp.asarray(a, F32) for a in (decay, rsc_b, zeta_b, gpow_b, gam1_b))


def _upper_tri(n):
    return jnp.asarray(np.triu(np.ones((n, n), np.float32)), BF16)


def _lower_tri(n):
    return jnp.asarray(np.tril(np.ones((n, n), np.float32)), BF16)


def _pad_lanes(v):
    return jnp.pad(v, (0, LANES - v.shape[0])).reshape(1, LANES)


def kernel(x_prompt, x_sample, cache_k, cache_v, cache_idx_k, state_ret, page_table,
           norm1_g, w_in, idx_k_norm_g, idx_k_norm_b, w_out, norm2_g, w_up, w_down, final_norm_g):
    batch, seq, _ = x_prompt.shape
    nb = x_sample.shape[0]
    past_len = page_table.shape[1] * cache_k.shape[1]
    half_mix = ATTN_HEADS * ATTN_HEAD_DIM

    wt = w_in.T.astype(BF16)
    wa = w_out[:half_mix].astype(BF16)
    wr = w_out[half_mix:].astype(BF16)
    g1 = norm1_g.reshape(1, D_MODEL)
    g2 = norm2_g.reshape(1, D_MODEL)
    gf = final_norm_g.reshape(1, D_MODEL)
    lng = _pad_lanes(idx_k_norm_g)
    lnb = _pad_lanes(idx_k_norm_b)
    decay, rsc_b, zeta_b, gpow_b, gam1_b = _retention_constants()

    xp = x_prompt.reshape(batch * seq, D_MODEL)
    xs = x_sample.reshape(nb, D_MODEL)
    cs_p = _rotary_table(np.arange(seq))
    cs_s = _rotary_table(np.full((nb,), past_len))
    (qa_p, qi_p, ka_p, va_p, ki_p, kd_p, wi_p), (qa_s, qi_s, ka_s, va_s, ki_s, _, wi_s) = (
        _project_attn(xp, xs, g1, wt, lng, lnb, tm=TILES["proj_attn_rows"]))
    main_p, main_s = _project_ret(xp, xs, g1, wt, cs_p, cs_s, tm=TILES["proj_ret_rows"])

    attn_p, wu, wd = _prompt_attention(qa_p, qi_p, wi_p, ka_p, va_p, kd_p, _lower_tri(KEY_TILE),
                                       w_up, w_down, batch, seq)
    rg_p, ret_state_p = _prompt_retention(main_p, decay, rsc_b, zeta_b, gpow_b, batch, seq)
    x1_p, h2_p = _out_projection(attn_p, rg_p, wa, wr, xp, g2, tm=TILES["out_proj_rows"])

    scores = _sample_index_scores(
        page_table,
        qi_s.transpose(1, 0, 2).reshape(nb, IDX_HEADS, IDX_HEAD_DIM),
        wi_s.T.reshape(nb, IDX_HEADS, 1),
        ki_s.reshape(nb, 1, IDX_HEAD_DIM),
        jnp.swapaxes(cache_idx_k, 1, 2))
    width = scores.shape[2]
    sel = _sample_select(scores.reshape(nb, width), _upper_tri(LANES))
    attn_s = _sample_attention(
        page_table,
        qa_s.transpose(1, 0, 2),
        sel.reshape(nb, 1, width),
        ka_s.reshape(nb, 1, ATTN_HEAD_DIM),
        va_s.reshape(nb, 1, ATTN_HEAD_DIM),
        cache_k, cache_v)
    y_p, rg_s, ret_state_s = _mlp_with_sample_retention(
        h2_p, wu, wd, x1_p, gf, main_s.reshape(nb, 32, LANES), state_ret, gam1_b,
        tm=TILES["mlp_rows"], tf=TILES["mlp_ff"])
    x1_s, h2_s = _out_projection(attn_s.reshape(nb, half_mix), rg_s.reshape(nb, RET_WIDTH),
                                 wa, wr, xs, g2, tm=nb)
    y_s = _mlp(h2_s, wu, wd, x1_s, gf, tm=nb, tf=TILES["mlp_ff"])

    return (
        y_p.reshape(batch, seq, D_MODEL),
        y_s.reshape(nb, 1, D_MODEL),
        ka_p.reshape(batch, seq, ATTN_HEAD_DIM),
        va_p.reshape(batch, seq, ATTN_HEAD_DIM),
        ki_p.reshape(batch, seq, IDX_HEAD_DIM),
        ret_state_p,
        ka_s.reshape(nb, 1, ATTN_HEAD_DIM),
        va_s.reshape(nb, 1, ATTN_HEAD_DIM),
        ki_s.reshape(nb, 1, IDX_HEAD_DIM),
        ret_state_s,
    )
```

```python
import functools

import numpy as np
import jax
import jax.numpy as jnp
from jax import lax
from jax.experimental import pallas as pl
from jax.experimental.pallas import tpu as pltpu

F32 = jnp.float32
BF16 = jnp.bfloat16
I32 = jnp.int32

D_MODEL = 2048
ATTN_HEADS = 8
ATTN_HEAD_DIM = 128
IDX_HEADS = 16
IDX_HEAD_DIM = 64
TOPK_MAX = 256
RET_HEADS = 8
RET_DK = 128
RET_DV = 128
RET_CHUNK = 256
ROPE_BASE = 10000.0
D_FF = 4 * D_MODEL
EPS = 1e-6
Q_BLOCK = 256

OFF_QA, OFF_KA, OFF_VA, OFF_QI, OFF_KI, OFF_WI = 0, 1024, 1152, 1280, 2304, 2368
OFF_QR, OFF_KR, OFF_VR, OFF_GR = 2384, 3408, 4432, 5456
RET_WIDTH = RET_HEADS * RET_DV

LANES = 128
PROJ_TILE = 512
KEY_TILE = 256
COUNT_ROWS = 32
PAGE_GROUP = 2
IDX_PAGE_SLOTS = 8
ATTN_PAGE_SLOTS = 6
SUM_ROWS = 16
LOG2_E = 1.4426950408889634
INT_MIN = -2 ** 31
KEY_NEG_INF = -2 ** 31 + 0x7FFFFF
BF16_KEY_NEG_INF = -2 ** 15 + 0x7F
NEG_BIG = -1e30
VMEM_LIMIT = 56 * 1024 * 1024

TILES = {
    "proj_attn_rows": 1024,
    "proj_ret_rows": 512,
    "out_proj_rows": 512,
    "mlp_rows": 512,
    "mlp_ff": 1024,
}


def _cparams(sem):
    return pltpu.CompilerParams(dimension_semantics=sem, vmem_limit_bytes=VMEM_LIMIT)


def _resident(shape):
    zeros = (0,) * len(shape)
    return pl.BlockSpec(shape, lambda *_: zeros, pipeline_mode=pl.Buffered(1))


def _normed_input(x_ref, g_ref, xn_ref):
    x = x_ref[...]
    ms = jnp.mean(x * x, axis=-1, keepdims=True)
    xn_ref[...] = (x * lax.rsqrt(ms + EPS) * g_ref[...]).astype(BF16)


def _matmul_rows(xn, wt_ref, r0, n):
    return lax.dot_general(xn, wt_ref[r0:r0 + n, :], (((1,), (1,)), ((), ())),
                           preferred_element_type=F32)


def _with_sample_rows(x_ref, xs_ref, g_ref, xn_ref, compute):
    tm, ns = x_ref.shape[0], xs_ref.shape[0]
    last = pl.program_id(0) == pl.num_programs(0) - 1

    @pl.when(last)
    def _prompt_and_sample_rows():
        _normed_input(x_ref, g_ref, xn_ref.at[pl.ds(0, tm)])
        _normed_input(xs_ref, g_ref, xn_ref.at[pl.ds(tm, ns)])
        compute(xn_ref[...], with_samples=True)

    @pl.when(jnp.logical_not(last))
    def _prompt_rows():
        _normed_input(x_ref, g_ref, xn_ref.at[pl.ds(0, tm)])
        compute(xn_ref[pl.ds(0, tm), :], with_samples=False)


def _proj_attn_body(x_ref, xs_ref, g_ref, wt_ref, lng_ref, lnb_ref, *refs):
    outs_p, outs_s, xn_ref = refs[0:7], refs[7:14], refs[14]
    tm, ns = x_ref.shape[0], xs_ref.shape[0]

    def compute(xn, with_samples):
        parts = [((0, tm), outs_p)] + ([((tm, tm + ns), outs_s)] if with_samples else [])
        mm = functools.partial(_matmul_rows, xn, wt_ref)
        for t in range(ATTN_HEADS * ATTN_HEAD_DIM // PROJ_TILE):
            acc = mm(OFF_QA + t * PROJ_TILE, PROJ_TILE)
            for hh in range(PROJ_TILE // LANES):
                piece = acc[:, hh * LANES:(hh + 1) * LANES].astype(BF16)
                for (a, b), o in parts:
                    o[0][4 * t + hh] = piece[a:b]
        for t in range(IDX_HEADS * IDX_HEAD_DIM // PROJ_TILE):
            acc = mm(OFF_QI + t * PROJ_TILE, PROJ_TILE)
            for hh in range(PROJ_TILE // LANES):
                piece = acc[:, hh * LANES:(hh + 1) * LANES].astype(BF16)
                for (a, b), o in parts:
                    o[1][4 * t + hh] = piece[a:b]
        kv = mm(OFF_KA, 2 * ATTN_HEAD_DIM)
        kw = mm(OFF_KI, LANES)
        lane = lax.broadcasted_iota(I32, kw.shape, 1)
        is_k = lane < IDX_HEAD_DIM
        mu = jnp.sum(jnp.where(is_k, kw, 0.0), axis=-1, keepdims=True) * (1.0 / IDX_HEAD_DIM)
        d = jnp.where(is_k, kw - mu, 0.0)
        var = jnp.sum(d * d, axis=-1, keepdims=True) * (1.0 / IDX_HEAD_DIM)
        kn = d * lax.rsqrt(var + EPS) * lng_ref[...] + lnb_ref[...]
        kd = jnp.where(is_k, kn, pltpu.roll(kn, IDX_HEAD_DIM, 1)).astype(BF16)
        wi = kw[:, IDX_HEAD_DIM:IDX_HEAD_DIM + IDX_HEADS] * (IDX_HEADS ** -0.5)
        for (a, b), o in parts:
            o[2][...] = kv[a:b, :ATTN_HEAD_DIM]
            o[3][...] = kv[a:b, ATTN_HEAD_DIM:]
            o[4][...] = kn[a:b, :IDX_HEAD_DIM]
            o[5][...] = kd[a:b]
            o[6][...] = wi[a:b].T

    _with_sample_rows(x_ref, xs_ref, g_ref, xn_ref, compute)


def _proj_ret_body(x_ref, xs_ref, g_ref, wt_ref, cs_ref, css_ref, main_ref, mains_ref, xn_ref):
    tm, ns = x_ref.shape[0], xs_ref.shape[0]

    def compute(xn, with_samples):
        parts = [((0, tm), main_ref)] + ([((tm, tm + ns), mains_ref)] if with_samples else [])
        cs = jnp.concatenate([cs_ref[...], css_ref[...]], axis=0) if with_samples else cs_ref[...]
        cosf = cs[:, :LANES]
        sinf = cs[:, LANES:]
        tiles = RET_WIDTH // PROJ_TILE
        for seg, (off, scale) in enumerate(((OFF_QR, None), (OFF_KR, RET_DK ** -0.5))):
            for t in range(tiles):
                acc = _matmul_rows(xn, wt_ref, off + t * PROJ_TILE, PROJ_TILE)
                for hh in range(PROJ_TILE // LANES):
                    xh = acc[:, hh * LANES:(hh + 1) * LANES]
                    r = xh * cosf + pltpu.roll(xh, RET_DK // 2, 1) * sinf
                    if scale is not None:
                        r = r * scale
                    c0 = seg * RET_WIDTH + t * PROJ_TILE + hh * LANES
                    r = r.astype(BF16)
                    for (a, b), o in parts:
                        o[:, c0:c0 + LANES] = r[a:b]
        for seg, off in ((2, OFF_VR), (3, OFF_GR)):
            for t in range(tiles):
                acc = _matmul_rows(xn, wt_ref, off + t * PROJ_TILE, PROJ_TILE).astype(BF16)
                c0 = seg * RET_WIDTH + t * PROJ_TILE
                for (a, b), o in parts:
                    o[:, c0:c0 + PROJ_TILE] = acc[a:b]

    _with_sample_rows(x_ref, xs_ref, g_ref, xn_ref, compute)


def _project_attn(x2d, xs2d, g1, wt, lng, lnb, tm):
    m, ns = x2d.shape[0], xs2d.shape[0]
    row = lambda i: (i, 0)

    def shapes(n):
        return (
            jax.ShapeDtypeStruct((ATTN_HEADS, n, ATTN_HEAD_DIM), BF16),
            jax.ShapeDtypeStruct((IDX_HEADS // 2, n, LANES), BF16),
            jax.ShapeDtypeStruct((n, ATTN_HEAD_DIM), F32),
            jax.ShapeDtypeStruct((n, ATTN_HEAD_DIM), F32),
            jax.ShapeDtypeStruct((n, IDX_HEAD_DIM), F32),
            jax.ShapeDtypeStruct((n, LANES), BF16),
            jax.ShapeDtypeStruct((IDX_HEADS, n), F32),
        )

    prompt_specs = (
        pl.BlockSpec((ATTN_HEADS, tm, ATTN_HEAD_DIM), lambda i: (0, i, 0)),
        pl.BlockSpec((IDX_HEADS // 2, tm, LANES), lambda i: (0, i, 0)),
        pl.BlockSpec((tm, ATTN_HEAD_DIM), row),
        pl.BlockSpec((tm, ATTN_HEAD_DIM), row),
        pl.BlockSpec((tm, IDX_HEAD_DIM), row),
        pl.BlockSpec((tm, LANES), row),
        pl.BlockSpec((IDX_HEADS, tm), lambda i: (0, i)),
    )
    sample_specs = tuple(pl.BlockSpec(s.shape, lambda i, nd=len(s.shape): (0,) * nd)
                         for s in shapes(ns))
    outs = pl.pallas_call(
        _proj_attn_body,
        grid=(m // tm,),
        in_specs=[pl.BlockSpec((tm, D_MODEL), row), _resident((ns, D_MODEL)),
                  _resident((1, D_MODEL)), _resident((OFF_QR + LANES, D_MODEL)),
                  _resident((1, LANES)), _resident((1, LANES))],
        out_specs=prompt_specs + sample_specs,
        out_shape=shapes(m) + shapes(ns),
        scratch_shapes=[pltpu.VMEM((tm + ns, D_MODEL), BF16)],
        compiler_params=_cparams(("arbitrary",)),
        name="proj_attn",
    )(x2d, xs2d, g1, wt, lng, lnb)
    return outs[:7], outs[7:]


def _project_ret(x2d, xs2d, g1, wt, cs, css, tm):
    m, ns = x2d.shape[0], xs2d.shape[0]
    n_pos_blocks = cs.shape[0] // tm
    row = lambda i: (i, 0)
    return pl.pallas_call(
        _proj_ret_body,
        grid=(m // tm,),
        in_specs=[pl.BlockSpec((tm, D_MODEL), row), _resident((ns, D_MODEL)),
                  _resident((1, D_MODEL)), _resident(wt.shape),
                  pl.BlockSpec((tm, 2 * LANES), lambda i: (i % n_pos_blocks, 0)),
                  _resident((ns, 2 * LANES))],
        out_specs=(pl.BlockSpec((tm, 4 * RET_WIDTH), row),
                   pl.BlockSpec((ns, 4 * RET_WIDTH), lambda i: (0, 0))),
        out_shape=(jax.ShapeDtypeStruct((m, 4 * RET_WIDTH), BF16),
                   jax.ShapeDtypeStruct((ns, 4 * RET_WIDTH), BF16)),
        scratch_shapes=[pltpu.VMEM((tm + ns, D_MODEL), BF16)],
        compiler_params=_cparams(("arbitrary",)),
        name="proj_ret",
    )(x2d, xs2d, g1, wt, cs, css)


def _key_to_float(key):
    bits = key ^ ((key >> 31) & 0x7FFFFFFF)
    return lax.bitcast_convert_type(bits, F32)


def _threshold_search(count_ge, n_iter, shape):
    def body(it, t):
        bit = lax.shift_left(jnp.int32(1), 31 - it)
        cand = t ^ bit
        cnt = count_ge(_key_to_float(cand))
        return jnp.where(cnt >= float(TOPK_MAX), cand, t)

    t = lax.fori_loop(0, n_iter, body, jnp.full(shape, INT_MIN, I32))
    return _key_to_float(jnp.maximum(t, KEY_NEG_INF))


def _bf16_key_to_f32_key(k16):
    return lax.shift_left(k16, 16) | jnp.where(k16 < 0, 0xFFFF, 0)


def _threshold_search_coarse_fine(count_ge_bf16, count_ge, run, shape):
    def coarse(it, u):
        cand = u | lax.shift_left(jnp.int32(1), 15 - it)
        c = _key_to_float(_bf16_key_to_f32_key(cand - 32768)).astype(BF16)
        return jnp.where(count_ge_bf16(c) >= float(TOPK_MAX), cand, u)

    u = lax.fori_loop(0, jnp.where(run, 16, 0), coarse, jnp.zeros(shape, I32))
    k1 = jnp.maximum(u - 32768, BF16_KEY_NEG_INF)
    lo = _bf16_key_to_f32_key(jnp.maximum(k1 - 1, -32768))
    hi = _bf16_key_to_f32_key(jnp.minimum(k1 + 1, 32767))

    def fine(it, t):
        cand = t + lax.shift_left(jnp.int32(1), 16 - it)
        ok = (cand < hi) & (count_ge(_key_to_float(cand)) >= float(TOPK_MAX))
        return jnp.where(ok, cand, t)

    t = lax.fori_loop(0, jnp.where(run, 17, 0), fine, lo)
    return _key_to_float(jnp.maximum(t, KEY_NEG_INF))


def _attn_body(qa_ref, qi_ref, wit_ref, ka_ref, va_ref, kd_ref, tri_ref, wu_ref, wd_ref, wo_ref,
               o_ref, wub_ref, wdb_ref, wob_ref, kbf, vtb, scr, mrun, acc_s, kmax, scr16):
    wub_ref[...] = wu_ref[...].astype(BF16)
    wdb_ref[...] = wd_ref[...].astype(BF16)
    wob_ref[...] = wo_ref[...].astype(BF16)

    qb = pl.program_id(1)
    n_heads_q = ATTN_HEADS * Q_BLOCK
    n_pairs = IDX_HEADS // 2
    dv = ATTN_HEAD_DIM
    logit_scale = ATTN_HEAD_DIM ** -0.5 * LOG2_E

    @pl.when(qb == 0)
    def _cast():
        ka = ka_ref[...]
        kbf[...] = ka.astype(BF16)
        kmax[...] = jnp.broadcast_to(jnp.max(jnp.sum(ka * ka, axis=1, keepdims=True)), kmax.shape)
        for kt in range(vtb.shape[0]):
            vtb[kt, :dv] = va_ref[kt * KEY_TILE:(kt + 1) * KEY_TILE, :].T.astype(BF16)
            vtb[kt, dv:] = jnp.ones((vtb.shape[1] - dv, KEY_TILE), BF16)

    nk = ((qb + 1) * Q_BLOCK + KEY_TILE - 1) // KEY_TILE
    wt = wit_ref[...] * (IDX_HEAD_DIM ** -0.5)
    qi2 = qi_ref[...].reshape(n_pairs * Q_BLOCK, LANES)
    lo_half = lax.broadcasted_iota(I32, (KEY_TILE, LANES), 1) < IDX_HEAD_DIM
    qidx = qb * Q_BLOCK + lax.broadcasted_iota(I32, (KEY_TILE, Q_BLOCK), 1)
    kidx0 = lax.broadcasted_iota(I32, (KEY_TILE, Q_BLOCK), 0)
    contract_last = (((1,), (1,)), ((), ()))

    def idx_body(kt, carry):
        off = pl.multiple_of(kt * KEY_TILE, KEY_TILE)
        kit = kd_ref[pl.ds(off, KEY_TILE), :]
        zero = jnp.zeros_like(kit)
        s_even = lax.dot_general(jnp.where(lo_half, kit, zero), qi2, contract_last,
                                 preferred_element_type=F32)
        s_odd = lax.dot_general(jnp.where(lo_half, zero, kit), qi2, contract_last,
                                preferred_element_type=F32)
        score = jnp.zeros((KEY_TILE, Q_BLOCK), F32)
        for g in range(n_pairs):
            cs = slice(g * Q_BLOCK, (g + 1) * Q_BLOCK)
            score = score + jnp.maximum(s_even[:, cs], 0.0) * wt[2 * g:2 * g + 1, :]
            score = score + jnp.maximum(s_odd[:, cs], 0.0) * wt[2 * g + 1:2 * g + 2, :]
        score = jnp.where(kidx0 + off <= qidx, score, -jnp.inf)
        scr[kt] = score
        scr16[kt] = score.astype(BF16)
        return carry

    def for_tiles(fn):
        def pair(j, carry):
            fn(2 * j, 0)
            fn(2 * j + 1, 0)
            return carry

        lax.fori_loop(0, nk // 2, pair, 0)

        @pl.when(nk % 2 == 1)
        def _last():
            fn(nk - 1, 0)

    for_tiles(idx_body)

    def count_ge_bf16(c):
        def body(kt, acc):
            hit = scr16[kt] >= c
            for r in range(KEY_TILE // COUNT_ROWS):
                acc = jnp.where(hit[r * COUNT_ROWS:(r + 1) * COUNT_ROWS], acc + 1.0, acc)
            return acc
        acc = lax.fori_loop(0, nk, body, jnp.zeros((COUNT_ROWS, Q_BLOCK), BF16))
        return jnp.sum(acc.astype(F32), axis=0, keepdims=True)

    def count_cmp(cmp):
        def body(kt, acc):
            hit = cmp(scr[kt])
            for r in range(KEY_TILE // COUNT_ROWS):
                acc = jnp.where(hit[r * COUNT_ROWS:(r + 1) * COUNT_ROWS], acc + 1.0, acc)
            return acc
        acc = lax.fori_loop(0, nk, body, jnp.zeros((COUNT_ROWS, Q_BLOCK), F32))
        return jnp.sum(acc, axis=0, keepdims=True)

    tf = _threshold_search_coarse_fine(
        count_ge_bf16, lambda c: count_cmp(lambda sc: sc >= c),
        qb >= TOPK_MAX // Q_BLOCK, (1, Q_BLOCK))
    excess = jnp.max(count_cmp(lambda sc: sc >= tf)) > float(TOPK_MAX)

    qa2 = qa_ref[...].reshape(n_heads_q, ATTN_HEAD_DIM)

    def logits(kt):
        off = pl.multiple_of(kt * KEY_TILE, KEY_TILE)
        s = lax.dot_general(kbf[pl.ds(off, KEY_TILE), :], qa2, contract_last,
                            preferred_element_type=F32)
        return s * logit_scale

    def sel_plain(kt, carry):
        off = pl.multiple_of(kt * KEY_TILE, KEY_TILE)
        return (scr[kt] >= tf) & (kidx0 + off <= qidx), carry

    def sel_ties(need, kt, tie_off):
        off = pl.multiple_of(kt * KEY_TILE, KEY_TILE)
        sc = scr[kt]
        eq = sc == tf
        tie = jnp.where(eq, 1.0, 0.0)
        rank = jnp.dot(tri_ref[...], tie.astype(BF16), preferred_element_type=F32) + tie_off
        sel = ((sc > tf) | (eq & (rank <= need))) & (kidx0 + off <= qidx)
        return sel, tie_off + jnp.sum(tie, axis=0, keepdims=True)

    no_ties = jnp.zeros((1, Q_BLOCK), F32)

    def softmax_sum(m, sel_fn):
        acc_s[...] = jnp.zeros(acc_s.shape, F32)

        def body(kt, carry):
            sel, carry = sel_fn(kt, carry)
            e = jnp.exp2(logits(kt) - m)
            parts = []
            for h in range(ATTN_HEADS):
                cs = slice(h * Q_BLOCK, (h + 1) * Q_BLOCK)
                parts.append(jnp.where(sel, e[:, cs], 0.0).astype(BF16))
            p = jnp.concatenate(parts, axis=1)
            acc_s[...] += jnp.dot(vtb[kt], p, preferred_element_type=F32)
            return carry

        if sel_fn is sel_plain:
            for_tiles(body)
        else:
            lax.fori_loop(0, nk, body, no_ties)

    def selected_max(sel_fn):
        mrun[...] = jnp.full(mrun.shape, NEG_BIG, F32)

        def body(kt, carry):
            sel, carry = sel_fn(kt, carry)
            s = logits(kt)
            for h in range(ATTN_HEADS):
                cs = slice(h * Q_BLOCK, (h + 1) * Q_BLOCK)
                sh = jnp.where(sel, s[:, cs], NEG_BIG)
                mrun[:, cs] = jnp.maximum(
                    mrun[:, cs], jnp.max(sh.reshape(KEY_TILE // 8, 8, Q_BLOCK), axis=0))
            return carry

        lax.fori_loop(0, nk, body, no_ties)
        return jnp.max(mrun[...], axis=0, keepdims=True)

    def fast_path():
        q2 = (qa2 * qa2).astype(BF16)
        qsq = lax.dot_general(jnp.ones((8, ATTN_HEAD_DIM), BF16), q2, contract_last,
                              preferred_element_type=F32)[0:1]
        softmax_sum(jnp.sqrt(qsq * kmax[0:1, 0:1]) * logit_scale, sel_plain)
        return (jnp.min(acc_s[dv:dv + 1, :]) > 0.0).astype(I32)

    done = lax.cond(excess, lambda: jnp.int32(0), fast_path) == 1

    @pl.when(jnp.logical_not(done) & excess)
    def _exact_with_ties():
        need = float(TOPK_MAX) - count_cmp(lambda sc: sc > tf)
        sel_fn = functools.partial(sel_ties, need)
        softmax_sum(selected_max(sel_fn), sel_fn)

    @pl.when(jnp.logical_not(done) & jnp.logical_not(excess))
    def _exact_without_ties():
        softmax_sum(selected_max(sel_plain), sel_plain)

    out = acc_s[:dv, :] / acc_s[dv:dv + 1, :]
    for h in range(ATTN_HEADS):
        oh = out[:, h * Q_BLOCK:(h + 1) * Q_BLOCK].T
        o_ref[:, h * ATTN_HEAD_DIM:(h + 1) * ATTN_HEAD_DIM] = oh.astype(BF16)


def _prompt_attention(qa_hm, qi_pm, wi_t, ka, va, kd, tri, weights, batch, seq):
    nq = seq // Q_BLOCK
    nkt = seq // KEY_TILE
    m = batch * seq
    n_heads_q = ATTN_HEADS * Q_BLOCK
    assert seq // COUNT_ROWS <= 256
    n_steps = batch * nq
    slab = lambda b, q: (b * nq + q, 0)
    slab_specs = []
    for w in weights:
        rows = w.shape[0] // n_steps
        assert rows * n_steps == w.shape[0] and rows % 16 == 0
        slab_specs.append(pl.BlockSpec((rows, w.shape[1]), slab))
    in_specs = [
        pl.BlockSpec((ATTN_HEADS, Q_BLOCK, ATTN_HEAD_DIM), lambda b, q: (0, b * nq + q, 0)),
        pl.BlockSpec((IDX_HEADS // 2, Q_BLOCK, LANES), lambda b, q: (0, b * nq + q, 0)),
        pl.BlockSpec((IDX_HEADS, Q_BLOCK), lambda b, q: (0, b * nq + q)),
        pl.BlockSpec((seq, ATTN_HEAD_DIM), lambda b, q: (b, 0)),
        pl.BlockSpec((seq, ATTN_HEAD_DIM), lambda b, q: (b, 0)),
        pl.BlockSpec((seq, LANES), lambda b, q: (b, 0)),
        pl.BlockSpec((KEY_TILE, KEY_TILE), lambda b, q: (0, 0)),
    ] + slab_specs
    return pl.pallas_call(
        _attn_body,
        grid=(batch, nq),
        in_specs=in_specs,
        out_specs=(pl.BlockSpec((Q_BLOCK, ATTN_HEADS * ATTN_HEAD_DIM), slab),) + tuple(slab_specs),
        out_shape=(jax.ShapeDtypeStruct((m, ATTN_HEADS * ATTN_HEAD_DIM), BF16),)
        + tuple(jax.ShapeDtypeStruct(w.shape, BF16) for w in weights),
        scratch_shapes=[
            pltpu.VMEM((seq, ATTN_HEAD_DIM), BF16),
            pltpu.VMEM((nkt, ATTN_HEAD_DIM + SUM_ROWS, KEY_TILE), BF16),
            pltpu.VMEM((nkt, KEY_TILE, Q_BLOCK), F32),
            pltpu.VMEM((8, n_heads_q), F32),
            pltpu.VMEM((ATTN_HEAD_DIM + SUM_ROWS, n_heads_q), F32),
            pltpu.VMEM((8, LANES), F32),
            pltpu.VMEM((nkt, KEY_TILE, Q_BLOCK), BF16),
        ],
        compiler_params=_cparams(("arbitrary", "arbitrary")),
        name="prompt_attn",
    )(qa_hm, qi_pm, wi_t, ka, va, kd, tri, *weights)


def _gate(o, g):
    rn = o * lax.rsqrt(jnp.mean(o * o, axis=-1, keepdims=True) + EPS)
    return rn * (g / (1.0 + jnp.exp(-g)))


def _ret_body(q_ref, k_ref, v_ref, g_ref, decay_ref, rsc_ref, zeta_ref, gpow_ref,
              rg_ref, st_ref):
    c = pl.program_id(1)

    @pl.when(c == 0)
    def _init():
        st_ref[...] = jnp.zeros(st_ref.shape, F32)

    for h in range(RET_HEADS):
        sl = slice(h * 128, (h + 1) * 128)
        q = q_ref[:, sl]
        k = k_ref[:, sl]
        v = v_ref[:, sl]
        r_old = st_ref[0, h]
        qk = lax.dot_general(q, k, (((1,), (1,)), ((), ())), preferred_element_type=F32)
        inner = jnp.dot((qk * decay_ref[h]).astype(BF16), v, preferred_element_type=F32)
        cross = jnp.dot(q, r_old.astype(BF16), preferred_element_type=F32) * rsc_ref[h]
        kz = (k.astype(F32) * zeta_ref[h]).astype(BF16)
        upd = lax.dot_general(kz, v, (((0,), (0,)), ((), ())), preferred_element_type=F32)
        st_ref[0, h] = r_old * gpow_ref[h] + upd
        rg_ref[:, sl] = _gate(inner + cross, g_ref[:, sl].astype(F32)).astype(BF16)


def _prompt_retention(main, decay, rsc, zeta, gpow, batch, seq):
    nc = seq // RET_CHUNK
    m = batch * seq
    width = RET_WIDTH
    const3 = lambda b, c: (0, 0, 0)
    in_specs = [
        pl.BlockSpec((RET_CHUNK, width), lambda b, c: (b * nc + c, 0)),
        pl.BlockSpec((RET_CHUNK, width), lambda b, c: (b * nc + c, 1)),
        pl.BlockSpec((RET_CHUNK, width), lambda b, c: (b * nc + c, 2)),
        pl.BlockSpec((RET_CHUNK, width), lambda b, c: (b * nc + c, 3)),
        pl.BlockSpec((RET_HEADS, RET_CHUNK, RET_CHUNK), const3),
        pl.BlockSpec((RET_HEADS, RET_CHUNK, RET_DV), const3),
        pl.BlockSpec((RET_HEADS, RET_CHUNK, RET_DK), const3),
        pl.BlockSpec((RET_HEADS, 1, RET_DV), const3),
    ]
    return pl.pallas_call(
        _ret_body,
        grid=(batch, nc),
        in_specs=in_specs,
        out_specs=(
            pl.BlockSpec((RET_CHUNK, width), lambda b, c: (b * nc + c, 0)),
            pl.BlockSpec((1, RET_HEADS, RET_DK, RET_DV), lambda b, c: (b, 0, 0, 0)),
        ),
        out_shape=(
            jax.ShapeDtypeStruct((m, width), BF16),
            jax.ShapeDtypeStruct((batch, RET_HEADS, RET_DK, RET_DV), F32),
        ),
        compiler_params=_cparams(("arbitrary", "arbitrary")),
        name="prompt_ret",
    )(main, main, main, main, decay, rsc, zeta, gpow)


def _outproj_body(a_ref, r_ref, wa_ref, wr_ref, x_ref, g2_ref, x1_ref, h2_ref):
    mixed = (jnp.dot(a_ref[...], wa_ref[...], preferred_element_type=F32)
             + jnp.dot(r_ref[...], wr_ref[...], preferred_element_type=F32))
    x1 = x_ref[...] + mixed
    x1_ref[...] = x1
    ms = jnp.mean(x1 * x1, axis=-1, keepdims=True)
    h2_ref[...] = (x1 * lax.rsqrt(ms + EPS) * g2_ref[...]).astype(BF16)


def _out_projection(attn_o, rg, wo, x2d, g2, tm):
    m = x2d.shape[0]
    half = attn_o.shape[1]
    in_specs = [
        pl.BlockSpec((tm, half), lambda i: (i, 0)),
        pl.BlockSpec((tm, half), lambda i: (i, 0)),
        pl.BlockSpec((half, D_MODEL), lambda i: (0, 0)),
        pl.BlockSpec((half, D_MODEL), lambda i: (1, 0)),
        pl.BlockSpec((tm, D_MODEL), lambda i: (i, 0)),
        pl.BlockSpec((1, D_MODEL), lambda i: (0, 0)),
    ]
    return pl.pallas_call(
        _outproj_body,
        grid=(m // tm,),
        in_specs=in_specs,
        out_specs=(pl.BlockSpec((tm, D_MODEL), lambda i: (i, 0)),
                   pl.BlockSpec((tm, D_MODEL), lambda i: (i, 0))),
        out_shape=(jax.ShapeDtypeStruct((m, D_MODEL), F32),
                   jax.ShapeDtypeStruct((m, D_MODEL), BF16)),
        compiler_params=_cparams(("arbitrary",)),
        name="out_proj",
    )(attn_o, rg, wo, wo, x2d, g2)


def _mlp_body(h2_ref, wu_ref, wd_ref, x1_ref, gf_ref, y_ref, acc_ref, side_job=None):
    f = pl.program_id(1)

    @pl.when(f == 0)
    def _init():
        acc_ref[...] = x1_ref[...]

    if side_job is not None:
        side_job()
    u = jnp.dot(h2_ref[...], wu_ref[...], preferred_element_type=F32)
    a = jnp.maximum(u, 0.0)
    acc_ref[...] += jnp.dot((a * a).astype(BF16), wd_ref[...], preferred_element_type=F32)

    @pl.when(f == pl.num_programs(1) - 1)
    def _final():
        x2 = acc_ref[...]
        ms = jnp.mean(x2 * x2, axis=-1, keepdims=True)
        y_ref[...] = x2 * lax.rsqrt(ms + EPS) * gf_ref[...]


def _mlp_ret_body(h2_ref, wu_ref, wd_ref, x1_ref, gf_ref, qkvg_ref, st_ref, gam_ref,
                  y_ref, rg_ref, so_ref, acc_ref):
    _mlp_body(h2_ref, wu_ref, wd_ref, x1_ref, gf_ref, y_ref, acc_ref,
              side_job=lambda: _sret_body(qkvg_ref, st_ref, gam_ref, rg_ref, so_ref))


def _mlp_with_sample_retention(h2, wu, wd, x1, gf, qkvg, state, gam, tm, tf):
    m = h2.shape[0]
    nb = state.shape[0]
    ni, nf = m // tm, D_FF // tf
    ns = nb // (ni * nf)
    assert ns * ni * nf == nb
    step = lambda i, f: (i * nf + f, 0, 0)
    return pl.pallas_call(
        _mlp_ret_body,
        grid=(ni, nf),
        in_specs=[
            pl.BlockSpec((tm, D_MODEL), lambda i, f: (i, 0)),
            pl.BlockSpec((D_MODEL, tf), lambda i, f: (0, f)),
            pl.BlockSpec((tf, D_MODEL), lambda i, f: (f, 0)),
            pl.BlockSpec((tm, D_MODEL), lambda i, f: (i, 0)),
            pl.BlockSpec((1, D_MODEL), lambda i, f: (0, 0)),
            pl.BlockSpec((ns, 32, LANES), step),
            pl.BlockSpec((ns, RET_HEADS, RET_DK, RET_DV), lambda i, f: (i * nf + f, 0, 0, 0)),
            pl.BlockSpec((RET_HEADS, 1, LANES), lambda i, f: (0, 0, 0)),
        ],
        out_specs=(
            pl.BlockSpec((tm, D_MODEL), lambda i, f: (i, 0)),
            pl.BlockSpec((ns, RET_HEADS, RET_DV), step),
            pl.BlockSpec((ns, RET_HEADS, RET_DK, RET_DV), lambda i, f: (i * nf + f, 0, 0, 0)),
        ),
        out_shape=(
            jax.ShapeDtypeStruct((m, D_MODEL), F32),
            jax.ShapeDtypeStruct((nb, RET_HEADS, RET_DV), BF16),
            jax.ShapeDtypeStruct(state.shape, F32),
        ),
        scratch_shapes=[pltpu.VMEM((tm, D_MODEL), F32)],
        compiler_params=_cparams(("arbitrary", "arbitrary")),
        name="mlp_ret",
    )(h2, wu, wd, x1, gf, qkvg, state, gam)


def _mlp(h2, wu, wd, x1, gf, tm, tf):
    m = h2.shape[0]
    in_specs = [
        pl.BlockSpec((tm, D_MODEL), lambda i, f: (i, 0)),
        pl.BlockSpec((D_MODEL, tf), lambda i, f: (0, f)),
        pl.BlockSpec((tf, D_MODEL), lambda i, f: (f, 0)),
        pl.BlockSpec((tm, D_MODEL), lambda i, f: (i, 0)),
        pl.BlockSpec((1, D_MODEL), lambda i, f: (0, 0)),
    ]
    return pl.pallas_call(
        _mlp_body,
        grid=(m // tm, D_FF // tf),
        in_specs=in_specs,
        out_specs=pl.BlockSpec((tm, D_MODEL), lambda i, f: (i, 0)),
        out_shape=jax.ShapeDtypeStruct((m, D_MODEL), F32),
        scratch_shapes=[pltpu.VMEM((tm, D_MODEL), F32)],
        compiler_params=_cparams(("arbitrary", "arbitrary")),
        name="mlp",
    )(h2, wu, wd, x1, gf)


def _fetch_pages(pt_ref, step, slot, streams, start):
    n_pages = pt_ref.shape[1]
    for hbm, buf, sem in streams:
        for j in range(n_pages):
            cp = pltpu.make_async_copy(hbm.at[pt_ref[step, j]], buf.at[slot, j], sem.at[slot])
            if start:
                cp.start()
            else:
                cp.wait()


def _paged_loop(pt_ref, streams, step_fn):
    nb = pt_ref.shape[0]
    n_slots = streams[0][1].shape[0]
    ahead = n_slots - PAGE_GROUP
    assert nb % PAGE_GROUP == 0 and ahead % PAGE_GROUP == 0 and PAGE_GROUP <= ahead <= nb
    for s in range(ahead):
        _fetch_pages(pt_ref, s, s, streams, start=True)

    def body(g, carry):
        b0 = g * PAGE_GROUP

        @pl.when(b0 + ahead < nb)
        def _next():
            for i in range(PAGE_GROUP):
                row = b0 + ahead + i
                _fetch_pages(pt_ref, row, row % n_slots, streams, start=True)

        for i in range(PAGE_GROUP):
            _fetch_pages(pt_ref, b0 + i, (b0 + i) % n_slots, streams, start=False)
        for i in range(PAGE_GROUP):
            step_fn(b0 + i, (b0 + i) % n_slots, i)
        return carry

    lax.fori_loop(0, nb // PAGE_GROUP, body, 0)


def _sidx_body(pt_ref, qi_ref, w_ref, kin_ref, cache_hbm, out_ref, kt_s, pbuf, sem):
    n_pages = pt_ref.shape[1]
    page = pbuf.shape[3]
    past = n_pages * page
    lane = lax.broadcasted_iota(I32, (1, LANES), 1)

    def step(b, slot, lane_of_trip):
        qi = qi_ref[b]
        w = w_ref[b] * (IDX_HEAD_DIM ** -0.5)
        for j in range(n_pages):
            kt_s[lane_of_trip, :, j * page:(j + 1) * page] = pbuf[slot, j].astype(BF16)
        s = jnp.dot(qi, kt_s[lane_of_trip], preferred_element_type=F32)
        out_ref[b, :, 0:past] = jnp.sum(jnp.maximum(s, 0.0) * w, axis=0, keepdims=True)
        sn = jnp.sum(qi.astype(F32) * kin_ref[b].astype(BF16).astype(F32), axis=1, keepdims=True)
        rn = jnp.sum(jnp.maximum(sn, 0.0) * w, axis=0, keepdims=True)
        out_ref[b, :, past:past + LANES] = jnp.where(lane == 0, rn, -jnp.inf)

    _paged_loop(pt_ref, ((cache_hbm, pbuf, sem),), step)


def _sample_index_scores(page_table, qi_s, wi_s, ki_s, cache_idx_k_t):
    nb, n_pages = page_table.shape
    page = cache_idx_k_t.shape[2]
    width = n_pages * page + LANES

    vmem = pl.BlockSpec(memory_space=pltpu.VMEM)
    return pl.pallas_call(
        _sidx_body,
        in_specs=[pl.BlockSpec(memory_space=pltpu.SMEM), vmem, vmem, vmem,
                  pl.BlockSpec(memory_space=pl.ANY)],
        out_specs=vmem,
        out_shape=jax.ShapeDtypeStruct((nb, 1, width), F32),
        scratch_shapes=[pltpu.VMEM((PAGE_GROUP, IDX_HEAD_DIM, n_pages * page), BF16),
                        pltpu.VMEM((IDX_PAGE_SLOTS, n_pages, IDX_HEAD_DIM, page), F32),
                        pltpu.SemaphoreType.DMA((IDX_PAGE_SLOTS,))],
        compiler_params=pltpu.CompilerParams(vmem_limit_bytes=VMEM_LIMIT),
        name="sample_idx",
    )(page_table, qi_s, wi_s, ki_s, cache_idx_k_t)


def _ssel_body(sc_ref, tri_ref, sel_ref):
    rows, width = sc_ref.shape
    nt = width // LANES
    n_valid = (nt - 1) * LANES + 1

    def tile(kt):
        return sc_ref[:, kt * LANES:(kt + 1) * LANES]

    def count_cmp(cmp):
        acc = jnp.zeros((rows, LANES), F32)
        for kt in range(nt):
            acc = acc + jnp.where(cmp(tile(kt)), 1.0, 0.0)
        return jnp.broadcast_to(jnp.sum(acc, axis=1, keepdims=True), (rows, LANES))

    tf = _threshold_search(lambda c: count_cmp(lambda sc: sc >= c), 32, (rows, LANES))
    need = float(TOPK_MAX) - count_cmp(lambda sc: sc > tf)
    tie_off = jnp.zeros((rows, LANES), F32)
    for kt in range(nt):
        col = kt * LANES + lax.broadcasted_iota(I32, (rows, LANES), 1)
        sc = tile(kt)
        eq = sc == tf
        tie = jnp.where(eq, 1.0, 0.0)
        rank = jnp.dot(tie.astype(BF16), tri_ref[...], preferred_element_type=F32) + tie_off
        sel = ((sc > tf) | (eq & (rank <= need))) & (col < n_valid)
        sel_ref[:, kt * LANES:(kt + 1) * LANES] = jnp.where(sel, 1.0, 0.0)
        tie_off = tie_off + jnp.broadcast_to(jnp.sum(tie, axis=1, keepdims=True), (rows, LANES))


def _sample_select(scores2d, tri):
    rows, width = scores2d.shape
    return pl.pallas_call(
        _ssel_body,
        out_shape=jax.ShapeDtypeStruct((rows, width), F32),
        compiler_params=pltpu.CompilerParams(vmem_limit_bytes=VMEM_LIMIT),
        name="sample_select",
    )(scores2d, tri)


def _sattn_body(pt_ref, q_ref, sel_ref, kn_ref, vn_ref, ck_hbm, cv_hbm, o_ref,
                kbuf, vbuf, ksem, vsem):
    n_pages = pt_ref.shape[1]
    page = kbuf.shape[2]
    past = n_pages * page
    scale = ATTN_HEAD_DIM ** -0.5

    def step(b, slot, lane_of_trip):
        q = q_ref[b]
        k_all = kbuf[slot].reshape(past, ATTN_HEAD_DIM).astype(BF16)
        v_all = vbuf[slot].reshape(past, ATTN_HEAD_DIM).astype(BF16)
        s = lax.dot_general(q, k_all, (((1,), (1,)), ((), ())), preferred_element_type=F32)
        s = jnp.where(sel_ref[b, :, 0:past] > 0.5, s * scale, NEG_BIG)
        kn = kn_ref[b].astype(BF16).astype(F32)
        sn = jnp.sum(q.astype(F32) * kn, axis=1, keepdims=True) * scale
        sn = jnp.where(sel_ref[b, :, past:past + 1] > 0.5, sn, NEG_BIG)
        m = jnp.maximum(jnp.max(s, axis=1, keepdims=True), sn)
        pn = jnp.exp(sn - m)
        p = jnp.exp(s - m)
        l = pn + jnp.sum(p, axis=1, keepdims=True)
        acc = (pn * vn_ref[b].astype(BF16).astype(F32)
               + jnp.dot(p.astype(BF16), v_all, preferred_element_type=F32))
        o_ref[b] = (acc / l).astype(BF16)

    _paged_loop(pt_ref, ((ck_hbm, kbuf, ksem), (cv_hbm, vbuf, vsem)), step)


def _sample_attention(page_table, qa_s, sel3, ka_s, va_s, cache_k, cache_v):
    nb, n_pages = page_table.shape
    page = cache_k.shape[1]

    vmem = pl.BlockSpec(memory_space=pltpu.VMEM)
    hbm = pl.BlockSpec(memory_space=pl.ANY)
    return pl.pallas_call(
        _sattn_body,
        in_specs=[pl.BlockSpec(memory_space=pltpu.SMEM), vmem, vmem, vmem, vmem, hbm, hbm],
        out_specs=vmem,
        out_shape=jax.ShapeDtypeStruct((nb, ATTN_HEADS, ATTN_HEAD_DIM), BF16),
        scratch_shapes=[pltpu.VMEM((ATTN_PAGE_SLOTS, n_pages, page, ATTN_HEAD_DIM), F32),
                        pltpu.VMEM((ATTN_PAGE_SLOTS, n_pages, page, ATTN_HEAD_DIM), F32),
                        pltpu.SemaphoreType.DMA((ATTN_PAGE_SLOTS,)),
                        pltpu.SemaphoreType.DMA((ATTN_PAGE_SLOTS,))],
        compiler_params=pltpu.CompilerParams(vmem_limit_bytes=VMEM_LIMIT),
        name="sample_attn",
    )(page_table, qa_s, sel3, ka_s, va_s, cache_k, cache_v)


def _sret_body(qkvg_ref, st_ref, gam_ref, rg_ref, so_ref):
    ns = st_ref.shape[0]
    for s in range(ns):
        blk = qkvg_ref[s].astype(F32)
        q8 = blk[0:8]
        k8 = blk[8:16]
        v8 = blk[16:24]
        g8 = blk[24:32]
        q_t = q8.T
        k_t = k8.T
        qk = jnp.sum(q8 * k8, axis=1, keepdims=True)
        rows = []
        for h in range(RET_HEADS):
            r_old = st_ref[s, h]
            gam = gam_ref[h]
            qcol = jnp.broadcast_to(q_t[:, h:h + 1], (RET_DK, RET_DV))
            kcol = jnp.broadcast_to(k_t[:, h:h + 1], (RET_DK, RET_DV))
            vrow = v8[h:h + 1]
            q_r = jnp.sum(qcol * r_old, axis=0, keepdims=True)
            rows.append(gam * q_r + qk[h:h + 1] * vrow)
            so_ref[s, h] = gam * r_old + kcol * vrow
        ret = jnp.concatenate(rows, axis=0)
        rg_ref[s] = _gate(ret, g8).astype(BF16)


def _rotary_table(pos):
    half = RET_DK // 2
    inv = ROPE_BASE ** (-np.arange(half, dtype=np.float64) / half)
    ang = np.asarray(pos, np.float64)[:, None] * inv[None, :]
    cos = np.cos(ang)
    sin = np.sin(ang)
    return jnp.asarray(np.concatenate([cos, cos, -sin, sin], axis=1), F32)


def _retention_constants():
    lg = np.log1p(-np.exp2(-5.0 - np.arange(RET_HEADS, dtype=np.float64)))
    n = RET_CHUNK
    i = np.arange(n, dtype=np.float64)
    diff = i[:, None] - i[None, :]
    decay = np.where(diff[None] >= 0, np.exp(np.maximum(diff, 0.0)[None] * lg[:, None, None]), 0.0)
    rsc = np.exp((i + 1.0)[None, :] * lg[:, None])
    zeta = np.exp((n - 1.0 - i)[None, :] * lg[:, None])
    gpow = np.exp(n * lg)
    gam1 = np.exp(lg)
    rsc_b = np.broadcast_to(rsc[:, :, None], (RET_HEADS, n, RET_DV))
    zeta_b = np.broadcast_to(zeta[:, :, None], (RET_HEADS, n, RET_DK))
    gpow_b = np.broadcast_to(gpow[:, None, None], (RET_HEADS, 1, RET_DV))
    gam1_b = np.broadcast_to(gam1[:, None, None], (RET_HEADS, 1, LANES))
    return tuple(jnp.asarray(a, F32) for a in (decay, rsc_b, zeta_b, gpow_b, gam1_b))


def _upper_tri(n):
    return jnp.asarray(np.triu(np.ones((n, n), np.float32)), BF16)


def _lower_tri(n):
    return jnp.asarray(np.tril(np.ones((n, n), np.float32)), BF16)


def _pad_lanes(v):
    return jnp.pad(v, (0, LANES - v.shape[0])).reshape(1, LANES)


def kernel(x_prompt, x_sample, cache_k, cache_v, cache_idx_k, state_ret, page_table,
           norm1_g, w_in, idx_k_norm_g, idx_k_norm_b, w_out, norm2_g, w_up, w_down, final_norm_g):
    batch, seq, _ = x_prompt.shape
    nb = x_sample.shape[0]
    past_len = page_table.shape[1] * cache_k.shape[1]
    half_mix = ATTN_HEADS * ATTN_HEAD_DIM

    wt = w_in.T.astype(BF16)
    g1 = norm1_g.reshape(1, D_MODEL)
    g2 = norm2_g.reshape(1, D_MODEL)
    gf = final_norm_g.reshape(1, D_MODEL)
    lng = _pad_lanes(idx_k_norm_g)
    lnb = _pad_lanes(idx_k_norm_b)
    decay, rsc_b, zeta_b, gpow_b, gam1_b = _retention_constants()

    xp = x_prompt.reshape(batch * seq, D_MODEL)
    xs = x_sample.reshape(nb, D_MODEL)
    cs_p = _rotary_table(np.arange(seq))
    cs_s = _rotary_table(np.full((nb,), past_len))
    (qa_p, qi_p, ka_p, va_p, ki_p, kd_p, wi_p), (qa_s, qi_s, ka_s, va_s, ki_s, _, wi_s) = (
        _project_attn(xp, xs, g1, wt, lng, lnb, tm=TILES["proj_attn_rows"]))
    main_p, main_s = _project_ret(xp, xs, g1, wt, cs_p, cs_s, tm=TILES["proj_ret_rows"])

    attn_p, wu, wd, wo = _prompt_attention(qa_p, qi_p, wi_p, ka_p, va_p, kd_p,
                                           _lower_tri(KEY_TILE), (w_up, w_down, w_out), batch, seq)
    rg_p, ret_state_p = _prompt_retention(main_p, decay, rsc_b, zeta_b, gpow_b, batch, seq)
    x1_p, h2_p = _out_projection(attn_p, rg_p, wo, xp, g2, tm=TILES["out_proj_rows"])

    scores = _sample_index_scores(
        page_table,
        qi_s.transpose(1, 0, 2).reshape(nb, IDX_HEADS, IDX_HEAD_DIM),
        wi_s.T.reshape(nb, IDX_HEADS, 1),
        ki_s.reshape(nb, 1, IDX_HEAD_DIM),
        jnp.swapaxes(cache_idx_k, 1, 2))
    width = scores.shape[2]
    sel = _sample_select(scores.reshape(nb, width), _upper_tri(LANES))
    attn_s = _sample_attention(
        page_table,
        qa_s.transpose(1, 0, 2),
        sel.reshape(nb, 1, width),
        ka_s.reshape(nb, 1, ATTN_HEAD_DIM),
        va_s.reshape(nb, 1, ATTN_HEAD_DIM),
        cache_k, cache_v)
    y_p, rg_s, ret_state_s = _mlp_with_sample_retention(
        h2_p, wu, wd, x1_p, gf, main_s.reshape(nb, 32, LANES), state_ret, gam1_b,
        tm=TILES["mlp_rows"], tf=TILES["mlp_ff"])
    x1_s, h2_s = _out_projection(attn_s.reshape(nb, half_mix), rg_s.reshape(nb, RET_WIDTH),
                                 wo, xs, g2, tm=nb)
    y_s = _mlp(h2_s, wu, wd, x1_s, gf, tm=nb, tf=TILES["mlp_ff"])

    return (
        y_p.reshape(batch, seq, D_MODEL),
        y_s.reshape(nb, 1, D_MODEL),
        ka_p.reshape(batch, seq, ATTN_HEAD_DIM),
        va_p.reshape(batch, seq, ATTN_HEAD_DIM),
        ki_p.reshape(batch, seq, IDX_HEAD_DIM),
        ret_state_p,
        ka_s.reshape(nb, 1, ATTN_HEAD_DIM),
        va_s.reshape(nb, 1, ATTN_HEAD_DIM),
        ki_s.reshape(nb, 1, IDX_HEAD_DIM),
        ret_state_s,
    )
```

```python
import functools

import numpy as np
import jax
import jax.numpy as jnp
from jax import lax
from jax.experimental import pallas as pl
from jax.experimental.pallas import tpu as pltpu

F32 = jnp.float32
BF16 = jnp.bfloat16
I32 = jnp.int32

D_MODEL = 2048
ATTN_HEADS = 8
ATTN_HEAD_DIM = 128
IDX_HEADS = 16
IDX_HEAD_DIM = 64
TOPK_MAX = 256
RET_HEADS = 8
RET_DK = 128
RET_DV = 128
RET_CHUNK = 256
ROPE_BASE = 10000.0
D_FF = 4 * D_MODEL
EPS = 1e-6
Q_BLOCK = 256

OFF_QA, OFF_KA, OFF_VA, OFF_QI, OFF_KI, OFF_WI = 0, 1024, 1152, 1280, 2304, 2368
OFF_QR, OFF_KR, OFF_VR, OFF_GR = 2384, 3408, 4432, 5456
RET_WIDTH = RET_HEADS * RET_DV

LANES = 128
PROJ_TILE = 512
KEY_TILE = 256
COUNT_ROWS = 32
PAGE_GROUP = 2
IDX_PAGE_SLOTS = 8
ATTN_PAGE_SLOTS = 6
SUM_ROWS = 16
LOG2_E = 1.4426950408889634
INT_MIN = -2 ** 31
KEY_NEG_INF = -2 ** 31 + 0x7FFFFF
BF16_KEY_NEG_INF = -2 ** 15 + 0x7F
NEG_BIG = -1e30
VMEM_LIMIT = 56 * 1024 * 1024

TILES = {
    "proj_attn_rows": 1024,
    "proj_ret_rows": 512,
    "out_proj_rows": 512,
    "mlp_rows": 512,
    "mlp_ff": 1024,
}


def _cparams(sem):
    return pltpu.CompilerParams(dimension_semantics=sem, vmem_limit_bytes=VMEM_LIMIT)


def _resident(shape):
    zeros = (0,) * len(shape)
    return pl.BlockSpec(shape, lambda *_: zeros, pipeline_mode=pl.Buffered(1))


def _normed_input(x_ref, g_ref, xn_ref):
    x = x_ref[...]
    ms = jnp.mean(x * x, axis=-1, keepdims=True)
    xn_ref[...] = (x * lax.rsqrt(ms + EPS) * g_ref[...]).astype(BF16)


def _matmul_rows(xn, wt_ref, r0, n):
    return lax.dot_general(xn, wt_ref[r0:r0 + n, :], (((1,), (1,)), ((), ())),
                           preferred_element_type=F32)


def _with_sample_rows(x_ref, xs_ref, g_ref, xn_ref, compute):
    tm, ns = x_ref.shape[0], xs_ref.shape[0]
    last = pl.program_id(0) == pl.num_programs(0) - 1

    @pl.when(last)
    def _prompt_and_sample_rows():
        _normed_input(x_ref, g_ref, xn_ref.at[pl.ds(0, tm)])
        _normed_input(xs_ref, g_ref, xn_ref.at[pl.ds(tm, ns)])
        compute(xn_ref[...], with_samples=True)

    @pl.when(jnp.logical_not(last))
    def _prompt_rows():
        _normed_input(x_ref, g_ref, xn_ref.at[pl.ds(0, tm)])
        compute(xn_ref[pl.ds(0, tm), :], with_samples=False)


def _proj_attn_body(x_ref, xs_ref, g_ref, wt_ref, lng_ref, lnb_ref, *refs):
    outs_p, outs_s, xn_ref = refs[0:7], refs[7:14], refs[14]
    tm, ns = x_ref.shape[0], xs_ref.shape[0]

    def compute(xn, with_samples):
        parts = [((0, tm), outs_p)] + ([((tm, tm + ns), outs_s)] if with_samples else [])
        mm = functools.partial(_matmul_rows, xn, wt_ref)
        for t in range(ATTN_HEADS * ATTN_HEAD_DIM // PROJ_TILE):
            acc = mm(OFF_QA + t * PROJ_TILE, PROJ_TILE)
            for hh in range(PROJ_TILE // LANES):
                piece = acc[:, hh * LANES:(hh + 1) * LANES].astype(BF16)
                for (a, b), o in parts:
                    o[0][4 * t + hh] = piece[a:b]
        for t in range(IDX_HEADS * IDX_HEAD_DIM // PROJ_TILE):
            acc = mm(OFF_QI + t * PROJ_TILE, PROJ_TILE)
            for hh in range(PROJ_TILE // LANES):
                piece = acc[:, hh * LANES:(hh + 1) * LANES].astype(BF16)
                for (a, b), o in parts:
                    o[1][4 * t + hh] = piece[a:b]
        kv = mm(OFF_KA, 2 * ATTN_HEAD_DIM)
        kw = mm(OFF_KI, LANES)
        lane = lax.broadcasted_iota(I32, kw.shape, 1)
        is_k = lane < IDX_HEAD_DIM
        mu = jnp.sum(jnp.where(is_k, kw, 0.0), axis=-1, keepdims=True) * (1.0 / IDX_HEAD_DIM)
        d = jnp.where(is_k, kw - mu, 0.0)
        var = jnp.sum(d * d, axis=-1, keepdims=True) * (1.0 / IDX_HEAD_DIM)
        kn = d * lax.rsqrt(var + EPS) * lng_ref[...] + lnb_ref[...]
        kd = jnp.where(is_k, kn, pltpu.roll(kn, IDX_HEAD_DIM, 1)).astype(BF16)
        wi = kw[:, IDX_HEAD_DIM:IDX_HEAD_DIM + IDX_HEADS] * (IDX_HEADS ** -0.5)
        for (a, b), o in parts:
            o[2][...] = kv[a:b, :ATTN_HEAD_DIM]
            o[3][...] = kv[a:b, ATTN_HEAD_DIM:]
            o[4][...] = kn[a:b, :IDX_HEAD_DIM]
            o[5][...] = kd[a:b]
            o[6][...] = wi[a:b].T

    _with_sample_rows(x_ref, xs_ref, g_ref, xn_ref, compute)


def _proj_ret_body(x_ref, xs_ref, g_ref, wt_ref, cs_ref, css_ref, main_ref, mains_ref, xn_ref):
    tm, ns = x_ref.shape[0], xs_ref.shape[0]

    def compute(xn, with_samples):
        parts = [((0, tm), main_ref)] + ([((tm, tm + ns), mains_ref)] if with_samples else [])
        cs = jnp.concatenate([cs_ref[...], css_ref[...]], axis=0) if with_samples else cs_ref[...]
        cosf = cs[:, :LANES]
        sinf = cs[:, LANES:]
        tiles = RET_WIDTH // PROJ_TILE
        for seg, (off, scale) in enumerate(((OFF_QR, None), (OFF_KR, RET_DK ** -0.5))):
            for t in range(tiles):
                acc = _matmul_rows(xn, wt_ref, off + t * PROJ_TILE, PROJ_TILE)
                for hh in range(PROJ_TILE // LANES):
                    xh = acc[:, hh * LANES:(hh + 1) * LANES]
                    r = xh * cosf + pltpu.roll(xh, RET_DK // 2, 1) * sinf
                    if scale is not None:
                        r = r * scale
                    c0 = seg * RET_WIDTH + t * PROJ_TILE + hh * LANES
                    r = r.astype(BF16)
                    for (a, b), o in parts:
                        o[:, c0:c0 + LANES] = r[a:b]
        for seg, off in ((2, OFF_VR), (3, OFF_GR)):
            for t in range(tiles):
                acc = _matmul_rows(xn, wt_ref, off + t * PROJ_TILE, PROJ_TILE).astype(BF16)
                c0 = seg * RET_WIDTH + t * PROJ_TILE
                for (a, b), o in parts:
                    o[:, c0:c0 + PROJ_TILE] = acc[a:b]

    _with_sample_rows(x_ref, xs_ref, g_ref, xn_ref, compute)


def _project_attn(x2d, xs2d, g1, wt, lng, lnb, tm):
    m, ns = x2d.shape[0], xs2d.shape[0]
    row = lambda i: (i, 0)

    def shapes(n):
        return (
            jax.ShapeDtypeStruct((ATTN_HEADS, n, ATTN_HEAD_DIM), BF16),
            jax.ShapeDtypeStruct((IDX_HEADS // 2, n, LANES), BF16),
            jax.ShapeDtypeStruct((n, ATTN_HEAD_DIM), F32),
            jax.ShapeDtypeStruct((n, ATTN_HEAD_DIM), F32),
            jax.ShapeDtypeStruct((n, IDX_HEAD_DIM), F32),
            jax.ShapeDtypeStruct((n, LANES), BF16),
            jax.ShapeDtypeStruct((IDX_HEADS, n), F32),
        )

    prompt_specs = (
        pl.BlockSpec((ATTN_HEADS, tm, ATTN_HEAD_DIM), lambda i: (0, i, 0)),
        pl.BlockSpec((IDX_HEADS // 2, tm, LANES), lambda i: (0, i, 0)),
        pl.BlockSpec((tm, ATTN_HEAD_DIM), row),
        pl.BlockSpec((tm, ATTN_HEAD_DIM), row),
        pl.BlockSpec((tm, IDX_HEAD_DIM), row),
        pl.BlockSpec((tm, LANES), row),
        pl.BlockSpec((IDX_HEADS, tm), lambda i: (0, i)),
    )
    sample_specs = tuple(pl.BlockSpec(s.shape, lambda i, nd=len(s.shape): (0,) * nd)
                         for s in shapes(ns))
    outs = pl.pallas_call(
        _proj_attn_body,
        grid=(m // tm,),
        in_specs=[pl.BlockSpec((tm, D_MODEL), row), _resident((ns, D_MODEL)),
                  _resident((1, D_MODEL)), _resident((OFF_QR + LANES, D_MODEL)),
                  _resident((1, LANES)), _resident((1, LANES))],
        out_specs=prompt_specs + sample_specs,
        out_shape=shapes(m) + shapes(ns),
        scratch_shapes=[pltpu.VMEM((tm + ns, D_MODEL), BF16)],
        compiler_params=_cparams(("arbitrary",)),
        name="proj_attn",
    )(x2d, xs2d, g1, wt, lng, lnb)
    return outs[:7], outs[7:]


def _project_ret(x2d, xs2d, g1, wt, cs, css, tm):
    m, ns = x2d.shape[0], xs2d.shape[0]
    n_pos_blocks = cs.shape[0] // tm
    row = lambda i: (i, 0)
    return pl.pallas_call(
        _proj_ret_body,
        grid=(m // tm,),
        in_specs=[pl.BlockSpec((tm, D_MODEL), row), _resident((ns, D_MODEL)),
                  _resident((1, D_MODEL)), _resident(wt.shape),
                  pl.BlockSpec((tm, 2 * LANES), lambda i: (i % n_pos_blocks, 0)),
                  _resident((ns, 2 * LANES))],
        out_specs=(pl.BlockSpec((tm, 4 * RET_WIDTH), row),
                   pl.BlockSpec((ns, 4 * RET_WIDTH), lambda i: (0, 0))),
        out_shape=(jax.ShapeDtypeStruct((m, 4 * RET_WIDTH), BF16),
                   jax.ShapeDtypeStruct((ns, 4 * RET_WIDTH), BF16)),
        scratch_shapes=[pltpu.VMEM((tm + ns, D_MODEL), BF16)],
        compiler_params=_cparams(("arbitrary",)),
        name="proj_ret",
    )(x2d, xs2d, g1, wt, cs, css)


def _key_to_float(key):
    bits = key ^ ((key >> 31) & 0x7FFFFFFF)
    return lax.bitcast_convert_type(bits, F32)


def _threshold_search(count_ge, n_iter, shape):
    def body(it, t):
        bit = lax.shift_left(jnp.int32(1), 31 - it)
        cand = t ^ bit
        cnt = count_ge(_key_to_float(cand))
        return jnp.where(cnt >= float(TOPK_MAX), cand, t)

    t = lax.fori_loop(0, n_iter, body, jnp.full(shape, INT_MIN, I32))
    return _key_to_float(jnp.maximum(t, KEY_NEG_INF))


def _bf16_key_to_f32_key(k16):
    return lax.shift_left(k16, 16) | jnp.where(k16 < 0, 0xFFFF, 0)


def _threshold_search_coarse_fine(count_ge_bf16, count_ge, run, shape):
    def coarse(it, u):
        cand = u | lax.shift_left(jnp.int32(1), 15 - it)
        c = _key_to_float(_bf16_key_to_f32_key(cand - 32768)).astype(BF16)
        return jnp.where(count_ge_bf16(c) >= float(TOPK_MAX), cand, u)

    u = lax.fori_loop(0, jnp.where(run, 16, 0), coarse, jnp.zeros(shape, I32))
    k1 = jnp.maximum(u - 32768, BF16_KEY_NEG_INF)
    lo = _bf16_key_to_f32_key(jnp.maximum(k1 - 1, -32768))
    hi = _bf16_key_to_f32_key(jnp.minimum(k1 + 1, 32767))

    def fine(it, t):
        cand = t + lax.shift_left(jnp.int32(1), 16 - it)
        ok = (cand < hi) & (count_ge(_key_to_float(cand)) >= float(TOPK_MAX))
        return jnp.where(ok, cand, t)

    t = lax.fori_loop(0, jnp.where(run, 17, 0), fine, lo)
    return _key_to_float(jnp.maximum(t, KEY_NEG_INF))


def _attn_body(qa_ref, qi_ref, wit_ref, ka_ref, va_ref, kd_ref, tri_ref, wu_ref, wd_ref, wo_ref,
               o_ref, wub_ref, wdb_ref, wob_ref, kbf, vtb, scr, mrun, acc_s, kmax, scr16):
    wub_ref[...] = wu_ref[...].astype(BF16)
    wdb_ref[...] = wd_ref[...].astype(BF16)
    wob_ref[...] = wo_ref[...].astype(BF16)

    qb = pl.program_id(1)
    n_heads_q = ATTN_HEADS * Q_BLOCK
    n_pairs = IDX_HEADS // 2
    dv = ATTN_HEAD_DIM
    logit_scale = ATTN_HEAD_DIM ** -0.5 * LOG2_E

    @pl.when(qb == 0)
    def _cast():
        ka = ka_ref[...]
        kbf[...] = ka.astype(BF16)
        kmax[...] = jnp.broadcast_to(jnp.max(jnp.sum(ka * ka, axis=1, keepdims=True)), kmax.shape)
        for kt in range(vtb.shape[0]):
            vtb[kt, :dv] = va_ref[kt * KEY_TILE:(kt + 1) * KEY_TILE, :].T.astype(BF16)
            vtb[kt, dv:] = jnp.ones((vtb.shape[1] - dv, KEY_TILE), BF16)

    nk = ((qb + 1) * Q_BLOCK + KEY_TILE - 1) // KEY_TILE
    wt = wit_ref[...] * (IDX_HEAD_DIM ** -0.5)
    qi2 = qi_ref[...].reshape(n_pairs * Q_BLOCK, LANES)
    lo_half = lax.broadcasted_iota(I32, (KEY_TILE, LANES), 1) < IDX_HEAD_DIM
    qidx = qb * Q_BLOCK + lax.broadcasted_iota(I32, (KEY_TILE, Q_BLOCK), 1)
    kidx0 = lax.broadcasted_iota(I32, (KEY_TILE, Q_BLOCK), 0)
    contract_last = (((1,), (1,)), ((), ()))

    def idx_body(kt, carry):
        off = pl.multiple_of(kt * KEY_TILE, KEY_TILE)
        kit = kd_ref[pl.ds(off, KEY_TILE), :]
        zero = jnp.zeros_like(kit)
        s_even = lax.dot_general(jnp.where(lo_half, kit, zero), qi2, contract_last,
                                 preferred_element_type=F32)
        s_odd = lax.dot_general(jnp.where(lo_half, zero, kit), qi2, contract_last,
                                preferred_element_type=F32)
        score = jnp.zeros((KEY_TILE, Q_BLOCK), F32)
        for g in range(n_pairs):
            cs = slice(g * Q_BLOCK, (g + 1) * Q_BLOCK)
            score = score + jnp.maximum(s_even[:, cs], 0.0) * wt[2 * g:2 * g + 1, :]
            score = score + jnp.maximum(s_odd[:, cs], 0.0) * wt[2 * g + 1:2 * g + 2, :]
        score = jnp.where(kidx0 + off <= qidx, score, -jnp.inf)
        scr[kt] = score
        scr16[kt] = score.astype(BF16)
        return carry

    def for_tiles(fn):
        def pair(j, carry):
            fn(2 * j, 0)
            fn(2 * j + 1, 0)
            return carry

        lax.fori_loop(0, nk // 2, pair, 0)

        @pl.when(nk % 2 == 1)
        def _last():
            fn(nk - 1, 0)

    for_tiles(idx_body)

    def count_ge_bf16(c):
        def body(kt, acc):
            hit = scr16[kt] >= c
            for r in range(KEY_TILE // COUNT_ROWS):
                acc = jnp.where(hit[r * COUNT_ROWS:(r + 1) * COUNT_ROWS], acc + 1.0, acc)
            return acc
        acc = lax.fori_loop(0, nk, body, jnp.zeros((COUNT_ROWS, Q_BLOCK), BF16))
        return jnp.sum(acc.astype(F32), axis=0, keepdims=True)

    def count_cmp(cmp):
        def body(kt, acc):
            hit = cmp(scr[kt])
            for r in range(KEY_TILE // COUNT_ROWS):
                acc = jnp.where(hit[r * COUNT_ROWS:(r + 1) * COUNT_ROWS], acc + 1.0, acc)
            return acc
        acc = lax.fori_loop(0, nk, body, jnp.zeros((COUNT_ROWS, Q_BLOCK), F32))
        return jnp.sum(acc, axis=0, keepdims=True)

    tf = _threshold_search_coarse_fine(
        count_ge_bf16, lambda c: count_cmp(lambda sc: sc >= c),
        qb >= TOPK_MAX // Q_BLOCK, (1, Q_BLOCK))
    excess = jnp.max(count_cmp(lambda sc: sc >= tf)) > float(TOPK_MAX)

    qa2 = qa_ref[...].reshape(n_heads_q, ATTN_HEAD_DIM)

    def logits(kt):
        off = pl.multiple_of(kt * KEY_TILE, KEY_TILE)
        s = lax.dot_general(kbf[pl.ds(off, KEY_TILE), :], qa2, contract_last,
                            preferred_element_type=F32)
        return s * logit_scale

    def sel_plain(kt, carry):
        off = pl.multiple_of(kt * KEY_TILE, KEY_TILE)
        return (scr[kt] >= tf) & (kidx0 + off <= qidx), carry

    def sel_ties(need, kt, tie_off):
        off = pl.multiple_of(kt * KEY_TILE, KEY_TILE)
        sc = scr[kt]
        eq = sc == tf
        tie = jnp.where(eq, 1.0, 0.0)
        rank = jnp.dot(tri_ref[...], tie.astype(BF16), preferred_element_type=F32) + tie_off
        sel = ((sc > tf) | (eq & (rank <= need))) & (kidx0 + off <= qidx)
        return sel, tie_off + jnp.sum(tie, axis=0, keepdims=True)

    no_ties = jnp.zeros((1, Q_BLOCK), F32)

    def softmax_sum(m, sel_fn):
        acc_s[...] = jnp.zeros(acc_s.shape, F32)

        def body(kt, carry):
            sel, carry = sel_fn(kt, carry)
            e = jnp.exp2(logits(kt) - m)
            parts = []
            for h in range(ATTN_HEADS):
                cs = slice(h * Q_BLOCK, (h + 1) * Q_BLOCK)
                parts.append(jnp.where(sel, e[:, cs], 0.0).astype(BF16))
            p = jnp.concatenate(parts, axis=1)
            acc_s[...] += jnp.dot(vtb[kt], p, preferred_element_type=F32)
            return carry

        if sel_fn is sel_plain:
            for_tiles(body)
        else:
            lax.fori_loop(0, nk, body, no_ties)

    def selected_max(sel_fn):
        mrun[...] = jnp.full(mrun.shape, NEG_BIG, F32)

        def body(kt, carry):
            sel, carry = sel_fn(kt, carry)
            s = logits(kt)
            for h in range(ATTN_HEADS):
                cs = slice(h * Q_BLOCK, (h + 1) * Q_BLOCK)
                sh = jnp.where(sel, s[:, cs], NEG_BIG)
                mrun[:, cs] = jnp.maximum(
                    mrun[:, cs], jnp.max(sh.reshape(KEY_TILE // 8, 8, Q_BLOCK), axis=0))
            return carry

        lax.fori_loop(0, nk, body, no_ties)
        return jnp.max(mrun[...], axis=0, keepdims=True)

    def fast_path():
        q2 = (qa2 * qa2).astype(BF16)
        qsq = lax.dot_general(jnp.ones((8, ATTN_HEAD_DIM), BF16), q2, contract_last,
                              preferred_element_type=F32)[0:1]
        softmax_sum(jnp.sqrt(qsq * kmax[0:1, 0:1]) * logit_scale, sel_plain)
        return (jnp.min(acc_s[dv:dv + 1, :]) > 0.0).astype(I32)

    done = lax.cond(excess, lambda: jnp.int32(0), fast_path) == 1

    @pl.when(jnp.logical_not(done) & excess)
    def _exact_with_ties():
        need = float(TOPK_MAX) - count_cmp(lambda sc: sc > tf)
        sel_fn = functools.partial(sel_ties, need)
        softmax_sum(selected_max(sel_fn), sel_fn)

    @pl.when(jnp.logical_not(done) & jnp.logical_not(excess))
    def _exact_without_ties():
        softmax_sum(selected_max(sel_plain), sel_plain)

    out = acc_s[:dv, :] / acc_s[dv:dv + 1, :]
    for h in range(ATTN_HEADS):
        oh = out[:, h * Q_BLOCK:(h + 1) * Q_BLOCK].T
        o_ref[:, h * ATTN_HEAD_DIM:(h + 1) * ATTN_HEAD_DIM] = oh.astype(BF16)


def _prompt_attention(qa_hm, qi_pm, wi_t, ka, va, kd, tri, weights, batch, seq):
    nq = seq // Q_BLOCK
    nkt = seq // KEY_TILE
    m = batch * seq
    n_heads_q = ATTN_HEADS * Q_BLOCK
    assert seq // COUNT_ROWS <= 256
    n_steps = batch * nq
    slab = lambda b, q: (b * nq + q, 0)
    slab_specs = []
    for w in weights:
        rows = w.shape[0] // n_steps
        assert rows * n_steps == w.shape[0] and rows % 16 == 0
        slab_specs.append(pl.BlockSpec((rows, w.shape[1]), slab))
    in_specs = [
        pl.BlockSpec((ATTN_HEADS, Q_BLOCK, ATTN_HEAD_DIM), lambda b, q: (0, b * nq + q, 0)),
        pl.BlockSpec((IDX_HEADS // 2, Q_BLOCK, LANES), lambda b, q: (0, b * nq + q, 0)),
        pl.BlockSpec((IDX_HEADS, Q_BLOCK), lambda b, q: (0, b * nq + q)),
        pl.BlockSpec((seq, ATTN_HEAD_DIM), lambda b, q: (b, 0)),
        pl.BlockSpec((seq, ATTN_HEAD_DIM), lambda b, q: (b, 0)),
        pl.BlockSpec((seq, LANES), lambda b, q: (b, 0)),
        pl.BlockSpec((KEY_TILE, KEY_TILE), lambda b, q: (0, 0)),
    ] + slab_specs
    return pl.pallas_call(
        _attn_body,
        grid=(batch, nq),
        in_specs=in_specs,
        out_specs=(pl.BlockSpec((Q_BLOCK, ATTN_HEADS * ATTN_HEAD_DIM), slab),) + tuple(slab_specs),
        out_shape=(jax.ShapeDtypeStruct((m, ATTN_HEADS * ATTN_HEAD_DIM), BF16),)
        + tuple(jax.ShapeDtypeStruct(w.shape, BF16) for w in weights),
        scratch_shapes=[
            pltpu.VMEM((seq, ATTN_HEAD_DIM), BF16),
            pltpu.VMEM((nkt, ATTN_HEAD_DIM + SUM_ROWS, KEY_TILE), BF16),
            pltpu.VMEM((nkt, KEY_TILE, Q_BLOCK), F32),
            pltpu.VMEM((8, n_heads_q), F32),
            pltpu.VMEM((ATTN_HEAD_DIM + SUM_ROWS, n_heads_q), F32),
            pltpu.VMEM((8, LANES), F32),
            pltpu.VMEM((nkt, KEY_TILE, Q_BLOCK), BF16),
        ],
        compiler_params=_cparams(("arbitrary", "arbitrary")),
        name="prompt_attn",
    )(qa_hm, qi_pm, wi_t, ka, va, kd, tri, *weights)


def _gate(o, g):
    rn = o * lax.rsqrt(jnp.mean(o * o, axis=-1, keepdims=True) + EPS)
    return rn * (g / (1.0 + jnp.exp(-g)))


def _ret_body(q_ref, k_ref, v_ref, g_ref, decay_ref, rsc_ref, zeta_ref, gpow_ref,
              rg_ref, st_ref):
    c = pl.program_id(1)

    @pl.when(c == 0)
    def _init():
        st_ref[...] = jnp.zeros(st_ref.shape, F32)

    for h in range(RET_HEADS):
        sl = slice(h * 128, (h + 1) * 128)
        q = q_ref[:, sl]
        k = k_ref[:, sl]
        v = v_ref[:, sl]
        r_old = st_ref[0, h]
        qk = lax.dot_general(q, k, (((1,), (1,)), ((), ())), preferred_element_type=F32)
        inner = jnp.dot((qk * decay_ref[h]).astype(BF16), v, preferred_element_type=F32)
        cross = jnp.dot(q, r_old.astype(BF16), preferred_element_type=F32) * rsc_ref[h]
        kz = (k.astype(F32) * zeta_ref[h]).astype(BF16)
        upd = lax.dot_general(kz, v, (((0,), (0,)), ((), ())), preferred_element_type=F32)
        st_ref[0, h] = r_old * gpow_ref[h] + upd
        rg_ref[:, sl] = _gate(inner + cross, g_ref[:, sl].astype(F32)).astype(BF16)


def _prompt_retention(main, decay, rsc, zeta, gpow, batch, seq):
    nc = seq // RET_CHUNK
    m = batch * seq
    width = RET_WIDTH
    const3 = lambda b, c: (0, 0, 0)
    in_specs = [
        pl.BlockSpec((RET_CHUNK, width), lambda b, c: (b * nc + c, 0)),
        pl.BlockSpec((RET_CHUNK, width), lambda b, c: (b * nc + c, 1)),
        pl.BlockSpec((RET_CHUNK, width), lambda b, c: (b * nc + c, 2)),
        pl.BlockSpec((RET_CHUNK, width), lambda b, c: (b * nc + c, 3)),
        pl.BlockSpec((RET_HEADS, RET_CHUNK, RET_CHUNK), const3),
        pl.BlockSpec((RET_HEADS, RET_CHUNK, RET_DV), const3),
        pl.BlockSpec((RET_HEADS, RET_CHUNK, RET_DK), const3),
        pl.BlockSpec((RET_HEADS, 1, RET_DV), const3),
    ]
    return pl.pallas_call(
        _ret_body,
        grid=(batch, nc),
        in_specs=in_specs,
        out_specs=(
            pl.BlockSpec((RET_CHUNK, width), lambda b, c: (b * nc + c, 0)),
            pl.BlockSpec((1, RET_HEADS, RET_DK, RET_DV), lambda b, c: (b, 0, 0, 0)),
        ),
        out_shape=(
            jax.ShapeDtypeStruct((m, width), BF16),
            jax.ShapeDtypeStruct((batch, RET_HEADS, RET_DK, RET_DV), F32),
        ),
        compiler_params=_cparams(("arbitrary", "arbitrary")),
        name="prompt_ret",
    )(main, main, main, main, decay, rsc, zeta, gpow)


def _outproj_body(a_ref, r_ref, wa_ref, wr_ref, x_ref, g2_ref, x1_ref, h2_ref):
    mixed = (jnp.dot(a_ref[...], wa_ref[...], preferred_element_type=F32)
             + jnp.dot(r_ref[...], wr_ref[...], preferred_element_type=F32))
    x1 = x_ref[...] + mixed
    x1_ref[...] = x1
    ms = jnp.mean(x1 * x1, axis=-1, keepdims=True)
    h2_ref[...] = (x1 * lax.rsqrt(ms + EPS) * g2_ref[...]).astype(BF16)


def _out_projection(attn_o, rg, wo, x2d, g2, tm):
    m = x2d.shape[0]
    half = attn_o.shape[1]
    in_specs = [
        pl.BlockSpec((tm, half), lambda i: (i, 0)),
        pl.BlockSpec((tm, half), lambda i: (i, 0)),
        pl.BlockSpec((half, D_MODEL), lambda i: (0, 0)),
        pl.BlockSpec((half, D_MODEL), lambda i: (1, 0)),
        pl.BlockSpec((tm, D_MODEL), lambda i: (i, 0)),
        pl.BlockSpec((1, D_MODEL), lambda i: (0, 0)),
    ]
    return pl.pallas_call(
        _outproj_body,
        grid=(m // tm,),
        in_specs=in_specs,
        out_specs=(pl.BlockSpec((tm, D_MODEL), lambda i: (i, 0)),
                   pl.BlockSpec((tm, D_MODEL), lambda i: (i, 0))),
        out_shape=(jax.ShapeDtypeStruct((m, D_MODEL), F32),
                   jax.ShapeDtypeStruct((m, D_MODEL), BF16)),
        compiler_params=_cparams(("arbitrary",)),
        name="out_proj",
    )(attn_o, rg, wo, wo, x2d, g2)


def _mlp_body(h2_ref, wu_ref, wd_ref, x1_ref, gf_ref, y_ref, acc_ref, side_job=None):
    f = pl.program_id(1)

    @pl.when(f == 0)
    def _init():
        acc_ref[...] = x1_ref[...]

    if side_job is not None:
        side_job()
    u = jnp.dot(h2_ref[...], wu_ref[...], preferred_element_type=F32)
    a = jnp.maximum(u, 0.0)
    acc_ref[...] += jnp.dot((a * a).astype(BF16), wd_ref[...], preferred_element_type=F32)

    @pl.when(f == pl.num_programs(1) - 1)
    def _final():
        x2 = acc_ref[...]
        ms = jnp.mean(x2 * x2, axis=-1, keepdims=True)
        y_ref[...] = x2 * lax.rsqrt(ms + EPS) * gf_ref[...]


def _mlp_ret_body(h2_ref, wu_ref, wd_ref, x1_ref, gf_ref, qkvg_ref, st_ref, gam_ref,
                  y_ref, rg_ref, so_ref, acc_ref):
    _mlp_body(h2_ref, wu_ref, wd_ref, x1_ref, gf_ref, y_ref, acc_ref,
              side_job=lambda: _sret_body(qkvg_ref, st_ref, gam_ref, rg_ref, so_ref))


def _mlp_with_sample_retention(h2, wu, wd, x1, gf, qkvg, state, gam, tm, tf):
    m = h2.shape[0]
    nb = state.shape[0]
    ni, nf = m // tm, D_FF // tf
    ns = nb // (ni * nf)
    assert ns * ni * nf == nb
    step = lambda i, f: (i * nf + f, 0, 0)
    return pl.pallas_call(
        _mlp_ret_body,
        grid=(ni, nf),
        in_specs=[
            pl.BlockSpec((tm, D_MODEL), lambda i, f: (i, 0)),
            pl.BlockSpec((D_MODEL, tf), lambda i, f: (0, f)),
            pl.BlockSpec((tf, D_MODEL), lambda i, f: (f, 0)),
            pl.BlockSpec((tm, D_MODEL), lambda i, f: (i, 0)),
            pl.BlockSpec((1, D_MODEL), lambda i, f: (0, 0)),
            pl.BlockSpec((ns, 32, LANES), step),
            pl.BlockSpec((ns, RET_HEADS, RET_DK, RET_DV), lambda i, f: (i * nf + f, 0, 0, 0)),
            pl.BlockSpec((RET_HEADS, 1, LANES), lambda i, f: (0, 0, 0)),
        ],
        out_specs=(
            pl.BlockSpec((tm, D_MODEL), lambda i, f: (i, 0)),
            pl.BlockSpec((ns, RET_HEADS, RET_DV), step),
            pl.BlockSpec((ns, RET_HEADS, RET_DK, RET_DV), lambda i, f: (i * nf + f, 0, 0, 0)),
        ),
        out_shape=(
            jax.ShapeDtypeStruct((m, D_MODEL), F32),
            jax.ShapeDtypeStruct((nb, RET_HEADS, RET_DV), BF16),
            jax.ShapeDtypeStruct(state.shape, F32),
        ),
        scratch_shapes=[pltpu.VMEM((tm, D_MODEL), F32)],
        compiler_params=_cparams(("arbitrary", "arbitrary")),
        name="mlp_ret",
    )(h2, wu, wd, x1, gf, qkvg, state, gam)


def _mlp(h2, wu, wd, x1, gf, tm, tf):
    m = h2.shape[0]
    in_specs = [
        pl.BlockSpec((tm, D_MODEL), lambda i, f: (i, 0)),
        pl.BlockSpec((D_MODEL, tf), lambda i, f: (0, f)),
        pl.BlockSpec((tf, D_MODEL), lambda i, f: (f, 0)),
        pl.BlockSpec((tm, D_MODEL), lambda i, f: (i, 0)),
        pl.BlockSpec((1, D_MODEL), lambda i, f: (0, 0)),
    ]
    return pl.pallas_call(
        _mlp_body,
        grid=(m // tm, D_FF // tf),
        in_specs=in_specs,
        out_specs=pl.BlockSpec((tm, D_MODEL), lambda i, f: (i, 0)),
        out_shape=jax.ShapeDtypeStruct((m, D_MODEL), F32),
        scratch_shapes=[pltpu.VMEM((tm, D_MODEL), F32)],
        compiler_params=_cparams(("arbitrary", "arbitrary")),
        name="mlp",
    )(h2, wu, wd, x1, gf)


def _fetch_pages(pt_ref, step, slot, streams, start):
    n_pages = pt_ref.shape[1]
    for hbm, buf, sem in streams:
        for j in range(n_pages):
            cp = pltpu.make_async_copy(hbm.at[pt_ref[step, j]], buf.at[slot, j], sem.at[slot])
            if start:
                cp.start()
            else:
                cp.wait()


def _paged_loop(pt_ref, streams, step_fn):
    nb = pt_ref.shape[0]
    n_slots = streams[0][1].shape[0]
    ahead = n_slots - PAGE_GROUP
    assert nb % PAGE_GROUP == 0 and ahead % PAGE_GROUP == 0 and PAGE_GROUP <= ahead <= nb
    for s in range(ahead):
        _fetch_pages(pt_ref, s, s, streams, start=True)

    def body(g, carry):
        b0 = g * PAGE_GROUP
        for i in range(PAGE_GROUP):
            _fetch_pages(pt_ref, b0 + i, (b0 + i) % n_slots, streams, start=False)
        for i in range(PAGE_GROUP):
            step_fn(b0 + i, (b0 + i) % n_slots, i)
        for i in range(PAGE_GROUP):
            row = b0 + ahead + i
            _fetch_pages(pt_ref, jnp.minimum(row, nb - 1), row % n_slots, streams, start=True)
        return carry

    lax.fori_loop(0, nb // PAGE_GROUP, body, 0)
    for row in range(nb, nb + ahead):
        _fetch_pages(pt_ref, nb - 1, row % n_slots, streams, start=False)


def _sidx_body(pt_ref, qi_ref, w_ref, kin_ref, cache_hbm, out_ref, kt_s, pbuf, sem):
    n_pages = pt_ref.shape[1]
    page = pbuf.shape[3]
    past = n_pages * page
    lane = lax.broadcasted_iota(I32, (1, LANES), 1)

    def step(b, slot, lane_of_trip):
        qi = qi_ref[b]
        w = w_ref[b] * (IDX_HEAD_DIM ** -0.5)
        for j in range(n_pages):
            kt_s[lane_of_trip, :, j * page:(j + 1) * page] = pbuf[slot, j].astype(BF16)
        s = jnp.dot(qi, kt_s[lane_of_trip], preferred_element_type=F32)
        out_ref[b, :, 0:past] = jnp.sum(jnp.maximum(s, 0.0) * w, axis=0, keepdims=True)
        sn = jnp.sum(qi.astype(F32) * kin_ref[b].astype(BF16).astype(F32), axis=1, keepdims=True)
        rn = jnp.sum(jnp.maximum(sn, 0.0) * w, axis=0, keepdims=True)
        out_ref[b, :, past:past + LANES] = jnp.where(lane == 0, rn, -jnp.inf)

    _paged_loop(pt_ref, ((cache_hbm, pbuf, sem),), step)


def _sample_index_scores(page_table, qi_s, wi_s, ki_s, cache_idx_k_t):
    nb, n_pages = page_table.shape
    page = cache_idx_k_t.shape[2]
    width = n_pages * page + LANES

    vmem = pl.BlockSpec(memory_space=pltpu.VMEM)
    return pl.pallas_call(
        _sidx_body,
        in_specs=[pl.BlockSpec(memory_space=pltpu.SMEM), vmem, vmem, vmem,
                  pl.BlockSpec(memory_space=pl.ANY)],
        out_specs=vmem,
        out_shape=jax.ShapeDtypeStruct((nb, 1, width), F32),
        scratch_shapes=[pltpu.VMEM((PAGE_GROUP, IDX_HEAD_DIM, n_pages * page), BF16),
                        pltpu.VMEM((IDX_PAGE_SLOTS, n_pages, IDX_HEAD_DIM, page), F32),
                        pltpu.SemaphoreType.DMA((IDX_PAGE_SLOTS,))],
        compiler_params=pltpu.CompilerParams(vmem_limit_bytes=VMEM_LIMIT),
        name="sample_idx",
    )(page_table, qi_s, wi_s, ki_s, cache_idx_k_t)


def _ssel_body(sc_ref, tri_ref, sel_ref):
    rows, width = sc_ref.shape
    nt = width // LANES
    n_valid = (nt - 1) * LANES + 1

    def tile(kt):
        return sc_ref[:, kt * LANES:(kt + 1) * LANES]

    def count_cmp(cmp):
        acc = jnp.zeros((rows, LANES), F32)
        for kt in range(nt):
            acc = acc + jnp.where(cmp(tile(kt)), 1.0, 0.0)
        return jnp.broadcast_to(jnp.sum(acc, axis=1, keepdims=True), (rows, LANES))

    tf = _threshold_search(lambda c: count_cmp(lambda sc: sc >= c), 32, (rows, LANES))
    need = float(TOPK_MAX) - count_cmp(lambda sc: sc > tf)
    tie_off = jnp.zeros((rows, LANES), F32)
    for kt in range(nt):
        col = kt * LANES + lax.broadcasted_iota(I32, (rows, LANES), 1)
        sc = tile(kt)
        eq = sc == tf
        tie = jnp.where(eq, 1.0, 0.0)
        rank = jnp.dot(tie.astype(BF16), tri_ref[...], preferred_element_type=F32) + tie_off
        sel = ((sc > tf) | (eq & (rank <= need))) & (col < n_valid)
        sel_ref[:, kt * LANES:(kt + 1) * LANES] = jnp.where(sel, 1.0, 0.0)
        tie_off = tie_off + jnp.broadcast_to(jnp.sum(tie, axis=1, keepdims=True), (rows, LANES))


def _sample_select(scores2d, tri):
    rows, width = scores2d.shape
    return pl.pallas_call(
        _ssel_body,
        out_shape=jax.ShapeDtypeStruct((rows, width), F32),
        compiler_params=pltpu.CompilerParams(vmem_limit_bytes=VMEM_LIMIT),
        name="sample_select",
    )(scores2d, tri)


def _sattn_body(pt_ref, q_ref, sel_ref, kn_ref, vn_ref, ck_hbm, cv_hbm, o_ref,
                kbuf, vbuf, ksem, vsem):
    n_pages = pt_ref.shape[1]
    page = kbuf.shape[2]
    past = n_pages * page
    scale = ATTN_HEAD_DIM ** -0.5

    def step(b, slot, lane_of_trip):
        q = q_ref[b]
        k_all = kbuf[slot].reshape(past, ATTN_HEAD_DIM).astype(BF16)
        v_all = vbuf[slot].reshape(past, ATTN_HEAD_DIM).astype(BF16)
        s = lax.dot_general(q, k_all, (((1,), (1,)), ((), ())), preferred_element_type=F32)
        s = jnp.where(sel_ref[b, :, 0:past] > 0.5, s * scale, NEG_BIG)
        kn = kn_ref[b].astype(BF16).astype(F32)
        sn = jnp.sum(q.astype(F32) * kn, axis=1, keepdims=True) * scale
        sn = jnp.where(sel_ref[b, :, past:past + 1] > 0.5, sn, NEG_BIG)
        m = jnp.maximum(jnp.max(s, axis=1, keepdims=True), sn)
        pn = jnp.exp(sn - m)
        p = jnp.exp(s - m)
        l = pn + jnp.sum(p, axis=1, keepdims=True)
        acc = (pn * vn_ref[b].astype(BF16).astype(F32)
               + jnp.dot(p.astype(BF16), v_all, preferred_element_type=F32))
        o_ref[b] = (acc / l).astype(BF16)

    _paged_loop(pt_ref, ((ck_hbm, kbuf, ksem), (cv_hbm, vbuf, vsem)), step)


def _sample_attention(page_table, qa_s, sel3, ka_s, va_s, cache_k, cache_v):
    nb, n_pages = page_table.shape
    page = cache_k.shape[1]

    vmem = pl.BlockSpec(memory_space=pltpu.VMEM)
    hbm = pl.BlockSpec(memory_space=pl.ANY)
    return pl.pallas_call(
        _sattn_body,
        in_specs=[pl.BlockSpec(memory_space=pltpu.SMEM), vmem, vmem, vmem, vmem, hbm, hbm],
        out_specs=vmem,
        out_shape=jax.ShapeDtypeStruct((nb, ATTN_HEADS, ATTN_HEAD_DIM), BF16),
        scratch_shapes=[pltpu.VMEM((ATTN_PAGE_SLOTS, n_pages, page, ATTN_HEAD_DIM), F32),
                        pltpu.VMEM((ATTN_PAGE_SLOTS, n_pages, page, ATTN_HEAD_DIM), F32),
                        pltpu.SemaphoreType.DMA((ATTN_PAGE_SLOTS,)),
                        pltpu.SemaphoreType.DMA((ATTN_PAGE_SLOTS,))],
        compiler_params=pltpu.CompilerParams(vmem_limit_bytes=VMEM_LIMIT),
        name="sample_attn",
    )(page_table, qa_s, sel3, ka_s, va_s, cache_k, cache_v)


def _sret_body(qkvg_ref, st_ref, gam_ref, rg_ref, so_ref):
    ns = st_ref.shape[0]
    for s in range(ns):
        blk = qkvg_ref[s].astype(F32)
        q8 = blk[0:8]
        k8 = blk[8:16]
        v8 = blk[16:24]
        g8 = blk[24:32]
        q_t = q8.T
        k_t = k8.T
        qk = jnp.sum(q8 * k8, axis=1, keepdims=True)
        rows = []
        for h in range(RET_HEADS):
            r_old = st_ref[s, h]
            gam = gam_ref[h]
            qcol = jnp.broadcast_to(q_t[:, h:h + 1], (RET_DK, RET_DV))
            kcol = jnp.broadcast_to(k_t[:, h:h + 1], (RET_DK, RET_DV))
            vrow = v8[h:h + 1]
            q_r = jnp.sum(qcol * r_old, axis=0, keepdims=True)
            rows.append(gam * q_r + qk[h:h + 1] * vrow)
            so_ref[s, h] = gam * r_old + kcol * vrow
        ret = jnp.concatenate(rows, axis=0)
        rg_ref[s] = _gate(ret, g8).astype(BF16)


def _rotary_table(pos):
    half = RET_DK // 2
    inv = ROPE_BASE ** (-np.arange(half, dtype=np.float64) / half)
    ang = np.asarray(pos, np.float64)[:, None] * inv[None, :]
    cos = np.cos(ang)
    sin = np.sin(ang)
    return jnp.asarray(np.concatenate([cos, cos, -sin, sin], axis=1), F32)


def _retention_constants():
    lg = np.log1p(-np.exp2(-5.0 - np.arange(RET_HEADS, dtype=np.float64)))
    n = RET_CHUNK
    i = np.arange(n, dtype=np.float64)
    diff = i[:, None] - i[None, :]
    decay = np.where(diff[None] >= 0, np.exp(np.maximum(diff, 0.0)[None] * lg[:, None, None]), 0.0)
    rsc = np.exp((i + 1.0)[None, :] * lg[:, None])
    zeta = np.exp((n - 1.0 - i)[None, :] * lg[:, None])
    gpow = np.exp(n * lg)
    gam1 = np.exp(lg)
    rsc_b = np.broadcast_to(rsc[:, :, None], (RET_HEADS, n, RET_DV))
    zeta_b = np.broadcast_to(zeta[:, :, None], (RET_HEADS, n, RET_DK))
    gpow_b = np.broadcast_to(gpow[:, None, None], (RET_HEADS, 1, RET_DV))
    gam1_b = np.broadcast_to(gam1[:, None, None], (RET_HEADS, 1, LANES))
    return tuple(jnp.asarray(a, F32) for a in (decay, rsc_b, zeta_b, gpow_b, gam1_b))


def _upper_tri(n):
    return jnp.asarray(np.triu(np.ones((n, n), np.float32)), BF16)


def _lower_tri(n):
    return jnp.asarray(np.tril(np.ones((n, n), np.float32)), BF16)


def _pad_lanes(v):
    return jnp.pad(v, (0, LANES - v.shape[0])).reshape(1, LANES)


def kernel(x_prompt, x_sample, cache_k, cache_v, cache_idx_k, state_ret, page_table,
           norm1_g, w_in, idx_k_norm_g, idx_k_norm_b, w_out, norm2_g, w_up, w_down, final_norm_g):
    batch, seq, _ = x_prompt.shape
    nb = x_sample.shape[0]
    past_len = page_table.shape[1] * cache_k.shape[1]
    half_mix = ATTN_HEADS * ATTN_HEAD_DIM

    wt = w_in.T.astype(BF16)
    g1 = norm1_g.reshape(1, D_MODEL)
    g2 = norm2_g.reshape(1, D_MODEL)
    gf = final_norm_g.reshape(1, D_MODEL)
    lng = _pad_lanes(idx_k_norm_g)
    lnb = _pad_lanes(idx_k_norm_b)
    decay, rsc_b, zeta_b, gpow_b, gam1_b = _retention_constants()

    xp = x_prompt.reshape(batch * seq, D_MODEL)
    xs = x_sample.reshape(nb, D_MODEL)
    cs_p = _rotary_table(np.arange(seq))
    cs_s = _rotary_table(np.full((nb,), past_len))
    (qa_p, qi_p, ka_p, va_p, ki_p, kd_p, wi_p), (qa_s, qi_s, ka_s, va_s, ki_s, _, wi_s) = (
        _project_attn(xp, xs, g1, wt, lng, lnb, tm=TILES["proj_attn_rows"]))
    main_p, main_s = _project_ret(xp, xs, g1, wt, cs_p, cs_s, tm=TILES["proj_ret_rows"])

    attn_p, wu, wd, wo = _prompt_attention(qa_p, qi_p, wi_p, ka_p, va_p, kd_p,
                                           _lower_tri(KEY_TILE), (w_up, w_down, w_out), batch, seq)
    rg_p, ret_state_p = _prompt_retention(main_p, decay, rsc_b, zeta_b, gpow_b, batch, seq)
    x1_p, h2_p = _out_projection(attn_p, rg_p, wo, xp, g2, tm=TILES["out_proj_rows"])

    scores = _sample_index_scores(
        page_table,
        qi_s.transpose(1, 0, 2).reshape(nb, IDX_HEADS, IDX_HEAD_DIM),
        wi_s.T.reshape(nb, IDX_HEADS, 1),
        ki_s.reshape(nb, 1, IDX_HEAD_DIM),
        jnp.swapaxes(cache_idx_k, 1, 2))
    width = scores.shape[2]
    sel = _sample_select(scores.reshape(nb, width), _upper_tri(LANES))
    attn_s = _sample_attention(
        page_table,
        qa_s.transpose(1, 0, 2),
        sel.reshape(nb, 1, width),
        ka_s.reshape(nb, 1, ATTN_HEAD_DIM),
        va_s.reshape(nb, 1, ATTN_HEAD_DIM),
        cache_k, cache_v)
    y_p, rg_s, ret_state_s = _mlp_with_sample_retention(
        h2_p, wu, wd, x1_p, gf, main_s.reshape(nb, 32, LANES), state_ret, gam1_b,
        tm=TILES["mlp_rows"], tf=TILES["mlp_ff"])
    x1_s, h2_s = _out_projection(attn_s.reshape(nb, half_mix), rg_s.reshape(nb, RET_WIDTH),
                                 wo, xs, g2, tm=nb)
    y_s = _mlp(h2_s, wu, wd, x1_s, gf, tm=nb, tf=TILES["mlp_ff"])

    return (
        y_p.reshape(batch, seq, D_MODEL),
        y_s.reshape(nb, 1, D_MODEL),
        ka_p.reshape(batch, seq, ATTN_HEAD_DIM),
        va_p.reshape(batch, seq, ATTN_HEAD_DIM),
        ki_p.reshape(batch, seq, IDX_HEAD_DIM),
        ret_state_p,
        ka_s.reshape(nb, 1, ATTN_HEAD_DIM),
        va_s.reshape(nb, 1, ATTN_HEAD_DIM),
        ki_s.reshape(nb, 1, IDX_HEAD_DIM),
        ret_state_s,
    )
```

```python
import functools

import numpy as np
import jax
import jax.numpy as jnp
from jax import lax
from jax.experimental import pallas as pl
from jax.experimental.pallas import tpu as pltpu

F32 = jnp.float32
BF16 = jnp.bfloat16
I32 = jnp.int32

D_MODEL = 2048
ATTN_HEADS = 8
ATTN_HEAD_DIM = 128
IDX_HEADS = 16
IDX_HEAD_DIM = 64
TOPK_MAX = 256
RET_HEADS = 8
RET_DK = 128
RET_DV = 128
RET_CHUNK = 256
ROPE_BASE = 10000.0
D_FF = 4 * D_MODEL
EPS = 1e-6
Q_BLOCK = 256

OFF_QA, OFF_KA, OFF_VA, OFF_QI, OFF_KI, OFF_WI = 0, 1024, 1152, 1280, 2304, 2368
OFF_QR, OFF_KR, OFF_VR, OFF_GR = 2384, 3408, 4432, 5456
RET_WIDTH = RET_HEADS * RET_DV

LANES = 128
PROJ_TILE = 512
KEY_TILE = 256
COUNT_ROWS = 32
PAGE_GROUP = 2
IDX_PAGE_SLOTS = 16
ATTN_PAGE_SLOTS = 10
SUM_ROWS = 16
LOG2_E = 1.4426950408889634
INT_MIN = -2 ** 31
KEY_NEG_INF = -2 ** 31 + 0x7FFFFF
BF16_KEY_NEG_INF = -2 ** 15 + 0x7F
NEG_BIG = -1e30
VMEM_LIMIT = 56 * 1024 * 1024

TILES = {
    "proj_attn_rows": 1024,
    "proj_ret_rows": 512,
    "out_proj_rows": 512,
    "mlp_rows": 512,
    "mlp_ff": 1024,
}


def _cparams(sem):
    return pltpu.CompilerParams(dimension_semantics=sem, vmem_limit_bytes=VMEM_LIMIT)


def _resident(shape):
    zeros = (0,) * len(shape)
    return pl.BlockSpec(shape, lambda *_: zeros, pipeline_mode=pl.Buffered(1))


def _normed_input(x_ref, g_ref, xn_ref):
    x = x_ref[...]
    ms = jnp.mean(x * x, axis=-1, keepdims=True)
    xn_ref[...] = (x * lax.rsqrt(ms + EPS) * g_ref[...]).astype(BF16)


def _matmul_rows(xn, wt_ref, r0, n):
    return lax.dot_general(xn, wt_ref[r0:r0 + n, :], (((1,), (1,)), ((), ())),
                           preferred_element_type=F32)


def _with_sample_rows(x_ref, xs_ref, g_ref, xn_ref, compute):
    tm, ns = x_ref.shape[0], xs_ref.shape[0]
    last = pl.program_id(0) == pl.num_programs(0) - 1

    @pl.when(last)
    def _prompt_and_sample_rows():
        _normed_input(x_ref, g_ref, xn_ref.at[pl.ds(0, tm)])
        _normed_input(xs_ref, g_ref, xn_ref.at[pl.ds(tm, ns)])
        compute(xn_ref[...], with_samples=True)

    @pl.when(jnp.logical_not(last))
    def _prompt_rows():
        _normed_input(x_ref, g_ref, xn_ref.at[pl.ds(0, tm)])
        compute(xn_ref[pl.ds(0, tm), :], with_samples=False)


def _proj_attn_body(x_ref, xs_ref, g_ref, wt_ref, lng_ref, lnb_ref, *refs):
    outs_p, outs_s, xn_ref = refs[0:7], refs[7:14], refs[14]
    tm, ns = x_ref.shape[0], xs_ref.shape[0]

    def compute(xn, with_samples):
        parts = [((0, tm), outs_p)] + ([((tm, tm + ns), outs_s)] if with_samples else [])
        mm = functools.partial(_matmul_rows, xn, wt_ref)
        for t in range(ATTN_HEADS * ATTN_HEAD_DIM // PROJ_TILE):
            acc = mm(OFF_QA + t * PROJ_TILE, PROJ_TILE)
            for hh in range(PROJ_TILE // LANES):
                piece = acc[:, hh * LANES:(hh + 1) * LANES].astype(BF16)
                for (a, b), o in parts:
                    o[0][4 * t + hh] = piece[a:b]
        for t in range(IDX_HEADS * IDX_HEAD_DIM // PROJ_TILE):
            acc = mm(OFF_QI + t * PROJ_TILE, PROJ_TILE)
            for hh in range(PROJ_TILE // LANES):
                piece = acc[:, hh * LANES:(hh + 1) * LANES].astype(BF16)
                for (a, b), o in parts:
                    o[1][4 * t + hh] = piece[a:b]
        kv = mm(OFF_KA, 2 * ATTN_HEAD_DIM)
        kw = mm(OFF_KI, LANES)
        lane = lax.broadcasted_iota(I32, kw.shape, 1)
        is_k = lane < IDX_HEAD_DIM
        mu = jnp.sum(jnp.where(is_k, kw, 0.0), axis=-1, keepdims=True) * (1.0 / IDX_HEAD_DIM)
        d = jnp.where(is_k, kw - mu, 0.0)
        var = jnp.sum(d * d, axis=-1, keepdims=True) * (1.0 / IDX_HEAD_DIM)
        kn = d * lax.rsqrt(var + EPS) * lng_ref[...] + lnb_ref[...]
        kd = jnp.where(is_k, kn, pltpu.roll(kn, IDX_HEAD_DIM, 1)).astype(BF16)
        wi = kw[:, IDX_HEAD_DIM:IDX_HEAD_DIM + IDX_HEADS] * (IDX_HEADS ** -0.5)
        for (a, b), o in parts:
            o[2][...] = kv[a:b, :ATTN_HEAD_DIM]
            o[3][...] = kv[a:b, ATTN_HEAD_DIM:]
            o[4][...] = kn[a:b, :IDX_HEAD_DIM]
            o[5][...] = kd[a:b]
            o[6][...] = wi[a:b].T

    _with_sample_rows(x_ref, xs_ref, g_ref, xn_ref, compute)


def _proj_ret_body(x_ref, xs_ref, g_ref, wt_ref, cs_ref, css_ref, main_ref, mains_ref, xn_ref):
    tm, ns = x_ref.shape[0], xs_ref.shape[0]

    def compute(xn, with_samples):
        parts = [((0, tm), main_ref)] + ([((tm, tm + ns), mains_ref)] if with_samples else [])
        cs = jnp.concatenate([cs_ref[...], css_ref[...]], axis=0) if with_samples else cs_ref[...]
        cosf = cs[:, :LANES]
        sinf = cs[:, LANES:]
        tiles = RET_WIDTH // PROJ_TILE
        for seg, (off, scale) in enumerate(((OFF_QR, None), (OFF_KR, RET_DK ** -0.5))):
            for t in range(tiles):
                acc = _matmul_rows(xn, wt_ref, off + t * PROJ_TILE, PROJ_TILE)
                for hh in range(PROJ_TILE // LANES):
                    xh = acc[:, hh * LANES:(hh + 1) * LANES]
                    r = xh * cosf + pltpu.roll(xh, RET_DK // 2, 1) * sinf
                    if scale is not None:
                        r = r * scale
                    c0 = seg * RET_WIDTH + t * PROJ_TILE + hh * LANES
                    r = r.astype(BF16)
                    for (a, b), o in parts:
                        o[:, c0:c0 + LANES] = r[a:b]
        for seg, off in ((2, OFF_VR), (3, OFF_GR)):
            for t in range(tiles):
                acc = _matmul_rows(xn, wt_ref, off + t * PROJ_TILE, PROJ_TILE).astype(BF16)
                c0 = seg * RET_WIDTH + t * PROJ_TILE
                for (a, b), o in parts:
                    o[:, c0:c0 + PROJ_TILE] = acc[a:b]

    _with_sample_rows(x_ref, xs_ref, g_ref, xn_ref, compute)


def _project_attn(x2d, xs2d, g1, wt, lng, lnb, tm):
    m, ns = x2d.shape[0], xs2d.shape[0]
    row = lambda i: (i, 0)

    def shapes(n):
        return (
            jax.ShapeDtypeStruct((ATTN_HEADS, n, ATTN_HEAD_DIM), BF16),
            jax.ShapeDtypeStruct((IDX_HEADS // 2, n, LANES), BF16),
            jax.ShapeDtypeStruct((n, ATTN_HEAD_DIM), F32),
            jax.ShapeDtypeStruct((n, ATTN_HEAD_DIM), F32),
            jax.ShapeDtypeStruct((n, IDX_HEAD_DIM), F32),
            jax.ShapeDtypeStruct((n, LANES), BF16),
            jax.ShapeDtypeStruct((IDX_HEADS, n), F32),
        )

    prompt_specs = (
        pl.BlockSpec((ATTN_HEADS, tm, ATTN_HEAD_DIM), lambda i: (0, i, 0)),
        pl.BlockSpec((IDX_HEADS // 2, tm, LANES), lambda i: (0, i, 0)),
        pl.BlockSpec((tm, ATTN_HEAD_DIM), row),
        pl.BlockSpec((tm, ATTN_HEAD_DIM), row),
        pl.BlockSpec((tm, IDX_HEAD_DIM), row),
        pl.BlockSpec((tm, LANES), row),
        pl.BlockSpec((IDX_HEADS, tm), lambda i: (0, i)),
    )
    sample_specs = tuple(pl.BlockSpec(s.shape, lambda i, nd=len(s.shape): (0,) * nd)
                         for s in shapes(ns))
    outs = pl.pallas_call(
        _proj_attn_body,
        grid=(m // tm,),
        in_specs=[pl.BlockSpec((tm, D_MODEL), row), _resident((ns, D_MODEL)),
                  _resident((1, D_MODEL)), _resident((OFF_QR + LANES, D_MODEL)),
                  _resident((1, LANES)), _resident((1, LANES))],
        out_specs=prompt_specs + sample_specs,
        out_shape=shapes(m) + shapes(ns),
        scratch_shapes=[pltpu.VMEM((tm + ns, D_MODEL), BF16)],
        compiler_params=_cparams(("arbitrary",)),
        name="proj_attn",
    )(x2d, xs2d, g1, wt, lng, lnb)
    return outs[:7], outs[7:]


def _project_ret(x2d, xs2d, g1, wt, cs, css, tm):
    m, ns = x2d.shape[0], xs2d.shape[0]
    n_pos_blocks = cs.shape[0] // tm
    row = lambda i: (i, 0)
    return pl.pallas_call(
        _proj_ret_body,
        grid=(m // tm,),
        in_specs=[pl.BlockSpec((tm, D_MODEL), row), _resident((ns, D_MODEL)),
                  _resident((1, D_MODEL)), _resident(wt.shape),
                  pl.BlockSpec((tm, 2 * LANES), lambda i: (i % n_pos_blocks, 0)),
                  _resident((ns, 2 * LANES))],
        out_specs=(pl.BlockSpec((tm, 4 * RET_WIDTH), row),
                   pl.BlockSpec((ns, 4 * RET_WIDTH), lambda i: (0, 0))),
        out_shape=(jax.ShapeDtypeStruct((m, 4 * RET_WIDTH), BF16),
                   jax.ShapeDtypeStruct((ns, 4 * RET_WIDTH), BF16)),
        scratch_shapes=[pltpu.VMEM((tm + ns, D_MODEL), BF16)],
        compiler_params=_cparams(("arbitrary",)),
        name="proj_ret",
    )(x2d, xs2d, g1, wt, cs, css)


def _key_to_float(key):
    bits = key ^ ((key >> 31) & 0x7FFFFFFF)
    return lax.bitcast_convert_type(bits, F32)


def _threshold_search(count_ge, n_iter, shape):
    def body(it, t):
        bit = lax.shift_left(jnp.int32(1), 31 - it)
        cand = t ^ bit
        cnt = count_ge(_key_to_float(cand))
        return jnp.where(cnt >= float(TOPK_MAX), cand, t)

    t = lax.fori_loop(0, n_iter, body, jnp.full(shape, INT_MIN, I32))
    return _key_to_float(jnp.maximum(t, KEY_NEG_INF))


def _bf16_key_to_f32_key(k16):
    return lax.shift_left(k16, 16) | jnp.where(k16 < 0, 0xFFFF, 0)


def _threshold_search_coarse_fine(count_ge_bf16, count_ge, run, shape):
    def coarse(it, u):
        cand = u | lax.shift_left(jnp.int32(1), 15 - it)
        c = _key_to_float(_bf16_key_to_f32_key(cand - 32768)).astype(BF16)
        return jnp.where(count_ge_bf16(c) >= float(TOPK_MAX), cand, u)

    u = lax.fori_loop(0, jnp.where(run, 16, 0), coarse, jnp.zeros(shape, I32))
    k1 = jnp.maximum(u - 32768, BF16_KEY_NEG_INF)
    lo = _bf16_key_to_f32_key(jnp.maximum(k1 - 1, -32768))
    hi = _bf16_key_to_f32_key(jnp.minimum(k1 + 1, 32767))

    def fine(it, t):
        cand = t + lax.shift_left(jnp.int32(1), 16 - it)
        ok = (cand < hi) & (count_ge(_key_to_float(cand)) >= float(TOPK_MAX))
        return jnp.where(ok, cand, t)

    t = lax.fori_loop(0, jnp.where(run, 17, 0), fine, lo)
    return _key_to_float(jnp.maximum(t, KEY_NEG_INF))


def _attn_body(qa_ref, qi_ref, wit_ref, ka_ref, va_ref, kd_ref, tri_ref, wu_ref, wd_ref, wo_ref,
               o_ref, wub_ref, wdb_ref, wob_ref, kbf, vtb, scr, mrun, acc_s, kmax, scr16):
    wub_ref[...] = wu_ref[...].astype(BF16)
    wdb_ref[...] = wd_ref[...].astype(BF16)
    wob_ref[...] = wo_ref[...].astype(BF16)

    qb = pl.program_id(1)
    n_heads_q = ATTN_HEADS * Q_BLOCK
    n_pairs = IDX_HEADS // 2
    dv = ATTN_HEAD_DIM
    logit_scale = ATTN_HEAD_DIM ** -0.5 * LOG2_E

    @pl.when(qb == 0)
    def _cast():
        ka = ka_ref[...]
        kbf[...] = ka.astype(BF16)
        kmax[...] = jnp.broadcast_to(jnp.max(jnp.sum(ka * ka, axis=1, keepdims=True)), kmax.shape)
        for kt in range(vtb.shape[0]):
            vtb[kt, :dv] = va_ref[kt * KEY_TILE:(kt + 1) * KEY_TILE, :].T.astype(BF16)
            vtb[kt, dv:] = jnp.ones((vtb.shape[1] - dv, KEY_TILE), BF16)

    nk = ((qb + 1) * Q_BLOCK + KEY_TILE - 1) // KEY_TILE
    wt = wit_ref[...] * (IDX_HEAD_DIM ** -0.5)
    qi2 = qi_ref[...].reshape(n_pairs * Q_BLOCK, LANES)
    lo_half = lax.broadcasted_iota(I32, (KEY_TILE, LANES), 1) < IDX_HEAD_DIM
    qidx = qb * Q_BLOCK + lax.broadcasted_iota(I32, (KEY_TILE, Q_BLOCK), 1)
    kidx0 = lax.broadcasted_iota(I32, (KEY_TILE, Q_BLOCK), 0)
    contract_last = (((1,), (1,)), ((), ()))

    def idx_body(kt, carry):
        off = pl.multiple_of(kt * KEY_TILE, KEY_TILE)
        kit = kd_ref[pl.ds(off, KEY_TILE), :]
        zero = jnp.zeros_like(kit)
        s_even = lax.dot_general(jnp.where(lo_half, kit, zero), qi2, contract_last,
                                 preferred_element_type=F32)
        s_odd = lax.dot_general(jnp.where(lo_half, zero, kit), qi2, contract_last,
                                preferred_element_type=F32)
        score = jnp.zeros((KEY_TILE, Q_BLOCK), F32)
        for g in range(n_pairs):
            cs = slice(g * Q_BLOCK, (g + 1) * Q_BLOCK)
            score = score + jnp.maximum(s_even[:, cs], 0.0) * wt[2 * g:2 * g + 1, :]
            score = score + jnp.maximum(s_odd[:, cs], 0.0) * wt[2 * g + 1:2 * g + 2, :]
        score = jnp.where(kidx0 + off <= qidx, score, -jnp.inf)
        scr[kt] = score
        scr16[kt] = score.astype(BF16)
        return carry

    def for_tiles(fn):
        def pair(j, carry):
            fn(2 * j, 0)
            fn(2 * j + 1, 0)
            return carry

        lax.fori_loop(0, nk // 2, pair, 0)

        @pl.when(nk % 2 == 1)
        def _last():
            fn(nk - 1, 0)

    for_tiles(idx_body)

    def count_ge_bf16(c):
        def body(kt, acc):
            hit = scr16[kt] >= c
            for r in range(KEY_TILE // COUNT_ROWS):
                acc = jnp.where(hit[r * COUNT_ROWS:(r + 1) * COUNT_ROWS], acc + 1.0, acc)
            return acc
        acc = lax.fori_loop(0, nk, body, jnp.zeros((COUNT_ROWS, Q_BLOCK), BF16))
        return jnp.sum(acc.astype(F32), axis=0, keepdims=True)

    def count_cmp(cmp):
        def body(kt, acc):
            hit = cmp(scr[kt])
            for r in range(KEY_TILE // COUNT_ROWS):
                acc = jnp.where(hit[r * COUNT_ROWS:(r + 1) * COUNT_ROWS], acc + 1.0, acc)
            return acc
        acc = lax.fori_loop(0, nk, body, jnp.zeros((COUNT_ROWS, Q_BLOCK), F32))
        return jnp.sum(acc, axis=0, keepdims=True)

    tf = _threshold_search_coarse_fine(
        count_ge_bf16, lambda c: count_cmp(lambda sc: sc >= c),
        qb >= TOPK_MAX // Q_BLOCK, (1, Q_BLOCK))
    excess = jnp.max(count_cmp(lambda sc: sc >= tf)) > float(TOPK_MAX)

    qa2 = qa_ref[...].reshape(n_heads_q, ATTN_HEAD_DIM)

    def logits(kt):
        off = pl.multiple_of(kt * KEY_TILE, KEY_TILE)
        s = lax.dot_general(kbf[pl.ds(off, KEY_TILE), :], qa2, contract_last,
                            preferred_element_type=F32)
        return s * logit_scale

    def sel_plain(kt, carry):
        off = pl.multiple_of(kt * KEY_TILE, KEY_TILE)
        return (scr[kt] >= tf) & (kidx0 + off <= qidx), carry

    def sel_ties(need, kt, tie_off):
        off = pl.multiple_of(kt * KEY_TILE, KEY_TILE)
        sc = scr[kt]
        eq = sc == tf
        tie = jnp.where(eq, 1.0, 0.0)
        rank = jnp.dot(tri_ref[...], tie.astype(BF16), preferred_element_type=F32) + tie_off
        sel = ((sc > tf) | (eq & (rank <= need))) & (kidx0 + off <= qidx)
        return sel, tie_off + jnp.sum(tie, axis=0, keepdims=True)

    no_ties = jnp.zeros((1, Q_BLOCK), F32)

    def softmax_sum(m, sel_fn):
        acc_s[...] = jnp.zeros(acc_s.shape, F32)

        def body(kt, carry):
            sel, carry = sel_fn(kt, carry)
            e = jnp.exp2(logits(kt) - m)
            parts = []
            for h in range(ATTN_HEADS):
                cs = slice(h * Q_BLOCK, (h + 1) * Q_BLOCK)
                parts.append(jnp.where(sel, e[:, cs], 0.0).astype(BF16))
            p = jnp.concatenate(parts, axis=1)
            acc_s[...] += jnp.dot(vtb[kt], p, preferred_element_type=F32)
            return carry

        if sel_fn is sel_plain:
            for_tiles(body)
        else:
            lax.fori_loop(0, nk, body, no_ties)

    def selected_max(sel_fn):
        mrun[...] = jnp.full(mrun.shape, NEG_BIG, F32)

        def body(kt, carry):
            sel, carry = sel_fn(kt, carry)
            s = logits(kt)
            for h in range(ATTN_HEADS):
                cs = slice(h * Q_BLOCK, (h + 1) * Q_BLOCK)
                sh = jnp.where(sel, s[:, cs], NEG_BIG)
                mrun[:, cs] = jnp.maximum(
                    mrun[:, cs], jnp.max(sh.reshape(KEY_TILE // 8, 8, Q_BLOCK), axis=0))
            return carry

        lax.fori_loop(0, nk, body, no_ties)
        return jnp.max(mrun[...], axis=0, keepdims=True)

    def fast_path():
        q2 = (qa2 * qa2).astype(BF16)
        qsq = lax.dot_general(jnp.ones((8, ATTN_HEAD_DIM), BF16), q2, contract_last,
                              preferred_element_type=F32)[0:1]
        softmax_sum(jnp.sqrt(qsq * kmax[0:1, 0:1]) * logit_scale, sel_plain)
        return (jnp.min(acc_s[dv:dv + 1, :]) > 0.0).astype(I32)

    done = lax.cond(excess, lambda: jnp.int32(0), fast_path) == 1

    @pl.when(jnp.logical_not(done) & excess)
    def _exact_with_ties():
        need = float(TOPK_MAX) - count_cmp(lambda sc: sc > tf)
        sel_fn = functools.partial(sel_ties, need)
        softmax_sum(selected_max(sel_fn), sel_fn)

    @pl.when(jnp.logical_not(done) & jnp.logical_not(excess))
    def _exact_without_ties():
        softmax_sum(selected_max(sel_plain), sel_plain)

    out = acc_s[:dv, :] / acc_s[dv:dv + 1, :]
    for h in range(ATTN_HEADS):
        oh = out[:, h * Q_BLOCK:(h + 1) * Q_BLOCK].T
        o_ref[:, h * ATTN_HEAD_DIM:(h + 1) * ATTN_HEAD_DIM] = oh.astype(BF16)


def _prompt_attention(qa_hm, qi_pm, wi_t, ka, va, kd, tri, weights, batch, seq):
    nq = seq // Q_BLOCK
    nkt = seq // KEY_TILE
    m = batch * seq
    n_heads_q = ATTN_HEADS * Q_BLOCK
    assert seq // COUNT_ROWS <= 256
    n_steps = batch * nq
    slab = lambda b, q: (b * nq + q, 0)
    slab_specs = []
    for w in weights:
        rows = w.shape[0] // n_steps
        assert rows * n_steps == w.shape[0] and rows % 16 == 0
        slab_specs.append(pl.BlockSpec((rows, w.shape[1]), slab))
    in_specs = [
        pl.BlockSpec((ATTN_HEADS, Q_BLOCK, ATTN_HEAD_DIM), lambda b, q: (0, b * nq + q, 0)),
        pl.BlockSpec((IDX_HEADS // 2, Q_BLOCK, LANES), lambda b, q: (0, b * nq + q, 0)),
        pl.BlockSpec((IDX_HEADS, Q_BLOCK), lambda b, q: (0, b * nq + q)),
        pl.BlockSpec((seq, ATTN_HEAD_DIM), lambda b, q: (b, 0)),
        pl.BlockSpec((seq, ATTN_HEAD_DIM), lambda b, q: (b, 0)),
        pl.BlockSpec((seq, LANES), lambda b, q: (b, 0)),
        pl.BlockSpec((KEY_TILE, KEY_TILE), lambda b, q: (0, 0)),
    ] + slab_specs
    return pl.pallas_call(
        _attn_body,
        grid=(batch, nq),
        in_specs=in_specs,
        out_specs=(pl.BlockSpec((Q_BLOCK, ATTN_HEADS * ATTN_HEAD_DIM), slab),) + tuple(slab_specs),
        out_shape=(jax.ShapeDtypeStruct((m, ATTN_HEADS * ATTN_HEAD_DIM), BF16),)
        + tuple(jax.ShapeDtypeStruct(w.shape, BF16) for w in weights),
        scratch_shapes=[
            pltpu.VMEM((seq, ATTN_HEAD_DIM), BF16),
            pltpu.VMEM((nkt, ATTN_HEAD_DIM + SUM_ROWS, KEY_TILE), BF16),
            pltpu.VMEM((nkt, KEY_TILE, Q_BLOCK), F32),
            pltpu.VMEM((8, n_heads_q), F32),
            pltpu.VMEM((ATTN_HEAD_DIM + SUM_ROWS, n_heads_q), F32),
            pltpu.VMEM((8, LANES), F32),
            pltpu.VMEM((nkt, KEY_TILE, Q_BLOCK), BF16),
        ],
        compiler_params=_cparams(("arbitrary", "arbitrary")),
        name="prompt_attn",
    )(qa_hm, qi_pm, wi_t, ka, va, kd, tri, *weights)


def _gate(o, g):
    rn = o * lax.rsqrt(jnp.mean(o * o, axis=-1, keepdims=True) + EPS)
    return rn * (g / (1.0 + jnp.exp(-g)))


def _ret_body(q_ref, k_ref, v_ref, g_ref, decay_ref, rsc_ref, zeta_ref, gpow_ref,
              rg_ref, st_ref):
    c = pl.program_id(1)

    @pl.when(c == 0)
    def _init():
        st_ref[...] = jnp.zeros(st_ref.shape, F32)

    for h in range(RET_HEADS):
        sl = slice(h * 128, (h + 1) * 128)
        q = q_ref[:, sl]
        k = k_ref[:, sl]
        v = v_ref[:, sl]
        r_old = st_ref[0, h]
        qk = lax.dot_general(q, k, (((1,), (1,)), ((), ())), preferred_element_type=F32)
        inner = jnp.dot((qk * decay_ref[h]).astype(BF16), v, preferred_element_type=F32)
        cross = jnp.dot(q, r_old.astype(BF16), preferred_element_type=F32) * rsc_ref[h]
        kz = (k.astype(F32) * zeta_ref[h]).astype(BF16)
        upd = lax.dot_general(kz, v, (((0,), (0,)), ((), ())), preferred_element_type=F32)
        st_ref[0, h] = r_old * gpow_ref[h] + upd
        rg_ref[:, sl] = _gate(inner + cross, g_ref[:, sl].astype(F32)).astype(BF16)


def _prompt_retention(main, decay, rsc, zeta, gpow, batch, seq):
    nc = seq // RET_CHUNK
    m = batch * seq
    width = RET_WIDTH
    const3 = lambda b, c: (0, 0, 0)
    in_specs = [
        pl.BlockSpec((RET_CHUNK, width), lambda b, c: (b * nc + c, 0)),
        pl.BlockSpec((RET_CHUNK, width), lambda b, c: (b * nc + c, 1)),
        pl.BlockSpec((RET_CHUNK, width), lambda b, c: (b * nc + c, 2)),
        pl.BlockSpec((RET_CHUNK, width), lambda b, c: (b * nc + c, 3)),
        pl.BlockSpec((RET_HEADS, RET_CHUNK, RET_CHUNK), const3),
        pl.BlockSpec((RET_HEADS, RET_CHUNK, RET_DV), const3),
        pl.BlockSpec((RET_HEADS, RET_CHUNK, RET_DK), const3),
        pl.BlockSpec((RET_HEADS, 1, RET_DV), const3),
    ]
    return pl.pallas_call(
        _ret_body,
        grid=(batch, nc),
        in_specs=in_specs,
        out_specs=(
            pl.BlockSpec((RET_CHUNK, width), lambda b, c: (b * nc + c, 0)),
            pl.BlockSpec((1, RET_HEADS, RET_DK, RET_DV), lambda b, c: (b, 0, 0, 0)),
        ),
        out_shape=(
            jax.ShapeDtypeStruct((m, width), BF16),
            jax.ShapeDtypeStruct((batch, RET_HEADS, RET_DK, RET_DV), F32),
        ),
        compiler_params=_cparams(("arbitrary", "arbitrary")),
        name="prompt_ret",
    )(main, main, main, main, decay, rsc, zeta, gpow)


def _outproj_body(a_ref, r_ref, wa_ref, wr_ref, x_ref, g2_ref, x1_ref, h2_ref):
    mixed = (jnp.dot(a_ref[...], wa_ref[...], preferred_element_type=F32)
             + jnp.dot(r_ref[...], wr_ref[...], preferred_element_type=F32))
    x1 = x_ref[...] + mixed
    x1_ref[...] = x1
    ms = jnp.mean(x1 * x1, axis=-1, keepdims=True)
    h2_ref[...] = (x1 * lax.rsqrt(ms + EPS) * g2_ref[...]).astype(BF16)


def _out_projection(attn_o, rg, wo, x2d, g2, tm):
    m = x2d.shape[0]
    half = attn_o.shape[1]
    in_specs = [
        pl.BlockSpec((tm, half), lambda i: (i, 0)),
        pl.BlockSpec((tm, half), lambda i: (i, 0)),
        pl.BlockSpec((half, D_MODEL), lambda i: (0, 0)),
        pl.BlockSpec((half, D_MODEL), lambda i: (1, 0)),
        pl.BlockSpec((tm, D_MODEL), lambda i: (i, 0)),
        pl.BlockSpec((1, D_MODEL), lambda i: (0, 0)),
    ]
    return pl.pallas_call(
        _outproj_body,
        grid=(m // tm,),
        in_specs=in_specs,
        out_specs=(pl.BlockSpec((tm, D_MODEL), lambda i: (i, 0)),
                   pl.BlockSpec((tm, D_MODEL), lambda i: (i, 0))),
        out_shape=(jax.ShapeDtypeStruct((m, D_MODEL), F32),
                   jax.ShapeDtypeStruct((m, D_MODEL), BF16)),
        compiler_params=_cparams(("arbitrary",)),
        name="out_proj",
    )(attn_o, rg, wo, wo, x2d, g2)


def _mlp_body(h2_ref, wu_ref, wd_ref, x1_ref, gf_ref, y_ref, acc_ref, side_job=None):
    f = pl.program_id(1)

    @pl.when(f == 0)
    def _init():
        acc_ref[...] = x1_ref[...]

    if side_job is not None:
        side_job()
    u = jnp.dot(h2_ref[...], wu_ref[...], preferred_element_type=F32)
    a = jnp.maximum(u, 0.0)
    acc_ref[...] += jnp.dot((a * a).astype(BF16), wd_ref[...], preferred_element_type=F32)

    @pl.when(f == pl.num_programs(1) - 1)
    def _final():
        x2 = acc_ref[...]
        ms = jnp.mean(x2 * x2, axis=-1, keepdims=True)
        y_ref[...] = x2 * lax.rsqrt(ms + EPS) * gf_ref[...]


def _mlp_ret_body(h2_ref, wu_ref, wd_ref, x1_ref, gf_ref, qkvg_ref, st_ref, gam_ref,
                  y_ref, rg_ref, so_ref, acc_ref):
    _mlp_body(h2_ref, wu_ref, wd_ref, x1_ref, gf_ref, y_ref, acc_ref,
              side_job=lambda: _sret_body(qkvg_ref, st_ref, gam_ref, rg_ref, so_ref))


def _mlp_with_sample_retention(h2, wu, wd, x1, gf, qkvg, state, gam, tm, tf):
    m = h2.shape[0]
    nb = state.shape[0]
    ni, nf = m // tm, D_FF // tf
    ns = nb // (ni * nf)
    assert ns * ni * nf == nb
    step = lambda i, f: (i * nf + f, 0, 0)
    return pl.pallas_call(
        _mlp_ret_body,
        grid=(ni, nf),
        in_specs=[
            pl.BlockSpec((tm, D_MODEL), lambda i, f: (i, 0)),
            pl.BlockSpec((D_MODEL, tf), lambda i, f: (0, f)),
            pl.BlockSpec((tf, D_MODEL), lambda i, f: (f, 0)),
            pl.BlockSpec((tm, D_MODEL), lambda i, f: (i, 0)),
            pl.BlockSpec((1, D_MODEL), lambda i, f: (0, 0)),
            pl.BlockSpec((ns, 32, LANES), step),
            pl.BlockSpec((ns, RET_HEADS, RET_DK, RET_DV), lambda i, f: (i * nf + f, 0, 0, 0)),
            pl.BlockSpec((RET_HEADS, 1, LANES), lambda i, f: (0, 0, 0)),
        ],
        out_specs=(
            pl.BlockSpec((tm, D_MODEL), lambda i, f: (i, 0)),
            pl.BlockSpec((ns, RET_HEADS, RET_DV), step),
            pl.BlockSpec((ns, RET_HEADS, RET_DK, RET_DV), lambda i, f: (i * nf + f, 0, 0, 0)),
        ),
        out_shape=(
            jax.ShapeDtypeStruct((m, D_MODEL), F32),
            jax.ShapeDtypeStruct((nb, RET_HEADS, RET_DV), BF16),
            jax.ShapeDtypeStruct(state.shape, F32),
        ),
        scratch_shapes=[pltpu.VMEM((tm, D_MODEL), F32)],
        compiler_params=_cparams(("arbitrary", "arbitrary")),
        name="mlp_ret",
    )(h2, wu, wd, x1, gf, qkvg, state, gam)


def _mlp(h2, wu, wd, x1, gf, tm, tf):
    m = h2.shape[0]
    in_specs = [
        pl.BlockSpec((tm, D_MODEL), lambda i, f: (i, 0)),
        pl.BlockSpec((D_MODEL, tf), lambda i, f: (0, f)),
        pl.BlockSpec((tf, D_MODEL), lambda i, f: (f, 0)),
        pl.BlockSpec((tm, D_MODEL), lambda i, f: (i, 0)),
        pl.BlockSpec((1, D_MODEL), lambda i, f: (0, 0)),
    ]
    return pl.pallas_call(
        _mlp_body,
        grid=(m // tm, D_FF // tf),
        in_specs=in_specs,
        out_specs=pl.BlockSpec((tm, D_MODEL), lambda i, f: (i, 0)),
        out_shape=jax.ShapeDtypeStruct((m, D_MODEL), F32),
        scratch_shapes=[pltpu.VMEM((tm, D_MODEL), F32)],
        compiler_params=_cparams(("arbitrary", "arbitrary")),
        name="mlp",
    )(h2, wu, wd, x1, gf)


def _fetch_pages(pt_ref, step, slot, streams, start):
    n_pages = pt_ref.shape[1]
    for hbm, buf, sem in streams:
        for j in range(n_pages):
            cp = pltpu.make_async_copy(hbm.at[pt_ref[step, j]], buf.at[slot, j], sem.at[slot])
            if start:
                cp.start()
            else:
                cp.wait()


def _paged_loop(pt_ref, streams, step_fn):
    nb = pt_ref.shape[0]
    n_slots = streams[0][1].shape[0]
    ahead = n_slots - PAGE_GROUP
    assert nb % PAGE_GROUP == 0 and ahead % PAGE_GROUP == 0 and PAGE_GROUP <= ahead <= nb
    for s in range(ahead):
        _fetch_pages(pt_ref, s, s, streams, start=True)

    def body(g, carry):
        b0 = g * PAGE_GROUP

        @pl.when(b0 + ahead < nb)
        def _next():
            for i in range(PAGE_GROUP):
                row = b0 + ahead + i
                _fetch_pages(pt_ref, row, row % n_slots, streams, start=True)

        for i in range(PAGE_GROUP):
            _fetch_pages(pt_ref, b0 + i, (b0 + i) % n_slots, streams, start=False)
        for i in range(PAGE_GROUP):
            step_fn(b0 + i, (b0 + i) % n_slots, i)
        return carry

    lax.fori_loop(0, nb // PAGE_GROUP, body, 0)


def _sidx_body(pt_ref, qi_ref, w_ref, kin_ref, cache_hbm, out_ref, kt_s, pbuf, sem):
    n_pages = pt_ref.shape[1]
    page = pbuf.shape[3]
    past = n_pages * page
    lane = lax.broadcasted_iota(I32, (1, LANES), 1)

    def step(b, slot, lane_of_trip):
        qi = qi_ref[b]
        w = w_ref[b] * (IDX_HEAD_DIM ** -0.5)
        for j in range(n_pages):
            kt_s[lane_of_trip, :, j * page:(j + 1) * page] = pbuf[slot, j].astype(BF16)
        s = jnp.dot(qi, kt_s[lane_of_trip], preferred_element_type=F32)
        out_ref[b, :, 0:past] = jnp.sum(jnp.maximum(s, 0.0) * w, axis=0, keepdims=True)
        sn = jnp.sum(qi.astype(F32) * kin_ref[b].astype(BF16).astype(F32), axis=1, keepdims=True)
        rn = jnp.sum(jnp.maximum(sn, 0.0) * w, axis=0, keepdims=True)
        out_ref[b, :, past:past + LANES] = jnp.where(lane == 0, rn, -jnp.inf)

    _paged_loop(pt_ref, ((cache_hbm, pbuf, sem),), step)


def _sample_index_scores(page_table, qi_s, wi_s, ki_s, cache_idx_k_t):
    nb, n_pages = page_table.shape
    page = cache_idx_k_t.shape[2]
    width = n_pages * page + LANES

    vmem = pl.BlockSpec(memory_space=pltpu.VMEM)
    return pl.pallas_call(
        _sidx_body,
        in_specs=[pl.BlockSpec(memory_space=pltpu.SMEM), vmem, vmem, vmem,
                  pl.BlockSpec(memory_space=pl.ANY)],
        out_specs=vmem,
        out_shape=jax.ShapeDtypeStruct((nb, 1, width), F32),
        scratch_shapes=[pltpu.VMEM((PAGE_GROUP, IDX_HEAD_DIM, n_pages * page), BF16),
                        pltpu.VMEM((IDX_PAGE_SLOTS, n_pages, IDX_HEAD_DIM, page), F32),
                        pltpu.SemaphoreType.DMA((IDX_PAGE_SLOTS,))],
        compiler_params=pltpu.CompilerParams(vmem_limit_bytes=VMEM_LIMIT),
        name="sample_idx",
    )(page_table, qi_s, wi_s, ki_s, cache_idx_k_t)


def _ssel_body(sc_ref, tri_ref, sel_ref):
    rows, width = sc_ref.shape
    nt = width // LANES
    n_valid = (nt - 1) * LANES + 1

    def tile(kt):
        return sc_ref[:, kt * LANES:(kt + 1) * LANES]

    def count_cmp(cmp):
        acc = jnp.zeros((rows, LANES), F32)
        for kt in range(nt):
            acc = acc + jnp.where(cmp(tile(kt)), 1.0, 0.0)
        return jnp.broadcast_to(jnp.sum(acc, axis=1, keepdims=True), (rows, LANES))

    tf = _threshold_search(lambda c: count_cmp(lambda sc: sc >= c), 32, (rows, LANES))
    need = float(TOPK_MAX) - count_cmp(lambda sc: sc > tf)
    tie_off = jnp.zeros((rows, LANES), F32)
    for kt in range(nt):
        col = kt * LANES + lax.broadcasted_iota(I32, (rows, LANES), 1)
        sc = tile(kt)
        eq = sc == tf
        tie = jnp.where(eq, 1.0, 0.0)
        rank = jnp.dot(tie.astype(BF16), tri_ref[...], preferred_element_type=F32) + tie_off
        sel = ((sc > tf) | (eq & (rank <= need))) & (col < n_valid)
        sel_ref[:, kt * LANES:(kt + 1) * LANES] = jnp.where(sel, 1.0, 0.0)
        tie_off = tie_off + jnp.broadcast_to(jnp.sum(tie, axis=1, keepdims=True), (rows, LANES))


def _sample_select(scores2d, tri):
    rows, width = scores2d.shape
    return pl.pallas_call(
        _ssel_body,
        out_shape=jax.ShapeDtypeStruct((rows, width), F32),
        compiler_params=pltpu.CompilerParams(vmem_limit_bytes=VMEM_LIMIT),
        name="sample_select",
    )(scores2d, tri)


def _sattn_body(pt_ref, q_ref, sel_ref, kn_ref, vn_ref, ck_hbm, cv_hbm, o_ref,
                kbuf, vbuf, ksem, vsem):
    n_pages = pt_ref.shape[1]
    page = kbuf.shape[2]
    past = n_pages * page
    scale = ATTN_HEAD_DIM ** -0.5

    def step(b, slot, lane_of_trip):
        q = q_ref[b]
        k_all = kbuf[slot].reshape(past, ATTN_HEAD_DIM).astype(BF16)
        v_all = vbuf[slot].reshape(past, ATTN_HEAD_DIM).astype(BF16)
        s = lax.dot_general(q, k_all, (((1,), (1,)), ((), ())), preferred_element_type=F32)
        s = jnp.where(sel_ref[b, :, 0:past] > 0.5, s * scale, NEG_BIG)
        kn = kn_ref[b].astype(BF16).astype(F32)
        sn = jnp.sum(q.astype(F32) * kn, axis=1, keepdims=True) * scale
        sn = jnp.where(sel_ref[b, :, past:past + 1] > 0.5, sn, NEG_BIG)
        m = jnp.maximum(jnp.max(s, axis=1, keepdims=True), sn)
        pn = jnp.exp(sn - m)
        p = jnp.exp(s - m)
        l = pn + jnp.sum(p, axis=1, keepdims=True)
        acc = (pn * vn_ref[b].astype(BF16).astype(F32)
               + jnp.dot(p.astype(BF16), v_all, preferred_element_type=F32))
        o_ref[b] = (acc / l).astype(BF16)

    _paged_loop(pt_ref, ((ck_hbm, kbuf, ksem), (cv_hbm, vbuf, vsem)), step)


def _sample_attention(page_table, qa_s, sel3, ka_s, va_s, cache_k, cache_v):
    nb, n_pages = page_table.shape
    page = cache_k.shape[1]

    vmem = pl.BlockSpec(memory_space=pltpu.VMEM)
    hbm = pl.BlockSpec(memory_space=pl.ANY)
    return pl.pallas_call(
        _sattn_body,
        in_specs=[pl.BlockSpec(memory_space=pltpu.SMEM), vmem, vmem, vmem, vmem, hbm, hbm],
        out_specs=vmem,
        out_shape=jax.ShapeDtypeStruct((nb, ATTN_HEADS, ATTN_HEAD_DIM), BF16),
        scratch_shapes=[pltpu.VMEM((ATTN_PAGE_SLOTS, n_pages, page, ATTN_HEAD_DIM), F32),
                        pltpu.VMEM((ATTN_PAGE_SLOTS, n_pages, page, ATTN_HEAD_DIM), F32),
                        pltpu.SemaphoreType.DMA((ATTN_PAGE_SLOTS,)),
                        pltpu.SemaphoreType.DMA((ATTN_PAGE_SLOTS,))],
        compiler_params=pltpu.CompilerParams(vmem_limit_bytes=VMEM_LIMIT),
        name="sample_attn",
    )(page_table, qa_s, sel3, ka_s, va_s, cache_k, cache_v)


def _sret_body(qkvg_ref, st_ref, gam_ref, rg_ref, so_ref):
    ns = st_ref.shape[0]
    for s in range(ns):
        blk = qkvg_ref[s].astype(F32)
        q8 = blk[0:8]
        k8 = blk[8:16]
        v8 = blk[16:24]
        g8 = blk[24:32]
        q_t = q8.T
        k_t = k8.T
        qk = jnp.sum(q8 * k8, axis=1, keepdims=True)
        rows = []
        for h in range(RET_HEADS):
            r_old = st_ref[s, h]
            gam = gam_ref[h]
            qcol = jnp.broadcast_to(q_t[:, h:h + 1], (RET_DK, RET_DV))
            kcol = jnp.broadcast_to(k_t[:, h:h + 1], (RET_DK, RET_DV))
            vrow = v8[h:h + 1]
            q_r = jnp.sum(qcol * r_old, axis=0, keepdims=True)
            rows.append(gam * q_r + qk[h:h + 1] * vrow)
            so_ref[s, h] = gam * r_old + kcol * vrow
        ret = jnp.concatenate(rows, axis=0)
        rg_ref[s] = _gate(ret, g8).astype(BF16)


def _rotary_table(pos):
    half = RET_DK // 2
    inv = ROPE_BASE ** (-np.arange(half, dtype=np.float64) / half)
    ang = np.asarray(pos, np.float64)[:, None] * inv[None, :]
    cos = np.cos(ang)
    sin = np.sin(ang)
    return jnp.asarray(np.concatenate([cos, cos, -sin, sin], axis=1), F32)


def _retention_constants():
    lg = np.log1p(-np.exp2(-5.0 - np.arange(RET_HEADS, dtype=np.float64)))
    n = RET_CHUNK
    i = np.arange(n, dtype=np.float64)
    diff = i[:, None] - i[None, :]
    decay = np.where(diff[None] >= 0, np.exp(np.maximum(diff, 0.0)[None] * lg[:, None, None]), 0.0)
    rsc = np.exp((i + 1.0)[None, :] * lg[:, None])
    zeta = np.exp((n - 1.0 - i)[None, :] * lg[:, None])
    gpow = np.exp(n * lg)
    gam1 = np.exp(lg)
    rsc_b = np.broadcast_to(rsc[:, :, None], (RET_HEADS, n, RET_DV))
    zeta_b = np.broadcast_to(zeta[:, :, None], (RET_HEADS, n, RET_DK))
    gpow_b = np.broadcast_to(gpow[:, None, None], (RET_HEADS, 1, RET_DV))
    gam1_b = np.broadcast_to(gam1[:, None, None], (RET_HEADS, 1, LANES))
    return tuple(jnp.asarray(a, F32) for a in (decay, rsc_b, zeta_b, gpow_b, gam1_b))


def _upper_tri(n):
    return jnp.asarray(np.triu(np.ones((n, n), np.float32)), BF16)


def _lower_tri(n):
    return jnp.asarray(np.tril(np.ones((n, n), np.float32)), BF16)


def _pad_lanes(v):
    return jnp.pad(v, (0, LANES - v.shape[0])).reshape(1, LANES)


def kernel(x_prompt, x_sample, cache_k, cache_v, cache_idx_k, state_ret, page_table,
           norm1_g, w_in, idx_k_norm_g, idx_k_norm_b, w_out, norm2_g, w_up, w_down, final_norm_g):
    batch, seq, _ = x_prompt.shape
    nb = x_sample.shape[0]
    past_len = page_table.shape[1] * cache_k.shape[1]
    half_mix = ATTN_HEADS * ATTN_HEAD_DIM

    wt = w_in.T.astype(BF16)
    g1 = norm1_g.reshape(1, D_MODEL)
    g2 = norm2_g.reshape(1, D_MODEL)
    gf = final_norm_g.reshape(1, D_MODEL)
    lng = _pad_lanes(idx_k_norm_g)
    lnb = _pad_lanes(idx_k_norm_b)
    decay, rsc_b, zeta_b, gpow_b, gam1_b = _retention_constants()

    xp = x_prompt.reshape(batch * seq, D_MODEL)
    xs = x_sample.reshape(nb, D_MODEL)
    cs_p = _rotary_table(np.arange(seq))
    cs_s = _rotary_table(np.full((nb,), past_len))
    (qa_p, qi_p, ka_p, va_p, ki_p, kd_p, wi_p), (qa_s, qi_s, ka_s, va_s, ki_s, _, wi_s) = (
        _project_attn(xp, xs, g1, wt, lng, lnb, tm=TILES["proj_attn_rows"]))
    main_p, main_s = _project_ret(xp, xs, g1, wt, cs_p, cs_s, tm=TILES["proj_ret_rows"])

    attn_p, wu, wd, wo = _prompt_attention(qa_p, qi_p, wi_p, ka_p, va_p, kd_p,
                                           _lower_tri(KEY_TILE), (w_up, w_down, w_out), batch, seq)
    rg_p, ret_state_p = _prompt_retention(main_p, decay, rsc_b, zeta_b, gpow_b, batch, seq)
    x1_p, h2_p = _out_projection(attn_p, rg_p, wo, xp, g2, tm=TILES["out_proj_rows"])

    scores = _sample_index_scores(
        page_table,
        qi_s.transpose(1, 0, 2).reshape(nb, IDX_HEADS, IDX_HEAD_DIM),
        wi_s.T.reshape(nb, IDX_HEADS, 1),
        ki_s.reshape(nb, 1, IDX_HEAD_DIM),
        jnp.swapaxes(cache_idx_k, 1, 2))
    width = scores.shape[2]
    sel = _sample_select(scores.reshape(nb, width), _upper_tri(LANES))
    attn_s = _sample_attention(
        page_table,
        qa_s.transpose(1, 0, 2),
        sel.reshape(nb, 1, width),
        ka_s.reshape(nb, 1, ATTN_HEAD_DIM),
        va_s.reshape(nb, 1, ATTN_HEAD_DIM),
        cache_k, cache_v)
    y_p, rg_s, ret_state_s = _mlp_with_sample_retention(
        h2_p, wu, wd, x1_p, gf, main_s.reshape(nb, 32, LANES), state_ret, gam1_b,
        tm=TILES["mlp_rows"], tf=TILES["mlp_ff"])
    x1_s, h2_s = _out_projection(attn_s.reshape(nb, half_mix), rg_s.reshape(nb, RET_WIDTH),
                                 wo, xs, g2, tm=nb)
    y_s = _mlp(h2_s, wu, wd, x1_s, gf, tm=nb, tf=TILES["mlp_ff"])

    return (
        y_p.reshape(batch, seq, D_MODEL),
        y_s.reshape(nb, 1, D_MODEL),
        ka_p.reshape(batch, seq, ATTN_HEAD_DIM),
        va_p.reshape(batch, seq, ATTN_HEAD_DIM),
        ki_p.reshape(batch, seq, IDX_HEAD_DIM),
        ret_state_p,
        ka_s.reshape(nb, 1, ATTN_HEAD_DIM),
        va_s.reshape(nb, 1, ATTN_HEAD_DIM),
        ki_s.reshape(nb, 1, IDX_HEAD_DIM),
        ret_state_s,
    )
```
